```python
import math
import jax
import jax.numpy as jnp
from jax import lax
import numpy as np


D_MODEL = 1024
BATCH = 8
SEQ = 8192
DEPTH = 2

GRID_W = 64
CTX_LEN = 256
EPS = 1e-6
N_MOD = 6

M_HEADS = 4
M_HEAD_DIM = 128
M_WIDTH = M_HEADS * M_HEAD_DIM
M_GATES = 4 * M_HEADS
M_CHUNK = 128
M_CONV = 3
ROPE_BASE = 10000.0

NA_HEADS = 8
NA_HEAD_DIM = 64
NA_WIDTH = NA_HEADS * NA_HEAD_DIM
WIN_R = 8
WIN_C = 16
Q_COLS = 16
K_COLS = 32

S5_GROUP = 16
S5_WIDTH = 512
S5_GROUPS = S5_WIDTH // S5_GROUP
S5_STATE = 64
S5_MAX_RE = -1e-4

N_BRANCH = 3
IN_SPLIT = (M_WIDTH, M_WIDTH, M_WIDTH, M_WIDTH, M_GATES, NA_WIDTH, NA_WIDTH, NA_WIDTH, S5_WIDTH, N_BRANCH * D_MODEL)
IN_COLS = 4 * M_WIDTH + M_GATES + 3 * NA_WIDTH + S5_WIDTH + N_BRANCH * D_MODEL

FF_DIM = 2816
N_EXPERTS = 8
TOP_K = 2
EXPERT_FF = 3584
N_DENSE_LAYERS = (DEPTH + 1) // 2
N_MOE_LAYERS = DEPTH // 2

kernel_name = 'hybrid_mlstm_na_s5_moe_dit'


def rms_norm(x, g):
    xf = x.astype(jnp.float32)
    y = xf * lax.rsqrt(jnp.mean(xf * xf, axis=-1, keepdims=True) + EPS)
    return (y * g.astype(jnp.float32)).astype(x.dtype)


def modulate(h, shift, scale):
    return h * (1 + scale) + shift


def adaln(cvec, w, b):
    return jnp.split(jax.nn.silu(cvec) @ w + b, N_MOD, axis=-1)


def split_columns(w):
    points, acc = [], 0
    for size in IN_SPLIT[:-1]:
        acc += size
        points.append(acc)
    return jnp.split(w, points, axis=-1)


def short_conv(x, w, b):
    pad = w.shape[0] // 2
    y = lax.conv_general_dilated(x, w[:, None, :], window_strides=(1,), padding=[(pad, pad)],
                                 dimension_numbers=('NWC', 'WIO', 'NWC'), feature_group_count=x.shape[-1])
    return y + b


def axial_rope_angles(n):
    pos = jnp.arange(n, dtype=jnp.int32)
    row = (pos // GRID_W).astype(jnp.float32)
    col = (pos % GRID_W).astype(jnp.float32)
    n_freq = M_HEAD_DIM // 4
    inv = ROPE_BASE ** (-jnp.arange(n_freq, dtype=jnp.float32) / n_freq)
    return row[:, None] * inv, col[:, None] * inv


def rotate_half(x, ang):
    x1, x2 = jnp.split(x, 2, axis=-1)
    cos = jnp.cos(ang)[:, None, :]
    sin = jnp.sin(ang)[:, None, :]
    return jnp.concatenate([x1 * cos - x2 * sin, x1 * sin + x2 * cos], axis=-1)


def axial_rope(x, ang_row, ang_col):
    x_row, x_col = jnp.split(x, 2, axis=-1)
    return jnp.concatenate([rotate_half(x_row, ang_row), rotate_half(x_col, ang_col)], axis=-1)


def mlstm_qkv(q_pre, k_pre, v, conv_w, conv_b, ang):
    bsz, n, _ = q_pre.shape
    qk = jax.nn.silu(short_conv(jnp.concatenate([q_pre, k_pre], axis=-1), conv_w, conv_b))
    q, k = jnp.split(qk, 2, axis=-1)
    heads = lambda a: a.reshape(bsz, n, M_HEADS, M_HEAD_DIM).astype(jnp.float32)
    q, k, v = heads(q), heads(k), heads(v)
    if ang is not None:
        q = axial_rope(q, ang[0], ang[1])
        k = axial_rope(k, ang[0], ang[1])
    return q, k * (M_HEAD_DIM ** -0.5), v


def mlstm_zero_state(bsz):
    return (jnp.zeros((bsz, M_HEADS, M_HEAD_DIM, M_HEAD_DIM), jnp.float32),
            jnp.zeros((bsz, M_HEADS, M_HEAD_DIM), jnp.float32),
            jnp.zeros((bsz, M_HEADS), jnp.float32))


def mlstm_chunk_scan(q, k, v, i_pre, log_f, state):
    bsz, n, n_heads, dh = q.shape
    n_chunks = n // M_CHUNK

    def chunks(a):
        a = a.reshape((bsz, n_chunks, M_CHUNK) + a.shape[2:])
        return jnp.moveaxis(a, (1, 2), (0, 3))

    tril = jnp.tril(jnp.ones((M_CHUNK, M_CHUNK), dtype=bool))

    def step(carry, xs):
        c_mat, n_vec, m = carry
        qc, kc, vc, ic, fc = xs
        b = jnp.cumsum(fc, axis=-1)
        log_w = jnp.where(tril, b[..., :, None] - b[..., None, :] + ic[..., None, :], -jnp.inf)
        carry_log = b + m[..., None]
        m_t = jnp.maximum(carry_log, jnp.max(log_w, axis=-1))
        w = jnp.exp(log_w - m_t[..., None])
        s = jnp.einsum('bhtd,bhsd->bhts', qc, kc) * w
        c_scale = jnp.exp(carry_log - m_t)
        num = jnp.einsum('bhts,bhsd->bhtd', s, vc) + c_scale[..., None] * jnp.einsum('bhde,bhte->bhtd', c_mat, qc)
        den = jnp.sum(s, axis=-1) + c_scale * jnp.einsum('bhe,bhte->bht', n_vec, qc)
        h = num / jnp.maximum(jnp.abs(den), jnp.exp(-m_t))[..., None]
        b_end = b[..., -1]
        log_u = b_end[..., None] - b + ic
        m_new = jnp.maximum(b_end + m, jnp.max(log_u, axis=-1))
        u = jnp.exp(log_u - m_new[..., None])
        decay = jnp.exp(b_end + m - m_new)
        c_new = decay[..., None, None] * c_mat + jnp.einsum('bhs,bhsd,bhse->bhde', u, vc, kc)
        n_new = decay[..., None] * n_vec + jnp.einsum('bhs,bhse->bhe', u, kc)
        return (c_new, n_new, m_new), h

    state, h = lax.scan(step, state, (chunks(q), chunks(k), chunks(v), chunks(i_pre), chunks(log_f)))
    h = jnp.moveaxis(h, (0, 3), (1, 2)).reshape(bsz, n, n_heads, dh)
    return h, state


def mlstm_bidir(q, k, v, gates, state_f, state_b):
    i_f, f_f, i_b, f_b = jnp.split(gates.astype(jnp.float32), 4, axis=-1)
    h_f, fin_f = mlstm_chunk_scan(q, k, v, i_f, jax.nn.log_sigmoid(f_f), state_f)
    flip = lambda a: jnp.flip(a, axis=1)
    h_b, fin_b = mlstm_chunk_scan(flip(q), flip(k), flip(v), flip(i_b), flip(jax.nn.log_sigmoid(f_b)), state_b)
    return h_f + flip(h_b), fin_f, fin_b


def mlstm_out(h, o_pre, norm_w):
    bsz, n = h.shape[:2]
    mu = jnp.mean(h, axis=-1, keepdims=True)
    var = jnp.mean(jnp.square(h - mu), axis=-1, keepdims=True)
    hn = ((h - mu) * lax.rsqrt(var + EPS)).reshape(bsz, n, M_WIDTH)
    return (hn * norm_w.astype(jnp.float32) * jax.nn.sigmoid(o_pre.astype(jnp.float32))).astype(o_pre.dtype)


def neighbourhood_attention(q, k, v, k_ctx, v_ctx, rpb):
    bsz, n, n_heads, dh = q.shape
    rows = n // GRID_W
    win_r = min(WIN_R, rows)
    n_cb = GRID_W // Q_COLS
    q_col = np.arange(GRID_W).reshape(n_cb, Q_COLS)
    col_start = np.clip(q_col - WIN_C // 2, 0, GRID_W - WIN_C)
    k_start = np.clip(np.arange(n_cb) * Q_COLS - WIN_C // 2, 0, GRID_W - K_COLS)
    k_col = k_start[:, None] + np.arange(K_COLS)
    col_ok = (k_col[:, None, :] >= col_start[:, :, None]) & (k_col[:, None, :] < col_start[:, :, None] + WIN_C)
    dc_idx = np.clip(k_col[:, None, :] - q_col[:, :, None] + WIN_C - 1, 0, 2 * WIN_C - 2)
    scale = dh ** -0.5
    n_loc = win_r * K_COLS
    q_g = q.reshape(bsz, rows, n_cb, Q_COLS, n_heads, dh)
    k_g = k.reshape(bsz, rows, GRID_W, n_heads, dh)[:, :, k_col]
    v_g = v.reshape(bsz, rows, GRID_W, n_heads, dh)[:, :, k_col]

    def row_block(r):
        r0 = jnp.clip(r - win_r // 2, 0, rows - win_r)
        k_r = lax.dynamic_slice_in_dim(k_g, r0, win_r, axis=1)
        v_r = lax.dynamic_slice_in_dim(v_g, r0, win_r, axis=1)
        q_r = lax.dynamic_index_in_dim(q_g, r, axis=1, keepdims=False)
        dr_idx = r0 + jnp.arange(win_r) - r + WIN_R - 1
        bias = rpb[:, dr_idx[None, None, :, None], dc_idx[:, :, None, :]].astype(jnp.float32)
        s_loc = jnp.einsum('bjqhd,bwjkhd->bhjqwk', q_r, k_r).astype(jnp.float32) * scale + bias
        s_loc = jnp.where(col_ok[:, :, None, :], s_loc, -jnp.inf).reshape(bsz, n_heads, n_cb, Q_COLS, n_loc)
        s_ctx = jnp.einsum('bjqhd,bchd->bhjqc', q_r, k_ctx).astype(jnp.float32) * scale
        p = jax.nn.softmax(jnp.concatenate([s_loc, s_ctx], axis=-1), axis=-1).astype(v.dtype)
        p_loc = p[..., :n_loc].reshape(bsz, n_heads, n_cb, Q_COLS, win_r, K_COLS)
        o = jnp.einsum('bhjqwk,bwjkhd->bjqhd', p_loc, v_r) + jnp.einsum('bhjqc,bchd->bjqhd', p[..., n_loc:], v_ctx)
        return o.reshape(bsz, GRID_W, n_heads, dh)

    out = lax.map(row_block, jnp.arange(rows))
    return jnp.moveaxis(out, 0, 1).reshape(bsz, n, n_heads, dh)


def context_attention(q, k, v):
    s = jnp.einsum('bqhd,bkhd->bhqk', q, k).astype(jnp.float32) * (q.shape[-1] ** -0.5)
    p = jax.nn.softmax(s, axis=-1).astype(v.dtype)
    return jnp.einsum('bhqk,bkhd->bqhd', p, v)


def s5_discretise(lam_re, lam_im, log_dt, b_re, b_im):
    lam = lax.complex(jnp.minimum(lam_re.astype(jnp.float32), S5_MAX_RE), lam_im.astype(jnp.float32))
    lam_bar = jnp.exp(lam * jnp.exp(log_dt.astype(jnp.float32))[:, None])
    b_bar = ((lam_bar - 1.0) / lam)[..., None] * lax.complex(b_re.astype(jnp.float32), b_im.astype(jnp.float32))
    return lam_bar, b_bar


def linear_recurrence_combine(left, right):
    a_l, b_l = left
    a_r, b_r = right
    return a_r * a_l, a_r * b_l + b_r


def s5_scan(u, lam_bar, b_bar, h0):
    bu = jnp.einsum('gph,bngh->bngp', b_bar, u.astype(jnp.complex64))
    bu = bu.at[:, 0].add(lam_bar * h0)
    a = jnp.broadcast_to(lam_bar, (1, u.shape[1]) + lam_bar.shape)
    _, states = lax.associative_scan(linear_recurrence_combine, (a, bu), axis=1)
    return states


def s5_states(u, disc, h0_f, h0_b):
    bsz, n, _ = u.shape
    ug = u.astype(jnp.float32).reshape(bsz, n, S5_GROUPS, S5_GROUP)
    st_f = s5_scan(ug, disc[0][0], disc[0][1], h0_f)
    st_b = jnp.flip(s5_scan(jnp.flip(ug, axis=1), disc[1][0], disc[1][1], h0_b), axis=1)
    return st_f, st_b


def s5_readout(st_f, st_b, c_f, c_b):
    y = jnp.real(jnp.einsum('ghp,bngp->bngh', c_f, st_f)) + jnp.real(jnp.einsum('ghp,bngp->bngh', c_b, st_b))
    return y.reshape(y.shape[0], y.shape[1], S5_WIDTH)


def s5_out(y_ssm, u, d, glu_w, glu_b):
    y = (y_ssm + d.astype(jnp.float32) * u.astype(jnp.float32)).astype(u.dtype)
    g = jax.nn.gelu(y)
    return g * jax.nn.sigmoid(g @ glu_w + glu_b)


def merge_branches(y_m, y_na, y_s5, gate_pre, lp):
    g = jax.nn.sigmoid(gate_pre.astype(jnp.float32)).astype(gate_pre.dtype)
    g_m, g_na, g_s5 = jnp.split(g, N_BRANCH, axis=-1)
    y = g_m * (y_m @ lp['w_branch_m']) + g_na * (y_na @ lp['w_branch_na']) + g_s5 * (y_s5 @ lp['w_branch_s5'])
    return y @ lp['w_out']


def token_mixers(hx, hc, lp, ang, ctx_out):
    bsz, n, _ = hx.shape
    n_ctx = hc.shape[1]
    w_parts = split_columns(lp['w_in'])
    px = [hx @ w for w in w_parts]
    pc = lambda i: hc @ w_parts[i]
    qx, kx, vx = mlstm_qkv(px[0], px[1], px[2], lp['m_conv_w'], lp['m_conv_b'], ang)
    qc, kc, vc = mlstm_qkv(pc(0), pc(1), pc(2), lp['m_conv_w'], lp['m_conv_b'], None)
    zero = mlstm_zero_state(bsz)
    hm_c, fin_f, fin_b = mlstm_bidir(qc, kc, vc, pc(4) + lp['m_gate_b'], zero, zero)
    hm_x, _, _ = mlstm_bidir(qx, kx, vx, px[4] + lp['m_gate_b'], fin_f, fin_b)
    ym_x = mlstm_out(hm_x, px[3], lp['m_norm'])
    heads = lambda a: a.reshape(a.shape[0], a.shape[1], NA_HEADS, NA_HEAD_DIM)
    k_ctx, v_ctx = heads(pc(6)), heads(pc(7))
    yn_x = neighbourhood_attention(heads(px[5]), heads(px[6]), heads(px[7]), k_ctx, v_ctx, lp['na_rpb']).reshape(bsz, n, NA_WIDTH)
    disc = [s5_discretise(lp['s5_lam_re'][d], lp['s5_lam_im'][d], lp['s5_log_dt'][d], lp['s5_b_re'][d], lp['s5_b_im'][d]) for d in range(2)]
    c_f, c_b = [lax.complex(lp['s5_c_re'][d].astype(jnp.float32), lp['s5_c_im'][d].astype(jnp.float32)) for d in range(2)]
    zero_s = jnp.zeros((bsz, S5_GROUPS, S5_STATE), jnp.complex64)
    u_c = pc(8)
    sc_f, sc_b = s5_states(u_c, disc, zero_s, zero_s)
    sx_f, sx_b = s5_states(px[8], disc, sc_f[:, -1], sc_b[:, 0])
    ys_x = s5_out(s5_readout(sx_f, sx_b, c_f, c_b), px[8], lp['s5_d'], lp['s5_glu_w'], lp['s5_glu_b'])
    out_x = merge_branches(ym_x, yn_x, ys_x, px[9], lp)
    if not ctx_out:
        return out_x, None
    ym_c = mlstm_out(hm_c, pc(3), lp['m_norm'])
    yn_c = context_attention(heads(pc(5)), k_ctx, v_ctx).reshape(bsz, n_ctx, NA_WIDTH)
    ys_c = s5_out(s5_readout(sc_f, sc_b, c_f, c_b), u_c, lp['s5_d'], lp['s5_glu_w'], lp['s5_glu_b'])
    out_c = merge_branches(ym_c, yn_c, ys_c, pc(9), lp)
    return out_x, out_c


def swiglu(h, w_gate, w_up, w_down):
    return (jax.nn.silu(h @ w_gate) * (h @ w_up)) @ w_down


def moe_swiglu(h, router, w_gate, w_up, w_down):
    shape = h.shape
    t = h.reshape(-1, shape[-1])
    logits = (t @ router).astype(jnp.float32)
    top_val, top_idx = lax.top_k(logits, TOP_K)
    top_p = jax.nn.softmax(top_val, axis=-1)
    combine = jnp.sum(jax.nn.one_hot(top_idx, N_EXPERTS, dtype=jnp.float32) * top_p[..., None], axis=1).astype(t.dtype)
    out = jnp.zeros_like(t)
    for e in range(N_EXPERTS):
        out = out + combine[:, e:e + 1] * swiglu(t, w_gate[e], w_up[e], w_down[e])
    return out.reshape(shape)


def setup_inputs(seed: int = 0) -> dict:
    key = jax.random.key(seed)
    ks = iter(jax.random.split(key, 40))

    def nrm(shape, scale):
        return jax.random.normal(next(ks), shape, jnp.float32) * scale

    def gain(shape):
        return 1.0 + nrm(shape, 0.05)

    d = D_MODEL
    forget_b = jnp.linspace(3.0, 6.0, M_HEADS, dtype=jnp.float32)
    zeros_h = jnp.zeros((M_HEADS,), jnp.float32)
    gate_b_base = jnp.concatenate([zeros_h, forget_b, zeros_h, forget_b])
    n_idx = jnp.arange(S5_STATE, dtype=jnp.float32)
    s5_shape = (DEPTH, 2, S5_GROUPS, S5_STATE)
    return {
        'x': nrm((BATCH, SEQ, d), 1.0),
        'c': nrm((BATCH, d), 1.0),
        'ctx': nrm((BATCH, CTX_LEN, d), 1.0),
        'c_ctx': nrm((d,), 1.0),
        'ada_w': nrm((DEPTH, d, N_MOD * d), 0.5 * d ** -0.5),
        'ada_b': nrm((DEPTH, N_MOD * d), 0.02),
        'norm_mix_pre': gain((DEPTH, d)),
        'norm_mix_post': gain((DEPTH, d)),
        'norm_ffn_pre': gain((DEPTH, d)),
        'norm_ffn_post': gain((DEPTH, d)),
        'w_in': nrm((DEPTH, d, IN_COLS), d ** -0.5),
        'm_gate_b': gate_b_base + nrm((DEPTH, M_GATES), 0.1),
        'm_conv_w': nrm((DEPTH, M_CONV, 2 * M_WIDTH), M_CONV ** -0.5),
        'm_conv_b': nrm((DEPTH, 2 * M_WIDTH), 0.02),
        'm_norm': gain((DEPTH, M_WIDTH)),
        'na_rpb': nrm((DEPTH, NA_HEADS, 2 * WIN_R - 1, 2 * WIN_C - 1), 0.02),
        's5_lam_re': -0.5 + nrm(s5_shape, 0.01),
        's5_lam_im': jnp.pi * n_idx + nrm(s5_shape, 0.01),
        's5_log_dt': jax.random.uniform(next(ks), (DEPTH, 2, S5_GROUPS), jnp.float32, minval=math.log(1e-3), maxval=math.log(1e-1)),
        's5_b_re': nrm((DEPTH, 2, S5_GROUPS, S5_STATE, S5_GROUP), (2 * S5_GROUP) ** -0.5),
        's5_b_im': nrm((DEPTH, 2, S5_GROUPS, S5_STATE, S5_GROUP), (2 * S5_GROUP) ** -0.5),
        's5_c_re': nrm((DEPTH, 2, S5_GROUPS, S5_GROUP, S5_STATE), S5_STATE ** -0.5),
        's5_c_im': nrm((DEPTH, 2, S5_GROUPS, S5_GROUP, S5_STATE), S5_STATE ** -0.5),
        's5_d': nrm((DEPTH, S5_WIDTH), 1.0),
        's5_glu_w': nrm((DEPTH, S5_WIDTH, S5_WIDTH), S5_WIDTH ** -0.5),
        's5_glu_b': nrm((DEPTH, S5_WIDTH), 0.02),
        'w_branch_m': nrm((DEPTH, M_WIDTH, d), M_WIDTH ** -0.5),
        'w_branch_na': nrm((DEPTH, NA_WIDTH, d), NA_WIDTH ** -0.5),
        'w_branch_s5': nrm((DEPTH, S5_WIDTH, d), S5_WIDTH ** -0.5),
        'w_out': nrm((DEPTH, d, d), d ** -0.5),
        'ffn_w_gate': nrm((N_DENSE_LAYERS, d, FF_DIM), d ** -0.5),
        'ffn_w_up': nrm((N_DENSE_LAYERS, d, FF_DIM), d ** -0.5),
        'ffn_w_down': nrm((N_DENSE_LAYERS, FF_DIM, d), FF_DIM ** -0.5),
        'moe_router': nrm((N_MOE_LAYERS, d, N_EXPERTS), d ** -0.5),
        'moe_w_gate': nrm((N_MOE_LAYERS, N_EXPERTS, d, EXPERT_FF), d ** -0.5),
        'moe_w_up': nrm((N_MOE_LAYERS, N_EXPERTS, d, EXPERT_FF), d ** -0.5),
        'moe_w_down': nrm((N_MOE_LAYERS, N_EXPERTS, EXPERT_FF, d), EXPERT_FF ** -0.5),
    }


def reference(x, c, ctx, c_ctx, ada_w, ada_b, norm_mix_pre, norm_mix_post, norm_ffn_pre, norm_ffn_post,
              w_in, m_gate_b, m_conv_w, m_conv_b, m_norm, na_rpb, s5_lam_re, s5_lam_im, s5_log_dt,
              s5_b_re, s5_b_im, s5_c_re, s5_c_im, s5_d, s5_glu_w, s5_glu_b, w_branch_m, w_branch_na,
              w_branch_s5, w_out, ffn_w_gate, ffn_w_up, ffn_w_down, moe_router, moe_w_gate, moe_w_up, moe_w_down):
    ang = axial_rope_angles(x.shape[1])
    xc = ctx
    for l in range(DEPTH):
        last = l == DEPTH - 1
        lp = {
            'w_in': w_in[l], 'm_gate_b': m_gate_b[l], 'm_conv_w': m_conv_w[l], 'm_conv_b': m_conv_b[l],
            'm_norm': m_norm[l], 'na_rpb': na_rpb[l], 's5_lam_re': s5_lam_re[l], 's5_lam_im': s5_lam_im[l],
            's5_log_dt': s5_log_dt[l], 's5_b_re': s5_b_re[l], 's5_b_im': s5_b_im[l], 's5_c_re': s5_c_re[l],
            's5_c_im': s5_c_im[l], 's5_d': s5_d[l], 's5_glu_w': s5_glu_w[l], 's5_glu_b': s5_glu_b[l],
            'w_branch_m': w_branch_m[l], 'w_branch_na': w_branch_na[l], 'w_branch_s5': w_branch_s5[l],
            'w_out': w_out[l],
        }
        mx = [m[:, None, :] for m in adaln(c, ada_w[l], ada_b[l])]
        mc = adaln(c_ctx, ada_w[l], ada_b[l])
        hx = modulate(rms_norm(x, norm_mix_pre[l]), mx[0], mx[1])
        hc = modulate(rms_norm(xc, norm_mix_pre[l]), mc[0], mc[1])
        out_x, out_c = token_mixers(hx, hc, lp, ang, not last)
        x = x + mx[2] * rms_norm(out_x, norm_mix_post[l])
        j = l // 2
        if l % 2 == 0:
            def ffn(h, j=j):
                return swiglu(h, ffn_w_gate[j], ffn_w_up[j], ffn_w_down[j])
        else:
            def ffn(h, j=j):
                return moe_swiglu(h, moe_router[j], moe_w_gate[j], moe_w_up[j], moe_w_down[j])
        x = x + mx[5] * rms_norm(ffn(modulate(rms_norm(x, norm_ffn_pre[l]), mx[3], mx[4])), norm_ffn_post[l])
        if not last:
            xc = xc + mc[2] * rms_norm(out_c, norm_mix_post[l])
            xc = xc + mc[5] * rms_norm(ffn(modulate(rms_norm(xc, norm_ffn_pre[l]), mc[3], mc[4])), norm_ffn_post[l])
    return x
```

```python
import functools
import math

import jax
import jax.numpy as jnp
from jax import lax
from jax.experimental import pallas as pl
from jax.experimental.pallas import tpu as pltpu

F32 = jnp.float32
BF16 = jnp.bfloat16

D_MODEL = 1024
EPS = 1e-6
N_MOD = 6
GRID_W = 64

M_HEADS = 4
M_HEAD_DIM = 128
M_WIDTH = M_HEADS * M_HEAD_DIM
M_GATES = 4 * M_HEADS
M_CHUNK = 128
ROPE_BASE = 10000.0

NA_HEADS = 8
NA_HEAD_DIM = 64
NA_WIDTH = NA_HEADS * NA_HEAD_DIM
WIN_R = 8
WIN_C = 16
NEG_BIG = -1e30

S5_GROUP = 16
S5_WIDTH = 512
S5_GROUPS = S5_WIDTH // S5_GROUP
S5_STATE = 64
S5_MAX_RE = -1e-4
S5_CHUNK = 16
S5_PAIRS = S5_GROUPS // 2

N_EXPERTS = 8
TOP_K = 2

LANES = 128
VMEM_LIMIT = 52 * 1024 * 1024

COL_QK, COL_V, COL_O, COL_NAQ, COL_NAK, COL_NAV, COL_U = 0, 2, 3, 4, 5, 6, 7
COL_GATE = 4
IN_MAIN = 7168


def _params(sem):
    return pltpu.CompilerParams(dimension_semantics=sem, vmem_limit_bytes=VMEM_LIMIT)


def _dot(a, b):
    return jnp.dot(a, b, preferred_element_type=F32)


def _dot_nt(a, b):
    return lax.dot_general(a, b, (((1,), (1,)), ((), ())), preferred_element_type=F32)


def _split3(x):
    hi = x.astype(BF16)
    r1 = x - hi.astype(F32)
    mid = r1.astype(BF16)
    lo = (r1 - mid.astype(F32)).astype(BF16)
    return hi, mid, lo


def _rms(x, g):
    return x * lax.rsqrt(jnp.mean(x * x, axis=-1, keepdims=True) + EPS) * g


def _inproj_kernel(x_ref, g_ref, sh_ref, sc_ref, w_ref, wg_ref, o_ref, og_ref, hn_ref):
    @pl.when(pl.program_id(1) == 0)
    def _():
        h = _rms(x_ref[...], g_ref[...]) * (1.0 + sc_ref[0]) + sh_ref[0]
        hb = h.astype(BF16)
        hn_ref[...] = hb
        og_ref[...] = _dot(hb, wg_ref[...])

    o_ref[...] = _dot(hn_ref[...], w_ref[...]).astype(o_ref.dtype)


def _inproj(x, g, shift, scale, w_main, w_gates, rows_per_mod):
    t = x.shape[0]
    tm = min(1024, rows_per_mod)
    tn = 512
    assert rows_per_mod % tm == 0 and t % tm == 0
    per = rows_per_mod // tm
    return pl.pallas_call(
        _inproj_kernel,
        grid=(t // tm, IN_MAIN // tn),
        in_specs=[
            pl.BlockSpec((tm, D_MODEL), lambda i, j: (i, 0)),
            pl.BlockSpec((1, D_MODEL), lambda i, j: (0, 0)),
            pl.BlockSpec((1, 1, D_MODEL), lambda i, j: (i // per, 0, 0)),
            pl.BlockSpec((1, 1, D_MODEL), lambda i, j: (i // per, 0, 0)),
            pl.BlockSpec((D_MODEL, tn), lambda i, j: (0, j)),
            pl.BlockSpec((D_MODEL, LANES), lambda i, j: (0, 0)),
        ],
        out_specs=[
            pl.BlockSpec((tm, tn), lambda i, j: (i, j)),
            pl.BlockSpec((tm, LANES), lambda i, j: (i, 0)),
        ],
        out_shape=[jax.ShapeDtypeStruct((t, IN_MAIN), BF16), jax.ShapeDtypeStruct((t, LANES), F32)],
        scratch_shapes=[pltpu.VMEM((tm, D_MODEL), BF16)],
        compiler_params=_params(("parallel", "arbitrary")),
        name="inproj",
    )(x, g, shift, scale, w_main, w_gates)


def _adaln_kernel(c_ref, w_ref, b_ref, o_ref):
    c = c_ref[...]
    s = c * jax.nn.sigmoid(c)
    o_ref[...] = _dot(s.astype(BF16), w_ref[...]) + b_ref[...]


def _adaln(cvecs, w, b):
    n = w.shape[1]
    tn = 1024
    return pl.pallas_call(
        _adaln_kernel,
        grid=(n // tn,),
        in_specs=[
            pl.BlockSpec(cvecs.shape, lambda j: (0, 0)),
            pl.BlockSpec((D_MODEL, tn), lambda j: (0, j)),
            pl.BlockSpec((1, tn), lambda j: (0, j)),
        ],
        out_specs=pl.BlockSpec((cvecs.shape[0], tn), lambda j: (0, j)),
        out_shape=jax.ShapeDtypeStruct((cvecs.shape[0], n), F32),
        compiler_params=_params(("parallel",)),
        name="adaln",
    )(cvecs, w, b)


HALO = 16


def _swap32(x):
    lane = lax.broadcasted_iota(jnp.int32, x.shape, 1)
    fwd = pltpu.roll(x, 96, 1)
    bwd = pltpu.roll(x, 32, 1)
    return jnp.where((lane % 64) < 32, fwd, bwd)


def _qkconv_kernel(x_ref, pv_ref, nx_ref, w_ref, b_ref, cos_ref, sin_ref, o_ref, *, tiles_per_seq, rope):
    i = pl.program_id(0)
    tm = x_ref.shape[0]
    x = x_ref[...].astype(F32)
    first = (i % tiles_per_seq) == 0
    last = (i % tiles_per_seq) == tiles_per_seq - 1
    prev_row = jnp.where(first, 0.0, pv_ref[HALO - 1:HALO, :].astype(F32))
    next_row = jnp.where(last, 0.0, nx_ref[0:1, :].astype(F32))
    row = lax.broadcasted_iota(jnp.int32, x.shape, 0)
    xp = jnp.where(row == 0, prev_row, pltpu.roll(x, 1, 0))
    xn = jnp.where(row == tm - 1, next_row, pltpu.roll(x, tm - 1, 0))
    y = xp * w_ref[0:1, :] + x * w_ref[1:2, :] + xn * w_ref[2:3, :] + b_ref[...]
    y = y * jax.nn.sigmoid(y)
    kscale = M_HEAD_DIM ** -0.5
    for hh in range(2 * M_HEADS):
        ys = y[:, hh * LANES:(hh + 1) * LANES]
        if rope:
            ys = ys * cos_ref[...] + _swap32(ys) * sin_ref[...]
        if hh >= M_HEADS:
            ys = ys * kscale
        o_ref[:, hh * LANES:(hh + 1) * LANES] = ys.astype(o_ref.dtype)


def _qkconv(px, conv_w, conv_b, cos_t, sin_t, seq_len, rope):
    t = px.shape[0]
    tm = min(512, seq_len)
    assert seq_len % tm == 0
    tiles = seq_len // tm
    hb = tm // HALO
    nh = t // HALO
    return pl.pallas_call(
        functools.partial(_qkconv_kernel, tiles_per_seq=tiles, rope=rope),
        grid=(t // tm,),
        in_specs=[
            pl.BlockSpec((tm, 2 * M_WIDTH), lambda i: (i, 0)),
            pl.BlockSpec((HALO, 2 * M_WIDTH), lambda i: (jnp.maximum(i * hb - 1, 0), 0)),
            pl.BlockSpec((HALO, 2 * M_WIDTH), lambda i: (jnp.minimum((i + 1) * hb, nh - 1), 0)),
            pl.BlockSpec((3, 2 * M_WIDTH), lambda i: (0, 0)),
            pl.BlockSpec((1, 2 * M_WIDTH), lambda i: (0, 0)),
            pl.BlockSpec((tm, LANES), lambda i: (i % tiles, 0)),
            pl.BlockSpec((tm, LANES), lambda i: (i % tiles, 0)),
        ],
        out_specs=pl.BlockSpec((tm, 2 * M_WIDTH), lambda i: (i, 0)),
        out_shape=jax.ShapeDtypeStruct((t, 2 * M_WIDTH), BF16),
        compiler_params=_params(("parallel",)),
        name="qkconv",
    )(px, px, px, conv_w, conv_b, cos_t, sin_t)


def _rope_tables(n):
    pos = jnp.arange(n, dtype=jnp.int32)
    row = (pos // GRID_W).astype(F32)
    col = (pos % GRID_W).astype(F32)
    n_freq = M_HEAD_DIM // 4
    inv = ROPE_BASE ** (-jnp.arange(n_freq, dtype=F32) / n_freq)
    ar, ac = row[:, None] * inv, col[:, None] * inv
    cos_t = jnp.concatenate([jnp.cos(ar), jnp.cos(ar), jnp.cos(ac), jnp.cos(ac)], axis=-1)
    sin_t = jnp.concatenate([-jnp.sin(ar), jnp.sin(ar), -jnp.sin(ac), jnp.sin(ac)], axis=-1)
    return cos_t, sin_t


def _mlstm_kernel(*refs, reverse, final):
    if final:
        (qk_ref, v_ref, g_ref, gb_ref, c0_ref, n0_ref, m0_ref, hp_ref, op_ref, nw_ref,
         out_ref, cf_ref, nf_ref, mf_ref, c_scr, n_scr, m_scr) = refs
    else:
        (qk_ref, v_ref, g_ref, gb_ref, c0_ref, n0_ref, m0_ref,
         out_ref, cf_ref, nf_ref, mf_ref, c_scr, n_scr, m_scr) = refs
    c_idx = pl.program_id(1)
    n_c = pl.num_programs(1)
    L = M_CHUNK

    @pl.when(c_idx == 0)
    def _():
        c_scr[...] = c0_ref[0]
        n_scr[...] = n0_ref[0]
        m_scr[...] = m0_ref[0]

    g = g_ref[...] + gb_ref[...]
    logf = jax.nn.log_sigmoid(g)
    ti = lax.broadcasted_iota(jnp.int32, (L, L), 0)
    si = lax.broadcasted_iota(jnp.int32, (L, L), 1)
    keep = (si >= ti) if reverse else (si <= ti)
    tri = jnp.where(keep, 1.0, 0.0).astype(BF16)
    f_hi, f_mid, f_lo = _split3(logf)
    bsum = _dot(tri, f_hi) + _dot(tri, f_mid) + _dot(tri, f_lo)
    g_t = g.T
    b_t = bsum.T
    i_off = 2 * M_HEADS if reverse else 0
    f_off = i_off + M_HEADS
    end = 0 if reverse else L - 1

    for h in range(M_HEADS):
        q = qk_ref[:, h * LANES:(h + 1) * LANES]
        k = qk_ref[:, M_WIDTH + h * LANES:M_WIDTH + (h + 1) * LANES]
        v = v_ref[:, h * LANES:(h + 1) * LANES]
        c_mat = c_scr[h]
        n_vec = n_scr[h]
        m_prev = m_scr[h][:, 0:1]
        b_col = bsum[:, f_off + h:f_off + h + 1]
        i_col = g[:, i_off + h:i_off + h + 1]
        b_row = b_t[f_off + h:f_off + h + 1, :]
        i_row = g_t[i_off + h:i_off + h + 1, :]
        b_end = b_row[:, end:end + 1]

        log_w = jnp.where(keep, b_col - b_row + i_row, -jnp.inf)
        carry_log = b_col + m_prev
        m_t = jnp.maximum(carry_log, jnp.max(log_w, axis=-1, keepdims=True))
        w = jnp.exp(log_w - m_t)
        s = _dot_nt(q, k) * w
        c_scale = jnp.exp(carry_log - m_t)
        qf = q.astype(F32)
        num = _dot(s.astype(BF16), v) + c_scale * _dot_nt(q, c_mat.astype(BF16))
        den = jnp.sum(s, axis=-1, keepdims=True) + c_scale * jnp.sum(qf * n_vec, axis=-1, keepdims=True)
        hh = num / jnp.maximum(jnp.abs(den), jnp.exp(-m_t))

        m_new = jnp.maximum(b_end + m_prev, jnp.max(b_end - b_row + i_row, axis=-1, keepdims=True))
        u_col = jnp.exp(b_end - b_col + i_col - m_new)
        decay = jnp.exp(b_end + m_prev - m_new)
        vu_t = (v.astype(F32) * u_col).T.astype(BF16)
        c_scr[h] = decay * c_mat + _dot(vu_t, k)
        n_scr[h] = decay * n_vec + jnp.sum(k.astype(F32) * u_col, axis=0, keepdims=True)
        m_scr[h] = jnp.broadcast_to(m_new, (1, LANES))

        if final:
            hs = hh + hp_ref[:, h * LANES:(h + 1) * LANES]
            mu = jnp.mean(hs, axis=-1, keepdims=True)
            var = jnp.mean(jnp.square(hs - mu), axis=-1, keepdims=True)
            hn = (hs - mu) * lax.rsqrt(var + EPS)
            o_pre = op_ref[:, h * LANES:(h + 1) * LANES].astype(F32)
            y = hn * nw_ref[:, h * LANES:(h + 1) * LANES] * jax.nn.sigmoid(o_pre)
            out_ref[:, h * LANES:(h + 1) * LANES] = y.astype(out_ref.dtype)
        else:
            out_ref[:, h * LANES:(h + 1) * LANES] = hh.astype(out_ref.dtype)

    @pl.when(c_idx == n_c - 1)
    def _():
        cf_ref[0] = c_scr[...]
        nf_ref[0] = n_scr[...]
        mf_ref[0] = m_scr[...]


def _mlstm_scan(qk, px, gates, gate_b, state, seq_len, reverse, hprev=None, norm_w=None):
    t = qk.shape[0]
    bsz = t // seq_len
    n_c = seq_len // M_CHUNK
    final = hprev is not None

    def row(b, c):
        return b * n_c + ((n_c - 1 - c) if reverse else c)

    in_specs = [
        pl.BlockSpec((M_CHUNK, 2 * M_WIDTH), lambda b, c: (row(b, c), 0)),
        pl.BlockSpec((M_CHUNK, M_WIDTH), lambda b, c: (row(b, c), COL_V)),
        pl.BlockSpec((M_CHUNK, LANES), lambda b, c: (row(b, c), 0)),
        pl.BlockSpec((1, LANES), lambda b, c: (0, 0)),
        pl.BlockSpec((1, M_HEADS, M_HEAD_DIM, M_HEAD_DIM), lambda b, c: (b, 0, 0, 0)),
        pl.BlockSpec((1, M_HEADS, 1, LANES), lambda b, c: (b, 0, 0, 0)),
        pl.BlockSpec((1, M_HEADS, 1, LANES), lambda b, c: (b, 0, 0, 0)),
    ]
    args = [qk, px, gates, gate_b, *state]
    if final:
        in_specs += [
            pl.BlockSpec((M_CHUNK, M_WIDTH), lambda b, c: (row(b, c), 0)),
            pl.BlockSpec((M_CHUNK, M_WIDTH), lambda b, c: (row(b, c), COL_O)),
            pl.BlockSpec((1, M_WIDTH), lambda b, c: (0, 0)),
        ]
        args += [hprev, px, norm_w]
    out_dtype = BF16 if final else F32
    return pl.pallas_call(
        functools.partial(_mlstm_kernel, reverse=reverse, final=final),
        grid=(bsz, n_c),
        in_specs=in_specs,
        out_specs=[
            pl.BlockSpec((M_CHUNK, M_WIDTH), lambda b, c: (row(b, c), 0)),
            pl.BlockSpec((1, M_HEADS, M_HEAD_DIM, M_HEAD_DIM), lambda b, c: (b, 0, 0, 0)),
            pl.BlockSpec((1, M_HEADS, 1, LANES), lambda b, c: (b, 0, 0, 0)),
            pl.BlockSpec((1, M_HEADS, 1, LANES), lambda b, c: (b, 0, 0, 0)),
        ],
        out_shape=[
            jax.ShapeDtypeStruct((t, M_WIDTH), out_dtype),
            jax.ShapeDtypeStruct((bsz, M_HEADS, M_HEAD_DIM, M_HEAD_DIM), F32),
            jax.ShapeDtypeStruct((bsz, M_HEADS, 1, LANES), F32),
            jax.ShapeDtypeStruct((bsz, M_HEADS, 1, LANES), F32),
        ],
        scratch_shapes=[
            pltpu.VMEM((M_HEADS, M_HEAD_DIM, M_HEAD_DIM), F32),
            pltpu.VMEM((M_HEADS, 1, LANES), F32),
            pltpu.VMEM((M_HEADS, 1, LANES), F32),
        ],
        compiler_params=_params(("parallel", "arbitrary")),
        name="mlstm_rev" if reverse else "mlstm_fwd",
    )(*args)


def _mlstm_zero_state(bsz):
    return (jnp.zeros((bsz, M_HEADS, M_HEAD_DIM, M_HEAD_DIM), F32),
            jnp.zeros((bsz, M_HEADS, 1, LANES), F32),
            jnp.zeros((bsz, M_HEADS, 1, LANES), F32))


def _mlstm_bidir(qk, px, gates, gate_b, norm_w, state_f, state_b, seq_len):
    h_f, *fin_f = _mlstm_scan(qk, px, gates, gate_b, state_f, seq_len, False)
    y, *fin_b = _mlstm_scan(qk, px, gates, gate_b, state_b, seq_len, True, hprev=h_f, norm_w=norm_w)
    return y, tuple(fin_f), tuple(fin_b)


NA_ROWS = 8


def _na_kernel(q_ref, kp_ref, kc_ref, kn_ref, vp_ref, vc_ref, vn_ref, kx_ref, vx_ref, bias_ref,
               o_ref, kwin, vwin, *, rows):
    r = pl.program_id(1)
    blk = NA_ROWS * GRID_W

    @pl.when(r % NA_ROWS == 0)
    def _():
        kwin[0:blk, :] = kp_ref[...]
        kwin[blk:2 * blk, :] = kc_ref[...]
        kwin[2 * blk:3 * blk, :] = kn_ref[...]
        vwin[0:blk, :] = vp_ref[...]
        vwin[blk:2 * blk, :] = vc_ref[...]
        vwin[2 * blk:3 * blk, :] = vn_ref[...]

    g = r // NA_ROWS
    r0 = jnp.clip(r - WIN_R // 2, 0, rows - WIN_R)
    off = pl.multiple_of((r0 - NA_ROWS * g + NA_ROWS) * GRID_W, GRID_W)
    n_loc = WIN_R * GRID_W
    lane = lax.broadcasted_iota(jnp.int32, (GRID_W, LANES), 1)
    scale = NA_HEAD_DIM ** -0.5
    for hp in range(NA_HEADS // 2):
        cols = slice(hp * LANES, (hp + 1) * LANES)
        q2 = q_ref[:, cols] * scale
        k2 = kwin[pl.ds(off, n_loc), cols]
        v2 = vwin[pl.ds(off, n_loc), cols]
        kx2 = kx_ref[:, cols]
        vx2 = vx_ref[:, cols]
        outs = []
        for sub in range(2):
            h = 2 * hp + sub
            sel = (lane < NA_HEAD_DIM) if sub == 0 else (lane >= NA_HEAD_DIM)
            qm = jnp.where(sel, q2, jnp.zeros_like(q2))
            s_loc = _dot_nt(qm, k2) + bias_ref[0, h]
            s_ctx = _dot_nt(qm, kx2)
            m = jnp.maximum(jnp.max(s_loc, axis=-1, keepdims=True), jnp.max(s_ctx, axis=-1, keepdims=True))
            p_loc = jnp.exp(s_loc - m)
            p_ctx = jnp.exp(s_ctx - m)
            l = jnp.sum(p_loc, axis=-1, keepdims=True) + jnp.sum(p_ctx, axis=-1, keepdims=True)
            o2 = _dot(p_loc.astype(BF16), v2) + _dot(p_ctx.astype(BF16), vx2)
            outs.append(o2 / l)
        o_ref[:, cols] = jnp.where(lane < NA_HEAD_DIM, outs[0], outs[1]).astype(o_ref.dtype)


def _na_bias_table(rpb):
    qc = jnp.arange(GRID_W)[:, None]
    kc = jnp.arange(GRID_W)[None, :]
    cs = jnp.clip(qc - WIN_C // 2, 0, GRID_W - WIN_C)
    ok = (kc >= cs) & (kc < cs + WIN_C)
    dc = jnp.clip(kc - qc + WIN_C - 1, 0, 2 * WIN_C - 2)
    dr = jnp.arange(WIN_R)[:, None] + jnp.arange(WIN_R)[None, :]
    tab = rpb.astype(F32)[:, dr][:, :, :, dc]
    tab = jnp.where(ok[None, None, None], tab, NEG_BIG)
    tab = jnp.transpose(tab, (1, 0, 3, 2, 4))
    return tab.reshape(WIN_R, NA_HEADS, GRID_W, WIN_R * GRID_W)


def _na_attention(px, pc, bias, seq_len, ctx_len):
    t = px.shape[0]
    bsz = t // seq_len
    rows = seq_len // GRID_W
    n_g = rows // NA_ROWS
    blk = NA_ROWS * GRID_W

    def kv_spec(col, shift):
        def imap(b, r):
            g = jnp.clip(r // NA_ROWS + shift, 0, n_g - 1)
            return (b * n_g + g, col)
        return pl.BlockSpec((blk, NA_WIDTH), imap)

    def bias_map(b, r):
        r0 = jnp.clip(r - WIN_R // 2, 0, rows - WIN_R)
        return (r0 - r + WIN_R - 1, 0, 0, 0)

    return pl.pallas_call(
        functools.partial(_na_kernel, rows=rows),
        grid=(bsz, rows),
        in_specs=[
            pl.BlockSpec((GRID_W, NA_WIDTH), lambda b, r: (b * rows + r, COL_NAQ)),
            kv_spec(COL_NAK, -1), kv_spec(COL_NAK, 0), kv_spec(COL_NAK, 1),
            kv_spec(COL_NAV, -1), kv_spec(COL_NAV, 0), kv_spec(COL_NAV, 1),
            pl.BlockSpec((ctx_len, NA_WIDTH), lambda b, r: (b, COL_NAK)),
            pl.BlockSpec((ctx_len, NA_WIDTH), lambda b, r: (b, COL_NAV)),
            pl.BlockSpec((1, NA_HEADS, GRID_W, WIN_R * GRID_W), bias_map),
        ],
        out_specs=pl.BlockSpec((GRID_W, NA_WIDTH), lambda b, r: (b * rows + r, 0)),
        out_shape=jax.ShapeDtypeStruct((t, NA_WIDTH), BF16),
        scratch_shapes=[pltpu.VMEM((3 * blk, NA_WIDTH), BF16), pltpu.VMEM((3 * blk, NA_WIDTH), BF16)],
        compiler_params=_params(("parallel", "arbitrary")),
        name="na_attn",
    )(px, px, px, px, px, px, px, pc, pc, bias)


def _ctx_attn_kernel(q_ref, k_ref, v_ref, o_ref):
    n = q_ref.shape[0]
    lane = lax.broadcasted_iota(jnp.int32, (n, LANES), 1)
    scale = NA_HEAD_DIM ** -0.5
    for hp in range(NA_HEADS // 2):
        cols = slice(hp * LANES, (hp + 1) * LANES)
        q2 = q_ref[:, cols] * scale
        k2 = k_ref[:, cols]
        v2 = v_ref[:, cols]
        outs = []
        for sub in range(2):
            sel = (lane < NA_HEAD_DIM) if sub == 0 else (lane >= NA_HEAD_DIM)
            s = _dot_nt(jnp.where(sel, q2, jnp.zeros_like(q2)), k2)
            p = jnp.exp(s - jnp.max(s, axis=-1, keepdims=True))
            outs.append(_dot(p.astype(BF16), v2) / jnp.sum(p, axis=-1, keepdims=True))
        o_ref[:, cols] = jnp.where(lane < NA_HEAD_DIM, outs[0], outs[1]).astype(o_ref.dtype)


def _ctx_attention(pc, ctx_len):
    t = pc.shape[0]
    return pl.pallas_call(
        _ctx_attn_kernel,
        grid=(t // ctx_len,),
        in_specs=[pl.BlockSpec((ctx_len, NA_WIDTH), lambda b, col=col: (b, col))
                  for col in (COL_NAQ, COL_NAK, COL_NAV)],
        out_specs=pl.BlockSpec((ctx_len, NA_WIDTH), lambda b: (b, 0)),
        out_shape=jax.ShapeDtypeStruct((t, NA_WIDTH), BF16),
        compiler_params=_params(("parallel",)),
        name="ctx_attn",
    )(pc, pc, pc)


def _s5_tables(lam_re, lam_im, log_dt, b_re, b_im, c_re, c_im):
    L, G, S, H = S5_CHUNK, S5_GROUPS, S5_STATE, S5_GROUP
    lam = lax.complex(jnp.minimum(lam_re.astype(F32), S5_MAX_RE), lam_im.astype(F32))
    dt = jnp.exp(log_dt.astype(F32))[..., None]
    lam_bar = jnp.exp(lam * dt)
    b_bar = ((lam_bar - 1.0) / lam)[..., None] * lax.complex(b_re.astype(F32), b_im.astype(F32))
    c = lax.complex(c_re.astype(F32), c_im.astype(F32))
    d = jnp.arange(L + 1, dtype=F32)
    lam_pow = jnp.exp((lam * dt)[:, None] * d[None, :, None, None])
    kern = jnp.real(jnp.einsum('xghp,xdgp,xgpj->xdghj', c, lam_pow[:, :L], b_bar))
    tt = jnp.arange(L)[None, :]
    ss = jnp.arange(L)[:, None]
    lag_f = jnp.clip(tt - ss, 0, L - 1)
    lag_b = jnp.clip(ss - tt, 0, L - 1)
    m_f = jnp.where((tt >= ss)[:, :, None, None, None], kern[0][lag_f], 0.0)
    m_b = jnp.where((ss >= tt)[:, :, None, None, None], kern[1][lag_b], 0.0)
    m = jnp.transpose(m_f + m_b, (2, 0, 4, 1, 3)).reshape(G, L * H, L * H)
    pw_f = lam_pow[0][L - 1 - jnp.arange(L)]
    pw_b = lam_pow[1][jnp.arange(L)]
    w_f = jnp.einsum('sgp,gpj->gsjp', pw_f, b_bar[0]).reshape(G, L * H, S)
    w_b = jnp.einsum('sgp,gpj->gsjp', pw_b, b_bar[1]).reshape(G, L * H, S)
    vv_f = jnp.einsum('ghp,tgp->gpth', c[0], lam_pow[0][1 + jnp.arange(L)]).reshape(G, S, L * H)
    vv_b = jnp.einsum('ghp,tgp->gpth', c[1], lam_pow[1][L - jnp.arange(L)]).reshape(G, S, L * H)

    def pair_diag(a):
        a = a.reshape(S5_PAIRS, 2, a.shape[1], a.shape[2])
        z = jnp.zeros_like(a[:, 0])
        return jnp.concatenate([jnp.concatenate([a[:, 0], z], axis=2),
                                jnp.concatenate([z, a[:, 1]], axis=2)], axis=1)

    m_pair = pair_diag(m)
    w_cat = jnp.concatenate([pair_diag(jnp.real(w_f)), pair_diag(jnp.imag(w_f)),
                             pair_diag(jnp.real(w_b)), pair_diag(jnp.imag(w_b))], axis=2)
    v_cat = jnp.concatenate([pair_diag(jnp.real(vv_f)), pair_diag(-jnp.imag(vv_f)),
                             pair_diag(jnp.real(vv_b)), pair_diag(-jnp.imag(vv_b))], axis=1)
    a = lam_pow[:, L].reshape(2, 1, G * S)
    return m_pair.astype(BF16), w_cat.astype(BF16), v_cat.astype(BF16), jnp.real(a), jnp.imag(a)


def _s5_to_pairs(u, seq_len):
    bsz = u.shape[0] // seq_len
    n_c = seq_len // S5_CHUNK
    a = u.reshape(bsz, n_c, S5_CHUNK, S5_PAIRS, 2, S5_GROUP)
    return jnp.transpose(a, (3, 1, 0, 4, 2, 5)).reshape(S5_PAIRS, n_c * bsz, 2 * S5_CHUNK * S5_GROUP)


def _s5_from_pairs(y, seq_len, bsz):
    n_c = seq_len // S5_CHUNK
    a = y.reshape(S5_PAIRS, n_c, bsz, 2, S5_CHUNK, S5_GROUP)
    return jnp.transpose(a, (2, 1, 4, 0, 3, 5)).reshape(bsz * seq_len, S5_WIDTH)


def _s5_local_kernel(u_ref, w_ref, sfr_ref, sfi_ref, sbr_ref, sbi_ref):
    s = _dot(u_ref[0], w_ref[0])
    sfr_ref[...] = s[:, 0:LANES]
    sfi_ref[...] = s[:, LANES:2 * LANES]
    sbr_ref[...] = s[:, 2 * LANES:3 * LANES]
    sbi_ref[...] = s[:, 3 * LANES:4 * LANES]


def _s5_local_states(up, w_cat):
    rows = up.shape[1]
    tr = min(1024, rows)
    assert rows % tr == 0
    width = S5_GROUPS * S5_STATE
    out = jax.ShapeDtypeStruct((rows, width), F32)
    return pl.pallas_call(
        _s5_local_kernel,
        grid=(S5_PAIRS, rows // tr),
        in_specs=[
            pl.BlockSpec((1, tr, 512), lambda p, i: (p, i, 0)),
            pl.BlockSpec((1, 512, 512), lambda p, i: (p, 0, 0)),
        ],
        out_specs=[pl.BlockSpec((tr, LANES), lambda p, i: (i, p))] * 4,
        out_shape=[out] * 4,
        compiler_params=_params(("parallel", "parallel")),
        name="s5_local",
    )(up, w_cat)


def _s5_scan_kernel(sr_ref, si_ref, ar_ref, ai_ref, h0r_ref, h0i_ref, hr_ref, hi_ref, fr_ref, fi_ref,
                    hr_scr, hi_scr, *, reverse, bsz):
    i = pl.program_id(1)
    n_i = pl.num_programs(1)
    steps = sr_ref.shape[0] // bsz

    @pl.when(i == 0)
    def _():
        hr_scr[...] = h0r_ref[...]
        hi_scr[...] = h0i_ref[...]

    ar = ar_ref[...]
    ai = ai_ref[...]

    def body(j, carry):
        hr, hi = carry
        c = (steps - 1 - j) if reverse else j
        rows = pl.ds(pl.multiple_of(c * bsz, bsz), bsz)
        hr_ref[rows, :] = hr
        hi_ref[rows, :] = hi
        nr = ar * hr - ai * hi + sr_ref[rows, :]
        ni = ar * hi + ai * hr + si_ref[rows, :]
        return nr, ni

    hr, hi = lax.fori_loop(0, steps, body, (hr_scr[...], hi_scr[...]))
    hr_scr[...] = hr
    hi_scr[...] = hi

    @pl.when(i == n_i - 1)
    def _():
        fr_ref[...] = hr
        fi_ref[...] = hi


def _s5_chunk_scan(s_re, s_im, a_re, a_im, h0_re, h0_im, bsz, reverse):
    rows, width = s_re.shape
    tr = min(256, rows)
    tc = 512
    n_i = rows // tr

    def rmap(j, i):
        return ((n_i - 1 - i) if reverse else i, j)

    big = jax.ShapeDtypeStruct((rows, width), F32)
    small = jax.ShapeDtypeStruct((bsz, width), F32)
    return pl.pallas_call(
        functools.partial(_s5_scan_kernel, reverse=reverse, bsz=bsz),
        grid=(width // tc, n_i),
        in_specs=[
            pl.BlockSpec((tr, tc), rmap), pl.BlockSpec((tr, tc), rmap),
            pl.BlockSpec((1, tc), lambda j, i: (0, j)), pl.BlockSpec((1, tc), lambda j, i: (0, j)),
            pl.BlockSpec((bsz, tc), lambda j, i: (0, j)), pl.BlockSpec((bsz, tc), lambda j, i: (0, j)),
        ],
        out_specs=[
            pl.BlockSpec((tr, tc), rmap), pl.BlockSpec((tr, tc), rmap),
            pl.BlockSpec((bsz, tc), lambda j, i: (0, j)), pl.BlockSpec((bsz, tc), lambda j, i: (0, j)),
        ],
        out_shape=[big, big, small, small],
        scratch_shapes=[pltpu.VMEM((bsz, tc), F32), pltpu.VMEM((bsz, tc), F32)],
        compiler_params=_params(("parallel", "arbitrary")),
        name="s5_scan_rev" if reverse else "s5_scan_fwd",
    )(s_re, s_im, a_re, a_im, h0_re, h0_im)


def _s5_out_kernel(u_ref, m_ref, v_ref, hfr_ref, hfi_ref, hbr_ref, hbi_ref, y_ref):
    v = v_ref[0]
    y = _dot(u_ref[0], m_ref[0])
    y += _dot(hfr_ref[...].astype(BF16), v[0:LANES])
    y += _dot(hfi_ref[...].astype(BF16), v[LANES:2 * LANES])
    y += _dot(hbr_ref[...].astype(BF16), v[2 * LANES:3 * LANES])
    y += _dot(hbi_ref[...].astype(BF16), v[3 * LANES:4 * LANES])
    y_ref[0] = y


def _s5_outputs(up, m_pair, v_cat, h_states):
    rows = up.shape[1]
    tr = min(1024, rows)
    return pl.pallas_call(
        _s5_out_kernel,
        grid=(S5_PAIRS, rows // tr),
        in_specs=[
            pl.BlockSpec((1, tr, 512), lambda p, i: (p, i, 0)),
            pl.BlockSpec((1, 512, 512), lambda p, i: (p, 0, 0)),
            pl.BlockSpec((1, 512, 512), lambda p, i: (p, 0, 0)),
        ] + [pl.BlockSpec((tr, LANES), lambda p, i: (i, p))] * 4,
        out_specs=pl.BlockSpec((1, tr, 512), lambda p, i: (p, i, 0)),
        out_shape=jax.ShapeDtypeStruct((S5_PAIRS, rows, 512), F32),
        compiler_params=_params(("parallel", "parallel")),
        name="s5_out",
    )(up, m_pair, v_cat, *h_states)


def _s5_branch(px, tables, seq_len, init, want_y):
    m_pair, w_cat, v_cat, a_re, a_im = tables
    bsz = px.shape[0] // seq_len
    up = _s5_to_pairs(px[:, COL_U * 512:(COL_U + 1) * 512], seq_len)
    sfr, sfi, sbr, sbi = _s5_local_states(up, w_cat)
    hfr, hfi, ffr, ffi = _s5_chunk_scan(sfr, sfi, a_re[0], a_im[0], init[0], init[1], bsz, False)
    hbr, hbi, fbr, fbi = _s5_chunk_scan(sbr, sbi, a_re[1], a_im[1], init[2], init[3], bsz, True)
    y = None
    if want_y:
        y = _s5_from_pairs(_s5_outputs(up, m_pair, v_cat, (hfr, hfi, hbr, hbi)), seq_len, bsz)
    return y, (ffr, ffi, fbr, fbi)


def _gelu_tanh(x):
    return 0.5 * x * (1.0 + jnp.tanh(math.sqrt(2.0 / math.pi) * (x + 0.044715 * (x * x * x))))


def _merge_kernel(x_ref, ym_ref, yn_ref, ys_ref, u_ref, gm_ref, gn_ref, gs_ref, d_ref, gw_ref, gb_ref,
                  wm_ref, wn_ref, ws_ref, wo_ref, gate_ref, npost_ref, o_ref):
    ys = ys_ref[...] + d_ref[...] * u_ref[...].astype(F32)
    g = _gelu_tanh(ys)
    gg = g * jax.nn.sigmoid(_dot(g.astype(BF16), gw_ref[...]) + gb_ref[...])
    y = jax.nn.sigmoid(gm_ref[...].astype(F32)) * _dot(ym_ref[...], wm_ref[...])
    y += jax.nn.sigmoid(gn_ref[...].astype(F32)) * _dot(yn_ref[...], wn_ref[...])
    y += jax.nn.sigmoid(gs_ref[...].astype(F32)) * _dot(gg.astype(BF16), ws_ref[...])
    out = _dot(y.astype(BF16), wo_ref[...])
    o_ref[...] = x_ref[...] + gate_ref[0] * _rms(out, npost_ref[...])


def _merge(x, ym, yn, ys, px, wts, gate, npost, rows_per_mod):
    t = x.shape[0]
    tm = min(512, rows_per_mod)
    per = rows_per_mod // tm
    s5_d, glu_w, glu_b, w_m, w_n, w_s, w_o = wts
    row = lambda i: (i, 0)
    const = lambda i: (0, 0)
    return pl.pallas_call(
        _merge_kernel,
        grid=(t // tm,),
        in_specs=[
            pl.BlockSpec((tm, D_MODEL), row),
            pl.BlockSpec((tm, M_WIDTH), row),
            pl.BlockSpec((tm, NA_WIDTH), row),
            pl.BlockSpec((tm, S5_WIDTH), row),
            pl.BlockSpec((tm, S5_WIDTH), lambda i: (i, COL_U)),
            pl.BlockSpec((tm, D_MODEL), lambda i: (i, COL_GATE)),
            pl.BlockSpec((tm, D_MODEL), lambda i: (i, COL_GATE + 1)),
            pl.BlockSpec((tm, D_MODEL), lambda i: (i, COL_GATE + 2)),
            pl.BlockSpec((1, S5_WIDTH), const),
            pl.BlockSpec((S5_WIDTH, S5_WIDTH), const),
            pl.BlockSpec((1, S5_WIDTH), const),
            pl.BlockSpec((M_WIDTH, D_MODEL), const),
            pl.BlockSpec((NA_WIDTH, D_MODEL), const),
            pl.BlockSpec((S5_WIDTH, D_MODEL), const),
            pl.BlockSpec((D_MODEL, D_MODEL), const),
            pl.BlockSpec((1, 1, D_MODEL), lambda i: (i // per, 0, 0)),
            pl.BlockSpec((1, D_MODEL), const),
        ],
        out_specs=pl.BlockSpec((tm, D_MODEL), row),
        out_shape=jax.ShapeDtypeStruct((t, D_MODEL), F32),
        compiler_params=_params(("parallel",)),
        name="merge",
    )(x, ym, yn, ys, px, px, px, px, s5_d, glu_w, glu_b, w_m, w_n, w_s, w_o, gate, npost)


def _route_kernel(x_ref, g_ref, sh_ref, sc_ref, rh_ref, rl_ref, comb_ref):
    h = _rms(x_ref[...], g_ref[...]) * (1.0 + sc_ref[0]) + sh_ref[0]
    hi = h.astype(BF16)
    lo = (h - hi.astype(F32)).astype(BF16)
    logits = _dot(hi, rh_ref[...]) + _dot(lo, rh_ref[...]) + _dot(hi, rl_ref[...])
    lane = lax.broadcasted_iota(jnp.int32, logits.shape, 1)
    logits = jnp.where(lane < N_EXPERTS, logits, -jnp.inf)
    m1 = jnp.max(logits, axis=-1, keepdims=True)
    i1 = jnp.min(jnp.where(logits == m1, lane, LANES), axis=-1, keepdims=True)
    rest = jnp.where(lane == i1, -jnp.inf, logits)
    m2 = jnp.max(rest, axis=-1, keepdims=True)
    i2 = jnp.min(jnp.where(rest == m2, lane, LANES), axis=-1, keepdims=True)
    e2 = jnp.exp(m2 - m1)
    p1 = 1.0 / (1.0 + e2)
    p2 = e2 / (1.0 + e2)
    comb_ref[...] = jnp.where(lane == i1, p1, 0.0) + jnp.where(lane == i2, p2, 0.0)


def _route(x, g, shift, scale, r_hi, r_lo, rows_per_mod):
    t = x.shape[0]
    tm = min(512, rows_per_mod)
    per = rows_per_mod // tm
    return pl.pallas_call(
        _route_kernel,
        grid=(t // tm,),
        in_specs=[
            pl.BlockSpec((tm, D_MODEL), lambda i: (i, 0)),
            pl.BlockSpec((1, D_MODEL), lambda i: (0, 0)),
            pl.BlockSpec((1, 1, D_MODEL), lambda i: (i // per, 0, 0)),
            pl.BlockSpec((1, 1, D_MODEL), lambda i: (i // per, 0, 0)),
            pl.BlockSpec((D_MODEL, LANES), lambda i: (0, 0)),
            pl.BlockSpec((D_MODEL, LANES), lambda i: (0, 0)),
        ],
        out_specs=pl.BlockSpec((tm, LANES), lambda i: (i, 0)),
        out_shape=jax.ShapeDtypeStruct((t, LANES), F32),
        compiler_params=_params(("parallel",)),
        name="route",
    )(x, g, shift, scale, r_hi, r_lo)


def _ffn_kernel(x_ref, g_ref, sh_ref, sc_ref, comb_ref, wg_ref, wu_ref, wd_ref, gate_ref, npost_ref,
                o_ref, hn_ref, acc_ref, *, n_experts):
    e = pl.program_id(1)
    j = pl.program_id(2)

    @pl.when((e == 0) & (j == 0))
    def _():
        h = _rms(x_ref[...], g_ref[...]) * (1.0 + sc_ref[0]) + sh_ref[0]
        hn_ref[...] = h.astype(BF16)
        acc_ref[...] = jnp.zeros_like(acc_ref)

    hn = hn_ref[...]
    a = _dot(hn, wg_ref[0])
    a = a * jax.nn.sigmoid(a) * _dot(hn, wu_ref[0])
    y = _dot(a.astype(BF16), wd_ref[0])
    if n_experts > 1:
        lane = lax.broadcasted_iota(jnp.int32, comb_ref.shape, 1)
        y = y * jnp.sum(jnp.where(lane == e, comb_ref[...], 0.0), axis=-1, keepdims=True)
    acc_ref[...] += y

    @pl.when((e == n_experts - 1) & (j == pl.num_programs(2) - 1))
    def _():
        o_ref[...] = x_ref[...] + gate_ref[0] * _rms(acc_ref[...], npost_ref[...])


def _ffn(x, g, shift, scale, comb, w_gate, w_up, w_down, gate, npost, rows_per_mod, ff_tile):
    t = x.shape[0]
    n_e, _, ff = w_gate.shape
    tm = min(512, rows_per_mod)
    per = rows_per_mod // tm
    assert ff % ff_tile == 0
    return pl.pallas_call(
        functools.partial(_ffn_kernel, n_experts=n_e),
        grid=(t // tm, n_e, ff // ff_tile),
        in_specs=[
            pl.BlockSpec((tm, D_MODEL), lambda i, e, j: (i, 0)),
            pl.BlockSpec((1, D_MODEL), lambda i, e, j: (0, 0)),
            pl.BlockSpec((1, 1, D_MODEL), lambda i, e, j: (i // per, 0, 0)),
            pl.BlockSpec((1, 1, D_MODEL), lambda i, e, j: (i // per, 0, 0)),
            pl.BlockSpec((tm, LANES), lambda i, e, j: (i, 0)),
            pl.BlockSpec((1, D_MODEL, ff_tile), lambda i, e, j: (e, 0, j)),
            pl.BlockSpec((1, D_MODEL, ff_tile), lambda i, e, j: (e, 0, j)),
            pl.BlockSpec((1, ff_tile, D_MODEL), lambda i, e, j: (e, j, 0)),
            pl.BlockSpec((1, 1, D_MODEL), lambda i, e, j: (i // per, 0, 0)),
            pl.BlockSpec((1, D_MODEL), lambda i, e, j: (0, 0)),
        ],
        out_specs=pl.BlockSpec((tm, D_MODEL), lambda i, e, j: (i, 0)),
        out_shape=jax.ShapeDtypeStruct((t, D_MODEL), F32),
        scratch_shapes=[pltpu.VMEM((tm, D_MODEL), BF16), pltpu.VMEM((tm, D_MODEL), F32)],
        compiler_params=_params(("parallel", "arbitrary", "arbitrary")),
        name="ffn" if n_e == 1 else "moe_ffn",
    )(x, g, shift, scale, comb, w_gate, w_up, w_down, gate, npost)


def _reorder_w_in(w):
    gates = w[:, 4 * M_WIDTH:4 * M_WIDTH + M_GATES]
    main = jnp.concatenate([w[:, :4 * M_WIDTH], w[:, 4 * M_WIDTH + M_GATES:]], axis=1)
    gates = jnp.pad(gates, ((0, 0), (0, LANES - M_GATES)))
    return main.astype(BF16), gates.astype(BF16)


def _row(v):
    return v.reshape(1, -1).astype(F32)


def _token_mixers(x, xc, mx, mc, lp, rope, seq_len, ctx_len, ctx_out):
    bsz = x.shape[0] // seq_len
    w_main, w_gates = _reorder_w_in(lp['w_in'])
    g_pre = _row(lp['norm_mix_pre'])
    px, gx = _inproj(x, g_pre, mx[0], mx[1], w_main, w_gates, seq_len)
    pc, gc = _inproj(xc, g_pre, mc[0], mc[1], w_main, w_gates, xc.shape[0])

    conv_w = lp['m_conv_w'].astype(F32)
    conv_b = _row(lp['m_conv_b'])
    gate_b = jnp.pad(_row(lp['m_gate_b']), ((0, 0), (0, LANES - M_GATES)))
    m_norm = _row(lp['m_norm'])
    qk_c = _qkconv(pc, conv_w, conv_b, rope[0][:ctx_len], rope[1][:ctx_len], ctx_len, False)
    qk_x = _qkconv(px, conv_w, conv_b, rope[0], rope[1], seq_len, True)
    zero = _mlstm_zero_state(bsz)
    ym_c, fin_f, fin_b = _mlstm_bidir(qk_c, pc, gc, gate_b, m_norm, zero, zero, ctx_len)
    ym_x, _, _ = _mlstm_bidir(qk_x, px, gx, gate_b, m_norm, fin_f, fin_b, seq_len)

    yn_x = _na_attention(px, pc, _na_bias_table(lp['na_rpb']), seq_len, ctx_len)

    tables = _s5_tables(lp['s5_lam_re'], lp['s5_lam_im'], lp['s5_log_dt'], lp['s5_b_re'], lp['s5_b_im'],
                        lp['s5_c_re'], lp['s5_c_im'])
    zs = jnp.zeros((bsz, S5_GROUPS * S5_STATE), F32)
    ys_c, fin_s = _s5_branch(pc, tables, ctx_len, (zs, zs, zs, zs), ctx_out)
    ys_x, _ = _s5_branch(px, tables, seq_len, fin_s, True)

    wts = (_row(lp['s5_d']), lp['s5_glu_w'].astype(BF16), _row(lp['s5_glu_b']),
           lp['w_branch_m'].astype(BF16), lp['w_branch_na'].astype(BF16), lp['w_branch_s5'].astype(BF16),
           lp['w_out'].astype(BF16))
    g_post = _row(lp['norm_mix_post'])
    x = _merge(x, ym_x, yn_x, ys_x, px, wts, mx[2], g_post, seq_len)
    if ctx_out:
        yn_c = _ctx_attention(pc, ctx_len)
        xc = _merge(xc, ym_c, yn_c, ys_c, pc, wts, mc[2], g_post, xc.shape[0])
    return x, xc


def kernel(x, c, ctx, c_ctx, ada_w, ada_b, norm_mix_pre, norm_mix_post, norm_ffn_pre, norm_ffn_post, w_in, m_gate_b, m_conv_w, m_conv_b, m_norm, na_rpb, s5_lam_re, s5_lam_im, s5_log_dt, s5_b_re, s5_b_im, s5_c_re, s5_c_im, s5_d, s5_glu_w, s5_glu_b, w_branch_m, w_branch_na, w_branch_s5, w_out, ffn_w_gate, ffn_w_up, ffn_w_down, moe_router, moe_w_gate, moe_w_up, moe_w_down):
    bsz, seq_len, d = x.shape
    ctx_len = ctx.shape[1]
    depth = w_in.shape[0]
    assert d == D_MODEL and seq_len % (GRID_W * NA_ROWS) == 0 and ctx_len % M_CHUNK == 0
    rope = _rope_tables(seq_len)
    xf = x.reshape(bsz * seq_len, d)
    xc = ctx.reshape(bsz * ctx_len, d)
    cvecs = jnp.zeros((16, d), F32).at[:bsz].set(c).at[bsz].set(c_ctx)
    for l in range(depth):
        last = l == depth - 1
        mod = _adaln(cvecs, ada_w[l].astype(BF16), _row(ada_b[l]))
        mx = [mod[:bsz, k * d:(k + 1) * d].reshape(bsz, 1, d) for k in range(N_MOD)]
        mc = [mod[bsz:bsz + 1, k * d:(k + 1) * d].reshape(1, 1, d) for k in range(N_MOD)]
        lp = {
            'norm_mix_pre': norm_mix_pre[l], 'norm_mix_post': norm_mix_post[l],
            'w_in': w_in[l], 'm_gate_b': m_gate_b[l], 'm_conv_w': m_conv_w[l], 'm_conv_b': m_conv_b[l],
            'm_norm': m_norm[l], 'na_rpb': na_rpb[l], 's5_lam_re': s5_lam_re[l], 's5_lam_im': s5_lam_im[l],
            's5_log_dt': s5_log_dt[l], 's5_b_re': s5_b_re[l], 's5_b_im': s5_b_im[l], 's5_c_re': s5_c_re[l],
            's5_c_im': s5_c_im[l], 's5_d': s5_d[l], 's5_glu_w': s5_glu_w[l], 's5_glu_b': s5_glu_b[l],
            'w_branch_m': w_branch_m[l], 'w_branch_na': w_branch_na[l], 'w_branch_s5': w_branch_s5[l],
            'w_out': w_out[l],
        }
        xf, xc = _token_mixers(xf, xc, mx, mc, lp, rope, seq_len, ctx_len, not last)
        j = l // 2
        g_pre, g_post = _row(norm_ffn_pre[l]), _row(norm_ffn_post[l])
        if l % 2 == 0:
            wg, wu, wd = (ffn_w_gate[j][None].astype(BF16), ffn_w_up[j][None].astype(BF16),
                          ffn_w_down[j][None].astype(BF16))
            ff_tile = wg.shape[2] // 2

            def channel(h, m, rows):
                comb = jnp.zeros((h.shape[0], LANES), F32)
                return _ffn(h, g_pre, m[3], m[4], comb, wg, wu, wd, m[5], g_post, rows, ff_tile)
        else:
            wg, wu, wd = moe_w_gate[j].astype(BF16), moe_w_up[j].astype(BF16), moe_w_down[j].astype(BF16)
            router = jnp.pad(moe_router[j].astype(F32), ((0, 0), (0, LANES - N_EXPERTS)))
            r_hi = router.astype(BF16)
            r_lo = (router - r_hi.astype(F32)).astype(BF16)
            ff_tile = wg.shape[2] // 2

            def channel(h, m, rows):
                comb = _route(h, g_pre, m[3], m[4], r_hi, r_lo, rows)
                return _ffn(h, g_pre, m[3], m[4], comb, wg, wu, wd, m[5], g_post, rows, ff_tile)
        xf = channel(xf, mx, seq_len)
        if not last:
            xc = channel(xc, mc, xc.shape[0])
    return xf.reshape(bsz, seq_len, d)
```

```python
import functools
import math

import jax
import jax.numpy as jnp
from jax import lax
from jax.experimental import pallas as pl
from jax.experimental.pallas import tpu as pltpu

F32 = jnp.float32
BF16 = jnp.bfloat16

D_MODEL = 1024
EPS = 1e-6
N_MOD = 6
GRID_W = 64

M_HEADS = 4
M_HEAD_DIM = 128
M_WIDTH = M_HEADS * M_HEAD_DIM
M_GATES = 4 * M_HEADS
M_CHUNK = 128
ROPE_BASE = 10000.0

NA_HEADS = 8
NA_HEAD_DIM = 64
NA_WIDTH = NA_HEADS * NA_HEAD_DIM
WIN_R = 8
WIN_C = 16
NEG_BIG = -1e30

S5_GROUP = 16
S5_WIDTH = 512
S5_GROUPS = S5_WIDTH // S5_GROUP
S5_STATE = 64
S5_MAX_RE = -1e-4

LANES = 128
VMEM_LIMIT = 52 * 1024 * 1024

S5_CHUNK = 16
S5_QUAD_GROUPS = LANES // S5_GROUP
S5_QUADS = S5_GROUPS // S5_QUAD_GROUPS
S5_QCOLS = S5_CHUNK * LANES
S5_QSTATE = S5_QUAD_GROUPS * S5_STATE

N_EXPERTS = 8
TOP_K = 2
MOE_TILE = 1024
MOE_CAP = 288
MOE_PASSES = -(-MOE_TILE // MOE_CAP)

COL_QK, COL_V, COL_O, COL_NAQ, COL_NAK, COL_NAV = 0, 2, 3, 10, 11, 12
COL_GATE = 2
IN_MAIN = 6656


def _params(sem):
    return pltpu.CompilerParams(dimension_semantics=sem, vmem_limit_bytes=VMEM_LIMIT)


def _dot(a, b):
    return jnp.dot(a, b, preferred_element_type=F32)


def _dot_nt(a, b):
    return lax.dot_general(a, b, (((1,), (1,)), ((), ())), preferred_element_type=F32)


def _split3(x):
    hi = x.astype(BF16)
    r1 = x - hi.astype(F32)
    mid = r1.astype(BF16)
    lo = (r1 - mid.astype(F32)).astype(BF16)
    return hi, mid, lo


def _rms(x, g):
    return x * lax.rsqrt(jnp.mean(x * x, axis=-1, keepdims=True) + EPS) * g


def _inproj_kernel(x_ref, g_ref, sh_ref, sc_ref, w_ref, wg_ref, wu_ref, o_ref, og_ref, ou_ref, hn_ref):
    @pl.when(pl.program_id(1) == 0)
    def _():
        h = _rms(x_ref[...], g_ref[...]) * (1.0 + sc_ref[0]) + sh_ref[0]
        hb = h.astype(BF16)
        hn_ref[...] = hb
        og_ref[...] = _dot(hb, wg_ref[...])
        ou_ref[...] = _dot(hb, wu_ref[...])

    o_ref[...] = _dot(hn_ref[...], w_ref[...]).astype(o_ref.dtype)


def _inproj(x, g, shift, scale, w_main, w_gates, w_u, rows_per_mod):
    t = x.shape[0]
    tm = min(1024, rows_per_mod)
    tn = 512
    assert rows_per_mod % tm == 0 and t % tm == 0
    per = rows_per_mod // tm
    return pl.pallas_call(
        _inproj_kernel,
        grid=(t // tm, IN_MAIN // tn),
        in_specs=[
            pl.BlockSpec((tm, D_MODEL), lambda i, j: (i, 0)),
            pl.BlockSpec((1, D_MODEL), lambda i, j: (0, 0)),
            pl.BlockSpec((1, 1, D_MODEL), lambda i, j: (i // per, 0, 0)),
            pl.BlockSpec((1, 1, D_MODEL), lambda i, j: (i // per, 0, 0)),
            pl.BlockSpec((D_MODEL, tn), lambda i, j: (0, j)),
            pl.BlockSpec((D_MODEL, LANES), lambda i, j: (0, 0)),
            pl.BlockSpec((D_MODEL, S5_WIDTH), lambda i, j: (0, 0)),
        ],
        out_specs=[
            pl.BlockSpec((tm, tn), lambda i, j: (i, j)),
            pl.BlockSpec((tm, LANES), lambda i, j: (i, 0)),
            pl.BlockSpec((tm, S5_WIDTH), lambda i, j: (i, 0)),
        ],
        out_shape=[jax.ShapeDtypeStruct((t, IN_MAIN), BF16), jax.ShapeDtypeStruct((t, LANES), F32),
                   jax.ShapeDtypeStruct((t, S5_WIDTH), F32)],
        scratch_shapes=[pltpu.VMEM((tm, D_MODEL), BF16)],
        compiler_params=_params(("parallel", "arbitrary")),
        name="inproj",
    )(x, g, shift, scale, w_main, w_gates, w_u)


def _adaln_kernel(c_ref, w_ref, b_ref, o_ref):
    c = c_ref[...]
    s = c * jax.nn.sigmoid(c)
    o_ref[...] = _dot(s.astype(BF16), w_ref[...]) + b_ref[...]


def _adaln(cvecs, w, b):
    n = w.shape[1]
    tn = 1024
    return pl.pallas_call(
        _adaln_kernel,
        grid=(n // tn,),
        in_specs=[
            pl.BlockSpec(cvecs.shape, lambda j: (0, 0)),
            pl.BlockSpec((D_MODEL, tn), lambda j: (0, j)),
            pl.BlockSpec((1, tn), lambda j: (0, j)),
        ],
        out_specs=pl.BlockSpec((cvecs.shape[0], tn), lambda j: (0, j)),
        out_shape=jax.ShapeDtypeStruct((cvecs.shape[0], n), F32),
        compiler_params=_params(("parallel",)),
        name="adaln",
    )(cvecs, w, b)


HALO = 16


def _swap32(x):
    lane = lax.broadcasted_iota(jnp.int32, x.shape, 1)
    fwd = pltpu.roll(x, 96, 1)
    bwd = pltpu.roll(x, 32, 1)
    return jnp.where((lane % 64) < 32, fwd, bwd)


def _qkconv_kernel(x_ref, pv_ref, nx_ref, w_ref, b_ref, cos_ref, sin_ref, o_ref, *, tiles_per_seq, rope):
    i = pl.program_id(0)
    tm = x_ref.shape[0]
    x = x_ref[...].astype(F32)
    first = (i % tiles_per_seq) == 0
    last = (i % tiles_per_seq) == tiles_per_seq - 1
    prev_row = jnp.where(first, 0.0, pv_ref[HALO - 1:HALO, :].astype(F32))
    next_row = jnp.where(last, 0.0, nx_ref[0:1, :].astype(F32))
    row = lax.broadcasted_iota(jnp.int32, x.shape, 0)
    xp = jnp.where(row == 0, prev_row, pltpu.roll(x, 1, 0))
    xn = jnp.where(row == tm - 1, next_row, pltpu.roll(x, tm - 1, 0))
    y = xp * w_ref[0:1, :] + x * w_ref[1:2, :] + xn * w_ref[2:3, :] + b_ref[...]
    y = y * jax.nn.sigmoid(y)
    kscale = M_HEAD_DIM ** -0.5
    for hh in range(2 * M_HEADS):
        ys = y[:, hh * LANES:(hh + 1) * LANES]
        if rope:
            ys = ys * cos_ref[...] + _swap32(ys) * sin_ref[...]
        if hh >= M_HEADS:
            ys = ys * kscale
        o_ref[:, hh * LANES:(hh + 1) * LANES] = ys.astype(o_ref.dtype)


def _qkconv(px, conv_w, conv_b, cos_t, sin_t, seq_len, rope):
    t = px.shape[0]
    tm = min(512, seq_len)
    assert seq_len % tm == 0
    tiles = seq_len // tm
    hb = tm // HALO
    nh = t // HALO
    return pl.pallas_call(
        functools.partial(_qkconv_kernel, tiles_per_seq=tiles, rope=rope),
        grid=(t // tm,),
        in_specs=[
            pl.BlockSpec((tm, 2 * M_WIDTH), lambda i: (i, 0)),
            pl.BlockSpec((HALO, 2 * M_WIDTH), lambda i: (jnp.maximum(i * hb - 1, 0), 0)),
            pl.BlockSpec((HALO, 2 * M_WIDTH), lambda i: (jnp.minimum((i + 1) * hb, nh - 1), 0)),
            pl.BlockSpec((3, 2 * M_WIDTH), lambda i: (0, 0)),
            pl.BlockSpec((1, 2 * M_WIDTH), lambda i: (0, 0)),
            pl.BlockSpec((tm, LANES), lambda i: (i % tiles, 0)),
            pl.BlockSpec((tm, LANES), lambda i: (i % tiles, 0)),
        ],
        out_specs=pl.BlockSpec((tm, 2 * M_WIDTH), lambda i: (i, 0)),
        out_shape=jax.ShapeDtypeStruct((t, 2 * M_WIDTH), BF16),
        compiler_params=_params(("parallel",)),
        name="qkconv",
    )(px, px, px, conv_w, conv_b, cos_t, sin_t)


def _rope_tables(n):
    pos = jnp.arange(n, dtype=jnp.int32)
    row = (pos // GRID_W).astype(F32)
    col = (pos % GRID_W).astype(F32)
    n_freq = M_HEAD_DIM // 4
    inv = ROPE_BASE ** (-jnp.arange(n_freq, dtype=F32) / n_freq)
    ar, ac = row[:, None] * inv, col[:, None] * inv
    cos_t = jnp.concatenate([jnp.cos(ar), jnp.cos(ar), jnp.cos(ac), jnp.cos(ac)], axis=-1)
    sin_t = jnp.concatenate([-jnp.sin(ar), jnp.sin(ar), -jnp.sin(ac), jnp.sin(ac)], axis=-1)
    return cos_t, sin_t


def _mlstm_kernel(*refs, reverse, final):
    if final:
        (qk_ref, v_ref, g_ref, gb_ref, c0_ref, n0_ref, m0_ref, hp_ref, op_ref, nw_ref,
         out_ref, cf_ref, nf_ref, mf_ref, c_scr, n_scr, m_scr) = refs
    else:
        (qk_ref, v_ref, g_ref, gb_ref, c0_ref, n0_ref, m0_ref,
         out_ref, cf_ref, nf_ref, mf_ref, c_scr, n_scr, m_scr) = refs
    c_idx = pl.program_id(1)
    n_c = pl.num_programs(1)
    L = M_CHUNK

    @pl.when(c_idx == 0)
    def _():
        c_scr[...] = c0_ref[0]
        n_scr[...] = n0_ref[0]
        m_scr[...] = m0_ref[0]

    g = g_ref[...] + gb_ref[...]
    logf = jax.nn.log_sigmoid(g)
    ti = lax.broadcasted_iota(jnp.int32, (L, L), 0)
    si = lax.broadcasted_iota(jnp.int32, (L, L), 1)
    keep = (si >= ti) if reverse else (si <= ti)
    tri = jnp.where(keep, 1.0, 0.0).astype(BF16)
    f_hi, f_mid, f_lo = _split3(logf)
    bsum = _dot(tri, f_hi) + _dot(tri, f_mid) + _dot(tri, f_lo)
    g_t = g.T
    b_t = bsum.T
    i_off = 2 * M_HEADS if reverse else 0
    f_off = i_off + M_HEADS
    end = 0 if reverse else L - 1

    for h in range(M_HEADS):
        q = qk_ref[:, h * LANES:(h + 1) * LANES]
        k = qk_ref[:, M_WIDTH + h * LANES:M_WIDTH + (h + 1) * LANES]
        v = v_ref[:, h * LANES:(h + 1) * LANES]
        c_mat = c_scr[h]
        n_vec = n_scr[h]
        m_prev = m_scr[h][:, 0:1]
        b_col = bsum[:, f_off + h:f_off + h + 1]
        i_col = g[:, i_off + h:i_off + h + 1]
        b_row = b_t[f_off + h:f_off + h + 1, :]
        i_row = g_t[i_off + h:i_off + h + 1, :]
        b_end = b_row[:, end:end + 1]

        log_w = jnp.where(keep, b_col - b_row + i_row, -jnp.inf)
        carry_log = b_col + m_prev
        m_t = jnp.maximum(carry_log, jnp.max(log_w, axis=-1, keepdims=True))
        w = jnp.exp(log_w - m_t)
        s = _dot_nt(q, k) * w
        c_scale = jnp.exp(carry_log - m_t)
        qf = q.astype(F32)
        num = _dot(s.astype(BF16), v) + c_scale * _dot_nt(q, c_mat.astype(BF16))
        den = jnp.sum(s, axis=-1, keepdims=True) + c_scale * jnp.sum(qf * n_vec, axis=-1, keepdims=True)
        hh = num / jnp.maximum(jnp.abs(den), jnp.exp(-m_t))

        m_new = jnp.maximum(b_end + m_prev, jnp.max(b_end - b_row + i_row, axis=-1, keepdims=True))
        u_col = jnp.exp(b_end - b_col + i_col - m_new)
        decay = jnp.exp(b_end + m_prev - m_new)
        vu_t = (v.astype(F32) * u_col).T.astype(BF16)
        c_scr[h] = decay * c_mat + _dot(vu_t, k)
        n_scr[h] = decay * n_vec + jnp.sum(k.astype(F32) * u_col, axis=0, keepdims=True)
        m_scr[h] = jnp.broadcast_to(m_new, (1, LANES))

        if final:
            hs = hh + hp_ref[:, h * LANES:(h + 1) * LANES]
            mu = jnp.mean(hs, axis=-1, keepdims=True)
            var = jnp.mean(jnp.square(hs - mu), axis=-1, keepdims=True)
            hn = (hs - mu) * lax.rsqrt(var + EPS)
            o_pre = op_ref[:, h * LANES:(h + 1) * LANES].astype(F32)
            y = hn * nw_ref[:, h * LANES:(h + 1) * LANES] * jax.nn.sigmoid(o_pre)
            out_ref[:, h * LANES:(h + 1) * LANES] = y.astype(out_ref.dtype)
        else:
            out_ref[:, h * LANES:(h + 1) * LANES] = hh.astype(out_ref.dtype)

    @pl.when(c_idx == n_c - 1)
    def _():
        cf_ref[0] = c_scr[...]
        nf_ref[0] = n_scr[...]
        mf_ref[0] = m_scr[...]


def _mlstm_scan(qk, px, gates, gate_b, state, seq_len, reverse, hprev=None, norm_w=None):
    t = qk.shape[0]
    bsz = t // seq_len
    n_c = seq_len // M_CHUNK
    final = hprev is not None

    def row(b, c):
        return b * n_c + ((n_c - 1 - c) if reverse else c)

    in_specs = [
        pl.BlockSpec((M_CHUNK, 2 * M_WIDTH), lambda b, c: (row(b, c), 0)),
        pl.BlockSpec((M_CHUNK, M_WIDTH), lambda b, c: (row(b, c), COL_V)),
        pl.BlockSpec((M_CHUNK, LANES), lambda b, c: (row(b, c), 0)),
        pl.BlockSpec((1, LANES), lambda b, c: (0, 0)),
        pl.BlockSpec((1, M_HEADS, M_HEAD_DIM, M_HEAD_DIM), lambda b, c: (b, 0, 0, 0)),
        pl.BlockSpec((1, M_HEADS, 1, LANES), lambda b, c: (b, 0, 0, 0)),
        pl.BlockSpec((1, M_HEADS, 1, LANES), lambda b, c: (b, 0, 0, 0)),
    ]
    args = [qk, px, gates, gate_b, *state]
    if final:
        in_specs += [
            pl.BlockSpec((M_CHUNK, M_WIDTH), lambda b, c: (row(b, c), 0)),
            pl.BlockSpec((M_CHUNK, M_WIDTH), lambda b, c: (row(b, c), COL_O)),
            pl.BlockSpec((1, M_WIDTH), lambda b, c: (0, 0)),
        ]
        args += [hprev, px, norm_w]
    out_dtype = BF16 if final else F32
    return pl.pallas_call(
        functools.partial(_mlstm_kernel, reverse=reverse, final=final),
        grid=(bsz, n_c),
        in_specs=in_specs,
        out_specs=[
            pl.BlockSpec((M_CHUNK, M_WIDTH), lambda b, c: (row(b, c), 0)),
            pl.BlockSpec((1, M_HEADS, M_HEAD_DIM, M_HEAD_DIM), lambda b, c: (b, 0, 0, 0)),
            pl.BlockSpec((1, M_HEADS, 1, LANES), lambda b, c: (b, 0, 0, 0)),
            pl.BlockSpec((1, M_HEADS, 1, LANES), lambda b, c: (b, 0, 0, 0)),
        ],
        out_shape=[
            jax.ShapeDtypeStruct((t, M_WIDTH), out_dtype),
            jax.ShapeDtypeStruct((bsz, M_HEADS, M_HEAD_DIM, M_HEAD_DIM), F32),
            jax.ShapeDtypeStruct((bsz, M_HEADS, 1, LANES), F32),
            jax.ShapeDtypeStruct((bsz, M_HEADS, 1, LANES), F32),
        ],
        scratch_shapes=[
            pltpu.VMEM((M_HEADS, M_HEAD_DIM, M_HEAD_DIM), F32),
            pltpu.VMEM((M_HEADS, 1, LANES), F32),
            pltpu.VMEM((M_HEADS, 1, LANES), F32),
        ],
        compiler_params=_params(("parallel", "arbitrary")),
        name="mlstm_rev" if reverse else "mlstm_fwd",
    )(*args)


def _mlstm_zero_state(bsz):
    return (jnp.zeros((bsz, M_HEADS, M_HEAD_DIM, M_HEAD_DIM), F32),
            jnp.zeros((bsz, M_HEADS, 1, LANES), F32),
            jnp.zeros((bsz, M_HEADS, 1, LANES), F32))


def _mlstm_bidir(qk, px, gates, gate_b, norm_w, state_f, state_b, seq_len):
    h_f, *fin_f = _mlstm_scan(qk, px, gates, gate_b, state_f, seq_len, False)
    y, *fin_b = _mlstm_scan(qk, px, gates, gate_b, state_b, seq_len, True, hprev=h_f, norm_w=norm_w)
    return y, tuple(fin_f), tuple(fin_b)


NA_ROWS = 8


def _na_kernel(q_ref, kp_ref, kc_ref, kn_ref, vp_ref, vc_ref, vn_ref, kx_ref, vx_ref, bias_ref,
               o_ref, kwin, vwin, *, rows):
    r = pl.program_id(1)
    blk = NA_ROWS * GRID_W

    @pl.when(r % NA_ROWS == 0)
    def _():
        kwin[0:blk, :] = kp_ref[...]
        kwin[blk:2 * blk, :] = kc_ref[...]
        kwin[2 * blk:3 * blk, :] = kn_ref[...]
        vwin[0:blk, :] = vp_ref[...]
        vwin[blk:2 * blk, :] = vc_ref[...]
        vwin[2 * blk:3 * blk, :] = vn_ref[...]

    g = r // NA_ROWS
    r0 = jnp.clip(r - WIN_R // 2, 0, rows - WIN_R)
    off = pl.multiple_of((r0 - NA_ROWS * g + NA_ROWS) * GRID_W, GRID_W)
    n_loc = WIN_R * GRID_W
    lane = lax.broadcasted_iota(jnp.int32, (GRID_W, LANES), 1)
    scale = NA_HEAD_DIM ** -0.5
    for hp in range(NA_HEADS // 2):
        cols = slice(hp * LANES, (hp + 1) * LANES)
        q2 = q_ref[:, cols] * scale
        k2 = kwin[pl.ds(off, n_loc), cols]
        v2 = vwin[pl.ds(off, n_loc), cols]
        kx2 = kx_ref[:, cols]
        vx2 = vx_ref[:, cols]
        outs = []
        for sub in range(2):
            h = 2 * hp + sub
            sel = (lane < NA_HEAD_DIM) if sub == 0 else (lane >= NA_HEAD_DIM)
            qm = jnp.where(sel, q2, jnp.zeros_like(q2))
            s_loc = _dot_nt(qm, k2) + bias_ref[0, h]
            s_ctx = _dot_nt(qm, kx2)
            m = jnp.maximum(jnp.max(s_loc, axis=-1, keepdims=True), jnp.max(s_ctx, axis=-1, keepdims=True))
            p_loc = jnp.exp(s_loc - m)
            p_ctx = jnp.exp(s_ctx - m)
            l = jnp.sum(p_loc, axis=-1, keepdims=True) + jnp.sum(p_ctx, axis=-1, keepdims=True)
            o2 = _dot(p_loc.astype(BF16), v2) + _dot(p_ctx.astype(BF16), vx2)
            outs.append(o2 / l)
        o_ref[:, cols] = jnp.where(lane < NA_HEAD_DIM, outs[0], outs[1]).astype(o_ref.dtype)


def _na_bias_table(rpb):
    qc = jnp.arange(GRID_W)[:, None]
    kc = jnp.arange(GRID_W)[None, :]
    cs = jnp.clip(qc - WIN_C // 2, 0, GRID_W - WIN_C)
    ok = (kc >= cs) & (kc < cs + WIN_C)
    dc = jnp.clip(kc - qc + WIN_C - 1, 0, 2 * WIN_C - 2)
    dr = jnp.arange(WIN_R)[:, None] + jnp.arange(WIN_R)[None, :]
    tab = rpb.astype(F32)[:, dr][:, :, :, dc]
    tab = jnp.where(ok[None, None, None], tab, NEG_BIG)
    tab = jnp.transpose(tab, (1, 0, 3, 2, 4))
    return tab.reshape(WIN_R, NA_HEADS, GRID_W, WIN_R * GRID_W)


def _na_attention(px, pc, bias, seq_len, ctx_len):
    t = px.shape[0]
    bsz = t // seq_len
    rows = seq_len // GRID_W
    n_g = rows // NA_ROWS
    blk = NA_ROWS * GRID_W

    def kv_spec(col, shift):
        def imap(b, r):
            g = jnp.clip(r // NA_ROWS + shift, 0, n_g - 1)
            return (b * n_g + g, col)
        return pl.BlockSpec((blk, NA_WIDTH), imap)

    def bias_map(b, r):
        r0 = jnp.clip(r - WIN_R // 2, 0, rows - WIN_R)
        return (r0 - r + WIN_R - 1, 0, 0, 0)

    return pl.pallas_call(
        functools.partial(_na_kernel, rows=rows),
        grid=(bsz, rows),
        in_specs=[
            pl.BlockSpec((GRID_W, NA_WIDTH), lambda b, r: (b * rows + r, COL_NAQ)),
            kv_spec(COL_NAK, -1), kv_spec(COL_NAK, 0), kv_spec(COL_NAK, 1),
            kv_spec(COL_NAV, -1), kv_spec(COL_NAV, 0), kv_spec(COL_NAV, 1),
            pl.BlockSpec((ctx_len, NA_WIDTH), lambda b, r: (b, COL_NAK)),
            pl.BlockSpec((ctx_len, NA_WIDTH), lambda b, r: (b, COL_NAV)),
            pl.BlockSpec((1, NA_HEADS, GRID_W, WIN_R * GRID_W), bias_map),
        ],
        out_specs=pl.BlockSpec((GRID_W, NA_WIDTH), lambda b, r: (b * rows + r, 0)),
        out_shape=jax.ShapeDtypeStruct((t, NA_WIDTH), BF16),
        scratch_shapes=[pltpu.VMEM((3 * blk, NA_WIDTH), BF16), pltpu.VMEM((3 * blk, NA_WIDTH), BF16)],
        compiler_params=_params(("parallel", "arbitrary")),
        name="na_attn",
    )(px, px, px, px, px, px, px, pc, pc, bias)


def _ctx_attn_kernel(q_ref, k_ref, v_ref, o_ref):
    n = q_ref.shape[0]
    lane = lax.broadcasted_iota(jnp.int32, (n, LANES), 1)
    scale = NA_HEAD_DIM ** -0.5
    for hp in range(NA_HEADS // 2):
        cols = slice(hp * LANES, (hp + 1) * LANES)
        q2 = q_ref[:, cols] * scale
        k2 = k_ref[:, cols]
        v2 = v_ref[:, cols]
        outs = []
        for sub in range(2):
            sel = (lane < NA_HEAD_DIM) if sub == 0 else (lane >= NA_HEAD_DIM)
            s = _dot_nt(jnp.where(sel, q2, jnp.zeros_like(q2)), k2)
            p = jnp.exp(s - jnp.max(s, axis=-1, keepdims=True))
            outs.append(_dot(p.astype(BF16), v2) / jnp.sum(p, axis=-1, keepdims=True))
        o_ref[:, cols] = jnp.where(lane < NA_HEAD_DIM, outs[0], outs[1]).astype(o_ref.dtype)


def _ctx_attention(pc, ctx_len):
    t = pc.shape[0]
    return pl.pallas_call(
        _ctx_attn_kernel,
        grid=(t // ctx_len,),
        in_specs=[pl.BlockSpec((ctx_len, NA_WIDTH), lambda b, col=col: (b, col))
                  for col in (COL_NAQ, COL_NAK, COL_NAV)],
        out_specs=pl.BlockSpec((ctx_len, NA_WIDTH), lambda b: (b, 0)),
        out_shape=jax.ShapeDtypeStruct((t, NA_WIDTH), BF16),
        compiler_params=_params(("parallel",)),
        name="ctx_attn",
    )(pc, pc, pc)


def _cmul(a, b):
    return a[0] * b[0] - a[1] * b[1], a[0] * b[1] + a[1] * b[0]


def _s5_tables(lam_re, lam_im, log_dt, b_re, b_im, c_re, c_im):
    L, G, S, H = S5_CHUNK, S5_GROUPS, S5_STATE, S5_GROUP
    lr = jnp.minimum(lam_re.astype(F32), S5_MAX_RE)
    li = lam_im.astype(F32)
    dt = jnp.exp(log_dt.astype(F32))[..., None]
    zr, zi = lr * dt, li * dt
    d = jnp.arange(L + 1, dtype=F32)[None, :, None, None]
    mag = jnp.exp(zr[:, None] * d)
    pw = (mag * jnp.cos(zi[:, None] * d), mag * jnp.sin(zi[:, None] * d))
    lb = (pw[0][:, 1], pw[1][:, 1])
    den = lr * lr + li * li
    ratio = (((lb[0] - 1.0) * lr + lb[1] * li) / den, (lb[1] * lr - (lb[0] - 1.0) * li) / den)
    bb = _cmul((ratio[0][..., None], ratio[1][..., None]), (b_re.astype(F32), b_im.astype(F32)))
    cc = (c_re.astype(F32), c_im.astype(F32))

    def kern(x):
        cl = _cmul((cc[0][x][None], cc[1][x][None]), (pw[0][x, :L, :, None, :], pw[1][x, :L, :, None, :]))
        return jnp.einsum('dghp,gpj->dghj', cl[0], bb[0][x]) - jnp.einsum('dghp,gpj->dghj', cl[1], bb[1][x])

    tt = jnp.arange(L)[None, :]
    ss = jnp.arange(L)[:, None]
    m_f = jnp.where((tt >= ss)[:, :, None, None, None], kern(0)[jnp.clip(tt - ss, 0, L - 1)], 0.0)
    m_b = jnp.where((ss >= tt)[:, :, None, None, None], kern(1)[jnp.clip(ss - tt, 0, L - 1)], 0.0)
    m = jnp.transpose(m_f + m_b, (2, 0, 4, 1, 3))

    def state_in(x, steps):
        p = (pw[0][x][steps][:, :, None, :], pw[1][x][steps][:, :, None, :])
        r = _cmul(p, (jnp.swapaxes(bb[0][x], 1, 2)[None], jnp.swapaxes(bb[1][x], 1, 2)[None]))
        return jnp.transpose(r[0], (1, 0, 2, 3)), jnp.transpose(r[1], (1, 0, 2, 3))

    def state_out(x, steps):
        p = (pw[0][x][steps][:, :, None, :], pw[1][x][steps][:, :, None, :])
        r = _cmul(p, (cc[0][x][None], cc[1][x][None]))
        return jnp.transpose(r[0], (1, 3, 0, 2)), jnp.transpose(r[1], (1, 3, 0, 2))

    steps = jnp.arange(L)
    wf = state_in(0, L - 1 - steps)
    wb = state_in(1, steps)
    vf = state_out(0, steps + 1)
    vb = state_out(1, L - steps)
    eye = jnp.eye(S5_QUAD_GROUPS, dtype=F32)
    Q, A = S5_QUADS, S5_QUAD_GROUPS

    def quad(a):
        return a.reshape((Q, A) + a.shape[1:])

    m_q = jnp.einsum('qasjth,ab->qsajtbh', quad(m), eye).reshape(Q, S5_QCOLS, S5_QCOLS)
    w_q = jnp.stack([jnp.einsum('qasjp,ab->qsajbp', quad(w), eye) for w in (wf[0], wf[1], wb[0], wb[1])],
                    axis=4).reshape(Q, S5_QCOLS, 4 * S5_QSTATE)
    v_q = jnp.stack([jnp.einsum('qbpth,ab->qbptah', quad(v), eye) for v in (vf[0], -vf[1], vb[0], -vb[1])],
                    axis=1).reshape(Q, 4 * S5_QSTATE, S5_QCOLS)
    a_re = pw[0][:, L].reshape(2, 1, G * S)
    a_im = pw[1][:, L].reshape(2, 1, G * S)
    return w_q.astype(BF16), m_q.astype(BF16), v_q.astype(BF16), a_re, a_im


def _s5_operand(u_ref):
    return jnp.concatenate([u_ref[:, s, :] for s in range(S5_CHUNK)], axis=1).astype(BF16)


def _s5_local_kernel(u_ref, w_ref, sfr_ref, sfi_ref, sbr_ref, sbi_ref):
    s = _dot(_s5_operand(u_ref), w_ref[0])
    n = S5_QSTATE
    sfr_ref[...] = s[:, 0:n]
    sfi_ref[...] = s[:, n:2 * n]
    sbr_ref[...] = s[:, 2 * n:3 * n]
    sbi_ref[...] = s[:, 3 * n:4 * n]


def _s5_local_states(u3, w_q):
    rows = u3.shape[0]
    tr = min(256, rows)
    assert rows % tr == 0
    out = jax.ShapeDtypeStruct((rows, S5_GROUPS * S5_STATE), F32)
    return pl.pallas_call(
        _s5_local_kernel,
        grid=(S5_QUADS, rows // tr),
        in_specs=[
            pl.BlockSpec((tr, S5_CHUNK, LANES), lambda q, i: (i, 0, q)),
            pl.BlockSpec((1, S5_QCOLS, 4 * S5_QSTATE), lambda q, i: (q, 0, 0)),
        ],
        out_specs=[pl.BlockSpec((tr, S5_QSTATE), lambda q, i: (i, q))] * 4,
        out_shape=[out] * 4,
        compiler_params=_params(("parallel", "parallel")),
        name="s5_local",
    )(u3, w_q)


def _s5_scan_kernel(sr_ref, si_ref, ar_ref, ai_ref, h0r_ref, h0i_ref, hr_ref, hi_ref, fr_ref, fi_ref,
                    hr_scr, hi_scr, *, reverse):
    i = pl.program_id(1)
    n_i = pl.num_programs(1)
    steps = sr_ref.shape[1]

    @pl.when(i == 0)
    def _():
        hr_scr[...] = h0r_ref[...]
        hi_scr[...] = h0i_ref[...]

    ar = ar_ref[...]
    ai = ai_ref[...]

    def body(j, carry):
        hr, hi = carry
        c = (steps - 1 - j) if reverse else j
        hr_ref[:, c, :] = hr
        hi_ref[:, c, :] = hi
        nr = ar * hr - ai * hi + sr_ref[:, c, :]
        ni = ar * hi + ai * hr + si_ref[:, c, :]
        return nr, ni

    hr, hi = lax.fori_loop(0, steps, body, (hr_scr[...], hi_scr[...]))
    hr_scr[...] = hr
    hi_scr[...] = hi

    @pl.when(i == n_i - 1)
    def _():
        fr_ref[...] = hr
        fi_ref[...] = hi


def _s5_chunk_scan(s_re, s_im, a_re, a_im, h0_re, h0_im, bsz, reverse):
    _, n_c, width = s_re.shape
    tr = min(64, n_c)
    tc = 512
    n_i = n_c // tr

    def rmap(j, i):
        return (0, (n_i - 1 - i) if reverse else i, j)

    big = jax.ShapeDtypeStruct((bsz, n_c, width), F32)
    small = jax.ShapeDtypeStruct((bsz, width), F32)
    return pl.pallas_call(
        functools.partial(_s5_scan_kernel, reverse=reverse),
        grid=(width // tc, n_i),
        in_specs=[
            pl.BlockSpec((bsz, tr, tc), rmap), pl.BlockSpec((bsz, tr, tc), rmap),
            pl.BlockSpec((1, tc), lambda j, i: (0, j)), pl.BlockSpec((1, tc), lambda j, i: (0, j)),
            pl.BlockSpec((bsz, tc), lambda j, i: (0, j)), pl.BlockSpec((bsz, tc), lambda j, i: (0, j)),
        ],
        out_specs=[
            pl.BlockSpec((bsz, tr, tc), rmap), pl.BlockSpec((bsz, tr, tc), rmap),
            pl.BlockSpec((bsz, tc), lambda j, i: (0, j)), pl.BlockSpec((bsz, tc), lambda j, i: (0, j)),
        ],
        out_shape=[big, big, small, small],
        scratch_shapes=[pltpu.VMEM((bsz, tc), F32), pltpu.VMEM((bsz, tc), F32)],
        compiler_params=_params(("parallel", "arbitrary")),
        name="s5_scan_rev" if reverse else "s5_scan_fwd",
    )(s_re, s_im, a_re, a_im, h0_re, h0_im)


def _s5_out_kernel(u_ref, m_ref, v_ref, hfr_ref, hfi_ref, hbr_ref, hbi_ref, y_ref):
    n = S5_QSTATE
    y = _dot(_s5_operand(u_ref), m_ref[0])
    for k, h_ref in enumerate((hfr_ref, hfi_ref, hbr_ref, hbi_ref)):
        y += _dot(h_ref[...].astype(BF16), v_ref[0, k * n:(k + 1) * n, :])
    for t in range(S5_CHUNK):
        y_ref[:, t, :] = y[:, t * LANES:(t + 1) * LANES]


def _s5_outputs(u3, m_q, v_q, h_states):
    rows = u3.shape[0]
    tr = min(256, rows)
    return pl.pallas_call(
        _s5_out_kernel,
        grid=(S5_QUADS, rows // tr),
        in_specs=[
            pl.BlockSpec((tr, S5_CHUNK, LANES), lambda q, i: (i, 0, q)),
            pl.BlockSpec((1, S5_QCOLS, S5_QCOLS), lambda q, i: (q, 0, 0)),
            pl.BlockSpec((1, 4 * S5_QSTATE, S5_QCOLS), lambda q, i: (q, 0, 0)),
        ] + [pl.BlockSpec((tr, S5_QSTATE), lambda q, i: (i, q))] * 4,
        out_specs=pl.BlockSpec((tr, S5_CHUNK, LANES), lambda q, i: (i, 0, q)),
        out_shape=jax.ShapeDtypeStruct((rows, S5_CHUNK, S5_WIDTH), F32),
        compiler_params=_params(("parallel", "parallel")),
        name="s5_out",
    )(u3, m_q, v_q, *h_states)


def _s5_branch(u, tables, seq_len, init, want_y):
    w_q, m_q, v_q, a_re, a_im = tables
    t = u.shape[0]
    bsz = t // seq_len
    n_c = seq_len // S5_CHUNK
    width = S5_GROUPS * S5_STATE
    u3 = u.reshape(t // S5_CHUNK, S5_CHUNK, S5_WIDTH)
    s_loc = [s.reshape(bsz, n_c, width) for s in _s5_local_states(u3, w_q)]
    hfr, hfi, ffr, ffi = _s5_chunk_scan(s_loc[0], s_loc[1], a_re[0], a_im[0], init[0], init[1], bsz, False)
    hbr, hbi, fbr, fbi = _s5_chunk_scan(s_loc[2], s_loc[3], a_re[1], a_im[1], init[2], init[3], bsz, True)
    y = None
    if want_y:
        h_states = [h.reshape(bsz * n_c, width) for h in (hfr, hfi, hbr, hbi)]
        y = _s5_outputs(u3, m_q, v_q, h_states).reshape(t, S5_WIDTH)
    return y, (ffr, ffi, fbr, fbi)


def _gelu_tanh(x):
    return 0.5 * x * (1.0 + jnp.tanh(math.sqrt(2.0 / math.pi) * (x + 0.044715 * (x * x * x))))


def _merge_kernel(x_ref, ym_ref, yn_ref, ys_ref, u_ref, gm_ref, gn_ref, gs_ref, d_ref, gw_ref, gb_ref,
                  wm_ref, wn_ref, ws_ref, wo_ref, gate_ref, npost_ref, o_ref):
    ys = ys_ref[...] + d_ref[...] * u_ref[...]
    g = _gelu_tanh(ys)
    gg = g * jax.nn.sigmoid(_dot(g.astype(BF16), gw_ref[...]) + gb_ref[...])
    y = jax.nn.sigmoid(gm_ref[...].astype(F32)) * _dot(ym_ref[...], wm_ref[...])
    y += jax.nn.sigmoid(gn_ref[...].astype(F32)) * _dot(yn_ref[...], wn_ref[...])
    y += jax.nn.sigmoid(gs_ref[...].astype(F32)) * _dot(gg.astype(BF16), ws_ref[...])
    out = _dot(y.astype(BF16), wo_ref[...])
    o_ref[...] = x_ref[...] + gate_ref[0] * _rms(out, npost_ref[...])


def _merge(x, ym, yn, ys, u, px, wts, gate, npost, rows_per_mod):
    t = x.shape[0]
    tm = min(512, rows_per_mod)
    per = rows_per_mod // tm
    s5_d, glu_w, glu_b, w_m, w_n, w_s, w_o = wts
    row = lambda i: (i, 0)
    const = lambda i: (0, 0)
    return pl.pallas_call(
        _merge_kernel,
        grid=(t // tm,),
        in_specs=[
            pl.BlockSpec((tm, D_MODEL), row),
            pl.BlockSpec((tm, M_WIDTH), row),
            pl.BlockSpec((tm, NA_WIDTH), row),
            pl.BlockSpec((tm, S5_WIDTH), row),
            pl.BlockSpec((tm, S5_WIDTH), row),
            pl.BlockSpec((tm, D_MODEL), lambda i: (i, COL_GATE)),
            pl.BlockSpec((tm, D_MODEL), lambda i: (i, COL_GATE + 1)),
            pl.BlockSpec((tm, D_MODEL), lambda i: (i, COL_GATE + 2)),
            pl.BlockSpec((1, S5_WIDTH), const),
            pl.BlockSpec((S5_WIDTH, S5_WIDTH), const),
            pl.BlockSpec((1, S5_WIDTH), const),
            pl.BlockSpec((M_WIDTH, D_MODEL), const),
            pl.BlockSpec((NA_WIDTH, D_MODEL), const),
            pl.BlockSpec((S5_WIDTH, D_MODEL), const),
            pl.BlockSpec((D_MODEL, D_MODEL), const),
            pl.BlockSpec((1, 1, D_MODEL), lambda i: (i // per, 0, 0)),
            pl.BlockSpec((1, D_MODEL), const),
        ],
        out_specs=pl.BlockSpec((tm, D_MODEL), row),
        out_shape=jax.ShapeDtypeStruct((t, D_MODEL), F32),
        compiler_params=_params(("parallel",)),
        name="merge",
    )(x, ym, yn, ys, u, px, px, px, s5_d, glu_w, glu_b, w_m, w_n, w_s, w_o, gate, npost)


def _swiglu(h, wg, wu, wd):
    a = _dot(h, wg)
    a = a * jax.nn.sigmoid(a) * _dot(h, wu)
    return _dot(a.astype(BF16), wd)


def _ffn_kernel(x_ref, g_ref, sh_ref, sc_ref, wg_ref, wu_ref, wd_ref, gate_ref, npost_ref,
                o_ref, hn_ref, acc_ref):
    j = pl.program_id(1)

    @pl.when(j == 0)
    def _():
        h = _rms(x_ref[...], g_ref[...]) * (1.0 + sc_ref[0]) + sh_ref[0]
        hn_ref[...] = h.astype(BF16)
        acc_ref[...] = jnp.zeros_like(acc_ref)

    acc_ref[...] += _swiglu(hn_ref[...], wg_ref[...], wu_ref[...], wd_ref[...])

    @pl.when(j == pl.num_programs(1) - 1)
    def _():
        o_ref[...] = x_ref[...] + gate_ref[0] * _rms(acc_ref[...], npost_ref[...])


def _ffn(x, g, shift, scale, w_gate, w_up, w_down, gate, npost, rows_per_mod):
    t = x.shape[0]
    ff = w_gate.shape[1]
    ff_tile = ff // 2
    tm = min(512, rows_per_mod)
    per = rows_per_mod // tm
    assert ff_tile % LANES == 0
    return pl.pallas_call(
        _ffn_kernel,
        grid=(t // tm, ff // ff_tile),
        in_specs=[
            pl.BlockSpec((tm, D_MODEL), lambda i, j: (i, 0)),
            pl.BlockSpec((1, D_MODEL), lambda i, j: (0, 0)),
            pl.BlockSpec((1, 1, D_MODEL), lambda i, j: (i // per, 0, 0)),
            pl.BlockSpec((1, 1, D_MODEL), lambda i, j: (i // per, 0, 0)),
            pl.BlockSpec((D_MODEL, ff_tile), lambda i, j: (0, j)),
            pl.BlockSpec((D_MODEL, ff_tile), lambda i, j: (0, j)),
            pl.BlockSpec((ff_tile, D_MODEL), lambda i, j: (j, 0)),
            pl.BlockSpec((1, 1, D_MODEL), lambda i, j: (i // per, 0, 0)),
            pl.BlockSpec((1, D_MODEL), lambda i, j: (0, 0)),
        ],
        out_specs=pl.BlockSpec((tm, D_MODEL), lambda i, j: (i, 0)),
        out_shape=jax.ShapeDtypeStruct((t, D_MODEL), F32),
        scratch_shapes=[pltpu.VMEM((tm, D_MODEL), BF16), pltpu.VMEM((tm, D_MODEL), F32)],
        compiler_params=_params(("parallel", "arbitrary")),
        name="ffn",
    )(x, g, shift, scale, w_gate, w_up, w_down, gate, npost)


def _moe_kernel(x_ref, g_ref, sh_ref, sc_ref, rh_ref, rl_ref, wg_ref, wu_ref, wd_ref, gate_ref, npost_ref,
                o_ref, hn_ref, acc_ref, comb_ref, rank_c_ref, rank_r_ref, mask_r_ref, xc_ref, yc_ref, cnt_ref):
    e = pl.program_id(1)
    j = pl.program_id(2)
    tm = x_ref.shape[0]
    cap = MOE_CAP

    @pl.when((e == 0) & (j == 0))
    def _():
        h = _rms(x_ref[...], g_ref[...]) * (1.0 + sc_ref[0]) + sh_ref[0]
        hi = h.astype(BF16)
        hn_ref[...] = hi
        acc_ref[...] = jnp.zeros_like(acc_ref)
        lo = (h - hi.astype(F32)).astype(BF16)
        logits = _dot(hi, rh_ref[...]) + _dot(lo, rh_ref[...]) + _dot(hi, rl_ref[...])
        lane = lax.broadcasted_iota(jnp.int32, logits.shape, 1)
        logits = jnp.where(lane < N_EXPERTS, logits, -jnp.inf)
        m1 = jnp.max(logits, axis=-1, keepdims=True)
        i1 = jnp.min(jnp.where(logits == m1, lane, LANES), axis=-1, keepdims=True)
        rest = jnp.where(lane == i1, -jnp.inf, logits)
        m2 = jnp.max(rest, axis=-1, keepdims=True)
        i2 = jnp.min(jnp.where(rest == m2, lane, LANES), axis=-1, keepdims=True)
        e2 = jnp.exp(m2 - m1)
        comb_ref[...] = jnp.where(lane == i1, 1.0 / (1.0 + e2), 0.0) + jnp.where(lane == i2, e2 / (1.0 + e2), 0.0)
        sel = jnp.where((lane == i1) | (lane == i2), 1.0, 0.0)
        ti = lax.broadcasted_iota(jnp.int32, (tm, tm), 0)
        si = lax.broadcasted_iota(jnp.int32, (tm, tm), 1)
        before = jnp.where(si < ti, 1.0, 0.0).astype(BF16)
        rank_c_ref[...] = _dot(before, sel.astype(BF16))
        sel_t = sel.T
        after = jnp.where(ti < si, 1.0, 0.0).astype(BF16)
        rank_r_ref[...] = _dot(sel_t.astype(BF16), after)
        mask_r_ref[...] = sel_t
        counts = jnp.sum(sel_t, axis=1, keepdims=True)
        for k in range(N_EXPERTS):
            cnt_ref[k] = jnp.sum(counts[k:k + 1, :]).astype(jnp.int32)

    n_pass = (cnt_ref[e] + (cap - 1)) // cap

    @pl.when(j == 0)
    def _():
        rank_row = rank_r_ref[pl.ds(e, 1), :]
        mask_row = mask_r_ref[pl.ds(e, 1), :]
        slot = lax.broadcasted_iota(jnp.int32, (cap, tm), 0).astype(F32)

        def compact(p, _):
            base = (p * cap).astype(F32)
            onehot = jnp.where((rank_row - base == slot) & (mask_row > 0.0), 1.0, 0.0).astype(BF16)
            rows = pl.ds(pl.multiple_of(p * cap, 16), cap)
            xc_ref[rows, :] = _dot(onehot, hn_ref[...]).astype(BF16)
            yc_ref[rows, :] = jnp.zeros((cap, D_MODEL), F32)
            return 0

        lax.fori_loop(0, n_pass, compact, 0)

    def expert(p, _):
        rows = pl.ds(pl.multiple_of(p * cap, 16), cap)
        yc_ref[rows, :] += _swiglu(xc_ref[rows, :], wg_ref[0], wu_ref[0], wd_ref[0])
        return 0

    lax.fori_loop(0, n_pass, expert, 0)

    @pl.when(j == pl.num_programs(2) - 1)
    def _():
        lane = lax.broadcasted_iota(jnp.int32, (tm, LANES), 1)
        pick = lane == e
        rank_col = jnp.sum(jnp.where(pick, rank_c_ref[...], 0.0), axis=-1, keepdims=True)
        w_col = jnp.sum(jnp.where(pick, comb_ref[...], 0.0), axis=-1, keepdims=True)
        slot = lax.broadcasted_iota(jnp.int32, (tm, cap), 1).astype(F32)

        def expand(p, _):
            base = (p * cap).astype(F32)
            onehot = jnp.where(rank_col - base == slot, 1.0, 0.0).astype(BF16)
            rows = pl.ds(pl.multiple_of(p * cap, 16), cap)
            acc_ref[...] += w_col * _dot(onehot, yc_ref[rows, :].astype(BF16))
            return 0

        lax.fori_loop(0, n_pass, expand, 0)

    @pl.when((e == N_EXPERTS - 1) & (j == pl.num_programs(2) - 1))
    def _():
        o_ref[...] = x_ref[...] + gate_ref[0] * _rms(acc_ref[...], npost_ref[...])


def _moe(x, g, shift, scale, r_hi, r_lo, w_gate, w_up, w_down, gate, npost, rows_per_mod):
    t = x.shape[0]
    n_e, _, ff = w_gate.shape
    assert n_e == N_EXPERTS
    ff_tile = ff // 4
    tm = min(MOE_TILE, rows_per_mod)
    per = rows_per_mod // tm
    passes = -(-tm // MOE_CAP)
    assert ff_tile % LANES == 0 and MOE_CAP % 16 == 0
    return pl.pallas_call(
        _moe_kernel,
        grid=(t // tm, n_e, ff // ff_tile),
        in_specs=[
            pl.BlockSpec((tm, D_MODEL), lambda i, e, j: (i, 0)),
            pl.BlockSpec((1, D_MODEL), lambda i, e, j: (0, 0)),
            pl.BlockSpec((1, 1, D_MODEL), lambda i, e, j: (i // per, 0, 0)),
            pl.BlockSpec((1, 1, D_MODEL), lambda i, e, j: (i // per, 0, 0)),
            pl.BlockSpec((D_MODEL, LANES), lambda i, e, j: (0, 0)),
            pl.BlockSpec((D_MODEL, LANES), lambda i, e, j: (0, 0)),
            pl.BlockSpec((1, D_MODEL, ff_tile), lambda i, e, j: (e, 0, j)),
            pl.BlockSpec((1, D_MODEL, ff_tile), lambda i, e, j: (e, 0, j)),
            pl.BlockSpec((1, ff_tile, D_MODEL), lambda i, e, j: (e, j, 0)),
            pl.BlockSpec((1, 1, D_MODEL), lambda i, e, j: (i // per, 0, 0)),
            pl.BlockSpec((1, D_MODEL), lambda i, e, j: (0, 0)),
        ],
        out_specs=pl.BlockSpec((tm, D_MODEL), lambda i, e, j: (i, 0)),
        out_shape=jax.ShapeDtypeStruct((t, D_MODEL), F32),
        scratch_shapes=[
            pltpu.VMEM((tm, D_MODEL), BF16),
            pltpu.VMEM((tm, D_MODEL), F32),
            pltpu.VMEM((tm, LANES), F32),
            pltpu.VMEM((tm, LANES), F32),
            pltpu.VMEM((LANES, tm), F32),
            pltpu.VMEM((LANES, tm), F32),
            pltpu.VMEM((passes * MOE_CAP, D_MODEL), BF16),
            pltpu.VMEM((passes * MOE_CAP, D_MODEL), F32),
            pltpu.SMEM((N_EXPERTS,), jnp.int32),
        ],
        compiler_params=_params(("parallel", "arbitrary", "arbitrary")),
        name="moe",
    )(x, g, shift, scale, r_hi, r_lo, w_gate, w_up, w_down, gate, npost)


def _reorder_w_in(w):
    o = 4 * M_WIDTH
    gates = jnp.pad(w[:, o:o + M_GATES], ((0, 0), (0, LANES - M_GATES)))
    na = w[:, o + M_GATES:o + M_GATES + 3 * NA_WIDTH]
    u0 = o + M_GATES + 3 * NA_WIDTH
    main = jnp.concatenate([w[:, :o], w[:, u0 + S5_WIDTH:], na], axis=1)
    return main.astype(BF16), gates.astype(BF16), w[:, u0:u0 + S5_WIDTH].astype(BF16)


def _row(v):
    return v.reshape(1, -1).astype(F32)


def _token_mixers(x, xc, mx, mc, lp, rope, seq_len, ctx_len, ctx_out):
    bsz = x.shape[0] // seq_len
    w_main, w_gates, w_u = _reorder_w_in(lp['w_in'])
    g_pre = _row(lp['norm_mix_pre'])
    px, gx, ux = _inproj(x, g_pre, mx[0], mx[1], w_main, w_gates, w_u, seq_len)
    pc, gc, uc = _inproj(xc, g_pre, mc[0], mc[1], w_main, w_gates, w_u, xc.shape[0])

    conv_w = lp['m_conv_w'].astype(F32)
    conv_b = _row(lp['m_conv_b'])
    gate_b = jnp.pad(_row(lp['m_gate_b']), ((0, 0), (0, LANES - M_GATES)))
    m_norm = _row(lp['m_norm'])
    qk_c = _qkconv(pc, conv_w, conv_b, rope[0][:ctx_len], rope[1][:ctx_len], ctx_len, False)
    qk_x = _qkconv(px, conv_w, conv_b, rope[0], rope[1], seq_len, True)
    zero = _mlstm_zero_state(bsz)
    ym_c, fin_f, fin_b = _mlstm_bidir(qk_c, pc, gc, gate_b, m_norm, zero, zero, ctx_len)
    ym_x, _, _ = _mlstm_bidir(qk_x, px, gx, gate_b, m_norm, fin_f, fin_b, seq_len)

    yn_x = _na_attention(px, pc, _na_bias_table(lp['na_rpb']), seq_len, ctx_len)

    tables = _s5_tables(lp['s5_lam_re'], lp['s5_lam_im'], lp['s5_log_dt'], lp['s5_b_re'], lp['s5_b_im'],
                        lp['s5_c_re'], lp['s5_c_im'])
    zs = jnp.zeros((bsz, S5_GROUPS * S5_STATE), F32)
    ys_c, fin_s = _s5_branch(uc, tables, ctx_len, (zs, zs, zs, zs), ctx_out)
    ys_x, _ = _s5_branch(ux, tables, seq_len, fin_s, True)

    wts = (_row(lp['s5_d']), lp['s5_glu_w'].astype(BF16), _row(lp['s5_glu_b']),
           lp['w_branch_m'].astype(BF16), lp['w_branch_na'].astype(BF16), lp['w_branch_s5'].astype(BF16),
           lp['w_out'].astype(BF16))
    g_post = _row(lp['norm_mix_post'])
    x = _merge(x, ym_x, yn_x, ys_x, ux, px, wts, mx[2], g_post, seq_len)
    if ctx_out:
        yn_c = _ctx_attention(pc, ctx_len)
        xc = _merge(xc, ym_c, yn_c, ys_c, uc, pc, wts, mc[2], g_post, xc.shape[0])
    return x, xc


def kernel(x, c, ctx, c_ctx, ada_w, ada_b, norm_mix_pre, norm_mix_post, norm_ffn_pre, norm_ffn_post, w_in, m_gate_b, m_conv_w, m_conv_b, m_norm, na_rpb, s5_lam_re, s5_lam_im, s5_log_dt, s5_b_re, s5_b_im, s5_c_re, s5_c_im, s5_d, s5_glu_w, s5_glu_b, w_branch_m, w_branch_na, w_branch_s5, w_out, ffn_w_gate, ffn_w_up, ffn_w_down, moe_router, moe_w_gate, moe_w_up, moe_w_down):
    bsz, seq_len, d = x.shape
    ctx_len = ctx.shape[1]
    depth = w_in.shape[0]
    assert d == D_MODEL and seq_len % (GRID_W * NA_ROWS) == 0 and ctx_len % M_CHUNK == 0
    rope = _rope_tables(seq_len)
    xf = x.reshape(bsz * seq_len, d)
    xc = ctx.reshape(bsz * ctx_len, d)
    cvecs = jnp.zeros((16, d), F32).at[:bsz].set(c).at[bsz].set(c_ctx)
    for l in range(depth):
        last = l == depth - 1
        mod = _adaln(cvecs, ada_w[l].astype(BF16), _row(ada_b[l]))
        mx = [mod[:bsz, k * d:(k + 1) * d].reshape(bsz, 1, d) for k in range(N_MOD)]
        mc = [mod[bsz:bsz + 1, k * d:(k + 1) * d].reshape(1, 1, d) for k in range(N_MOD)]
        lp = {
            'norm_mix_pre': norm_mix_pre[l], 'norm_mix_post': norm_mix_post[l],
            'w_in': w_in[l], 'm_gate_b': m_gate_b[l], 'm_conv_w': m_conv_w[l], 'm_conv_b': m_conv_b[l],
            'm_norm': m_norm[l], 'na_rpb': na_rpb[l], 's5_lam_re': s5_lam_re[l], 's5_lam_im': s5_lam_im[l],
            's5_log_dt': s5_log_dt[l], 's5_b_re': s5_b_re[l], 's5_b_im': s5_b_im[l], 's5_c_re': s5_c_re[l],
            's5_c_im': s5_c_im[l], 's5_d': s5_d[l], 's5_glu_w': s5_glu_w[l], 's5_glu_b': s5_glu_b[l],
            'w_branch_m': w_branch_m[l], 'w_branch_na': w_branch_na[l], 'w_branch_s5': w_branch_s5[l],
            'w_out': w_out[l],
        }
        xf, xc = _token_mixers(xf, xc, mx, mc, lp, rope, seq_len, ctx_len, not last)
        j = l // 2
        g_pre, g_post = _row(norm_ffn_pre[l]), _row(norm_ffn_post[l])
        if l % 2 == 0:
            wg, wu, wd = ffn_w_gate[j].astype(BF16), ffn_w_up[j].astype(BF16), ffn_w_down[j].astype(BF16)

            def channel(h, m, rows):
                return _ffn(h, g_pre, m[3], m[4], wg, wu, wd, m[5], g_post, rows)
        else:
            wg, wu, wd = moe_w_gate[j].astype(BF16), moe_w_up[j].astype(BF16), moe_w_down[j].astype(BF16)
            router = jnp.pad(moe_router[j].astype(F32), ((0, 0), (0, LANES - N_EXPERTS)))
            r_hi = router.astype(BF16)
            r_lo = (router - r_hi.astype(F32)).astype(BF16)

            def channel(h, m, rows):
                return _moe(h, g_pre, m[3], m[4], r_hi, r_lo, wg, wu, wd, m[5], g_post, rows)
        xf = channel(xf, mx, seq_len)
        if not last:
            xc = channel(xc, mc, xc.shape[0])
    return xf.reshape(bsz, seq_len, d)
```

```python
import functools
import math

import jax
import jax.numpy as jnp
from jax import lax
from jax.experimental import pallas as pl
from jax.experimental.pallas import tpu as pltpu

F32 = jnp.float32
BF16 = jnp.bfloat16

D_MODEL = 1024
EPS = 1e-6
N_MOD = 6
GRID_W = 64

M_HEADS = 4
M_HEAD_DIM = 128
M_WIDTH = M_HEADS * M_HEAD_DIM
M_GATES = 4 * M_HEADS
M_CHUNK = 128
ROPE_BASE = 10000.0

NA_HEADS = 8
NA_HEAD_DIM = 64
NA_WIDTH = NA_HEADS * NA_HEAD_DIM
WIN_R = 8
WIN_C = 16
NEG_BIG = -1e30

S5_GROUP = 16
S5_WIDTH = 512
S5_GROUPS = S5_WIDTH // S5_GROUP
S5_STATE = 64
S5_MAX_RE = -1e-4

LANES = 128
VMEM_LIMIT = 52 * 1024 * 1024

S5_CHUNK = 16
S5_QUAD_GROUPS = LANES // S5_GROUP
S5_QUADS = S5_GROUPS // S5_QUAD_GROUPS
S5_QCOLS = S5_CHUNK * LANES
S5_QSTATE = S5_QUAD_GROUPS * S5_STATE

N_EXPERTS = 8
TOP_K = 2
MOE_TILE = 1024
MOE_CAP = 256
MOE_GRAIN = 64

COL_QK, COL_V, COL_O, COL_NAQ, COL_NAK, COL_NAV = 0, 2, 3, 10, 11, 12
COL_GATE = 2
IN_MAIN = 6656


def _params(sem):
    return pltpu.CompilerParams(dimension_semantics=sem, vmem_limit_bytes=VMEM_LIMIT)


def _dot(a, b):
    return jnp.dot(a, b, preferred_element_type=F32)


def _dot_nt(a, b):
    return lax.dot_general(a, b, (((1,), (1,)), ((), ())), preferred_element_type=F32)


def _split3(x):
    hi = x.astype(BF16)
    r1 = x - hi.astype(F32)
    mid = r1.astype(BF16)
    lo = (r1 - mid.astype(F32)).astype(BF16)
    return hi, mid, lo


def _rms(x, g):
    return x * lax.rsqrt(jnp.mean(x * x, axis=-1, keepdims=True) + EPS) * g


def _inproj_kernel(x_ref, g_ref, sh_ref, sc_ref, w_ref, wg_ref, wu_ref, o_ref, og_ref, ou_ref, hn_ref):
    @pl.when(pl.program_id(1) == 0)
    def _():
        h = _rms(x_ref[...], g_ref[...]) * (1.0 + sc_ref[0]) + sh_ref[0]
        hb = h.astype(BF16)
        hn_ref[...] = hb
        og_ref[...] = _dot(hb, wg_ref[...])
        ou_ref[...] = _dot(hb, wu_ref[...])

    o_ref[...] = _dot(hn_ref[...], w_ref[...]).astype(o_ref.dtype)


def _inproj(x, g, shift, scale, w_main, w_gates, w_u, rows_per_mod):
    t = x.shape[0]
    tm = min(1024, rows_per_mod)
    tn = IN_MAIN // 4
    assert rows_per_mod % tm == 0 and t % tm == 0 and tn % LANES == 0
    per = rows_per_mod // tm
    return pl.pallas_call(
        _inproj_kernel,
        grid=(t // tm, IN_MAIN // tn),
        in_specs=[
            pl.BlockSpec((tm, D_MODEL), lambda i, j: (i, 0)),
            pl.BlockSpec((1, D_MODEL), lambda i, j: (0, 0)),
            pl.BlockSpec((1, 1, D_MODEL), lambda i, j: (i // per, 0, 0)),
            pl.BlockSpec((1, 1, D_MODEL), lambda i, j: (i // per, 0, 0)),
            pl.BlockSpec((D_MODEL, tn), lambda i, j: (0, j)),
            pl.BlockSpec((D_MODEL, LANES), lambda i, j: (0, 0)),
            pl.BlockSpec((D_MODEL, S5_WIDTH), lambda i, j: (0, 0)),
        ],
        out_specs=[
            pl.BlockSpec((tm, tn), lambda i, j: (i, j)),
            pl.BlockSpec((tm, LANES), lambda i, j: (i, 0)),
            pl.BlockSpec((tm, S5_WIDTH), lambda i, j: (i, 0)),
        ],
        out_shape=[jax.ShapeDtypeStruct((t, IN_MAIN), BF16), jax.ShapeDtypeStruct((t, LANES), F32),
                   jax.ShapeDtypeStruct((t, S5_WIDTH), F32)],
        scratch_shapes=[pltpu.VMEM((tm, D_MODEL), BF16)],
        compiler_params=_params(("parallel", "arbitrary")),
        name="inproj",
    )(x, g, shift, scale, w_main, w_gates, w_u)


def _adaln_kernel(c_ref, w_ref, b_ref, o_ref):
    c = c_ref[...]
    s = c * jax.nn.sigmoid(c)
    o_ref[...] = _dot(s.astype(BF16), w_ref[...]) + b_ref[...]


def _adaln(cvecs, w, b):
    n = w.shape[1]
    tn = 1024
    return pl.pallas_call(
        _adaln_kernel,
        grid=(n // tn,),
        in_specs=[
            pl.BlockSpec(cvecs.shape, lambda j: (0, 0)),
            pl.BlockSpec((D_MODEL, tn), lambda j: (0, j)),
            pl.BlockSpec((1, tn), lambda j: (0, j)),
        ],
        out_specs=pl.BlockSpec((cvecs.shape[0], tn), lambda j: (0, j)),
        out_shape=jax.ShapeDtypeStruct((cvecs.shape[0], n), F32),
        compiler_params=_params(("parallel",)),
        name="adaln",
    )(cvecs, w, b)


HALO = 16


def _swap32(x):
    lane = lax.broadcasted_iota(jnp.int32, x.shape, 1)
    fwd = pltpu.roll(x, 96, 1)
    bwd = pltpu.roll(x, 32, 1)
    return jnp.where((lane % 64) < 32, fwd, bwd)


def _qkconv_kernel(x_ref, pv_ref, nx_ref, w_ref, b_ref, cos_ref, sin_ref, o_ref, *, tiles_per_seq, rope):
    i = pl.program_id(0)
    tm = x_ref.shape[0]
    x = x_ref[...].astype(F32)
    first = (i % tiles_per_seq) == 0
    last = (i % tiles_per_seq) == tiles_per_seq - 1
    prev_row = jnp.where(first, 0.0, pv_ref[HALO - 1:HALO, :].astype(F32))
    next_row = jnp.where(last, 0.0, nx_ref[0:1, :].astype(F32))
    row = lax.broadcasted_iota(jnp.int32, x.shape, 0)
    xp = jnp.where(row == 0, prev_row, pltpu.roll(x, 1, 0))
    xn = jnp.where(row == tm - 1, next_row, pltpu.roll(x, tm - 1, 0))
    y = xp * w_ref[0:1, :] + x * w_ref[1:2, :] + xn * w_ref[2:3, :] + b_ref[...]
    y = y * jax.nn.sigmoid(y)
    kscale = M_HEAD_DIM ** -0.5
    for hh in range(2 * M_HEADS):
        ys = y[:, hh * LANES:(hh + 1) * LANES]
        if rope:
            ys = ys * cos_ref[...] + _swap32(ys) * sin_ref[...]
        if hh >= M_HEADS:
            ys = ys * kscale
        o_ref[:, hh * LANES:(hh + 1) * LANES] = ys.astype(o_ref.dtype)


def _qkconv(px, conv_w, conv_b, cos_t, sin_t, seq_len, rope):
    t = px.shape[0]
    tm = min(512, seq_len)
    assert seq_len % tm == 0
    tiles = seq_len // tm
    hb = tm // HALO
    nh = t // HALO
    return pl.pallas_call(
        functools.partial(_qkconv_kernel, tiles_per_seq=tiles, rope=rope),
        grid=(t // tm,),
        in_specs=[
            pl.BlockSpec((tm, 2 * M_WIDTH), lambda i: (i, 0)),
            pl.BlockSpec((HALO, 2 * M_WIDTH), lambda i: (jnp.maximum(i * hb - 1, 0), 0)),
            pl.BlockSpec((HALO, 2 * M_WIDTH), lambda i: (jnp.minimum((i + 1) * hb, nh - 1), 0)),
            pl.BlockSpec((3, 2 * M_WIDTH), lambda i: (0, 0)),
            pl.BlockSpec((1, 2 * M_WIDTH), lambda i: (0, 0)),
            pl.BlockSpec((tm, LANES), lambda i: (i % tiles, 0)),
            pl.BlockSpec((tm, LANES), lambda i: (i % tiles, 0)),
        ],
        out_specs=pl.BlockSpec((tm, 2 * M_WIDTH), lambda i: (i, 0)),
        out_shape=jax.ShapeDtypeStruct((t, 2 * M_WIDTH), BF16),
        compiler_params=_params(("parallel",)),
        name="qkconv",
    )(px, px, px, conv_w, conv_b, cos_t, sin_t)


def _rope_tables(n):
    pos = jnp.arange(n, dtype=jnp.int32)
    row = (pos // GRID_W).astype(F32)
    col = (pos % GRID_W).astype(F32)
    n_freq = M_HEAD_DIM // 4
    inv = ROPE_BASE ** (-jnp.arange(n_freq, dtype=F32) / n_freq)
    ar, ac = row[:, None] * inv, col[:, None] * inv
    cos_t = jnp.concatenate([jnp.cos(ar), jnp.cos(ar), jnp.cos(ac), jnp.cos(ac)], axis=-1)
    sin_t = jnp.concatenate([-jnp.sin(ar), jnp.sin(ar), -jnp.sin(ac), jnp.sin(ac)], axis=-1)
    return cos_t, sin_t


def _mlstm_kernel(*refs, reverse, final, nb):
    if final:
        (qk_ref, v_ref, g_ref, gb_ref, c0_ref, n0_ref, m0_ref, hp_ref, op_ref, nw_ref,
         out_ref, cf_ref, nf_ref, mf_ref, c_scr, n_scr, m_scr, qk_scr, qc_scr, s_scr, vu_scr, dec_scr) = refs
    else:
        (qk_ref, v_ref, g_ref, gb_ref, c0_ref, n0_ref, m0_ref,
         out_ref, cf_ref, nf_ref, mf_ref, c_scr, n_scr, m_scr, qk_scr, qc_scr, s_scr, vu_scr, dec_scr) = refs
    c_idx = pl.program_id(1)
    n_c = pl.num_programs(1)
    L = M_CHUNK

    @pl.when(c_idx == 0)
    def _():
        c_scr[...] = c0_ref[...]
        n_scr[...] = n0_ref[...]
        m_scr[...] = m0_ref[...]

    ti = lax.broadcasted_iota(jnp.int32, (L, L), 0)
    si = lax.broadcasted_iota(jnp.int32, (L, L), 1)
    keep = (si >= ti) if reverse else (si <= ti)
    tri = jnp.where(keep, 1.0, 0.0).astype(BF16)
    i_off = 2 * M_HEADS if reverse else 0
    f_off = i_off + M_HEADS
    end = 0 if reverse else L - 1

    units = [(a, b) for a in range(nb) for b in range(M_HEADS)]

    def head(ref, gi, h, base=0):
        return ref[gi, :, base + h * LANES:base + (h + 1) * LANES]

    for u, (gi, h) in enumerate(units):
        q = head(qk_ref, gi, h)
        qk_scr[u] = _dot_nt(q, head(qk_ref, gi, h, M_WIDTH))
        qc_scr[u] = _dot_nt(q, c_scr[gi, h].astype(BF16))

    for u, (gi, h) in enumerate(units):
        if h == 0:
            g = g_ref[gi] + gb_ref[...]
            f_hi, f_mid, f_lo = _split3(jax.nn.log_sigmoid(g))
            bsum = _dot(tri, f_hi) + _dot(tri, f_mid) + _dot(tri, f_lo)
            g_t = g.T
            b_t = bsum.T
        n_vec = n_scr[gi, h]
        m_prev = m_scr[gi, h][:, 0:1]
        b_col = bsum[:, f_off + h:f_off + h + 1]
        i_col = g[:, i_off + h:i_off + h + 1]
        b_row = b_t[f_off + h:f_off + h + 1, :]
        i_row = g_t[i_off + h:i_off + h + 1, :]
        b_end = b_row[:, end:end + 1]

        log_w = jnp.where(keep, b_col - b_row + i_row, -jnp.inf)
        carry_log = b_col + m_prev
        m_t = jnp.maximum(carry_log, jnp.max(log_w, axis=-1, keepdims=True))
        s = qk_scr[u] * jnp.exp(log_w - m_t)
        c_scale = jnp.exp(carry_log - m_t)
        qf = head(qk_ref, gi, h).astype(F32)
        den = jnp.sum(s, axis=-1, keepdims=True) + c_scale * jnp.sum(qf * n_vec, axis=-1, keepdims=True)
        inv = 1.0 / jnp.maximum(jnp.abs(den), jnp.exp(-m_t))
        s_scr[u] = (s * inv).astype(BF16)
        qc_scr[u] = qc_scr[u] * (c_scale * inv)

        m_new = jnp.maximum(b_end + m_prev, jnp.max(b_end - b_row + i_row, axis=-1, keepdims=True))
        u_col = jnp.exp(b_end - b_col + i_col - m_new)
        decay = jnp.exp(b_end + m_prev - m_new)
        vu_scr[u] = (head(v_ref, gi, h).astype(F32) * u_col).T.astype(BF16)
        k_f = head(qk_ref, gi, h, M_WIDTH).astype(F32)
        n_scr[gi, h] = decay * n_vec + jnp.sum(k_f * u_col, axis=0, keepdims=True)
        m_scr[gi, h] = jnp.broadcast_to(m_new, (1, LANES))
        dec_scr[u] = jnp.broadcast_to(decay, (1, LANES))

    for u, (gi, h) in enumerate(units):
        qc_scr[u] = _dot(s_scr[u], head(v_ref, gi, h)) + qc_scr[u]
        c_scr[gi, h] = dec_scr[u][:, 0:1] * c_scr[gi, h] + _dot(vu_scr[u], head(qk_ref, gi, h, M_WIDTH))

    for u, (gi, h) in enumerate(units):
        hh = qc_scr[u]
        if final:
            hs = hh + head(hp_ref, gi, h)
            mu = jnp.mean(hs, axis=-1, keepdims=True)
            var = jnp.mean(jnp.square(hs - mu), axis=-1, keepdims=True)
            hn = (hs - mu) * lax.rsqrt(var + EPS)
            y = hn * nw_ref[:, h * LANES:(h + 1) * LANES] * jax.nn.sigmoid(head(op_ref, gi, h).astype(F32))
            out_ref[gi, :, h * LANES:(h + 1) * LANES] = y.astype(out_ref.dtype)
        else:
            out_ref[gi, :, h * LANES:(h + 1) * LANES] = hh.astype(out_ref.dtype)

    @pl.when(c_idx == n_c - 1)
    def _():
        cf_ref[...] = c_scr[...]
        nf_ref[...] = n_scr[...]
        mf_ref[...] = m_scr[...]


def _mlstm_scan(qk, px, gates, gate_b, state, seq_len, reverse, hprev=None, norm_w=None):
    t = qk.shape[0]
    bsz = t // seq_len
    n_c = seq_len // M_CHUNK
    final = hprev is not None
    nb = 4 if bsz % 4 == 0 else (2 if bsz % 2 == 0 else 1)
    qk, px, gates = (a.reshape(bsz, seq_len, a.shape[1]) for a in (qk, px, gates))

    def chunk(c):
        return (n_c - 1 - c) if reverse else c

    state_specs = [
        pl.BlockSpec((nb, M_HEADS, M_HEAD_DIM, M_HEAD_DIM), lambda b, c: (b, 0, 0, 0)),
        pl.BlockSpec((nb, M_HEADS, 1, LANES), lambda b, c: (b, 0, 0, 0)),
        pl.BlockSpec((nb, M_HEADS, 1, LANES), lambda b, c: (b, 0, 0, 0)),
    ]
    in_specs = [
        pl.BlockSpec((nb, M_CHUNK, 2 * M_WIDTH), lambda b, c: (b, chunk(c), 0)),
        pl.BlockSpec((nb, M_CHUNK, M_WIDTH), lambda b, c: (b, chunk(c), COL_V)),
        pl.BlockSpec((nb, M_CHUNK, LANES), lambda b, c: (b, chunk(c), 0)),
        pl.BlockSpec((1, LANES), lambda b, c: (0, 0)),
    ] + state_specs
    args = [qk, px, gates, gate_b, *state]
    if final:
        in_specs += [
            pl.BlockSpec((nb, M_CHUNK, M_WIDTH), lambda b, c: (b, chunk(c), 0)),
            pl.BlockSpec((nb, M_CHUNK, M_WIDTH), lambda b, c: (b, chunk(c), COL_O)),
            pl.BlockSpec((1, M_WIDTH), lambda b, c: (0, 0)),
        ]
        args += [hprev.reshape(bsz, seq_len, M_WIDTH), px, norm_w]
    out_dtype = BF16 if final else F32
    out, *fin = pl.pallas_call(
        functools.partial(_mlstm_kernel, reverse=reverse, final=final, nb=nb),
        grid=(bsz // nb, n_c),
        in_specs=in_specs,
        out_specs=[pl.BlockSpec((nb, M_CHUNK, M_WIDTH), lambda b, c: (b, chunk(c), 0))] + state_specs,
        out_shape=[
            jax.ShapeDtypeStruct((bsz, seq_len, M_WIDTH), out_dtype),
            jax.ShapeDtypeStruct((bsz, M_HEADS, M_HEAD_DIM, M_HEAD_DIM), F32),
            jax.ShapeDtypeStruct((bsz, M_HEADS, 1, LANES), F32),
            jax.ShapeDtypeStruct((bsz, M_HEADS, 1, LANES), F32),
        ],
        scratch_shapes=[
            pltpu.VMEM((nb, M_HEADS, M_HEAD_DIM, M_HEAD_DIM), F32),
            pltpu.VMEM((nb, M_HEADS, 1, LANES), F32),
            pltpu.VMEM((nb, M_HEADS, 1, LANES), F32),
            pltpu.VMEM((nb * M_HEADS, M_CHUNK, M_CHUNK), F32),
            pltpu.VMEM((nb * M_HEADS, M_CHUNK, M_HEAD_DIM), F32),
            pltpu.VMEM((nb * M_HEADS, M_CHUNK, M_CHUNK), BF16),
            pltpu.VMEM((nb * M_HEADS, M_HEAD_DIM, M_CHUNK), BF16),
            pltpu.VMEM((nb * M_HEADS, 1, LANES), F32),
        ],
        compiler_params=_params(("parallel", "arbitrary")),
        name="mlstm_rev" if reverse else "mlstm_fwd",
    )(*args)
    return [out.reshape(t, M_WIDTH)] + fin


def _mlstm_zero_state(bsz):
    return (jnp.zeros((bsz, M_HEADS, M_HEAD_DIM, M_HEAD_DIM), F32),
            jnp.zeros((bsz, M_HEADS, 1, LANES), F32),
            jnp.zeros((bsz, M_HEADS, 1, LANES), F32))


def _mlstm_bidir(qk, px, gates, gate_b, norm_w, state_f, state_b, seq_len):
    h_f, *fin_f = _mlstm_scan(qk, px, gates, gate_b, state_f, seq_len, False)
    y, *fin_b = _mlstm_scan(qk, px, gates, gate_b, state_b, seq_len, True, hprev=h_f, norm_w=norm_w)
    return y, tuple(fin_f), tuple(fin_b)


NA_ROWS = 8
NA_STEP = 4


def _na_kernel(q_ref, kp_ref, kc_ref, kn_ref, vp_ref, vc_ref, vn_ref, kx_ref, vx_ref, *rest, rows):
    bias_refs = rest[:NA_STEP]
    o_ref, kwin, vwin, s_scr, p_scr = rest[NA_STEP:]
    step = pl.program_id(1)
    blk = NA_ROWS * GRID_W

    @pl.when(step % (NA_ROWS // NA_STEP) == 0)
    def _():
        kwin[0:blk, :] = kp_ref[...]
        kwin[blk:2 * blk, :] = kc_ref[...]
        kwin[2 * blk:3 * blk, :] = kn_ref[...]
        vwin[0:blk, :] = vp_ref[...]
        vwin[blk:2 * blk, :] = vc_ref[...]
        vwin[2 * blk:3 * blk, :] = vn_ref[...]

    g = step // (NA_ROWS // NA_STEP)
    n_loc = WIN_R * GRID_W
    lane = lax.broadcasted_iota(jnp.int32, (GRID_W, LANES), 1)
    scale = NA_HEAD_DIM ** -0.5
    n_ctx = kx_ref.shape[0]
    units = [(kk, hp) for kk in range(NA_STEP) for hp in range(NA_HEADS // 2)]
    offs = []
    for kk in range(NA_STEP):
        r = step * NA_STEP + kk
        r0 = jnp.clip(r - WIN_R // 2, 0, rows - WIN_R)
        offs.append(pl.multiple_of((r0 - NA_ROWS * g + NA_ROWS) * GRID_W, GRID_W))

    for u, (kk, hp) in enumerate(units):
        cols = slice(hp * LANES, (hp + 1) * LANES)
        q2 = q_ref[kk * GRID_W:(kk + 1) * GRID_W, cols] * scale
        zero = jnp.zeros_like(q2)
        qs = jnp.concatenate([jnp.where(lane < NA_HEAD_DIM, q2, zero), jnp.where(lane < NA_HEAD_DIM, zero, q2)],
                             axis=0)
        s_scr[u, :, 0:n_loc] = _dot_nt(qs, kwin[pl.ds(offs[kk], n_loc), cols])
        s_scr[u, :, n_loc:n_loc + n_ctx] = _dot_nt(qs, kx_ref[:, cols])

    for u, (kk, hp) in enumerate(units):
        s_loc = s_scr[u, :, 0:n_loc] + bias_refs[kk][0, hp]
        s_ctx = s_scr[u, :, n_loc:n_loc + n_ctx]
        m = jnp.maximum(jnp.max(s_loc, axis=-1, keepdims=True), jnp.max(s_ctx, axis=-1, keepdims=True))
        p_loc = jnp.exp(s_loc - m)
        p_ctx = jnp.exp(s_ctx - m)
        inv = 1.0 / (jnp.sum(p_loc, axis=-1, keepdims=True) + jnp.sum(p_ctx, axis=-1, keepdims=True))
        p_scr[u, :, 0:n_loc] = (p_loc * inv).astype(BF16)
        p_scr[u, :, n_loc:n_loc + n_ctx] = (p_ctx * inv).astype(BF16)

    for u, (kk, hp) in enumerate(units):
        cols = slice(hp * LANES, (hp + 1) * LANES)
        o2 = _dot(p_scr[u, :, 0:n_loc], vwin[pl.ds(offs[kk], n_loc), cols])
        o2 += _dot(p_scr[u, :, n_loc:n_loc + n_ctx], vx_ref[:, cols])
        out = jnp.where(lane < NA_HEAD_DIM, o2[0:GRID_W], o2[GRID_W:2 * GRID_W])
        o_ref[kk * GRID_W:(kk + 1) * GRID_W, cols] = out.astype(o_ref.dtype)


def _na_bias_table(rpb):
    qc = jnp.arange(GRID_W)[:, None]
    kc = jnp.arange(GRID_W)[None, :]
    cs = jnp.clip(qc - WIN_C // 2, 0, GRID_W - WIN_C)
    ok = (kc >= cs) & (kc < cs + WIN_C)
    dc = jnp.clip(kc - qc + WIN_C - 1, 0, 2 * WIN_C - 2)
    dr = jnp.arange(WIN_R)[:, None] + jnp.arange(WIN_R)[None, :]
    tab = rpb.astype(F32)[:, dr][:, :, :, dc]
    tab = jnp.where(ok[None, None, None], tab, NEG_BIG)
    tab = jnp.transpose(tab, (1, 0, 3, 2, 4))
    return tab.reshape(WIN_R, NA_HEADS // 2, 2 * GRID_W, WIN_R * GRID_W)


def _na_attention(px, pc, bias, seq_len, ctx_len):
    t = px.shape[0]
    bsz = t // seq_len
    rows = seq_len // GRID_W
    n_g = rows // NA_ROWS
    blk = NA_ROWS * GRID_W

    steps = rows // NA_STEP
    per_blk = NA_ROWS // NA_STEP

    def kv_spec(col, shift):
        def imap(b, s):
            g = jnp.clip(s // per_blk + shift, 0, n_g - 1)
            return (b * n_g + g, col)
        return pl.BlockSpec((blk, NA_WIDTH), imap)

    def bias_spec(kk):
        def imap(b, s):
            r = s * NA_STEP + kk
            r0 = jnp.clip(r - WIN_R // 2, 0, rows - WIN_R)
            return (r0 - r + WIN_R - 1, 0, 0, 0)
        return pl.BlockSpec((1, NA_HEADS // 2, 2 * GRID_W, WIN_R * GRID_W), imap)

    return pl.pallas_call(
        functools.partial(_na_kernel, rows=rows),
        grid=(bsz, steps),
        in_specs=[
            pl.BlockSpec((NA_STEP * GRID_W, NA_WIDTH), lambda b, s: (b * steps + s, COL_NAQ)),
            kv_spec(COL_NAK, -1), kv_spec(COL_NAK, 0), kv_spec(COL_NAK, 1),
            kv_spec(COL_NAV, -1), kv_spec(COL_NAV, 0), kv_spec(COL_NAV, 1),
            pl.BlockSpec((ctx_len, NA_WIDTH), lambda b, s: (b, COL_NAK)),
            pl.BlockSpec((ctx_len, NA_WIDTH), lambda b, s: (b, COL_NAV)),
        ] + [bias_spec(kk) for kk in range(NA_STEP)],
        out_specs=pl.BlockSpec((NA_STEP * GRID_W, NA_WIDTH), lambda b, s: (b * steps + s, 0)),
        out_shape=jax.ShapeDtypeStruct((t, NA_WIDTH), BF16),
        scratch_shapes=[
            pltpu.VMEM((3 * blk, NA_WIDTH), BF16), pltpu.VMEM((3 * blk, NA_WIDTH), BF16),
            pltpu.VMEM((NA_STEP * NA_HEADS // 2, 2 * GRID_W, WIN_R * GRID_W + ctx_len), F32),
            pltpu.VMEM((NA_STEP * NA_HEADS // 2, 2 * GRID_W, WIN_R * GRID_W + ctx_len), BF16),
        ],
        compiler_params=_params(("parallel", "arbitrary")),
        name="na_attn",
    )(px, px, px, px, px, px, px, pc, pc, *([bias] * NA_STEP))


def _ctx_attn_kernel(q_ref, k_ref, v_ref, o_ref):
    n = q_ref.shape[0]
    lane = lax.broadcasted_iota(jnp.int32, (n, LANES), 1)
    scale = NA_HEAD_DIM ** -0.5
    for hp in range(NA_HEADS // 2):
        cols = slice(hp * LANES, (hp + 1) * LANES)
        q2 = q_ref[:, cols] * scale
        k2 = k_ref[:, cols]
        v2 = v_ref[:, cols]
        outs = []
        for sub in range(2):
            sel = (lane < NA_HEAD_DIM) if sub == 0 else (lane >= NA_HEAD_DIM)
            s = _dot_nt(jnp.where(sel, q2, jnp.zeros_like(q2)), k2)
            p = jnp.exp(s - jnp.max(s, axis=-1, keepdims=True))
            outs.append(_dot(p.astype(BF16), v2) / jnp.sum(p, axis=-1, keepdims=True))
        o_ref[:, cols] = jnp.where(lane < NA_HEAD_DIM, outs[0], outs[1]).astype(o_ref.dtype)


def _ctx_attention(pc, ctx_len):
    t = pc.shape[0]
    return pl.pallas_call(
        _ctx_attn_kernel,
        grid=(t // ctx_len,),
        in_specs=[pl.BlockSpec((ctx_len, NA_WIDTH), lambda b, col=col: (b, col))
                  for col in (COL_NAQ, COL_NAK, COL_NAV)],
        out_specs=pl.BlockSpec((ctx_len, NA_WIDTH), lambda b: (b, 0)),
        out_shape=jax.ShapeDtypeStruct((t, NA_WIDTH), BF16),
        compiler_params=_params(("parallel",)),
        name="ctx_attn",
    )(pc, pc, pc)


def _cmul(a, b):
    return a[0] * b[0] - a[1] * b[1], a[0] * b[1] + a[1] * b[0]


def _s5_tables(lam_re, lam_im, log_dt, b_re, b_im, c_re, c_im):
    L, G, S, H = S5_CHUNK, S5_GROUPS, S5_STATE, S5_GROUP
    lr = jnp.minimum(lam_re.astype(F32), S5_MAX_RE)
    li = lam_im.astype(F32)
    dt = jnp.exp(log_dt.astype(F32))[..., None]
    zr, zi = lr * dt, li * dt
    d = jnp.arange(L + 1, dtype=F32)[None, :, None, None]
    mag = jnp.exp(zr[:, None] * d)
    pw = (mag * jnp.cos(zi[:, None] * d), mag * jnp.sin(zi[:, None] * d))
    lb = (pw[0][:, 1], pw[1][:, 1])
    den = lr * lr + li * li
    ratio = (((lb[0] - 1.0) * lr + lb[1] * li) / den, (lb[1] * lr - (lb[0] - 1.0) * li) / den)
    bb = _cmul((ratio[0][..., None], ratio[1][..., None]), (b_re.astype(F32), b_im.astype(F32)))
    cc = (c_re.astype(F32), c_im.astype(F32))

    def kern(x):
        cl = _cmul((cc[0][x][None], cc[1][x][None]), (pw[0][x, :L, :, None, :], pw[1][x, :L, :, None, :]))
        return jnp.einsum('dghp,gpj->dghj', cl[0], bb[0][x]) - jnp.einsum('dghp,gpj->dghj', cl[1], bb[1][x])

    tt = jnp.arange(L)[None, :]
    ss = jnp.arange(L)[:, None]
    m_f = jnp.where((tt >= ss)[:, :, None, None, None], kern(0)[jnp.clip(tt - ss, 0, L - 1)], 0.0)
    m_b = jnp.where((ss >= tt)[:, :, None, None, None], kern(1)[jnp.clip(ss - tt, 0, L - 1)], 0.0)
    m = jnp.transpose(m_f + m_b, (2, 0, 4, 1, 3))

    def state_in(x, steps):
        p = (pw[0][x][steps][:, :, None, :], pw[1][x][steps][:, :, None, :])
        r = _cmul(p, (jnp.swapaxes(bb[0][x], 1, 2)[None], jnp.swapaxes(bb[1][x], 1, 2)[None]))
        return jnp.transpose(r[0], (1, 0, 2, 3)), jnp.transpose(r[1], (1, 0, 2, 3))

    def state_out(x, steps):
        p = (pw[0][x][steps][:, :, None, :], pw[1][x][steps][:, :, None, :])
        r = _cmul(p, (cc[0][x][None], cc[1][x][None]))
        return jnp.transpose(r[0], (1, 3, 0, 2)), jnp.transpose(r[1], (1, 3, 0, 2))

    steps = jnp.arange(L)
    wf = state_in(0, L - 1 - steps)
    wb = state_in(1, steps)
    vf = state_out(0, steps + 1)
    vb = state_out(1, L - steps)
    eye = jnp.eye(S5_QUAD_GROUPS, dtype=F32)
    Q, A = S5_QUADS, S5_QUAD_GROUPS

    def quad(a):
        return a.reshape((Q, A) + a.shape[1:])

    m_q = jnp.einsum('qasjth,ab->qsajtbh', quad(m), eye).reshape(Q, S5_QCOLS, S5_QCOLS)
    w_q = jnp.stack([jnp.einsum('qasjp,ab->qsajbp', quad(w), eye) for w in (wf[0], wf[1], wb[0], wb[1])],
                    axis=4).reshape(Q, S5_QCOLS, 4 * S5_QSTATE)
    v_q = jnp.stack([jnp.einsum('qbpth,ab->qbptah', quad(v), eye) for v in (vf[0], -vf[1], vb[0], -vb[1])],
                    axis=1).reshape(Q, 4 * S5_QSTATE, S5_QCOLS)
    a_re = pw[0][:, L].reshape(2, 1, G * S)
    a_im = pw[1][:, L].reshape(2, 1, G * S)
    return w_q.astype(BF16), m_q.astype(BF16), v_q.astype(BF16), a_re, a_im


def _s5_operand(u_ref):
    return jnp.concatenate([u_ref[:, s, :] for s in range(S5_CHUNK)], axis=1).astype(BF16)


def _s5_local_kernel(u_ref, w_ref, sfr_ref, sfi_ref, sbr_ref, sbi_ref):
    s = _dot(_s5_operand(u_ref), w_ref[0])
    n = S5_QSTATE
    sfr_ref[...] = s[:, 0:n]
    sfi_ref[...] = s[:, n:2 * n]
    sbr_ref[...] = s[:, 2 * n:3 * n]
    sbi_ref[...] = s[:, 3 * n:4 * n]


def _s5_local_states(u3, w_q):
    rows = u3.shape[0]
    tr = min(256, rows)
    assert rows % tr == 0
    out = jax.ShapeDtypeStruct((rows, S5_GROUPS * S5_STATE), F32)
    return pl.pallas_call(
        _s5_local_kernel,
        grid=(S5_QUADS, rows // tr),
        in_specs=[
            pl.BlockSpec((tr, S5_CHUNK, LANES), lambda q, i: (i, 0, q)),
            pl.BlockSpec((1, S5_QCOLS, 4 * S5_QSTATE), lambda q, i: (q, 0, 0)),
        ],
        out_specs=[pl.BlockSpec((tr, S5_QSTATE), lambda q, i: (i, q))] * 4,
        out_shape=[out] * 4,
        compiler_params=_params(("parallel", "parallel")),
        name="s5_local",
    )(u3, w_q)


def _s5_scan_kernel(sr_ref, si_ref, ar_ref, ai_ref, h0r_ref, h0i_ref, hr_ref, hi_ref, fr_ref, fi_ref,
                    hr_scr, hi_scr, *, reverse):
    i = pl.program_id(1)
    n_i = pl.num_programs(1)
    steps = sr_ref.shape[1]

    @pl.when(i == 0)
    def _():
        hr_scr[...] = h0r_ref[...]
        hi_scr[...] = h0i_ref[...]

    ar = ar_ref[...]
    ai = ai_ref[...]

    def body(j, carry):
        hr, hi = carry
        c = (steps - 1 - j) if reverse else j
        hr_ref[:, c, :] = hr
        hi_ref[:, c, :] = hi
        nr = ar * hr - ai * hi + sr_ref[:, c, :]
        ni = ar * hi + ai * hr + si_ref[:, c, :]
        return nr, ni

    hr, hi = lax.fori_loop(0, steps, body, (hr_scr[...], hi_scr[...]))
    hr_scr[...] = hr
    hi_scr[...] = hi

    @pl.when(i == n_i - 1)
    def _():
        fr_ref[...] = hr
        fi_ref[...] = hi


def _s5_chunk_scan(s_re, s_im, a_re, a_im, h0_re, h0_im, bsz, reverse):
    _, n_c, width = s_re.shape
    tr = min(64, n_c)
    tc = 512
    n_i = n_c // tr

    def rmap(j, i):
        return (0, (n_i - 1 - i) if reverse else i, j)

    big = jax.ShapeDtypeStruct((bsz, n_c, width), F32)
    small = jax.ShapeDtypeStruct((bsz, width), F32)
    return pl.pallas_call(
        functools.partial(_s5_scan_kernel, reverse=reverse),
        grid=(width // tc, n_i),
        in_specs=[
            pl.BlockSpec((bsz, tr, tc), rmap), pl.BlockSpec((bsz, tr, tc), rmap),
            pl.BlockSpec((1, tc), lambda j, i: (0, j)), pl.BlockSpec((1, tc), lambda j, i: (0, j)),
            pl.BlockSpec((bsz, tc), lambda j, i: (0, j)), pl.BlockSpec((bsz, tc), lambda j, i: (0, j)),
        ],
        out_specs=[
            pl.BlockSpec((bsz, tr, tc), rmap), pl.BlockSpec((bsz, tr, tc), rmap),
            pl.BlockSpec((bsz, tc), lambda j, i: (0, j)), pl.BlockSpec((bsz, tc), lambda j, i: (0, j)),
        ],
        out_shape=[big, big, small, small],
        scratch_shapes=[pltpu.VMEM((bsz, tc), F32), pltpu.VMEM((bsz, tc), F32)],
        compiler_params=_params(("parallel", "arbitrary")),
        name="s5_scan_rev" if reverse else "s5_scan_fwd",
    )(s_re, s_im, a_re, a_im, h0_re, h0_im)


def _s5_out_kernel(u_ref, m_ref, v_ref, hfr_ref, hfi_ref, hbr_ref, hbi_ref, y_ref):
    n = S5_QSTATE
    y = _dot(_s5_operand(u_ref), m_ref[0])
    for k, h_ref in enumerate((hfr_ref, hfi_ref, hbr_ref, hbi_ref)):
        y += _dot(h_ref[...].astype(BF16), v_ref[0, k * n:(k + 1) * n, :])
    for t in range(S5_CHUNK):
        y_ref[:, t, :] = y[:, t * LANES:(t + 1) * LANES]


def _s5_outputs(u3, m_q, v_q, h_states):
    rows = u3.shape[0]
    tr = min(256, rows)
    return pl.pallas_call(
        _s5_out_kernel,
        grid=(S5_QUADS, rows // tr),
        in_specs=[
            pl.BlockSpec((tr, S5_CHUNK, LANES), lambda q, i: (i, 0, q)),
            pl.BlockSpec((1, S5_QCOLS, S5_QCOLS), lambda q, i: (q, 0, 0)),
            pl.BlockSpec((1, 4 * S5_QSTATE, S5_QCOLS), lambda q, i: (q, 0, 0)),
        ] + [pl.BlockSpec((tr, S5_QSTATE), lambda q, i: (i, q))] * 4,
        out_specs=pl.BlockSpec((tr, S5_CHUNK, LANES), lambda q, i: (i, 0, q)),
        out_shape=jax.ShapeDtypeStruct((rows, S5_CHUNK, S5_WIDTH), F32),
        compiler_params=_params(("parallel", "parallel")),
        name="s5_out",
    )(u3, m_q, v_q, *h_states)


def _s5_branch(u, tables, seq_len, init, want_y):
    w_q, m_q, v_q, a_re, a_im = tables
    t = u.shape[0]
    bsz = t // seq_len
    n_c = seq_len // S5_CHUNK
    width = S5_GROUPS * S5_STATE
    u3 = u.reshape(t // S5_CHUNK, S5_CHUNK, S5_WIDTH)
    s_loc = [s.reshape(bsz, n_c, width) for s in _s5_local_states(u3, w_q)]
    hfr, hfi, ffr, ffi = _s5_chunk_scan(s_loc[0], s_loc[1], a_re[0], a_im[0], init[0], init[1], bsz, False)
    hbr, hbi, fbr, fbi = _s5_chunk_scan(s_loc[2], s_loc[3], a_re[1], a_im[1], init[2], init[3], bsz, True)
    y = None
    if want_y:
        h_states = [h.reshape(bsz * n_c, width) for h in (hfr, hfi, hbr, hbi)]
        y = _s5_outputs(u3, m_q, v_q, h_states).reshape(t, S5_WIDTH)
    return y, (ffr, ffi, fbr, fbi)


def _gelu_tanh(x):
    return 0.5 * x * (1.0 + jnp.tanh(math.sqrt(2.0 / math.pi) * (x + 0.044715 * (x * x * x))))


def _merge_kernel(x_ref, ym_ref, yn_ref, ys_ref, u_ref, gm_ref, gn_ref, gs_ref, d_ref, gw_ref, gb_ref,
                  wm_ref, wn_ref, ws_ref, wo_ref, gate_ref, npost_ref, o_ref):
    ys = ys_ref[...] + d_ref[...] * u_ref[...]
    g = _gelu_tanh(ys)
    gg = g * jax.nn.sigmoid(_dot(g.astype(BF16), gw_ref[...]) + gb_ref[...])
    y = jax.nn.sigmoid(gm_ref[...].astype(F32)) * _dot(ym_ref[...], wm_ref[...])
    y += jax.nn.sigmoid(gn_ref[...].astype(F32)) * _dot(yn_ref[...], wn_ref[...])
    y += jax.nn.sigmoid(gs_ref[...].astype(F32)) * _dot(gg.astype(BF16), ws_ref[...])
    out = _dot(y.astype(BF16), wo_ref[...])
    o_ref[...] = x_ref[...] + gate_ref[0] * _rms(out, npost_ref[...])


def _merge(x, ym, yn, ys, u, px, wts, gate, npost, rows_per_mod):
    t = x.shape[0]
    tm = min(512, rows_per_mod)
    per = rows_per_mod // tm
    s5_d, glu_w, glu_b, w_m, w_n, w_s, w_o = wts
    row = lambda i: (i, 0)
    const = lambda i: (0, 0)
    return pl.pallas_call(
        _merge_kernel,
        grid=(t // tm,),
        in_specs=[
            pl.BlockSpec((tm, D_MODEL), row),
            pl.BlockSpec((tm, M_WIDTH), row),
            pl.BlockSpec((tm, NA_WIDTH), row),
            pl.BlockSpec((tm, S5_WIDTH), row),
            pl.BlockSpec((tm, S5_WIDTH), row),
            pl.BlockSpec((tm, D_MODEL), lambda i: (i, COL_GATE)),
            pl.BlockSpec((tm, D_MODEL), lambda i: (i, COL_GATE + 1)),
            pl.BlockSpec((tm, D_MODEL), lambda i: (i, COL_GATE + 2)),
            pl.BlockSpec((1, S5_WIDTH), const),
            pl.BlockSpec((S5_WIDTH, S5_WIDTH), const),
            pl.BlockSpec((1, S5_WIDTH), const),
            pl.BlockSpec((M_WIDTH, D_MODEL), const),
            pl.BlockSpec((NA_WIDTH, D_MODEL), const),
            pl.BlockSpec((S5_WIDTH, D_MODEL), const),
            pl.BlockSpec((D_MODEL, D_MODEL), const),
            pl.BlockSpec((1, 1, D_MODEL), lambda i: (i // per, 0, 0)),
            pl.BlockSpec((1, D_MODEL), const),
        ],
        out_specs=pl.BlockSpec((tm, D_MODEL), row),
        out_shape=jax.ShapeDtypeStruct((t, D_MODEL), F32),
        compiler_params=_params(("parallel",)),
        name="merge",
    )(x, ym, yn, ys, u, px, px, px, s5_d, glu_w, glu_b, w_m, w_n, w_s, w_o, gate, npost)


def _swiglu(h, wg, wu, wd):
    a = _dot(h, wg)
    a = a * jax.nn.sigmoid(a) * _dot(h, wu)
    return _dot(a.astype(BF16), wd)


def _ffn_kernel(x_ref, g_ref, sh_ref, sc_ref, wg_ref, wu_ref, wd_ref, gate_ref, npost_ref,
                o_ref, hn_ref, acc_ref):
    j = pl.program_id(1)

    @pl.when(j == 0)
    def _():
        h = _rms(x_ref[...], g_ref[...]) * (1.0 + sc_ref[0]) + sh_ref[0]
        hn_ref[...] = h.astype(BF16)
        acc_ref[...] = jnp.zeros_like(acc_ref)

    acc_ref[...] += _swiglu(hn_ref[...], wg_ref[...], wu_ref[...], wd_ref[...])

    @pl.when(j == pl.num_programs(1) - 1)
    def _():
        o_ref[...] = x_ref[...] + gate_ref[0] * _rms(acc_ref[...], npost_ref[...])


def _ffn(x, g, shift, scale, w_gate, w_up, w_down, gate, npost, rows_per_mod):
    t = x.shape[0]
    ff = w_gate.shape[1]
    ff_tile = ff // 2
    tm = min(512, rows_per_mod)
    per = rows_per_mod // tm
    assert ff_tile % LANES == 0
    return pl.pallas_call(
        _ffn_kernel,
        grid=(t // tm, ff // ff_tile),
        in_specs=[
            pl.BlockSpec((tm, D_MODEL), lambda i, j: (i, 0)),
            pl.BlockSpec((1, D_MODEL), lambda i, j: (0, 0)),
            pl.BlockSpec((1, 1, D_MODEL), lambda i, j: (i // per, 0, 0)),
            pl.BlockSpec((1, 1, D_MODEL), lambda i, j: (i // per, 0, 0)),
            pl.BlockSpec((D_MODEL, ff_tile), lambda i, j: (0, j)),
            pl.BlockSpec((D_MODEL, ff_tile), lambda i, j: (0, j)),
            pl.BlockSpec((ff_tile, D_MODEL), lambda i, j: (j, 0)),
            pl.BlockSpec((1, 1, D_MODEL), lambda i, j: (i // per, 0, 0)),
            pl.BlockSpec((1, D_MODEL), lambda i, j: (0, 0)),
        ],
        out_specs=pl.BlockSpec((tm, D_MODEL), lambda i, j: (i, 0)),
        out_shape=jax.ShapeDtypeStruct((t, D_MODEL), F32),
        scratch_shapes=[pltpu.VMEM((tm, D_MODEL), BF16), pltpu.VMEM((tm, D_MODEL), F32)],
        compiler_params=_params(("parallel", "arbitrary")),
        name="ffn",
    )(x, g, shift, scale, w_gate, w_up, w_down, gate, npost)


def _moe_kernel(x_ref, g_ref, sh_ref, sc_ref, rh_ref, rl_ref, wg_ref, wu_ref, wd_ref, gate_ref, npost_ref,
                o_ref, hn_ref, acc_ref, comb_ref, rank_c_ref, rank_r_ref, mask_r_ref, xc_ref, yc_ref, cnt_ref):
    e = pl.program_id(1)
    j = pl.program_id(2)
    tm = x_ref.shape[0]
    cap = MOE_CAP

    @pl.when((e == 0) & (j == 0))
    def _():
        h = _rms(x_ref[...], g_ref[...]) * (1.0 + sc_ref[0]) + sh_ref[0]
        hi = h.astype(BF16)
        hn_ref[...] = hi
        acc_ref[...] = jnp.zeros_like(acc_ref)
        lo = (h - hi.astype(F32)).astype(BF16)
        logits = _dot(hi, rh_ref[...]) + _dot(lo, rh_ref[...]) + _dot(hi, rl_ref[...])
        lane = lax.broadcasted_iota(jnp.int32, logits.shape, 1)
        logits = jnp.where(lane < N_EXPERTS, logits, -jnp.inf)
        m1 = jnp.max(logits, axis=-1, keepdims=True)
        i1 = jnp.min(jnp.where(logits == m1, lane, LANES), axis=-1, keepdims=True)
        rest = jnp.where(lane == i1, -jnp.inf, logits)
        m2 = jnp.max(rest, axis=-1, keepdims=True)
        i2 = jnp.min(jnp.where(rest == m2, lane, LANES), axis=-1, keepdims=True)
        e2 = jnp.exp(m2 - m1)
        comb_ref[...] = jnp.where(lane == i1, 1.0 / (1.0 + e2), 0.0) + jnp.where(lane == i2, e2 / (1.0 + e2), 0.0)
        sel = jnp.where((lane == i1) | (lane == i2), 1.0, 0.0)
        ti = lax.broadcasted_iota(jnp.int32, (tm, tm), 0)
        si = lax.broadcasted_iota(jnp.int32, (tm, tm), 1)
        before = jnp.where(si < ti, 1.0, 0.0).astype(BF16)
        rank_c_ref[...] = _dot(before, sel.astype(BF16))
        sel_t = sel.T
        after = jnp.where(ti < si, 1.0, 0.0).astype(BF16)
        rank_r_ref[...] = _dot(sel_t.astype(BF16), after)
        mask_r_ref[...] = sel_t
        counts = jnp.sum(sel_t, axis=1, keepdims=True)
        for k in range(N_EXPERTS):
            cnt_ref[k] = jnp.sum(counts[k:k + 1, :]).astype(jnp.int32)

    n_pass = (cnt_ref[e] + (cap - 1)) // cap

    @pl.when(j == 0)
    def _():
        rank_row = rank_r_ref[pl.ds(e, 1), :]
        mask_row = mask_r_ref[pl.ds(e, 1), :]
        slot = lax.broadcasted_iota(jnp.int32, (cap, tm), 0).astype(F32)

        def compact(p, _):
            base = (p * cap).astype(F32)
            onehot = jnp.where((rank_row - base == slot) & (mask_row > 0.0), 1.0, 0.0).astype(BF16)
            rows = pl.ds(pl.multiple_of(p * cap, 16), cap)
            xc_ref[rows, :] = _dot(onehot, hn_ref[...]).astype(BF16)
            yc_ref[rows, :] = jnp.zeros((cap, D_MODEL), F32)
            return 0

        lax.fori_loop(0, n_pass, compact, 0)

    def expert(start, size):
        rows = pl.ds(pl.multiple_of(start, MOE_GRAIN), size)
        yc_ref[rows, :] += _swiglu(xc_ref[rows, :], wg_ref[0], wu_ref[0], wd_ref[0])

    n_tok = cnt_ref[e]
    n_full = n_tok // cap
    rem_blocks = (n_tok - n_full * cap + (MOE_GRAIN - 1)) // MOE_GRAIN
    n_full = n_full + (rem_blocks == cap // MOE_GRAIN).astype(jnp.int32)

    def full_pass(p, _):
        expert(p * cap, cap)
        return 0

    lax.fori_loop(0, n_full, full_pass, 0)
    for blocks in range(1, cap // MOE_GRAIN):
        @pl.when(rem_blocks == blocks)
        def _(blocks=blocks):
            expert(n_full * cap, blocks * MOE_GRAIN)

    @pl.when(j == pl.num_programs(2) - 1)
    def _():
        lane = lax.broadcasted_iota(jnp.int32, (tm, LANES), 1)
        pick = lane == e
        rank_col = jnp.sum(jnp.where(pick, rank_c_ref[...], 0.0), axis=-1, keepdims=True)
        w_col = jnp.sum(jnp.where(pick, comb_ref[...], 0.0), axis=-1, keepdims=True)
        slot = lax.broadcasted_iota(jnp.int32, (tm, cap), 1).astype(F32)

        def expand(p, _):
            base = (p * cap).astype(F32)
            onehot = jnp.where(rank_col - base == slot, 1.0, 0.0).astype(BF16)
            rows = pl.ds(pl.multiple_of(p * cap, 16), cap)
            acc_ref[...] += w_col * _dot(onehot, yc_ref[rows, :].astype(BF16))
            return 0

        lax.fori_loop(0, n_pass, expand, 0)

    @pl.when((e == N_EXPERTS - 1) & (j == pl.num_programs(2) - 1))
    def _():
        o_ref[...] = x_ref[...] + gate_ref[0] * _rms(acc_ref[...], npost_ref[...])


def _moe(x, g, shift, scale, r_hi, r_lo, w_gate, w_up, w_down, gate, npost, rows_per_mod):
    t = x.shape[0]
    n_e, _, ff = w_gate.shape
    assert n_e == N_EXPERTS
    ff_tile = ff // 4
    tm = min(MOE_TILE, rows_per_mod)
    per = rows_per_mod // tm
    passes = -(-tm // MOE_CAP)
    assert ff_tile % LANES == 0 and MOE_CAP % MOE_GRAIN == 0 and MOE_GRAIN % 16 == 0
    return pl.pallas_call(
        _moe_kernel,
        grid=(t // tm, n_e, ff // ff_tile),
        in_specs=[
            pl.BlockSpec((tm, D_MODEL), lambda i, e, j: (i, 0)),
            pl.BlockSpec((1, D_MODEL), lambda i, e, j: (0, 0)),
            pl.BlockSpec((1, 1, D_MODEL), lambda i, e, j: (i // per, 0, 0)),
            pl.BlockSpec((1, 1, D_MODEL), lambda i, e, j: (i // per, 0, 0)),
            pl.BlockSpec((D_MODEL, LANES), lambda i, e, j: (0, 0)),
            pl.BlockSpec((D_MODEL, LANES), lambda i, e, j: (0, 0)),
            pl.BlockSpec((1, D_MODEL, ff_tile), lambda i, e, j: (e, 0, j)),
            pl.BlockSpec((1, D_MODEL, ff_tile), lambda i, e, j: (e, 0, j)),
            pl.BlockSpec((1, ff_tile, D_MODEL), lambda i, e, j: (e, j, 0)),
            pl.BlockSpec((1, 1, D_MODEL), lambda i, e, j: (i // per, 0, 0)),
            pl.BlockSpec((1, D_MODEL), lambda i, e, j: (0, 0)),
        ],
        out_specs=pl.BlockSpec((tm, D_MODEL), lambda i, e, j: (i, 0)),
        out_shape=jax.ShapeDtypeStruct((t, D_MODEL), F32),
        scratch_shapes=[
            pltpu.VMEM((tm, D_MODEL), BF16),
            pltpu.VMEM((tm, D_MODEL), F32),
            pltpu.VMEM((tm, LANES), F32),
            pltpu.VMEM((tm, LANES), F32),
            pltpu.VMEM((LANES, tm), F32),
            pltpu.VMEM((LANES, tm), F32),
            pltpu.VMEM((passes * MOE_CAP, D_MODEL), BF16),
            pltpu.VMEM((passes * MOE_CAP, D_MODEL), F32),
            pltpu.SMEM((N_EXPERTS,), jnp.int32),
        ],
        compiler_params=_params(("parallel", "arbitrary", "arbitrary")),
        name="moe",
    )(x, g, shift, scale, r_hi, r_lo, w_gate, w_up, w_down, gate, npost)


def _reorder_w_in(w):
    o = 4 * M_WIDTH
    gates = jnp.pad(w[:, o:o + M_GATES], ((0, 0), (0, LANES - M_GATES)))
    na = w[:, o + M_GATES:o + M_GATES + 3 * NA_WIDTH]
    u0 = o + M_GATES + 3 * NA_WIDTH
    main = jnp.concatenate([w[:, :o], w[:, u0 + S5_WIDTH:], na], axis=1)
    return main.astype(BF16), gates.astype(BF16), w[:, u0:u0 + S5_WIDTH].astype(BF16)


def _row(v):
    return v.reshape(1, -1).astype(F32)


def _token_mixers(x, xc, mx, mc, lp, rope, seq_len, ctx_len, ctx_out):
    bsz = x.shape[0] // seq_len
    w_main, w_gates, w_u = _reorder_w_in(lp['w_in'])
    g_pre = _row(lp['norm_mix_pre'])
    px, gx, ux = _inproj(x, g_pre, mx[0], mx[1], w_main, w_gates, w_u, seq_len)
    pc, gc, uc = _inproj(xc, g_pre, mc[0], mc[1], w_main, w_gates, w_u, xc.shape[0])

    conv_w = lp['m_conv_w'].astype(F32)
    conv_b = _row(lp['m_conv_b'])
    gate_b = jnp.pad(_row(lp['m_gate_b']), ((0, 0), (0, LANES - M_GATES)))
    m_norm = _row(lp['m_norm'])
    qk_c = _qkconv(pc, conv_w, conv_b, rope[0][:ctx_len], rope[1][:ctx_len], ctx_len, False)
    qk_x = _qkconv(px, conv_w, conv_b, rope[0], rope[1], seq_len, True)
    zero = _mlstm_zero_state(bsz)
    ym_c, fin_f, fin_b = _mlstm_bidir(qk_c, pc, gc, gate_b, m_norm, zero, zero, ctx_len)
    ym_x, _, _ = _mlstm_bidir(qk_x, px, gx, gate_b, m_norm, fin_f, fin_b, seq_len)

    yn_x = _na_attention(px, pc, _na_bias_table(lp['na_rpb']), seq_len, ctx_len)

    tables = _s5_tables(lp['s5_lam_re'], lp['s5_lam_im'], lp['s5_log_dt'], lp['s5_b_re'], lp['s5_b_im'],
                        lp['s5_c_re'], lp['s5_c_im'])
    zs = jnp.zeros((bsz, S5_GROUPS * S5_STATE), F32)
    ys_c, fin_s = _s5_branch(uc, tables, ctx_len, (zs, zs, zs, zs), ctx_out)
    ys_x, _ = _s5_branch(ux, tables, seq_len, fin_s, True)

    wts = (_row(lp['s5_d']), lp['s5_glu_w'].astype(BF16), _row(lp['s5_glu_b']),
           lp['w_branch_m'].astype(BF16), lp['w_branch_na'].astype(BF16), lp['w_branch_s5'].astype(BF16),
           lp['w_out'].astype(BF16))
    g_post = _row(lp['norm_mix_post'])
    x = _merge(x, ym_x, yn_x, ys_x, ux, px, wts, mx[2], g_post, seq_len)
    if ctx_out:
        yn_c = _ctx_attention(pc, ctx_len)
        xc = _merge(xc, ym_c, yn_c, ys_c, uc, pc, wts, mc[2], g_post, xc.shape[0])
    return x, xc


def kernel(x, c, ctx, c_ctx, ada_w, ada_b, norm_mix_pre, norm_mix_post, norm_ffn_pre, norm_ffn_post, w_in, m_gate_b, m_conv_w, m_conv_b, m_norm, na_rpb, s5_lam_re, s5_lam_im, s5_log_dt, s5_b_re, s5_b_im, s5_c_re, s5_c_im, s5_d, s5_glu_w, s5_glu_b, w_branch_m, w_branch_na, w_branch_s5, w_out, ffn_w_gate, ffn_w_up, ffn_w_down, moe_router, moe_w_gate, moe_w_up, moe_w_down):
    bsz, seq_len, d = x.shape
    ctx_len = ctx.shape[1]
    depth = w_in.shape[0]
    assert d == D_MODEL and seq_len % (GRID_W * NA_ROWS) == 0 and ctx_len % M_CHUNK == 0
    rope = _rope_tables(seq_len)
    xf = x.reshape(bsz * seq_len, d)
    xc = ctx.reshape(bsz * ctx_len, d)
    cvecs = jnp.zeros((16, d), F32).at[:bsz].set(c).at[bsz].set(c_ctx)
    for l in range(depth):
        last = l == depth - 1
        mod = _adaln(cvecs, ada_w[l].astype(BF16), _row(ada_b[l]))
        mx = [mod[:bsz, k * d:(k + 1) * d].reshape(bsz, 1, d) for k in range(N_MOD)]
        mc = [mod[bsz:bsz + 1, k * d:(k + 1) * d].reshape(1, 1, d) for k in range(N_MOD)]
        lp = {
            'norm_mix_pre': norm_mix_pre[l], 'norm_mix_post': norm_mix_post[l],
            'w_in': w_in[l], 'm_gate_b': m_gate_b[l], 'm_conv_w': m_conv_w[l], 'm_conv_b': m_conv_b[l],
            'm_norm': m_norm[l], 'na_rpb': na_rpb[l], 's5_lam_re': s5_lam_re[l], 's5_lam_im': s5_lam_im[l],
            's5_log_dt': s5_log_dt[l], 's5_b_re': s5_b_re[l], 's5_b_im': s5_b_im[l], 's5_c_re': s5_c_re[l],
            's5_c_im': s5_c_im[l], 's5_d': s5_d[l], 's5_glu_w': s5_glu_w[l], 's5_glu_b': s5_glu_b[l],
            'w_branch_m': w_branch_m[l], 'w_branch_na': w_branch_na[l], 'w_branch_s5': w_branch_s5[l],
            'w_out': w_out[l],
        }
        xf, xc = _token_mixers(xf, xc, mx, mc, lp, rope, seq_len, ctx_len, not last)
        j = l // 2
        g_pre, g_post = _row(norm_ffn_pre[l]), _row(norm_ffn_post[l])
        if l % 2 == 0:
            wg, wu, wd = ffn_w_gate[j].astype(BF16), ffn_w_up[j].astype(BF16), ffn_w_down[j].astype(BF16)

            def channel(h, m, rows):
                return _ffn(h, g_pre, m[3], m[4], wg, wu, wd, m[5], g_post, rows)
        else:
            wg, wu, wd = moe_w_gate[j].astype(BF16), moe_w_up[j].astype(BF16), moe_w_down[j].astype(BF16)
            router = jnp.pad(moe_router[j].astype(F32), ((0, 0), (0, LANES - N_EXPERTS)))
            r_hi = router.astype(BF16)
            r_lo = (router - r_hi.astype(F32)).astype(BF16)

            def channel(h, m, rows):
                return _moe(h, g_pre, m[3], m[4], r_hi, r_lo, wg, wu, wd, m[5], g_post, rows)
        xf = channel(xf, mx, seq_len)
        if not last:
            xc = channel(xc, mc, xc.shape[0])
    return xf.reshape(bsz, seq_len, d)
```

```python
import functools
import math

import jax
import jax.numpy as jnp
from jax import lax
from jax.experimental import pallas as pl
from jax.experimental.pallas import tpu as pltpu

F32 = jnp.float32
BF16 = jnp.bfloat16

D_MODEL = 1024
EPS = 1e-6
N_MOD = 6
GRID_W = 64

M_HEADS = 4
M_HEAD_DIM = 128
M_WIDTH = M_HEADS * M_HEAD_DIM
M_GATES = 4 * M_HEADS
M_CHUNK = 128
ROPE_BASE = 10000.0

NA_HEADS = 8
NA_HEAD_DIM = 64
NA_WIDTH = NA_HEADS * NA_HEAD_DIM
WIN_R = 8
WIN_C = 16
NEG_BIG = -1e30

S5_GROUP = 16
S5_WIDTH = 512
S5_GROUPS = S5_WIDTH // S5_GROUP
S5_STATE = 64
S5_MAX_RE = -1e-4

LANES = 128
VMEM_LIMIT = 52 * 1024 * 1024

S5_CHUNK = 16
S5_QUAD_GROUPS = LANES // S5_GROUP
S5_QUADS = S5_GROUPS // S5_QUAD_GROUPS
S5_QCOLS = S5_CHUNK * LANES
S5_QSTATE = S5_QUAD_GROUPS * S5_STATE

N_EXPERTS = 8
TOP_K = 2
MOE_TILE = 1024
MOE_CAP = 256
MOE_GRAIN = 64

COL_QK, COL_V, COL_O, COL_NAQ, COL_NAK, COL_NAV = 0, 2, 3, 10, 11, 12
COL_GATE = 2
IN_MAIN = 6656


def _params(sem):
    return pltpu.CompilerParams(dimension_semantics=sem, vmem_limit_bytes=VMEM_LIMIT)


def _dot(a, b):
    return jnp.dot(a, b, preferred_element_type=F32)


def _dot_nt(a, b):
    return lax.dot_general(a, b, (((1,), (1,)), ((), ())), preferred_element_type=F32)


def _split3(x):
    hi = x.astype(BF16)
    r1 = x - hi.astype(F32)
    mid = r1.astype(BF16)
    lo = (r1 - mid.astype(F32)).astype(BF16)
    return hi, mid, lo


def _rms(x, g):
    return x * lax.rsqrt(jnp.mean(x * x, axis=-1, keepdims=True) + EPS) * g


def _inproj_kernel(x_ref, g_ref, sh_ref, sc_ref, w_ref, wg_ref, wu_ref, o_ref, og_ref, ou_ref, hn_ref):
    @pl.when(pl.program_id(1) == 0)
    def _():
        h = _rms(x_ref[...], g_ref[...]) * (1.0 + sc_ref[0]) + sh_ref[0]
        hb = h.astype(BF16)
        hn_ref[...] = hb
        og_ref[...] = _dot(hb, wg_ref[...])
        ou_ref[...] = _dot(hb, wu_ref[...])

    o_ref[...] = _dot(hn_ref[...], w_ref[...]).astype(o_ref.dtype)


def _inproj(x, g, shift, scale, w_main, w_gates, w_u, rows_per_mod):
    t = x.shape[0]
    tm = min(1024, rows_per_mod)
    tn = IN_MAIN // 4
    assert rows_per_mod % tm == 0 and t % tm == 0 and tn % LANES == 0
    per = rows_per_mod // tm
    return pl.pallas_call(
        _inproj_kernel,
        grid=(t // tm, IN_MAIN // tn),
        in_specs=[
            pl.BlockSpec((tm, D_MODEL), lambda i, j: (i, 0)),
            pl.BlockSpec((1, D_MODEL), lambda i, j: (0, 0)),
            pl.BlockSpec((1, 1, D_MODEL), lambda i, j: (i // per, 0, 0)),
            pl.BlockSpec((1, 1, D_MODEL), lambda i, j: (i // per, 0, 0)),
            pl.BlockSpec((D_MODEL, tn), lambda i, j: (0, j)),
            pl.BlockSpec((D_MODEL, LANES), lambda i, j: (0, 0)),
            pl.BlockSpec((D_MODEL, S5_WIDTH), lambda i, j: (0, 0)),
        ],
        out_specs=[
            pl.BlockSpec((tm, tn), lambda i, j: (i, j)),
            pl.BlockSpec((tm, LANES), lambda i, j: (i, 0)),
            pl.BlockSpec((tm, S5_WIDTH), lambda i, j: (i, 0)),
        ],
        out_shape=[jax.ShapeDtypeStruct((t, IN_MAIN), BF16), jax.ShapeDtypeStruct((t, LANES), F32),
                   jax.ShapeDtypeStruct((t, S5_WIDTH), F32)],
        scratch_shapes=[pltpu.VMEM((tm, D_MODEL), BF16)],
        compiler_params=_params(("parallel", "arbitrary")),
        name="inproj",
    )(x, g, shift, scale, w_main, w_gates, w_u)


def _adaln_kernel(c_ref, w_ref, b_ref, o_ref):
    c = c_ref[...]
    s = c * jax.nn.sigmoid(c)
    o_ref[...] = _dot(s.astype(BF16), w_ref[...]) + b_ref[...]


def _adaln(cvecs, w, b):
    n = w.shape[1]
    tn = 1024
    return pl.pallas_call(
        _adaln_kernel,
        grid=(n // tn,),
        in_specs=[
            pl.BlockSpec(cvecs.shape, lambda j: (0, 0)),
            pl.BlockSpec((D_MODEL, tn), lambda j: (0, j)),
            pl.BlockSpec((1, tn), lambda j: (0, j)),
        ],
        out_specs=pl.BlockSpec((cvecs.shape[0], tn), lambda j: (0, j)),
        out_shape=jax.ShapeDtypeStruct((cvecs.shape[0], n), F32),
        compiler_params=_params(("parallel",)),
        name="adaln",
    )(cvecs, w, b)


HALO = 16


def _swap32(x):
    lane = lax.broadcasted_iota(jnp.int32, x.shape, 1)
    fwd = pltpu.roll(x, 96, 1)
    bwd = pltpu.roll(x, 32, 1)
    return jnp.where((lane % 64) < 32, fwd, bwd)


def _qkconv_kernel(x_ref, pv_ref, nx_ref, w_ref, b_ref, cos_ref, sin_ref, o_ref, *, tiles_per_seq, rope):
    i = pl.program_id(0)
    tm = x_ref.shape[0]
    x = x_ref[...].astype(F32)
    first = (i % tiles_per_seq) == 0
    last = (i % tiles_per_seq) == tiles_per_seq - 1
    prev_row = jnp.where(first, 0.0, pv_ref[HALO - 1:HALO, :].astype(F32))
    next_row = jnp.where(last, 0.0, nx_ref[0:1, :].astype(F32))
    row = lax.broadcasted_iota(jnp.int32, x.shape, 0)
    xp = jnp.where(row == 0, prev_row, pltpu.roll(x, 1, 0))
    xn = jnp.where(row == tm - 1, next_row, pltpu.roll(x, tm - 1, 0))
    y = xp * w_ref[0:1, :] + x * w_ref[1:2, :] + xn * w_ref[2:3, :] + b_ref[...]
    y = y * jax.nn.sigmoid(y)
    kscale = M_HEAD_DIM ** -0.5
    for hh in range(2 * M_HEADS):
        ys = y[:, hh * LANES:(hh + 1) * LANES]
        if rope:
            ys = ys * cos_ref[...] + _swap32(ys) * sin_ref[...]
        if hh >= M_HEADS:
            ys = ys * kscale
        o_ref[:, hh * LANES:(hh + 1) * LANES] = ys.astype(o_ref.dtype)


def _qkconv(px, conv_w, conv_b, cos_t, sin_t, seq_len, rope):
    t = px.shape[0]
    tm = min(512, seq_len)
    assert seq_len % tm == 0
    tiles = seq_len // tm
    hb = tm // HALO
    nh = t // HALO
    return pl.pallas_call(
        functools.partial(_qkconv_kernel, tiles_per_seq=tiles, rope=rope),
        grid=(t // tm,),
        in_specs=[
            pl.BlockSpec((tm, 2 * M_WIDTH), lambda i: (i, 0)),
            pl.BlockSpec((HALO, 2 * M_WIDTH), lambda i: (jnp.maximum(i * hb - 1, 0), 0)),
            pl.BlockSpec((HALO, 2 * M_WIDTH), lambda i: (jnp.minimum((i + 1) * hb, nh - 1), 0)),
            pl.BlockSpec((3, 2 * M_WIDTH), lambda i: (0, 0)),
            pl.BlockSpec((1, 2 * M_WIDTH), lambda i: (0, 0)),
            pl.BlockSpec((tm, LANES), lambda i: (i % tiles, 0)),
            pl.BlockSpec((tm, LANES), lambda i: (i % tiles, 0)),
        ],
        out_specs=pl.BlockSpec((tm, 2 * M_WIDTH), lambda i: (i, 0)),
        out_shape=jax.ShapeDtypeStruct((t, 2 * M_WIDTH), BF16),
        compiler_params=_params(("parallel",)),
        name="qkconv",
    )(px, px, px, conv_w, conv_b, cos_t, sin_t)


def _rope_tables(n):
    pos = jnp.arange(n, dtype=jnp.int32)
    row = (pos // GRID_W).astype(F32)
    col = (pos % GRID_W).astype(F32)
    n_freq = M_HEAD_DIM // 4
    inv = ROPE_BASE ** (-jnp.arange(n_freq, dtype=F32) / n_freq)
    ar, ac = row[:, None] * inv, col[:, None] * inv
    cos_t = jnp.concatenate([jnp.cos(ar), jnp.cos(ar), jnp.cos(ac), jnp.cos(ac)], axis=-1)
    sin_t = jnp.concatenate([-jnp.sin(ar), jnp.sin(ar), -jnp.sin(ac), jnp.sin(ac)], axis=-1)
    return cos_t, sin_t


def _mlstm_kernel(*refs, reverse, final, nb):
    if final:
        (qk_ref, v_ref, g_ref, gb_ref, c0_ref, n0_ref, m0_ref, hp_ref, op_ref, nw_ref,
         out_ref, cf_ref, nf_ref, mf_ref, c_scr, n_scr, m_scr, qk_scr, qc_scr, qn_scr, s_scr, vt_scr, vu_scr, dec_scr) = refs
    else:
        (qk_ref, v_ref, g_ref, gb_ref, c0_ref, n0_ref, m0_ref,
         out_ref, cf_ref, nf_ref, mf_ref, c_scr, n_scr, m_scr, qk_scr, qc_scr, qn_scr, s_scr, vt_scr, vu_scr, dec_scr) = refs
    c_idx = pl.program_id(1)
    n_c = pl.num_programs(1)
    L = M_CHUNK

    @pl.when(c_idx == 0)
    def _():
        c_scr[...] = c0_ref[...]
        n_scr[...] = n0_ref[...]
        m_scr[...] = m0_ref[...]

    ti = lax.broadcasted_iota(jnp.int32, (L, L), 0)
    si = lax.broadcasted_iota(jnp.int32, (L, L), 1)
    keep = (si >= ti) if reverse else (si <= ti)
    tri = jnp.where(keep, 1.0, 0.0).astype(BF16)
    i_off = 2 * M_HEADS if reverse else 0
    f_off = i_off + M_HEADS
    end = 0 if reverse else L - 1

    units = [(a, b) for a in range(nb) for b in range(M_HEADS)]

    def head(ref, gi, h, base=0):
        return ref[gi, :, base + h * LANES:base + (h + 1) * LANES]

    keep_t = (ti >= si) if reverse else (ti <= si)

    for u, (gi, h) in enumerate(units):
        q = head(qk_ref, gi, h)
        qk_scr[u] = _dot_nt(head(qk_ref, gi, h, M_WIDTH), q)
        qc_scr[u] = _dot_nt(c_scr[gi, h].astype(BF16), q)
        qn_scr[u] = _dot_nt(jnp.broadcast_to(n_scr[gi, h], (8, LANES)).astype(BF16), q)

    for u, (gi, h) in enumerate(units):
        if h == 0:
            g = g_ref[gi] + gb_ref[...]
            f_hi, f_mid, f_lo = _split3(jax.nn.log_sigmoid(g))
            bsum = _dot(tri, f_hi) + _dot(tri, f_mid) + _dot(tri, f_lo)
            g_t = g.T
            b_t = bsum.T
        m_prev = m_scr[gi, h][:, 0:1]
        b_row = b_t[f_off + h:f_off + h + 1, :]
        i_row = g_t[i_off + h:i_off + h + 1, :]
        b_end = b_row[:, end:end + 1]
        d_col = g[:, i_off + h:i_off + h + 1] - bsum[:, f_off + h:f_off + h + 1]

        log_w = jnp.where(keep_t, b_row + d_col, -jnp.inf)
        carry_log = b_row + m_prev
        m_t = jnp.maximum(carry_log, jnp.max(log_w, axis=0, keepdims=True))
        s_t = qk_scr[u] * jnp.exp(log_w - m_t)
        c_scale = jnp.exp(carry_log - m_t)
        den = jnp.sum(s_t, axis=0, keepdims=True) + c_scale * qn_scr[u][0:1, :]
        inv = 1.0 / jnp.maximum(jnp.abs(den), jnp.exp(-m_t))
        s_scr[u] = (s_t * inv).astype(BF16)
        qc_scr[u] = qc_scr[u] * (c_scale * inv)

        m_new = jnp.maximum(b_end + m_prev, jnp.max(b_end - b_row + i_row, axis=-1, keepdims=True))
        u_row = jnp.exp(b_end - b_row + i_row - m_new)
        decay = jnp.exp(b_end + m_prev - m_new)
        v_t = head(v_ref, gi, h).astype(F32).T
        vt_scr[u] = v_t.astype(BF16)
        vu_scr[u] = (v_t * u_row).astype(BF16)
        u8 = jnp.broadcast_to(u_row, (8, L)).astype(BF16)
        n_scr[gi, h] = decay * n_scr[gi, h] + _dot(u8, head(qk_ref, gi, h, M_WIDTH))[0:1, :]
        m_scr[gi, h] = jnp.broadcast_to(m_new, (1, LANES))
        dec_scr[u] = jnp.broadcast_to(decay, (1, LANES))

    for u, (gi, h) in enumerate(units):
        qc_scr[u] = _dot(vt_scr[u], s_scr[u]) + qc_scr[u]
        c_scr[gi, h] = dec_scr[u][:, 0:1] * c_scr[gi, h] + _dot(vu_scr[u], head(qk_ref, gi, h, M_WIDTH))

    for u, (gi, h) in enumerate(units):
        rows = slice(h * M_HEAD_DIM, (h + 1) * M_HEAD_DIM)
        if final:
            hs = qc_scr[u] + hp_ref[gi, rows, :]
            mu = jnp.mean(hs, axis=0, keepdims=True)
            var = jnp.mean(jnp.square(hs - mu), axis=0, keepdims=True)
            hn = ((hs - mu) * lax.rsqrt(var + EPS)).T
            y = hn * nw_ref[:, h * LANES:(h + 1) * LANES] * jax.nn.sigmoid(head(op_ref, gi, h).astype(F32))
            out_ref[gi, :, h * LANES:(h + 1) * LANES] = y.astype(out_ref.dtype)
        else:
            out_ref[gi, rows, :] = qc_scr[u]

    @pl.when(c_idx == n_c - 1)
    def _():
        cf_ref[...] = c_scr[...]
        nf_ref[...] = n_scr[...]
        mf_ref[...] = m_scr[...]


def _mlstm_scan(qk, px, gates, gate_b, state, seq_len, reverse, hprev=None, norm_w=None):
    t = qk.shape[0]
    bsz = t // seq_len
    n_c = seq_len // M_CHUNK
    final = hprev is not None
    nb = 4 if bsz % 4 == 0 else (2 if bsz % 2 == 0 else 1)
    qk, px, gates = (a.reshape(bsz, seq_len, a.shape[1]) for a in (qk, px, gates))

    def chunk(c):
        return (n_c - 1 - c) if reverse else c

    state_specs = [
        pl.BlockSpec((nb, M_HEADS, M_HEAD_DIM, M_HEAD_DIM), lambda b, c: (b, 0, 0, 0)),
        pl.BlockSpec((nb, M_HEADS, 1, LANES), lambda b, c: (b, 0, 0, 0)),
        pl.BlockSpec((nb, M_HEADS, 1, LANES), lambda b, c: (b, 0, 0, 0)),
    ]
    in_specs = [
        pl.BlockSpec((nb, M_CHUNK, 2 * M_WIDTH), lambda b, c: (b, chunk(c), 0)),
        pl.BlockSpec((nb, M_CHUNK, M_WIDTH), lambda b, c: (b, chunk(c), COL_V)),
        pl.BlockSpec((nb, M_CHUNK, LANES), lambda b, c: (b, chunk(c), 0)),
        pl.BlockSpec((1, LANES), lambda b, c: (0, 0)),
    ] + state_specs
    args = [qk, px, gates, gate_b, *state]
    t_spec = pl.BlockSpec((nb, M_WIDTH, M_CHUNK), lambda b, c: (b, 0, chunk(c)))
    if final:
        in_specs += [
            t_spec,
            pl.BlockSpec((nb, M_CHUNK, M_WIDTH), lambda b, c: (b, chunk(c), COL_O)),
            pl.BlockSpec((1, M_WIDTH), lambda b, c: (0, 0)),
        ]
        args += [hprev, px, norm_w]
        out_spec = pl.BlockSpec((nb, M_CHUNK, M_WIDTH), lambda b, c: (b, chunk(c), 0))
        out_shape = jax.ShapeDtypeStruct((bsz, seq_len, M_WIDTH), BF16)
    else:
        out_spec = t_spec
        out_shape = jax.ShapeDtypeStruct((bsz, M_WIDTH, seq_len), F32)
    units = nb * M_HEADS
    out, *fin = pl.pallas_call(
        functools.partial(_mlstm_kernel, reverse=reverse, final=final, nb=nb),
        grid=(bsz // nb, n_c),
        in_specs=in_specs,
        out_specs=[out_spec] + state_specs,
        out_shape=[
            out_shape,
            jax.ShapeDtypeStruct((bsz, M_HEADS, M_HEAD_DIM, M_HEAD_DIM), F32),
            jax.ShapeDtypeStruct((bsz, M_HEADS, 1, LANES), F32),
            jax.ShapeDtypeStruct((bsz, M_HEADS, 1, LANES), F32),
        ],
        scratch_shapes=[
            pltpu.VMEM((nb, M_HEADS, M_HEAD_DIM, M_HEAD_DIM), F32),
            pltpu.VMEM((nb, M_HEADS, 1, LANES), F32),
            pltpu.VMEM((nb, M_HEADS, 1, LANES), F32),
            pltpu.VMEM((units, M_CHUNK, M_CHUNK), F32),
            pltpu.VMEM((units, M_HEAD_DIM, M_CHUNK), F32),
            pltpu.VMEM((units, 8, M_CHUNK), F32),
            pltpu.VMEM((units, M_CHUNK, M_CHUNK), BF16),
            pltpu.VMEM((units, M_HEAD_DIM, M_CHUNK), BF16),
            pltpu.VMEM((units, M_HEAD_DIM, M_CHUNK), BF16),
            pltpu.VMEM((units, 1, LANES), F32),
        ],
        compiler_params=_params(("parallel", "arbitrary")),
        name="mlstm_rev" if reverse else "mlstm_fwd",
    )(*args)
    return [out.reshape(t, M_WIDTH) if final else out] + fin


def _mlstm_zero_state(bsz):
    return (jnp.zeros((bsz, M_HEADS, M_HEAD_DIM, M_HEAD_DIM), F32),
            jnp.zeros((bsz, M_HEADS, 1, LANES), F32),
            jnp.zeros((bsz, M_HEADS, 1, LANES), F32))


def _mlstm_bidir(qk, px, gates, gate_b, norm_w, state_f, state_b, seq_len):
    h_f, *fin_f = _mlstm_scan(qk, px, gates, gate_b, state_f, seq_len, False)
    y, *fin_b = _mlstm_scan(qk, px, gates, gate_b, state_b, seq_len, True, hprev=h_f, norm_w=norm_w)
    return y, tuple(fin_f), tuple(fin_b)


NA_ROWS = 8
NA_STEP = 4


def _na_kernel(q_ref, kp_ref, kc_ref, kn_ref, vp_ref, vc_ref, vn_ref, kx_ref, vx_ref, *rest, rows):
    bias_refs = rest[:NA_STEP]
    o_ref, kwin, vwin, s_scr, p_scr = rest[NA_STEP:]
    step = pl.program_id(1)
    blk = NA_ROWS * GRID_W

    @pl.when(step % (NA_ROWS // NA_STEP) == 0)
    def _():
        kwin[0:blk, :] = kp_ref[...]
        kwin[blk:2 * blk, :] = kc_ref[...]
        kwin[2 * blk:3 * blk, :] = kn_ref[...]
        vwin[0:blk, :] = vp_ref[...]
        vwin[blk:2 * blk, :] = vc_ref[...]
        vwin[2 * blk:3 * blk, :] = vn_ref[...]

    g = step // (NA_ROWS // NA_STEP)
    n_loc = WIN_R * GRID_W
    lane = lax.broadcasted_iota(jnp.int32, (GRID_W, LANES), 1)
    scale = NA_HEAD_DIM ** -0.5
    n_ctx = kx_ref.shape[0]
    units = [(kk, hp) for kk in range(NA_STEP) for hp in range(NA_HEADS // 2)]
    offs = []
    for kk in range(NA_STEP):
        r = step * NA_STEP + kk
        r0 = jnp.clip(r - WIN_R // 2, 0, rows - WIN_R)
        offs.append(pl.multiple_of((r0 - NA_ROWS * g + NA_ROWS) * GRID_W, GRID_W))

    for u, (kk, hp) in enumerate(units):
        cols = slice(hp * LANES, (hp + 1) * LANES)
        q2 = q_ref[kk * GRID_W:(kk + 1) * GRID_W, cols] * scale
        zero = jnp.zeros_like(q2)
        qs = jnp.concatenate([jnp.where(lane < NA_HEAD_DIM, q2, zero), jnp.where(lane < NA_HEAD_DIM, zero, q2)],
                             axis=0)
        s_scr[u, :, 0:n_loc] = _dot_nt(qs, kwin[pl.ds(offs[kk], n_loc), cols])
        s_scr[u, :, n_loc:n_loc + n_ctx] = _dot_nt(qs, kx_ref[:, cols])

    for u, (kk, hp) in enumerate(units):
        s_loc = s_scr[u, :, 0:n_loc] + bias_refs[kk][0, hp]
        s_ctx = s_scr[u, :, n_loc:n_loc + n_ctx]
        m = jnp.maximum(jnp.max(s_loc, axis=-1, keepdims=True), jnp.max(s_ctx, axis=-1, keepdims=True))
        p_loc = jnp.exp(s_loc - m)
        p_ctx = jnp.exp(s_ctx - m)
        inv = 1.0 / (jnp.sum(p_loc, axis=-1, keepdims=True) + jnp.sum(p_ctx, axis=-1, keepdims=True))
        p_scr[u, :, 0:n_loc] = (p_loc * inv).astype(BF16)
        p_scr[u, :, n_loc:n_loc + n_ctx] = (p_ctx * inv).astype(BF16)

    for u, (kk, hp) in enumerate(units):
        cols = slice(hp * LANES, (hp + 1) * LANES)
        o2 = _dot(p_scr[u, :, 0:n_loc], vwin[pl.ds(offs[kk], n_loc), cols])
        o2 += _dot(p_scr[u, :, n_loc:n_loc + n_ctx], vx_ref[:, cols])
        out = jnp.where(lane < NA_HEAD_DIM, o2[0:GRID_W], o2[GRID_W:2 * GRID_W])
        o_ref[kk * GRID_W:(kk + 1) * GRID_W, cols] = out.astype(o_ref.dtype)


def _na_bias_table(rpb):
    qc = jnp.arange(GRID_W)[:, None]
    kc = jnp.arange(GRID_W)[None, :]
    cs = jnp.clip(qc - WIN_C // 2, 0, GRID_W - WIN_C)
    ok = (kc >= cs) & (kc < cs + WIN_C)
    dc = jnp.clip(kc - qc + WIN_C - 1, 0, 2 * WIN_C - 2)
    dr = jnp.arange(WIN_R)[:, None] + jnp.arange(WIN_R)[None, :]
    tab = rpb.astype(F32)[:, dr][:, :, :, dc]
    tab = jnp.where(ok[None, None, None], tab, NEG_BIG)
    tab = jnp.transpose(tab, (1, 0, 3, 2, 4))
    return tab.reshape(WIN_R, NA_HEADS // 2, 2 * GRID_W, WIN_R * GRID_W)


def _na_attention(px, pc, bias, seq_len, ctx_len):
    t = px.shape[0]
    bsz = t // seq_len
    rows = seq_len // GRID_W
    n_g = rows // NA_ROWS
    blk = NA_ROWS * GRID_W

    steps = rows // NA_STEP
    per_blk = NA_ROWS // NA_STEP

    def kv_spec(col, shift):
        def imap(b, s):
            g = jnp.clip(s // per_blk + shift, 0, n_g - 1)
            return (b * n_g + g, col)
        return pl.BlockSpec((blk, NA_WIDTH), imap)

    def bias_spec(kk):
        def imap(b, s):
            r = s * NA_STEP + kk
            r0 = jnp.clip(r - WIN_R // 2, 0, rows - WIN_R)
            return (r0 - r + WIN_R - 1, 0, 0, 0)
        return pl.BlockSpec((1, NA_HEADS // 2, 2 * GRID_W, WIN_R * GRID_W), imap)

    return pl.pallas_call(
        functools.partial(_na_kernel, rows=rows),
        grid=(bsz, steps),
        in_specs=[
            pl.BlockSpec((NA_STEP * GRID_W, NA_WIDTH), lambda b, s: (b * steps + s, COL_NAQ)),
            kv_spec(COL_NAK, -1), kv_spec(COL_NAK, 0), kv_spec(COL_NAK, 1),
            kv_spec(COL_NAV, -1), kv_spec(COL_NAV, 0), kv_spec(COL_NAV, 1),
            pl.BlockSpec((ctx_len, NA_WIDTH), lambda b, s: (b, COL_NAK)),
            pl.BlockSpec((ctx_len, NA_WIDTH), lambda b, s: (b, COL_NAV)),
        ] + [bias_spec(kk) for kk in range(NA_STEP)],
        out_specs=pl.BlockSpec((NA_STEP * GRID_W, NA_WIDTH), lambda b, s: (b * steps + s, 0)),
        out_shape=jax.ShapeDtypeStruct((t, NA_WIDTH), BF16),
        scratch_shapes=[
            pltpu.VMEM((3 * blk, NA_WIDTH), BF16), pltpu.VMEM((3 * blk, NA_WIDTH), BF16),
            pltpu.VMEM((NA_STEP * NA_HEADS // 2, 2 * GRID_W, WIN_R * GRID_W + ctx_len), F32),
            pltpu.VMEM((NA_STEP * NA_HEADS // 2, 2 * GRID_W, WIN_R * GRID_W + ctx_len), BF16),
        ],
        compiler_params=_params(("parallel", "arbitrary")),
        name="na_attn",
    )(px, px, px, px, px, px, px, pc, pc, *([bias] * NA_STEP))


def _ctx_attn_kernel(q_ref, k_ref, v_ref, o_ref):
    n = q_ref.shape[0]
    lane = lax.broadcasted_iota(jnp.int32, (n, LANES), 1)
    scale = NA_HEAD_DIM ** -0.5
    for hp in range(NA_HEADS // 2):
        cols = slice(hp * LANES, (hp + 1) * LANES)
        q2 = q_ref[:, cols] * scale
        k2 = k_ref[:, cols]
        v2 = v_ref[:, cols]
        outs = []
        for sub in range(2):
            sel = (lane < NA_HEAD_DIM) if sub == 0 else (lane >= NA_HEAD_DIM)
            s = _dot_nt(jnp.where(sel, q2, jnp.zeros_like(q2)), k2)
            p = jnp.exp(s - jnp.max(s, axis=-1, keepdims=True))
            outs.append(_dot(p.astype(BF16), v2) / jnp.sum(p, axis=-1, keepdims=True))
        o_ref[:, cols] = jnp.where(lane < NA_HEAD_DIM, outs[0], outs[1]).astype(o_ref.dtype)


def _ctx_attention(pc, ctx_len):
    t = pc.shape[0]
    return pl.pallas_call(
        _ctx_attn_kernel,
        grid=(t // ctx_len,),
        in_specs=[pl.BlockSpec((ctx_len, NA_WIDTH), lambda b, col=col: (b, col))
                  for col in (COL_NAQ, COL_NAK, COL_NAV)],
        out_specs=pl.BlockSpec((ctx_len, NA_WIDTH), lambda b: (b, 0)),
        out_shape=jax.ShapeDtypeStruct((t, NA_WIDTH), BF16),
        compiler_params=_params(("parallel",)),
        name="ctx_attn",
    )(pc, pc, pc)


def _cmul(a, b):
    return a[0] * b[0] - a[1] * b[1], a[0] * b[1] + a[1] * b[0]


def _s5_tables(lam_re, lam_im, log_dt, b_re, b_im, c_re, c_im):
    L, G, S, H = S5_CHUNK, S5_GROUPS, S5_STATE, S5_GROUP
    lr = jnp.minimum(lam_re.astype(F32), S5_MAX_RE)
    li = lam_im.astype(F32)
    dt = jnp.exp(log_dt.astype(F32))[..., None]
    zr, zi = lr * dt, li * dt
    d = jnp.arange(L + 1, dtype=F32)[None, :, None, None]
    mag = jnp.exp(zr[:, None] * d)
    pw = (mag * jnp.cos(zi[:, None] * d), mag * jnp.sin(zi[:, None] * d))
    lb = (pw[0][:, 1], pw[1][:, 1])
    den = lr * lr + li * li
    ratio = (((lb[0] - 1.0) * lr + lb[1] * li) / den, (lb[1] * lr - (lb[0] - 1.0) * li) / den)
    bb = _cmul((ratio[0][..., None], ratio[1][..., None]), (b_re.astype(F32), b_im.astype(F32)))
    cc = (c_re.astype(F32), c_im.astype(F32))

    def kern(x):
        cl = _cmul((cc[0][x][None], cc[1][x][None]), (pw[0][x, :L, :, None, :], pw[1][x, :L, :, None, :]))
        return jnp.einsum('dghp,gpj->dghj', cl[0], bb[0][x]) - jnp.einsum('dghp,gpj->dghj', cl[1], bb[1][x])

    tt = jnp.arange(L)[None, :]
    ss = jnp.arange(L)[:, None]
    m_f = jnp.where((tt >= ss)[:, :, None, None, None], kern(0)[jnp.clip(tt - ss, 0, L - 1)], 0.0)
    m_b = jnp.where((ss >= tt)[:, :, None, None, None], kern(1)[jnp.clip(ss - tt, 0, L - 1)], 0.0)
    m = jnp.transpose(m_f + m_b, (2, 0, 4, 1, 3))

    def state_in(x, steps):
        p = (pw[0][x][steps][:, :, None, :], pw[1][x][steps][:, :, None, :])
        r = _cmul(p, (jnp.swapaxes(bb[0][x], 1, 2)[None], jnp.swapaxes(bb[1][x], 1, 2)[None]))
        return jnp.transpose(r[0], (1, 0, 2, 3)), jnp.transpose(r[1], (1, 0, 2, 3))

    def state_out(x, steps):
        p = (pw[0][x][steps][:, :, None, :], pw[1][x][steps][:, :, None, :])
        r = _cmul(p, (cc[0][x][None], cc[1][x][None]))
        return jnp.transpose(r[0], (1, 3, 0, 2)), jnp.transpose(r[1], (1, 3, 0, 2))

    steps = jnp.arange(L)
    wf = state_in(0, L - 1 - steps)
    wb = state_in(1, steps)
    vf = state_out(0, steps + 1)
    vb = state_out(1, L - steps)
    eye = jnp.eye(S5_QUAD_GROUPS, dtype=F32)
    Q, A = S5_QUADS, S5_QUAD_GROUPS

    def quad(a):
        return a.reshape((Q, A) + a.shape[1:])

    m_s = jnp.transpose(quad(m), (0, 2, 1, 3, 4, 5))
    m_q = (m_s[:, :, :, :, :, None, :] * eye[None, None, :, None, None, :, None]).astype(BF16)
    m_q = m_q.reshape(Q, S5_QCOLS, S5_QCOLS)
    w_s = jnp.stack([jnp.transpose(quad(w), (0, 2, 1, 3, 4)) for w in (wf[0], wf[1], wb[0], wb[1])],
                    axis=4)
    w_q = (w_s[:, :, :, :, :, None, :] * eye[None, None, :, None, None, :, None]).astype(BF16)
    w_q = w_q.reshape(Q, S5_QCOLS, 4 * S5_QSTATE)
    v_s = jnp.stack([quad(v) for v in (vf[0], -vf[1], vb[0], -vb[1])], axis=1)
    v_q = (v_s[:, :, :, :, :, None, :] * eye[None, None, :, None, None, :, None]).astype(BF16)
    v_q = v_q.reshape(Q, 4 * S5_QSTATE, S5_QCOLS)
    a_re = pw[0][:, L].reshape(2, 1, G * S)
    a_im = pw[1][:, L].reshape(2, 1, G * S)
    return w_q.astype(BF16), m_q.astype(BF16), v_q.astype(BF16), a_re, a_im


def _s5_operand(u_ref):
    return jnp.concatenate([u_ref[:, s, :] for s in range(S5_CHUNK)], axis=1).astype(BF16)


def _s5_local_kernel(u_ref, w_ref, sfr_ref, sfi_ref, sbr_ref, sbi_ref):
    s = _dot(_s5_operand(u_ref), w_ref[0])
    n = S5_QSTATE
    sfr_ref[...] = s[:, 0:n]
    sfi_ref[...] = s[:, n:2 * n]
    sbr_ref[...] = s[:, 2 * n:3 * n]
    sbi_ref[...] = s[:, 3 * n:4 * n]


def _s5_local_states(u3, w_q):
    rows = u3.shape[0]
    tr = min(256, rows)
    assert rows % tr == 0
    out = jax.ShapeDtypeStruct((rows, S5_GROUPS * S5_STATE), F32)
    return pl.pallas_call(
        _s5_local_kernel,
        grid=(S5_QUADS, rows // tr),
        in_specs=[
            pl.BlockSpec((tr, S5_CHUNK, LANES), lambda q, i: (i, 0, q)),
            pl.BlockSpec((1, S5_QCOLS, 4 * S5_QSTATE), lambda q, i: (q, 0, 0)),
        ],
        out_specs=[pl.BlockSpec((tr, S5_QSTATE), lambda q, i: (i, q))] * 4,
        out_shape=[out] * 4,
        compiler_params=_params(("parallel", "parallel")),
        name="s5_local",
    )(u3, w_q)


def _s5_scan_kernel(sr_ref, si_ref, ar_ref, ai_ref, h0r_ref, h0i_ref, hr_ref, hi_ref, fr_ref, fi_ref,
                    hr_scr, hi_scr, *, reverse):
    i = pl.program_id(1)
    n_i = pl.num_programs(1)
    steps = sr_ref.shape[1]

    @pl.when(i == 0)
    def _():
        hr_scr[...] = h0r_ref[...]
        hi_scr[...] = h0i_ref[...]

    ar = ar_ref[...]
    ai = ai_ref[...]

    def body(j, carry):
        hr, hi = carry
        c = (steps - 1 - j) if reverse else j
        hr_ref[:, c, :] = hr
        hi_ref[:, c, :] = hi
        nr = ar * hr - ai * hi + sr_ref[:, c, :]
        ni = ar * hi + ai * hr + si_ref[:, c, :]
        return nr, ni

    hr, hi = lax.fori_loop(0, steps, body, (hr_scr[...], hi_scr[...]))
    hr_scr[...] = hr
    hi_scr[...] = hi

    @pl.when(i == n_i - 1)
    def _():
        fr_ref[...] = hr
        fi_ref[...] = hi


def _s5_chunk_scan(s_re, s_im, a_re, a_im, h0_re, h0_im, bsz, reverse):
    _, n_c, width = s_re.shape
    tr = min(64, n_c)
    tc = 512
    n_i = n_c // tr

    def rmap(j, i):
        return (0, (n_i - 1 - i) if reverse else i, j)

    big = jax.ShapeDtypeStruct((bsz, n_c, width), F32)
    small = jax.ShapeDtypeStruct((bsz, width), F32)
    return pl.pallas_call(
        functools.partial(_s5_scan_kernel, reverse=reverse),
        grid=(width // tc, n_i),
        in_specs=[
            pl.BlockSpec((bsz, tr, tc), rmap), pl.BlockSpec((bsz, tr, tc), rmap),
            pl.BlockSpec((1, tc), lambda j, i: (0, j)), pl.BlockSpec((1, tc), lambda j, i: (0, j)),
            pl.BlockSpec((bsz, tc), lambda j, i: (0, j)), pl.BlockSpec((bsz, tc), lambda j, i: (0, j)),
        ],
        out_specs=[
            pl.BlockSpec((bsz, tr, tc), rmap), pl.BlockSpec((bsz, tr, tc), rmap),
            pl.BlockSpec((bsz, tc), lambda j, i: (0, j)), pl.BlockSpec((bsz, tc), lambda j, i: (0, j)),
        ],
        out_shape=[big, big, small, small],
        scratch_shapes=[pltpu.VMEM((bsz, tc), F32), pltpu.VMEM((bsz, tc), F32)],
        compiler_params=_params(("parallel", "arbitrary")),
        name="s5_scan_rev" if reverse else "s5_scan_fwd",
    )(s_re, s_im, a_re, a_im, h0_re, h0_im)


def _s5_out_kernel(u_ref, m_ref, v_ref, hfr_ref, hfi_ref, hbr_ref, hbi_ref, y_ref):
    n = S5_QSTATE
    y = _dot(_s5_operand(u_ref), m_ref[0])
    for k, h_ref in enumerate((hfr_ref, hfi_ref, hbr_ref, hbi_ref)):
        y += _dot(h_ref[...].astype(BF16), v_ref[0, k * n:(k + 1) * n, :])
    for t in range(S5_CHUNK):
        y_ref[:, t, :] = y[:, t * LANES:(t + 1) * LANES]


def _s5_outputs(u3, m_q, v_q, h_states):
    rows = u3.shape[0]
    tr = min(256, rows)
    return pl.pallas_call(
        _s5_out_kernel,
        grid=(S5_QUADS, rows // tr),
        in_specs=[
            pl.BlockSpec((tr, S5_CHUNK, LANES), lambda q, i: (i, 0, q)),
            pl.BlockSpec((1, S5_QCOLS, S5_QCOLS), lambda q, i: (q, 0, 0)),
            pl.BlockSpec((1, 4 * S5_QSTATE, S5_QCOLS), lambda q, i: (q, 0, 0)),
        ] + [pl.BlockSpec((tr, S5_QSTATE), lambda q, i: (i, q))] * 4,
        out_specs=pl.BlockSpec((tr, S5_CHUNK, LANES), lambda q, i: (i, 0, q)),
        out_shape=jax.ShapeDtypeStruct((rows, S5_CHUNK, S5_WIDTH), F32),
        compiler_params=_params(("parallel", "parallel")),
        name="s5_out",
    )(u3, m_q, v_q, *h_states)


def _s5_branch(u, tables, seq_len, init, want_y):
    w_q, m_q, v_q, a_re, a_im = tables
    t = u.shape[0]
    bsz = t // seq_len
    n_c = seq_len // S5_CHUNK
    width = S5_GROUPS * S5_STATE
    u3 = u.reshape(t // S5_CHUNK, S5_CHUNK, S5_WIDTH)
    s_loc = [s.reshape(bsz, n_c, width) for s in _s5_local_states(u3, w_q)]
    hfr, hfi, ffr, ffi = _s5_chunk_scan(s_loc[0], s_loc[1], a_re[0], a_im[0], init[0], init[1], bsz, False)
    hbr, hbi, fbr, fbi = _s5_chunk_scan(s_loc[2], s_loc[3], a_re[1], a_im[1], init[2], init[3], bsz, True)
    y = None
    if want_y:
        h_states = [h.reshape(bsz * n_c, width) for h in (hfr, hfi, hbr, hbi)]
        y = _s5_outputs(u3, m_q, v_q, h_states).reshape(t, S5_WIDTH)
    return y, (ffr, ffi, fbr, fbi)


def _gelu_tanh(x):
    return 0.5 * x * (1.0 + jnp.tanh(math.sqrt(2.0 / math.pi) * (x + 0.044715 * (x * x * x))))


def _merge_kernel(x_ref, ym_ref, yn_ref, ys_ref, u_ref, gm_ref, gn_ref, gs_ref, d_ref, gw_ref, gb_ref,
                  wm_ref, wn_ref, ws_ref, wo_ref, gate_ref, npost_ref, o_ref):
    ys = ys_ref[...] + d_ref[...] * u_ref[...]
    g = _gelu_tanh(ys)
    gg = g * jax.nn.sigmoid(_dot(g.astype(BF16), gw_ref[...]) + gb_ref[...])
    y = jax.nn.sigmoid(gm_ref[...].astype(F32)) * _dot(ym_ref[...], wm_ref[...])
    y += jax.nn.sigmoid(gn_ref[...].astype(F32)) * _dot(yn_ref[...], wn_ref[...])
    y += jax.nn.sigmoid(gs_ref[...].astype(F32)) * _dot(gg.astype(BF16), ws_ref[...])
    out = _dot(y.astype(BF16), wo_ref[...])
    o_ref[...] = x_ref[...] + gate_ref[0] * _rms(out, npost_ref[...])


def _merge(x, ym, yn, ys, u, px, wts, gate, npost, rows_per_mod):
    t = x.shape[0]
    tm = min(512, rows_per_mod)
    per = rows_per_mod // tm
    s5_d, glu_w, glu_b, w_m, w_n, w_s, w_o = wts
    row = lambda i: (i, 0)
    const = lambda i: (0, 0)
    return pl.pallas_call(
        _merge_kernel,
        grid=(t // tm,),
        in_specs=[
            pl.BlockSpec((tm, D_MODEL), row),
            pl.BlockSpec((tm, M_WIDTH), row),
            pl.BlockSpec((tm, NA_WIDTH), row),
            pl.BlockSpec((tm, S5_WIDTH), row),
            pl.BlockSpec((tm, S5_WIDTH), row),
            pl.BlockSpec((tm, D_MODEL), lambda i: (i, COL_GATE)),
            pl.BlockSpec((tm, D_MODEL), lambda i: (i, COL_GATE + 1)),
            pl.BlockSpec((tm, D_MODEL), lambda i: (i, COL_GATE + 2)),
            pl.BlockSpec((1, S5_WIDTH), const),
            pl.BlockSpec((S5_WIDTH, S5_WIDTH), const),
            pl.BlockSpec((1, S5_WIDTH), const),
            pl.BlockSpec((M_WIDTH, D_MODEL), const),
            pl.BlockSpec((NA_WIDTH, D_MODEL), const),
            pl.BlockSpec((S5_WIDTH, D_MODEL), const),
            pl.BlockSpec((D_MODEL, D_MODEL), const),
            pl.BlockSpec((1, 1, D_MODEL), lambda i: (i // per, 0, 0)),
            pl.BlockSpec((1, D_MODEL), const),
        ],
        out_specs=pl.BlockSpec((tm, D_MODEL), row),
        out_shape=jax.ShapeDtypeStruct((t, D_MODEL), F32),
        compiler_params=_params(("parallel",)),
        name="merge",
    )(x, ym, yn, ys, u, px, px, px, s5_d, glu_w, glu_b, w_m, w_n, w_s, w_o, gate, npost)


def _swiglu(h, wg, wu, wd):
    a = _dot(h, wg)
    a = a * jax.nn.sigmoid(a) * _dot(h, wu)
    return _dot(a.astype(BF16), wd)


def _ffn_kernel(x_ref, g_ref, sh_ref, sc_ref, wg_ref, wu_ref, wd_ref, gate_ref, npost_ref,
                o_ref, hn_ref, acc_ref):
    j = pl.program_id(1)

    @pl.when(j == 0)
    def _():
        h = _rms(x_ref[...], g_ref[...]) * (1.0 + sc_ref[0]) + sh_ref[0]
        hn_ref[...] = h.astype(BF16)
        acc_ref[...] = jnp.zeros_like(acc_ref)

    acc_ref[...] += _swiglu(hn_ref[...], wg_ref[...], wu_ref[...], wd_ref[...])

    @pl.when(j == pl.num_programs(1) - 1)
    def _():
        o_ref[...] = x_ref[...] + gate_ref[0] * _rms(acc_ref[...], npost_ref[...])


def _ffn(x, g, shift, scale, w_gate, w_up, w_down, gate, npost, rows_per_mod):
    t = x.shape[0]
    ff = w_gate.shape[1]
    ff_tile = ff // 2
    tm = min(512, rows_per_mod)
    per = rows_per_mod // tm
    assert ff_tile % LANES == 0
    return pl.pallas_call(
        _ffn_kernel,
        grid=(t // tm, ff // ff_tile),
        in_specs=[
            pl.BlockSpec((tm, D_MODEL), lambda i, j: (i, 0)),
            pl.BlockSpec((1, D_MODEL), lambda i, j: (0, 0)),
            pl.BlockSpec((1, 1, D_MODEL), lambda i, j: (i // per, 0, 0)),
            pl.BlockSpec((1, 1, D_MODEL), lambda i, j: (i // per, 0, 0)),
            pl.BlockSpec((D_MODEL, ff_tile), lambda i, j: (0, j)),
            pl.BlockSpec((D_MODEL, ff_tile), lambda i, j: (0, j)),
            pl.BlockSpec((ff_tile, D_MODEL), lambda i, j: (j, 0)),
            pl.BlockSpec((1, 1, D_MODEL), lambda i, j: (i // per, 0, 0)),
            pl.BlockSpec((1, D_MODEL), lambda i, j: (0, 0)),
        ],
        out_specs=pl.BlockSpec((tm, D_MODEL), lambda i, j: (i, 0)),
        out_shape=jax.ShapeDtypeStruct((t, D_MODEL), F32),
        scratch_shapes=[pltpu.VMEM((tm, D_MODEL), BF16), pltpu.VMEM((tm, D_MODEL), F32)],
        compiler_params=_params(("parallel", "arbitrary")),
        name="ffn",
    )(x, g, shift, scale, w_gate, w_up, w_down, gate, npost)


def _moe_kernel(x_ref, g_ref, sh_ref, sc_ref, rh_ref, rl_ref, wg_ref, wu_ref, wd_ref, gate_ref, npost_ref,
                o_ref, hn_ref, acc_ref, comb_ref, rank_c_ref, rank_r_ref, mask_r_ref, xc_ref, yc_ref, cnt_ref):
    e = pl.program_id(1)
    j = pl.program_id(2)
    tm = x_ref.shape[0]
    cap = MOE_CAP

    @pl.when((e == 0) & (j == 0))
    def _():
        h = _rms(x_ref[...], g_ref[...]) * (1.0 + sc_ref[0]) + sh_ref[0]
        hi = h.astype(BF16)
        hn_ref[...] = hi
        acc_ref[...] = jnp.zeros_like(acc_ref)
        lo = (h - hi.astype(F32)).astype(BF16)
        logits = _dot(hi, rh_ref[...]) + _dot(lo, rh_ref[...]) + _dot(hi, rl_ref[...])
        lane = lax.broadcasted_iota(jnp.int32, logits.shape, 1)
        logits = jnp.where(lane < N_EXPERTS, logits, -jnp.inf)
        m1 = jnp.max(logits, axis=-1, keepdims=True)
        i1 = jnp.min(jnp.where(logits == m1, lane, LANES), axis=-1, keepdims=True)
        rest = jnp.where(lane == i1, -jnp.inf, logits)
        m2 = jnp.max(rest, axis=-1, keepdims=True)
        i2 = jnp.min(jnp.where(rest == m2, lane, LANES), axis=-1, keepdims=True)
        e2 = jnp.exp(m2 - m1)
        comb_ref[...] = jnp.where(lane == i1, 1.0 / (1.0 + e2), 0.0) + jnp.where(lane == i2, e2 / (1.0 + e2), 0.0)
        sel = jnp.where((lane == i1) | (lane == i2), 1.0, 0.0)
        ti = lax.broadcasted_iota(jnp.int32, (tm, tm), 0)
        si = lax.broadcasted_iota(jnp.int32, (tm, tm), 1)
        before = jnp.where(si < ti, 1.0, 0.0).astype(BF16)
        rank_c_ref[...] = _dot(before, sel.astype(BF16))
        sel_t = sel.T
        after = jnp.where(ti < si, 1.0, 0.0).astype(BF16)
        rank_r_ref[...] = _dot(sel_t.astype(BF16), after)
        mask_r_ref[...] = sel_t
        counts = jnp.sum(sel_t, axis=1, keepdims=True)
        for k in range(N_EXPERTS):
            cnt_ref[k] = jnp.sum(counts[k:k + 1, :]).astype(jnp.int32)

    n_pass = (cnt_ref[e] + (cap - 1)) // cap

    @pl.when(j == 0)
    def _():
        rank_row = rank_r_ref[pl.ds(e, 1), :]
        mask_row = mask_r_ref[pl.ds(e, 1), :]
        slot = lax.broadcasted_iota(jnp.int32, (cap, tm), 0).astype(F32)

        def compact(p, _):
            base = (p * cap).astype(F32)
            onehot = jnp.where((rank_row - base == slot) & (mask_row > 0.0), 1.0, 0.0).astype(BF16)
            rows = pl.ds(pl.multiple_of(p * cap, 16), cap)
            xc_ref[rows, :] = _dot(onehot, hn_ref[...]).astype(BF16)
            yc_ref[rows, :] = jnp.zeros((cap, D_MODEL), F32)
            return 0

        lax.fori_loop(0, n_pass, compact, 0)

    def expert(start, size):
        rows = pl.ds(pl.multiple_of(start, MOE_GRAIN), size)
        yc_ref[rows, :] += _swiglu(xc_ref[rows, :], wg_ref[0], wu_ref[0], wd_ref[0])

    n_tok = cnt_ref[e]
    n_full = n_tok // cap
    rem_blocks = (n_tok - n_full * cap + (MOE_GRAIN - 1)) // MOE_GRAIN
    n_full = n_full + (rem_blocks == cap // MOE_GRAIN).astype(jnp.int32)

    def full_pass(p, _):
        expert(p * cap, cap)
        return 0

    lax.fori_loop(0, n_full, full_pass, 0)
    for blocks in range(1, cap // MOE_GRAIN):
        @pl.when(rem_blocks == blocks)
        def _(blocks=blocks):
            expert(n_full * cap, blocks * MOE_GRAIN)

    @pl.when(j == pl.num_programs(2) - 1)
    def _():
        lane = lax.broadcasted_iota(jnp.int32, (tm, LANES), 1)
        pick = lane == e
        rank_col = jnp.sum(jnp.where(pick, rank_c_ref[...], 0.0), axis=-1, keepdims=True)
        w_col = jnp.sum(jnp.where(pick, comb_ref[...], 0.0), axis=-1, keepdims=True)
        slot = lax.broadcasted_iota(jnp.int32, (tm, cap), 1).astype(F32)

        def expand(p, _):
            base = (p * cap).astype(F32)
            onehot = jnp.where(rank_col - base == slot, 1.0, 0.0).astype(BF16)
            rows = pl.ds(pl.multiple_of(p * cap, 16), cap)
            acc_ref[...] += w_col * _dot(onehot, yc_ref[rows, :].astype(BF16))
            return 0

        lax.fori_loop(0, n_pass, expand, 0)

    @pl.when((e == N_EXPERTS - 1) & (j == pl.num_programs(2) - 1))
    def _():
        o_ref[...] = x_ref[...] + gate_ref[0] * _rms(acc_ref[...], npost_ref[...])


def _moe(x, g, shift, scale, r_hi, r_lo, w_gate, w_up, w_down, gate, npost, rows_per_mod):
    t = x.shape[0]
    n_e, _, ff = w_gate.shape
    assert n_e == N_EXPERTS
    ff_tile = ff // 4
    tm = min(MOE_TILE, rows_per_mod)
    per = rows_per_mod // tm
    passes = -(-tm // MOE_CAP)
    assert ff_tile % LANES == 0 and MOE_CAP % MOE_GRAIN == 0 and MOE_GRAIN % 16 == 0
    return pl.pallas_call(
        _moe_kernel,
        grid=(t // tm, n_e, ff // ff_tile),
        in_specs=[
            pl.BlockSpec((tm, D_MODEL), lambda i, e, j: (i, 0)),
            pl.BlockSpec((1, D_MODEL), lambda i, e, j: (0, 0)),
            pl.BlockSpec((1, 1, D_MODEL), lambda i, e, j: (i // per, 0, 0)),
            pl.BlockSpec((1, 1, D_MODEL), lambda i, e, j: (i // per, 0, 0)),
            pl.BlockSpec((D_MODEL, LANES), lambda i, e, j: (0, 0)),
            pl.BlockSpec((D_MODEL, LANES), lambda i, e, j: (0, 0)),
            pl.BlockSpec((1, D_MODEL, ff_tile), lambda i, e, j: (e, 0, j)),
            pl.BlockSpec((1, D_MODEL, ff_tile), lambda i, e, j: (e, 0, j)),
            pl.BlockSpec((1, ff_tile, D_MODEL), lambda i, e, j: (e, j, 0)),
            pl.BlockSpec((1, 1, D_MODEL), lambda i, e, j: (i // per, 0, 0)),
            pl.BlockSpec((1, D_MODEL), lambda i, e, j: (0, 0)),
        ],
        out_specs=pl.BlockSpec((tm, D_MODEL), lambda i, e, j: (i, 0)),
        out_shape=jax.ShapeDtypeStruct((t, D_MODEL), F32),
        scratch_shapes=[
            pltpu.VMEM((tm, D_MODEL), BF16),
            pltpu.VMEM((tm, D_MODEL), F32),
            pltpu.VMEM((tm, LANES), F32),
            pltpu.VMEM((tm, LANES), F32),
            pltpu.VMEM((LANES, tm), F32),
            pltpu.VMEM((LANES, tm), F32),
            pltpu.VMEM((passes * MOE_CAP, D_MODEL), BF16),
            pltpu.VMEM((passes * MOE_CAP, D_MODEL), F32),
            pltpu.SMEM((N_EXPERTS,), jnp.int32),
        ],
        compiler_params=_params(("parallel", "arbitrary", "arbitrary")),
        name="moe",
    )(x, g, shift, scale, r_hi, r_lo, w_gate, w_up, w_down, gate, npost)


def _reorder_w_in(w):
    o = 4 * M_WIDTH
    gates = jnp.pad(w[:, o:o + M_GATES], ((0, 0), (0, LANES - M_GATES)))
    na = w[:, o + M_GATES:o + M_GATES + 3 * NA_WIDTH]
    u0 = o + M_GATES + 3 * NA_WIDTH
    main = jnp.concatenate([w[:, :o], w[:, u0 + S5_WIDTH:], na], axis=1)
    return main.astype(BF16), gates.astype(BF16), w[:, u0:u0 + S5_WIDTH].astype(BF16)


def _row(v):
    return v.reshape(1, -1).astype(F32)


def _token_mixers(x, xc, mx, mc, lp, rope, seq_len, ctx_len, ctx_out):
    bsz = x.shape[0] // seq_len
    w_main, w_gates, w_u = _reorder_w_in(lp['w_in'])
    g_pre = _row(lp['norm_mix_pre'])
    px, gx, ux = _inproj(x, g_pre, mx[0], mx[1], w_main, w_gates, w_u, seq_len)
    pc, gc, uc = _inproj(xc, g_pre, mc[0], mc[1], w_main, w_gates, w_u, xc.shape[0])

    conv_w = lp['m_conv_w'].astype(F32)
    conv_b = _row(lp['m_conv_b'])
    gate_b = jnp.pad(_row(lp['m_gate_b']), ((0, 0), (0, LANES - M_GATES)))
    m_norm = _row(lp['m_norm'])
    qk_c = _qkconv(pc, conv_w, conv_b, rope[0][:ctx_len], rope[1][:ctx_len], ctx_len, False)
    qk_x = _qkconv(px, conv_w, conv_b, rope[0], rope[1], seq_len, True)
    zero = _mlstm_zero_state(bsz)
    ym_c, fin_f, fin_b = _mlstm_bidir(qk_c, pc, gc, gate_b, m_norm, zero, zero, ctx_len)
    ym_x, _, _ = _mlstm_bidir(qk_x, px, gx, gate_b, m_norm, fin_f, fin_b, seq_len)

    yn_x = _na_attention(px, pc, _na_bias_table(lp['na_rpb']), seq_len, ctx_len)

    tables = _s5_tables(lp['s5_lam_re'], lp['s5_lam_im'], lp['s5_log_dt'], lp['s5_b_re'], lp['s5_b_im'],
                        lp['s5_c_re'], lp['s5_c_im'])
    zs = jnp.zeros((bsz, S5_GROUPS * S5_STATE), F32)
    ys_c, fin_s = _s5_branch(uc, tables, ctx_len, (zs, zs, zs, zs), ctx_out)
    ys_x, _ = _s5_branch(ux, tables, seq_len, fin_s, True)

    wts = (_row(lp['s5_d']), lp['s5_glu_w'].astype(BF16), _row(lp['s5_glu_b']),
           lp['w_branch_m'].astype(BF16), lp['w_branch_na'].astype(BF16), lp['w_branch_s5'].astype(BF16),
           lp['w_out'].astype(BF16))
    g_post = _row(lp['norm_mix_post'])
    x = _merge(x, ym_x, yn_x, ys_x, ux, px, wts, mx[2], g_post, seq_len)
    if ctx_out:
        yn_c = _ctx_attention(pc, ctx_len)
        xc = _merge(xc, ym_c, yn_c, ys_c, uc, pc, wts, mc[2], g_post, xc.shape[0])
    return x, xc


def kernel(x, c, ctx, c_ctx, ada_w, ada_b, norm_mix_pre, norm_mix_post, norm_ffn_pre, norm_ffn_post, w_in, m_gate_b, m_conv_w, m_conv_b, m_norm, na_rpb, s5_lam_re, s5_lam_im, s5_log_dt, s5_b_re, s5_b_im, s5_c_re, s5_c_im, s5_d, s5_glu_w, s5_glu_b, w_branch_m, w_branch_na, w_branch_s5, w_out, ffn_w_gate, ffn_w_up, ffn_w_down, moe_router, moe_w_gate, moe_w_up, moe_w_down):
    bsz, seq_len, d = x.shape
    ctx_len = ctx.shape[1]
    depth = w_in.shape[0]
    assert d == D_MODEL and seq_len % (GRID_W * NA_ROWS) == 0 and ctx_len % M_CHUNK == 0
    rope = _rope_tables(seq_len)
    xf = x.reshape(bsz * seq_len, d)
    xc = ctx.reshape(bsz * ctx_len, d)
    cvecs = jnp.zeros((16, d), F32).at[:bsz].set(c).at[bsz].set(c_ctx)
    for l in range(depth):
        last = l == depth - 1
        mod = _adaln(cvecs, ada_w[l].astype(BF16), _row(ada_b[l]))
        mx = [mod[:bsz, k * d:(k + 1) * d].reshape(bsz, 1, d) for k in range(N_MOD)]
        mc = [mod[bsz:bsz + 1, k * d:(k + 1) * d].reshape(1, 1, d) for k in range(N_MOD)]
        lp = {
            'norm_mix_pre': norm_mix_pre[l], 'norm_mix_post': norm_mix_post[l],
            'w_in': w_in[l], 'm_gate_b': m_gate_b[l], 'm_conv_w': m_conv_w[l], 'm_conv_b': m_conv_b[l],
            'm_norm': m_norm[l], 'na_rpb': na_rpb[l], 's5_lam_re': s5_lam_re[l], 's5_lam_im': s5_lam_im[l],
            's5_log_dt': s5_log_dt[l], 's5_b_re': s5_b_re[l], 's5_b_im': s5_b_im[l], 's5_c_re': s5_c_re[l],
            's5_c_im': s5_c_im[l], 's5_d': s5_d[l], 's5_glu_w': s5_glu_w[l], 's5_glu_b': s5_glu_b[l],
            'w_branch_m': w_branch_m[l], 'w_branch_na': w_branch_na[l], 'w_branch_s5': w_branch_s5[l],
            'w_out': w_out[l],
        }
        xf, xc = _token_mixers(xf, xc, mx, mc, lp, rope, seq_len, ctx_len, not last)
        j = l // 2
        g_pre, g_post = _row(norm_ffn_pre[l]), _row(norm_ffn_post[l])
        if l % 2 == 0:
            wg, wu, wd = ffn_w_gate[j].astype(BF16), ffn_w_up[j].astype(BF16), ffn_w_down[j].astype(BF16)

            def channel(h, m, rows):
                return _ffn(h, g_pre, m[3], m[4], wg, wu, wd, m[5], g_post, rows)
        else:
            wg, wu, wd = moe_w_gate[j].astype(BF16), moe_w_up[j].astype(BF16), moe_w_down[j].astype(BF16)
            router = jnp.pad(moe_router[j].astype(F32), ((0, 0), (0, LANES - N_EXPERTS)))
            r_hi = router.astype(BF16)
            r_lo = (router - r_hi.astype(F32)).astype(BF16)

            def channel(h, m, rows):
                return _moe(h, g_pre, m[3], m[4], r_hi, r_lo, wg, wu, wd, m[5], g_post, rows)
        xf = channel(xf, mx, seq_len)
        if not last:
            xc = channel(xc, mc, xc.shape[0])
    return xf.reshape(bsz, seq_len, d)
```

```python
import functools
import math

import jax
import jax.numpy as jnp
from jax import lax
from jax.experimental import pallas as pl
from jax.experimental.pallas import tpu as pltpu

F32 = jnp.float32
BF16 = jnp.bfloat16

D_MODEL = 1024
EPS = 1e-6
N_MOD = 6
GRID_W = 64

M_HEADS = 4
M_HEAD_DIM = 128
M_WIDTH = M_HEADS * M_HEAD_DIM
M_GATES = 4 * M_HEADS
M_CHUNK = 128
ROPE_BASE = 10000.0

NA_HEADS = 8
NA_HEAD_DIM = 64
NA_WIDTH = NA_HEADS * NA_HEAD_DIM
WIN_R = 8
WIN_C = 16
NEG_BIG = -1e30

S5_GROUP = 16
S5_WIDTH = 512
S5_GROUPS = S5_WIDTH // S5_GROUP
S5_STATE = 64
S5_MAX_RE = -1e-4

LANES = 128
VMEM_LIMIT = 52 * 1024 * 1024

S5_CHUNK = 16
S5_QUAD_GROUPS = LANES // S5_GROUP
S5_QUADS = S5_GROUPS // S5_QUAD_GROUPS
S5_QCOLS = S5_CHUNK * LANES
S5_QSTATE = S5_QUAD_GROUPS * S5_STATE

N_EXPERTS = 8
TOP_K = 2
MOE_TILE = 1024
MOE_CAP = 256
MOE_GRAIN = 64

COL_QK, COL_V, COL_O, COL_NAQ, COL_NAK, COL_NAV = 0, 2, 3, 10, 11, 12
COL_GATE = 2
IN_MAIN = 6656


def _params(sem):
    return pltpu.CompilerParams(dimension_semantics=sem, vmem_limit_bytes=VMEM_LIMIT)


def _dot(a, b):
    return jnp.dot(a, b, preferred_element_type=F32)


def _dot_nt(a, b):
    return lax.dot_general(a, b, (((1,), (1,)), ((), ())), preferred_element_type=F32)


def _split3(x):
    hi = x.astype(BF16)
    r1 = x - hi.astype(F32)
    mid = r1.astype(BF16)
    lo = (r1 - mid.astype(F32)).astype(BF16)
    return hi, mid, lo


def _rms(x, g):
    return x * lax.rsqrt(jnp.mean(x * x, axis=-1, keepdims=True) + EPS) * g


def _inproj_kernel(x_ref, g_ref, sh_ref, sc_ref, w_ref, wg_ref, wu_ref, o_ref, og_ref, ou_ref, hn_ref):
    @pl.when(pl.program_id(1) == 0)
    def _():
        h = _rms(x_ref[...], g_ref[...]) * (1.0 + sc_ref[0]) + sh_ref[0]
        hb = h.astype(BF16)
        hn_ref[...] = hb
        og_ref[...] = _dot(hb, wg_ref[...])
        ou_ref[...] = _dot(hb, wu_ref[...])

    o_ref[...] = _dot(hn_ref[...], w_ref[...]).astype(o_ref.dtype)


def _inproj(x, g, shift, scale, w_main, w_gates, w_u, rows_per_mod):
    t = x.shape[0]
    tm = min(1024, rows_per_mod)
    tn = IN_MAIN // 4
    assert rows_per_mod % tm == 0 and t % tm == 0 and tn % LANES == 0
    per = rows_per_mod // tm
    return pl.pallas_call(
        _inproj_kernel,
        grid=(t // tm, IN_MAIN // tn),
        in_specs=[
            pl.BlockSpec((tm, D_MODEL), lambda i, j: (i, 0)),
            pl.BlockSpec((1, D_MODEL), lambda i, j: (0, 0)),
            pl.BlockSpec((1, 1, D_MODEL), lambda i, j: (i // per, 0, 0)),
            pl.BlockSpec((1, 1, D_MODEL), lambda i, j: (i // per, 0, 0)),
            pl.BlockSpec((D_MODEL, tn), lambda i, j: (0, j)),
            pl.BlockSpec((D_MODEL, LANES), lambda i, j: (0, 0)),
            pl.BlockSpec((D_MODEL, S5_WIDTH), lambda i, j: (0, 0)),
        ],
        out_specs=[
            pl.BlockSpec((tm, tn), lambda i, j: (i, j)),
            pl.BlockSpec((tm, LANES), lambda i, j: (i, 0)),
            pl.BlockSpec((tm, S5_WIDTH), lambda i, j: (i, 0)),
        ],
        out_shape=[jax.ShapeDtypeStruct((t, IN_MAIN), BF16), jax.ShapeDtypeStruct((t, LANES), F32),
                   jax.ShapeDtypeStruct((t, S5_WIDTH), F32)],
        scratch_shapes=[pltpu.VMEM((tm, D_MODEL), BF16)],
        compiler_params=_params(("parallel", "arbitrary")),
        name="inproj",
    )(x, g, shift, scale, w_main, w_gates, w_u)


def _adaln_kernel(c_ref, w_ref, b_ref, o_ref):
    c = c_ref[...]
    s = c * jax.nn.sigmoid(c)
    o_ref[...] = _dot(s.astype(BF16), w_ref[...]) + b_ref[...]


def _adaln(cvecs, w, b):
    n = w.shape[1]
    tn = 1024
    return pl.pallas_call(
        _adaln_kernel,
        grid=(n // tn,),
        in_specs=[
            pl.BlockSpec(cvecs.shape, lambda j: (0, 0)),
            pl.BlockSpec((D_MODEL, tn), lambda j: (0, j)),
            pl.BlockSpec((1, tn), lambda j: (0, j)),
        ],
        out_specs=pl.BlockSpec((cvecs.shape[0], tn), lambda j: (0, j)),
        out_shape=jax.ShapeDtypeStruct((cvecs.shape[0], n), F32),
        compiler_params=_params(("parallel",)),
        name="adaln",
    )(cvecs, w, b)


HALO = 16


def _swap32(x):
    lane = lax.broadcasted_iota(jnp.int32, x.shape, 1)
    fwd = pltpu.roll(x, 96, 1)
    bwd = pltpu.roll(x, 32, 1)
    return jnp.where((lane % 64) < 32, fwd, bwd)


def _qkconv_kernel(x_ref, pv_ref, nx_ref, w_ref, b_ref, cos_ref, sin_ref, o_ref, *, tiles_per_seq, rope):
    i = pl.program_id(0)
    tm = x_ref.shape[0]
    x = x_ref[...].astype(F32)
    first = (i % tiles_per_seq) == 0
    last = (i % tiles_per_seq) == tiles_per_seq - 1
    prev_row = jnp.where(first, 0.0, pv_ref[HALO - 1:HALO, :].astype(F32))
    next_row = jnp.where(last, 0.0, nx_ref[0:1, :].astype(F32))
    row = lax.broadcasted_iota(jnp.int32, x.shape, 0)
    xp = jnp.where(row == 0, prev_row, pltpu.roll(x, 1, 0))
    xn = jnp.where(row == tm - 1, next_row, pltpu.roll(x, tm - 1, 0))
    y = xp * w_ref[0:1, :] + x * w_ref[1:2, :] + xn * w_ref[2:3, :] + b_ref[...]
    y = y * jax.nn.sigmoid(y)
    kscale = M_HEAD_DIM ** -0.5
    for hh in range(2 * M_HEADS):
        ys = y[:, hh * LANES:(hh + 1) * LANES]
        if rope:
            ys = ys * cos_ref[...] + _swap32(ys) * sin_ref[...]
        if hh >= M_HEADS:
            ys = ys * kscale
        o_ref[:, hh * LANES:(hh + 1) * LANES] = ys.astype(o_ref.dtype)


def _qkconv(px, conv_w, conv_b, cos_t, sin_t, seq_len, rope):
    t = px.shape[0]
    tm = min(512, seq_len)
    assert seq_len % tm == 0
    tiles = seq_len // tm
    hb = tm // HALO
    nh = t // HALO
    return pl.pallas_call(
        functools.partial(_qkconv_kernel, tiles_per_seq=tiles, rope=rope),
        grid=(t // tm,),
        in_specs=[
            pl.BlockSpec((tm, 2 * M_WIDTH), lambda i: (i, 0)),
            pl.BlockSpec((HALO, 2 * M_WIDTH), lambda i: (jnp.maximum(i * hb - 1, 0), 0)),
            pl.BlockSpec((HALO, 2 * M_WIDTH), lambda i: (jnp.minimum((i + 1) * hb, nh - 1), 0)),
            pl.BlockSpec((3, 2 * M_WIDTH), lambda i: (0, 0)),
            pl.BlockSpec((1, 2 * M_WIDTH), lambda i: (0, 0)),
            pl.BlockSpec((tm, LANES), lambda i: (i % tiles, 0)),
            pl.BlockSpec((tm, LANES), lambda i: (i % tiles, 0)),
        ],
        out_specs=pl.BlockSpec((tm, 2 * M_WIDTH), lambda i: (i, 0)),
        out_shape=jax.ShapeDtypeStruct((t, 2 * M_WIDTH), BF16),
        compiler_params=_params(("parallel",)),
        name="qkconv",
    )(px, px, px, conv_w, conv_b, cos_t, sin_t)


def _rope_tables(n):
    pos = jnp.arange(n, dtype=jnp.int32)
    row = (pos // GRID_W).astype(F32)
    col = (pos % GRID_W).astype(F32)
    n_freq = M_HEAD_DIM // 4
    inv = ROPE_BASE ** (-jnp.arange(n_freq, dtype=F32) / n_freq)
    ar, ac = row[:, None] * inv, col[:, None] * inv
    cos_t = jnp.concatenate([jnp.cos(ar), jnp.cos(ar), jnp.cos(ac), jnp.cos(ac)], axis=-1)
    sin_t = jnp.concatenate([-jnp.sin(ar), jnp.sin(ar), -jnp.sin(ac), jnp.sin(ac)], axis=-1)
    return cos_t, sin_t


def _mlstm_kernel(*refs, reverse, final, nb):
    if final:
        (qk_ref, v_ref, g_ref, gb_ref, c0_ref, n0_ref, m0_ref, hp_ref, op_ref, nw_ref,
         out_ref, cf_ref, nf_ref, mf_ref, c_scr, n_scr, m_scr, qk_scr, qc_scr, qn_scr, s_scr, vt_scr, vu_scr, dec_scr) = refs
    else:
        (qk_ref, v_ref, g_ref, gb_ref, c0_ref, n0_ref, m0_ref,
         out_ref, cf_ref, nf_ref, mf_ref, c_scr, n_scr, m_scr, qk_scr, qc_scr, qn_scr, s_scr, vt_scr, vu_scr, dec_scr) = refs
    c_idx = pl.program_id(1)
    n_c = pl.num_programs(1)
    L = M_CHUNK

    @pl.when(c_idx == 0)
    def _():
        c_scr[...] = c0_ref[...]
        n_scr[...] = n0_ref[...]
        m_scr[...] = m0_ref[...]

    ti = lax.broadcasted_iota(jnp.int32, (L, L), 0)
    si = lax.broadcasted_iota(jnp.int32, (L, L), 1)
    keep = (si >= ti) if reverse else (si <= ti)
    tri = jnp.where(keep, 1.0, 0.0).astype(BF16)
    i_off = 2 * M_HEADS if reverse else 0
    f_off = i_off + M_HEADS
    end = 0 if reverse else L - 1

    units = [(a, b) for a in range(nb) for b in range(M_HEADS)]

    def head(ref, gi, h, base=0):
        return ref[gi, :, base + h * LANES:base + (h + 1) * LANES]

    keep_t = (ti >= si) if reverse else (ti <= si)

    for u, (gi, h) in enumerate(units):
        q = head(qk_ref, gi, h)
        qk_scr[u] = _dot_nt(head(qk_ref, gi, h, M_WIDTH), q)
        qc_scr[u] = _dot_nt(c_scr[gi, h].astype(BF16), q)
        qn_scr[u] = _dot_nt(jnp.broadcast_to(n_scr[gi, h], (8, LANES)).astype(BF16), q)

    for u, (gi, h) in enumerate(units):
        if h == 0:
            g = g_ref[gi] + gb_ref[...]
            f_hi, f_mid, f_lo = _split3(jax.nn.log_sigmoid(g))
            bsum = _dot(tri, f_hi) + _dot(tri, f_mid) + _dot(tri, f_lo)
            g_t = g.T
            b_t = bsum.T
        m_prev = m_scr[gi, h][:, 0:1]
        b_row = b_t[f_off + h:f_off + h + 1, :]
        i_row = g_t[i_off + h:i_off + h + 1, :]
        b_end = b_row[:, end:end + 1]
        d_col = g[:, i_off + h:i_off + h + 1] - bsum[:, f_off + h:f_off + h + 1]

        log_w = jnp.where(keep_t, b_row + d_col, -jnp.inf)
        carry_log = b_row + m_prev
        m_t = jnp.maximum(carry_log, jnp.max(log_w, axis=0, keepdims=True))
        s_t = qk_scr[u] * jnp.exp(log_w - m_t)
        c_scale = jnp.exp(carry_log - m_t)
        den = jnp.sum(s_t, axis=0, keepdims=True) + c_scale * qn_scr[u][0:1, :]
        inv = 1.0 / jnp.maximum(jnp.abs(den), jnp.exp(-m_t))
        s_scr[u] = (s_t * inv).astype(BF16)
        qc_scr[u] = qc_scr[u] * (c_scale * inv)

        m_new = jnp.maximum(b_end + m_prev, jnp.max(b_end - b_row + i_row, axis=-1, keepdims=True))
        u_row = jnp.exp(b_end - b_row + i_row - m_new)
        decay = jnp.exp(b_end + m_prev - m_new)
        v_t = head(v_ref, gi, h).astype(F32).T
        vt_scr[u] = v_t.astype(BF16)
        vu_scr[u] = (v_t * u_row).astype(BF16)
        u8 = jnp.broadcast_to(u_row, (8, L)).astype(BF16)
        n_scr[gi, h] = decay * n_scr[gi, h] + _dot(u8, head(qk_ref, gi, h, M_WIDTH))[0:1, :]
        m_scr[gi, h] = jnp.broadcast_to(m_new, (1, LANES))
        dec_scr[u] = jnp.broadcast_to(decay, (1, LANES))

    for u, (gi, h) in enumerate(units):
        qc_scr[u] = _dot(vt_scr[u], s_scr[u]) + qc_scr[u]
        c_scr[gi, h] = dec_scr[u][:, 0:1] * c_scr[gi, h] + _dot(vu_scr[u], head(qk_ref, gi, h, M_WIDTH))

    for u, (gi, h) in enumerate(units):
        rows = slice(h * M_HEAD_DIM, (h + 1) * M_HEAD_DIM)
        if final:
            hs = qc_scr[u] + hp_ref[gi, rows, :]
            mu = jnp.mean(hs, axis=0, keepdims=True)
            var = jnp.mean(jnp.square(hs - mu), axis=0, keepdims=True)
            hn = ((hs - mu) * lax.rsqrt(var + EPS)).T
            y = hn * nw_ref[:, h * LANES:(h + 1) * LANES] * jax.nn.sigmoid(head(op_ref, gi, h).astype(F32))
            out_ref[gi, :, h * LANES:(h + 1) * LANES] = y.astype(out_ref.dtype)
        else:
            out_ref[gi, rows, :] = qc_scr[u]

    @pl.when(c_idx == n_c - 1)
    def _():
        cf_ref[...] = c_scr[...]
        nf_ref[...] = n_scr[...]
        mf_ref[...] = m_scr[...]


def _mlstm_scan(qk, px, gates, gate_b, state, seq_len, reverse, hprev=None, norm_w=None):
    t = qk.shape[0]
    bsz = t // seq_len
    n_c = seq_len // M_CHUNK
    final = hprev is not None
    nb = 4 if bsz % 4 == 0 else (2 if bsz % 2 == 0 else 1)
    qk, px, gates = (a.reshape(bsz, seq_len, a.shape[1]) for a in (qk, px, gates))

    def chunk(c):
        return (n_c - 1 - c) if reverse else c

    state_specs = [
        pl.BlockSpec((nb, M_HEADS, M_HEAD_DIM, M_HEAD_DIM), lambda b, c: (b, 0, 0, 0)),
        pl.BlockSpec((nb, M_HEADS, 1, LANES), lambda b, c: (b, 0, 0, 0)),
        pl.BlockSpec((nb, M_HEADS, 1, LANES), lambda b, c: (b, 0, 0, 0)),
    ]
    in_specs = [
        pl.BlockSpec((nb, M_CHUNK, 2 * M_WIDTH), lambda b, c: (b, chunk(c), 0)),
        pl.BlockSpec((nb, M_CHUNK, M_WIDTH), lambda b, c: (b, chunk(c), COL_V)),
        pl.BlockSpec((nb, M_CHUNK, LANES), lambda b, c: (b, chunk(c), 0)),
        pl.BlockSpec((1, LANES), lambda b, c: (0, 0)),
    ] + state_specs
    args = [qk, px, gates, gate_b, *state]
    t_spec = pl.BlockSpec((nb, M_WIDTH, M_CHUNK), lambda b, c: (b, 0, chunk(c)))
    if final:
        in_specs += [
            t_spec,
            pl.BlockSpec((nb, M_CHUNK, M_WIDTH), lambda b, c: (b, chunk(c), COL_O)),
            pl.BlockSpec((1, M_WIDTH), lambda b, c: (0, 0)),
        ]
        args += [hprev, px, norm_w]
        out_spec = pl.BlockSpec((nb, M_CHUNK, M_WIDTH), lambda b, c: (b, chunk(c), 0))
        out_shape = jax.ShapeDtypeStruct((bsz, seq_len, M_WIDTH), BF16)
    else:
        out_spec = t_spec
        out_shape = jax.ShapeDtypeStruct((bsz, M_WIDTH, seq_len), F32)
    units = nb * M_HEADS
    out, *fin = pl.pallas_call(
        functools.partial(_mlstm_kernel, reverse=reverse, final=final, nb=nb),
        grid=(bsz // nb, n_c),
        in_specs=in_specs,
        out_specs=[out_spec] + state_specs,
        out_shape=[
            out_shape,
            jax.ShapeDtypeStruct((bsz, M_HEADS, M_HEAD_DIM, M_HEAD_DIM), F32),
            jax.ShapeDtypeStruct((bsz, M_HEADS, 1, LANES), F32),
            jax.ShapeDtypeStruct((bsz, M_HEADS, 1, LANES), F32),
        ],
        scratch_shapes=[
            pltpu.VMEM((nb, M_HEADS, M_HEAD_DIM, M_HEAD_DIM), F32),
            pltpu.VMEM((nb, M_HEADS, 1, LANES), F32),
            pltpu.VMEM((nb, M_HEADS, 1, LANES), F32),
            pltpu.VMEM((units, M_CHUNK, M_CHUNK), F32),
            pltpu.VMEM((units, M_HEAD_DIM, M_CHUNK), F32),
            pltpu.VMEM((units, 8, M_CHUNK), F32),
            pltpu.VMEM((units, M_CHUNK, M_CHUNK), BF16),
            pltpu.VMEM((units, M_HEAD_DIM, M_CHUNK), BF16),
            pltpu.VMEM((units, M_HEAD_DIM, M_CHUNK), BF16),
            pltpu.VMEM((units, 1, LANES), F32),
        ],
        compiler_params=_params(("parallel", "arbitrary")),
        name="mlstm_rev" if reverse else "mlstm_fwd",
    )(*args)
    return [out.reshape(t, M_WIDTH) if final else out] + fin


def _mlstm_zero_state(bsz):
    return (jnp.zeros((bsz, M_HEADS, M_HEAD_DIM, M_HEAD_DIM), F32),
            jnp.zeros((bsz, M_HEADS, 1, LANES), F32),
            jnp.zeros((bsz, M_HEADS, 1, LANES), F32))


def _mlstm_bidir(qk, px, gates, gate_b, norm_w, state_f, state_b, seq_len):
    h_f, *fin_f = _mlstm_scan(qk, px, gates, gate_b, state_f, seq_len, False)
    y, *fin_b = _mlstm_scan(qk, px, gates, gate_b, state_b, seq_len, True, hprev=h_f, norm_w=norm_w)
    return y, tuple(fin_f), tuple(fin_b)


NA_ROWS = 8
NA_STEP = 4


def _na_kernel(q_ref, kp_ref, kc_ref, kn_ref, vp_ref, vc_ref, vn_ref, kx_ref, vx_ref, *rest, rows):
    bias_refs = rest[:NA_STEP]
    o_ref, kwin, vwin, s_scr, p_scr = rest[NA_STEP:]
    step = pl.program_id(1)
    blk = NA_ROWS * GRID_W

    @pl.when(step % (NA_ROWS // NA_STEP) == 0)
    def _():
        kwin[0:blk, :] = kp_ref[...]
        kwin[blk:2 * blk, :] = kc_ref[...]
        kwin[2 * blk:3 * blk, :] = kn_ref[...]
        vwin[0:blk, :] = vp_ref[...]
        vwin[blk:2 * blk, :] = vc_ref[...]
        vwin[2 * blk:3 * blk, :] = vn_ref[...]

    g = step // (NA_ROWS // NA_STEP)
    n_loc = WIN_R * GRID_W
    lane = lax.broadcasted_iota(jnp.int32, (GRID_W, LANES), 1)
    scale = NA_HEAD_DIM ** -0.5
    n_ctx = kx_ref.shape[0]
    units = [(kk, hp) for kk in range(NA_STEP) for hp in range(NA_HEADS // 2)]
    offs = []
    for kk in range(NA_STEP):
        r = step * NA_STEP + kk
        r0 = jnp.clip(r - WIN_R // 2, 0, rows - WIN_R)
        offs.append(pl.multiple_of((r0 - NA_ROWS * g + NA_ROWS) * GRID_W, GRID_W))

    for u, (kk, hp) in enumerate(units):
        cols = slice(hp * LANES, (hp + 1) * LANES)
        q2 = q_ref[kk * GRID_W:(kk + 1) * GRID_W, cols] * scale
        zero = jnp.zeros_like(q2)
        qs = jnp.concatenate([jnp.where(lane < NA_HEAD_DIM, q2, zero), jnp.where(lane < NA_HEAD_DIM, zero, q2)],
                             axis=0)
        s_scr[u, :, 0:n_loc] = _dot_nt(qs, kwin[pl.ds(offs[kk], n_loc), cols])
        s_scr[u, :, n_loc:n_loc + n_ctx] = _dot_nt(qs, kx_ref[:, cols])

    for u, (kk, hp) in enumerate(units):
        s_loc = s_scr[u, :, 0:n_loc] + bias_refs[kk][0, hp]
        s_ctx = s_scr[u, :, n_loc:n_loc + n_ctx]
        m = jnp.maximum(jnp.max(s_loc, axis=-1, keepdims=True), jnp.max(s_ctx, axis=-1, keepdims=True))
        p_loc = jnp.exp(s_loc - m)
        p_ctx = jnp.exp(s_ctx - m)
        inv = 1.0 / (jnp.sum(p_loc, axis=-1, keepdims=True) + jnp.sum(p_ctx, axis=-1, keepdims=True))
        p_scr[u, :, 0:n_loc] = (p_loc * inv).astype(BF16)
        p_scr[u, :, n_loc:n_loc + n_ctx] = (p_ctx * inv).astype(BF16)

    for u, (kk, hp) in enumerate(units):
        cols = slice(hp * LANES, (hp + 1) * LANES)
        o2 = _dot(p_scr[u, :, 0:n_loc], vwin[pl.ds(offs[kk], n_loc), cols])
        o2 += _dot(p_scr[u, :, n_loc:n_loc + n_ctx], vx_ref[:, cols])
        out = jnp.where(lane < NA_HEAD_DIM, o2[0:GRID_W], o2[GRID_W:2 * GRID_W])
        o_ref[kk * GRID_W:(kk + 1) * GRID_W, cols] = out.astype(o_ref.dtype)


def _na_bias_table(rpb):
    qc = jnp.arange(GRID_W)[:, None]
    kc = jnp.arange(GRID_W)[None, :]
    cs = jnp.clip(qc - WIN_C // 2, 0, GRID_W - WIN_C)
    ok = (kc >= cs) & (kc < cs + WIN_C)
    dc = jnp.clip(kc - qc + WIN_C - 1, 0, 2 * WIN_C - 2)
    dr = jnp.arange(WIN_R)[:, None] + jnp.arange(WIN_R)[None, :]
    tab = rpb.astype(F32)[:, dr][:, :, :, dc]
    tab = jnp.where(ok[None, None, None], tab, NEG_BIG)
    tab = jnp.transpose(tab, (1, 0, 3, 2, 4))
    return tab.reshape(WIN_R, NA_HEADS // 2, 2 * GRID_W, WIN_R * GRID_W)


def _na_attention(px, pc, bias, seq_len, ctx_len):
    t = px.shape[0]
    bsz = t // seq_len
    rows = seq_len // GRID_W
    n_g = rows // NA_ROWS
    blk = NA_ROWS * GRID_W

    steps = rows // NA_STEP
    per_blk = NA_ROWS // NA_STEP

    def kv_spec(col, shift):
        def imap(b, s):
            g = jnp.clip(s // per_blk + shift, 0, n_g - 1)
            return (b * n_g + g, col)
        return pl.BlockSpec((blk, NA_WIDTH), imap)

    def bias_spec(kk):
        def imap(b, s):
            r = s * NA_STEP + kk
            r0 = jnp.clip(r - WIN_R // 2, 0, rows - WIN_R)
            return (r0 - r + WIN_R - 1, 0, 0, 0)
        return pl.BlockSpec((1, NA_HEADS // 2, 2 * GRID_W, WIN_R * GRID_W), imap)

    return pl.pallas_call(
        functools.partial(_na_kernel, rows=rows),
        grid=(bsz, steps),
        in_specs=[
            pl.BlockSpec((NA_STEP * GRID_W, NA_WIDTH), lambda b, s: (b * steps + s, COL_NAQ)),
            kv_spec(COL_NAK, -1), kv_spec(COL_NAK, 0), kv_spec(COL_NAK, 1),
            kv_spec(COL_NAV, -1), kv_spec(COL_NAV, 0), kv_spec(COL_NAV, 1),
            pl.BlockSpec((ctx_len, NA_WIDTH), lambda b, s: (b, COL_NAK)),
            pl.BlockSpec((ctx_len, NA_WIDTH), lambda b, s: (b, COL_NAV)),
        ] + [bias_spec(kk) for kk in range(NA_STEP)],
        out_specs=pl.BlockSpec((NA_STEP * GRID_W, NA_WIDTH), lambda b, s: (b * steps + s, 0)),
        out_shape=jax.ShapeDtypeStruct((t, NA_WIDTH), BF16),
        scratch_shapes=[
            pltpu.VMEM((3 * blk, NA_WIDTH), BF16), pltpu.VMEM((3 * blk, NA_WIDTH), BF16),
            pltpu.VMEM((NA_STEP * NA_HEADS // 2, 2 * GRID_W, WIN_R * GRID_W + ctx_len), F32),
            pltpu.VMEM((NA_STEP * NA_HEADS // 2, 2 * GRID_W, WIN_R * GRID_W + ctx_len), BF16),
        ],
        compiler_params=_params(("parallel", "arbitrary")),
        name="na_attn",
    )(px, px, px, px, px, px, px, pc, pc, *([bias] * NA_STEP))


def _ctx_attn_kernel(q_ref, k_ref, v_ref, o_ref):
    n = q_ref.shape[0]
    lane = lax.broadcasted_iota(jnp.int32, (n, LANES), 1)
    scale = NA_HEAD_DIM ** -0.5
    for hp in range(NA_HEADS // 2):
        cols = slice(hp * LANES, (hp + 1) * LANES)
        q2 = q_ref[:, cols] * scale
        k2 = k_ref[:, cols]
        v2 = v_ref[:, cols]
        outs = []
        for sub in range(2):
            sel = (lane < NA_HEAD_DIM) if sub == 0 else (lane >= NA_HEAD_DIM)
            s = _dot_nt(jnp.where(sel, q2, jnp.zeros_like(q2)), k2)
            p = jnp.exp(s - jnp.max(s, axis=-1, keepdims=True))
            outs.append(_dot(p.astype(BF16), v2) / jnp.sum(p, axis=-1, keepdims=True))
        o_ref[:, cols] = jnp.where(lane < NA_HEAD_DIM, outs[0], outs[1]).astype(o_ref.dtype)


def _ctx_attention(pc, ctx_len):
    t = pc.shape[0]
    return pl.pallas_call(
        _ctx_attn_kernel,
        grid=(t // ctx_len,),
        in_specs=[pl.BlockSpec((ctx_len, NA_WIDTH), lambda b, col=col: (b, col))
                  for col in (COL_NAQ, COL_NAK, COL_NAV)],
        out_specs=pl.BlockSpec((ctx_len, NA_WIDTH), lambda b: (b, 0)),
        out_shape=jax.ShapeDtypeStruct((t, NA_WIDTH), BF16),
        compiler_params=_params(("parallel",)),
        name="ctx_attn",
    )(pc, pc, pc)


def _cmul(a, b):
    return a[0] * b[0] - a[1] * b[1], a[0] * b[1] + a[1] * b[0]


def _s5_tables(lam_re, lam_im, log_dt, b_re, b_im, c_re, c_im):
    L, G, S, H = S5_CHUNK, S5_GROUPS, S5_STATE, S5_GROUP
    lr = jnp.minimum(lam_re.astype(F32), S5_MAX_RE)
    li = lam_im.astype(F32)
    dt = jnp.exp(log_dt.astype(F32))[..., None]
    zr, zi = lr * dt, li * dt
    d = jnp.arange(L + 1, dtype=F32)[None, :, None, None]
    mag = jnp.exp(zr[:, None] * d)
    pw = (mag * jnp.cos(zi[:, None] * d), mag * jnp.sin(zi[:, None] * d))
    lb = (pw[0][:, 1], pw[1][:, 1])
    den = lr * lr + li * li
    ratio = (((lb[0] - 1.0) * lr + lb[1] * li) / den, (lb[1] * lr - (lb[0] - 1.0) * li) / den)
    bb = _cmul((ratio[0][..., None], ratio[1][..., None]), (b_re.astype(F32), b_im.astype(F32)))
    cc = (c_re.astype(F32), c_im.astype(F32))

    def kern(x):
        cl = _cmul((cc[0][x][None], cc[1][x][None]), (pw[0][x, :L, :, None, :], pw[1][x, :L, :, None, :]))
        return jnp.einsum('dghp,gpj->dghj', cl[0], bb[0][x]) - jnp.einsum('dghp,gpj->dghj', cl[1], bb[1][x])

    def state_in(x, steps):
        p = (pw[0][x][steps][:, :, None, :], pw[1][x][steps][:, :, None, :])
        r = _cmul(p, (jnp.swapaxes(bb[0][x], 1, 2)[None], jnp.swapaxes(bb[1][x], 1, 2)[None]))
        return jnp.transpose(r[0], (1, 0, 2, 3)), jnp.transpose(r[1], (1, 0, 2, 3))

    def state_out(x, steps):
        p = (pw[0][x][steps][:, :, None, :], pw[1][x][steps][:, :, None, :])
        r = _cmul(p, (cc[0][x][None], cc[1][x][None]))
        return jnp.transpose(r[0], (1, 3, 0, 2)), jnp.transpose(r[1], (1, 3, 0, 2))

    steps = jnp.arange(L)
    wf = state_in(0, L - 1 - steps)
    wb = state_in(1, steps)
    vf = state_out(0, steps + 1)
    vb = state_out(1, L - steps)
    Q, A = S5_QUADS, S5_QUAD_GROUPS
    kd = jnp.stack([kern(0), kern(1)]).reshape(2, L, Q, A, H, H)
    kd = jnp.transpose(kd, (0, 2, 1, 5, 3, 4)).reshape(2, Q, L, H, A * H)
    wk = jnp.stack([wf[0], wf[1], wb[0], wb[1]]).reshape(4, Q, A // 2, 2, L, H, S)
    wd = jnp.transpose(wk, (0, 1, 4, 2, 5, 3, 6)).reshape(4, Q, L, (A // 2) * H, 2 * S)
    vk = jnp.stack([vf[0], -vf[1], vb[0], -vb[1]]).reshape(4, Q, A, S, L, H)
    vd = jnp.transpose(vk, (0, 1, 4, 3, 2, 5)).reshape(4, Q, L, S, A * H)
    a_re = pw[0][:, L].reshape(2, 1, G * S)
    a_im = pw[1][:, L].reshape(2, 1, G * S)
    return wd, kd, vd, a_re, a_im


def _s5_operand(u_ref):
    return jnp.concatenate([u_ref[:, s, :] for s in range(S5_CHUNK)], axis=1).astype(BF16)


def _s5_expand_w(wd_ref, w_scr):
    H, half = S5_GROUP, S5_STATE
    w_scr[...] = jnp.zeros_like(w_scr)
    lane = lax.broadcasted_iota(jnp.int32, (H, LANES), 1)
    for k in range(4):
        for s in range(S5_CHUNK):
            for c in range(S5_QUAD_GROUPS // 2):
                blk = wd_ref[k, 0, s, c * H:(c + 1) * H, :]
                cols = slice(k * S5_QSTATE + c * LANES, k * S5_QSTATE + (c + 1) * LANES)
                row = s * LANES + 2 * c * H
                w_scr[row:row + H, cols] = jnp.where(lane < half, blk, 0.0).astype(BF16)
                w_scr[row + H:row + 2 * H, cols] = jnp.where(lane >= half, blk, 0.0).astype(BF16)


def _s5_expand_mv(kd_ref, vd_ref, m_scr, v_scr):
    H, L = S5_GROUP, S5_CHUNK
    rg = lax.broadcasted_iota(jnp.int32, (LANES, LANES), 0) // H
    lg = lax.broadcasted_iota(jnp.int32, (LANES, LANES), 1) // H
    tiles = []
    for x in range(2):
        tiles.append([jnp.where(rg == lg, jnp.concatenate([kd_ref[x, 0, d]] * S5_QUAD_GROUPS, axis=0), 0.0)
                      for d in range(L)])
    for s in range(L):
        for t in range(L):
            blk = tiles[0][t - s] if t > s else (tiles[1][s - t] if s > t else tiles[0][0] + tiles[1][0])
            m_scr[s * LANES:(s + 1) * LANES, t * LANES:(t + 1) * LANES] = blk.astype(BF16)
    lg2 = lax.broadcasted_iota(jnp.int32, (S5_STATE, LANES), 1) // H
    for k in range(4):
        for t in range(L):
            blk = vd_ref[k, 0, t]
            for b in range(S5_QUAD_GROUPS):
                row = k * S5_QSTATE + b * S5_STATE
                v_scr[row:row + S5_STATE, t * LANES:(t + 1) * LANES] = jnp.where(lg2 == b, blk, 0.0).astype(BF16)


def _s5_local_kernel(u_ref, wd_ref, sfr_ref, sfi_ref, sbr_ref, sbi_ref, w_scr):
    @pl.when(pl.program_id(1) == 0)
    def _():
        _s5_expand_w(wd_ref, w_scr)

    s = _dot(_s5_operand(u_ref), w_scr[...])
    n = S5_QSTATE
    sfr_ref[...] = s[:, 0:n]
    sfi_ref[...] = s[:, n:2 * n]
    sbr_ref[...] = s[:, 2 * n:3 * n]
    sbi_ref[...] = s[:, 3 * n:4 * n]


def _s5_local_states(u3, wd):
    rows = u3.shape[0]
    tr = min(256, rows)
    assert rows % tr == 0
    out = jax.ShapeDtypeStruct((rows, S5_GROUPS * S5_STATE), F32)
    return pl.pallas_call(
        _s5_local_kernel,
        grid=(S5_QUADS, rows // tr),
        in_specs=[
            pl.BlockSpec((tr, S5_CHUNK, LANES), lambda q, i: (i, 0, q)),
            pl.BlockSpec((4, 1) + wd.shape[2:], lambda q, i: (0, q, 0, 0, 0)),
        ],
        out_specs=[pl.BlockSpec((tr, S5_QSTATE), lambda q, i: (i, q))] * 4,
        out_shape=[out] * 4,
        scratch_shapes=[pltpu.VMEM((S5_QCOLS, 4 * S5_QSTATE), BF16)],
        compiler_params=_params(("arbitrary", "arbitrary")),
        name="s5_local",
    )(u3, wd)


def _s5_scan_kernel(sr_ref, si_ref, ar_ref, ai_ref, h0r_ref, h0i_ref, hr_ref, hi_ref, fr_ref, fi_ref,
                    hr_scr, hi_scr, *, reverse):
    i = pl.program_id(1)
    n_i = pl.num_programs(1)
    steps = sr_ref.shape[1]

    @pl.when(i == 0)
    def _():
        hr_scr[...] = h0r_ref[...]
        hi_scr[...] = h0i_ref[...]

    ar = ar_ref[...]
    ai = ai_ref[...]

    def body(j, carry):
        hr, hi = carry
        c = (steps - 1 - j) if reverse else j
        hr_ref[:, c, :] = hr
        hi_ref[:, c, :] = hi
        nr = ar * hr - ai * hi + sr_ref[:, c, :]
        ni = ar * hi + ai * hr + si_ref[:, c, :]
        return nr, ni

    hr, hi = lax.fori_loop(0, steps, body, (hr_scr[...], hi_scr[...]))
    hr_scr[...] = hr
    hi_scr[...] = hi

    @pl.when(i == n_i - 1)
    def _():
        fr_ref[...] = hr
        fi_ref[...] = hi


def _s5_chunk_scan(s_re, s_im, a_re, a_im, h0_re, h0_im, bsz, reverse):
    _, n_c, width = s_re.shape
    tr = min(64, n_c)
    tc = 512
    n_i = n_c // tr

    def rmap(j, i):
        return (0, (n_i - 1 - i) if reverse else i, j)

    big = jax.ShapeDtypeStruct((bsz, n_c, width), F32)
    small = jax.ShapeDtypeStruct((bsz, width), F32)
    return pl.pallas_call(
        functools.partial(_s5_scan_kernel, reverse=reverse),
        grid=(width // tc, n_i),
        in_specs=[
            pl.BlockSpec((bsz, tr, tc), rmap), pl.BlockSpec((bsz, tr, tc), rmap),
            pl.BlockSpec((1, tc), lambda j, i: (0, j)), pl.BlockSpec((1, tc), lambda j, i: (0, j)),
            pl.BlockSpec((bsz, tc), lambda j, i: (0, j)), pl.BlockSpec((bsz, tc), lambda j, i: (0, j)),
        ],
        out_specs=[
            pl.BlockSpec((bsz, tr, tc), rmap), pl.BlockSpec((bsz, tr, tc), rmap),
            pl.BlockSpec((bsz, tc), lambda j, i: (0, j)), pl.BlockSpec((bsz, tc), lambda j, i: (0, j)),
        ],
        out_shape=[big, big, small, small],
        scratch_shapes=[pltpu.VMEM((bsz, tc), F32), pltpu.VMEM((bsz, tc), F32)],
        compiler_params=_params(("parallel", "arbitrary")),
        name="s5_scan_rev" if reverse else "s5_scan_fwd",
    )(s_re, s_im, a_re, a_im, h0_re, h0_im)


def _s5_out_kernel(u_ref, kd_ref, vd_ref, hfr_ref, hfi_ref, hbr_ref, hbi_ref, y_ref, m_scr, v_scr):
    @pl.when(pl.program_id(1) == 0)
    def _():
        _s5_expand_mv(kd_ref, vd_ref, m_scr, v_scr)

    n = S5_QSTATE
    y = _dot(_s5_operand(u_ref), m_scr[...])
    for k, h_ref in enumerate((hfr_ref, hfi_ref, hbr_ref, hbi_ref)):
        y += _dot(h_ref[...].astype(BF16), v_scr[k * n:(k + 1) * n, :])
    for t in range(S5_CHUNK):
        y_ref[:, t, :] = y[:, t * LANES:(t + 1) * LANES]


def _s5_outputs(u3, kd, vd, h_states):
    rows = u3.shape[0]
    tr = min(256, rows)
    return pl.pallas_call(
        _s5_out_kernel,
        grid=(S5_QUADS, rows // tr),
        in_specs=[
            pl.BlockSpec((tr, S5_CHUNK, LANES), lambda q, i: (i, 0, q)),
            pl.BlockSpec((2, 1) + kd.shape[2:], lambda q, i: (0, q, 0, 0, 0)),
            pl.BlockSpec((4, 1) + vd.shape[2:], lambda q, i: (0, q, 0, 0, 0)),
        ] + [pl.BlockSpec((tr, S5_QSTATE), lambda q, i: (i, q))] * 4,
        out_specs=pl.BlockSpec((tr, S5_CHUNK, LANES), lambda q, i: (i, 0, q)),
        out_shape=jax.ShapeDtypeStruct((rows, S5_CHUNK, S5_WIDTH), F32),
        scratch_shapes=[pltpu.VMEM((S5_QCOLS, S5_QCOLS), BF16), pltpu.VMEM((4 * S5_QSTATE, S5_QCOLS), BF16)],
        compiler_params=_params(("arbitrary", "arbitrary")),
        name="s5_out",
    )(u3, kd, vd, *h_states)


def _s5_branch(u, tables, seq_len, init, want_y):
    w_q, m_q, v_q, a_re, a_im = tables
    t = u.shape[0]
    bsz = t // seq_len
    n_c = seq_len // S5_CHUNK
    width = S5_GROUPS * S5_STATE
    u3 = u.reshape(t // S5_CHUNK, S5_CHUNK, S5_WIDTH)
    s_loc = [s.reshape(bsz, n_c, width) for s in _s5_local_states(u3, w_q)]
    hfr, hfi, ffr, ffi = _s5_chunk_scan(s_loc[0], s_loc[1], a_re[0], a_im[0], init[0], init[1], bsz, False)
    hbr, hbi, fbr, fbi = _s5_chunk_scan(s_loc[2], s_loc[3], a_re[1], a_im[1], init[2], init[3], bsz, True)
    y = None
    if want_y:
        h_states = [h.reshape(bsz * n_c, width) for h in (hfr, hfi, hbr, hbi)]
        y = _s5_outputs(u3, m_q, v_q, h_states).reshape(t, S5_WIDTH)
    return y, (ffr, ffi, fbr, fbi)


def _gelu_tanh(x):
    return 0.5 * x * (1.0 + jnp.tanh(math.sqrt(2.0 / math.pi) * (x + 0.044715 * (x * x * x))))


def _merge_kernel(x_ref, ym_ref, yn_ref, ys_ref, u_ref, gm_ref, gn_ref, gs_ref, d_ref, gw_ref, gb_ref,
                  wm_ref, wn_ref, ws_ref, wo_ref, gate_ref, npost_ref, o_ref):
    ys = ys_ref[...] + d_ref[...] * u_ref[...]
    g = _gelu_tanh(ys)
    gg = g * jax.nn.sigmoid(_dot(g.astype(BF16), gw_ref[...]) + gb_ref[...])
    y = jax.nn.sigmoid(gm_ref[...].astype(F32)) * _dot(ym_ref[...], wm_ref[...])
    y += jax.nn.sigmoid(gn_ref[...].astype(F32)) * _dot(yn_ref[...], wn_ref[...])
    y += jax.nn.sigmoid(gs_ref[...].astype(F32)) * _dot(gg.astype(BF16), ws_ref[...])
    out = _dot(y.astype(BF16), wo_ref[...])
    o_ref[...] = x_ref[...] + gate_ref[0] * _rms(out, npost_ref[...])


def _merge(x, ym, yn, ys, u, px, wts, gate, npost, rows_per_mod):
    t = x.shape[0]
    tm = min(512, rows_per_mod)
    per = rows_per_mod // tm
    s5_d, glu_w, glu_b, w_m, w_n, w_s, w_o = wts
    row = lambda i: (i, 0)
    const = lambda i: (0, 0)
    return pl.pallas_call(
        _merge_kernel,
        grid=(t // tm,),
        in_specs=[
            pl.BlockSpec((tm, D_MODEL), row),
            pl.BlockSpec((tm, M_WIDTH), row),
            pl.BlockSpec((tm, NA_WIDTH), row),
            pl.BlockSpec((tm, S5_WIDTH), row),
            pl.BlockSpec((tm, S5_WIDTH), row),
            pl.BlockSpec((tm, D_MODEL), lambda i: (i, COL_GATE)),
            pl.BlockSpec((tm, D_MODEL), lambda i: (i, COL_GATE + 1)),
            pl.BlockSpec((tm, D_MODEL), lambda i: (i, COL_GATE + 2)),
            pl.BlockSpec((1, S5_WIDTH), const),
            pl.BlockSpec((S5_WIDTH, S5_WIDTH), const),
            pl.BlockSpec((1, S5_WIDTH), const),
            pl.BlockSpec((M_WIDTH, D_MODEL), const),
            pl.BlockSpec((NA_WIDTH, D_MODEL), const),
            pl.BlockSpec((S5_WIDTH, D_MODEL), const),
            pl.BlockSpec((D_MODEL, D_MODEL), const),
            pl.BlockSpec((1, 1, D_MODEL), lambda i: (i // per, 0, 0)),
            pl.BlockSpec((1, D_MODEL), const),
        ],
        out_specs=pl.BlockSpec((tm, D_MODEL), row),
        out_shape=jax.ShapeDtypeStruct((t, D_MODEL), F32),
        compiler_params=_params(("parallel",)),
        name="merge",
    )(x, ym, yn, ys, u, px, px, px, s5_d, glu_w, glu_b, w_m, w_n, w_s, w_o, gate, npost)


def _swiglu(h, wg, wu, wd):
    a = _dot(h, wg)
    a = a * jax.nn.sigmoid(a) * _dot(h, wu)
    return _dot(a.astype(BF16), wd)


def _ffn_kernel(x_ref, g_ref, sh_ref, sc_ref, wg_ref, wu_ref, wd_ref, gate_ref, npost_ref,
                o_ref, hn_ref, acc_ref):
    j = pl.program_id(1)

    @pl.when(j == 0)
    def _():
        h = _rms(x_ref[...], g_ref[...]) * (1.0 + sc_ref[0]) + sh_ref[0]
        hn_ref[...] = h.astype(BF16)
        acc_ref[...] = jnp.zeros_like(acc_ref)

    acc_ref[...] += _swiglu(hn_ref[...], wg_ref[...], wu_ref[...], wd_ref[...])

    @pl.when(j == pl.num_programs(1) - 1)
    def _():
        o_ref[...] = x_ref[...] + gate_ref[0] * _rms(acc_ref[...], npost_ref[...])


def _ffn(x, g, shift, scale, w_gate, w_up, w_down, gate, npost, rows_per_mod):
    t = x.shape[0]
    ff = w_gate.shape[1]
    ff_tile = ff // 2
    tm = min(512, rows_per_mod)
    per = rows_per_mod // tm
    assert ff_tile % LANES == 0
    return pl.pallas_call(
        _ffn_kernel,
        grid=(t // tm, ff // ff_tile),
        in_specs=[
            pl.BlockSpec((tm, D_MODEL), lambda i, j: (i, 0)),
            pl.BlockSpec((1, D_MODEL), lambda i, j: (0, 0)),
            pl.BlockSpec((1, 1, D_MODEL), lambda i, j: (i // per, 0, 0)),
            pl.BlockSpec((1, 1, D_MODEL), lambda i, j: (i // per, 0, 0)),
            pl.BlockSpec((D_MODEL, ff_tile), lambda i, j: (0, j)),
            pl.BlockSpec((D_MODEL, ff_tile), lambda i, j: (0, j)),
            pl.BlockSpec((ff_tile, D_MODEL), lambda i, j: (j, 0)),
            pl.BlockSpec((1, 1, D_MODEL), lambda i, j: (i // per, 0, 0)),
            pl.BlockSpec((1, D_MODEL), lambda i, j: (0, 0)),
        ],
        out_specs=pl.BlockSpec((tm, D_MODEL), lambda i, j: (i, 0)),
        out_shape=jax.ShapeDtypeStruct((t, D_MODEL), F32),
        scratch_shapes=[pltpu.VMEM((tm, D_MODEL), BF16), pltpu.VMEM((tm, D_MODEL), F32)],
        compiler_params=_params(("parallel", "arbitrary")),
        name="ffn",
    )(x, g, shift, scale, w_gate, w_up, w_down, gate, npost)


def _moe_kernel(x_ref, g_ref, sh_ref, sc_ref, rh_ref, rl_ref, wg_ref, wu_ref, wd_ref, gate_ref, npost_ref,
                o_ref, hn_ref, acc_ref, comb_ref, rank_c_ref, rank_r_ref, mask_r_ref, xc_ref, yc_ref, cnt_ref):
    e = pl.program_id(1)
    j = pl.program_id(2)
    tm = x_ref.shape[0]
    cap = MOE_CAP

    @pl.when((e == 0) & (j == 0))
    def _():
        h = _rms(x_ref[...], g_ref[...]) * (1.0 + sc_ref[0]) + sh_ref[0]
        hi = h.astype(BF16)
        hn_ref[...] = hi
        acc_ref[...] = jnp.zeros_like(acc_ref)
        lo = (h - hi.astype(F32)).astype(BF16)
        logits = _dot(hi, rh_ref[...]) + _dot(lo, rh_ref[...]) + _dot(hi, rl_ref[...])
        lane = lax.broadcasted_iota(jnp.int32, logits.shape, 1)
        logits = jnp.where(lane < N_EXPERTS, logits, -jnp.inf)
        m1 = jnp.max(logits, axis=-1, keepdims=True)
        i1 = jnp.min(jnp.where(logits == m1, lane, LANES), axis=-1, keepdims=True)
        rest = jnp.where(lane == i1, -jnp.inf, logits)
        m2 = jnp.max(rest, axis=-1, keepdims=True)
        i2 = jnp.min(jnp.where(rest == m2, lane, LANES), axis=-1, keepdims=True)
        e2 = jnp.exp(m2 - m1)
        comb_ref[...] = jnp.where(lane == i1, 1.0 / (1.0 + e2), 0.0) + jnp.where(lane == i2, e2 / (1.0 + e2), 0.0)
        sel = jnp.where((lane == i1) | (lane == i2), 1.0, 0.0)
        ti = lax.broadcasted_iota(jnp.int32, (tm, tm), 0)
        si = lax.broadcasted_iota(jnp.int32, (tm, tm), 1)
        before = jnp.where(si < ti, 1.0, 0.0).astype(BF16)
        rank_c_ref[...] = _dot(before, sel.astype(BF16))
        sel_t = sel.T
        after = jnp.where(ti < si, 1.0, 0.0).astype(BF16)
        rank_r_ref[...] = _dot(sel_t.astype(BF16), after)
        mask_r_ref[...] = sel_t
        counts = jnp.sum(sel_t, axis=1, keepdims=True)
        for k in range(N_EXPERTS):
            cnt_ref[k] = jnp.sum(counts[k:k + 1, :]).astype(jnp.int32)

    n_pass = (cnt_ref[e] + (cap - 1)) // cap

    @pl.when(j == 0)
    def _():
        rank_row = rank_r_ref[pl.ds(e, 1), :]
        mask_row = mask_r_ref[pl.ds(e, 1), :]
        slot = lax.broadcasted_iota(jnp.int32, (cap, tm), 0).astype(F32)

        def compact(p, _):
            base = (p * cap).astype(F32)
            onehot = jnp.where((rank_row - base == slot) & (mask_row > 0.0), 1.0, 0.0).astype(BF16)
            rows = pl.ds(pl.multiple_of(p * cap, 16), cap)
            xc_ref[rows, :] = _dot(onehot, hn_ref[...]).astype(BF16)
            yc_ref[rows, :] = jnp.zeros((cap, D_MODEL), F32)
            return 0

        lax.fori_loop(0, n_pass, compact, 0)

    def expert(start, size):
        rows = pl.ds(pl.multiple_of(start, MOE_GRAIN), size)
        yc_ref[rows, :] += _swiglu(xc_ref[rows, :], wg_ref[0], wu_ref[0], wd_ref[0])

    n_tok = cnt_ref[e]
    n_full = n_tok // cap
    rem_blocks = (n_tok - n_full * cap + (MOE_GRAIN - 1)) // MOE_GRAIN
    n_full = n_full + (rem_blocks == cap // MOE_GRAIN).astype(jnp.int32)

    def full_pass(p, _):
        expert(p * cap, cap)
        return 0

    lax.fori_loop(0, n_full, full_pass, 0)
    for blocks in range(1, cap // MOE_GRAIN):
        @pl.when(rem_blocks == blocks)
        def _(blocks=blocks):
            expert(n_full * cap, blocks * MOE_GRAIN)

    @pl.when(j == pl.num_programs(2) - 1)
    def _():
        lane = lax.broadcasted_iota(jnp.int32, (tm, LANES), 1)
        pick = lane == e
        rank_col = jnp.sum(jnp.where(pick, rank_c_ref[...], 0.0), axis=-1, keepdims=True)
        w_col = jnp.sum(jnp.where(pick, comb_ref[...], 0.0), axis=-1, keepdims=True)
        slot = lax.broadcasted_iota(jnp.int32, (tm, cap), 1).astype(F32)

        def expand(p, _):
            base = (p * cap).astype(F32)
            onehot = jnp.where(rank_col - base == slot, 1.0, 0.0).astype(BF16)
            rows = pl.ds(pl.multiple_of(p * cap, 16), cap)
            acc_ref[...] += w_col * _dot(onehot, yc_ref[rows, :].astype(BF16))
            return 0

        lax.fori_loop(0, n_pass, expand, 0)

    @pl.when((e == N_EXPERTS - 1) & (j == pl.num_programs(2) - 1))
    def _():
        o_ref[...] = x_ref[...] + gate_ref[0] * _rms(acc_ref[...], npost_ref[...])


def _moe(x, g, shift, scale, r_hi, r_lo, w_gate, w_up, w_down, gate, npost, rows_per_mod):
    t = x.shape[0]
    n_e, _, ff = w_gate.shape
    assert n_e == N_EXPERTS
    ff_tile = ff // 4
    tm = min(MOE_TILE, rows_per_mod)
    per = rows_per_mod // tm
    passes = -(-tm // MOE_CAP)
    assert ff_tile % LANES == 0 and MOE_CAP % MOE_GRAIN == 0 and MOE_GRAIN % 16 == 0
    return pl.pallas_call(
        _moe_kernel,
        grid=(t // tm, n_e, ff // ff_tile),
        in_specs=[
            pl.BlockSpec((tm, D_MODEL), lambda i, e, j: (i, 0)),
            pl.BlockSpec((1, D_MODEL), lambda i, e, j: (0, 0)),
            pl.BlockSpec((1, 1, D_MODEL), lambda i, e, j: (i // per, 0, 0)),
            pl.BlockSpec((1, 1, D_MODEL), lambda i, e, j: (i // per, 0, 0)),
            pl.BlockSpec((D_MODEL, LANES), lambda i, e, j: (0, 0)),
            pl.BlockSpec((D_MODEL, LANES), lambda i, e, j: (0, 0)),
            pl.BlockSpec((1, D_MODEL, ff_tile), lambda i, e, j: (e, 0, j)),
            pl.BlockSpec((1, D_MODEL, ff_tile), lambda i, e, j: (e, 0, j)),
            pl.BlockSpec((1, ff_tile, D_MODEL), lambda i, e, j: (e, j, 0)),
            pl.BlockSpec((1, 1, D_MODEL), lambda i, e, j: (i // per, 0, 0)),
            pl.BlockSpec((1, D_MODEL), lambda i, e, j: (0, 0)),
        ],
        out_specs=pl.BlockSpec((tm, D_MODEL), lambda i, e, j: (i, 0)),
        out_shape=jax.ShapeDtypeStruct((t, D_MODEL), F32),
        scratch_shapes=[
            pltpu.VMEM((tm, D_MODEL), BF16),
            pltpu.VMEM((tm, D_MODEL), F32),
            pltpu.VMEM((tm, LANES), F32),
            pltpu.VMEM((tm, LANES), F32),
            pltpu.VMEM((LANES, tm), F32),
            pltpu.VMEM((LANES, tm), F32),
            pltpu.VMEM((passes * MOE_CAP, D_MODEL), BF16),
            pltpu.VMEM((passes * MOE_CAP, D_MODEL), F32),
            pltpu.SMEM((N_EXPERTS,), jnp.int32),
        ],
        compiler_params=_params(("parallel", "arbitrary", "arbitrary")),
        name="moe",
    )(x, g, shift, scale, r_hi, r_lo, w_gate, w_up, w_down, gate, npost)


MOE_ROWS = MOE_TILE * TOP_K + N_EXPERTS * MOE_GRAIN
MOE_META = 2 * N_EXPERTS


def _moe_dispatch_kernel(x_ref, g_ref, sh_ref, sc_ref, rh_ref, rl_ref, xs_ref, route_ref, meta_ref):
    tm = x_ref.shape[0]
    h = _rms(x_ref[...], g_ref[...]) * (1.0 + sc_ref[0]) + sh_ref[0]
    hi = h.astype(BF16)
    lo = (h - hi.astype(F32)).astype(BF16)
    logits = _dot(hi, rh_ref[...]) + _dot(lo, rh_ref[...]) + _dot(hi, rl_ref[...])
    lane = lax.broadcasted_iota(jnp.int32, logits.shape, 1)
    logits = jnp.where(lane < N_EXPERTS, logits, -jnp.inf)
    m1 = jnp.max(logits, axis=-1, keepdims=True)
    i1 = jnp.min(jnp.where(logits == m1, lane, LANES), axis=-1, keepdims=True)
    rest = jnp.where(lane == i1, -jnp.inf, logits)
    m2 = jnp.max(rest, axis=-1, keepdims=True)
    i2 = jnp.min(jnp.where(rest == m2, lane, LANES), axis=-1, keepdims=True)
    e2 = jnp.exp(m2 - m1)
    sel1 = jnp.where(lane == i1, 1.0, 0.0)
    sel2 = jnp.where(lane == i2, 1.0, 0.0)
    sel = sel1 + sel2
    sel1_t, sel2_t = sel1.T, sel2.T
    sel_t = sel1_t + sel2_t

    ti = lax.broadcasted_iota(jnp.int32, (tm, tm), 0)
    si = lax.broadcasted_iota(jnp.int32, (tm, tm), 1)
    rank_c = _dot(jnp.where(si < ti, 1.0, 0.0).astype(BF16), sel.astype(BF16))
    rank_r = _dot(sel_t.astype(BF16), jnp.where(ti < si, 1.0, 0.0).astype(BF16))
    blk_row = jnp.ceil(jnp.sum(sel, axis=0, keepdims=True) * (1.0 / MOE_GRAIN))
    blk_col = jnp.ceil(jnp.sum(sel_t, axis=1, keepdims=True) * (1.0 / MOE_GRAIN))
    ea = lax.broadcasted_iota(jnp.int32, (LANES, LANES), 0)
    eb = lax.broadcasted_iota(jnp.int32, (LANES, LANES), 1)
    off_row = _dot(jnp.broadcast_to(blk_row, (8, LANES)).astype(BF16),
                   jnp.where(ea < eb, 1.0, 0.0).astype(BF16))[0:1, :]
    off_col = _dot(jnp.where(eb < ea, 1.0, 0.0).astype(BF16),
                   jnp.broadcast_to(blk_col, (LANES, LANES)).astype(BF16))[:, 0:1]

    pos_c = off_row * MOE_GRAIN + rank_c
    pos1_c = jnp.sum(jnp.where(lane == i1, pos_c, 0.0), axis=-1, keepdims=True)
    pos2_c = jnp.sum(jnp.where(lane == i2, pos_c, 0.0), axis=-1, keepdims=True)
    route_ref[...] = (jnp.where(lane == 0, pos1_c, 0.0) + jnp.where(lane == 1, pos2_c, 0.0)
                      + jnp.where(lane == 2, 1.0 / (1.0 + e2), 0.0) + jnp.where(lane == 3, e2 / (1.0 + e2), 0.0))
    pos_r = off_col * MOE_GRAIN + rank_r
    pos1_r = jnp.sum(sel1_t * pos_r, axis=0, keepdims=True)
    pos2_r = jnp.sum(sel2_t * pos_r, axis=0, keepdims=True)
    for c in range(MOE_ROWS // MOE_CAP):
        slot = (lax.broadcasted_iota(jnp.int32, (MOE_CAP, tm), 0) + c * MOE_CAP).astype(F32)
        onehot = jnp.where(slot == pos1_r, 1.0, jnp.where(slot == pos2_r, 1.0, 0.0)).astype(BF16)
        xs_ref[c * MOE_CAP:(c + 1) * MOE_CAP, :] = _dot(onehot, hi).astype(BF16)

    lane8 = lax.broadcasted_iota(jnp.int32, (8, LANES), 1)
    blk8 = jnp.broadcast_to(blk_row, (8, LANES))
    off8 = pltpu.roll(jnp.broadcast_to(off_row, (8, LANES)), N_EXPERTS, 1)
    meta = jnp.where(lane8 < N_EXPERTS, blk8, jnp.where(lane8 < MOE_META, off8, 0.0))
    meta_ref[0] = meta.astype(jnp.int32)


def _moe_dispatch(x, g, shift, scale, r_hi, r_lo, rows_per_mod):
    t = x.shape[0]
    tm = MOE_TILE
    assert rows_per_mod % tm == 0
    per = rows_per_mod // tm
    n_t = t // tm
    return pl.pallas_call(
        _moe_dispatch_kernel,
        grid=(n_t,),
        in_specs=[
            pl.BlockSpec((tm, D_MODEL), lambda i: (i, 0)),
            pl.BlockSpec((1, D_MODEL), lambda i: (0, 0)),
            pl.BlockSpec((1, 1, D_MODEL), lambda i: (i // per, 0, 0)),
            pl.BlockSpec((1, 1, D_MODEL), lambda i: (i // per, 0, 0)),
            pl.BlockSpec((D_MODEL, LANES), lambda i: (0, 0)),
            pl.BlockSpec((D_MODEL, LANES), lambda i: (0, 0)),
        ],
        out_specs=[
            pl.BlockSpec((MOE_ROWS, D_MODEL), lambda i: (i, 0)),
            pl.BlockSpec((tm, LANES), lambda i: (i, 0)),
            pl.BlockSpec((1, 8, LANES), lambda i: (i, 0, 0)),
        ],
        out_shape=[
            jax.ShapeDtypeStruct((n_t * MOE_ROWS, D_MODEL), BF16),
            jax.ShapeDtypeStruct((t, LANES), F32),
            jax.ShapeDtypeStruct((n_t, 8, LANES), jnp.int32),
        ],
        compiler_params=_params(("parallel",)),
        name="moe_dispatch",
    )(x, g, shift, scale, r_hi, r_lo)


def _moe_experts_kernel(meta_ref, xs_ref, wg_ref, wu_ref, wd_ref, out_ref, *, ff_tile):
    e = pl.program_id(0)
    i = pl.program_id(1)
    out_ref[...] = xs_ref[...]
    n_blk = meta_ref[i * MOE_META + e]
    first = meta_ref[i * MOE_META + N_EXPERTS + e] * MOE_GRAIN
    per_pass = MOE_CAP // MOE_GRAIN
    n_full = n_blk // per_pass
    rem = n_blk - n_full * per_pass
    ff = wg_ref.shape[2]

    def run(start, size):
        rows = pl.ds(pl.multiple_of(start, MOE_GRAIN), size)
        xin = xs_ref[rows, :]
        y = jnp.zeros((size, D_MODEL), F32)
        for j in range(ff // ff_tile):
            cols = slice(j * ff_tile, (j + 1) * ff_tile)
            y += _swiglu(xin, wg_ref[0, :, cols], wu_ref[0, :, cols], wd_ref[0, cols, :])
        out_ref[rows, :] = y.astype(out_ref.dtype)

    def full_pass(p, _):
        run(first + p * MOE_CAP, MOE_CAP)
        return 0

    lax.fori_loop(0, n_full, full_pass, 0)
    for blocks in range(1, per_pass):
        @pl.when(rem == blocks)
        def _(blocks=blocks):
            run(first + n_full * MOE_CAP, blocks * MOE_GRAIN)


def _moe_experts(xs, meta, w_gate, w_up, w_down):
    n_e, _, ff = w_gate.shape
    n_t = xs.shape[0] // MOE_ROWS
    ff_tile = 512
    assert ff % ff_tile == 0
    resident = pl.Buffered(1)
    grid_spec = pltpu.PrefetchScalarGridSpec(
        num_scalar_prefetch=1,
        grid=(n_e, n_t),
        in_specs=[
            pl.BlockSpec((MOE_ROWS, D_MODEL), lambda e, i, m: (i, 0)),
            pl.BlockSpec((1, D_MODEL, ff), lambda e, i, m: (e, 0, 0), pipeline_mode=resident),
            pl.BlockSpec((1, D_MODEL, ff), lambda e, i, m: (e, 0, 0), pipeline_mode=resident),
            pl.BlockSpec((1, ff, D_MODEL), lambda e, i, m: (e, 0, 0), pipeline_mode=resident),
        ],
        out_specs=pl.BlockSpec((MOE_ROWS, D_MODEL), lambda e, i, m: (i, 0)),
    )
    return pl.pallas_call(
        functools.partial(_moe_experts_kernel, ff_tile=ff_tile),
        grid_spec=grid_spec,
        out_shape=jax.ShapeDtypeStruct(xs.shape, xs.dtype),
        input_output_aliases={1: 0},
        compiler_params=_params(("arbitrary", "arbitrary")),
        name="moe_experts",
    )(meta, xs, w_gate, w_up, w_down)


def _moe_combine_kernel(ys_ref, route_ref, x_ref, gate_ref, npost_ref, o_ref):
    tm = x_ref.shape[0]
    lane = lax.broadcasted_iota(jnp.int32, (tm, LANES), 1)
    route = route_ref[...]
    pos1, pos2, w1, w2 = (jnp.sum(jnp.where(lane == k, route, 0.0), axis=-1, keepdims=True) for k in range(4))
    width = 512
    acc = jnp.zeros((tm, D_MODEL), F32)
    for c in range(MOE_ROWS // width):
        slot = (lax.broadcasted_iota(jnp.int32, (tm, width), 1) + c * width).astype(F32)
        wmat = jnp.where(slot == pos1, w1, 0.0) + jnp.where(slot == pos2, w2, 0.0)
        acc += _dot(wmat.astype(BF16), ys_ref[c * width:(c + 1) * width, :])
    o_ref[...] = x_ref[...] + gate_ref[0] * _rms(acc, npost_ref[...])


def _moe_combine(ys, route, x, gate, npost, rows_per_mod):
    t = x.shape[0]
    tm = MOE_TILE
    per = rows_per_mod // tm
    return pl.pallas_call(
        _moe_combine_kernel,
        grid=(t // tm,),
        in_specs=[
            pl.BlockSpec((MOE_ROWS, D_MODEL), lambda i: (i, 0)),
            pl.BlockSpec((tm, LANES), lambda i: (i, 0)),
            pl.BlockSpec((tm, D_MODEL), lambda i: (i, 0)),
            pl.BlockSpec((1, 1, D_MODEL), lambda i: (i // per, 0, 0)),
            pl.BlockSpec((1, D_MODEL), lambda i: (0, 0)),
        ],
        out_specs=pl.BlockSpec((tm, D_MODEL), lambda i: (i, 0)),
        out_shape=jax.ShapeDtypeStruct((t, D_MODEL), F32),
        compiler_params=_params(("parallel",)),
        name="moe_combine",
    )(ys, route, x, gate, npost)


def _moe_sorted(x, g, shift, scale, r_hi, r_lo, w_gate, w_up, w_down, gate, npost, rows_per_mod):
    xs, route, meta = _moe_dispatch(x, g, shift, scale, r_hi, r_lo, rows_per_mod)
    meta = meta[:, 0, :MOE_META].reshape(-1)
    ys = _moe_experts(xs, meta, w_gate, w_up, w_down)
    return _moe_combine(ys, route, x, gate, npost, rows_per_mod)


def _reorder_w_in(w):
    o = 4 * M_WIDTH
    gates = jnp.pad(w[:, o:o + M_GATES], ((0, 0), (0, LANES - M_GATES)))
    na = w[:, o + M_GATES:o + M_GATES + 3 * NA_WIDTH]
    u0 = o + M_GATES + 3 * NA_WIDTH
    main = jnp.concatenate([w[:, :o], w[:, u0 + S5_WIDTH:], na], axis=1)
    return main.astype(BF16), gates.astype(BF16), w[:, u0:u0 + S5_WIDTH].astype(BF16)


def _row(v):
    return v.reshape(1, -1).astype(F32)


def _token_mixers(x, xc, mx, mc, lp, rope, seq_len, ctx_len, ctx_out):
    bsz = x.shape[0] // seq_len
    w_main, w_gates, w_u = _reorder_w_in(lp['w_in'])
    g_pre = _row(lp['norm_mix_pre'])
    px, gx, ux = _inproj(x, g_pre, mx[0], mx[1], w_main, w_gates, w_u, seq_len)
    pc, gc, uc = _inproj(xc, g_pre, mc[0], mc[1], w_main, w_gates, w_u, xc.shape[0])

    conv_w = lp['m_conv_w'].astype(F32)
    conv_b = _row(lp['m_conv_b'])
    gate_b = jnp.pad(_row(lp['m_gate_b']), ((0, 0), (0, LANES - M_GATES)))
    m_norm = _row(lp['m_norm'])
    qk_c = _qkconv(pc, conv_w, conv_b, rope[0][:ctx_len], rope[1][:ctx_len], ctx_len, False)
    qk_x = _qkconv(px, conv_w, conv_b, rope[0], rope[1], seq_len, True)
    zero = _mlstm_zero_state(bsz)
    ym_c, fin_f, fin_b = _mlstm_bidir(qk_c, pc, gc, gate_b, m_norm, zero, zero, ctx_len)
    ym_x, _, _ = _mlstm_bidir(qk_x, px, gx, gate_b, m_norm, fin_f, fin_b, seq_len)

    yn_x = _na_attention(px, pc, _na_bias_table(lp['na_rpb']), seq_len, ctx_len)

    tables = _s5_tables(lp['s5_lam_re'], lp['s5_lam_im'], lp['s5_log_dt'], lp['s5_b_re'], lp['s5_b_im'],
                        lp['s5_c_re'], lp['s5_c_im'])
    zs = jnp.zeros((bsz, S5_GROUPS * S5_STATE), F32)
    ys_c, fin_s = _s5_branch(uc, tables, ctx_len, (zs, zs, zs, zs), ctx_out)
    ys_x, _ = _s5_branch(ux, tables, seq_len, fin_s, True)

    wts = (_row(lp['s5_d']), lp['s5_glu_w'].astype(BF16), _row(lp['s5_glu_b']),
           lp['w_branch_m'].astype(BF16), lp['w_branch_na'].astype(BF16), lp['w_branch_s5'].astype(BF16),
           lp['w_out'].astype(BF16))
    g_post = _row(lp['norm_mix_post'])
    x = _merge(x, ym_x, yn_x, ys_x, ux, px, wts, mx[2], g_post, seq_len)
    if ctx_out:
        yn_c = _ctx_attention(pc, ctx_len)
        xc = _merge(xc, ym_c, yn_c, ys_c, uc, pc, wts, mc[2], g_post, xc.shape[0])
    return x, xc


def kernel(x, c, ctx, c_ctx, ada_w, ada_b, norm_mix_pre, norm_mix_post, norm_ffn_pre, norm_ffn_post, w_in, m_gate_b, m_conv_w, m_conv_b, m_norm, na_rpb, s5_lam_re, s5_lam_im, s5_log_dt, s5_b_re, s5_b_im, s5_c_re, s5_c_im, s5_d, s5_glu_w, s5_glu_b, w_branch_m, w_branch_na, w_branch_s5, w_out, ffn_w_gate, ffn_w_up, ffn_w_down, moe_router, moe_w_gate, moe_w_up, moe_w_down):
    bsz, seq_len, d = x.shape
    ctx_len = ctx.shape[1]
    depth = w_in.shape[0]
    assert d == D_MODEL and seq_len % (GRID_W * NA_ROWS) == 0 and ctx_len % M_CHUNK == 0
    rope = _rope_tables(seq_len)
    xf = x.reshape(bsz * seq_len, d)
    xc = ctx.reshape(bsz * ctx_len, d)
    cvecs = jnp.zeros((16, d), F32).at[:bsz].set(c).at[bsz].set(c_ctx)
    for l in range(depth):
        last = l == depth - 1
        mod = _adaln(cvecs, ada_w[l].astype(BF16), _row(ada_b[l]))
        mx = [mod[:bsz, k * d:(k + 1) * d].reshape(bsz, 1, d) for k in range(N_MOD)]
        mc = [mod[bsz:bsz + 1, k * d:(k + 1) * d].reshape(1, 1, d) for k in range(N_MOD)]
        lp = {
            'norm_mix_pre': norm_mix_pre[l], 'norm_mix_post': norm_mix_post[l],
            'w_in': w_in[l], 'm_gate_b': m_gate_b[l], 'm_conv_w': m_conv_w[l], 'm_conv_b': m_conv_b[l],
            'm_norm': m_norm[l], 'na_rpb': na_rpb[l], 's5_lam_re': s5_lam_re[l], 's5_lam_im': s5_lam_im[l],
            's5_log_dt': s5_log_dt[l], 's5_b_re': s5_b_re[l], 's5_b_im': s5_b_im[l], 's5_c_re': s5_c_re[l],
            's5_c_im': s5_c_im[l], 's5_d': s5_d[l], 's5_glu_w': s5_glu_w[l], 's5_glu_b': s5_glu_b[l],
            'w_branch_m': w_branch_m[l], 'w_branch_na': w_branch_na[l], 'w_branch_s5': w_branch_s5[l],
            'w_out': w_out[l],
        }
        xf, xc = _token_mixers(xf, xc, mx, mc, lp, rope, seq_len, ctx_len, not last)
        j = l // 2
        g_pre, g_post = _row(norm_ffn_pre[l]), _row(norm_ffn_post[l])
        if l % 2 == 0:
            wg, wu, wd = ffn_w_gate[j].astype(BF16), ffn_w_up[j].astype(BF16), ffn_w_down[j].astype(BF16)

            def channel(h, m, rows):
                return _ffn(h, g_pre, m[3], m[4], wg, wu, wd, m[5], g_post, rows)
        else:
            wg, wu, wd = moe_w_gate[j].astype(BF16), moe_w_up[j].astype(BF16), moe_w_down[j].astype(BF16)
            router = jnp.pad(moe_router[j].astype(F32), ((0, 0), (0, LANES - N_EXPERTS)))
            r_hi = router.astype(BF16)
            r_lo = (router - r_hi.astype(F32)).astype(BF16)

            def channel(h, m, rows):
                return _moe_sorted(h, g_pre, m[3], m[4], r_hi, r_lo, wg, wu, wd, m[5], g_post, rows)
        xf = channel(xf, mx, seq_len)
        if not last:
            xc = channel(xc, mc, xc.shape[0])
    return xf.reshape(bsz, seq_len, d)
```

```python
import functools
import math

import jax
import jax.numpy as jnp
from jax import lax
from jax.experimental import pallas as pl
from jax.experimental.pallas import tpu as pltpu

F32 = jnp.float32
BF16 = jnp.bfloat16

D_MODEL = 1024
EPS = 1e-6
N_MOD = 6
GRID_W = 64

M_HEADS = 4
M_HEAD_DIM = 128
M_WIDTH = M_HEADS * M_HEAD_DIM
M_GATES = 4 * M_HEADS
M_CHUNK = 128
ROPE_BASE = 10000.0

NA_HEADS = 8
NA_HEAD_DIM = 64
NA_WIDTH = NA_HEADS * NA_HEAD_DIM
WIN_R = 8
WIN_C = 16
NEG_BIG = -1e30

S5_GROUP = 16
S5_WIDTH = 512
S5_GROUPS = S5_WIDTH // S5_GROUP
S5_STATE = 64
S5_MAX_RE = -1e-4

LANES = 128
VMEM_LIMIT = 52 * 1024 * 1024

S5_CHUNK = 16
S5_QUAD_GROUPS = LANES // S5_GROUP
S5_QUADS = S5_GROUPS // S5_QUAD_GROUPS
S5_QCOLS = S5_CHUNK * LANES
S5_QSTATE = S5_QUAD_GROUPS * S5_STATE

N_EXPERTS = 8
TOP_K = 2
MOE_TILE = 1024
MOE_CAP = 256
MOE_GRAIN = 64

COL_QK, COL_V, COL_O, COL_NAQ, COL_NAK, COL_NAV = 0, 2, 3, 10, 11, 12
COL_GATE = 2
IN_MAIN = 6656


def _params(sem):
    return pltpu.CompilerParams(dimension_semantics=sem, vmem_limit_bytes=VMEM_LIMIT)


def _dot(a, b):
    return jnp.dot(a, b, preferred_element_type=F32)


def _dot_nt(a, b):
    return lax.dot_general(a, b, (((1,), (1,)), ((), ())), preferred_element_type=F32)


def _split3(x):
    hi = x.astype(BF16)
    r1 = x - hi.astype(F32)
    mid = r1.astype(BF16)
    lo = (r1 - mid.astype(F32)).astype(BF16)
    return hi, mid, lo


def _rms(x, g):
    return x * lax.rsqrt(jnp.mean(x * x, axis=-1, keepdims=True) + EPS) * g


def _inproj_kernel(x_ref, g_ref, sh_ref, sc_ref, w_ref, wg_ref, wu_ref, o_ref, og_ref, ou_ref, hn_ref):
    @pl.when(pl.program_id(1) == 0)
    def _():
        h = _rms(x_ref[...], g_ref[...]) * (1.0 + sc_ref[0]) + sh_ref[0]
        hb = h.astype(BF16)
        hn_ref[...] = hb
        og_ref[...] = _dot(hb, wg_ref[...])
        ou_ref[...] = _dot(hb, wu_ref[...])

    o_ref[...] = _dot(hn_ref[...], w_ref[...]).astype(o_ref.dtype)


def _inproj(x, g, shift, scale, w_main, w_gates, w_u, rows_per_mod):
    t = x.shape[0]
    tm = min(1024, rows_per_mod)
    tn = IN_MAIN // 4
    assert rows_per_mod % tm == 0 and t % tm == 0 and tn % LANES == 0
    per = rows_per_mod // tm
    return pl.pallas_call(
        _inproj_kernel,
        grid=(t // tm, IN_MAIN // tn),
        in_specs=[
            pl.BlockSpec((tm, D_MODEL), lambda i, j: (i, 0)),
            pl.BlockSpec((1, D_MODEL), lambda i, j: (0, 0)),
            pl.BlockSpec((1, 1, D_MODEL), lambda i, j: (i // per, 0, 0)),
            pl.BlockSpec((1, 1, D_MODEL), lambda i, j: (i // per, 0, 0)),
            pl.BlockSpec((D_MODEL, tn), lambda i, j: (0, j)),
            pl.BlockSpec((D_MODEL, LANES), lambda i, j: (0, 0)),
            pl.BlockSpec((D_MODEL, S5_WIDTH), lambda i, j: (0, 0)),
        ],
        out_specs=[
            pl.BlockSpec((tm, tn), lambda i, j: (i, j)),
            pl.BlockSpec((tm, LANES), lambda i, j: (i, 0)),
            pl.BlockSpec((tm, S5_WIDTH), lambda i, j: (i, 0)),
        ],
        out_shape=[jax.ShapeDtypeStruct((t, IN_MAIN), BF16), jax.ShapeDtypeStruct((t, LANES), F32),
                   jax.ShapeDtypeStruct((t, S5_WIDTH), F32)],
        scratch_shapes=[pltpu.VMEM((tm, D_MODEL), BF16)],
        compiler_params=_params(("parallel", "arbitrary")),
        name="inproj",
    )(x, g, shift, scale, w_main, w_gates, w_u)


def _adaln_kernel(c_ref, w_ref, b_ref, o_ref):
    c = c_ref[...]
    s = c * jax.nn.sigmoid(c)
    o_ref[...] = _dot(s.astype(BF16), w_ref[...]) + b_ref[...]


def _adaln(cvecs, w, b):
    n = w.shape[1]
    tn = 1024
    return pl.pallas_call(
        _adaln_kernel,
        grid=(n // tn,),
        in_specs=[
            pl.BlockSpec(cvecs.shape, lambda j: (0, 0)),
            pl.BlockSpec((D_MODEL, tn), lambda j: (0, j)),
            pl.BlockSpec((1, tn), lambda j: (0, j)),
        ],
        out_specs=pl.BlockSpec((cvecs.shape[0], tn), lambda j: (0, j)),
        out_shape=jax.ShapeDtypeStruct((cvecs.shape[0], n), F32),
        compiler_params=_params(("parallel",)),
        name="adaln",
    )(cvecs, w, b)


HALO = 16


def _swap32(x):
    lane = lax.broadcasted_iota(jnp.int32, x.shape, 1)
    fwd = pltpu.roll(x, 96, 1)
    bwd = pltpu.roll(x, 32, 1)
    return jnp.where((lane % 64) < 32, fwd, bwd)


def _qkconv_kernel(x_ref, pv_ref, nx_ref, w_ref, b_ref, cos_ref, sin_ref, o_ref, *, tiles_per_seq, rope):
    i = pl.program_id(0)
    tm = x_ref.shape[0]
    x = x_ref[...].astype(F32)
    first = (i % tiles_per_seq) == 0
    last = (i % tiles_per_seq) == tiles_per_seq - 1
    prev_row = jnp.where(first, 0.0, pv_ref[HALO - 1:HALO, :].astype(F32))
    next_row = jnp.where(last, 0.0, nx_ref[0:1, :].astype(F32))
    row = lax.broadcasted_iota(jnp.int32, x.shape, 0)
    xp = jnp.where(row == 0, prev_row, pltpu.roll(x, 1, 0))
    xn = jnp.where(row == tm - 1, next_row, pltpu.roll(x, tm - 1, 0))
    y = xp * w_ref[0:1, :] + x * w_ref[1:2, :] + xn * w_ref[2:3, :] + b_ref[...]
    y = y * jax.nn.sigmoid(y)
    kscale = M_HEAD_DIM ** -0.5
    for hh in range(2 * M_HEADS):
        ys = y[:, hh * LANES:(hh + 1) * LANES]
        if rope:
            ys = ys * cos_ref[...] + _swap32(ys) * sin_ref[...]
        if hh >= M_HEADS:
            ys = ys * kscale
        o_ref[:, hh * LANES:(hh + 1) * LANES] = ys.astype(o_ref.dtype)


def _qkconv(px, conv_w, conv_b, cos_t, sin_t, seq_len, rope):
    t = px.shape[0]
    tm = min(512, seq_len)
    assert seq_len % tm == 0
    tiles = seq_len // tm
    hb = tm // HALO
    nh = t // HALO
    return pl.pallas_call(
        functools.partial(_qkconv_kernel, tiles_per_seq=tiles, rope=rope),
        grid=(t // tm,),
        in_specs=[
            pl.BlockSpec((tm, 2 * M_WIDTH), lambda i: (i, 0)),
            pl.BlockSpec((HALO, 2 * M_WIDTH), lambda i: (jnp.maximum(i * hb - 1, 0), 0)),
            pl.BlockSpec((HALO, 2 * M_WIDTH), lambda i: (jnp.minimum((i + 1) * hb, nh - 1), 0)),
            pl.BlockSpec((3, 2 * M_WIDTH), lambda i: (0, 0)),
            pl.BlockSpec((1, 2 * M_WIDTH), lambda i: (0, 0)),
            pl.BlockSpec((tm, LANES), lambda i: (i % tiles, 0)),
            pl.BlockSpec((tm, LANES), lambda i: (i % tiles, 0)),
        ],
        out_specs=pl.BlockSpec((tm, 2 * M_WIDTH), lambda i: (i, 0)),
        out_shape=jax.ShapeDtypeStruct((t, 2 * M_WIDTH), BF16),
        compiler_params=_params(("parallel",)),
        name="qkconv",
    )(px, px, px, conv_w, conv_b, cos_t, sin_t)


def _rope_tables(n):
    pos = jnp.arange(n, dtype=jnp.int32)
    row = (pos // GRID_W).astype(F32)
    col = (pos % GRID_W).astype(F32)
    n_freq = M_HEAD_DIM // 4
    inv = ROPE_BASE ** (-jnp.arange(n_freq, dtype=F32) / n_freq)
    ar, ac = row[:, None] * inv, col[:, None] * inv
    cos_t = jnp.concatenate([jnp.cos(ar), jnp.cos(ar), jnp.cos(ac), jnp.cos(ac)], axis=-1)
    sin_t = jnp.concatenate([-jnp.sin(ar), jnp.sin(ar), -jnp.sin(ac), jnp.sin(ac)], axis=-1)
    return cos_t, sin_t


def _mlstm_kernel(*refs, reverse, final, nb):
    if final:
        (qk_ref, v_ref, g_ref, gb_ref, c0_ref, n0_ref, m0_ref, hp_ref, op_ref, nw_ref,
         out_ref, cf_ref, nf_ref, mf_ref, c_scr, n_scr, m_scr, qk_scr, qc_scr, qn_scr, s_scr, vt_scr, vu_scr, dec_scr) = refs
    else:
        (qk_ref, v_ref, g_ref, gb_ref, c0_ref, n0_ref, m0_ref,
         out_ref, cf_ref, nf_ref, mf_ref, c_scr, n_scr, m_scr, qk_scr, qc_scr, qn_scr, s_scr, vt_scr, vu_scr, dec_scr) = refs
    c_idx = pl.program_id(1)
    n_c = pl.num_programs(1)
    L = M_CHUNK

    @pl.when(c_idx == 0)
    def _():
        c_scr[...] = c0_ref[...]
        n_scr[...] = n0_ref[...]
        m_scr[...] = m0_ref[...]

    ti = lax.broadcasted_iota(jnp.int32, (L, L), 0)
    si = lax.broadcasted_iota(jnp.int32, (L, L), 1)
    keep = (si >= ti) if reverse else (si <= ti)
    tri = jnp.where(keep, 1.0, 0.0).astype(BF16)
    i_off = 2 * M_HEADS if reverse else 0
    f_off = i_off + M_HEADS
    end = 0 if reverse else L - 1

    units = [(a, b) for a in range(nb) for b in range(M_HEADS)]

    def head(ref, gi, h, base=0):
        return ref[gi, :, base + h * LANES:base + (h + 1) * LANES]

    keep_t = (ti >= si) if reverse else (ti <= si)

    for u, (gi, h) in enumerate(units):
        q = head(qk_ref, gi, h)
        qk_scr[u] = _dot_nt(head(qk_ref, gi, h, M_WIDTH), q)
        qc_scr[u] = _dot_nt(c_scr[gi, h].astype(BF16), q)
        qn_scr[u] = _dot_nt(jnp.broadcast_to(n_scr[gi, h], (8, LANES)).astype(BF16), q)

    for u, (gi, h) in enumerate(units):
        if h == 0:
            g = g_ref[gi] + gb_ref[...]
            f_hi, f_mid, f_lo = _split3(jax.nn.log_sigmoid(g))
            bsum = _dot(tri, f_hi) + _dot(tri, f_mid) + _dot(tri, f_lo)
            g_t = g.T
            b_t = bsum.T
        m_prev = m_scr[gi, h][:, 0:1]
        b_row = b_t[f_off + h:f_off + h + 1, :]
        i_row = g_t[i_off + h:i_off + h + 1, :]
        b_end = b_row[:, end:end + 1]
        d_col = g[:, i_off + h:i_off + h + 1] - bsum[:, f_off + h:f_off + h + 1]

        log_w = jnp.where(keep_t, b_row + d_col, -jnp.inf)
        carry_log = b_row + m_prev
        m_t = jnp.maximum(carry_log, jnp.max(log_w, axis=0, keepdims=True))
        s_t = qk_scr[u] * jnp.exp(log_w - m_t)
        c_scale = jnp.exp(carry_log - m_t)
        den = jnp.sum(s_t, axis=0, keepdims=True) + c_scale * qn_scr[u][0:1, :]
        inv = 1.0 / jnp.maximum(jnp.abs(den), jnp.exp(-m_t))
        s_scr[u] = (s_t * inv).astype(BF16)
        qc_scr[u] = qc_scr[u] * (c_scale * inv)

        m_new = jnp.maximum(b_end + m_prev, jnp.max(b_end - b_row + i_row, axis=-1, keepdims=True))
        u_row = jnp.exp(b_end - b_row + i_row - m_new)
        decay = jnp.exp(b_end + m_prev - m_new)
        v_t = head(v_ref, gi, h).astype(F32).T
        vt_scr[u] = v_t.astype(BF16)
        vu_scr[u] = (v_t * u_row).astype(BF16)
        u8 = jnp.broadcast_to(u_row, (8, L)).astype(BF16)
        n_scr[gi, h] = decay * n_scr[gi, h] + _dot(u8, head(qk_ref, gi, h, M_WIDTH))[0:1, :]
        m_scr[gi, h] = jnp.broadcast_to(m_new, (1, LANES))
        dec_scr[u] = jnp.broadcast_to(decay, (1, LANES))

    for u, (gi, h) in enumerate(units):
        qc_scr[u] = _dot(vt_scr[u], s_scr[u]) + qc_scr[u]
        c_scr[gi, h] = dec_scr[u][:, 0:1] * c_scr[gi, h] + _dot(vu_scr[u], head(qk_ref, gi, h, M_WIDTH))

    for u, (gi, h) in enumerate(units):
        rows = slice(h * M_HEAD_DIM, (h + 1) * M_HEAD_DIM)
        if final:
            hs = qc_scr[u] + hp_ref[gi, rows, :]
            mu = jnp.mean(hs, axis=0, keepdims=True)
            var = jnp.mean(jnp.square(hs - mu), axis=0, keepdims=True)
            hn = ((hs - mu) * lax.rsqrt(var + EPS)).T
            y = hn * nw_ref[:, h * LANES:(h + 1) * LANES] * jax.nn.sigmoid(head(op_ref, gi, h).astype(F32))
            out_ref[gi, :, h * LANES:(h + 1) * LANES] = y.astype(out_ref.dtype)
        else:
            out_ref[gi, rows, :] = qc_scr[u]

    @pl.when(c_idx == n_c - 1)
    def _():
        cf_ref[...] = c_scr[...]
        nf_ref[...] = n_scr[...]
        mf_ref[...] = m_scr[...]


def _mlstm_scan(qk, px, gates, gate_b, state, seq_len, reverse, hprev=None, norm_w=None):
    t = qk.shape[0]
    bsz = t // seq_len
    n_c = seq_len // M_CHUNK
    final = hprev is not None
    nb = 4 if bsz % 4 == 0 else (2 if bsz % 2 == 0 else 1)
    qk, px, gates = (a.reshape(bsz, seq_len, a.shape[1]) for a in (qk, px, gates))

    def chunk(c):
        return (n_c - 1 - c) if reverse else c

    state_specs = [
        pl.BlockSpec((nb, M_HEADS, M_HEAD_DIM, M_HEAD_DIM), lambda b, c: (b, 0, 0, 0)),
        pl.BlockSpec((nb, M_HEADS, 1, LANES), lambda b, c: (b, 0, 0, 0)),
        pl.BlockSpec((nb, M_HEADS, 1, LANES), lambda b, c: (b, 0, 0, 0)),
    ]
    in_specs = [
        pl.BlockSpec((nb, M_CHUNK, 2 * M_WIDTH), lambda b, c: (b, chunk(c), 0)),
        pl.BlockSpec((nb, M_CHUNK, M_WIDTH), lambda b, c: (b, chunk(c), COL_V)),
        pl.BlockSpec((nb, M_CHUNK, LANES), lambda b, c: (b, chunk(c), 0)),
        pl.BlockSpec((1, LANES), lambda b, c: (0, 0)),
    ] + state_specs
    args = [qk, px, gates, gate_b, *state]
    t_spec = pl.BlockSpec((nb, M_WIDTH, M_CHUNK), lambda b, c: (b, 0, chunk(c)))
    if final:
        in_specs += [
            t_spec,
            pl.BlockSpec((nb, M_CHUNK, M_WIDTH), lambda b, c: (b, chunk(c), COL_O)),
            pl.BlockSpec((1, M_WIDTH), lambda b, c: (0, 0)),
        ]
        args += [hprev, px, norm_w]
        out_spec = pl.BlockSpec((nb, M_CHUNK, M_WIDTH), lambda b, c: (b, chunk(c), 0))
        out_shape = jax.ShapeDtypeStruct((bsz, seq_len, M_WIDTH), BF16)
    else:
        out_spec = t_spec
        out_shape = jax.ShapeDtypeStruct((bsz, M_WIDTH, seq_len), F32)
    units = nb * M_HEADS
    out, *fin = pl.pallas_call(
        functools.partial(_mlstm_kernel, reverse=reverse, final=final, nb=nb),
        grid=(bsz // nb, n_c),
        in_specs=in_specs,
        out_specs=[out_spec] + state_specs,
        out_shape=[
            out_shape,
            jax.ShapeDtypeStruct((bsz, M_HEADS, M_HEAD_DIM, M_HEAD_DIM), F32),
            jax.ShapeDtypeStruct((bsz, M_HEADS, 1, LANES), F32),
            jax.ShapeDtypeStruct((bsz, M_HEADS, 1, LANES), F32),
        ],
        scratch_shapes=[
            pltpu.VMEM((nb, M_HEADS, M_HEAD_DIM, M_HEAD_DIM), F32),
            pltpu.VMEM((nb, M_HEADS, 1, LANES), F32),
            pltpu.VMEM((nb, M_HEADS, 1, LANES), F32),
            pltpu.VMEM((units, M_CHUNK, M_CHUNK), F32),
            pltpu.VMEM((units, M_HEAD_DIM, M_CHUNK), F32),
            pltpu.VMEM((units, 8, M_CHUNK), F32),
            pltpu.VMEM((units, M_CHUNK, M_CHUNK), BF16),
            pltpu.VMEM((units, M_HEAD_DIM, M_CHUNK), BF16),
            pltpu.VMEM((units, M_HEAD_DIM, M_CHUNK), BF16),
            pltpu.VMEM((units, 1, LANES), F32),
        ],
        compiler_params=_params(("parallel", "arbitrary")),
        name="mlstm_rev" if reverse else "mlstm_fwd",
    )(*args)
    return [out.reshape(t, M_WIDTH) if final else out] + fin


def _mlstm_zero_state(bsz):
    return (jnp.zeros((bsz, M_HEADS, M_HEAD_DIM, M_HEAD_DIM), F32),
            jnp.zeros((bsz, M_HEADS, 1, LANES), F32),
            jnp.zeros((bsz, M_HEADS, 1, LANES), F32))


def _mlstm_bidir(qk, px, gates, gate_b, norm_w, state_f, state_b, seq_len):
    h_f, *fin_f = _mlstm_scan(qk, px, gates, gate_b, state_f, seq_len, False)
    y, *fin_b = _mlstm_scan(qk, px, gates, gate_b, state_b, seq_len, True, hprev=h_f, norm_w=norm_w)
    return y, tuple(fin_f), tuple(fin_b)


NA_ROWS = 8
NA_STEP = 4


def _na_kernel(q_ref, kp_ref, kc_ref, kn_ref, vp_ref, vc_ref, vn_ref, kx_ref, vx_ref, *rest, rows):
    bias_refs = rest[:NA_STEP]
    o_ref, kwin, vwin, s_scr, p_scr = rest[NA_STEP:]
    step = pl.program_id(1)
    blk = NA_ROWS * GRID_W

    @pl.when(step % (NA_ROWS // NA_STEP) == 0)
    def _():
        kwin[0:blk, :] = kp_ref[...]
        kwin[blk:2 * blk, :] = kc_ref[...]
        kwin[2 * blk:3 * blk, :] = kn_ref[...]
        vwin[0:blk, :] = vp_ref[...]
        vwin[blk:2 * blk, :] = vc_ref[...]
        vwin[2 * blk:3 * blk, :] = vn_ref[...]

    g = step // (NA_ROWS // NA_STEP)
    n_loc = WIN_R * GRID_W
    lane = lax.broadcasted_iota(jnp.int32, (GRID_W, LANES), 1)
    scale = NA_HEAD_DIM ** -0.5
    n_ctx = kx_ref.shape[0]
    units = [(kk, hp) for kk in range(NA_STEP) for hp in range(NA_HEADS // 2)]
    offs = []
    for kk in range(NA_STEP):
        r = step * NA_STEP + kk
        r0 = jnp.clip(r - WIN_R // 2, 0, rows - WIN_R)
        offs.append(pl.multiple_of((r0 - NA_ROWS * g + NA_ROWS) * GRID_W, GRID_W))

    for u, (kk, hp) in enumerate(units):
        cols = slice(hp * LANES, (hp + 1) * LANES)
        q2 = q_ref[kk * GRID_W:(kk + 1) * GRID_W, cols] * scale
        zero = jnp.zeros_like(q2)
        qs = jnp.concatenate([jnp.where(lane < NA_HEAD_DIM, q2, zero), jnp.where(lane < NA_HEAD_DIM, zero, q2)],
                             axis=0)
        s_scr[u, :, 0:n_loc] = _dot_nt(qs, kwin[pl.ds(offs[kk], n_loc), cols])
        s_scr[u, :, n_loc:n_loc + n_ctx] = _dot_nt(qs, kx_ref[:, cols])

    for u, (kk, hp) in enumerate(units):
        s_loc = s_scr[u, :, 0:n_loc] + bias_refs[kk][0, hp]
        s_ctx = s_scr[u, :, n_loc:n_loc + n_ctx]
        m = jnp.maximum(jnp.max(s_loc, axis=-1, keepdims=True), jnp.max(s_ctx, axis=-1, keepdims=True))
        p_loc = jnp.exp(s_loc - m)
        p_ctx = jnp.exp(s_ctx - m)
        inv = 1.0 / (jnp.sum(p_loc, axis=-1, keepdims=True) + jnp.sum(p_ctx, axis=-1, keepdims=True))
        p_scr[u, :, 0:n_loc] = (p_loc * inv).astype(BF16)
        p_scr[u, :, n_loc:n_loc + n_ctx] = (p_ctx * inv).astype(BF16)

    for u, (kk, hp) in enumerate(units):
        cols = slice(hp * LANES, (hp + 1) * LANES)
        o2 = _dot(p_scr[u, :, 0:n_loc], vwin[pl.ds(offs[kk], n_loc), cols])
        o2 += _dot(p_scr[u, :, n_loc:n_loc + n_ctx], vx_ref[:, cols])
        out = jnp.where(lane < NA_HEAD_DIM, o2[0:GRID_W], o2[GRID_W:2 * GRID_W])
        o_ref[kk * GRID_W:(kk + 1) * GRID_W, cols] = out.astype(o_ref.dtype)


def _na_bias_table(rpb):
    qc = jnp.arange(GRID_W)[:, None]
    kc = jnp.arange(GRID_W)[None, :]
    cs = jnp.clip(qc - WIN_C // 2, 0, GRID_W - WIN_C)
    ok = (kc >= cs) & (kc < cs + WIN_C)
    dc = jnp.clip(kc - qc + WIN_C - 1, 0, 2 * WIN_C - 2)
    dr = jnp.arange(WIN_R)[:, None] + jnp.arange(WIN_R)[None, :]
    tab = rpb.astype(F32)[:, dr][:, :, :, dc]
    tab = jnp.where(ok[None, None, None], tab, NEG_BIG)
    tab = jnp.transpose(tab, (1, 0, 3, 2, 4))
    return tab.reshape(WIN_R, NA_HEADS // 2, 2 * GRID_W, WIN_R * GRID_W)


def _na_attention(px, pc, bias, seq_len, ctx_len):
    t = px.shape[0]
    bsz = t // seq_len
    rows = seq_len // GRID_W
    n_g = rows // NA_ROWS
    blk = NA_ROWS * GRID_W

    steps = rows // NA_STEP
    per_blk = NA_ROWS // NA_STEP

    def kv_spec(col, shift):
        def imap(b, s):
            g = jnp.clip(s // per_blk + shift, 0, n_g - 1)
            return (b * n_g + g, col)
        return pl.BlockSpec((blk, NA_WIDTH), imap)

    def bias_spec(kk):
        def imap(b, s):
            r = s * NA_STEP + kk
            r0 = jnp.clip(r - WIN_R // 2, 0, rows - WIN_R)
            return (r0 - r + WIN_R - 1, 0, 0, 0)
        return pl.BlockSpec((1, NA_HEADS // 2, 2 * GRID_W, WIN_R * GRID_W), imap)

    return pl.pallas_call(
        functools.partial(_na_kernel, rows=rows),
        grid=(bsz, steps),
        in_specs=[
            pl.BlockSpec((NA_STEP * GRID_W, NA_WIDTH), lambda b, s: (b * steps + s, COL_NAQ)),
            kv_spec(COL_NAK, -1), kv_spec(COL_NAK, 0), kv_spec(COL_NAK, 1),
            kv_spec(COL_NAV, -1), kv_spec(COL_NAV, 0), kv_spec(COL_NAV, 1),
            pl.BlockSpec((ctx_len, NA_WIDTH), lambda b, s: (b, COL_NAK)),
            pl.BlockSpec((ctx_len, NA_WIDTH), lambda b, s: (b, COL_NAV)),
        ] + [bias_spec(kk) for kk in range(NA_STEP)],
        out_specs=pl.BlockSpec((NA_STEP * GRID_W, NA_WIDTH), lambda b, s: (b * steps + s, 0)),
        out_shape=jax.ShapeDtypeStruct((t, NA_WIDTH), BF16),
        scratch_shapes=[
            pltpu.VMEM((3 * blk, NA_WIDTH), BF16), pltpu.VMEM((3 * blk, NA_WIDTH), BF16),
            pltpu.VMEM((NA_STEP * NA_HEADS // 2, 2 * GRID_W, WIN_R * GRID_W + ctx_len), F32),
            pltpu.VMEM((NA_STEP * NA_HEADS // 2, 2 * GRID_W, WIN_R * GRID_W + ctx_len), BF16),
        ],
        compiler_params=_params(("parallel", "arbitrary")),
        name="na_attn",
    )(px, px, px, px, px, px, px, pc, pc, *([bias] * NA_STEP))


def _ctx_attn_kernel(q_ref, k_ref, v_ref, o_ref):
    n = q_ref.shape[0]
    lane = lax.broadcasted_iota(jnp.int32, (n, LANES), 1)
    scale = NA_HEAD_DIM ** -0.5
    for hp in range(NA_HEADS // 2):
        cols = slice(hp * LANES, (hp + 1) * LANES)
        q2 = q_ref[:, cols] * scale
        k2 = k_ref[:, cols]
        v2 = v_ref[:, cols]
        outs = []
        for sub in range(2):
            sel = (lane < NA_HEAD_DIM) if sub == 0 else (lane >= NA_HEAD_DIM)
            s = _dot_nt(jnp.where(sel, q2, jnp.zeros_like(q2)), k2)
            p = jnp.exp(s - jnp.max(s, axis=-1, keepdims=True))
            outs.append(_dot(p.astype(BF16), v2) / jnp.sum(p, axis=-1, keepdims=True))
        o_ref[:, cols] = jnp.where(lane < NA_HEAD_DIM, outs[0], outs[1]).astype(o_ref.dtype)


def _ctx_attention(pc, ctx_len):
    t = pc.shape[0]
    return pl.pallas_call(
        _ctx_attn_kernel,
        grid=(t // ctx_len,),
        in_specs=[pl.BlockSpec((ctx_len, NA_WIDTH), lambda b, col=col: (b, col))
                  for col in (COL_NAQ, COL_NAK, COL_NAV)],
        out_specs=pl.BlockSpec((ctx_len, NA_WIDTH), lambda b: (b, 0)),
        out_shape=jax.ShapeDtypeStruct((t, NA_WIDTH), BF16),
        compiler_params=_params(("parallel",)),
        name="ctx_attn",
    )(pc, pc, pc)


def _cmul(a, b):
    return a[0] * b[0] - a[1] * b[1], a[0] * b[1] + a[1] * b[0]


def _s5_tables(lam_re, lam_im, log_dt, b_re, b_im, c_re, c_im):
    L, G, S, H = S5_CHUNK, S5_GROUPS, S5_STATE, S5_GROUP
    lr = jnp.minimum(lam_re.astype(F32), S5_MAX_RE)
    li = lam_im.astype(F32)
    dt = jnp.exp(log_dt.astype(F32))[..., None]
    zr, zi = lr * dt, li * dt
    d = jnp.arange(L + 1, dtype=F32)[None, :, None, None]
    mag = jnp.exp(zr[:, None] * d)
    pw = (mag * jnp.cos(zi[:, None] * d), mag * jnp.sin(zi[:, None] * d))
    lb = (pw[0][:, 1], pw[1][:, 1])
    den = lr * lr + li * li
    ratio = (((lb[0] - 1.0) * lr + lb[1] * li) / den, (lb[1] * lr - (lb[0] - 1.0) * li) / den)
    bb = _cmul((ratio[0][..., None], ratio[1][..., None]), (b_re.astype(F32), b_im.astype(F32)))
    cc = (c_re.astype(F32), c_im.astype(F32))

    def kern(x):
        cl = _cmul((cc[0][x][None], cc[1][x][None]), (pw[0][x, :L, :, None, :], pw[1][x, :L, :, None, :]))
        return jnp.einsum('dghp,gpj->dghj', cl[0], bb[0][x]) - jnp.einsum('dghp,gpj->dghj', cl[1], bb[1][x])

    def state_in(x, steps):
        p = (pw[0][x][steps][:, :, None, :], pw[1][x][steps][:, :, None, :])
        r = _cmul(p, (jnp.swapaxes(bb[0][x], 1, 2)[None], jnp.swapaxes(bb[1][x], 1, 2)[None]))
        return jnp.transpose(r[0], (1, 0, 2, 3)), jnp.transpose(r[1], (1, 0, 2, 3))

    def state_out(x, steps):
        p = (pw[0][x][steps][:, :, None, :], pw[1][x][steps][:, :, None, :])
        r = _cmul(p, (cc[0][x][None], cc[1][x][None]))
        return jnp.transpose(r[0], (1, 3, 0, 2)), jnp.transpose(r[1], (1, 3, 0, 2))

    steps = jnp.arange(L)
    wf = state_in(0, L - 1 - steps)
    wb = state_in(1, steps)
    vf = state_out(0, steps + 1)
    vb = state_out(1, L - steps)
    Q, A = S5_QUADS, S5_QUAD_GROUPS
    kd = jnp.stack([kern(0), kern(1)]).reshape(2, L, Q, A, H, H)
    kd = jnp.transpose(kd, (0, 2, 1, 5, 3, 4)).reshape(2, Q, L, H, A * H)
    wk = jnp.stack([wf[0], wf[1], wb[0], wb[1]]).reshape(4, Q, A // 2, 2, L, H, S)
    wd = jnp.transpose(wk, (0, 1, 4, 2, 5, 3, 6)).reshape(4, Q, L, (A // 2) * H, 2 * S)
    vk = jnp.stack([vf[0], -vf[1], vb[0], -vb[1]]).reshape(4, Q, A, S, L, H)
    vd = jnp.transpose(vk, (0, 1, 4, 3, 2, 5)).reshape(4, Q, L, S, A * H)
    a_re = pw[0][:, L].reshape(2, 1, G * S)
    a_im = pw[1][:, L].reshape(2, 1, G * S)
    return wd, kd, vd, a_re, a_im


def _s5_operand(u_ref):
    return jnp.concatenate([u_ref[:, s, :] for s in range(S5_CHUNK)], axis=1).astype(BF16)


def _s5_expand_w(wd_ref, w_scr):
    H, half = S5_GROUP, S5_STATE
    w_scr[...] = jnp.zeros_like(w_scr)
    lane = lax.broadcasted_iota(jnp.int32, (H, LANES), 1)
    for k in range(4):
        for s in range(S5_CHUNK):
            for c in range(S5_QUAD_GROUPS // 2):
                blk = wd_ref[k, 0, s, c * H:(c + 1) * H, :]
                cols = slice(k * S5_QSTATE + c * LANES, k * S5_QSTATE + (c + 1) * LANES)
                row = s * LANES + 2 * c * H
                w_scr[row:row + H, cols] = jnp.where(lane < half, blk, 0.0).astype(BF16)
                w_scr[row + H:row + 2 * H, cols] = jnp.where(lane >= half, blk, 0.0).astype(BF16)


def _s5_expand_mv(kd_ref, vd_ref, m_scr, v_scr):
    H, L = S5_GROUP, S5_CHUNK
    rg = lax.broadcasted_iota(jnp.int32, (LANES, LANES), 0) // H
    lg = lax.broadcasted_iota(jnp.int32, (LANES, LANES), 1) // H
    tiles = []
    for x in range(2):
        tiles.append([jnp.where(rg == lg, jnp.concatenate([kd_ref[x, 0, d]] * S5_QUAD_GROUPS, axis=0), 0.0)
                      for d in range(L)])
    for s in range(L):
        for t in range(L):
            blk = tiles[0][t - s] if t > s else (tiles[1][s - t] if s > t else tiles[0][0] + tiles[1][0])
            m_scr[s * LANES:(s + 1) * LANES, t * LANES:(t + 1) * LANES] = blk.astype(BF16)
    lg2 = lax.broadcasted_iota(jnp.int32, (S5_STATE, LANES), 1) // H
    for k in range(4):
        for t in range(L):
            blk = vd_ref[k, 0, t]
            for b in range(S5_QUAD_GROUPS):
                row = k * S5_QSTATE + b * S5_STATE
                v_scr[row:row + S5_STATE, t * LANES:(t + 1) * LANES] = jnp.where(lg2 == b, blk, 0.0).astype(BF16)


def _s5_local_kernel(u_ref, wd_ref, sfr_ref, sfi_ref, sbr_ref, sbi_ref, w_scr):
    @pl.when(pl.program_id(1) == 0)
    def _():
        _s5_expand_w(wd_ref, w_scr)

    s = _dot(_s5_operand(u_ref), w_scr[...])
    n = S5_QSTATE
    sfr_ref[...] = s[:, 0:n]
    sfi_ref[...] = s[:, n:2 * n]
    sbr_ref[...] = s[:, 2 * n:3 * n]
    sbi_ref[...] = s[:, 3 * n:4 * n]


def _s5_local_states(u3, wd):
    rows = u3.shape[0]
    tr = min(256, rows)
    assert rows % tr == 0
    out = jax.ShapeDtypeStruct((rows, S5_GROUPS * S5_STATE), F32)
    return pl.pallas_call(
        _s5_local_kernel,
        grid=(S5_QUADS, rows // tr),
        in_specs=[
            pl.BlockSpec((tr, S5_CHUNK, LANES), lambda q, i: (i, 0, q)),
            pl.BlockSpec((4, 1) + wd.shape[2:], lambda q, i: (0, q, 0, 0, 0)),
        ],
        out_specs=[pl.BlockSpec((tr, S5_QSTATE), lambda q, i: (i, q))] * 4,
        out_shape=[out] * 4,
        scratch_shapes=[pltpu.VMEM((S5_QCOLS, 4 * S5_QSTATE), BF16)],
        compiler_params=_params(("arbitrary", "arbitrary")),
        name="s5_local",
    )(u3, wd)


def _s5_scan_kernel(sr_ref, si_ref, ar_ref, ai_ref, h0r_ref, h0i_ref, hr_ref, hi_ref, fr_ref, fi_ref,
                    hr_scr, hi_scr, *, reverse):
    i = pl.program_id(1)
    n_i = pl.num_programs(1)
    steps = sr_ref.shape[1]

    @pl.when(i == 0)
    def _():
        hr_scr[...] = h0r_ref[...]
        hi_scr[...] = h0i_ref[...]

    ar = ar_ref[...]
    ai = ai_ref[...]

    def body(j, carry):
        hr, hi = carry
        c = (steps - 1 - j) if reverse else j
        hr_ref[:, c, :] = hr
        hi_ref[:, c, :] = hi
        nr = ar * hr - ai * hi + sr_ref[:, c, :]
        ni = ar * hi + ai * hr + si_ref[:, c, :]
        return nr, ni

    hr, hi = lax.fori_loop(0, steps, body, (hr_scr[...], hi_scr[...]))
    hr_scr[...] = hr
    hi_scr[...] = hi

    @pl.when(i == n_i - 1)
    def _():
        fr_ref[...] = hr
        fi_ref[...] = hi


def _s5_chunk_scan(s_re, s_im, a_re, a_im, h0_re, h0_im, bsz, reverse):
    _, n_c, width = s_re.shape
    tr = min(64, n_c)
    tc = 512
    n_i = n_c // tr

    def rmap(j, i):
        return (0, (n_i - 1 - i) if reverse else i, j)

    big = jax.ShapeDtypeStruct((bsz, n_c, width), F32)
    small = jax.ShapeDtypeStruct((bsz, width), F32)
    return pl.pallas_call(
        functools.partial(_s5_scan_kernel, reverse=reverse),
        grid=(width // tc, n_i),
        in_specs=[
            pl.BlockSpec((bsz, tr, tc), rmap), pl.BlockSpec((bsz, tr, tc), rmap),
            pl.BlockSpec((1, tc), lambda j, i: (0, j)), pl.BlockSpec((1, tc), lambda j, i: (0, j)),
            pl.BlockSpec((bsz, tc), lambda j, i: (0, j)), pl.BlockSpec((bsz, tc), lambda j, i: (0, j)),
        ],
        out_specs=[
            pl.BlockSpec((bsz, tr, tc), rmap), pl.BlockSpec((bsz, tr, tc), rmap),
            pl.BlockSpec((bsz, tc), lambda j, i: (0, j)), pl.BlockSpec((bsz, tc), lambda j, i: (0, j)),
        ],
        out_shape=[big, big, small, small],
        scratch_shapes=[pltpu.VMEM((bsz, tc), F32), pltpu.VMEM((bsz, tc), F32)],
        compiler_params=_params(("parallel", "arbitrary")),
        name="s5_scan_rev" if reverse else "s5_scan_fwd",
    )(s_re, s_im, a_re, a_im, h0_re, h0_im)


def _s5_out_kernel(u_ref, kd_ref, vd_ref, hfr_ref, hfi_ref, hbr_ref, hbi_ref, y_ref, m_scr, v_scr):
    @pl.when(pl.program_id(1) == 0)
    def _():
        _s5_expand_mv(kd_ref, vd_ref, m_scr, v_scr)

    n = S5_QSTATE
    y = _dot(_s5_operand(u_ref), m_scr[...])
    for k, h_ref in enumerate((hfr_ref, hfi_ref, hbr_ref, hbi_ref)):
        y += _dot(h_ref[...].astype(BF16), v_scr[k * n:(k + 1) * n, :])
    for t in range(S5_CHUNK):
        y_ref[:, t, :] = y[:, t * LANES:(t + 1) * LANES]


def _s5_outputs(u3, kd, vd, h_states):
    rows = u3.shape[0]
    tr = min(256, rows)
    return pl.pallas_call(
        _s5_out_kernel,
        grid=(S5_QUADS, rows // tr),
        in_specs=[
            pl.BlockSpec((tr, S5_CHUNK, LANES), lambda q, i: (i, 0, q)),
            pl.BlockSpec((2, 1) + kd.shape[2:], lambda q, i: (0, q, 0, 0, 0)),
            pl.BlockSpec((4, 1) + vd.shape[2:], lambda q, i: (0, q, 0, 0, 0)),
        ] + [pl.BlockSpec((tr, S5_QSTATE), lambda q, i: (i, q))] * 4,
        out_specs=pl.BlockSpec((tr, S5_CHUNK, LANES), lambda q, i: (i, 0, q)),
        out_shape=jax.ShapeDtypeStruct((rows, S5_CHUNK, S5_WIDTH), F32),
        scratch_shapes=[pltpu.VMEM((S5_QCOLS, S5_QCOLS), BF16), pltpu.VMEM((4 * S5_QSTATE, S5_QCOLS), BF16)],
        compiler_params=_params(("arbitrary", "arbitrary")),
        name="s5_out",
    )(u3, kd, vd, *h_states)


def _s5_branch(u, tables, seq_len, init, want_y):
    w_q, m_q, v_q, a_re, a_im = tables
    t = u.shape[0]
    bsz = t // seq_len
    n_c = seq_len // S5_CHUNK
    width = S5_GROUPS * S5_STATE
    u3 = u.reshape(t // S5_CHUNK, S5_CHUNK, S5_WIDTH)
    s_loc = [s.reshape(bsz, n_c, width) for s in _s5_local_states(u3, w_q)]
    hfr, hfi, ffr, ffi = _s5_chunk_scan(s_loc[0], s_loc[1], a_re[0], a_im[0], init[0], init[1], bsz, False)
    hbr, hbi, fbr, fbi = _s5_chunk_scan(s_loc[2], s_loc[3], a_re[1], a_im[1], init[2], init[3], bsz, True)
    y = None
    if want_y:
        h_states = [h.reshape(bsz * n_c, width) for h in (hfr, hfi, hbr, hbi)]
        y = _s5_outputs(u3, m_q, v_q, h_states).reshape(t, S5_WIDTH)
    return y, (ffr, ffi, fbr, fbi)


def _gelu_tanh(x):
    return 0.5 * x * (1.0 + jnp.tanh(math.sqrt(2.0 / math.pi) * (x + 0.044715 * (x * x * x))))


def _merge_kernel(x_ref, ym_ref, yn_ref, ys_ref, u_ref, gm_ref, gn_ref, gs_ref, d_ref, gw_ref, gb_ref,
                  wm_ref, wn_ref, ws_ref, wo_ref, gate_ref, npost_ref, o_ref):
    ys = ys_ref[...] + d_ref[...] * u_ref[...]
    g = _gelu_tanh(ys)
    gg = g * jax.nn.sigmoid(_dot(g.astype(BF16), gw_ref[...]) + gb_ref[...])
    y = jax.nn.sigmoid(gm_ref[...].astype(F32)) * _dot(ym_ref[...], wm_ref[...])
    y += jax.nn.sigmoid(gn_ref[...].astype(F32)) * _dot(yn_ref[...], wn_ref[...])
    y += jax.nn.sigmoid(gs_ref[...].astype(F32)) * _dot(gg.astype(BF16), ws_ref[...])
    out = _dot(y.astype(BF16), wo_ref[...])
    o_ref[...] = x_ref[...] + gate_ref[0] * _rms(out, npost_ref[...])


def _merge(x, ym, yn, ys, u, px, wts, gate, npost, rows_per_mod):
    t = x.shape[0]
    tm = min(512, rows_per_mod)
    per = rows_per_mod // tm
    s5_d, glu_w, glu_b, w_m, w_n, w_s, w_o = wts
    row = lambda i: (i, 0)
    const = lambda i: (0, 0)
    return pl.pallas_call(
        _merge_kernel,
        grid=(t // tm,),
        in_specs=[
            pl.BlockSpec((tm, D_MODEL), row),
            pl.BlockSpec((tm, M_WIDTH), row),
            pl.BlockSpec((tm, NA_WIDTH), row),
            pl.BlockSpec((tm, S5_WIDTH), row),
            pl.BlockSpec((tm, S5_WIDTH), row),
            pl.BlockSpec((tm, D_MODEL), lambda i: (i, COL_GATE)),
            pl.BlockSpec((tm, D_MODEL), lambda i: (i, COL_GATE + 1)),
            pl.BlockSpec((tm, D_MODEL), lambda i: (i, COL_GATE + 2)),
            pl.BlockSpec((1, S5_WIDTH), const),
            pl.BlockSpec((S5_WIDTH, S5_WIDTH), const),
            pl.BlockSpec((1, S5_WIDTH), const),
            pl.BlockSpec((M_WIDTH, D_MODEL), const),
            pl.BlockSpec((NA_WIDTH, D_MODEL), const),
            pl.BlockSpec((S5_WIDTH, D_MODEL), const),
            pl.BlockSpec((D_MODEL, D_MODEL), const),
            pl.BlockSpec((1, 1, D_MODEL), lambda i: (i // per, 0, 0)),
            pl.BlockSpec((1, D_MODEL), const),
        ],
        out_specs=pl.BlockSpec((tm, D_MODEL), row),
        out_shape=jax.ShapeDtypeStruct((t, D_MODEL), F32),
        compiler_params=_params(("parallel",)),
        name="merge",
    )(x, ym, yn, ys, u, px, px, px, s5_d, glu_w, glu_b, w_m, w_n, w_s, w_o, gate, npost)


def _swiglu(h, wg, wu, wd):
    a = _dot(h, wg)
    a = a * jax.nn.sigmoid(a) * _dot(h, wu)
    return _dot(a.astype(BF16), wd)


def _ffn_kernel(x_ref, g_ref, sh_ref, sc_ref, wg_ref, wu_ref, wd_ref, gate_ref, npost_ref,
                o_ref, hn_ref, acc_ref):
    j = pl.program_id(1)

    @pl.when(j == 0)
    def _():
        h = _rms(x_ref[...], g_ref[...]) * (1.0 + sc_ref[0]) + sh_ref[0]
        hn_ref[...] = h.astype(BF16)
        acc_ref[...] = jnp.zeros_like(acc_ref)

    acc_ref[...] += _swiglu(hn_ref[...], wg_ref[...], wu_ref[...], wd_ref[...])

    @pl.when(j == pl.num_programs(1) - 1)
    def _():
        o_ref[...] = x_ref[...] + gate_ref[0] * _rms(acc_ref[...], npost_ref[...])


def _ffn(x, g, shift, scale, w_gate, w_up, w_down, gate, npost, rows_per_mod):
    t = x.shape[0]
    ff = w_gate.shape[1]
    ff_tile = ff // 2
    tm = min(512, rows_per_mod)
    per = rows_per_mod // tm
    assert ff_tile % LANES == 0
    return pl.pallas_call(
        _ffn_kernel,
        grid=(t // tm, ff // ff_tile),
        in_specs=[
            pl.BlockSpec((tm, D_MODEL), lambda i, j: (i, 0)),
            pl.BlockSpec((1, D_MODEL), lambda i, j: (0, 0)),
            pl.BlockSpec((1, 1, D_MODEL), lambda i, j: (i // per, 0, 0)),
            pl.BlockSpec((1, 1, D_MODEL), lambda i, j: (i // per, 0, 0)),
            pl.BlockSpec((D_MODEL, ff_tile), lambda i, j: (0, j)),
            pl.BlockSpec((D_MODEL, ff_tile), lambda i, j: (0, j)),
            pl.BlockSpec((ff_tile, D_MODEL), lambda i, j: (j, 0)),
            pl.BlockSpec((1, 1, D_MODEL), lambda i, j: (i // per, 0, 0)),
            pl.BlockSpec((1, D_MODEL), lambda i, j: (0, 0)),
        ],
        out_specs=pl.BlockSpec((tm, D_MODEL), lambda i, j: (i, 0)),
        out_shape=jax.ShapeDtypeStruct((t, D_MODEL), F32),
        scratch_shapes=[pltpu.VMEM((tm, D_MODEL), BF16), pltpu.VMEM((tm, D_MODEL), F32)],
        compiler_params=_params(("parallel", "arbitrary")),
        name="ffn",
    )(x, g, shift, scale, w_gate, w_up, w_down, gate, npost)


def _moe_kernel(x_ref, g_ref, sh_ref, sc_ref, rh_ref, rl_ref, wg_ref, wu_ref, wd_ref, gate_ref, npost_ref,
                o_ref, hn_ref, acc_ref, comb_ref, rank_c_ref, rank_r_ref, mask_r_ref, xc_ref, yc_ref, cnt_ref):
    e = pl.program_id(1)
    j = pl.program_id(2)
    tm = x_ref.shape[0]
    cap = MOE_CAP

    @pl.when((e == 0) & (j == 0))
    def _():
        h = _rms(x_ref[...], g_ref[...]) * (1.0 + sc_ref[0]) + sh_ref[0]
        hi = h.astype(BF16)
        hn_ref[...] = hi
        acc_ref[...] = jnp.zeros_like(acc_ref)
        lo = (h - hi.astype(F32)).astype(BF16)
        logits = _dot(hi, rh_ref[...]) + _dot(lo, rh_ref[...]) + _dot(hi, rl_ref[...])
        lane = lax.broadcasted_iota(jnp.int32, logits.shape, 1)
        logits = jnp.where(lane < N_EXPERTS, logits, -jnp.inf)
        m1 = jnp.max(logits, axis=-1, keepdims=True)
        i1 = jnp.min(jnp.where(logits == m1, lane, LANES), axis=-1, keepdims=True)
        rest = jnp.where(lane == i1, -jnp.inf, logits)
        m2 = jnp.max(rest, axis=-1, keepdims=True)
        i2 = jnp.min(jnp.where(rest == m2, lane, LANES), axis=-1, keepdims=True)
        e2 = jnp.exp(m2 - m1)
        comb_ref[...] = jnp.where(lane == i1, 1.0 / (1.0 + e2), 0.0) + jnp.where(lane == i2, e2 / (1.0 + e2), 0.0)
        sel = jnp.where((lane == i1) | (lane == i2), 1.0, 0.0)
        ti = lax.broadcasted_iota(jnp.int32, (tm, tm), 0)
        si = lax.broadcasted_iota(jnp.int32, (tm, tm), 1)
        before = jnp.where(si < ti, 1.0, 0.0).astype(BF16)
        rank_c_ref[...] = _dot(before, sel.astype(BF16))
        sel_t = sel.T
        after = jnp.where(ti < si, 1.0, 0.0).astype(BF16)
        rank_r_ref[...] = _dot(sel_t.astype(BF16), after)
        mask_r_ref[...] = sel_t
        counts = jnp.sum(sel_t, axis=1, keepdims=True)
        for k in range(N_EXPERTS):
            cnt_ref[k] = jnp.sum(counts[k:k + 1, :]).astype(jnp.int32)

    n_pass = (cnt_ref[e] + (cap - 1)) // cap

    @pl.when(j == 0)
    def _():
        rank_row = rank_r_ref[pl.ds(e, 1), :]
        mask_row = mask_r_ref[pl.ds(e, 1), :]
        slot = lax.broadcasted_iota(jnp.int32, (cap, tm), 0).astype(F32)

        def compact(p, _):
            base = (p * cap).astype(F32)
            onehot = jnp.where((rank_row - base == slot) & (mask_row > 0.0), 1.0, 0.0).astype(BF16)
            rows = pl.ds(pl.multiple_of(p * cap, 16), cap)
            xc_ref[rows, :] = _dot(onehot, hn_ref[...]).astype(BF16)
            yc_ref[rows, :] = jnp.zeros((cap, D_MODEL), F32)
            return 0

        lax.fori_loop(0, n_pass, compact, 0)

    def expert(start, size):
        rows = pl.ds(pl.multiple_of(start, MOE_GRAIN), size)
        yc_ref[rows, :] += _swiglu(xc_ref[rows, :], wg_ref[0], wu_ref[0], wd_ref[0])

    n_tok = cnt_ref[e]
    n_full = n_tok // cap
    rem_blocks = (n_tok - n_full * cap + (MOE_GRAIN - 1)) // MOE_GRAIN
    n_full = n_full + (rem_blocks == cap // MOE_GRAIN).astype(jnp.int32)

    def full_pass(p, _):
        expert(p * cap, cap)
        return 0

    lax.fori_loop(0, n_full, full_pass, 0)
    for blocks in range(1, cap // MOE_GRAIN):
        @pl.when(rem_blocks == blocks)
        def _(blocks=blocks):
            expert(n_full * cap, blocks * MOE_GRAIN)

    @pl.when(j == pl.num_programs(2) - 1)
    def _():
        lane = lax.broadcasted_iota(jnp.int32, (tm, LANES), 1)
        pick = lane == e
        rank_col = jnp.sum(jnp.where(pick, rank_c_ref[...], 0.0), axis=-1, keepdims=True)
        w_col = jnp.sum(jnp.where(pick, comb_ref[...], 0.0), axis=-1, keepdims=True)
        slot = lax.broadcasted_iota(jnp.int32, (tm, cap), 1).astype(F32)

        def expand(p, _):
            base = (p * cap).astype(F32)
            onehot = jnp.where(rank_col - base == slot, 1.0, 0.0).astype(BF16)
            rows = pl.ds(pl.multiple_of(p * cap, 16), cap)
            acc_ref[...] += w_col * _dot(onehot, yc_ref[rows, :].astype(BF16))
            return 0

        lax.fori_loop(0, n_pass, expand, 0)

    @pl.when((e == N_EXPERTS - 1) & (j == pl.num_programs(2) - 1))
    def _():
        o_ref[...] = x_ref[...] + gate_ref[0] * _rms(acc_ref[...], npost_ref[...])


def _moe(x, g, shift, scale, r_hi, r_lo, w_gate, w_up, w_down, gate, npost, rows_per_mod):
    t = x.shape[0]
    n_e, _, ff = w_gate.shape
    assert n_e == N_EXPERTS
    ff_tile = ff // 4
    tm = min(MOE_TILE, rows_per_mod)
    per = rows_per_mod // tm
    passes = -(-tm // MOE_CAP)
    assert ff_tile % LANES == 0 and MOE_CAP % MOE_GRAIN == 0 and MOE_GRAIN % 16 == 0
    return pl.pallas_call(
        _moe_kernel,
        grid=(t // tm, n_e, ff // ff_tile),
        in_specs=[
            pl.BlockSpec((tm, D_MODEL), lambda i, e, j: (i, 0)),
            pl.BlockSpec((1, D_MODEL), lambda i, e, j: (0, 0)),
            pl.BlockSpec((1, 1, D_MODEL), lambda i, e, j: (i // per, 0, 0)),
            pl.BlockSpec((1, 1, D_MODEL), lambda i, e, j: (i // per, 0, 0)),
            pl.BlockSpec((D_MODEL, LANES), lambda i, e, j: (0, 0)),
            pl.BlockSpec((D_MODEL, LANES), lambda i, e, j: (0, 0)),
            pl.BlockSpec((1, D_MODEL, ff_tile), lambda i, e, j: (e, 0, j)),
            pl.BlockSpec((1, D_MODEL, ff_tile), lambda i, e, j: (e, 0, j)),
            pl.BlockSpec((1, ff_tile, D_MODEL), lambda i, e, j: (e, j, 0)),
            pl.BlockSpec((1, 1, D_MODEL), lambda i, e, j: (i // per, 0, 0)),
            pl.BlockSpec((1, D_MODEL), lambda i, e, j: (0, 0)),
        ],
        out_specs=pl.BlockSpec((tm, D_MODEL), lambda i, e, j: (i, 0)),
        out_shape=jax.ShapeDtypeStruct((t, D_MODEL), F32),
        scratch_shapes=[
            pltpu.VMEM((tm, D_MODEL), BF16),
            pltpu.VMEM((tm, D_MODEL), F32),
            pltpu.VMEM((tm, LANES), F32),
            pltpu.VMEM((tm, LANES), F32),
            pltpu.VMEM((LANES, tm), F32),
            pltpu.VMEM((LANES, tm), F32),
            pltpu.VMEM((passes * MOE_CAP, D_MODEL), BF16),
            pltpu.VMEM((passes * MOE_CAP, D_MODEL), F32),
            pltpu.SMEM((N_EXPERTS,), jnp.int32),
        ],
        compiler_params=_params(("parallel", "arbitrary", "arbitrary")),
        name="moe",
    )(x, g, shift, scale, r_hi, r_lo, w_gate, w_up, w_down, gate, npost)


MOE_ROWS = MOE_TILE * TOP_K + N_EXPERTS * MOE_GRAIN
MOE_META = 2 * N_EXPERTS
MOE_PASS_ROWS = 512


def _moe_dispatch_kernel(x_ref, g_ref, sh_ref, sc_ref, rh_ref, rl_ref, xs_ref, route_ref, meta_ref):
    tm = x_ref.shape[0]
    h = _rms(x_ref[...], g_ref[...]) * (1.0 + sc_ref[0]) + sh_ref[0]
    hi = h.astype(BF16)
    lo = (h - hi.astype(F32)).astype(BF16)
    logits = _dot(hi, rh_ref[...]) + _dot(lo, rh_ref[...]) + _dot(hi, rl_ref[...])
    lane = lax.broadcasted_iota(jnp.int32, logits.shape, 1)
    logits = jnp.where(lane < N_EXPERTS, logits, -jnp.inf)
    m1 = jnp.max(logits, axis=-1, keepdims=True)
    i1 = jnp.min(jnp.where(logits == m1, lane, LANES), axis=-1, keepdims=True)
    rest = jnp.where(lane == i1, -jnp.inf, logits)
    m2 = jnp.max(rest, axis=-1, keepdims=True)
    i2 = jnp.min(jnp.where(rest == m2, lane, LANES), axis=-1, keepdims=True)
    e2 = jnp.exp(m2 - m1)
    sel1 = jnp.where(lane == i1, 1.0, 0.0)
    sel2 = jnp.where(lane == i2, 1.0, 0.0)
    sel = sel1 + sel2
    sel1_t, sel2_t = sel1.T, sel2.T
    sel_t = sel1_t + sel2_t

    ti = lax.broadcasted_iota(jnp.int32, (tm, tm), 0)
    si = lax.broadcasted_iota(jnp.int32, (tm, tm), 1)
    rank_c = _dot(jnp.where(si < ti, 1.0, 0.0).astype(BF16), sel.astype(BF16))
    rank_r = _dot(sel_t.astype(BF16), jnp.where(ti < si, 1.0, 0.0).astype(BF16))
    blk_row = jnp.ceil(jnp.sum(sel, axis=0, keepdims=True) * (1.0 / MOE_GRAIN))
    blk_col = jnp.ceil(jnp.sum(sel_t, axis=1, keepdims=True) * (1.0 / MOE_GRAIN))
    ea = lax.broadcasted_iota(jnp.int32, (LANES, LANES), 0)
    eb = lax.broadcasted_iota(jnp.int32, (LANES, LANES), 1)
    off_row = _dot(jnp.broadcast_to(blk_row, (8, LANES)).astype(BF16),
                   jnp.where(ea < eb, 1.0, 0.0).astype(BF16))[0:1, :]
    off_col = _dot(jnp.where(eb < ea, 1.0, 0.0).astype(BF16),
                   jnp.broadcast_to(blk_col, (LANES, LANES)).astype(BF16))[:, 0:1]

    pos_c = off_row * MOE_GRAIN + rank_c
    pos1_c = jnp.sum(jnp.where(lane == i1, pos_c, 0.0), axis=-1, keepdims=True)
    pos2_c = jnp.sum(jnp.where(lane == i2, pos_c, 0.0), axis=-1, keepdims=True)
    route_ref[...] = (jnp.where(lane == 0, pos1_c, 0.0) + jnp.where(lane == 1, pos2_c, 0.0)
                      + jnp.where(lane == 2, 1.0 / (1.0 + e2), 0.0) + jnp.where(lane == 3, e2 / (1.0 + e2), 0.0))
    pos_r = off_col * MOE_GRAIN + rank_r
    pos1_r = jnp.sum(sel1_t * pos_r, axis=0, keepdims=True)
    pos2_r = jnp.sum(sel2_t * pos_r, axis=0, keepdims=True)
    for c in range(MOE_ROWS // MOE_CAP):
        slot = (lax.broadcasted_iota(jnp.int32, (MOE_CAP, tm), 0) + c * MOE_CAP).astype(F32)
        onehot = jnp.where(slot == pos1_r, 1.0, jnp.where(slot == pos2_r, 1.0, 0.0)).astype(BF16)
        xs_ref[c * MOE_CAP:(c + 1) * MOE_CAP, :] = _dot(onehot, hi).astype(BF16)

    lane8 = lax.broadcasted_iota(jnp.int32, (8, LANES), 1)
    blk8 = jnp.broadcast_to(blk_row, (8, LANES))
    off8 = pltpu.roll(jnp.broadcast_to(off_row, (8, LANES)), N_EXPERTS, 1)
    meta = jnp.where(lane8 < N_EXPERTS, blk8, jnp.where(lane8 < MOE_META, off8, 0.0))
    meta_ref[0] = meta.astype(jnp.int32)


def _moe_dispatch(x, g, shift, scale, r_hi, r_lo, rows_per_mod):
    t = x.shape[0]
    tm = MOE_TILE
    assert rows_per_mod % tm == 0
    per = rows_per_mod // tm
    n_t = t // tm
    return pl.pallas_call(
        _moe_dispatch_kernel,
        grid=(n_t,),
        in_specs=[
            pl.BlockSpec((tm, D_MODEL), lambda i: (i, 0)),
            pl.BlockSpec((1, D_MODEL), lambda i: (0, 0)),
            pl.BlockSpec((1, 1, D_MODEL), lambda i: (i // per, 0, 0)),
            pl.BlockSpec((1, 1, D_MODEL), lambda i: (i // per, 0, 0)),
            pl.BlockSpec((D_MODEL, LANES), lambda i: (0, 0)),
            pl.BlockSpec((D_MODEL, LANES), lambda i: (0, 0)),
        ],
        out_specs=[
            pl.BlockSpec((MOE_ROWS, D_MODEL), lambda i: (i, 0)),
            pl.BlockSpec((tm, LANES), lambda i: (i, 0)),
            pl.BlockSpec((1, 8, LANES), lambda i: (i, 0, 0)),
        ],
        out_shape=[
            jax.ShapeDtypeStruct((n_t * MOE_ROWS, D_MODEL), BF16),
            jax.ShapeDtypeStruct((t, LANES), F32),
            jax.ShapeDtypeStruct((n_t, 8, LANES), jnp.int32),
        ],
        compiler_params=_params(("parallel",)),
        name="moe_dispatch",
    )(x, g, shift, scale, r_hi, r_lo)


def _moe_experts_kernel(meta_ref, xs_ref, wg_ref, wu_ref, wd_ref, out_ref, *, ff_tile):
    e = pl.program_id(0)
    i = pl.program_id(1)
    out_ref[...] = xs_ref[...]
    n_blk = meta_ref[i * MOE_META + e]
    first = meta_ref[i * MOE_META + N_EXPERTS + e] * MOE_GRAIN
    per_pass = MOE_PASS_ROWS // MOE_GRAIN
    n_full = n_blk // per_pass
    rem = n_blk - n_full * per_pass
    ff = wg_ref.shape[2]

    def run(start, size):
        rows = pl.ds(pl.multiple_of(start, MOE_GRAIN), size)
        xin = xs_ref[rows, :]
        y = jnp.zeros((size, D_MODEL), F32)
        for j in range(ff // ff_tile):
            cols = slice(j * ff_tile, (j + 1) * ff_tile)
            y += _swiglu(xin, wg_ref[0, :, cols], wu_ref[0, :, cols], wd_ref[0, cols, :])
        out_ref[rows, :] = y.astype(out_ref.dtype)

    def full_pass(p, _):
        run(first + p * MOE_PASS_ROWS, MOE_PASS_ROWS)
        return 0

    lax.fori_loop(0, n_full, full_pass, 0)
    for blocks in range(1, per_pass):
        @pl.when(rem == blocks)
        def _(blocks=blocks):
            run(first + n_full * MOE_PASS_ROWS, blocks * MOE_GRAIN)


def _moe_experts(xs, meta, w_gate, w_up, w_down):
    n_e, _, ff = w_gate.shape
    n_t = xs.shape[0] // MOE_ROWS
    ff_tile = 512
    assert ff % ff_tile == 0
    resident = pl.Buffered(1)
    grid_spec = pltpu.PrefetchScalarGridSpec(
        num_scalar_prefetch=1,
        grid=(n_e, n_t),
        in_specs=[
            pl.BlockSpec((MOE_ROWS, D_MODEL), lambda e, i, m: (i, 0)),
            pl.BlockSpec((1, D_MODEL, ff), lambda e, i, m: (e, 0, 0), pipeline_mode=resident),
            pl.BlockSpec((1, D_MODEL, ff), lambda e, i, m: (e, 0, 0), pipeline_mode=resident),
            pl.BlockSpec((1, ff, D_MODEL), lambda e, i, m: (e, 0, 0), pipeline_mode=resident),
        ],
        out_specs=pl.BlockSpec((MOE_ROWS, D_MODEL), lambda e, i, m: (i, 0)),
    )
    return pl.pallas_call(
        functools.partial(_moe_experts_kernel, ff_tile=ff_tile),
        grid_spec=grid_spec,
        out_shape=jax.ShapeDtypeStruct(xs.shape, xs.dtype),
        input_output_aliases={1: 0},
        compiler_params=_params(("arbitrary", "arbitrary")),
        name="moe_experts",
    )(meta, xs, w_gate, w_up, w_down)


def _moe_combine_kernel(ys_ref, route_ref, x_ref, gate_ref, npost_ref, o_ref):
    tm = x_ref.shape[0]
    lane = lax.broadcasted_iota(jnp.int32, (tm, LANES), 1)
    route = route_ref[...]
    pos1, pos2, w1, w2 = (jnp.sum(jnp.where(lane == k, route, 0.0), axis=-1, keepdims=True) for k in range(4))
    width = 512
    acc = jnp.zeros((tm, D_MODEL), F32)
    for c in range(MOE_ROWS // width):
        slot = (lax.broadcasted_iota(jnp.int32, (tm, width), 1) + c * width).astype(F32)
        wmat = jnp.where(slot == pos1, w1, 0.0) + jnp.where(slot == pos2, w2, 0.0)
        acc += _dot(wmat.astype(BF16), ys_ref[c * width:(c + 1) * width, :])
    o_ref[...] = x_ref[...] + gate_ref[0] * _rms(acc, npost_ref[...])


def _moe_combine(ys, route, x, gate, npost, rows_per_mod):
    t = x.shape[0]
    tm = MOE_TILE
    per = rows_per_mod // tm
    return pl.pallas_call(
        _moe_combine_kernel,
        grid=(t // tm,),
        in_specs=[
            pl.BlockSpec((MOE_ROWS, D_MODEL), lambda i: (i, 0)),
            pl.BlockSpec((tm, LANES), lambda i: (i, 0)),
            pl.BlockSpec((tm, D_MODEL), lambda i: (i, 0)),
            pl.BlockSpec((1, 1, D_MODEL), lambda i: (i // per, 0, 0)),
            pl.BlockSpec((1, D_MODEL), lambda i: (0, 0)),
        ],
        out_specs=pl.BlockSpec((tm, D_MODEL), lambda i: (i, 0)),
        out_shape=jax.ShapeDtypeStruct((t, D_MODEL), F32),
        compiler_params=_params(("parallel",)),
        name="moe_combine",
    )(ys, route, x, gate, npost)


def _moe_sorted(x, g, shift, scale, r_hi, r_lo, w_gate, w_up, w_down, gate, npost, rows_per_mod):
    xs, route, meta = _moe_dispatch(x, g, shift, scale, r_hi, r_lo, rows_per_mod)
    meta = meta[:, 0, :MOE_META].reshape(-1)
    ys = _moe_experts(xs, meta, w_gate, w_up, w_down)
    return _moe_combine(ys, route, x, gate, npost, rows_per_mod)


def _reorder_w_in(w):
    o = 4 * M_WIDTH
    gates = jnp.pad(w[:, o:o + M_GATES], ((0, 0), (0, LANES - M_GATES)))
    na = w[:, o + M_GATES:o + M_GATES + 3 * NA_WIDTH]
    u0 = o + M_GATES + 3 * NA_WIDTH
    main = jnp.concatenate([w[:, :o], w[:, u0 + S5_WIDTH:], na], axis=1)
    return main.astype(BF16), gates.astype(BF16), w[:, u0:u0 + S5_WIDTH].astype(BF16)


def _row(v):
    return v.reshape(1, -1).astype(F32)


def _token_mixers(x, xc, mx, mc, lp, rope, seq_len, ctx_len, ctx_out):
    bsz = x.shape[0] // seq_len
    w_main, w_gates, w_u = _reorder_w_in(lp['w_in'])
    g_pre = _row(lp['norm_mix_pre'])
    px, gx, ux = _inproj(x, g_pre, mx[0], mx[1], w_main, w_gates, w_u, seq_len)
    pc, gc, uc = _inproj(xc, g_pre, mc[0], mc[1], w_main, w_gates, w_u, xc.shape[0])

    conv_w = lp['m_conv_w'].astype(F32)
    conv_b = _row(lp['m_conv_b'])
    gate_b = jnp.pad(_row(lp['m_gate_b']), ((0, 0), (0, LANES - M_GATES)))
    m_norm = _row(lp['m_norm'])
    qk_c = _qkconv(pc, conv_w, conv_b, rope[0][:ctx_len], rope[1][:ctx_len], ctx_len, False)
    qk_x = _qkconv(px, conv_w, conv_b, rope[0], rope[1], seq_len, True)
    zero = _mlstm_zero_state(bsz)
    ym_c, fin_f, fin_b = _mlstm_bidir(qk_c, pc, gc, gate_b, m_norm, zero, zero, ctx_len)
    ym_x, _, _ = _mlstm_bidir(qk_x, px, gx, gate_b, m_norm, fin_f, fin_b, seq_len)

    yn_x = _na_attention(px, pc, _na_bias_table(lp['na_rpb']), seq_len, ctx_len)

    tables = _s5_tables(lp['s5_lam_re'], lp['s5_lam_im'], lp['s5_log_dt'], lp['s5_b_re'], lp['s5_b_im'],
                        lp['s5_c_re'], lp['s5_c_im'])
    zs = jnp.zeros((bsz, S5_GROUPS * S5_STATE), F32)
    ys_c, fin_s = _s5_branch(uc, tables, ctx_len, (zs, zs, zs, zs), ctx_out)
    ys_x, _ = _s5_branch(ux, tables, seq_len, fin_s, True)

    wts = (_row(lp['s5_d']), lp['s5_glu_w'].astype(BF16), _row(lp['s5_glu_b']),
           lp['w_branch_m'].astype(BF16), lp['w_branch_na'].astype(BF16), lp['w_branch_s5'].astype(BF16),
           lp['w_out'].astype(BF16))
    g_post = _row(lp['norm_mix_post'])
    x = _merge(x, ym_x, yn_x, ys_x, ux, px, wts, mx[2], g_post, seq_len)
    if ctx_out:
        yn_c = _ctx_attention(pc, ctx_len)
        xc = _merge(xc, ym_c, yn_c, ys_c, uc, pc, wts, mc[2], g_post, xc.shape[0])
    return x, xc


def kernel(x, c, ctx, c_ctx, ada_w, ada_b, norm_mix_pre, norm_mix_post, norm_ffn_pre, norm_ffn_post, w_in, m_gate_b, m_conv_w, m_conv_b, m_norm, na_rpb, s5_lam_re, s5_lam_im, s5_log_dt, s5_b_re, s5_b_im, s5_c_re, s5_c_im, s5_d, s5_glu_w, s5_glu_b, w_branch_m, w_branch_na, w_branch_s5, w_out, ffn_w_gate, ffn_w_up, ffn_w_down, moe_router, moe_w_gate, moe_w_up, moe_w_down):
    bsz, seq_len, d = x.shape
    ctx_len = ctx.shape[1]
    depth = w_in.shape[0]
    assert d == D_MODEL and seq_len % (GRID_W * NA_ROWS) == 0 and ctx_len % M_CHUNK == 0
    rope = _rope_tables(seq_len)
    xf = x.reshape(bsz * seq_len, d)
    xc = ctx.reshape(bsz * ctx_len, d)
    cvecs = jnp.zeros((16, d), F32).at[:bsz].set(c).at[bsz].set(c_ctx)
    for l in range(depth):
        last = l == depth - 1
        mod = _adaln(cvecs, ada_w[l].astype(BF16), _row(ada_b[l]))
        mx = [mod[:bsz, k * d:(k + 1) * d].reshape(bsz, 1, d) for k in range(N_MOD)]
        mc = [mod[bsz:bsz + 1, k * d:(k + 1) * d].reshape(1, 1, d) for k in range(N_MOD)]
        lp = {
            'norm_mix_pre': norm_mix_pre[l], 'norm_mix_post': norm_mix_post[l],
            'w_in': w_in[l], 'm_gate_b': m_gate_b[l], 'm_conv_w': m_conv_w[l], 'm_conv_b': m_conv_b[l],
            'm_norm': m_norm[l], 'na_rpb': na_rpb[l], 's5_lam_re': s5_lam_re[l], 's5_lam_im': s5_lam_im[l],
            's5_log_dt': s5_log_dt[l], 's5_b_re': s5_b_re[l], 's5_b_im': s5_b_im[l], 's5_c_re': s5_c_re[l],
            's5_c_im': s5_c_im[l], 's5_d': s5_d[l], 's5_glu_w': s5_glu_w[l], 's5_glu_b': s5_glu_b[l],
            'w_branch_m': w_branch_m[l], 'w_branch_na': w_branch_na[l], 'w_branch_s5': w_branch_s5[l],
            'w_out': w_out[l],
        }
        xf, xc = _token_mixers(xf, xc, mx, mc, lp, rope, seq_len, ctx_len, not last)
        j = l // 2
        g_pre, g_post = _row(norm_ffn_pre[l]), _row(norm_ffn_post[l])
        if l % 2 == 0:
            wg, wu, wd = ffn_w_gate[j].astype(BF16), ffn_w_up[j].astype(BF16), ffn_w_down[j].astype(BF16)

            def channel(h, m, rows):
                return _ffn(h, g_pre, m[3], m[4], wg, wu, wd, m[5], g_post, rows)
        else:
            wg, wu, wd = moe_w_gate[j].astype(BF16), moe_w_up[j].astype(BF16), moe_w_down[j].astype(BF16)
            router = jnp.pad(moe_router[j].astype(F32), ((0, 0), (0, LANES - N_EXPERTS)))
            r_hi = router.astype(BF16)
            r_lo = (router - r_hi.astype(F32)).astype(BF16)

            def channel(h, m, rows):
                return _moe_sorted(h, g_pre, m[3], m[4], r_hi, r_lo, wg, wu, wd, m[5], g_post, rows)
        xf = channel(xf, mx, seq_len)
        if not last:
            xc = channel(xc, mc, xc.shape[0])
    return xf.reshape(bsz, seq_len, d)
```

```python
import functools
import math

import jax
import jax.numpy as jnp
from jax import lax
from jax.experimental import pallas as pl
from jax.experimental.pallas import tpu as pltpu

F32 = jnp.float32
BF16 = jnp.bfloat16

D_MODEL = 1024
EPS = 1e-6
N_MOD = 6
GRID_W = 64

M_HEADS = 4
M_HEAD_DIM = 128
M_WIDTH = M_HEADS * M_HEAD_DIM
M_GATES = 4 * M_HEADS
M_CHUNK = 128
ROPE_BASE = 10000.0

NA_HEADS = 8
NA_HEAD_DIM = 64
NA_WIDTH = NA_HEADS * NA_HEAD_DIM
WIN_R = 8
WIN_C = 16
NEG_BIG = -1e30

S5_GROUP = 16
S5_WIDTH = 512
S5_GROUPS = S5_WIDTH // S5_GROUP
S5_STATE = 64
S5_MAX_RE = -1e-4

LANES = 128
VMEM_LIMIT = 52 * 1024 * 1024

S5_CHUNK = 16
S5_QUAD_GROUPS = LANES // S5_GROUP
S5_QUADS = S5_GROUPS // S5_QUAD_GROUPS
S5_QCOLS = S5_CHUNK * LANES
S5_QSTATE = S5_QUAD_GROUPS * S5_STATE

N_EXPERTS = 8
TOP_K = 2
MOE_TILE = 1024
MOE_CAP = 256
MOE_GRAIN = 64

COL_QK, COL_V, COL_O, COL_NAQ, COL_NAK, COL_NAV = 0, 2, 3, 10, 11, 12
COL_GATE = 2
IN_MAIN = 6656


def _params(sem):
    return pltpu.CompilerParams(dimension_semantics=sem, vmem_limit_bytes=VMEM_LIMIT)


def _dot(a, b):
    return jnp.dot(a, b, preferred_element_type=F32)


def _dot_nt(a, b):
    return lax.dot_general(a, b, (((1,), (1,)), ((), ())), preferred_element_type=F32)


def _split3(x):
    hi = x.astype(BF16)
    r1 = x - hi.astype(F32)
    mid = r1.astype(BF16)
    lo = (r1 - mid.astype(F32)).astype(BF16)
    return hi, mid, lo


def _rms(x, g):
    return x * lax.rsqrt(jnp.mean(x * x, axis=-1, keepdims=True) + EPS) * g


def _inproj_kernel(x_ref, g_ref, sh_ref, sc_ref, w_ref, wg_ref, wu_ref, o_ref, og_ref, ou_ref, hn_ref):
    @pl.when(pl.program_id(1) == 0)
    def _():
        h = _rms(x_ref[...], g_ref[...]) * (1.0 + sc_ref[0]) + sh_ref[0]
        hb = h.astype(BF16)
        hn_ref[...] = hb
        og_ref[...] = _dot(hb, wg_ref[...])
        ou_ref[...] = _dot(hb, wu_ref[...])

    o_ref[...] = _dot(hn_ref[...], w_ref[...]).astype(o_ref.dtype)


def _inproj(x, g, shift, scale, w_main, w_gates, w_u, rows_per_mod):
    t = x.shape[0]
    tm = min(1024, rows_per_mod)
    tn = IN_MAIN // 4
    assert rows_per_mod % tm == 0 and t % tm == 0 and tn % LANES == 0
    per = rows_per_mod // tm
    return pl.pallas_call(
        _inproj_kernel,
        grid=(t // tm, IN_MAIN // tn),
        in_specs=[
            pl.BlockSpec((tm, D_MODEL), lambda i, j: (i, 0)),
            pl.BlockSpec((1, D_MODEL), lambda i, j: (0, 0)),
            pl.BlockSpec((1, 1, D_MODEL), lambda i, j: (i // per, 0, 0)),
            pl.BlockSpec((1, 1, D_MODEL), lambda i, j: (i // per, 0, 0)),
            pl.BlockSpec((D_MODEL, tn), lambda i, j: (0, j)),
            pl.BlockSpec((D_MODEL, LANES), lambda i, j: (0, 0)),
            pl.BlockSpec((D_MODEL, S5_WIDTH), lambda i, j: (0, 0)),
        ],
        out_specs=[
            pl.BlockSpec((tm, tn), lambda i, j: (i, j)),
            pl.BlockSpec((tm, LANES), lambda i, j: (i, 0)),
            pl.BlockSpec((tm, S5_WIDTH), lambda i, j: (i, 0)),
        ],
        out_shape=[jax.ShapeDtypeStruct((t, IN_MAIN), BF16), jax.ShapeDtypeStruct((t, LANES), F32),
                   jax.ShapeDtypeStruct((t, S5_WIDTH), F32)],
        scratch_shapes=[pltpu.VMEM((tm, D_MODEL), BF16)],
        compiler_params=_params(("parallel", "arbitrary")),
        name="inproj",
    )(x, g, shift, scale, w_main, w_gates, w_u)


def _adaln_kernel(c_ref, w_ref, b_ref, o_ref):
    c = c_ref[...]
    s = c * jax.nn.sigmoid(c)
    o_ref[...] = _dot(s.astype(BF16), w_ref[...]) + b_ref[...]


def _adaln(cvecs, w, b):
    n = w.shape[1]
    tn = 1024
    return pl.pallas_call(
        _adaln_kernel,
        grid=(n // tn,),
        in_specs=[
            pl.BlockSpec(cvecs.shape, lambda j: (0, 0)),
            pl.BlockSpec((D_MODEL, tn), lambda j: (0, j)),
            pl.BlockSpec((1, tn), lambda j: (0, j)),
        ],
        out_specs=pl.BlockSpec((cvecs.shape[0], tn), lambda j: (0, j)),
        out_shape=jax.ShapeDtypeStruct((cvecs.shape[0], n), F32),
        compiler_params=_params(("parallel",)),
        name="adaln",
    )(cvecs, w, b)


HALO = 16


def _swap32(x):
    lane = lax.broadcasted_iota(jnp.int32, x.shape, 1)
    fwd = pltpu.roll(x, 96, 1)
    bwd = pltpu.roll(x, 32, 1)
    return jnp.where((lane % 64) < 32, fwd, bwd)


def _qkconv_kernel(x_ref, pv_ref, nx_ref, w_ref, b_ref, cos_ref, sin_ref, o_ref, *, tiles_per_seq, rope):
    i = pl.program_id(0)
    tm = x_ref.shape[0]
    x = x_ref[...].astype(F32)
    first = (i % tiles_per_seq) == 0
    last = (i % tiles_per_seq) == tiles_per_seq - 1
    prev_row = jnp.where(first, 0.0, pv_ref[HALO - 1:HALO, :].astype(F32))
    next_row = jnp.where(last, 0.0, nx_ref[0:1, :].astype(F32))
    row = lax.broadcasted_iota(jnp.int32, x.shape, 0)
    xp = jnp.where(row == 0, prev_row, pltpu.roll(x, 1, 0))
    xn = jnp.where(row == tm - 1, next_row, pltpu.roll(x, tm - 1, 0))
    y = xp * w_ref[0:1, :] + x * w_ref[1:2, :] + xn * w_ref[2:3, :] + b_ref[...]
    y = y * jax.nn.sigmoid(y)
    kscale = M_HEAD_DIM ** -0.5
    for hh in range(2 * M_HEADS):
        ys = y[:, hh * LANES:(hh + 1) * LANES]
        if rope:
            ys = ys * cos_ref[...] + _swap32(ys) * sin_ref[...]
        if hh >= M_HEADS:
            ys = ys * kscale
        o_ref[:, hh * LANES:(hh + 1) * LANES] = ys.astype(o_ref.dtype)


def _qkconv(px, conv_w, conv_b, cos_t, sin_t, seq_len, rope):
    t = px.shape[0]
    tm = min(512, seq_len)
    assert seq_len % tm == 0
    tiles = seq_len // tm
    hb = tm // HALO
    nh = t // HALO
    return pl.pallas_call(
        functools.partial(_qkconv_kernel, tiles_per_seq=tiles, rope=rope),
        grid=(t // tm,),
        in_specs=[
            pl.BlockSpec((tm, 2 * M_WIDTH), lambda i: (i, 0)),
            pl.BlockSpec((HALO, 2 * M_WIDTH), lambda i: (jnp.maximum(i * hb - 1, 0), 0)),
            pl.BlockSpec((HALO, 2 * M_WIDTH), lambda i: (jnp.minimum((i + 1) * hb, nh - 1), 0)),
            pl.BlockSpec((3, 2 * M_WIDTH), lambda i: (0, 0)),
            pl.BlockSpec((1, 2 * M_WIDTH), lambda i: (0, 0)),
            pl.BlockSpec((tm, LANES), lambda i: (i % tiles, 0)),
            pl.BlockSpec((tm, LANES), lambda i: (i % tiles, 0)),
        ],
        out_specs=pl.BlockSpec((tm, 2 * M_WIDTH), lambda i: (i, 0)),
        out_shape=jax.ShapeDtypeStruct((t, 2 * M_WIDTH), BF16),
        compiler_params=_params(("parallel",)),
        name="qkconv",
    )(px, px, px, conv_w, conv_b, cos_t, sin_t)


def _rope_tables(n):
    pos = jnp.arange(n, dtype=jnp.int32)
    row = (pos // GRID_W).astype(F32)
    col = (pos % GRID_W).astype(F32)
    n_freq = M_HEAD_DIM // 4
    inv = ROPE_BASE ** (-jnp.arange(n_freq, dtype=F32) / n_freq)
    ar, ac = row[:, None] * inv, col[:, None] * inv
    cos_t = jnp.concatenate([jnp.cos(ar), jnp.cos(ar), jnp.cos(ac), jnp.cos(ac)], axis=-1)
    sin_t = jnp.concatenate([-jnp.sin(ar), jnp.sin(ar), -jnp.sin(ac), jnp.sin(ac)], axis=-1)
    return cos_t, sin_t


def _mlstm_kernel(*refs, reverse, final, nb):
    if final:
        (qk_ref, v_ref, g_ref, gb_ref, c0_ref, n0_ref, m0_ref, hp_ref, op_ref, nw_ref,
         out_ref, cf_ref, nf_ref, mf_ref, c_scr, n_scr, m_scr, qk_scr, qc_scr, qn_scr, s_scr, vt_scr, vu_scr, dec_scr) = refs
    else:
        (qk_ref, v_ref, g_ref, gb_ref, c0_ref, n0_ref, m0_ref,
         out_ref, cf_ref, nf_ref, mf_ref, c_scr, n_scr, m_scr, qk_scr, qc_scr, qn_scr, s_scr, vt_scr, vu_scr, dec_scr) = refs
    c_idx = pl.program_id(1)
    n_c = pl.num_programs(1)
    L = M_CHUNK

    @pl.when(c_idx == 0)
    def _():
        c_scr[...] = c0_ref[...]
        n_scr[...] = n0_ref[...]
        m_scr[...] = m0_ref[...]

    ti = lax.broadcasted_iota(jnp.int32, (L, L), 0)
    si = lax.broadcasted_iota(jnp.int32, (L, L), 1)
    keep = (si >= ti) if reverse else (si <= ti)
    tri = jnp.where(keep, 1.0, 0.0).astype(BF16)
    i_off = 2 * M_HEADS if reverse else 0
    f_off = i_off + M_HEADS
    end = 0 if reverse else L - 1

    units = [(a, b) for a in range(nb) for b in range(M_HEADS)]

    def head(ref, gi, h, base=0):
        return ref[gi, :, base + h * LANES:base + (h + 1) * LANES]

    keep_t = (ti >= si) if reverse else (ti <= si)

    for u, (gi, h) in enumerate(units):
        q = head(qk_ref, gi, h)
        qk_scr[u] = _dot_nt(head(qk_ref, gi, h, M_WIDTH), q)
        qc_scr[u] = _dot_nt(c_scr[gi, h].astype(BF16), q)
        qn_scr[u] = _dot_nt(jnp.broadcast_to(n_scr[gi, h], (8, LANES)).astype(BF16), q)

    for u, (gi, h) in enumerate(units):
        if h == 0:
            g = g_ref[gi] + gb_ref[...]
            f_hi, f_mid, f_lo = _split3(jax.nn.log_sigmoid(g))
            bsum = _dot(tri, f_hi) + _dot(tri, f_mid) + _dot(tri, f_lo)
            g_t = g.T
            b_t = bsum.T
        m_prev = m_scr[gi, h][:, 0:1]
        b_row = b_t[f_off + h:f_off + h + 1, :]
        i_row = g_t[i_off + h:i_off + h + 1, :]
        b_end = b_row[:, end:end + 1]
        d_col = g[:, i_off + h:i_off + h + 1] - bsum[:, f_off + h:f_off + h + 1]

        log_w = jnp.where(keep_t, b_row + d_col, -jnp.inf)
        carry_log = b_row + m_prev
        m_t = jnp.maximum(carry_log, jnp.max(log_w, axis=0, keepdims=True))
        s_t = qk_scr[u] * jnp.exp(log_w - m_t)
        c_scale = jnp.exp(carry_log - m_t)
        den = jnp.sum(s_t, axis=0, keepdims=True) + c_scale * qn_scr[u][0:1, :]
        inv = 1.0 / jnp.maximum(jnp.abs(den), jnp.exp(-m_t))
        s_scr[u] = (s_t * inv).astype(BF16)
        qc_scr[u] = qc_scr[u] * (c_scale * inv)

        m_new = jnp.maximum(b_end + m_prev, jnp.max(b_end - b_row + i_row, axis=-1, keepdims=True))
        u_row = jnp.exp(b_end - b_row + i_row - m_new)
        decay = jnp.exp(b_end + m_prev - m_new)
        v_t = head(v_ref, gi, h).astype(F32).T
        vt_scr[u] = v_t.astype(BF16)
        vu_scr[u] = (v_t * u_row).astype(BF16)
        u8 = jnp.broadcast_to(u_row, (8, L)).astype(BF16)
        n_scr[gi, h] = decay * n_scr[gi, h] + _dot(u8, head(qk_ref, gi, h, M_WIDTH))[0:1, :]
        m_scr[gi, h] = jnp.broadcast_to(m_new, (1, LANES))
        dec_scr[u] = jnp.broadcast_to(decay, (1, LANES))

    for u, (gi, h) in enumerate(units):
        qc_scr[u] = _dot(vt_scr[u], s_scr[u]) + qc_scr[u]
        c_scr[gi, h] = dec_scr[u][:, 0:1] * c_scr[gi, h] + _dot(vu_scr[u], head(qk_ref, gi, h, M_WIDTH))

    for u, (gi, h) in enumerate(units):
        rows = slice(h * M_HEAD_DIM, (h + 1) * M_HEAD_DIM)
        if final:
            hs = qc_scr[u] + hp_ref[gi, rows, :]
            mu = jnp.mean(hs, axis=0, keepdims=True)
            var = jnp.mean(jnp.square(hs - mu), axis=0, keepdims=True)
            hn = ((hs - mu) * lax.rsqrt(var + EPS)).T
            y = hn * nw_ref[:, h * LANES:(h + 1) * LANES] * jax.nn.sigmoid(head(op_ref, gi, h).astype(F32))
            out_ref[gi, :, h * LANES:(h + 1) * LANES] = y.astype(out_ref.dtype)
        else:
            out_ref[gi, rows, :] = qc_scr[u]

    @pl.when(c_idx == n_c - 1)
    def _():
        cf_ref[...] = c_scr[...]
        nf_ref[...] = n_scr[...]
        mf_ref[...] = m_scr[...]


def _mlstm_scan(qk, px, gates, gate_b, state, seq_len, reverse, hprev=None, norm_w=None):
    t = qk.shape[0]
    bsz = t // seq_len
    n_c = seq_len // M_CHUNK
    final = hprev is not None
    nb = 4 if bsz % 4 == 0 else (2 if bsz % 2 == 0 else 1)
    qk, px, gates = (a.reshape(bsz, seq_len, a.shape[1]) for a in (qk, px, gates))

    def chunk(c):
        return (n_c - 1 - c) if reverse else c

    state_specs = [
        pl.BlockSpec((nb, M_HEADS, M_HEAD_DIM, M_HEAD_DIM), lambda b, c: (b, 0, 0, 0)),
        pl.BlockSpec((nb, M_HEADS, 1, LANES), lambda b, c: (b, 0, 0, 0)),
        pl.BlockSpec((nb, M_HEADS, 1, LANES), lambda b, c: (b, 0, 0, 0)),
    ]
    in_specs = [
        pl.BlockSpec((nb, M_CHUNK, 2 * M_WIDTH), lambda b, c: (b, chunk(c), 0)),
        pl.BlockSpec((nb, M_CHUNK, M_WIDTH), lambda b, c: (b, chunk(c), COL_V)),
        pl.BlockSpec((nb, M_CHUNK, LANES), lambda b, c: (b, chunk(c), 0)),
        pl.BlockSpec((1, LANES), lambda b, c: (0, 0)),
    ] + state_specs
    args = [qk, px, gates, gate_b, *state]
    t_spec = pl.BlockSpec((nb, M_WIDTH, M_CHUNK), lambda b, c: (b, 0, chunk(c)))
    if final:
        in_specs += [
            t_spec,
            pl.BlockSpec((nb, M_CHUNK, M_WIDTH), lambda b, c: (b, chunk(c), COL_O)),
            pl.BlockSpec((1, M_WIDTH), lambda b, c: (0, 0)),
        ]
        args += [hprev, px, norm_w]
        out_spec = pl.BlockSpec((nb, M_CHUNK, M_WIDTH), lambda b, c: (b, chunk(c), 0))
        out_shape = jax.ShapeDtypeStruct((bsz, seq_len, M_WIDTH), BF16)
    else:
        out_spec = t_spec
        out_shape = jax.ShapeDtypeStruct((bsz, M_WIDTH, seq_len), F32)
    units = nb * M_HEADS
    out, *fin = pl.pallas_call(
        functools.partial(_mlstm_kernel, reverse=reverse, final=final, nb=nb),
        grid=(bsz // nb, n_c),
        in_specs=in_specs,
        out_specs=[out_spec] + state_specs,
        out_shape=[
            out_shape,
            jax.ShapeDtypeStruct((bsz, M_HEADS, M_HEAD_DIM, M_HEAD_DIM), F32),
            jax.ShapeDtypeStruct((bsz, M_HEADS, 1, LANES), F32),
            jax.ShapeDtypeStruct((bsz, M_HEADS, 1, LANES), F32),
        ],
        scratch_shapes=[
            pltpu.VMEM((nb, M_HEADS, M_HEAD_DIM, M_HEAD_DIM), F32),
            pltpu.VMEM((nb, M_HEADS, 1, LANES), F32),
            pltpu.VMEM((nb, M_HEADS, 1, LANES), F32),
            pltpu.VMEM((units, M_CHUNK, M_CHUNK), F32),
            pltpu.VMEM((units, M_HEAD_DIM, M_CHUNK), F32),
            pltpu.VMEM((units, 8, M_CHUNK), F32),
            pltpu.VMEM((units, M_CHUNK, M_CHUNK), BF16),
            pltpu.VMEM((units, M_HEAD_DIM, M_CHUNK), BF16),
            pltpu.VMEM((units, M_HEAD_DIM, M_CHUNK), BF16),
            pltpu.VMEM((units, 1, LANES), F32),
        ],
        compiler_params=_params(("parallel", "arbitrary")),
        name="mlstm_rev" if reverse else "mlstm_fwd",
    )(*args)
    return [out.reshape(t, M_WIDTH) if final else out] + fin


def _mlstm_zero_state(bsz):
    return (jnp.zeros((bsz, M_HEADS, M_HEAD_DIM, M_HEAD_DIM), F32),
            jnp.zeros((bsz, M_HEADS, 1, LANES), F32),
            jnp.zeros((bsz, M_HEADS, 1, LANES), F32))


def _mlstm_bidir(qk, px, gates, gate_b, norm_w, state_f, state_b, seq_len):
    h_f, *fin_f = _mlstm_scan(qk, px, gates, gate_b, state_f, seq_len, False)
    y, *fin_b = _mlstm_scan(qk, px, gates, gate_b, state_b, seq_len, True, hprev=h_f, norm_w=norm_w)
    return y, tuple(fin_f), tuple(fin_b)


NA_ROWS = 8
NA_STEP = 4


def _na_kernel(q_ref, kp_ref, kc_ref, kn_ref, vp_ref, vc_ref, vn_ref, kx_ref, vx_ref, *rest, rows):
    bias_refs = rest[:NA_STEP]
    o_ref, kwin, vwin, s_scr, p_scr = rest[NA_STEP:]
    step = pl.program_id(1)
    blk = NA_ROWS * GRID_W

    @pl.when(step % (NA_ROWS // NA_STEP) == 0)
    def _():
        kwin[0:blk, :] = kp_ref[...]
        kwin[blk:2 * blk, :] = kc_ref[...]
        kwin[2 * blk:3 * blk, :] = kn_ref[...]
        vwin[0:blk, :] = vp_ref[...]
        vwin[blk:2 * blk, :] = vc_ref[...]
        vwin[2 * blk:3 * blk, :] = vn_ref[...]

    g = step // (NA_ROWS // NA_STEP)
    n_loc = WIN_R * GRID_W
    lane = lax.broadcasted_iota(jnp.int32, (GRID_W, LANES), 1)
    scale = NA_HEAD_DIM ** -0.5
    n_ctx = kx_ref.shape[0]
    units = [(kk, hp) for kk in range(NA_STEP) for hp in range(NA_HEADS // 2)]
    offs = []
    for kk in range(NA_STEP):
        r = step * NA_STEP + kk
        r0 = jnp.clip(r - WIN_R // 2, 0, rows - WIN_R)
        offs.append(pl.multiple_of((r0 - NA_ROWS * g + NA_ROWS) * GRID_W, GRID_W))

    for u, (kk, hp) in enumerate(units):
        cols = slice(hp * LANES, (hp + 1) * LANES)
        q2 = q_ref[kk * GRID_W:(kk + 1) * GRID_W, cols] * scale
        zero = jnp.zeros_like(q2)
        qs = jnp.concatenate([jnp.where(lane < NA_HEAD_DIM, q2, zero), jnp.where(lane < NA_HEAD_DIM, zero, q2)],
                             axis=0)
        s_scr[u, :, 0:n_loc] = _dot_nt(qs, kwin[pl.ds(offs[kk], n_loc), cols])
        s_scr[u, :, n_loc:n_loc + n_ctx] = _dot_nt(qs, kx_ref[:, cols])

    for u, (kk, hp) in enumerate(units):
        s_loc = s_scr[u, :, 0:n_loc] + bias_refs[kk][0, hp]
        s_ctx = s_scr[u, :, n_loc:n_loc + n_ctx]
        m = jnp.maximum(jnp.max(s_loc, axis=-1, keepdims=True), jnp.max(s_ctx, axis=-1, keepdims=True))
        p_loc = jnp.exp(s_loc - m)
        p_ctx = jnp.exp(s_ctx - m)
        inv = 1.0 / (jnp.sum(p_loc, axis=-1, keepdims=True) + jnp.sum(p_ctx, axis=-1, keepdims=True))
        p_scr[u, :, 0:n_loc] = (p_loc * inv).astype(BF16)
        p_scr[u, :, n_loc:n_loc + n_ctx] = (p_ctx * inv).astype(BF16)

    for u, (kk, hp) in enumerate(units):
        cols = slice(hp * LANES, (hp + 1) * LANES)
        o2 = _dot(p_scr[u, :, 0:n_loc], vwin[pl.ds(offs[kk], n_loc), cols])
        o2 += _dot(p_scr[u, :, n_loc:n_loc + n_ctx], vx_ref[:, cols])
        out = jnp.where(lane < NA_HEAD_DIM, o2[0:GRID_W], o2[GRID_W:2 * GRID_W])
        o_ref[kk * GRID_W:(kk + 1) * GRID_W, cols] = out.astype(o_ref.dtype)


def _na_bias_table(rpb):
    qc = jnp.arange(GRID_W)[:, None]
    kc = jnp.arange(GRID_W)[None, :]
    cs = jnp.clip(qc - WIN_C // 2, 0, GRID_W - WIN_C)
    ok = (kc >= cs) & (kc < cs + WIN_C)
    dc = jnp.clip(kc - qc + WIN_C - 1, 0, 2 * WIN_C - 2)
    dr = jnp.arange(WIN_R)[:, None] + jnp.arange(WIN_R)[None, :]
    tab = rpb.astype(F32)[:, dr][:, :, :, dc]
    tab = jnp.where(ok[None, None, None], tab, NEG_BIG)
    tab = jnp.transpose(tab, (1, 0, 3, 2, 4))
    return tab.reshape(WIN_R, NA_HEADS // 2, 2 * GRID_W, WIN_R * GRID_W)


def _na_attention(px, pc, bias, seq_len, ctx_len):
    t = px.shape[0]
    bsz = t // seq_len
    rows = seq_len // GRID_W
    n_g = rows // NA_ROWS
    blk = NA_ROWS * GRID_W

    steps = rows // NA_STEP
    per_blk = NA_ROWS // NA_STEP

    def kv_spec(col, shift):
        def imap(b, s):
            g = jnp.clip(s // per_blk + shift, 0, n_g - 1)
            return (b * n_g + g, col)
        return pl.BlockSpec((blk, NA_WIDTH), imap)

    def bias_spec(kk):
        def imap(b, s):
            r = s * NA_STEP + kk
            r0 = jnp.clip(r - WIN_R // 2, 0, rows - WIN_R)
            return (r0 - r + WIN_R - 1, 0, 0, 0)
        return pl.BlockSpec((1, NA_HEADS // 2, 2 * GRID_W, WIN_R * GRID_W), imap)

    return pl.pallas_call(
        functools.partial(_na_kernel, rows=rows),
        grid=(bsz, steps),
        in_specs=[
            pl.BlockSpec((NA_STEP * GRID_W, NA_WIDTH), lambda b, s: (b * steps + s, COL_NAQ)),
            kv_spec(COL_NAK, -1), kv_spec(COL_NAK, 0), kv_spec(COL_NAK, 1),
            kv_spec(COL_NAV, -1), kv_spec(COL_NAV, 0), kv_spec(COL_NAV, 1),
            pl.BlockSpec((ctx_len, NA_WIDTH), lambda b, s: (b, COL_NAK)),
            pl.BlockSpec((ctx_len, NA_WIDTH), lambda b, s: (b, COL_NAV)),
        ] + [bias_spec(kk) for kk in range(NA_STEP)],
        out_specs=pl.BlockSpec((NA_STEP * GRID_W, NA_WIDTH), lambda b, s: (b * steps + s, 0)),
        out_shape=jax.ShapeDtypeStruct((t, NA_WIDTH), BF16),
        scratch_shapes=[
            pltpu.VMEM((3 * blk, NA_WIDTH), BF16), pltpu.VMEM((3 * blk, NA_WIDTH), BF16),
            pltpu.VMEM((NA_STEP * NA_HEADS // 2, 2 * GRID_W, WIN_R * GRID_W + ctx_len), F32),
            pltpu.VMEM((NA_STEP * NA_HEADS // 2, 2 * GRID_W, WIN_R * GRID_W + ctx_len), BF16),
        ],
        compiler_params=_params(("parallel", "arbitrary")),
        name="na_attn",
    )(px, px, px, px, px, px, px, pc, pc, *([bias] * NA_STEP))


def _ctx_attn_kernel(q_ref, k_ref, v_ref, o_ref):
    n = q_ref.shape[0]
    lane = lax.broadcasted_iota(jnp.int32, (n, LANES), 1)
    scale = NA_HEAD_DIM ** -0.5
    for hp in range(NA_HEADS // 2):
        cols = slice(hp * LANES, (hp + 1) * LANES)
        q2 = q_ref[:, cols] * scale
        k2 = k_ref[:, cols]
        v2 = v_ref[:, cols]
        outs = []
        for sub in range(2):
            sel = (lane < NA_HEAD_DIM) if sub == 0 else (lane >= NA_HEAD_DIM)
            s = _dot_nt(jnp.where(sel, q2, jnp.zeros_like(q2)), k2)
            p = jnp.exp(s - jnp.max(s, axis=-1, keepdims=True))
            outs.append(_dot(p.astype(BF16), v2) / jnp.sum(p, axis=-1, keepdims=True))
        o_ref[:, cols] = jnp.where(lane < NA_HEAD_DIM, outs[0], outs[1]).astype(o_ref.dtype)


def _ctx_attention(pc, ctx_len):
    t = pc.shape[0]
    return pl.pallas_call(
        _ctx_attn_kernel,
        grid=(t // ctx_len,),
        in_specs=[pl.BlockSpec((ctx_len, NA_WIDTH), lambda b, col=col: (b, col))
                  for col in (COL_NAQ, COL_NAK, COL_NAV)],
        out_specs=pl.BlockSpec((ctx_len, NA_WIDTH), lambda b: (b, 0)),
        out_shape=jax.ShapeDtypeStruct((t, NA_WIDTH), BF16),
        compiler_params=_params(("parallel",)),
        name="ctx_attn",
    )(pc, pc, pc)


def _cmul(a, b):
    return a[0] * b[0] - a[1] * b[1], a[0] * b[1] + a[1] * b[0]


def _s5_tables(lam_re, lam_im, log_dt, b_re, b_im, c_re, c_im):
    L, G, S, H = S5_CHUNK, S5_GROUPS, S5_STATE, S5_GROUP
    lr = jnp.minimum(lam_re.astype(F32), S5_MAX_RE)
    li = lam_im.astype(F32)
    dt = jnp.exp(log_dt.astype(F32))[..., None]
    zr, zi = lr * dt, li * dt
    d = jnp.arange(L + 1, dtype=F32)[None, :, None, None]
    mag = jnp.exp(zr[:, None] * d)
    pw = (mag * jnp.cos(zi[:, None] * d), mag * jnp.sin(zi[:, None] * d))
    lb = (pw[0][:, 1], pw[1][:, 1])
    den = lr * lr + li * li
    ratio = (((lb[0] - 1.0) * lr + lb[1] * li) / den, (lb[1] * lr - (lb[0] - 1.0) * li) / den)
    bb = _cmul((ratio[0][..., None], ratio[1][..., None]), (b_re.astype(F32), b_im.astype(F32)))
    cc = (c_re.astype(F32), c_im.astype(F32))

    def kern(x):
        cl = _cmul((cc[0][x][None], cc[1][x][None]), (pw[0][x, :L, :, None, :], pw[1][x, :L, :, None, :]))
        return jnp.einsum('dghp,gpj->dghj', cl[0], bb[0][x]) - jnp.einsum('dghp,gpj->dghj', cl[1], bb[1][x])

    def state_in(x, steps):
        p = (pw[0][x][steps][:, :, None, :], pw[1][x][steps][:, :, None, :])
        r = _cmul(p, (jnp.swapaxes(bb[0][x], 1, 2)[None], jnp.swapaxes(bb[1][x], 1, 2)[None]))
        return jnp.transpose(r[0], (1, 0, 2, 3)), jnp.transpose(r[1], (1, 0, 2, 3))

    def state_out(x, steps):
        p = (pw[0][x][steps][:, :, None, :], pw[1][x][steps][:, :, None, :])
        r = _cmul(p, (cc[0][x][None], cc[1][x][None]))
        return jnp.transpose(r[0], (1, 3, 0, 2)), jnp.transpose(r[1], (1, 3, 0, 2))

    steps = jnp.arange(L)
    wf = state_in(0, L - 1 - steps)
    wb = state_in(1, steps)
    vf = state_out(0, steps + 1)
    vb = state_out(1, L - steps)
    Q, A = S5_QUADS, S5_QUAD_GROUPS
    kd = jnp.stack([kern(0), kern(1)]).reshape(2, L, Q, A, H, H)
    kd = jnp.transpose(kd, (0, 2, 1, 5, 3, 4)).reshape(2, Q, L, H, A * H)
    wk = jnp.stack([wf[0], wf[1], wb[0], wb[1]]).reshape(4, Q, A // 2, 2, L, H, S)
    wd = jnp.transpose(wk, (0, 1, 4, 2, 5, 3, 6)).reshape(4, Q, L, (A // 2) * H, 2 * S)
    vk = jnp.stack([vf[0], -vf[1], vb[0], -vb[1]]).reshape(4, Q, A, S, L, H)
    vd = jnp.transpose(vk, (0, 1, 4, 3, 2, 5)).reshape(4, Q, L, S, A * H)
    a_re = pw[0][:, L].reshape(2, 1, G * S)
    a_im = pw[1][:, L].reshape(2, 1, G * S)
    return wd, kd, vd, a_re, a_im


def _s5_operand(u_ref):
    return jnp.concatenate([u_ref[:, s, :] for s in range(S5_CHUNK)], axis=1).astype(BF16)


def _s5_expand_w(wd_ref, w_scr):
    H, half = S5_GROUP, S5_STATE
    w_scr[...] = jnp.zeros_like(w_scr)
    lane = lax.broadcasted_iota(jnp.int32, (H, LANES), 1)
    for k in range(4):
        for s in range(S5_CHUNK):
            for c in range(S5_QUAD_GROUPS // 2):
                blk = wd_ref[k, 0, s, c * H:(c + 1) * H, :]
                cols = slice(k * S5_QSTATE + c * LANES, k * S5_QSTATE + (c + 1) * LANES)
                row = s * LANES + 2 * c * H
                w_scr[row:row + H, cols] = jnp.where(lane < half, blk, 0.0).astype(BF16)
                w_scr[row + H:row + 2 * H, cols] = jnp.where(lane >= half, blk, 0.0).astype(BF16)


def _s5_expand_mv(kd_ref, vd_ref, m_scr, v_scr):
    H, L = S5_GROUP, S5_CHUNK
    rg = lax.broadcasted_iota(jnp.int32, (LANES, LANES), 0) // H
    lg = lax.broadcasted_iota(jnp.int32, (LANES, LANES), 1) // H
    tiles = []
    for x in range(2):
        tiles.append([jnp.where(rg == lg, jnp.concatenate([kd_ref[x, 0, d]] * S5_QUAD_GROUPS, axis=0), 0.0)
                      for d in range(L)])
    for s in range(L):
        for t in range(L):
            blk = tiles[0][t - s] if t > s else (tiles[1][s - t] if s > t else tiles[0][0] + tiles[1][0])
            m_scr[s * LANES:(s + 1) * LANES, t * LANES:(t + 1) * LANES] = blk.astype(BF16)
    lg2 = lax.broadcasted_iota(jnp.int32, (S5_STATE, LANES), 1) // H
    for k in range(4):
        for t in range(L):
            blk = vd_ref[k, 0, t]
            for b in range(S5_QUAD_GROUPS):
                row = k * S5_QSTATE + b * S5_STATE
                v_scr[row:row + S5_STATE, t * LANES:(t + 1) * LANES] = jnp.where(lg2 == b, blk, 0.0).astype(BF16)


def _s5_local_kernel(u_ref, wd_ref, sfr_ref, sfi_ref, sbr_ref, sbi_ref, w_scr):
    @pl.when(pl.program_id(1) == 0)
    def _():
        _s5_expand_w(wd_ref, w_scr)

    s = _dot(_s5_operand(u_ref), w_scr[...])
    n = S5_QSTATE
    sfr_ref[...] = s[:, 0:n]
    sfi_ref[...] = s[:, n:2 * n]
    sbr_ref[...] = s[:, 2 * n:3 * n]
    sbi_ref[...] = s[:, 3 * n:4 * n]


def _s5_local_states(u3, wd):
    rows = u3.shape[0]
    tr = min(256, rows)
    assert rows % tr == 0
    out = jax.ShapeDtypeStruct((rows, S5_GROUPS * S5_STATE), F32)
    return pl.pallas_call(
        _s5_local_kernel,
        grid=(S5_QUADS, rows // tr),
        in_specs=[
            pl.BlockSpec((tr, S5_CHUNK, LANES), lambda q, i: (i, 0, q)),
            pl.BlockSpec((4, 1) + wd.shape[2:], lambda q, i: (0, q, 0, 0, 0)),
        ],
        out_specs=[pl.BlockSpec((tr, S5_QSTATE), lambda q, i: (i, q))] * 4,
        out_shape=[out] * 4,
        scratch_shapes=[pltpu.VMEM((S5_QCOLS, 4 * S5_QSTATE), BF16)],
        compiler_params=_params(("arbitrary", "arbitrary")),
        name="s5_local",
    )(u3, wd)


def _s5_scan_kernel(sr_ref, si_ref, ar_ref, ai_ref, h0r_ref, h0i_ref, hr_ref, hi_ref, fr_ref, fi_ref,
                    hr_scr, hi_scr, *, reverse):
    i = pl.program_id(1)
    n_i = pl.num_programs(1)
    steps = sr_ref.shape[1]

    @pl.when(i == 0)
    def _():
        hr_scr[...] = h0r_ref[...]
        hi_scr[...] = h0i_ref[...]

    ar = ar_ref[...]
    ai = ai_ref[...]

    def body(j, carry):
        hr, hi = carry
        c = (steps - 1 - j) if reverse else j
        hr_ref[:, c, :] = hr
        hi_ref[:, c, :] = hi
        nr = ar * hr - ai * hi + sr_ref[:, c, :]
        ni = ar * hi + ai * hr + si_ref[:, c, :]
        return nr, ni

    hr, hi = lax.fori_loop(0, steps, body, (hr_scr[...], hi_scr[...]))
    hr_scr[...] = hr
    hi_scr[...] = hi

    @pl.when(i == n_i - 1)
    def _():
        fr_ref[...] = hr
        fi_ref[...] = hi


def _s5_chunk_scan(s_re, s_im, a_re, a_im, h0_re, h0_im, bsz, reverse):
    _, n_c, width = s_re.shape
    tr = min(64, n_c)
    tc = 512
    n_i = n_c // tr

    def rmap(j, i):
        return (0, (n_i - 1 - i) if reverse else i, j)

    big = jax.ShapeDtypeStruct((bsz, n_c, width), F32)
    small = jax.ShapeDtypeStruct((bsz, width), F32)
    return pl.pallas_call(
        functools.partial(_s5_scan_kernel, reverse=reverse),
        grid=(width // tc, n_i),
        in_specs=[
            pl.BlockSpec((bsz, tr, tc), rmap), pl.BlockSpec((bsz, tr, tc), rmap),
            pl.BlockSpec((1, tc), lambda j, i: (0, j)), pl.BlockSpec((1, tc), lambda j, i: (0, j)),
            pl.BlockSpec((bsz, tc), lambda j, i: (0, j)), pl.BlockSpec((bsz, tc), lambda j, i: (0, j)),
        ],
        out_specs=[
            pl.BlockSpec((bsz, tr, tc), rmap), pl.BlockSpec((bsz, tr, tc), rmap),
            pl.BlockSpec((bsz, tc), lambda j, i: (0, j)), pl.BlockSpec((bsz, tc), lambda j, i: (0, j)),
        ],
        out_shape=[big, big, small, small],
        scratch_shapes=[pltpu.VMEM((bsz, tc), F32), pltpu.VMEM((bsz, tc), F32)],
        compiler_params=_params(("parallel", "arbitrary")),
        name="s5_scan_rev" if reverse else "s5_scan_fwd",
    )(s_re, s_im, a_re, a_im, h0_re, h0_im)


def _s5_out_kernel(u_ref, kd_ref, vd_ref, hfr_ref, hfi_ref, hbr_ref, hbi_ref, y_ref, m_scr, v_scr):
    @pl.when(pl.program_id(1) == 0)
    def _():
        _s5_expand_mv(kd_ref, vd_ref, m_scr, v_scr)

    n = S5_QSTATE
    y = _dot(_s5_operand(u_ref), m_scr[...])
    for k, h_ref in enumerate((hfr_ref, hfi_ref, hbr_ref, hbi_ref)):
        y += _dot(h_ref[...].astype(BF16), v_scr[k * n:(k + 1) * n, :])
    for t in range(S5_CHUNK):
        y_ref[:, t, :] = y[:, t * LANES:(t + 1) * LANES]


def _s5_outputs(u3, kd, vd, h_states):
    rows = u3.shape[0]
    tr = min(256, rows)
    return pl.pallas_call(
        _s5_out_kernel,
        grid=(S5_QUADS, rows // tr),
        in_specs=[
            pl.BlockSpec((tr, S5_CHUNK, LANES), lambda q, i: (i, 0, q)),
            pl.BlockSpec((2, 1) + kd.shape[2:], lambda q, i: (0, q, 0, 0, 0)),
            pl.BlockSpec((4, 1) + vd.shape[2:], lambda q, i: (0, q, 0, 0, 0)),
        ] + [pl.BlockSpec((tr, S5_QSTATE), lambda q, i: (i, q))] * 4,
        out_specs=pl.BlockSpec((tr, S5_CHUNK, LANES), lambda q, i: (i, 0, q)),
        out_shape=jax.ShapeDtypeStruct((rows, S5_CHUNK, S5_WIDTH), F32),
        scratch_shapes=[pltpu.VMEM((S5_QCOLS, S5_QCOLS), BF16), pltpu.VMEM((4 * S5_QSTATE, S5_QCOLS), BF16)],
        compiler_params=_params(("arbitrary", "arbitrary")),
        name="s5_out",
    )(u3, kd, vd, *h_states)


def _s5_branch(u, tables, seq_len, init, want_y):
    w_q, m_q, v_q, a_re, a_im = tables
    t = u.shape[0]
    bsz = t // seq_len
    n_c = seq_len // S5_CHUNK
    width = S5_GROUPS * S5_STATE
    u3 = u.reshape(t // S5_CHUNK, S5_CHUNK, S5_WIDTH)
    s_loc = [s.reshape(bsz, n_c, width) for s in _s5_local_states(u3, w_q)]
    hfr, hfi, ffr, ffi = _s5_chunk_scan(s_loc[0], s_loc[1], a_re[0], a_im[0], init[0], init[1], bsz, False)
    hbr, hbi, fbr, fbi = _s5_chunk_scan(s_loc[2], s_loc[3], a_re[1], a_im[1], init[2], init[3], bsz, True)
    y = None
    if want_y:
        h_states = [h.reshape(bsz * n_c, width) for h in (hfr, hfi, hbr, hbi)]
        y = _s5_outputs(u3, m_q, v_q, h_states).reshape(t, S5_WIDTH)
    return y, (ffr, ffi, fbr, fbi)


def _gelu_tanh(x):
    return 0.5 * x * (1.0 + jnp.tanh(math.sqrt(2.0 / math.pi) * (x + 0.044715 * (x * x * x))))


def _merge_kernel(x_ref, ym_ref, yn_ref, ys_ref, u_ref, gm_ref, gn_ref, gs_ref, d_ref, gw_ref, gb_ref,
                  wm_ref, wn_ref, ws_ref, wo_ref, gate_ref, npost_ref, o_ref):
    ys = ys_ref[...] + d_ref[...] * u_ref[...]
    g = _gelu_tanh(ys)
    gg = g * jax.nn.sigmoid(_dot(g.astype(BF16), gw_ref[...]) + gb_ref[...])
    y = jax.nn.sigmoid(gm_ref[...].astype(F32)) * _dot(ym_ref[...], wm_ref[...])
    y += jax.nn.sigmoid(gn_ref[...].astype(F32)) * _dot(yn_ref[...], wn_ref[...])
    y += jax.nn.sigmoid(gs_ref[...].astype(F32)) * _dot(gg.astype(BF16), ws_ref[...])
    out = _dot(y.astype(BF16), wo_ref[...])
    o_ref[...] = x_ref[...] + gate_ref[0] * _rms(out, npost_ref[...])


def _merge(x, ym, yn, ys, u, px, wts, gate, npost, rows_per_mod):
    t = x.shape[0]
    tm = min(512, rows_per_mod)
    per = rows_per_mod // tm
    s5_d, glu_w, glu_b, w_m, w_n, w_s, w_o = wts
    row = lambda i: (i, 0)
    const = lambda i: (0, 0)
    return pl.pallas_call(
        _merge_kernel,
        grid=(t // tm,),
        in_specs=[
            pl.BlockSpec((tm, D_MODEL), row),
            pl.BlockSpec((tm, M_WIDTH), row),
            pl.BlockSpec((tm, NA_WIDTH), row),
            pl.BlockSpec((tm, S5_WIDTH), row),
            pl.BlockSpec((tm, S5_WIDTH), row),
            pl.BlockSpec((tm, D_MODEL), lambda i: (i, COL_GATE)),
            pl.BlockSpec((tm, D_MODEL), lambda i: (i, COL_GATE + 1)),
            pl.BlockSpec((tm, D_MODEL), lambda i: (i, COL_GATE + 2)),
            pl.BlockSpec((1, S5_WIDTH), const),
            pl.BlockSpec((S5_WIDTH, S5_WIDTH), const),
            pl.BlockSpec((1, S5_WIDTH), const),
            pl.BlockSpec((M_WIDTH, D_MODEL), const),
            pl.BlockSpec((NA_WIDTH, D_MODEL), const),
            pl.BlockSpec((S5_WIDTH, D_MODEL), const),
            pl.BlockSpec((D_MODEL, D_MODEL), const),
            pl.BlockSpec((1, 1, D_MODEL), lambda i: (i // per, 0, 0)),
            pl.BlockSpec((1, D_MODEL), const),
        ],
        out_specs=pl.BlockSpec((tm, D_MODEL), row),
        out_shape=jax.ShapeDtypeStruct((t, D_MODEL), F32),
        compiler_params=_params(("parallel",)),
        name="merge",
    )(x, ym, yn, ys, u, px, px, px, s5_d, glu_w, glu_b, w_m, w_n, w_s, w_o, gate, npost)


def _swiglu(h, wg, wu, wd):
    a = _dot(h, wg)
    a = a * jax.nn.sigmoid(a) * _dot(h, wu)
    return _dot(a.astype(BF16), wd)


def _ffn_kernel(x_ref, g_ref, sh_ref, sc_ref, wg_ref, wu_ref, wd_ref, gate_ref, npost_ref,
                o_ref, hn_ref, acc_ref):
    j = pl.program_id(1)

    @pl.when(j == 0)
    def _():
        h = _rms(x_ref[...], g_ref[...]) * (1.0 + sc_ref[0]) + sh_ref[0]
        hn_ref[...] = h.astype(BF16)
        acc_ref[...] = jnp.zeros_like(acc_ref)

    acc_ref[...] += _swiglu(hn_ref[...], wg_ref[...], wu_ref[...], wd_ref[...])

    @pl.when(j == pl.num_programs(1) - 1)
    def _():
        o_ref[...] = x_ref[...] + gate_ref[0] * _rms(acc_ref[...], npost_ref[...])


def _ffn(x, g, shift, scale, w_gate, w_up, w_down, gate, npost, rows_per_mod):
    t = x.shape[0]
    ff = w_gate.shape[1]
    ff_tile = ff // 2
    tm = min(512, rows_per_mod)
    per = rows_per_mod // tm
    assert ff_tile % LANES == 0
    return pl.pallas_call(
        _ffn_kernel,
        grid=(t // tm, ff // ff_tile),
        in_specs=[
            pl.BlockSpec((tm, D_MODEL), lambda i, j: (i, 0)),
            pl.BlockSpec((1, D_MODEL), lambda i, j: (0, 0)),
            pl.BlockSpec((1, 1, D_MODEL), lambda i, j: (i // per, 0, 0)),
            pl.BlockSpec((1, 1, D_MODEL), lambda i, j: (i // per, 0, 0)),
            pl.BlockSpec((D_MODEL, ff_tile), lambda i, j: (0, j)),
            pl.BlockSpec((D_MODEL, ff_tile), lambda i, j: (0, j)),
            pl.BlockSpec((ff_tile, D_MODEL), lambda i, j: (j, 0)),
            pl.BlockSpec((1, 1, D_MODEL), lambda i, j: (i // per, 0, 0)),
            pl.BlockSpec((1, D_MODEL), lambda i, j: (0, 0)),
        ],
        out_specs=pl.BlockSpec((tm, D_MODEL), lambda i, j: (i, 0)),
        out_shape=jax.ShapeDtypeStruct((t, D_MODEL), F32),
        scratch_shapes=[pltpu.VMEM((tm, D_MODEL), BF16), pltpu.VMEM((tm, D_MODEL), F32)],
        compiler_params=_params(("parallel", "arbitrary")),
        name="ffn",
    )(x, g, shift, scale, w_gate, w_up, w_down, gate, npost)


def _moe_kernel(x_ref, g_ref, sh_ref, sc_ref, rh_ref, rl_ref, wg_ref, wu_ref, wd_ref, gate_ref, npost_ref,
                o_ref, hn_ref, acc_ref, comb_ref, rank_c_ref, rank_r_ref, mask_r_ref, xc_ref, yc_ref, cnt_ref):
    e = pl.program_id(1)
    j = pl.program_id(2)
    tm = x_ref.shape[0]
    cap = MOE_CAP

    @pl.when((e == 0) & (j == 0))
    def _():
        h = _rms(x_ref[...], g_ref[...]) * (1.0 + sc_ref[0]) + sh_ref[0]
        hi = h.astype(BF16)
        hn_ref[...] = hi
        acc_ref[...] = jnp.zeros_like(acc_ref)
        lo = (h - hi.astype(F32)).astype(BF16)
        logits = _dot(hi, rh_ref[...]) + _dot(lo, rh_ref[...]) + _dot(hi, rl_ref[...])
        lane = lax.broadcasted_iota(jnp.int32, logits.shape, 1)
        logits = jnp.where(lane < N_EXPERTS, logits, -jnp.inf)
        m1 = jnp.max(logits, axis=-1, keepdims=True)
        i1 = jnp.min(jnp.where(logits == m1, lane, LANES), axis=-1, keepdims=True)
        rest = jnp.where(lane == i1, -jnp.inf, logits)
        m2 = jnp.max(rest, axis=-1, keepdims=True)
        i2 = jnp.min(jnp.where(rest == m2, lane, LANES), axis=-1, keepdims=True)
        e2 = jnp.exp(m2 - m1)
        comb_ref[...] = jnp.where(lane == i1, 1.0 / (1.0 + e2), 0.0) + jnp.where(lane == i2, e2 / (1.0 + e2), 0.0)
        sel = jnp.where((lane == i1) | (lane == i2), 1.0, 0.0)
        ti = lax.broadcasted_iota(jnp.int32, (tm, tm), 0)
        si = lax.broadcasted_iota(jnp.int32, (tm, tm), 1)
        before = jnp.where(si < ti, 1.0, 0.0).astype(BF16)
        rank_c_ref[...] = _dot(before, sel.astype(BF16))
        sel_t = sel.T
        after = jnp.where(ti < si, 1.0, 0.0).astype(BF16)
        rank_r_ref[...] = _dot(sel_t.astype(BF16), after)
        mask_r_ref[...] = sel_t
        counts = jnp.sum(sel_t, axis=1, keepdims=True)
        for k in range(N_EXPERTS):
            cnt_ref[k] = jnp.sum(counts[k:k + 1, :]).astype(jnp.int32)

    n_pass = (cnt_ref[e] + (cap - 1)) // cap

    @pl.when(j == 0)
    def _():
        rank_row = rank_r_ref[pl.ds(e, 1), :]
        mask_row = mask_r_ref[pl.ds(e, 1), :]
        slot = lax.broadcasted_iota(jnp.int32, (cap, tm), 0).astype(F32)

        def compact(p, _):
            base = (p * cap).astype(F32)
            onehot = jnp.where((rank_row - base == slot) & (mask_row > 0.0), 1.0, 0.0).astype(BF16)
            rows = pl.ds(pl.multiple_of(p * cap, 16), cap)
            xc_ref[rows, :] = _dot(onehot, hn_ref[...]).astype(BF16)
            yc_ref[rows, :] = jnp.zeros((cap, D_MODEL), F32)
            return 0

        lax.fori_loop(0, n_pass, compact, 0)

    def expert(start, size):
        rows = pl.ds(pl.multiple_of(start, MOE_GRAIN), size)
        yc_ref[rows, :] += _swiglu(xc_ref[rows, :], wg_ref[0], wu_ref[0], wd_ref[0])

    n_tok = cnt_ref[e]
    n_full = n_tok // cap
    rem_blocks = (n_tok - n_full * cap + (MOE_GRAIN - 1)) // MOE_GRAIN
    n_full = n_full + (rem_blocks == cap // MOE_GRAIN).astype(jnp.int32)

    def full_pass(p, _):
        expert(p * cap, cap)
        return 0

    lax.fori_loop(0, n_full, full_pass, 0)
    for blocks in range(1, cap // MOE_GRAIN):
        @pl.when(rem_blocks == blocks)
        def _(blocks=blocks):
            expert(n_full * cap, blocks * MOE_GRAIN)

    @pl.when(j == pl.num_programs(2) - 1)
    def _():
        lane = lax.broadcasted_iota(jnp.int32, (tm, LANES), 1)
        pick = lane == e
        rank_col = jnp.sum(jnp.where(pick, rank_c_ref[...], 0.0), axis=-1, keepdims=True)
        w_col = jnp.sum(jnp.where(pick, comb_ref[...], 0.0), axis=-1, keepdims=True)
        slot = lax.broadcasted_iota(jnp.int32, (tm, cap), 1).astype(F32)

        def expand(p, _):
            base = (p * cap).astype(F32)
            onehot = jnp.where(rank_col - base == slot, 1.0, 0.0).astype(BF16)
            rows = pl.ds(pl.multiple_of(p * cap, 16), cap)
            acc_ref[...] += w_col * _dot(onehot, yc_ref[rows, :].astype(BF16))
            return 0

        lax.fori_loop(0, n_pass, expand, 0)

    @pl.when((e == N_EXPERTS - 1) & (j == pl.num_programs(2) - 1))
    def _():
        o_ref[...] = x_ref[...] + gate_ref[0] * _rms(acc_ref[...], npost_ref[...])


def _moe(x, g, shift, scale, r_hi, r_lo, w_gate, w_up, w_down, gate, npost, rows_per_mod):
    t = x.shape[0]
    n_e, _, ff = w_gate.shape
    assert n_e == N_EXPERTS
    ff_tile = ff // 4
    tm = min(MOE_TILE, rows_per_mod)
    per = rows_per_mod // tm
    passes = -(-tm // MOE_CAP)
    assert ff_tile % LANES == 0 and MOE_CAP % MOE_GRAIN == 0 and MOE_GRAIN % 16 == 0
    return pl.pallas_call(
        _moe_kernel,
        grid=(t // tm, n_e, ff // ff_tile),
        in_specs=[
            pl.BlockSpec((tm, D_MODEL), lambda i, e, j: (i, 0)),
            pl.BlockSpec((1, D_MODEL), lambda i, e, j: (0, 0)),
            pl.BlockSpec((1, 1, D_MODEL), lambda i, e, j: (i // per, 0, 0)),
            pl.BlockSpec((1, 1, D_MODEL), lambda i, e, j: (i // per, 0, 0)),
            pl.BlockSpec((D_MODEL, LANES), lambda i, e, j: (0, 0)),
            pl.BlockSpec((D_MODEL, LANES), lambda i, e, j: (0, 0)),
            pl.BlockSpec((1, D_MODEL, ff_tile), lambda i, e, j: (e, 0, j)),
            pl.BlockSpec((1, D_MODEL, ff_tile), lambda i, e, j: (e, 0, j)),
            pl.BlockSpec((1, ff_tile, D_MODEL), lambda i, e, j: (e, j, 0)),
            pl.BlockSpec((1, 1, D_MODEL), lambda i, e, j: (i // per, 0, 0)),
            pl.BlockSpec((1, D_MODEL), lambda i, e, j: (0, 0)),
        ],
        out_specs=pl.BlockSpec((tm, D_MODEL), lambda i, e, j: (i, 0)),
        out_shape=jax.ShapeDtypeStruct((t, D_MODEL), F32),
        scratch_shapes=[
            pltpu.VMEM((tm, D_MODEL), BF16),
            pltpu.VMEM((tm, D_MODEL), F32),
            pltpu.VMEM((tm, LANES), F32),
            pltpu.VMEM((tm, LANES), F32),
            pltpu.VMEM((LANES, tm), F32),
            pltpu.VMEM((LANES, tm), F32),
            pltpu.VMEM((passes * MOE_CAP, D_MODEL), BF16),
            pltpu.VMEM((passes * MOE_CAP, D_MODEL), F32),
            pltpu.SMEM((N_EXPERTS,), jnp.int32),
        ],
        compiler_params=_params(("parallel", "arbitrary", "arbitrary")),
        name="moe",
    )(x, g, shift, scale, r_hi, r_lo, w_gate, w_up, w_down, gate, npost)


MOE_ROWS = MOE_TILE * TOP_K + N_EXPERTS * MOE_GRAIN
MOE_META = 2 * N_EXPERTS
MOE_PASS_ROWS = 256


def _moe_dispatch_kernel(x_ref, g_ref, sh_ref, sc_ref, rh_ref, rl_ref, xs_ref, route_ref, meta_ref):
    tm = x_ref.shape[0]
    h = _rms(x_ref[...], g_ref[...]) * (1.0 + sc_ref[0]) + sh_ref[0]
    hi = h.astype(BF16)
    lo = (h - hi.astype(F32)).astype(BF16)
    logits = _dot(hi, rh_ref[...]) + _dot(lo, rh_ref[...]) + _dot(hi, rl_ref[...])
    lane = lax.broadcasted_iota(jnp.int32, logits.shape, 1)
    logits = jnp.where(lane < N_EXPERTS, logits, -jnp.inf)
    m1 = jnp.max(logits, axis=-1, keepdims=True)
    i1 = jnp.min(jnp.where(logits == m1, lane, LANES), axis=-1, keepdims=True)
    rest = jnp.where(lane == i1, -jnp.inf, logits)
    m2 = jnp.max(rest, axis=-1, keepdims=True)
    i2 = jnp.min(jnp.where(rest == m2, lane, LANES), axis=-1, keepdims=True)
    e2 = jnp.exp(m2 - m1)
    sel1 = jnp.where(lane == i1, 1.0, 0.0)
    sel2 = jnp.where(lane == i2, 1.0, 0.0)
    sel = sel1 + sel2
    sel1_t, sel2_t = sel1.T, sel2.T
    sel_t = sel1_t + sel2_t

    ti = lax.broadcasted_iota(jnp.int32, (tm, tm), 0)
    si = lax.broadcasted_iota(jnp.int32, (tm, tm), 1)
    rank_c = _dot(jnp.where(si < ti, 1.0, 0.0).astype(BF16), sel.astype(BF16))
    rank_r = _dot(sel_t.astype(BF16), jnp.where(ti < si, 1.0, 0.0).astype(BF16))
    blk_row = jnp.ceil(jnp.sum(sel, axis=0, keepdims=True) * (1.0 / MOE_GRAIN))
    blk_col = jnp.ceil(jnp.sum(sel_t, axis=1, keepdims=True) * (1.0 / MOE_GRAIN))
    ea = lax.broadcasted_iota(jnp.int32, (LANES, LANES), 0)
    eb = lax.broadcasted_iota(jnp.int32, (LANES, LANES), 1)
    off_row = _dot(jnp.broadcast_to(blk_row, (8, LANES)).astype(BF16),
                   jnp.where(ea < eb, 1.0, 0.0).astype(BF16))[0:1, :]
    off_col = _dot(jnp.where(eb < ea, 1.0, 0.0).astype(BF16),
                   jnp.broadcast_to(blk_col, (LANES, LANES)).astype(BF16))[:, 0:1]

    pos_c = off_row * MOE_GRAIN + rank_c
    pos1_c = jnp.sum(jnp.where(lane == i1, pos_c, 0.0), axis=-1, keepdims=True)
    pos2_c = jnp.sum(jnp.where(lane == i2, pos_c, 0.0), axis=-1, keepdims=True)
    route_ref[...] = (jnp.where(lane == 0, pos1_c, 0.0) + jnp.where(lane == 1, pos2_c, 0.0)
                      + jnp.where(lane == 2, 1.0 / (1.0 + e2), 0.0) + jnp.where(lane == 3, e2 / (1.0 + e2), 0.0))
    pos_r = off_col * MOE_GRAIN + rank_r
    pos1_r = jnp.sum(sel1_t * pos_r, axis=0, keepdims=True)
    pos2_r = jnp.sum(sel2_t * pos_r, axis=0, keepdims=True)
    for c in range(MOE_ROWS // MOE_CAP):
        slot = (lax.broadcasted_iota(jnp.int32, (MOE_CAP, tm), 0) + c * MOE_CAP).astype(F32)
        onehot = jnp.where(slot == pos1_r, 1.0, jnp.where(slot == pos2_r, 1.0, 0.0)).astype(BF16)
        xs_ref[c * MOE_CAP:(c + 1) * MOE_CAP, :] = _dot(onehot, hi).astype(BF16)

    lane8 = lax.broadcasted_iota(jnp.int32, (8, LANES), 1)
    blk8 = jnp.broadcast_to(blk_row, (8, LANES))
    off8 = pltpu.roll(jnp.broadcast_to(off_row, (8, LANES)), N_EXPERTS, 1)
    meta = jnp.where(lane8 < N_EXPERTS, blk8, jnp.where(lane8 < MOE_META, off8, 0.0))
    meta_ref[0] = meta.astype(jnp.int32)


def _moe_dispatch(x, g, shift, scale, r_hi, r_lo, rows_per_mod):
    t = x.shape[0]
    tm = MOE_TILE
    assert rows_per_mod % tm == 0
    per = rows_per_mod // tm
    n_t = t // tm
    return pl.pallas_call(
        _moe_dispatch_kernel,
        grid=(n_t,),
        in_specs=[
            pl.BlockSpec((tm, D_MODEL), lambda i: (i, 0)),
            pl.BlockSpec((1, D_MODEL), lambda i: (0, 0)),
            pl.BlockSpec((1, 1, D_MODEL), lambda i: (i // per, 0, 0)),
            pl.BlockSpec((1, 1, D_MODEL), lambda i: (i // per, 0, 0)),
            pl.BlockSpec((D_MODEL, LANES), lambda i: (0, 0)),
            pl.BlockSpec((D_MODEL, LANES), lambda i: (0, 0)),
        ],
        out_specs=[
            pl.BlockSpec((MOE_ROWS, D_MODEL), lambda i: (i, 0)),
            pl.BlockSpec((tm, LANES), lambda i: (i, 0)),
            pl.BlockSpec((1, 8, LANES), lambda i: (i, 0, 0)),
        ],
        out_shape=[
            jax.ShapeDtypeStruct((n_t * MOE_ROWS, D_MODEL), BF16),
            jax.ShapeDtypeStruct((t, LANES), F32),
            jax.ShapeDtypeStruct((n_t, 8, LANES), jnp.int32),
        ],
        compiler_params=_params(("parallel",)),
        name="moe_dispatch",
    )(x, g, shift, scale, r_hi, r_lo)


def _moe_experts_kernel(meta_ref, xs_ref, wg_ref, wu_ref, wd_ref, out_ref, *, ff_tile):
    e = pl.program_id(0)
    i = pl.program_id(1)
    out_ref[...] = xs_ref[...]
    n_blk = meta_ref[i * MOE_META + e]
    first = meta_ref[i * MOE_META + N_EXPERTS + e] * MOE_GRAIN
    per_pass = MOE_PASS_ROWS // MOE_GRAIN
    n_full = n_blk // per_pass
    rem = n_blk - n_full * per_pass
    ff = wg_ref.shape[2]

    def run(start, size):
        rows = pl.ds(pl.multiple_of(start, MOE_GRAIN), size)
        xin = xs_ref[rows, :]
        y = jnp.zeros((size, D_MODEL), F32)
        for j in range(ff // ff_tile):
            cols = slice(j * ff_tile, (j + 1) * ff_tile)
            y += _swiglu(xin, wg_ref[0, :, cols], wu_ref[0, :, cols], wd_ref[0, cols, :])
        out_ref[rows, :] = y.astype(out_ref.dtype)

    def full_pass(p, _):
        run(first + p * MOE_PASS_ROWS, MOE_PASS_ROWS)
        return 0

    lax.fori_loop(0, n_full, full_pass, 0)
    for blocks in range(1, per_pass):
        @pl.when(rem == blocks)
        def _(blocks=blocks):
            run(first + n_full * MOE_PASS_ROWS, blocks * MOE_GRAIN)


MOE_FF_TILE = 512


def _moe_experts(xs, meta, w_gate, w_up, w_down):
    n_e, _, ff = w_gate.shape
    n_t = xs.shape[0] // MOE_ROWS
    assert ff % MOE_FF_TILE == 0
    resident = pl.Buffered(1)
    grid_spec = pltpu.PrefetchScalarGridSpec(
        num_scalar_prefetch=1,
        grid=(n_e, n_t),
        in_specs=[
            pl.BlockSpec((MOE_ROWS, D_MODEL), lambda e, i, m: (i, 0)),
            pl.BlockSpec((1, D_MODEL, ff), lambda e, i, m: (e, 0, 0), pipeline_mode=resident),
            pl.BlockSpec((1, D_MODEL, ff), lambda e, i, m: (e, 0, 0), pipeline_mode=resident),
            pl.BlockSpec((1, ff, D_MODEL), lambda e, i, m: (e, 0, 0), pipeline_mode=resident),
        ],
        out_specs=pl.BlockSpec((MOE_ROWS, D_MODEL), lambda e, i, m: (i, 0)),
    )
    return pl.pallas_call(
        functools.partial(_moe_experts_kernel, ff_tile=MOE_FF_TILE),
        grid_spec=grid_spec,
        out_shape=jax.ShapeDtypeStruct(xs.shape, xs.dtype),
        input_output_aliases={1: 0},
        compiler_params=_params(("arbitrary", "arbitrary")),
        name="moe_experts",
    )(meta, xs, w_gate, w_up, w_down)


def _moe_combine_kernel(ys_ref, route_ref, x_ref, gate_ref, npost_ref, o_ref):
    tm = x_ref.shape[0]
    lane = lax.broadcasted_iota(jnp.int32, (tm, LANES), 1)
    route = route_ref[...]
    pos1, pos2, w1, w2 = (jnp.sum(jnp.where(lane == k, route, 0.0), axis=-1, keepdims=True) for k in range(4))
    width = 512
    acc = jnp.zeros((tm, D_MODEL), F32)
    for c in range(MOE_ROWS // width):
        slot = (lax.broadcasted_iota(jnp.int32, (tm, width), 1) + c * width).astype(F32)
        wmat = jnp.where(slot == pos1, w1, 0.0) + jnp.where(slot == pos2, w2, 0.0)
        acc += _dot(wmat.astype(BF16), ys_ref[c * width:(c + 1) * width, :])
    o_ref[...] = x_ref[...] + gate_ref[0] * _rms(acc, npost_ref[...])


def _moe_combine(ys, route, x, gate, npost, rows_per_mod):
    t = x.shape[0]
    tm = MOE_TILE
    per = rows_per_mod // tm
    return pl.pallas_call(
        _moe_combine_kernel,
        grid=(t // tm,),
        in_specs=[
            pl.BlockSpec((MOE_ROWS, D_MODEL), lambda i: (i, 0)),
            pl.BlockSpec((tm, LANES), lambda i: (i, 0)),
            pl.BlockSpec((tm, D_MODEL), lambda i: (i, 0)),
            pl.BlockSpec((1, 1, D_MODEL), lambda i: (i // per, 0, 0)),
            pl.BlockSpec((1, D_MODEL), lambda i: (0, 0)),
        ],
        out_specs=pl.BlockSpec((tm, D_MODEL), lambda i: (i, 0)),
        out_shape=jax.ShapeDtypeStruct((t, D_MODEL), F32),
        compiler_params=_params(("parallel",)),
        name="moe_combine",
    )(ys, route, x, gate, npost)


MOE_BLOCK = 512
MOE_DMA_WINDOW = 64


def _moe_plan(meta, n_t):
    per_tile = MOE_ROWS // MOE_GRAIN
    per_blk = MOE_BLOCK // MOE_GRAIN
    nblk = meta[:, 0, :N_EXPERTS]
    first = meta[:, 0, N_EXPERTS:MOE_META]
    tot = jnp.sum(nblk, axis=0)
    tot_pad = ((tot + per_blk - 1) // per_blk) * per_blk
    ends = jnp.cumsum(tot_pad)
    seg_dst = (ends - tot_pad)[None, :] + jnp.cumsum(nblk, axis=0) - nblk
    k = jnp.arange(per_tile, dtype=jnp.int32)[None, :, None]
    inside = (k >= first[:, None, :]) & (k < (first + nblk)[:, None, :])
    dst = jnp.sum(jnp.where(inside, seg_dst[:, None, :] + k - first[:, None, :], 0), axis=-1)
    dst = jnp.where(jnp.any(inside, axis=-1), dst, -1).reshape(-1).astype(jnp.int32)
    n_big = (n_t * per_tile + N_EXPERTS * per_blk) // per_blk
    big = jnp.arange(n_big, dtype=jnp.int32)[:, None] * per_blk
    expert = jnp.minimum(jnp.sum(big >= ends[None, :], axis=-1), N_EXPERTS - 1).astype(jnp.int32)
    valid = (big[:, 0] < ends[-1]).astype(jnp.int32)
    return dst, jnp.concatenate([expert, valid])


def _moe_regroup_kernel(map_ref, src_ref, dst_in_ref, dst_ref, sem, *, n_blocks, to_global):
    del dst_in_ref

    def copy(b):
        t_rows = pl.ds(pl.multiple_of(b * MOE_GRAIN, MOE_GRAIN), MOE_GRAIN)
        g_rows = pl.ds(pl.multiple_of(map_ref[b] * MOE_GRAIN, MOE_GRAIN), MOE_GRAIN)
        if to_global:
            return pltpu.make_async_copy(src_ref.at[t_rows], dst_ref.at[g_rows], sem)
        return pltpu.make_async_copy(src_ref.at[g_rows], dst_ref.at[t_rows], sem)

    def start(b):
        @pl.when(map_ref[b] >= 0)
        def _():
            copy(b).start()

    def wait(b):
        @pl.when(map_ref[b] >= 0)
        def _():
            copy(b).wait()

    win = min(MOE_DMA_WINDOW, n_blocks)

    def fill(b, _):
        start(b)
        return 0

    def steady(b, _):
        wait(b)
        start(b + win)
        return 0

    def drain(b, _):
        wait(b)
        return 0

    lax.fori_loop(0, win, fill, 0)
    lax.fori_loop(0, n_blocks - win, steady, 0)
    lax.fori_loop(n_blocks - win, n_blocks, drain, 0)


def _moe_regroup(block_map, src, dst, to_global):
    n_blocks = block_map.shape[0]
    return pl.pallas_call(
        functools.partial(_moe_regroup_kernel, n_blocks=n_blocks, to_global=to_global),
        in_specs=[
            pl.BlockSpec(memory_space=pltpu.SMEM),
            pl.BlockSpec(memory_space=pl.ANY),
            pl.BlockSpec(memory_space=pl.ANY),
        ],
        out_specs=pl.BlockSpec(memory_space=pl.ANY),
        out_shape=jax.ShapeDtypeStruct(dst.shape, dst.dtype),
        scratch_shapes=[pltpu.SemaphoreType.DMA(())],
        input_output_aliases={2: 0},
        compiler_params=pltpu.CompilerParams(has_side_effects=True),
        name="moe_to_global" if to_global else "moe_to_tiles",
    )(block_map, src, dst)


def _moe_global_kernel(plan_ref, x_ref, wg_ref, wu_ref, wd_ref, out_ref, *, ff_tile):
    b = pl.program_id(0)
    valid = plan_ref[pl.num_programs(0) + b]
    ff = wg_ref.shape[2]

    @pl.when(valid == 1)
    def _():
        xin = x_ref[...]
        y = jnp.zeros((MOE_BLOCK, D_MODEL), F32)
        for j in range(ff // ff_tile):
            cols = slice(j * ff_tile, (j + 1) * ff_tile)
            y += _swiglu(xin, wg_ref[0, :, cols], wu_ref[0, :, cols], wd_ref[0, cols, :])
        out_ref[...] = y.astype(out_ref.dtype)

    @pl.when(valid == 0)
    def _():
        out_ref[...] = x_ref[...]


def _moe_experts_global(xg, plan, w_gate, w_up, w_down):
    n_e, _, ff = w_gate.shape
    n_big = xg.shape[0] // MOE_BLOCK
    resident = pl.Buffered(1)
    grid_spec = pltpu.PrefetchScalarGridSpec(
        num_scalar_prefetch=1,
        grid=(n_big,),
        in_specs=[
            pl.BlockSpec((MOE_BLOCK, D_MODEL), lambda b, p: (b, 0)),
            pl.BlockSpec((1, D_MODEL, ff), lambda b, p: (p[b], 0, 0), pipeline_mode=resident),
            pl.BlockSpec((1, D_MODEL, ff), lambda b, p: (p[b], 0, 0), pipeline_mode=resident),
            pl.BlockSpec((1, ff, D_MODEL), lambda b, p: (p[b], 0, 0), pipeline_mode=resident),
        ],
        out_specs=pl.BlockSpec((MOE_BLOCK, D_MODEL), lambda b, p: (b, 0)),
    )
    return pl.pallas_call(
        functools.partial(_moe_global_kernel, ff_tile=MOE_FF_TILE),
        grid_spec=grid_spec,
        out_shape=jax.ShapeDtypeStruct(xg.shape, xg.dtype),
        input_output_aliases={1: 0},
        compiler_params=_params(("arbitrary",)),
        name="moe_experts",
    )(plan, xg, w_gate, w_up, w_down)


def _moe_global(x, g, shift, scale, r_hi, r_lo, w_gate, w_up, w_down, gate, npost, rows_per_mod):
    xs, route, meta = _moe_dispatch(x, g, shift, scale, r_hi, r_lo, rows_per_mod)
    n_t = xs.shape[0] // MOE_ROWS
    block_map, plan = _moe_plan(meta, n_t)
    xg = jnp.zeros((xs.shape[0] + N_EXPERTS * MOE_BLOCK, D_MODEL), BF16)
    xg = _moe_regroup(block_map, xs, xg, True)
    yg = _moe_experts_global(xg, plan, w_gate, w_up, w_down)
    ys = _moe_regroup(block_map, yg, xs, False)
    return _moe_combine(ys, route, x, gate, npost, rows_per_mod)


def _moe_sorted(x, g, shift, scale, r_hi, r_lo, w_gate, w_up, w_down, gate, npost, rows_per_mod):
    xs, route, meta = _moe_dispatch(x, g, shift, scale, r_hi, r_lo, rows_per_mod)
    meta = meta[:, 0, :MOE_META].reshape(-1)
    ys = _moe_experts(xs, meta, w_gate, w_up, w_down)
    return _moe_combine(ys, route, x, gate, npost, rows_per_mod)


def _reorder_w_in(w):
    o = 4 * M_WIDTH
    gates = jnp.pad(w[:, o:o + M_GATES], ((0, 0), (0, LANES - M_GATES)))
    na = w[:, o + M_GATES:o + M_GATES + 3 * NA_WIDTH]
    u0 = o + M_GATES + 3 * NA_WIDTH
    main = jnp.concatenate([w[:, :o], w[:, u0 + S5_WIDTH:], na], axis=1)
    return main.astype(BF16), gates.astype(BF16), w[:, u0:u0 + S5_WIDTH].astype(BF16)


def _row(v):
    return v.reshape(1, -1).astype(F32)


def _token_mixers(x, xc, mx, mc, lp, rope, seq_len, ctx_len, ctx_out):
    bsz = x.shape[0] // seq_len
    w_main, w_gates, w_u = _reorder_w_in(lp['w_in'])
    g_pre = _row(lp['norm_mix_pre'])
    px, gx, ux = _inproj(x, g_pre, mx[0], mx[1], w_main, w_gates, w_u, seq_len)
    pc, gc, uc = _inproj(xc, g_pre, mc[0], mc[1], w_main, w_gates, w_u, xc.shape[0])

    conv_w = lp['m_conv_w'].astype(F32)
    conv_b = _row(lp['m_conv_b'])
    gate_b = jnp.pad(_row(lp['m_gate_b']), ((0, 0), (0, LANES - M_GATES)))
    m_norm = _row(lp['m_norm'])
    qk_c = _qkconv(pc, conv_w, conv_b, rope[0][:ctx_len], rope[1][:ctx_len], ctx_len, False)
    qk_x = _qkconv(px, conv_w, conv_b, rope[0], rope[1], seq_len, True)
    zero = _mlstm_zero_state(bsz)
    ym_c, fin_f, fin_b = _mlstm_bidir(qk_c, pc, gc, gate_b, m_norm, zero, zero, ctx_len)
    ym_x, _, _ = _mlstm_bidir(qk_x, px, gx, gate_b, m_norm, fin_f, fin_b, seq_len)

    yn_x = _na_attention(px, pc, _na_bias_table(lp['na_rpb']), seq_len, ctx_len)

    tables = _s5_tables(lp['s5_lam_re'], lp['s5_lam_im'], lp['s5_log_dt'], lp['s5_b_re'], lp['s5_b_im'],
                        lp['s5_c_re'], lp['s5_c_im'])
    zs = jnp.zeros((bsz, S5_GROUPS * S5_STATE), F32)
    ys_c, fin_s = _s5_branch(uc, tables, ctx_len, (zs, zs, zs, zs), ctx_out)
    ys_x, _ = _s5_branch(ux, tables, seq_len, fin_s, True)

    wts = (_row(lp['s5_d']), lp['s5_glu_w'].astype(BF16), _row(lp['s5_glu_b']),
           lp['w_branch_m'].astype(BF16), lp['w_branch_na'].astype(BF16), lp['w_branch_s5'].astype(BF16),
           lp['w_out'].astype(BF16))
    g_post = _row(lp['norm_mix_post'])
    x = _merge(x, ym_x, yn_x, ys_x, ux, px, wts, mx[2], g_post, seq_len)
    if ctx_out:
        yn_c = _ctx_attention(pc, ctx_len)
        xc = _merge(xc, ym_c, yn_c, ys_c, uc, pc, wts, mc[2], g_post, xc.shape[0])
    return x, xc


def kernel(x, c, ctx, c_ctx, ada_w, ada_b, norm_mix_pre, norm_mix_post, norm_ffn_pre, norm_ffn_post, w_in, m_gate_b, m_conv_w, m_conv_b, m_norm, na_rpb, s5_lam_re, s5_lam_im, s5_log_dt, s5_b_re, s5_b_im, s5_c_re, s5_c_im, s5_d, s5_glu_w, s5_glu_b, w_branch_m, w_branch_na, w_branch_s5, w_out, ffn_w_gate, ffn_w_up, ffn_w_down, moe_router, moe_w_gate, moe_w_up, moe_w_down):
    bsz, seq_len, d = x.shape
    ctx_len = ctx.shape[1]
    depth = w_in.shape[0]
    assert d == D_MODEL and seq_len % (GRID_W * NA_ROWS) == 0 and ctx_len % M_CHUNK == 0
    rope = _rope_tables(seq_len)
    xf = x.reshape(bsz * seq_len, d)
    xc = ctx.reshape(bsz * ctx_len, d)
    cvecs = jnp.zeros((16, d), F32).at[:bsz].set(c).at[bsz].set(c_ctx)
    for l in range(depth):
        last = l == depth - 1
        mod = _adaln(cvecs, ada_w[l].astype(BF16), _row(ada_b[l]))
        mx = [mod[:bsz, k * d:(k + 1) * d].reshape(bsz, 1, d) for k in range(N_MOD)]
        mc = [mod[bsz:bsz + 1, k * d:(k + 1) * d].reshape(1, 1, d) for k in range(N_MOD)]
        lp = {
            'norm_mix_pre': norm_mix_pre[l], 'norm_mix_post': norm_mix_post[l],
            'w_in': w_in[l], 'm_gate_b': m_gate_b[l], 'm_conv_w': m_conv_w[l], 'm_conv_b': m_conv_b[l],
            'm_norm': m_norm[l], 'na_rpb': na_rpb[l], 's5_lam_re': s5_lam_re[l], 's5_lam_im': s5_lam_im[l],
            's5_log_dt': s5_log_dt[l], 's5_b_re': s5_b_re[l], 's5_b_im': s5_b_im[l], 's5_c_re': s5_c_re[l],
            's5_c_im': s5_c_im[l], 's5_d': s5_d[l], 's5_glu_w': s5_glu_w[l], 's5_glu_b': s5_glu_b[l],
            'w_branch_m': w_branch_m[l], 'w_branch_na': w_branch_na[l], 'w_branch_s5': w_branch_s5[l],
            'w_out': w_out[l],
        }
        xf, xc = _token_mixers(xf, xc, mx, mc, lp, rope, seq_len, ctx_len, not last)
        j = l // 2
        g_pre, g_post = _row(norm_ffn_pre[l]), _row(norm_ffn_post[l])
        if l % 2 == 0:
            wg, wu, wd = ffn_w_gate[j].astype(BF16), ffn_w_up[j].astype(BF16), ffn_w_down[j].astype(BF16)

            def channel(h, m, rows):
                return _ffn(h, g_pre, m[3], m[4], wg, wu, wd, m[5], g_post, rows)
        else:
            wg, wu, wd = moe_w_gate[j].astype(BF16), moe_w_up[j].astype(BF16), moe_w_down[j].astype(BF16)
            router = jnp.pad(moe_router[j].astype(F32), ((0, 0), (0, LANES - N_EXPERTS)))
            r_hi = router.astype(BF16)
            r_lo = (router - r_hi.astype(F32)).astype(BF16)

            def channel(h, m, rows):
                return _moe_global(h, g_pre, m[3], m[4], r_hi, r_lo, wg, wu, wd, m[5], g_post, rows)
        xf = channel(xf, mx, seq_len)
        if not last:
            xc = channel(xc, mc, xc.shape[0])
    return xf.reshape(bsz, seq_len, d)
```

```python
import functools
import math

import jax
import jax.numpy as jnp
from jax import lax
from jax.experimental import pallas as pl
from jax.experimental.pallas import tpu as pltpu

F32 = jnp.float32
BF16 = jnp.bfloat16

D_MODEL = 1024
EPS = 1e-6
N_MOD = 6
GRID_W = 64

M_HEADS = 4
M_HEAD_DIM = 128
M_WIDTH = M_HEADS * M_HEAD_DIM
M_GATES = 4 * M_HEADS
M_CHUNK = 128
ROPE_BASE = 10000.0

NA_HEADS = 8
NA_HEAD_DIM = 64
NA_WIDTH = NA_HEADS * NA_HEAD_DIM
WIN_R = 8
WIN_C = 16
NEG_BIG = -1e30

S5_GROUP = 16
S5_WIDTH = 512
S5_GROUPS = S5_WIDTH // S5_GROUP
S5_STATE = 64
S5_MAX_RE = -1e-4

LANES = 128
VMEM_LIMIT = 52 * 1024 * 1024

S5_CHUNK = 16
S5_QUAD_GROUPS = LANES // S5_GROUP
S5_QUADS = S5_GROUPS // S5_QUAD_GROUPS
S5_QCOLS = S5_CHUNK * LANES
S5_QSTATE = S5_QUAD_GROUPS * S5_STATE

N_EXPERTS = 8
TOP_K = 2
MOE_TILE = 1024
MOE_CAP = 256
MOE_GRAIN = 64

COL_QK, COL_V, COL_O, COL_NAQ, COL_NAK, COL_NAV = 0, 2, 3, 10, 11, 12
COL_GATE = 2
IN_MAIN = 6656


def _params(sem):
    return pltpu.CompilerParams(dimension_semantics=sem, vmem_limit_bytes=VMEM_LIMIT)


def _dot(a, b):
    return jnp.dot(a, b, preferred_element_type=F32)


def _dot_nt(a, b):
    return lax.dot_general(a, b, (((1,), (1,)), ((), ())), preferred_element_type=F32)


def _split3(x):
    hi = x.astype(BF16)
    r1 = x - hi.astype(F32)
    mid = r1.astype(BF16)
    lo = (r1 - mid.astype(F32)).astype(BF16)
    return hi, mid, lo


def _rms(x, g):
    return x * lax.rsqrt(jnp.mean(x * x, axis=-1, keepdims=True) + EPS) * g


def _inproj_kernel(x_ref, g_ref, sh_ref, sc_ref, w_ref, wg_ref, wu_ref, o_ref, og_ref, ou_ref, hn_ref):
    @pl.when(pl.program_id(1) == 0)
    def _():
        h = _rms(x_ref[...], g_ref[...]) * (1.0 + sc_ref[0]) + sh_ref[0]
        hb = h.astype(BF16)
        hn_ref[...] = hb
        og_ref[...] = _dot(hb, wg_ref[...])
        ou_ref[...] = _dot(hb, wu_ref[...])

    o_ref[...] = _dot(hn_ref[...], w_ref[...]).astype(o_ref.dtype)


def _inproj(x, g, shift, scale, w_main, w_gates, w_u, rows_per_mod):
    t = x.shape[0]
    tm = min(1024, rows_per_mod)
    tn = IN_MAIN // 4
    assert rows_per_mod % tm == 0 and t % tm == 0 and tn % LANES == 0
    per = rows_per_mod // tm
    return pl.pallas_call(
        _inproj_kernel,
        grid=(t // tm, IN_MAIN // tn),
        in_specs=[
            pl.BlockSpec((tm, D_MODEL), lambda i, j: (i, 0)),
            pl.BlockSpec((1, D_MODEL), lambda i, j: (0, 0)),
            pl.BlockSpec((1, 1, D_MODEL), lambda i, j: (i // per, 0, 0)),
            pl.BlockSpec((1, 1, D_MODEL), lambda i, j: (i // per, 0, 0)),
            pl.BlockSpec((D_MODEL, tn), lambda i, j: (0, j)),
            pl.BlockSpec((D_MODEL, LANES), lambda i, j: (0, 0)),
            pl.BlockSpec((D_MODEL, S5_WIDTH), lambda i, j: (0, 0)),
        ],
        out_specs=[
            pl.BlockSpec((tm, tn), lambda i, j: (i, j)),
            pl.BlockSpec((tm, LANES), lambda i, j: (i, 0)),
            pl.BlockSpec((tm, S5_WIDTH), lambda i, j: (i, 0)),
        ],
        out_shape=[jax.ShapeDtypeStruct((t, IN_MAIN), BF16), jax.ShapeDtypeStruct((t, LANES), F32),
                   jax.ShapeDtypeStruct((t, S5_WIDTH), F32)],
        scratch_shapes=[pltpu.VMEM((tm, D_MODEL), BF16)],
        compiler_params=_params(("parallel", "arbitrary")),
        name="inproj",
    )(x, g, shift, scale, w_main, w_gates, w_u)


def _adaln_kernel(c_ref, w_ref, b_ref, o_ref):
    c = c_ref[...]
    s = c * jax.nn.sigmoid(c)
    o_ref[...] = _dot(s.astype(BF16), w_ref[...]) + b_ref[...]


def _adaln(cvecs, w, b):
    n = w.shape[1]
    tn = 1024
    return pl.pallas_call(
        _adaln_kernel,
        grid=(n // tn,),
        in_specs=[
            pl.BlockSpec(cvecs.shape, lambda j: (0, 0)),
            pl.BlockSpec((D_MODEL, tn), lambda j: (0, j)),
            pl.BlockSpec((1, tn), lambda j: (0, j)),
        ],
        out_specs=pl.BlockSpec((cvecs.shape[0], tn), lambda j: (0, j)),
        out_shape=jax.ShapeDtypeStruct((cvecs.shape[0], n), F32),
        compiler_params=_params(("parallel",)),
        name="adaln",
    )(cvecs, w, b)


HALO = 16


def _swap32(x):
    lane = lax.broadcasted_iota(jnp.int32, x.shape, 1)
    fwd = pltpu.roll(x, 96, 1)
    bwd = pltpu.roll(x, 32, 1)
    return jnp.where((lane % 64) < 32, fwd, bwd)


def _qkconv_kernel(x_ref, pv_ref, nx_ref, w_ref, b_ref, cos_ref, sin_ref, o_ref, *, tiles_per_seq, rope):
    i = pl.program_id(0)
    tm = x_ref.shape[0]
    x = x_ref[...].astype(F32)
    first = (i % tiles_per_seq) == 0
    last = (i % tiles_per_seq) == tiles_per_seq - 1
    prev_row = jnp.where(first, 0.0, pv_ref[HALO - 1:HALO, :].astype(F32))
    next_row = jnp.where(last, 0.0, nx_ref[0:1, :].astype(F32))
    row = lax.broadcasted_iota(jnp.int32, x.shape, 0)
    xp = jnp.where(row == 0, prev_row, pltpu.roll(x, 1, 0))
    xn = jnp.where(row == tm - 1, next_row, pltpu.roll(x, tm - 1, 0))
    y = xp * w_ref[0:1, :] + x * w_ref[1:2, :] + xn * w_ref[2:3, :] + b_ref[...]
    y = y * jax.nn.sigmoid(y)
    kscale = M_HEAD_DIM ** -0.5
    for hh in range(2 * M_HEADS):
        ys = y[:, hh * LANES:(hh + 1) * LANES]
        if rope:
            ys = ys * cos_ref[...] + _swap32(ys) * sin_ref[...]
        if hh >= M_HEADS:
            ys = ys * kscale
        o_ref[:, hh * LANES:(hh + 1) * LANES] = ys.astype(o_ref.dtype)


def _qkconv(px, conv_w, conv_b, cos_t, sin_t, seq_len, rope):
    t = px.shape[0]
    tm = min(512, seq_len)
    assert seq_len % tm == 0
    tiles = seq_len // tm
    hb = tm // HALO
    nh = t // HALO
    return pl.pallas_call(
        functools.partial(_qkconv_kernel, tiles_per_seq=tiles, rope=rope),
        grid=(t // tm,),
        in_specs=[
            pl.BlockSpec((tm, 2 * M_WIDTH), lambda i: (i, 0)),
            pl.BlockSpec((HALO, 2 * M_WIDTH), lambda i: (jnp.maximum(i * hb - 1, 0), 0)),
            pl.BlockSpec((HALO, 2 * M_WIDTH), lambda i: (jnp.minimum((i + 1) * hb, nh - 1), 0)),
            pl.BlockSpec((3, 2 * M_WIDTH), lambda i: (0, 0)),
            pl.BlockSpec((1, 2 * M_WIDTH), lambda i: (0, 0)),
            pl.BlockSpec((tm, LANES), lambda i: (i % tiles, 0)),
            pl.BlockSpec((tm, LANES), lambda i: (i % tiles, 0)),
        ],
        out_specs=pl.BlockSpec((tm, 2 * M_WIDTH), lambda i: (i, 0)),
        out_shape=jax.ShapeDtypeStruct((t, 2 * M_WIDTH), BF16),
        compiler_params=_params(("parallel",)),
        name="qkconv",
    )(px, px, px, conv_w, conv_b, cos_t, sin_t)


def _rope_tables(n):
    pos = jnp.arange(n, dtype=jnp.int32)
    row = (pos // GRID_W).astype(F32)
    col = (pos % GRID_W).astype(F32)
    n_freq = M_HEAD_DIM // 4
    inv = ROPE_BASE ** (-jnp.arange(n_freq, dtype=F32) / n_freq)
    ar, ac = row[:, None] * inv, col[:, None] * inv
    cos_t = jnp.concatenate([jnp.cos(ar), jnp.cos(ar), jnp.cos(ac), jnp.cos(ac)], axis=-1)
    sin_t = jnp.concatenate([-jnp.sin(ar), jnp.sin(ar), -jnp.sin(ac), jnp.sin(ac)], axis=-1)
    return cos_t, sin_t


def _mlstm_kernel(*refs, reverse, final, nb):
    if final:
        (qk_ref, v_ref, g_ref, gb_ref, c0_ref, n0_ref, m0_ref, hp_ref, op_ref, nw_ref,
         out_ref, cf_ref, nf_ref, mf_ref, c_scr, n_scr, m_scr, qk_scr, qc_scr, qn_scr, s_scr, vt_scr, vu_scr, dec_scr) = refs
    else:
        (qk_ref, v_ref, g_ref, gb_ref, c0_ref, n0_ref, m0_ref,
         out_ref, cf_ref, nf_ref, mf_ref, c_scr, n_scr, m_scr, qk_scr, qc_scr, qn_scr, s_scr, vt_scr, vu_scr, dec_scr) = refs
    c_idx = pl.program_id(1)
    n_c = pl.num_programs(1)
    L = M_CHUNK

    @pl.when(c_idx == 0)
    def _():
        c_scr[...] = c0_ref[...]
        n_scr[...] = n0_ref[...]
        m_scr[...] = m0_ref[...]

    ti = lax.broadcasted_iota(jnp.int32, (L, L), 0)
    si = lax.broadcasted_iota(jnp.int32, (L, L), 1)
    keep = (si >= ti) if reverse else (si <= ti)
    tri = jnp.where(keep, 1.0, 0.0).astype(BF16)
    i_off = 2 * M_HEADS if reverse else 0
    f_off = i_off + M_HEADS
    end = 0 if reverse else L - 1

    units = [(a, b) for a in range(nb) for b in range(M_HEADS)]

    def head(ref, gi, h, base=0):
        return ref[gi, :, base + h * LANES:base + (h + 1) * LANES]

    keep_t = (ti >= si) if reverse else (ti <= si)

    for u, (gi, h) in enumerate(units):
        q = head(qk_ref, gi, h)
        qk_scr[u] = _dot_nt(head(qk_ref, gi, h, M_WIDTH), q)
        qc_scr[u] = _dot_nt(c_scr[gi, h].astype(BF16), q)
        qn_scr[u] = _dot_nt(jnp.broadcast_to(n_scr[gi, h], (8, LANES)).astype(BF16), q)

    for u, (gi, h) in enumerate(units):
        if h == 0:
            g = g_ref[gi] + gb_ref[...]
            f_hi, f_mid, f_lo = _split3(jax.nn.log_sigmoid(g))
            bsum = _dot(tri, f_hi) + _dot(tri, f_mid) + _dot(tri, f_lo)
            g_t = g.T
            b_t = bsum.T
        m_prev = m_scr[gi, h][:, 0:1]
        b_row = b_t[f_off + h:f_off + h + 1, :]
        i_row = g_t[i_off + h:i_off + h + 1, :]
        b_end = b_row[:, end:end + 1]
        d_col = g[:, i_off + h:i_off + h + 1] - bsum[:, f_off + h:f_off + h + 1]

        log_w = jnp.where(keep_t, b_row + d_col, -jnp.inf)
        carry_log = b_row + m_prev
        m_t = jnp.maximum(carry_log, jnp.max(log_w, axis=0, keepdims=True))
        s_t = qk_scr[u] * jnp.exp(log_w - m_t)
        c_scale = jnp.exp(carry_log - m_t)
        den = jnp.sum(s_t, axis=0, keepdims=True) + c_scale * qn_scr[u][0:1, :]
        inv = 1.0 / jnp.maximum(jnp.abs(den), jnp.exp(-m_t))
        s_scr[u] = (s_t * inv).astype(BF16)
        qc_scr[u] = qc_scr[u] * (c_scale * inv)

        m_new = jnp.maximum(b_end + m_prev, jnp.max(b_end - b_row + i_row, axis=-1, keepdims=True))
        u_row = jnp.exp(b_end - b_row + i_row - m_new)
        decay = jnp.exp(b_end + m_prev - m_new)
        v_t = head(v_ref, gi, h).astype(F32).T
        vt_scr[u] = v_t.astype(BF16)
        vu_scr[u] = (v_t * u_row).astype(BF16)
        u8 = jnp.broadcast_to(u_row, (8, L)).astype(BF16)
        n_scr[gi, h] = decay * n_scr[gi, h] + _dot(u8, head(qk_ref, gi, h, M_WIDTH))[0:1, :]
        m_scr[gi, h] = jnp.broadcast_to(m_new, (1, LANES))
        dec_scr[u] = jnp.broadcast_to(decay, (1, LANES))

    for u, (gi, h) in enumerate(units):
        qc_scr[u] = _dot(vt_scr[u], s_scr[u]) + qc_scr[u]
        c_scr[gi, h] = dec_scr[u][:, 0:1] * c_scr[gi, h] + _dot(vu_scr[u], head(qk_ref, gi, h, M_WIDTH))

    for u, (gi, h) in enumerate(units):
        rows = slice(h * M_HEAD_DIM, (h + 1) * M_HEAD_DIM)
        if final:
            hs = qc_scr[u] + hp_ref[gi, rows, :]
            mu = jnp.mean(hs, axis=0, keepdims=True)
            var = jnp.mean(jnp.square(hs - mu), axis=0, keepdims=True)
            hn = ((hs - mu) * lax.rsqrt(var + EPS)).T
            y = hn * nw_ref[:, h * LANES:(h + 1) * LANES] * jax.nn.sigmoid(head(op_ref, gi, h).astype(F32))
            out_ref[gi, :, h * LANES:(h + 1) * LANES] = y.astype(out_ref.dtype)
        else:
            out_ref[gi, rows, :] = qc_scr[u]

    @pl.when(c_idx == n_c - 1)
    def _():
        cf_ref[...] = c_scr[...]
        nf_ref[...] = n_scr[...]
        mf_ref[...] = m_scr[...]


def _mlstm_scan(qk, px, gates, gate_b, state, seq_len, reverse, hprev=None, norm_w=None):
    t = qk.shape[0]
    bsz = t // seq_len
    n_c = seq_len // M_CHUNK
    final = hprev is not None
    nb = 4 if bsz % 4 == 0 else (2 if bsz % 2 == 0 else 1)
    qk, px, gates = (a.reshape(bsz, seq_len, a.shape[1]) for a in (qk, px, gates))

    def chunk(c):
        return (n_c - 1 - c) if reverse else c

    state_specs = [
        pl.BlockSpec((nb, M_HEADS, M_HEAD_DIM, M_HEAD_DIM), lambda b, c: (b, 0, 0, 0)),
        pl.BlockSpec((nb, M_HEADS, 1, LANES), lambda b, c: (b, 0, 0, 0)),
        pl.BlockSpec((nb, M_HEADS, 1, LANES), lambda b, c: (b, 0, 0, 0)),
    ]
    in_specs = [
        pl.BlockSpec((nb, M_CHUNK, 2 * M_WIDTH), lambda b, c: (b, chunk(c), 0)),
        pl.BlockSpec((nb, M_CHUNK, M_WIDTH), lambda b, c: (b, chunk(c), COL_V)),
        pl.BlockSpec((nb, M_CHUNK, LANES), lambda b, c: (b, chunk(c), 0)),
        pl.BlockSpec((1, LANES), lambda b, c: (0, 0)),
    ] + state_specs
    args = [qk, px, gates, gate_b, *state]
    t_spec = pl.BlockSpec((nb, M_WIDTH, M_CHUNK), lambda b, c: (b, 0, chunk(c)))
    if final:
        in_specs += [
            t_spec,
            pl.BlockSpec((nb, M_CHUNK, M_WIDTH), lambda b, c: (b, chunk(c), COL_O)),
            pl.BlockSpec((1, M_WIDTH), lambda b, c: (0, 0)),
        ]
        args += [hprev, px, norm_w]
        out_spec = pl.BlockSpec((nb, M_CHUNK, M_WIDTH), lambda b, c: (b, chunk(c), 0))
        out_shape = jax.ShapeDtypeStruct((bsz, seq_len, M_WIDTH), BF16)
    else:
        out_spec = t_spec
        out_shape = jax.ShapeDtypeStruct((bsz, M_WIDTH, seq_len), F32)
    units = nb * M_HEADS
    out, *fin = pl.pallas_call(
        functools.partial(_mlstm_kernel, reverse=reverse, final=final, nb=nb),
        grid=(bsz // nb, n_c),
        in_specs=in_specs,
        out_specs=[out_spec] + state_specs,
        out_shape=[
            out_shape,
            jax.ShapeDtypeStruct((bsz, M_HEADS, M_HEAD_DIM, M_HEAD_DIM), F32),
            jax.ShapeDtypeStruct((bsz, M_HEADS, 1, LANES), F32),
            jax.ShapeDtypeStruct((bsz, M_HEADS, 1, LANES), F32),
        ],
        scratch_shapes=[
            pltpu.VMEM((nb, M_HEADS, M_HEAD_DIM, M_HEAD_DIM), F32),
            pltpu.VMEM((nb, M_HEADS, 1, LANES), F32),
            pltpu.VMEM((nb, M_HEADS, 1, LANES), F32),
            pltpu.VMEM((units, M_CHUNK, M_CHUNK), F32),
            pltpu.VMEM((units, M_HEAD_DIM, M_CHUNK), F32),
            pltpu.VMEM((units, 8, M_CHUNK), F32),
            pltpu.VMEM((units, M_CHUNK, M_CHUNK), BF16),
            pltpu.VMEM((units, M_HEAD_DIM, M_CHUNK), BF16),
            pltpu.VMEM((units, M_HEAD_DIM, M_CHUNK), BF16),
            pltpu.VMEM((units, 1, LANES), F32),
        ],
        compiler_params=_params(("parallel", "arbitrary")),
        name="mlstm_rev" if reverse else "mlstm_fwd",
    )(*args)
    return [out.reshape(t, M_WIDTH) if final else out] + fin


def _mlstm_zero_state(bsz):
    return (jnp.zeros((bsz, M_HEADS, M_HEAD_DIM, M_HEAD_DIM), F32),
            jnp.zeros((bsz, M_HEADS, 1, LANES), F32),
            jnp.zeros((bsz, M_HEADS, 1, LANES), F32))


def _mlstm_bidir(qk, px, gates, gate_b, norm_w, state_f, state_b, seq_len):
    h_f, *fin_f = _mlstm_scan(qk, px, gates, gate_b, state_f, seq_len, False)
    y, *fin_b = _mlstm_scan(qk, px, gates, gate_b, state_b, seq_len, True, hprev=h_f, norm_w=norm_w)
    return y, tuple(fin_f), tuple(fin_b)


NA_ROWS = 8
NA_STEP = 4


def _na_kernel(q_ref, kp_ref, kc_ref, kn_ref, vp_ref, vc_ref, vn_ref, kx_ref, vx_ref, *rest, rows):
    bias_refs = rest[:NA_STEP]
    o_ref, kwin, vwin, s_scr, p_scr = rest[NA_STEP:]
    step = pl.program_id(1)
    blk = NA_ROWS * GRID_W

    @pl.when(step % (NA_ROWS // NA_STEP) == 0)
    def _():
        kwin[0:blk, :] = kp_ref[...]
        kwin[blk:2 * blk, :] = kc_ref[...]
        kwin[2 * blk:3 * blk, :] = kn_ref[...]
        vwin[0:blk, :] = vp_ref[...]
        vwin[blk:2 * blk, :] = vc_ref[...]
        vwin[2 * blk:3 * blk, :] = vn_ref[...]

    g = step // (NA_ROWS // NA_STEP)
    n_loc = WIN_R * GRID_W
    lane = lax.broadcasted_iota(jnp.int32, (GRID_W, LANES), 1)
    scale = NA_HEAD_DIM ** -0.5
    n_ctx = kx_ref.shape[0]
    units = [(kk, hp) for kk in range(NA_STEP) for hp in range(NA_HEADS // 2)]
    offs = []
    for kk in range(NA_STEP):
        r = step * NA_STEP + kk
        r0 = jnp.clip(r - WIN_R // 2, 0, rows - WIN_R)
        offs.append(pl.multiple_of((r0 - NA_ROWS * g + NA_ROWS) * GRID_W, GRID_W))

    for u, (kk, hp) in enumerate(units):
        cols = slice(hp * LANES, (hp + 1) * LANES)
        q2 = q_ref[kk * GRID_W:(kk + 1) * GRID_W, cols] * scale
        zero = jnp.zeros_like(q2)
        qs = jnp.concatenate([jnp.where(lane < NA_HEAD_DIM, q2, zero), jnp.where(lane < NA_HEAD_DIM, zero, q2)],
                             axis=0)
        s_scr[u, :, 0:n_loc] = _dot_nt(qs, kwin[pl.ds(offs[kk], n_loc), cols])
        s_scr[u, :, n_loc:n_loc + n_ctx] = _dot_nt(qs, kx_ref[:, cols])

    for u, (kk, hp) in enumerate(units):
        s_loc = s_scr[u, :, 0:n_loc] + bias_refs[kk][0, hp]
        s_ctx = s_scr[u, :, n_loc:n_loc + n_ctx]
        m = jnp.maximum(jnp.max(s_loc, axis=-1, keepdims=True), jnp.max(s_ctx, axis=-1, keepdims=True))
        p_loc = jnp.exp(s_loc - m)
        p_ctx = jnp.exp(s_ctx - m)
        inv = 1.0 / (jnp.sum(p_loc, axis=-1, keepdims=True) + jnp.sum(p_ctx, axis=-1, keepdims=True))
        p_scr[u, :, 0:n_loc] = (p_loc * inv).astype(BF16)
        p_scr[u, :, n_loc:n_loc + n_ctx] = (p_ctx * inv).astype(BF16)

    for u, (kk, hp) in enumerate(units):
        cols = slice(hp * LANES, (hp + 1) * LANES)
        o2 = _dot(p_scr[u, :, 0:n_loc], vwin[pl.ds(offs[kk], n_loc), cols])
        o2 += _dot(p_scr[u, :, n_loc:n_loc + n_ctx], vx_ref[:, cols])
        out = jnp.where(lane < NA_HEAD_DIM, o2[0:GRID_W], o2[GRID_W:2 * GRID_W])
        o_ref[kk * GRID_W:(kk + 1) * GRID_W, cols] = out.astype(o_ref.dtype)


def _na_bias_table(rpb):
    qc = jnp.arange(GRID_W)[:, None]
    kc = jnp.arange(GRID_W)[None, :]
    cs = jnp.clip(qc - WIN_C // 2, 0, GRID_W - WIN_C)
    ok = (kc >= cs) & (kc < cs + WIN_C)
    dc = jnp.clip(kc - qc + WIN_C - 1, 0, 2 * WIN_C - 2)
    dr = jnp.arange(WIN_R)[:, None] + jnp.arange(WIN_R)[None, :]
    tab = rpb.astype(F32)[:, dr][:, :, :, dc]
    tab = jnp.where(ok[None, None, None], tab, NEG_BIG)
    tab = jnp.transpose(tab, (1, 0, 3, 2, 4))
    return tab.reshape(WIN_R, NA_HEADS // 2, 2 * GRID_W, WIN_R * GRID_W)


def _na_attention(px, pc, bias, seq_len, ctx_len):
    t = px.shape[0]
    bsz = t // seq_len
    rows = seq_len // GRID_W
    n_g = rows // NA_ROWS
    blk = NA_ROWS * GRID_W

    steps = rows // NA_STEP
    per_blk = NA_ROWS // NA_STEP

    def kv_spec(col, shift):
        def imap(b, s):
            g = jnp.clip(s // per_blk + shift, 0, n_g - 1)
            return (b * n_g + g, col)
        return pl.BlockSpec((blk, NA_WIDTH), imap)

    def bias_spec(kk):
        def imap(b, s):
            r = s * NA_STEP + kk
            r0 = jnp.clip(r - WIN_R // 2, 0, rows - WIN_R)
            return (r0 - r + WIN_R - 1, 0, 0, 0)
        return pl.BlockSpec((1, NA_HEADS // 2, 2 * GRID_W, WIN_R * GRID_W), imap)

    return pl.pallas_call(
        functools.partial(_na_kernel, rows=rows),
        grid=(bsz, steps),
        in_specs=[
            pl.BlockSpec((NA_STEP * GRID_W, NA_WIDTH), lambda b, s: (b * steps + s, COL_NAQ)),
            kv_spec(COL_NAK, -1), kv_spec(COL_NAK, 0), kv_spec(COL_NAK, 1),
            kv_spec(COL_NAV, -1), kv_spec(COL_NAV, 0), kv_spec(COL_NAV, 1),
            pl.BlockSpec((ctx_len, NA_WIDTH), lambda b, s: (b, COL_NAK)),
            pl.BlockSpec((ctx_len, NA_WIDTH), lambda b, s: (b, COL_NAV)),
        ] + [bias_spec(kk) for kk in range(NA_STEP)],
        out_specs=pl.BlockSpec((NA_STEP * GRID_W, NA_WIDTH), lambda b, s: (b * steps + s, 0)),
        out_shape=jax.ShapeDtypeStruct((t, NA_WIDTH), BF16),
        scratch_shapes=[
            pltpu.VMEM((3 * blk, NA_WIDTH), BF16), pltpu.VMEM((3 * blk, NA_WIDTH), BF16),
            pltpu.VMEM((NA_STEP * NA_HEADS // 2, 2 * GRID_W, WIN_R * GRID_W + ctx_len), F32),
            pltpu.VMEM((NA_STEP * NA_HEADS // 2, 2 * GRID_W, WIN_R * GRID_W + ctx_len), BF16),
        ],
        compiler_params=_params(("parallel", "arbitrary")),
        name="na_attn",
    )(px, px, px, px, px, px, px, pc, pc, *([bias] * NA_STEP))


def _ctx_attn_kernel(q_ref, k_ref, v_ref, o_ref):
    n = q_ref.shape[0]
    lane = lax.broadcasted_iota(jnp.int32, (n, LANES), 1)
    scale = NA_HEAD_DIM ** -0.5
    for hp in range(NA_HEADS // 2):
        cols = slice(hp * LANES, (hp + 1) * LANES)
        q2 = q_ref[:, cols] * scale
        k2 = k_ref[:, cols]
        v2 = v_ref[:, cols]
        outs = []
        for sub in range(2):
            sel = (lane < NA_HEAD_DIM) if sub == 0 else (lane >= NA_HEAD_DIM)
            s = _dot_nt(jnp.where(sel, q2, jnp.zeros_like(q2)), k2)
            p = jnp.exp(s - jnp.max(s, axis=-1, keepdims=True))
            outs.append(_dot(p.astype(BF16), v2) / jnp.sum(p, axis=-1, keepdims=True))
        o_ref[:, cols] = jnp.where(lane < NA_HEAD_DIM, outs[0], outs[1]).astype(o_ref.dtype)


def _ctx_attention(pc, ctx_len):
    t = pc.shape[0]
    return pl.pallas_call(
        _ctx_attn_kernel,
        grid=(t // ctx_len,),
        in_specs=[pl.BlockSpec((ctx_len, NA_WIDTH), lambda b, col=col: (b, col))
                  for col in (COL_NAQ, COL_NAK, COL_NAV)],
        out_specs=pl.BlockSpec((ctx_len, NA_WIDTH), lambda b: (b, 0)),
        out_shape=jax.ShapeDtypeStruct((t, NA_WIDTH), BF16),
        compiler_params=_params(("parallel",)),
        name="ctx_attn",
    )(pc, pc, pc)


def _cmul(a, b):
    return a[0] * b[0] - a[1] * b[1], a[0] * b[1] + a[1] * b[0]


def _s5_tables(lam_re, lam_im, log_dt, b_re, b_im, c_re, c_im):
    L, G, S, H = S5_CHUNK, S5_GROUPS, S5_STATE, S5_GROUP
    lr = jnp.minimum(lam_re.astype(F32), S5_MAX_RE)
    li = lam_im.astype(F32)
    dt = jnp.exp(log_dt.astype(F32))[..., None]
    zr, zi = lr * dt, li * dt
    d = jnp.arange(L + 1, dtype=F32)[None, :, None, None]
    mag = jnp.exp(zr[:, None] * d)
    pw = (mag * jnp.cos(zi[:, None] * d), mag * jnp.sin(zi[:, None] * d))
    lb = (pw[0][:, 1], pw[1][:, 1])
    den = lr * lr + li * li
    ratio = (((lb[0] - 1.0) * lr + lb[1] * li) / den, (lb[1] * lr - (lb[0] - 1.0) * li) / den)
    bb = _cmul((ratio[0][..., None], ratio[1][..., None]), (b_re.astype(F32), b_im.astype(F32)))
    cc = (c_re.astype(F32), c_im.astype(F32))

    def kern(x):
        cl = _cmul((cc[0][x][None], cc[1][x][None]), (pw[0][x, :L, :, None, :], pw[1][x, :L, :, None, :]))
        return jnp.einsum('dghp,gpj->dghj', cl[0], bb[0][x]) - jnp.einsum('dghp,gpj->dghj', cl[1], bb[1][x])

    def state_in(x, steps):
        p = (pw[0][x][steps][:, :, None, :], pw[1][x][steps][:, :, None, :])
        r = _cmul(p, (jnp.swapaxes(bb[0][x], 1, 2)[None], jnp.swapaxes(bb[1][x], 1, 2)[None]))
        return jnp.transpose(r[0], (1, 0, 2, 3)), jnp.transpose(r[1], (1, 0, 2, 3))

    def state_out(x, steps):
        p = (pw[0][x][steps][:, :, None, :], pw[1][x][steps][:, :, None, :])
        r = _cmul(p, (cc[0][x][None], cc[1][x][None]))
        return jnp.transpose(r[0], (1, 3, 0, 2)), jnp.transpose(r[1], (1, 3, 0, 2))

    steps = jnp.arange(L)
    wf = state_in(0, L - 1 - steps)
    wb = state_in(1, steps)
    vf = state_out(0, steps + 1)
    vb = state_out(1, L - steps)
    Q, A = S5_QUADS, S5_QUAD_GROUPS
    kd = jnp.stack([kern(0), kern(1)]).reshape(2, L, Q, A, H, H)
    kd = jnp.transpose(kd, (0, 2, 1, 5, 3, 4)).reshape(2, Q, L, H, A * H)
    wk = jnp.stack([wf[0], wf[1], wb[0], wb[1]]).reshape(4, Q, A // 2, 2, L, H, S)
    wd = jnp.transpose(wk, (0, 1, 4, 2, 5, 3, 6)).reshape(4, Q, L, (A // 2) * H, 2 * S)
    vk = jnp.stack([vf[0], -vf[1], vb[0], -vb[1]]).reshape(4, Q, A, S, L, H)
    vd = jnp.transpose(vk, (0, 1, 4, 3, 2, 5)).reshape(4, Q, L, S, A * H)
    a_re = pw[0][:, L].reshape(2, 1, G * S)
    a_im = pw[1][:, L].reshape(2, 1, G * S)
    return wd, kd, vd, a_re, a_im


def _s5_operand(u_ref):
    return jnp.concatenate([u_ref[:, s, :] for s in range(S5_CHUNK)], axis=1).astype(BF16)


def _s5_expand_w(wd_ref, w_scr):
    H, half = S5_GROUP, S5_STATE
    w_scr[...] = jnp.zeros_like(w_scr)
    lane = lax.broadcasted_iota(jnp.int32, (H, LANES), 1)
    for k in range(4):
        for s in range(S5_CHUNK):
            for c in range(S5_QUAD_GROUPS // 2):
                blk = wd_ref[k, 0, s, c * H:(c + 1) * H, :]
                cols = slice(k * S5_QSTATE + c * LANES, k * S5_QSTATE + (c + 1) * LANES)
                row = s * LANES + 2 * c * H
                w_scr[row:row + H, cols] = jnp.where(lane < half, blk, 0.0).astype(BF16)
                w_scr[row + H:row + 2 * H, cols] = jnp.where(lane >= half, blk, 0.0).astype(BF16)


def _s5_expand_mv(kd_ref, vd_ref, m_scr, v_scr):
    H, L = S5_GROUP, S5_CHUNK
    rg = lax.broadcasted_iota(jnp.int32, (LANES, LANES), 0) // H
    lg = lax.broadcasted_iota(jnp.int32, (LANES, LANES), 1) // H
    tiles = []
    for x in range(2):
        tiles.append([jnp.where(rg == lg, jnp.concatenate([kd_ref[x, 0, d]] * S5_QUAD_GROUPS, axis=0), 0.0)
                      for d in range(L)])
    for s in range(L):
        for t in range(L):
            blk = tiles[0][t - s] if t > s else (tiles[1][s - t] if s > t else tiles[0][0] + tiles[1][0])
            m_scr[s * LANES:(s + 1) * LANES, t * LANES:(t + 1) * LANES] = blk.astype(BF16)
    lg2 = lax.broadcasted_iota(jnp.int32, (S5_STATE, LANES), 1) // H
    for k in range(4):
        for t in range(L):
            blk = vd_ref[k, 0, t]
            for b in range(S5_QUAD_GROUPS):
                row = k * S5_QSTATE + b * S5_STATE
                v_scr[row:row + S5_STATE, t * LANES:(t + 1) * LANES] = jnp.where(lg2 == b, blk, 0.0).astype(BF16)


def _s5_local_kernel(u_ref, wd_ref, sfr_ref, sfi_ref, sbr_ref, sbi_ref, w_scr):
    @pl.when(pl.program_id(1) == 0)
    def _():
        _s5_expand_w(wd_ref, w_scr)

    s = _dot(_s5_operand(u_ref), w_scr[...])
    n = S5_QSTATE
    sfr_ref[...] = s[:, 0:n]
    sfi_ref[...] = s[:, n:2 * n]
    sbr_ref[...] = s[:, 2 * n:3 * n]
    sbi_ref[...] = s[:, 3 * n:4 * n]


def _s5_local_states(u3, wd):
    rows = u3.shape[0]
    tr = min(256, rows)
    assert rows % tr == 0
    out = jax.ShapeDtypeStruct((rows, S5_GROUPS * S5_STATE), F32)
    return pl.pallas_call(
        _s5_local_kernel,
        grid=(S5_QUADS, rows // tr),
        in_specs=[
            pl.BlockSpec((tr, S5_CHUNK, LANES), lambda q, i: (i, 0, q)),
            pl.BlockSpec((4, 1) + wd.shape[2:], lambda q, i: (0, q, 0, 0, 0)),
        ],
        out_specs=[pl.BlockSpec((tr, S5_QSTATE), lambda q, i: (i, q))] * 4,
        out_shape=[out] * 4,
        scratch_shapes=[pltpu.VMEM((S5_QCOLS, 4 * S5_QSTATE), BF16)],
        compiler_params=_params(("arbitrary", "arbitrary")),
        name="s5_local",
    )(u3, wd)


def _s5_scan_kernel(sr_ref, si_ref, ar_ref, ai_ref, h0r_ref, h0i_ref, hr_ref, hi_ref, fr_ref, fi_ref,
                    hr_scr, hi_scr, *, reverse):
    i = pl.program_id(1)
    n_i = pl.num_programs(1)
    steps = sr_ref.shape[1]

    @pl.when(i == 0)
    def _():
        hr_scr[...] = h0r_ref[...]
        hi_scr[...] = h0i_ref[...]

    ar = ar_ref[...]
    ai = ai_ref[...]

    def body(j, carry):
        hr, hi = carry
        c = (steps - 1 - j) if reverse else j
        hr_ref[:, c, :] = hr
        hi_ref[:, c, :] = hi
        nr = ar * hr - ai * hi + sr_ref[:, c, :]
        ni = ar * hi + ai * hr + si_ref[:, c, :]
        return nr, ni

    hr, hi = lax.fori_loop(0, steps, body, (hr_scr[...], hi_scr[...]))
    hr_scr[...] = hr
    hi_scr[...] = hi

    @pl.when(i == n_i - 1)
    def _():
        fr_ref[...] = hr
        fi_ref[...] = hi


def _s5_chunk_scan(s_re, s_im, a_re, a_im, h0_re, h0_im, bsz, reverse):
    _, n_c, width = s_re.shape
    tr = min(64, n_c)
    tc = 512
    n_i = n_c // tr

    def rmap(j, i):
        return (0, (n_i - 1 - i) if reverse else i, j)

    big = jax.ShapeDtypeStruct((bsz, n_c, width), F32)
    small = jax.ShapeDtypeStruct((bsz, width), F32)
    return pl.pallas_call(
        functools.partial(_s5_scan_kernel, reverse=reverse),
        grid=(width // tc, n_i),
        in_specs=[
            pl.BlockSpec((bsz, tr, tc), rmap), pl.BlockSpec((bsz, tr, tc), rmap),
            pl.BlockSpec((1, tc), lambda j, i: (0, j)), pl.BlockSpec((1, tc), lambda j, i: (0, j)),
            pl.BlockSpec((bsz, tc), lambda j, i: (0, j)), pl.BlockSpec((bsz, tc), lambda j, i: (0, j)),
        ],
        out_specs=[
            pl.BlockSpec((bsz, tr, tc), rmap), pl.BlockSpec((bsz, tr, tc), rmap),
            pl.BlockSpec((bsz, tc), lambda j, i: (0, j)), pl.BlockSpec((bsz, tc), lambda j, i: (0, j)),
        ],
        out_shape=[big, big, small, small],
        scratch_shapes=[pltpu.VMEM((bsz, tc), F32), pltpu.VMEM((bsz, tc), F32)],
        compiler_params=_params(("parallel", "arbitrary")),
        name="s5_scan_rev" if reverse else "s5_scan_fwd",
    )(s_re, s_im, a_re, a_im, h0_re, h0_im)


def _s5_out_kernel(u_ref, kd_ref, vd_ref, hfr_ref, hfi_ref, hbr_ref, hbi_ref, y_ref, m_scr, v_scr):
    @pl.when(pl.program_id(1) == 0)
    def _():
        _s5_expand_mv(kd_ref, vd_ref, m_scr, v_scr)

    n = S5_QSTATE
    y = _dot(_s5_operand(u_ref), m_scr[...])
    for k, h_ref in enumerate((hfr_ref, hfi_ref, hbr_ref, hbi_ref)):
        y += _dot(h_ref[...].astype(BF16), v_scr[k * n:(k + 1) * n, :])
    for t in range(S5_CHUNK):
        y_ref[:, t, :] = y[:, t * LANES:(t + 1) * LANES]


def _s5_outputs(u3, kd, vd, h_states):
    rows = u3.shape[0]
    tr = min(256, rows)
    return pl.pallas_call(
        _s5_out_kernel,
        grid=(S5_QUADS, rows // tr),
        in_specs=[
            pl.BlockSpec((tr, S5_CHUNK, LANES), lambda q, i: (i, 0, q)),
            pl.BlockSpec((2, 1) + kd.shape[2:], lambda q, i: (0, q, 0, 0, 0)),
            pl.BlockSpec((4, 1) + vd.shape[2:], lambda q, i: (0, q, 0, 0, 0)),
        ] + [pl.BlockSpec((tr, S5_QSTATE), lambda q, i: (i, q))] * 4,
        out_specs=pl.BlockSpec((tr, S5_CHUNK, LANES), lambda q, i: (i, 0, q)),
        out_shape=jax.ShapeDtypeStruct((rows, S5_CHUNK, S5_WIDTH), F32),
        scratch_shapes=[pltpu.VMEM((S5_QCOLS, S5_QCOLS), BF16), pltpu.VMEM((4 * S5_QSTATE, S5_QCOLS), BF16)],
        compiler_params=_params(("arbitrary", "arbitrary")),
        name="s5_out",
    )(u3, kd, vd, *h_states)


def _s5_branch(u, tables, seq_len, init, want_y):
    w_q, m_q, v_q, a_re, a_im = tables
    t = u.shape[0]
    bsz = t // seq_len
    n_c = seq_len // S5_CHUNK
    width = S5_GROUPS * S5_STATE
    u3 = u.reshape(t // S5_CHUNK, S5_CHUNK, S5_WIDTH)
    s_loc = [s.reshape(bsz, n_c, width) for s in _s5_local_states(u3, w_q)]
    hfr, hfi, ffr, ffi = _s5_chunk_scan(s_loc[0], s_loc[1], a_re[0], a_im[0], init[0], init[1], bsz, False)
    hbr, hbi, fbr, fbi = _s5_chunk_scan(s_loc[2], s_loc[3], a_re[1], a_im[1], init[2], init[3], bsz, True)
    y = None
    if want_y:
        h_states = [h.reshape(bsz * n_c, width) for h in (hfr, hfi, hbr, hbi)]
        y = _s5_outputs(u3, m_q, v_q, h_states).reshape(t, S5_WIDTH)
    return y, (ffr, ffi, fbr, fbi)


def _gelu_tanh(x):
    return 0.5 * x * (1.0 + jnp.tanh(math.sqrt(2.0 / math.pi) * (x + 0.044715 * (x * x * x))))


def _merge_kernel(x_ref, ym_ref, yn_ref, ys_ref, u_ref, gm_ref, gn_ref, gs_ref, d_ref, gw_ref, gb_ref,
                  wm_ref, wn_ref, ws_ref, wo_ref, gate_ref, npost_ref, o_ref):
    ys = ys_ref[...] + d_ref[...] * u_ref[...]
    g = _gelu_tanh(ys)
    gg = g * jax.nn.sigmoid(_dot(g.astype(BF16), gw_ref[...]) + gb_ref[...])
    y = jax.nn.sigmoid(gm_ref[...].astype(F32)) * _dot(ym_ref[...], wm_ref[...])
    y += jax.nn.sigmoid(gn_ref[...].astype(F32)) * _dot(yn_ref[...], wn_ref[...])
    y += jax.nn.sigmoid(gs_ref[...].astype(F32)) * _dot(gg.astype(BF16), ws_ref[...])
    out = _dot(y.astype(BF16), wo_ref[...])
    o_ref[...] = x_ref[...] + gate_ref[0] * _rms(out, npost_ref[...])


def _merge(x, ym, yn, ys, u, px, wts, gate, npost, rows_per_mod):
    t = x.shape[0]
    tm = min(512, rows_per_mod)
    per = rows_per_mod // tm
    s5_d, glu_w, glu_b, w_m, w_n, w_s, w_o = wts
    row = lambda i: (i, 0)
    const = lambda i: (0, 0)
    return pl.pallas_call(
        _merge_kernel,
        grid=(t // tm,),
        in_specs=[
            pl.BlockSpec((tm, D_MODEL), row),
            pl.BlockSpec((tm, M_WIDTH), row),
            pl.BlockSpec((tm, NA_WIDTH), row),
            pl.BlockSpec((tm, S5_WIDTH), row),
            pl.BlockSpec((tm, S5_WIDTH), row),
            pl.BlockSpec((tm, D_MODEL), lambda i: (i, COL_GATE)),
            pl.BlockSpec((tm, D_MODEL), lambda i: (i, COL_GATE + 1)),
            pl.BlockSpec((tm, D_MODEL), lambda i: (i, COL_GATE + 2)),
            pl.BlockSpec((1, S5_WIDTH), const),
            pl.BlockSpec((S5_WIDTH, S5_WIDTH), const),
            pl.BlockSpec((1, S5_WIDTH), const),
            pl.BlockSpec((M_WIDTH, D_MODEL), const),
            pl.BlockSpec((NA_WIDTH, D_MODEL), const),
            pl.BlockSpec((S5_WIDTH, D_MODEL), const),
            pl.BlockSpec((D_MODEL, D_MODEL), const),
            pl.BlockSpec((1, 1, D_MODEL), lambda i: (i // per, 0, 0)),
            pl.BlockSpec((1, D_MODEL), const),
        ],
        out_specs=pl.BlockSpec((tm, D_MODEL), row),
        out_shape=jax.ShapeDtypeStruct((t, D_MODEL), F32),
        compiler_params=_params(("parallel",)),
        name="merge",
    )(x, ym, yn, ys, u, px, px, px, s5_d, glu_w, glu_b, w_m, w_n, w_s, w_o, gate, npost)


def _swiglu(h, wg, wu, wd):
    a = _dot(h, wg)
    a = a * jax.nn.sigmoid(a) * _dot(h, wu)
    return _dot(a.astype(BF16), wd)


def _ffn_kernel(x_ref, g_ref, sh_ref, sc_ref, wg_ref, wu_ref, wd_ref, gate_ref, npost_ref,
                o_ref, hn_ref, acc_ref):
    j = pl.program_id(1)

    @pl.when(j == 0)
    def _():
        h = _rms(x_ref[...], g_ref[...]) * (1.0 + sc_ref[0]) + sh_ref[0]
        hn_ref[...] = h.astype(BF16)
        acc_ref[...] = jnp.zeros_like(acc_ref)

    acc_ref[...] += _swiglu(hn_ref[...], wg_ref[...], wu_ref[...], wd_ref[...])

    @pl.when(j == pl.num_programs(1) - 1)
    def _():
        o_ref[...] = x_ref[...] + gate_ref[0] * _rms(acc_ref[...], npost_ref[...])


def _ffn(x, g, shift, scale, w_gate, w_up, w_down, gate, npost, rows_per_mod):
    t = x.shape[0]
    ff = w_gate.shape[1]
    ff_tile = ff // 2
    tm = min(512, rows_per_mod)
    per = rows_per_mod // tm
    assert ff_tile % LANES == 0
    return pl.pallas_call(
        _ffn_kernel,
        grid=(t // tm, ff // ff_tile),
        in_specs=[
            pl.BlockSpec((tm, D_MODEL), lambda i, j: (i, 0)),
            pl.BlockSpec((1, D_MODEL), lambda i, j: (0, 0)),
            pl.BlockSpec((1, 1, D_MODEL), lambda i, j: (i // per, 0, 0)),
            pl.BlockSpec((1, 1, D_MODEL), lambda i, j: (i // per, 0, 0)),
            pl.BlockSpec((D_MODEL, ff_tile), lambda i, j: (0, j)),
            pl.BlockSpec((D_MODEL, ff_tile), lambda i, j: (0, j)),
            pl.BlockSpec((ff_tile, D_MODEL), lambda i, j: (j, 0)),
            pl.BlockSpec((1, 1, D_MODEL), lambda i, j: (i // per, 0, 0)),
            pl.BlockSpec((1, D_MODEL), lambda i, j: (0, 0)),
        ],
        out_specs=pl.BlockSpec((tm, D_MODEL), lambda i, j: (i, 0)),
        out_shape=jax.ShapeDtypeStruct((t, D_MODEL), F32),
        scratch_shapes=[pltpu.VMEM((tm, D_MODEL), BF16), pltpu.VMEM((tm, D_MODEL), F32)],
        compiler_params=_params(("parallel", "arbitrary")),
        name="ffn",
    )(x, g, shift, scale, w_gate, w_up, w_down, gate, npost)


def _moe_kernel(x_ref, g_ref, sh_ref, sc_ref, rh_ref, rl_ref, wg_ref, wu_ref, wd_ref, gate_ref, npost_ref,
                o_ref, hn_ref, acc_ref, comb_ref, rank_c_ref, rank_r_ref, mask_r_ref, xc_ref, yc_ref, cnt_ref):
    e = pl.program_id(1)
    j = pl.program_id(2)
    tm = x_ref.shape[0]
    cap = MOE_CAP

    @pl.when((e == 0) & (j == 0))
    def _():
        h = _rms(x_ref[...], g_ref[...]) * (1.0 + sc_ref[0]) + sh_ref[0]
        hi = h.astype(BF16)
        hn_ref[...] = hi
        acc_ref[...] = jnp.zeros_like(acc_ref)
        lo = (h - hi.astype(F32)).astype(BF16)
        logits = _dot(hi, rh_ref[...]) + _dot(lo, rh_ref[...]) + _dot(hi, rl_ref[...])
        lane = lax.broadcasted_iota(jnp.int32, logits.shape, 1)
        logits = jnp.where(lane < N_EXPERTS, logits, -jnp.inf)
        m1 = jnp.max(logits, axis=-1, keepdims=True)
        i1 = jnp.min(jnp.where(logits == m1, lane, LANES), axis=-1, keepdims=True)
        rest = jnp.where(lane == i1, -jnp.inf, logits)
        m2 = jnp.max(rest, axis=-1, keepdims=True)
        i2 = jnp.min(jnp.where(rest == m2, lane, LANES), axis=-1, keepdims=True)
        e2 = jnp.exp(m2 - m1)
        comb_ref[...] = jnp.where(lane == i1, 1.0 / (1.0 + e2), 0.0) + jnp.where(lane == i2, e2 / (1.0 + e2), 0.0)
        sel = jnp.where((lane == i1) | (lane == i2), 1.0, 0.0)
        ti = lax.broadcasted_iota(jnp.int32, (tm, tm), 0)
        si = lax.broadcasted_iota(jnp.int32, (tm, tm), 1)
        before = jnp.where(si < ti, 1.0, 0.0).astype(BF16)
        rank_c_ref[...] = _dot(before, sel.astype(BF16))
        sel_t = sel.T
        after = jnp.where(ti < si, 1.0, 0.0).astype(BF16)
        rank_r_ref[...] = _dot(sel_t.astype(BF16), after)
        mask_r_ref[...] = sel_t
        counts = jnp.sum(sel_t, axis=1, keepdims=True)
        for k in range(N_EXPERTS):
            cnt_ref[k] = jnp.sum(counts[k:k + 1, :]).astype(jnp.int32)

    n_pass = (cnt_ref[e] + (cap - 1)) // cap

    @pl.when(j == 0)
    def _():
        rank_row = rank_r_ref[pl.ds(e, 1), :]
        mask_row = mask_r_ref[pl.ds(e, 1), :]
        slot = lax.broadcasted_iota(jnp.int32, (cap, tm), 0).astype(F32)

        def compact(p, _):
            base = (p * cap).astype(F32)
            onehot = jnp.where((rank_row - base == slot) & (mask_row > 0.0), 1.0, 0.0).astype(BF16)
            rows = pl.ds(pl.multiple_of(p * cap, 16), cap)
            xc_ref[rows, :] = _dot(onehot, hn_ref[...]).astype(BF16)
            yc_ref[rows, :] = jnp.zeros((cap, D_MODEL), F32)
            return 0

        lax.fori_loop(0, n_pass, compact, 0)

    def expert(start, size):
        rows = pl.ds(pl.multiple_of(start, MOE_GRAIN), size)
        yc_ref[rows, :] += _swiglu(xc_ref[rows, :], wg_ref[0], wu_ref[0], wd_ref[0])

    n_tok = cnt_ref[e]
    n_full = n_tok // cap
    rem_blocks = (n_tok - n_full * cap + (MOE_GRAIN - 1)) // MOE_GRAIN
    n_full = n_full + (rem_blocks == cap // MOE_GRAIN).astype(jnp.int32)

    def full_pass(p, _):
        expert(p * cap, cap)
        return 0

    lax.fori_loop(0, n_full, full_pass, 0)
    for blocks in range(1, cap // MOE_GRAIN):
        @pl.when(rem_blocks == blocks)
        def _(blocks=blocks):
            expert(n_full * cap, blocks * MOE_GRAIN)

    @pl.when(j == pl.num_programs(2) - 1)
    def _():
        lane = lax.broadcasted_iota(jnp.int32, (tm, LANES), 1)
        pick = lane == e
        rank_col = jnp.sum(jnp.where(pick, rank_c_ref[...], 0.0), axis=-1, keepdims=True)
        w_col = jnp.sum(jnp.where(pick, comb_ref[...], 0.0), axis=-1, keepdims=True)
        slot = lax.broadcasted_iota(jnp.int32, (tm, cap), 1).astype(F32)

        def expand(p, _):
            base = (p * cap).astype(F32)
            onehot = jnp.where(rank_col - base == slot, 1.0, 0.0).astype(BF16)
            rows = pl.ds(pl.multiple_of(p * cap, 16), cap)
            acc_ref[...] += w_col * _dot(onehot, yc_ref[rows, :].astype(BF16))
            return 0

        lax.fori_loop(0, n_pass, expand, 0)

    @pl.when((e == N_EXPERTS - 1) & (j == pl.num_programs(2) - 1))
    def _():
        o_ref[...] = x_ref[...] + gate_ref[0] * _rms(acc_ref[...], npost_ref[...])


def _moe(x, g, shift, scale, r_hi, r_lo, w_gate, w_up, w_down, gate, npost, rows_per_mod):
    t = x.shape[0]
    n_e, _, ff = w_gate.shape
    assert n_e == N_EXPERTS
    ff_tile = ff // 4
    tm = min(MOE_TILE, rows_per_mod)
    per = rows_per_mod // tm
    passes = -(-tm // MOE_CAP)
    assert ff_tile % LANES == 0 and MOE_CAP % MOE_GRAIN == 0 and MOE_GRAIN % 16 == 0
    return pl.pallas_call(
        _moe_kernel,
        grid=(t // tm, n_e, ff // ff_tile),
        in_specs=[
            pl.BlockSpec((tm, D_MODEL), lambda i, e, j: (i, 0)),
            pl.BlockSpec((1, D_MODEL), lambda i, e, j: (0, 0)),
            pl.BlockSpec((1, 1, D_MODEL), lambda i, e, j: (i // per, 0, 0)),
            pl.BlockSpec((1, 1, D_MODEL), lambda i, e, j: (i // per, 0, 0)),
            pl.BlockSpec((D_MODEL, LANES), lambda i, e, j: (0, 0)),
            pl.BlockSpec((D_MODEL, LANES), lambda i, e, j: (0, 0)),
            pl.BlockSpec((1, D_MODEL, ff_tile), lambda i, e, j: (e, 0, j)),
            pl.BlockSpec((1, D_MODEL, ff_tile), lambda i, e, j: (e, 0, j)),
            pl.BlockSpec((1, ff_tile, D_MODEL), lambda i, e, j: (e, j, 0)),
            pl.BlockSpec((1, 1, D_MODEL), lambda i, e, j: (i // per, 0, 0)),
            pl.BlockSpec((1, D_MODEL), lambda i, e, j: (0, 0)),
        ],
        out_specs=pl.BlockSpec((tm, D_MODEL), lambda i, e, j: (i, 0)),
        out_shape=jax.ShapeDtypeStruct((t, D_MODEL), F32),
        scratch_shapes=[
            pltpu.VMEM((tm, D_MODEL), BF16),
            pltpu.VMEM((tm, D_MODEL), F32),
            pltpu.VMEM((tm, LANES), F32),
            pltpu.VMEM((tm, LANES), F32),
            pltpu.VMEM((LANES, tm), F32),
            pltpu.VMEM((LANES, tm), F32),
            pltpu.VMEM((passes * MOE_CAP, D_MODEL), BF16),
            pltpu.VMEM((passes * MOE_CAP, D_MODEL), F32),
            pltpu.SMEM((N_EXPERTS,), jnp.int32),
        ],
        compiler_params=_params(("parallel", "arbitrary", "arbitrary")),
        name="moe",
    )(x, g, shift, scale, r_hi, r_lo, w_gate, w_up, w_down, gate, npost)


MOE_ROWS = MOE_TILE * TOP_K + N_EXPERTS * MOE_GRAIN
MOE_META = 2 * N_EXPERTS
MOE_PASS_ROWS = 256


def _moe_dispatch_kernel(x_ref, g_ref, sh_ref, sc_ref, rh_ref, rl_ref, xs_ref, route_ref, meta_ref):
    tm = x_ref.shape[0]
    h = _rms(x_ref[...], g_ref[...]) * (1.0 + sc_ref[0]) + sh_ref[0]
    hi = h.astype(BF16)
    lo = (h - hi.astype(F32)).astype(BF16)
    logits = _dot(hi, rh_ref[...]) + _dot(lo, rh_ref[...]) + _dot(hi, rl_ref[...])
    lane = lax.broadcasted_iota(jnp.int32, logits.shape, 1)
    logits = jnp.where(lane < N_EXPERTS, logits, -jnp.inf)
    m1 = jnp.max(logits, axis=-1, keepdims=True)
    i1 = jnp.min(jnp.where(logits == m1, lane, LANES), axis=-1, keepdims=True)
    rest = jnp.where(lane == i1, -jnp.inf, logits)
    m2 = jnp.max(rest, axis=-1, keepdims=True)
    i2 = jnp.min(jnp.where(rest == m2, lane, LANES), axis=-1, keepdims=True)
    e2 = jnp.exp(m2 - m1)
    sel1 = jnp.where(lane == i1, 1.0, 0.0)
    sel2 = jnp.where(lane == i2, 1.0, 0.0)
    sel = sel1 + sel2
    sel1_t, sel2_t = sel1.T, sel2.T
    sel_t = sel1_t + sel2_t

    ti = lax.broadcasted_iota(jnp.int32, (tm, tm), 0)
    si = lax.broadcasted_iota(jnp.int32, (tm, tm), 1)
    rank_c = _dot(jnp.where(si < ti, 1.0, 0.0).astype(BF16), sel.astype(BF16))
    rank_r = _dot(sel_t.astype(BF16), jnp.where(ti < si, 1.0, 0.0).astype(BF16))
    blk_row = jnp.ceil(jnp.sum(sel, axis=0, keepdims=True) * (1.0 / MOE_GRAIN))
    blk_col = jnp.ceil(jnp.sum(sel_t, axis=1, keepdims=True) * (1.0 / MOE_GRAIN))
    ea = lax.broadcasted_iota(jnp.int32, (LANES, LANES), 0)
    eb = lax.broadcasted_iota(jnp.int32, (LANES, LANES), 1)
    off_row = _dot(jnp.broadcast_to(blk_row, (8, LANES)).astype(BF16),
                   jnp.where(ea < eb, 1.0, 0.0).astype(BF16))[0:1, :]
    off_col = _dot(jnp.where(eb < ea, 1.0, 0.0).astype(BF16),
                   jnp.broadcast_to(blk_col, (LANES, LANES)).astype(BF16))[:, 0:1]

    pos_c = off_row * MOE_GRAIN + rank_c
    pos1_c = jnp.sum(jnp.where(lane == i1, pos_c, 0.0), axis=-1, keepdims=True)
    pos2_c = jnp.sum(jnp.where(lane == i2, pos_c, 0.0), axis=-1, keepdims=True)
    route_ref[...] = (jnp.where(lane == 0, pos1_c, 0.0) + jnp.where(lane == 1, pos2_c, 0.0)
                      + jnp.where(lane == 2, 1.0 / (1.0 + e2), 0.0) + jnp.where(lane == 3, e2 / (1.0 + e2), 0.0))
    pos_r = off_col * MOE_GRAIN + rank_r
    pos1_r = jnp.sum(sel1_t * pos_r, axis=0, keepdims=True)
    pos2_r = jnp.sum(sel2_t * pos_r, axis=0, keepdims=True)
    for c in range(MOE_ROWS // MOE_CAP):
        slot = (lax.broadcasted_iota(jnp.int32, (MOE_CAP, tm), 0) + c * MOE_CAP).astype(F32)
        onehot = jnp.where(slot == pos1_r, 1.0, jnp.where(slot == pos2_r, 1.0, 0.0)).astype(BF16)
        xs_ref[c * MOE_CAP:(c + 1) * MOE_CAP, :] = _dot(onehot, hi).astype(BF16)

    lane8 = lax.broadcasted_iota(jnp.int32, (8, LANES), 1)
    blk8 = jnp.broadcast_to(blk_row, (8, LANES))
    off8 = pltpu.roll(jnp.broadcast_to(off_row, (8, LANES)), N_EXPERTS, 1)
    meta = jnp.where(lane8 < N_EXPERTS, blk8, jnp.where(lane8 < MOE_META, off8, 0.0))
    meta_ref[0] = meta.astype(jnp.int32)


def _moe_dispatch(x, g, shift, scale, r_hi, r_lo, rows_per_mod):
    t = x.shape[0]
    tm = MOE_TILE
    assert rows_per_mod % tm == 0
    per = rows_per_mod // tm
    n_t = t // tm
    return pl.pallas_call(
        _moe_dispatch_kernel,
        grid=(n_t,),
        in_specs=[
            pl.BlockSpec((tm, D_MODEL), lambda i: (i, 0)),
            pl.BlockSpec((1, D_MODEL), lambda i: (0, 0)),
            pl.BlockSpec((1, 1, D_MODEL), lambda i: (i // per, 0, 0)),
            pl.BlockSpec((1, 1, D_MODEL), lambda i: (i // per, 0, 0)),
            pl.BlockSpec((D_MODEL, LANES), lambda i: (0, 0)),
            pl.BlockSpec((D_MODEL, LANES), lambda i: (0, 0)),
        ],
        out_specs=[
            pl.BlockSpec((MOE_ROWS, D_MODEL), lambda i: (i, 0)),
            pl.BlockSpec((tm, LANES), lambda i: (i, 0)),
            pl.BlockSpec((1, 8, LANES), lambda i: (i, 0, 0)),
        ],
        out_shape=[
            jax.ShapeDtypeStruct((n_t * MOE_ROWS, D_MODEL), BF16),
            jax.ShapeDtypeStruct((t, LANES), F32),
            jax.ShapeDtypeStruct((n_t, 8, LANES), jnp.int32),
        ],
        compiler_params=_params(("parallel",)),
        name="moe_dispatch",
    )(x, g, shift, scale, r_hi, r_lo)


def _moe_experts_kernel(meta_ref, xs_ref, wg_ref, wu_ref, wd_ref, out_ref, *, ff_tile):
    e = pl.program_id(0)
    i = pl.program_id(1)
    out_ref[...] = xs_ref[...]
    n_blk = meta_ref[i * MOE_META + e]
    first = meta_ref[i * MOE_META + N_EXPERTS + e] * MOE_GRAIN
    per_pass = MOE_PASS_ROWS // MOE_GRAIN
    n_full = n_blk // per_pass
    rem = n_blk - n_full * per_pass
    ff = wg_ref.shape[2]

    def run(start, size):
        rows = pl.ds(pl.multiple_of(start, MOE_GRAIN), size)
        xin = xs_ref[rows, :]
        y = jnp.zeros((size, D_MODEL), F32)
        for j in range(ff // ff_tile):
            cols = slice(j * ff_tile, (j + 1) * ff_tile)
            y += _swiglu(xin, wg_ref[0, :, cols], wu_ref[0, :, cols], wd_ref[0, cols, :])
        out_ref[rows, :] = y.astype(out_ref.dtype)

    def full_pass(p, _):
        run(first + p * MOE_PASS_ROWS, MOE_PASS_ROWS)
        return 0

    lax.fori_loop(0, n_full, full_pass, 0)
    for blocks in range(1, per_pass):
        @pl.when(rem == blocks)
        def _(blocks=blocks):
            run(first + n_full * MOE_PASS_ROWS, blocks * MOE_GRAIN)


MOE_FF_TILE = 512


def _moe_experts(xs, meta, w_gate, w_up, w_down):
    n_e, _, ff = w_gate.shape
    n_t = xs.shape[0] // MOE_ROWS
    assert ff % MOE_FF_TILE == 0
    resident = pl.Buffered(1)
    grid_spec = pltpu.PrefetchScalarGridSpec(
        num_scalar_prefetch=1,
        grid=(n_e, n_t),
        in_specs=[
            pl.BlockSpec((MOE_ROWS, D_MODEL), lambda e, i, m: (i, 0)),
            pl.BlockSpec((1, D_MODEL, ff), lambda e, i, m: (e, 0, 0), pipeline_mode=resident),
            pl.BlockSpec((1, D_MODEL, ff), lambda e, i, m: (e, 0, 0), pipeline_mode=resident),
            pl.BlockSpec((1, ff, D_MODEL), lambda e, i, m: (e, 0, 0), pipeline_mode=resident),
        ],
        out_specs=pl.BlockSpec((MOE_ROWS, D_MODEL), lambda e, i, m: (i, 0)),
    )
    return pl.pallas_call(
        functools.partial(_moe_experts_kernel, ff_tile=MOE_FF_TILE),
        grid_spec=grid_spec,
        out_shape=jax.ShapeDtypeStruct(xs.shape, xs.dtype),
        input_output_aliases={1: 0},
        compiler_params=_params(("arbitrary", "arbitrary")),
        name="moe_experts",
    )(meta, xs, w_gate, w_up, w_down)


def _moe_combine_kernel(ys_ref, route_ref, x_ref, gate_ref, npost_ref, o_ref):
    tm = x_ref.shape[0]
    lane = lax.broadcasted_iota(jnp.int32, (tm, LANES), 1)
    route = route_ref[...]
    pos1, pos2, w1, w2 = (jnp.sum(jnp.where(lane == k, route, 0.0), axis=-1, keepdims=True) for k in range(4))
    width = 512
    acc = jnp.zeros((tm, D_MODEL), F32)
    for c in range(MOE_ROWS // width):
        slot = (lax.broadcasted_iota(jnp.int32, (tm, width), 1) + c * width).astype(F32)
        wmat = jnp.where(slot == pos1, w1, 0.0) + jnp.where(slot == pos2, w2, 0.0)
        acc += _dot(wmat.astype(BF16), ys_ref[c * width:(c + 1) * width, :])
    o_ref[...] = x_ref[...] + gate_ref[0] * _rms(acc, npost_ref[...])


def _moe_combine(ys, route, x, gate, npost, rows_per_mod):
    t = x.shape[0]
    tm = MOE_TILE
    per = rows_per_mod // tm
    return pl.pallas_call(
        _moe_combine_kernel,
        grid=(t // tm,),
        in_specs=[
            pl.BlockSpec((MOE_ROWS, D_MODEL), lambda i: (i, 0)),
            pl.BlockSpec((tm, LANES), lambda i: (i, 0)),
            pl.BlockSpec((tm, D_MODEL), lambda i: (i, 0)),
            pl.BlockSpec((1, 1, D_MODEL), lambda i: (i // per, 0, 0)),
            pl.BlockSpec((1, D_MODEL), lambda i: (0, 0)),
        ],
        out_specs=pl.BlockSpec((tm, D_MODEL), lambda i: (i, 0)),
        out_shape=jax.ShapeDtypeStruct((t, D_MODEL), F32),
        compiler_params=_params(("parallel",)),
        name="moe_combine",
    )(ys, route, x, gate, npost)


MOE_BLOCK = 512
MOE_DMA_WINDOW = 64


def _moe_plan(meta, n_t):
    per_tile = MOE_ROWS // MOE_GRAIN
    per_blk = MOE_BLOCK // MOE_GRAIN
    nblk = meta[:, 0, :N_EXPERTS]
    first = meta[:, 0, N_EXPERTS:MOE_META]
    tot = jnp.sum(nblk, axis=0)
    tot_pad = ((tot + per_blk - 1) // per_blk) * per_blk
    ends = jnp.cumsum(tot_pad)
    seg_dst = (ends - tot_pad)[None, :] + jnp.cumsum(nblk, axis=0) - nblk
    k = jnp.arange(per_tile, dtype=jnp.int32)[None, :, None]
    inside = (k >= first[:, None, :]) & (k < (first + nblk)[:, None, :])
    dst = jnp.sum(jnp.where(inside, seg_dst[:, None, :] + k - first[:, None, :], 0), axis=-1)
    used = jnp.any(inside, axis=-1).reshape(-1)
    dst = dst.reshape(-1).astype(jnp.int32)
    n_big = (n_t * per_tile + N_EXPERTS * per_blk) // per_blk
    big = jnp.arange(n_big, dtype=jnp.int32)[:, None] * per_blk
    expert = jnp.minimum(jnp.sum(big >= ends[None, :], axis=-1), N_EXPERTS - 1).astype(jnp.int32)
    valid = (big[:, 0] < ends[-1]).astype(jnp.int32)
    src = jnp.zeros((n_big * per_blk,), jnp.int32).at[jnp.where(used, dst, n_big * per_blk)].set(
        jnp.arange(dst.shape[0], dtype=jnp.int32), mode='drop')
    return jnp.where(used, dst, 0), jnp.concatenate([expert, valid, src]), n_big


def _moe_global_kernel(plan_ref, *refs, ff_tile):
    per_blk = MOE_BLOCK // MOE_GRAIN
    x_refs = refs[:per_blk]
    wg_ref, wu_ref, wd_ref, out_ref, xin_scr = refs[per_blk:]
    b = pl.program_id(0)
    valid = plan_ref[pl.num_programs(0) + b]
    ff = wg_ref.shape[2]

    @pl.when(valid == 1)
    def _():
        for k in range(per_blk):
            xin_scr[k * MOE_GRAIN:(k + 1) * MOE_GRAIN, :] = x_refs[k][...]
        xin = xin_scr[...]
        y = jnp.zeros((MOE_BLOCK, D_MODEL), F32)
        for j in range(ff // ff_tile):
            cols = slice(j * ff_tile, (j + 1) * ff_tile)
            y += _swiglu(xin, wg_ref[0, :, cols], wu_ref[0, :, cols], wd_ref[0, cols, :])
        out_ref[...] = y.astype(out_ref.dtype)

    @pl.when(valid == 0)
    def _():
        out_ref[...] = jnp.zeros_like(out_ref)


def _moe_experts_global(xs, plan, n_big, w_gate, w_up, w_down):
    n_e, _, ff = w_gate.shape
    per_blk = MOE_BLOCK // MOE_GRAIN
    resident = pl.Buffered(1)

    def src_spec(k):
        return pl.BlockSpec((MOE_GRAIN, D_MODEL), lambda b, p: (p[2 * n_big + b * per_blk + k], 0))

    grid_spec = pltpu.PrefetchScalarGridSpec(
        num_scalar_prefetch=1,
        grid=(n_big,),
        in_specs=[src_spec(k) for k in range(per_blk)] + [
            pl.BlockSpec((1, D_MODEL, ff), lambda b, p: (p[b], 0, 0), pipeline_mode=resident),
            pl.BlockSpec((1, D_MODEL, ff), lambda b, p: (p[b], 0, 0), pipeline_mode=resident),
            pl.BlockSpec((1, ff, D_MODEL), lambda b, p: (p[b], 0, 0), pipeline_mode=resident),
        ],
        out_specs=pl.BlockSpec((MOE_BLOCK, D_MODEL), lambda b, p: (b, 0)),
        scratch_shapes=[pltpu.VMEM((MOE_BLOCK, D_MODEL), BF16)],
    )
    return pl.pallas_call(
        functools.partial(_moe_global_kernel, ff_tile=MOE_FF_TILE),
        grid_spec=grid_spec,
        out_shape=jax.ShapeDtypeStruct((n_big * MOE_BLOCK, D_MODEL), BF16),
        compiler_params=_params(("arbitrary",)),
        name="moe_experts",
    )(plan, *([xs] * per_blk), w_gate, w_up, w_down)


def _moe_global(x, g, shift, scale, r_hi, r_lo, w_gate, w_up, w_down, gate, npost, rows_per_mod):
    xs, route, meta = _moe_dispatch(x, g, shift, scale, r_hi, r_lo, rows_per_mod)
    n_t = xs.shape[0] // MOE_ROWS
    block_map, plan, n_big = _moe_plan(meta, n_t)
    yg = _moe_experts_global(xs, plan, n_big, w_gate, w_up, w_down)
    return _moe_gather_combine(yg, block_map, route, x, gate, npost, rows_per_mod)


def _moe_gather_combine_kernel(map_ref, *refs):
    per_tile = MOE_ROWS // MOE_GRAIN
    y_refs = refs[:per_tile]
    route_ref, x_ref, gate_ref, npost_ref, o_ref, ys_scr = refs[per_tile:]
    del map_ref
    for k in range(per_tile):
        ys_scr[k * MOE_GRAIN:(k + 1) * MOE_GRAIN, :] = y_refs[k][...]
    _moe_combine_kernel(ys_scr, route_ref, x_ref, gate_ref, npost_ref, o_ref)


def _moe_gather_combine(yg, block_map, route, x, gate, npost, rows_per_mod):
    t = x.shape[0]
    tm = MOE_TILE
    per = rows_per_mod // tm
    per_tile = MOE_ROWS // MOE_GRAIN

    def y_spec(k):
        return pl.BlockSpec((MOE_GRAIN, D_MODEL), lambda i, m: (m[i * per_tile + k], 0))

    grid_spec = pltpu.PrefetchScalarGridSpec(
        num_scalar_prefetch=1,
        grid=(t // tm,),
        in_specs=[y_spec(k) for k in range(per_tile)] + [
            pl.BlockSpec((tm, LANES), lambda i, m: (i, 0)),
            pl.BlockSpec((tm, D_MODEL), lambda i, m: (i, 0)),
            pl.BlockSpec((1, 1, D_MODEL), lambda i, m: (i // per, 0, 0)),
            pl.BlockSpec((1, D_MODEL), lambda i, m: (0, 0)),
        ],
        out_specs=pl.BlockSpec((tm, D_MODEL), lambda i, m: (i, 0)),
        scratch_shapes=[pltpu.VMEM((MOE_ROWS, D_MODEL), BF16)],
    )
    return pl.pallas_call(
        _moe_gather_combine_kernel,
        grid_spec=grid_spec,
        out_shape=jax.ShapeDtypeStruct((t, D_MODEL), F32),
        compiler_params=_params(("arbitrary",)),
        name="moe_combine",
    )(block_map, *([yg] * per_tile), route, x, gate, npost)


def _moe_sorted(x, g, shift, scale, r_hi, r_lo, w_gate, w_up, w_down, gate, npost, rows_per_mod):
    xs, route, meta = _moe_dispatch(x, g, shift, scale, r_hi, r_lo, rows_per_mod)
    meta = meta[:, 0, :MOE_META].reshape(-1)
    ys = _moe_experts(xs, meta, w_gate, w_up, w_down)
    return _moe_combine(ys, route, x, gate, npost, rows_per_mod)


def _reorder_w_in(w):
    o = 4 * M_WIDTH
    gates = jnp.pad(w[:, o:o + M_GATES], ((0, 0), (0, LANES - M_GATES)))
    na = w[:, o + M_GATES:o + M_GATES + 3 * NA_WIDTH]
    u0 = o + M_GATES + 3 * NA_WIDTH
    main = jnp.concatenate([w[:, :o], w[:, u0 + S5_WIDTH:], na], axis=1)
    return main.astype(BF16), gates.astype(BF16), w[:, u0:u0 + S5_WIDTH].astype(BF16)


def _row(v):
    return v.reshape(1, -1).astype(F32)


def _token_mixers(x, xc, mx, mc, lp, rope, seq_len, ctx_len, ctx_out):
    bsz = x.shape[0] // seq_len
    w_main, w_gates, w_u = _reorder_w_in(lp['w_in'])
    g_pre = _row(lp['norm_mix_pre'])
    px, gx, ux = _inproj(x, g_pre, mx[0], mx[1], w_main, w_gates, w_u, seq_len)
    pc, gc, uc = _inproj(xc, g_pre, mc[0], mc[1], w_main, w_gates, w_u, xc.shape[0])

    conv_w = lp['m_conv_w'].astype(F32)
    conv_b = _row(lp['m_conv_b'])
    gate_b = jnp.pad(_row(lp['m_gate_b']), ((0, 0), (0, LANES - M_GATES)))
    m_norm = _row(lp['m_norm'])
    qk_c = _qkconv(pc, conv_w, conv_b, rope[0][:ctx_len], rope[1][:ctx_len], ctx_len, False)
    qk_x = _qkconv(px, conv_w, conv_b, rope[0], rope[1], seq_len, True)
    zero = _mlstm_zero_state(bsz)
    ym_c, fin_f, fin_b = _mlstm_bidir(qk_c, pc, gc, gate_b, m_norm, zero, zero, ctx_len)
    ym_x, _, _ = _mlstm_bidir(qk_x, px, gx, gate_b, m_norm, fin_f, fin_b, seq_len)

    yn_x = _na_attention(px, pc, _na_bias_table(lp['na_rpb']), seq_len, ctx_len)

    tables = _s5_tables(lp['s5_lam_re'], lp['s5_lam_im'], lp['s5_log_dt'], lp['s5_b_re'], lp['s5_b_im'],
                        lp['s5_c_re'], lp['s5_c_im'])
    zs = jnp.zeros((bsz, S5_GROUPS * S5_STATE), F32)
    ys_c, fin_s = _s5_branch(uc, tables, ctx_len, (zs, zs, zs, zs), ctx_out)
    ys_x, _ = _s5_branch(ux, tables, seq_len, fin_s, True)

    wts = (_row(lp['s5_d']), lp['s5_glu_w'].astype(BF16), _row(lp['s5_glu_b']),
           lp['w_branch_m'].astype(BF16), lp['w_branch_na'].astype(BF16), lp['w_branch_s5'].astype(BF16),
           lp['w_out'].astype(BF16))
    g_post = _row(lp['norm_mix_post'])
    x = _merge(x, ym_x, yn_x, ys_x, ux, px, wts, mx[2], g_post, seq_len)
    if ctx_out:
        yn_c = _ctx_attention(pc, ctx_len)
        xc = _merge(xc, ym_c, yn_c, ys_c, uc, pc, wts, mc[2], g_post, xc.shape[0])
    return x, xc


def kernel(x, c, ctx, c_ctx, ada_w, ada_b, norm_mix_pre, norm_mix_post, norm_ffn_pre, norm_ffn_post, w_in, m_gate_b, m_conv_w, m_conv_b, m_norm, na_rpb, s5_lam_re, s5_lam_im, s5_log_dt, s5_b_re, s5_b_im, s5_c_re, s5_c_im, s5_d, s5_glu_w, s5_glu_b, w_branch_m, w_branch_na, w_branch_s5, w_out, ffn_w_gate, ffn_w_up, ffn_w_down, moe_router, moe_w_gate, moe_w_up, moe_w_down):
    bsz, seq_len, d = x.shape
    ctx_len = ctx.shape[1]
    depth = w_in.shape[0]
    assert d == D_MODEL and seq_len % (GRID_W * NA_ROWS) == 0 and ctx_len % M_CHUNK == 0
    rope = _rope_tables(seq_len)
    xf = x.reshape(bsz * seq_len, d)
    xc = ctx.reshape(bsz * ctx_len, d)
    cvecs = jnp.zeros((16, d), F32).at[:bsz].set(c).at[bsz].set(c_ctx)
    for l in range(depth):
        last = l == depth - 1
        mod = _adaln(cvecs, ada_w[l].astype(BF16), _row(ada_b[l]))
        mx = [mod[:bsz, k * d:(k + 1) * d].reshape(bsz, 1, d) for k in range(N_MOD)]
        mc = [mod[bsz:bsz + 1, k * d:(k + 1) * d].reshape(1, 1, d) for k in range(N_MOD)]
        lp = {
            'norm_mix_pre': norm_mix_pre[l], 'norm_mix_post': norm_mix_post[l],
            'w_in': w_in[l], 'm_gate_b': m_gate_b[l], 'm_conv_w': m_conv_w[l], 'm_conv_b': m_conv_b[l],
            'm_norm': m_norm[l], 'na_rpb': na_rpb[l], 's5_lam_re': s5_lam_re[l], 's5_lam_im': s5_lam_im[l],
            's5_log_dt': s5_log_dt[l], 's5_b_re': s5_b_re[l], 's5_b_im': s5_b_im[l], 's5_c_re': s5_c_re[l],
            's5_c_im': s5_c_im[l], 's5_d': s5_d[l], 's5_glu_w': s5_glu_w[l], 's5_glu_b': s5_glu_b[l],
            'w_branch_m': w_branch_m[l], 'w_branch_na': w_branch_na[l], 'w_branch_s5': w_branch_s5[l],
            'w_out': w_out[l],
        }
        xf, xc = _token_mixers(xf, xc, mx, mc, lp, rope, seq_len, ctx_len, not last)
        j = l // 2
        g_pre, g_post = _row(norm_ffn_pre[l]), _row(norm_ffn_post[l])
        if l % 2 == 0:
            wg, wu, wd = ffn_w_gate[j].astype(BF16), ffn_w_up[j].astype(BF16), ffn_w_down[j].astype(BF16)

            def channel(h, m, rows):
                return _ffn(h, g_pre, m[3], m[4], wg, wu, wd, m[5], g_post, rows)
        else:
            wg, wu, wd = moe_w_gate[j].astype(BF16), moe_w_up[j].astype(BF16), moe_w_down[j].astype(BF16)
            router = jnp.pad(moe_router[j].astype(F32), ((0, 0), (0, LANES - N_EXPERTS)))
            r_hi = router.astype(BF16)
            r_lo = (router - r_hi.astype(F32)).astype(BF16)

            def channel(h, m, rows):
                return _moe_global(h, g_pre, m[3], m[4], r_hi, r_lo, wg, wu, wd, m[5], g_post, rows)
        xf = channel(xf, mx, seq_len)
        if not last:
            xc = channel(xc, mc, xc.shape[0])
    return xf.reshape(bsz, seq_len, d)
```

```python
import functools
import math

import jax
import jax.numpy as jnp
from jax import lax
from jax.experimental import pallas as pl
from jax.experimental.pallas import tpu as pltpu

F32 = jnp.float32
BF16 = jnp.bfloat16

D_MODEL = 1024
EPS = 1e-6
N_MOD = 6
GRID_W = 64

M_HEADS = 4
M_HEAD_DIM = 128
M_WIDTH = M_HEADS * M_HEAD_DIM
M_GATES = 4 * M_HEADS
M_CHUNK = 128
ROPE_BASE = 10000.0

NA_HEADS = 8
NA_HEAD_DIM = 64
NA_WIDTH = NA_HEADS * NA_HEAD_DIM
WIN_R = 8
WIN_C = 16
NEG_BIG = -1e30

S5_GROUP = 16
S5_WIDTH = 512
S5_GROUPS = S5_WIDTH // S5_GROUP
S5_STATE = 64
S5_MAX_RE = -1e-4

LANES = 128
VMEM_LIMIT = 52 * 1024 * 1024

S5_CHUNK = 16
S5_QUAD_GROUPS = LANES // S5_GROUP
S5_QUADS = S5_GROUPS // S5_QUAD_GROUPS
S5_QCOLS = S5_CHUNK * LANES
S5_QSTATE = S5_QUAD_GROUPS * S5_STATE

N_EXPERTS = 8
TOP_K = 2
MOE_TILE = 1024
MOE_CAP = 256
MOE_GRAIN = 64

COL_QK, COL_V, COL_O, COL_NAQ, COL_NAK, COL_NAV = 0, 2, 3, 10, 11, 12
COL_GATE = 2
IN_MAIN = 6656


def _params(sem):
    return pltpu.CompilerParams(dimension_semantics=sem, vmem_limit_bytes=VMEM_LIMIT)


def _dot(a, b):
    return jnp.dot(a, b, preferred_element_type=F32)


def _dot_nt(a, b):
    return lax.dot_general(a, b, (((1,), (1,)), ((), ())), preferred_element_type=F32)


def _split3(x):
    hi = x.astype(BF16)
    r1 = x - hi.astype(F32)
    mid = r1.astype(BF16)
    lo = (r1 - mid.astype(F32)).astype(BF16)
    return hi, mid, lo


def _rms(x, g):
    return x * lax.rsqrt(jnp.mean(x * x, axis=-1, keepdims=True) + EPS) * g


def _inproj_kernel(x_ref, g_ref, sh_ref, sc_ref, w_ref, wg_ref, wu_ref, o_ref, og_ref, ou_ref, *, tn):
    h = _rms(x_ref[...], g_ref[...]) * (1.0 + sc_ref[0]) + sh_ref[0]
    hb = h.astype(BF16)
    og_ref[...] = _dot(hb, wg_ref[...])
    ou_ref[...] = _dot(hb, wu_ref[...])
    for c in range(IN_MAIN // tn):
        cols = slice(c * tn, (c + 1) * tn)
        o_ref[:, cols] = _dot(hb, w_ref[:, cols]).astype(o_ref.dtype)


def _inproj(x, g, shift, scale, w_main, w_gates, w_u, rows_per_mod):
    t = x.shape[0]
    tm = min(512, rows_per_mod)
    tn = IN_MAIN // 4
    assert rows_per_mod % tm == 0 and t % tm == 0 and tn % LANES == 0
    per = rows_per_mod // tm
    resident = pl.Buffered(1)
    return pl.pallas_call(
        functools.partial(_inproj_kernel, tn=tn),
        grid=(t // tm,),
        in_specs=[
            pl.BlockSpec((tm, D_MODEL), lambda i: (i, 0)),
            pl.BlockSpec((1, D_MODEL), lambda i: (0, 0)),
            pl.BlockSpec((1, 1, D_MODEL), lambda i: (i // per, 0, 0)),
            pl.BlockSpec((1, 1, D_MODEL), lambda i: (i // per, 0, 0)),
            pl.BlockSpec((D_MODEL, IN_MAIN), lambda i: (0, 0), pipeline_mode=resident),
            pl.BlockSpec((D_MODEL, LANES), lambda i: (0, 0), pipeline_mode=resident),
            pl.BlockSpec((D_MODEL, S5_WIDTH), lambda i: (0, 0), pipeline_mode=resident),
        ],
        out_specs=[
            pl.BlockSpec((tm, IN_MAIN), lambda i: (i, 0)),
            pl.BlockSpec((tm, LANES), lambda i: (i, 0)),
            pl.BlockSpec((tm, S5_WIDTH), lambda i: (i, 0)),
        ],
        out_shape=[jax.ShapeDtypeStruct((t, IN_MAIN), BF16), jax.ShapeDtypeStruct((t, LANES), F32),
                   jax.ShapeDtypeStruct((t, S5_WIDTH), F32)],
        compiler_params=_params(("parallel",)),
        name="inproj",
    )(x, g, shift, scale, w_main, w_gates, w_u)


def _adaln_kernel(c_ref, w_ref, b_ref, o_ref):
    c = c_ref[...]
    s = c * jax.nn.sigmoid(c)
    o_ref[...] = _dot(s.astype(BF16), w_ref[...]) + b_ref[...]


def _adaln(cvecs, w, b):
    n = w.shape[1]
    tn = 1024
    return pl.pallas_call(
        _adaln_kernel,
        grid=(n // tn,),
        in_specs=[
            pl.BlockSpec(cvecs.shape, lambda j: (0, 0)),
            pl.BlockSpec((D_MODEL, tn), lambda j: (0, j)),
            pl.BlockSpec((1, tn), lambda j: (0, j)),
        ],
        out_specs=pl.BlockSpec((cvecs.shape[0], tn), lambda j: (0, j)),
        out_shape=jax.ShapeDtypeStruct((cvecs.shape[0], n), F32),
        compiler_params=_params(("parallel",)),
        name="adaln",
    )(cvecs, w, b)


HALO = 16


def _swap32(x):
    lane = lax.broadcasted_iota(jnp.int32, x.shape, 1)
    fwd = pltpu.roll(x, 96, 1)
    bwd = pltpu.roll(x, 32, 1)
    return jnp.where((lane % 64) < 32, fwd, bwd)


def _qkconv_kernel(x_ref, pv_ref, nx_ref, w_ref, b_ref, cos_ref, sin_ref, o_ref, *, tiles_per_seq, rope):
    i = pl.program_id(0)
    tm = x_ref.shape[0]
    x = x_ref[...].astype(F32)
    first = (i % tiles_per_seq) == 0
    last = (i % tiles_per_seq) == tiles_per_seq - 1
    prev_row = jnp.where(first, 0.0, pv_ref[HALO - 1:HALO, :].astype(F32))
    next_row = jnp.where(last, 0.0, nx_ref[0:1, :].astype(F32))
    row = lax.broadcasted_iota(jnp.int32, x.shape, 0)
    xp = jnp.where(row == 0, prev_row, pltpu.roll(x, 1, 0))
    xn = jnp.where(row == tm - 1, next_row, pltpu.roll(x, tm - 1, 0))
    y = xp * w_ref[0:1, :] + x * w_ref[1:2, :] + xn * w_ref[2:3, :] + b_ref[...]
    y = y * jax.nn.sigmoid(y)
    kscale = M_HEAD_DIM ** -0.5
    for hh in range(2 * M_HEADS):
        ys = y[:, hh * LANES:(hh + 1) * LANES]
        if rope:
            ys = ys * cos_ref[...] + _swap32(ys) * sin_ref[...]
        if hh >= M_HEADS:
            ys = ys * kscale
        o_ref[:, hh * LANES:(hh + 1) * LANES] = ys.astype(o_ref.dtype)


def _qkconv(px, conv_w, conv_b, cos_t, sin_t, seq_len, rope):
    t = px.shape[0]
    tm = min(512, seq_len)
    assert seq_len % tm == 0
    tiles = seq_len // tm
    hb = tm // HALO
    nh = t // HALO
    return pl.pallas_call(
        functools.partial(_qkconv_kernel, tiles_per_seq=tiles, rope=rope),
        grid=(t // tm,),
        in_specs=[
            pl.BlockSpec((tm, 2 * M_WIDTH), lambda i: (i, 0)),
            pl.BlockSpec((HALO, 2 * M_WIDTH), lambda i: (jnp.maximum(i * hb - 1, 0), 0)),
            pl.BlockSpec((HALO, 2 * M_WIDTH), lambda i: (jnp.minimum((i + 1) * hb, nh - 1), 0)),
            pl.BlockSpec((3, 2 * M_WIDTH), lambda i: (0, 0)),
            pl.BlockSpec((1, 2 * M_WIDTH), lambda i: (0, 0)),
            pl.BlockSpec((tm, LANES), lambda i: (i % tiles, 0)),
            pl.BlockSpec((tm, LANES), lambda i: (i % tiles, 0)),
        ],
        out_specs=pl.BlockSpec((tm, 2 * M_WIDTH), lambda i: (i, 0)),
        out_shape=jax.ShapeDtypeStruct((t, 2 * M_WIDTH), BF16),
        compiler_params=_params(("parallel",)),
        name="qkconv",
    )(px, px, px, conv_w, conv_b, cos_t, sin_t)


def _rope_tables(n):
    pos = jnp.arange(n, dtype=jnp.int32)
    row = (pos // GRID_W).astype(F32)
    col = (pos % GRID_W).astype(F32)
    n_freq = M_HEAD_DIM // 4
    inv = ROPE_BASE ** (-jnp.arange(n_freq, dtype=F32) / n_freq)
    ar, ac = row[:, None] * inv, col[:, None] * inv
    cos_t = jnp.concatenate([jnp.cos(ar), jnp.cos(ar), jnp.cos(ac), jnp.cos(ac)], axis=-1)
    sin_t = jnp.concatenate([-jnp.sin(ar), jnp.sin(ar), -jnp.sin(ac), jnp.sin(ac)], axis=-1)
    return cos_t, sin_t


def _mlstm_kernel(*refs, reverse, final, nb):
    if final:
        (qk_ref, v_ref, g_ref, gb_ref, c0_ref, n0_ref, m0_ref, hp_ref, op_ref, nw_ref,
         out_ref, cf_ref, nf_ref, mf_ref, c_scr, n_scr, m_scr, qk_scr, qc_scr, qn_scr, s_scr, vt_scr, vu_scr, dec_scr) = refs
    else:
        (qk_ref, v_ref, g_ref, gb_ref, c0_ref, n0_ref, m0_ref,
         out_ref, cf_ref, nf_ref, mf_ref, c_scr, n_scr, m_scr, qk_scr, qc_scr, qn_scr, s_scr, vt_scr, vu_scr, dec_scr) = refs
    c_idx = pl.program_id(1)
    n_c = pl.num_programs(1)
    L = M_CHUNK

    @pl.when(c_idx == 0)
    def _():
        c_scr[...] = c0_ref[...]
        n_scr[...] = n0_ref[...]
        m_scr[...] = m0_ref[...]

    ti = lax.broadcasted_iota(jnp.int32, (L, L), 0)
    si = lax.broadcasted_iota(jnp.int32, (L, L), 1)
    keep = (si >= ti) if reverse else (si <= ti)
    tri = jnp.where(keep, 1.0, 0.0).astype(BF16)
    i_off = 2 * M_HEADS if reverse else 0
    f_off = i_off + M_HEADS
    end = 0 if reverse else L - 1

    units = [(a, b) for a in range(nb) for b in range(M_HEADS)]

    def head(ref, gi, h, base=0):
        return ref[gi, :, base + h * LANES:base + (h + 1) * LANES]

    keep_t = (ti >= si) if reverse else (ti <= si)

    for u, (gi, h) in enumerate(units):
        q = head(qk_ref, gi, h)
        qk_scr[u] = _dot_nt(head(qk_ref, gi, h, M_WIDTH), q)
        qc_scr[u] = _dot_nt(c_scr[gi, h].astype(BF16), q)
        qn_scr[u] = _dot_nt(jnp.broadcast_to(n_scr[gi, h], (8, LANES)).astype(BF16), q)

    for u, (gi, h) in enumerate(units):
        if h == 0:
            g = g_ref[gi] + gb_ref[...]
            f_hi, f_mid, f_lo = _split3(jax.nn.log_sigmoid(g))
            bsum = _dot(tri, f_hi) + _dot(tri, f_mid) + _dot(tri, f_lo)
            g_t = g.T
            b_t = bsum.T
        m_prev = m_scr[gi, h][:, 0:1]
        b_row = b_t[f_off + h:f_off + h + 1, :]
        i_row = g_t[i_off + h:i_off + h + 1, :]
        b_end = b_row[:, end:end + 1]
        d_col = g[:, i_off + h:i_off + h + 1] - bsum[:, f_off + h:f_off + h + 1]

        log_w = jnp.where(keep_t, b_row + d_col, -jnp.inf)
        carry_log = b_row + m_prev
        m_t = jnp.maximum(carry_log, jnp.max(log_w, axis=0, keepdims=True))
        s_t = qk_scr[u] * jnp.exp(log_w - m_t)
        c_scale = jnp.exp(carry_log - m_t)
        den = jnp.sum(s_t, axis=0, keepdims=True) + c_scale * qn_scr[u][0:1, :]
        inv = 1.0 / jnp.maximum(jnp.abs(den), jnp.exp(-m_t))
        s_scr[u] = (s_t * inv).astype(BF16)
        qc_scr[u] = qc_scr[u] * (c_scale * inv)

        m_new = jnp.maximum(b_end + m_prev, jnp.max(b_end - b_row + i_row, axis=-1, keepdims=True))
        u_row = jnp.exp(b_end - b_row + i_row - m_new)
        decay = jnp.exp(b_end + m_prev - m_new)
        v_t = head(v_ref, gi, h).astype(F32).T
        vt_scr[u] = v_t.astype(BF16)
        vu_scr[u] = (v_t * u_row).astype(BF16)
        u8 = jnp.broadcast_to(u_row, (8, L)).astype(BF16)
        n_scr[gi, h] = decay * n_scr[gi, h] + _dot(u8, head(qk_ref, gi, h, M_WIDTH))[0:1, :]
        m_scr[gi, h] = jnp.broadcast_to(m_new, (1, LANES))
        dec_scr[u] = jnp.broadcast_to(decay, (1, LANES))

    for u, (gi, h) in enumerate(units):
        qc_scr[u] = _dot(vt_scr[u], s_scr[u]) + qc_scr[u]
        c_scr[gi, h] = dec_scr[u][:, 0:1] * c_scr[gi, h] + _dot(vu_scr[u], head(qk_ref, gi, h, M_WIDTH))

    for u, (gi, h) in enumerate(units):
        rows = slice(h * M_HEAD_DIM, (h + 1) * M_HEAD_DIM)
        if final:
            hs = qc_scr[u] + hp_ref[gi, rows, :]
            mu = jnp.mean(hs, axis=0, keepdims=True)
            var = jnp.mean(jnp.square(hs - mu), axis=0, keepdims=True)
            hn = ((hs - mu) * lax.rsqrt(var + EPS)).T
            y = hn * nw_ref[:, h * LANES:(h + 1) * LANES] * jax.nn.sigmoid(head(op_ref, gi, h).astype(F32))
            out_ref[gi, :, h * LANES:(h + 1) * LANES] = y.astype(out_ref.dtype)
        else:
            out_ref[gi, rows, :] = qc_scr[u]

    @pl.when(c_idx == n_c - 1)
    def _():
        cf_ref[...] = c_scr[...]
        nf_ref[...] = n_scr[...]
        mf_ref[...] = m_scr[...]


def _mlstm_scan(qk, px, gates, gate_b, state, seq_len, reverse, hprev=None, norm_w=None):
    t = qk.shape[0]
    bsz = t // seq_len
    n_c = seq_len // M_CHUNK
    final = hprev is not None
    nb = 4 if bsz % 4 == 0 else (2 if bsz % 2 == 0 else 1)
    qk, px, gates = (a.reshape(bsz, seq_len, a.shape[1]) for a in (qk, px, gates))

    def chunk(c):
        return (n_c - 1 - c) if reverse else c

    state_specs = [
        pl.BlockSpec((nb, M_HEADS, M_HEAD_DIM, M_HEAD_DIM), lambda b, c: (b, 0, 0, 0)),
        pl.BlockSpec((nb, M_HEADS, 1, LANES), lambda b, c: (b, 0, 0, 0)),
        pl.BlockSpec((nb, M_HEADS, 1, LANES), lambda b, c: (b, 0, 0, 0)),
    ]
    in_specs = [
        pl.BlockSpec((nb, M_CHUNK, 2 * M_WIDTH), lambda b, c: (b, chunk(c), 0)),
        pl.BlockSpec((nb, M_CHUNK, M_WIDTH), lambda b, c: (b, chunk(c), COL_V)),
        pl.BlockSpec((nb, M_CHUNK, LANES), lambda b, c: (b, chunk(c), 0)),
        pl.BlockSpec((1, LANES), lambda b, c: (0, 0)),
    ] + state_specs
    args = [qk, px, gates, gate_b, *state]
    t_spec = pl.BlockSpec((nb, M_WIDTH, M_CHUNK), lambda b, c: (b, 0, chunk(c)))
    if final:
        in_specs += [
            t_spec,
            pl.BlockSpec((nb, M_CHUNK, M_WIDTH), lambda b, c: (b, chunk(c), COL_O)),
            pl.BlockSpec((1, M_WIDTH), lambda b, c: (0, 0)),
        ]
        args += [hprev, px, norm_w]
        out_spec = pl.BlockSpec((nb, M_CHUNK, M_WIDTH), lambda b, c: (b, chunk(c), 0))
        out_shape = jax.ShapeDtypeStruct((bsz, seq_len, M_WIDTH), BF16)
    else:
        out_spec = t_spec
        out_shape = jax.ShapeDtypeStruct((bsz, M_WIDTH, seq_len), F32)
    units = nb * M_HEADS
    out, *fin = pl.pallas_call(
        functools.partial(_mlstm_kernel, reverse=reverse, final=final, nb=nb),
        grid=(bsz // nb, n_c),
        in_specs=in_specs,
        out_specs=[out_spec] + state_specs,
        out_shape=[
            out_shape,
            jax.ShapeDtypeStruct((bsz, M_HEADS, M_HEAD_DIM, M_HEAD_DIM), F32),
            jax.ShapeDtypeStruct((bsz, M_HEADS, 1, LANES), F32),
            jax.ShapeDtypeStruct((bsz, M_HEADS, 1, LANES), F32),
        ],
        scratch_shapes=[
            pltpu.VMEM((nb, M_HEADS, M_HEAD_DIM, M_HEAD_DIM), F32),
            pltpu.VMEM((nb, M_HEADS, 1, LANES), F32),
            pltpu.VMEM((nb, M_HEADS, 1, LANES), F32),
            pltpu.VMEM((units, M_CHUNK, M_CHUNK), F32),
            pltpu.VMEM((units, M_HEAD_DIM, M_CHUNK), F32),
            pltpu.VMEM((units, 8, M_CHUNK), F32),
            pltpu.VMEM((units, M_CHUNK, M_CHUNK), BF16),
            pltpu.VMEM((units, M_HEAD_DIM, M_CHUNK), BF16),
            pltpu.VMEM((units, M_HEAD_DIM, M_CHUNK), BF16),
            pltpu.VMEM((units, 1, LANES), F32),
        ],
        compiler_params=_params(("parallel", "arbitrary")),
        name="mlstm_rev" if reverse else "mlstm_fwd",
    )(*args)
    return [out.reshape(t, M_WIDTH) if final else out] + fin


def _mlstm_zero_state(bsz):
    return (jnp.zeros((bsz, M_HEADS, M_HEAD_DIM, M_HEAD_DIM), F32),
            jnp.zeros((bsz, M_HEADS, 1, LANES), F32),
            jnp.zeros((bsz, M_HEADS, 1, LANES), F32))


def _mlstm_bidir(qk, px, gates, gate_b, norm_w, state_f, state_b, seq_len):
    h_f, *fin_f = _mlstm_scan(qk, px, gates, gate_b, state_f, seq_len, False)
    y, *fin_b = _mlstm_scan(qk, px, gates, gate_b, state_b, seq_len, True, hprev=h_f, norm_w=norm_w)
    return y, tuple(fin_f), tuple(fin_b)


NA_ROWS = 8
NA_STEP = 4


def _na_kernel(q_ref, kp_ref, kc_ref, kn_ref, vp_ref, vc_ref, vn_ref, kx_ref, vx_ref, *rest, rows):
    bias_refs = rest[:NA_STEP]
    o_ref, kwin, vwin, s_scr, p_scr = rest[NA_STEP:]
    step = pl.program_id(1)
    blk = NA_ROWS * GRID_W

    @pl.when(step % (NA_ROWS // NA_STEP) == 0)
    def _():
        kwin[0:blk, :] = kp_ref[...]
        kwin[blk:2 * blk, :] = kc_ref[...]
        kwin[2 * blk:3 * blk, :] = kn_ref[...]
        vwin[0:blk, :] = vp_ref[...]
        vwin[blk:2 * blk, :] = vc_ref[...]
        vwin[2 * blk:3 * blk, :] = vn_ref[...]

    g = step // (NA_ROWS // NA_STEP)
    n_loc = WIN_R * GRID_W
    lane = lax.broadcasted_iota(jnp.int32, (GRID_W, LANES), 1)
    scale = NA_HEAD_DIM ** -0.5
    n_ctx = kx_ref.shape[0]
    units = [(kk, hp) for kk in range(NA_STEP) for hp in range(NA_HEADS // 2)]
    offs = []
    for kk in range(NA_STEP):
        r = step * NA_STEP + kk
        r0 = jnp.clip(r - WIN_R // 2, 0, rows - WIN_R)
        offs.append(pl.multiple_of((r0 - NA_ROWS * g + NA_ROWS) * GRID_W, GRID_W))

    for u, (kk, hp) in enumerate(units):
        cols = slice(hp * LANES, (hp + 1) * LANES)
        q2 = q_ref[kk * GRID_W:(kk + 1) * GRID_W, cols] * scale
        zero = jnp.zeros_like(q2)
        qs = jnp.concatenate([jnp.where(lane < NA_HEAD_DIM, q2, zero), jnp.where(lane < NA_HEAD_DIM, zero, q2)],
                             axis=0)
        s_scr[u, :, 0:n_loc] = _dot_nt(qs, kwin[pl.ds(offs[kk], n_loc), cols])
        s_scr[u, :, n_loc:n_loc + n_ctx] = _dot_nt(qs, kx_ref[:, cols])

    for u, (kk, hp) in enumerate(units):
        s_loc = s_scr[u, :, 0:n_loc] + bias_refs[kk][0, hp]
        s_ctx = s_scr[u, :, n_loc:n_loc + n_ctx]
        m = jnp.maximum(jnp.max(s_loc, axis=-1, keepdims=True), jnp.max(s_ctx, axis=-1, keepdims=True))
        p_loc = jnp.exp(s_loc - m)
        p_ctx = jnp.exp(s_ctx - m)
        inv = 1.0 / (jnp.sum(p_loc, axis=-1, keepdims=True) + jnp.sum(p_ctx, axis=-1, keepdims=True))
        p_scr[u, :, 0:n_loc] = (p_loc * inv).astype(BF16)
        p_scr[u, :, n_loc:n_loc + n_ctx] = (p_ctx * inv).astype(BF16)

    for u, (kk, hp) in enumerate(units):
        cols = slice(hp * LANES, (hp + 1) * LANES)
        o2 = _dot(p_scr[u, :, 0:n_loc], vwin[pl.ds(offs[kk], n_loc), cols])
        o2 += _dot(p_scr[u, :, n_loc:n_loc + n_ctx], vx_ref[:, cols])
        out = jnp.where(lane < NA_HEAD_DIM, o2[0:GRID_W], o2[GRID_W:2 * GRID_W])
        o_ref[kk * GRID_W:(kk + 1) * GRID_W, cols] = out.astype(o_ref.dtype)


def _na_bias_table(rpb):
    qc = jnp.arange(GRID_W)[:, None]
    kc = jnp.arange(GRID_W)[None, :]
    cs = jnp.clip(qc - WIN_C // 2, 0, GRID_W - WIN_C)
    ok = (kc >= cs) & (kc < cs + WIN_C)
    dc = jnp.clip(kc - qc + WIN_C - 1, 0, 2 * WIN_C - 2)
    dr = jnp.arange(WIN_R)[:, None] + jnp.arange(WIN_R)[None, :]
    tab = rpb.astype(F32)[:, dr][:, :, :, dc]
    tab = jnp.where(ok[None, None, None], tab, NEG_BIG)
    tab = jnp.transpose(tab, (1, 0, 3, 2, 4))
    return tab.reshape(WIN_R, NA_HEADS // 2, 2 * GRID_W, WIN_R * GRID_W)


def _na_attention(px, pc, bias, seq_len, ctx_len):
    t = px.shape[0]
    bsz = t // seq_len
    rows = seq_len // GRID_W
    n_g = rows // NA_ROWS
    blk = NA_ROWS * GRID_W

    steps = rows // NA_STEP
    per_blk = NA_ROWS // NA_STEP

    def kv_spec(col, shift):
        def imap(b, s):
            g = jnp.clip(s // per_blk + shift, 0, n_g - 1)
            return (b * n_g + g, col)
        return pl.BlockSpec((blk, NA_WIDTH), imap)

    def bias_spec(kk):
        def imap(b, s):
            r = s * NA_STEP + kk
            r0 = jnp.clip(r - WIN_R // 2, 0, rows - WIN_R)
            return (r0 - r + WIN_R - 1, 0, 0, 0)
        return pl.BlockSpec((1, NA_HEADS // 2, 2 * GRID_W, WIN_R * GRID_W), imap)

    return pl.pallas_call(
        functools.partial(_na_kernel, rows=rows),
        grid=(bsz, steps),
        in_specs=[
            pl.BlockSpec((NA_STEP * GRID_W, NA_WIDTH), lambda b, s: (b * steps + s, COL_NAQ)),
            kv_spec(COL_NAK, -1), kv_spec(COL_NAK, 0), kv_spec(COL_NAK, 1),
            kv_spec(COL_NAV, -1), kv_spec(COL_NAV, 0), kv_spec(COL_NAV, 1),
            pl.BlockSpec((ctx_len, NA_WIDTH), lambda b, s: (b, COL_NAK)),
            pl.BlockSpec((ctx_len, NA_WIDTH), lambda b, s: (b, COL_NAV)),
        ] + [bias_spec(kk) for kk in range(NA_STEP)],
        out_specs=pl.BlockSpec((NA_STEP * GRID_W, NA_WIDTH), lambda b, s: (b * steps + s, 0)),
        out_shape=jax.ShapeDtypeStruct((t, NA_WIDTH), BF16),
        scratch_shapes=[
            pltpu.VMEM((3 * blk, NA_WIDTH), BF16), pltpu.VMEM((3 * blk, NA_WIDTH), BF16),
            pltpu.VMEM((NA_STEP * NA_HEADS // 2, 2 * GRID_W, WIN_R * GRID_W + ctx_len), F32),
            pltpu.VMEM((NA_STEP * NA_HEADS // 2, 2 * GRID_W, WIN_R * GRID_W + ctx_len), BF16),
        ],
        compiler_params=_params(("parallel", "arbitrary")),
        name="na_attn",
    )(px, px, px, px, px, px, px, pc, pc, *([bias] * NA_STEP))


def _ctx_attn_kernel(q_ref, k_ref, v_ref, o_ref):
    n = q_ref.shape[0]
    lane = lax.broadcasted_iota(jnp.int32, (n, LANES), 1)
    scale = NA_HEAD_DIM ** -0.5
    for hp in range(NA_HEADS // 2):
        cols = slice(hp * LANES, (hp + 1) * LANES)
        q2 = q_ref[:, cols] * scale
        k2 = k_ref[:, cols]
        v2 = v_ref[:, cols]
        outs = []
        for sub in range(2):
            sel = (lane < NA_HEAD_DIM) if sub == 0 else (lane >= NA_HEAD_DIM)
            s = _dot_nt(jnp.where(sel, q2, jnp.zeros_like(q2)), k2)
            p = jnp.exp(s - jnp.max(s, axis=-1, keepdims=True))
            outs.append(_dot(p.astype(BF16), v2) / jnp.sum(p, axis=-1, keepdims=True))
        o_ref[:, cols] = jnp.where(lane < NA_HEAD_DIM, outs[0], outs[1]).astype(o_ref.dtype)


def _ctx_attention(pc, ctx_len):
    t = pc.shape[0]
    return pl.pallas_call(
        _ctx_attn_kernel,
        grid=(t // ctx_len,),
        in_specs=[pl.BlockSpec((ctx_len, NA_WIDTH), lambda b, col=col: (b, col))
                  for col in (COL_NAQ, COL_NAK, COL_NAV)],
        out_specs=pl.BlockSpec((ctx_len, NA_WIDTH), lambda b: (b, 0)),
        out_shape=jax.ShapeDtypeStruct((t, NA_WIDTH), BF16),
        compiler_params=_params(("parallel",)),
        name="ctx_attn",
    )(pc, pc, pc)


def _cmul(a, b):
    return a[0] * b[0] - a[1] * b[1], a[0] * b[1] + a[1] * b[0]


def _s5_tables(lam_re, lam_im, log_dt, b_re, b_im, c_re, c_im):
    L, G, S, H = S5_CHUNK, S5_GROUPS, S5_STATE, S5_GROUP
    lr = jnp.minimum(lam_re.astype(F32), S5_MAX_RE)
    li = lam_im.astype(F32)
    dt = jnp.exp(log_dt.astype(F32))[..., None]
    zr, zi = lr * dt, li * dt
    d = jnp.arange(L + 1, dtype=F32)[None, :, None, None]
    mag = jnp.exp(zr[:, None] * d)
    pw = (mag * jnp.cos(zi[:, None] * d), mag * jnp.sin(zi[:, None] * d))
    lb = (pw[0][:, 1], pw[1][:, 1])
    den = lr * lr + li * li
    ratio = (((lb[0] - 1.0) * lr + lb[1] * li) / den, (lb[1] * lr - (lb[0] - 1.0) * li) / den)
    bb = _cmul((ratio[0][..., None], ratio[1][..., None]), (b_re.astype(F32), b_im.astype(F32)))
    cc = (c_re.astype(F32), c_im.astype(F32))

    def kern(x):
        cl = _cmul((cc[0][x][None], cc[1][x][None]), (pw[0][x, :L, :, None, :], pw[1][x, :L, :, None, :]))
        return jnp.einsum('dghp,gpj->dghj', cl[0], bb[0][x]) - jnp.einsum('dghp,gpj->dghj', cl[1], bb[1][x])

    def state_in(x, steps):
        p = (pw[0][x][steps][:, :, None, :], pw[1][x][steps][:, :, None, :])
        r = _cmul(p, (jnp.swapaxes(bb[0][x], 1, 2)[None], jnp.swapaxes(bb[1][x], 1, 2)[None]))
        return jnp.transpose(r[0], (1, 0, 2, 3)), jnp.transpose(r[1], (1, 0, 2, 3))

    def state_out(x, steps):
        p = (pw[0][x][steps][:, :, None, :], pw[1][x][steps][:, :, None, :])
        r = _cmul(p, (cc[0][x][None], cc[1][x][None]))
        return jnp.transpose(r[0], (1, 3, 0, 2)), jnp.transpose(r[1], (1, 3, 0, 2))

    steps = jnp.arange(L)
    wf = state_in(0, L - 1 - steps)
    wb = state_in(1, steps)
    vf = state_out(0, steps + 1)
    vb = state_out(1, L - steps)
    Q, A = S5_QUADS, S5_QUAD_GROUPS
    kd = jnp.stack([kern(0), kern(1)]).reshape(2, L, Q, A, H, H)
    kd = jnp.transpose(kd, (0, 2, 1, 5, 3, 4)).reshape(2, Q, L, H, A * H)
    wk = jnp.stack([wf[0], wf[1], wb[0], wb[1]]).reshape(4, Q, A // 2, 2, L, H, S)
    wd = jnp.transpose(wk, (0, 1, 4, 2, 5, 3, 6)).reshape(4, Q, L, (A // 2) * H, 2 * S)
    vk = jnp.stack([vf[0], -vf[1], vb[0], -vb[1]]).reshape(4, Q, A, S, L, H)
    vd = jnp.transpose(vk, (0, 1, 4, 3, 2, 5)).reshape(4, Q, L, S, A * H)
    a_re = pw[0][:, L].reshape(2, 1, G * S)
    a_im = pw[1][:, L].reshape(2, 1, G * S)
    return wd, kd, vd, a_re, a_im


def _s5_operand(u_ref):
    return jnp.concatenate([u_ref[:, s, :] for s in range(S5_CHUNK)], axis=1).astype(BF16)


def _s5_expand_w(wd_ref, w_scr):
    H, half = S5_GROUP, S5_STATE
    w_scr[...] = jnp.zeros_like(w_scr)
    lane = lax.broadcasted_iota(jnp.int32, (H, LANES), 1)
    for k in range(4):
        for s in range(S5_CHUNK):
            for c in range(S5_QUAD_GROUPS // 2):
                blk = wd_ref[k, 0, s, c * H:(c + 1) * H, :]
                cols = slice(k * S5_QSTATE + c * LANES, k * S5_QSTATE + (c + 1) * LANES)
                row = s * LANES + 2 * c * H
                w_scr[row:row + H, cols] = jnp.where(lane < half, blk, 0.0).astype(BF16)
                w_scr[row + H:row + 2 * H, cols] = jnp.where(lane >= half, blk, 0.0).astype(BF16)


def _s5_expand_mv(kd_ref, vd_ref, m_scr, v_scr):
    H, L = S5_GROUP, S5_CHUNK
    rg = lax.broadcasted_iota(jnp.int32, (LANES, LANES), 0) // H
    lg = lax.broadcasted_iota(jnp.int32, (LANES, LANES), 1) // H
    tiles = []
    for x in range(2):
        tiles.append([jnp.where(rg == lg, jnp.concatenate([kd_ref[x, 0, d]] * S5_QUAD_GROUPS, axis=0), 0.0)
                      for d in range(L)])
    for s in range(L):
        for t in range(L):
            blk = tiles[0][t - s] if t > s else (tiles[1][s - t] if s > t else tiles[0][0] + tiles[1][0])
            m_scr[s * LANES:(s + 1) * LANES, t * LANES:(t + 1) * LANES] = blk.astype(BF16)
    lg2 = lax.broadcasted_iota(jnp.int32, (S5_STATE, LANES), 1) // H
    for k in range(4):
        for t in range(L):
            blk = vd_ref[k, 0, t]
            for b in range(S5_QUAD_GROUPS):
                row = k * S5_QSTATE + b * S5_STATE
                v_scr[row:row + S5_STATE, t * LANES:(t + 1) * LANES] = jnp.where(lg2 == b, blk, 0.0).astype(BF16)


def _s5_local_kernel(u_ref, wd_ref, sfr_ref, sfi_ref, sbr_ref, sbi_ref, w_scr):
    @pl.when(pl.program_id(1) == 0)
    def _():
        _s5_expand_w(wd_ref, w_scr)

    s = _dot(_s5_operand(u_ref), w_scr[...])
    n = S5_QSTATE
    sfr_ref[...] = s[:, 0:n]
    sfi_ref[...] = s[:, n:2 * n]
    sbr_ref[...] = s[:, 2 * n:3 * n]
    sbi_ref[...] = s[:, 3 * n:4 * n]


def _s5_local_states(u3, wd):
    rows = u3.shape[0]
    tr = min(256, rows)
    assert rows % tr == 0
    out = jax.ShapeDtypeStruct((rows, S5_GROUPS * S5_STATE), F32)
    return pl.pallas_call(
        _s5_local_kernel,
        grid=(S5_QUADS, rows // tr),
        in_specs=[
            pl.BlockSpec((tr, S5_CHUNK, LANES), lambda q, i: (i, 0, q)),
            pl.BlockSpec((4, 1) + wd.shape[2:], lambda q, i: (0, q, 0, 0, 0)),
        ],
        out_specs=[pl.BlockSpec((tr, S5_QSTATE), lambda q, i: (i, q))] * 4,
        out_shape=[out] * 4,
        scratch_shapes=[pltpu.VMEM((S5_QCOLS, 4 * S5_QSTATE), BF16)],
        compiler_params=_params(("arbitrary", "arbitrary")),
        name="s5_local",
    )(u3, wd)


def _s5_scan_kernel(sr_ref, si_ref, ar_ref, ai_ref, h0r_ref, h0i_ref, hr_ref, hi_ref, fr_ref, fi_ref,
                    hr_scr, hi_scr, *, reverse):
    i = pl.program_id(1)
    n_i = pl.num_programs(1)
    steps = sr_ref.shape[1]

    @pl.when(i == 0)
    def _():
        hr_scr[...] = h0r_ref[...]
        hi_scr[...] = h0i_ref[...]

    ar = ar_ref[...]
    ai = ai_ref[...]

    def body(j, carry):
        hr, hi = carry
        c = (steps - 1 - j) if reverse else j
        hr_ref[:, c, :] = hr
        hi_ref[:, c, :] = hi
        nr = ar * hr - ai * hi + sr_ref[:, c, :]
        ni = ar * hi + ai * hr + si_ref[:, c, :]
        return nr, ni

    hr, hi = lax.fori_loop(0, steps, body, (hr_scr[...], hi_scr[...]))
    hr_scr[...] = hr
    hi_scr[...] = hi

    @pl.when(i == n_i - 1)
    def _():
        fr_ref[...] = hr
        fi_ref[...] = hi


def _s5_chunk_scan(s_re, s_im, a_re, a_im, h0_re, h0_im, bsz, reverse):
    _, n_c, width = s_re.shape
    tr = min(64, n_c)
    tc = 512
    n_i = n_c // tr

    def rmap(j, i):
        return (0, (n_i - 1 - i) if reverse else i, j)

    big = jax.ShapeDtypeStruct((bsz, n_c, width), F32)
    small = jax.ShapeDtypeStruct((bsz, width), F32)
    return pl.pallas_call(
        functools.partial(_s5_scan_kernel, reverse=reverse),
        grid=(width // tc, n_i),
        in_specs=[
            pl.BlockSpec((bsz, tr, tc), rmap), pl.BlockSpec((bsz, tr, tc), rmap),
            pl.BlockSpec((1, tc), lambda j, i: (0, j)), pl.BlockSpec((1, tc), lambda j, i: (0, j)),
            pl.BlockSpec((bsz, tc), lambda j, i: (0, j)), pl.BlockSpec((bsz, tc), lambda j, i: (0, j)),
        ],
        out_specs=[
            pl.BlockSpec((bsz, tr, tc), rmap), pl.BlockSpec((bsz, tr, tc), rmap),
            pl.BlockSpec((bsz, tc), lambda j, i: (0, j)), pl.BlockSpec((bsz, tc), lambda j, i: (0, j)),
        ],
        out_shape=[big, big, small, small],
        scratch_shapes=[pltpu.VMEM((bsz, tc), F32), pltpu.VMEM((bsz, tc), F32)],
        compiler_params=_params(("parallel", "arbitrary")),
        name="s5_scan_rev" if reverse else "s5_scan_fwd",
    )(s_re, s_im, a_re, a_im, h0_re, h0_im)


def _s5_out_kernel(u_ref, kd_ref, vd_ref, hfr_ref, hfi_ref, hbr_ref, hbi_ref, y_ref, m_scr, v_scr):
    @pl.when(pl.program_id(1) == 0)
    def _():
        _s5_expand_mv(kd_ref, vd_ref, m_scr, v_scr)

    n = S5_QSTATE
    y = _dot(_s5_operand(u_ref), m_scr[...])
    for k, h_ref in enumerate((hfr_ref, hfi_ref, hbr_ref, hbi_ref)):
        y += _dot(h_ref[...].astype(BF16), v_scr[k * n:(k + 1) * n, :])
    for t in range(S5_CHUNK):
        y_ref[:, t, :] = y[:, t * LANES:(t + 1) * LANES]


def _s5_outputs(u3, kd, vd, h_states):
    rows = u3.shape[0]
    tr = min(256, rows)
    return pl.pallas_call(
        _s5_out_kernel,
        grid=(S5_QUADS, rows // tr),
        in_specs=[
            pl.BlockSpec((tr, S5_CHUNK, LANES), lambda q, i: (i, 0, q)),
            pl.BlockSpec((2, 1) + kd.shape[2:], lambda q, i: (0, q, 0, 0, 0)),
            pl.BlockSpec((4, 1) + vd.shape[2:], lambda q, i: (0, q, 0, 0, 0)),
        ] + [pl.BlockSpec((tr, S5_QSTATE), lambda q, i: (i, q))] * 4,
        out_specs=pl.BlockSpec((tr, S5_CHUNK, LANES), lambda q, i: (i, 0, q)),
        out_shape=jax.ShapeDtypeStruct((rows, S5_CHUNK, S5_WIDTH), F32),
        scratch_shapes=[pltpu.VMEM((S5_QCOLS, S5_QCOLS), BF16), pltpu.VMEM((4 * S5_QSTATE, S5_QCOLS), BF16)],
        compiler_params=_params(("arbitrary", "arbitrary")),
        name="s5_out",
    )(u3, kd, vd, *h_states)


def _s5_branch(u, tables, seq_len, init, want_y):
    w_q, m_q, v_q, a_re, a_im = tables
    t = u.shape[0]
    bsz = t // seq_len
    n_c = seq_len // S5_CHUNK
    width = S5_GROUPS * S5_STATE
    u3 = u.reshape(t // S5_CHUNK, S5_CHUNK, S5_WIDTH)
    s_loc = [s.reshape(bsz, n_c, width) for s in _s5_local_states(u3, w_q)]
    hfr, hfi, ffr, ffi = _s5_chunk_scan(s_loc[0], s_loc[1], a_re[0], a_im[0], init[0], init[1], bsz, False)
    hbr, hbi, fbr, fbi = _s5_chunk_scan(s_loc[2], s_loc[3], a_re[1], a_im[1], init[2], init[3], bsz, True)
    y = None
    if want_y:
        h_states = [h.reshape(bsz * n_c, width) for h in (hfr, hfi, hbr, hbi)]
        y = _s5_outputs(u3, m_q, v_q, h_states).reshape(t, S5_WIDTH)
    return y, (ffr, ffi, fbr, fbi)


def _gelu_tanh(x):
    return 0.5 * x * (1.0 + jnp.tanh(math.sqrt(2.0 / math.pi) * (x + 0.044715 * (x * x * x))))


def _merge_kernel(x_ref, ym_ref, yn_ref, ys_ref, u_ref, gm_ref, gn_ref, gs_ref, d_ref, gw_ref, gb_ref,
                  wm_ref, wn_ref, ws_ref, wo_ref, gate_ref, npost_ref, o_ref):
    ys = ys_ref[...] + d_ref[...] * u_ref[...]
    g = _gelu_tanh(ys)
    gg = g * jax.nn.sigmoid(_dot(g.astype(BF16), gw_ref[...]) + gb_ref[...])
    y = jax.nn.sigmoid(gm_ref[...].astype(F32)) * _dot(ym_ref[...], wm_ref[...])
    y += jax.nn.sigmoid(gn_ref[...].astype(F32)) * _dot(yn_ref[...], wn_ref[...])
    y += jax.nn.sigmoid(gs_ref[...].astype(F32)) * _dot(gg.astype(BF16), ws_ref[...])
    out = _dot(y.astype(BF16), wo_ref[...])
    o_ref[...] = x_ref[...] + gate_ref[0] * _rms(out, npost_ref[...])


def _merge(x, ym, yn, ys, u, px, wts, gate, npost, rows_per_mod):
    t = x.shape[0]
    tm = min(512, rows_per_mod)
    per = rows_per_mod // tm
    s5_d, glu_w, glu_b, w_m, w_n, w_s, w_o = wts
    row = lambda i: (i, 0)
    const = lambda i: (0, 0)
    return pl.pallas_call(
        _merge_kernel,
        grid=(t // tm,),
        in_specs=[
            pl.BlockSpec((tm, D_MODEL), row),
            pl.BlockSpec((tm, M_WIDTH), row),
            pl.BlockSpec((tm, NA_WIDTH), row),
            pl.BlockSpec((tm, S5_WIDTH), row),
            pl.BlockSpec((tm, S5_WIDTH), row),
            pl.BlockSpec((tm, D_MODEL), lambda i: (i, COL_GATE)),
            pl.BlockSpec((tm, D_MODEL), lambda i: (i, COL_GATE + 1)),
            pl.BlockSpec((tm, D_MODEL), lambda i: (i, COL_GATE + 2)),
            pl.BlockSpec((1, S5_WIDTH), const),
            pl.BlockSpec((S5_WIDTH, S5_WIDTH), const),
            pl.BlockSpec((1, S5_WIDTH), const),
            pl.BlockSpec((M_WIDTH, D_MODEL), const),
            pl.BlockSpec((NA_WIDTH, D_MODEL), const),
            pl.BlockSpec((S5_WIDTH, D_MODEL), const),
            pl.BlockSpec((D_MODEL, D_MODEL), const),
            pl.BlockSpec((1, 1, D_MODEL), lambda i: (i // per, 0, 0)),
            pl.BlockSpec((1, D_MODEL), const),
        ],
        out_specs=pl.BlockSpec((tm, D_MODEL), row),
        out_shape=jax.ShapeDtypeStruct((t, D_MODEL), F32),
        compiler_params=_params(("parallel",)),
        name="merge",
    )(x, ym, yn, ys, u, px, px, px, s5_d, glu_w, glu_b, w_m, w_n, w_s, w_o, gate, npost)


def _swiglu(h, wg, wu, wd):
    a = _dot(h, wg)
    a = a * jax.nn.sigmoid(a) * _dot(h, wu)
    return _dot(a.astype(BF16), wd)


def _ffn_kernel(x_ref, g_ref, sh_ref, sc_ref, wg_ref, wu_ref, wd_ref, gate_ref, npost_ref, o_ref, *, ff_tile):
    h = (_rms(x_ref[...], g_ref[...]) * (1.0 + sc_ref[0]) + sh_ref[0]).astype(BF16)
    acc = jnp.zeros(x_ref.shape, F32)
    for j in range(wg_ref.shape[1] // ff_tile):
        cols = slice(j * ff_tile, (j + 1) * ff_tile)
        acc += _swiglu(h, wg_ref[:, cols], wu_ref[:, cols], wd_ref[cols, :])
    o_ref[...] = x_ref[...] + gate_ref[0] * _rms(acc, npost_ref[...])


def _ffn(x, g, shift, scale, w_gate, w_up, w_down, gate, npost, rows_per_mod):
    t = x.shape[0]
    ff = w_gate.shape[1]
    ff_tile = ff // 2
    tm = min(512, rows_per_mod)
    per = rows_per_mod // tm
    assert ff_tile % LANES == 0
    resident = pl.Buffered(1)
    return pl.pallas_call(
        functools.partial(_ffn_kernel, ff_tile=ff_tile),
        grid=(t // tm,),
        in_specs=[
            pl.BlockSpec((tm, D_MODEL), lambda i: (i, 0)),
            pl.BlockSpec((1, D_MODEL), lambda i: (0, 0)),
            pl.BlockSpec((1, 1, D_MODEL), lambda i: (i // per, 0, 0)),
            pl.BlockSpec((1, 1, D_MODEL), lambda i: (i // per, 0, 0)),
            pl.BlockSpec((D_MODEL, ff), lambda i: (0, 0), pipeline_mode=resident),
            pl.BlockSpec((D_MODEL, ff), lambda i: (0, 0), pipeline_mode=resident),
            pl.BlockSpec((ff, D_MODEL), lambda i: (0, 0), pipeline_mode=resident),
            pl.BlockSpec((1, 1, D_MODEL), lambda i: (i // per, 0, 0)),
            pl.BlockSpec((1, D_MODEL), lambda i: (0, 0)),
        ],
        out_specs=pl.BlockSpec((tm, D_MODEL), lambda i: (i, 0)),
        out_shape=jax.ShapeDtypeStruct((t, D_MODEL), F32),
        compiler_params=_params(("parallel",)),
        name="ffn",
    )(x, g, shift, scale, w_gate, w_up, w_down, gate, npost)


def _moe_kernel(x_ref, g_ref, sh_ref, sc_ref, rh_ref, rl_ref, wg_ref, wu_ref, wd_ref, gate_ref, npost_ref,
                o_ref, hn_ref, acc_ref, comb_ref, rank_c_ref, rank_r_ref, mask_r_ref, xc_ref, yc_ref, cnt_ref):
    e = pl.program_id(1)
    j = pl.program_id(2)
    tm = x_ref.shape[0]
    cap = MOE_CAP

    @pl.when((e == 0) & (j == 0))
    def _():
        h = _rms(x_ref[...], g_ref[...]) * (1.0 + sc_ref[0]) + sh_ref[0]
        hi = h.astype(BF16)
        hn_ref[...] = hi
        acc_ref[...] = jnp.zeros_like(acc_ref)
        lo = (h - hi.astype(F32)).astype(BF16)
        logits = _dot(hi, rh_ref[...]) + _dot(lo, rh_ref[...]) + _dot(hi, rl_ref[...])
        lane = lax.broadcasted_iota(jnp.int32, logits.shape, 1)
        logits = jnp.where(lane < N_EXPERTS, logits, -jnp.inf)
        m1 = jnp.max(logits, axis=-1, keepdims=True)
        i1 = jnp.min(jnp.where(logits == m1, lane, LANES), axis=-1, keepdims=True)
        rest = jnp.where(lane == i1, -jnp.inf, logits)
        m2 = jnp.max(rest, axis=-1, keepdims=True)
        i2 = jnp.min(jnp.where(rest == m2, lane, LANES), axis=-1, keepdims=True)
        e2 = jnp.exp(m2 - m1)
        comb_ref[...] = jnp.where(lane == i1, 1.0 / (1.0 + e2), 0.0) + jnp.where(lane == i2, e2 / (1.0 + e2), 0.0)
        sel = jnp.where((lane == i1) | (lane == i2), 1.0, 0.0)
        ti = lax.broadcasted_iota(jnp.int32, (tm, tm), 0)
        si = lax.broadcasted_iota(jnp.int32, (tm, tm), 1)
        before = jnp.where(si < ti, 1.0, 0.0).astype(BF16)
        rank_c_ref[...] = _dot(before, sel.astype(BF16))
        sel_t = sel.T
        after = jnp.where(ti < si, 1.0, 0.0).astype(BF16)
        rank_r_ref[...] = _dot(sel_t.astype(BF16), after)
        mask_r_ref[...] = sel_t
        counts = jnp.sum(sel_t, axis=1, keepdims=True)
        for k in range(N_EXPERTS):
            cnt_ref[k] = jnp.sum(counts[k:k + 1, :]).astype(jnp.int32)

    n_pass = (cnt_ref[e] + (cap - 1)) // cap

    @pl.when(j == 0)
    def _():
        rank_row = rank_r_ref[pl.ds(e, 1), :]
        mask_row = mask_r_ref[pl.ds(e, 1), :]
        slot = lax.broadcasted_iota(jnp.int32, (cap, tm), 0).astype(F32)

        def compact(p, _):
            base = (p * cap).astype(F32)
            onehot = jnp.where((rank_row - base == slot) & (mask_row > 0.0), 1.0, 0.0).astype(BF16)
            rows = pl.ds(pl.multiple_of(p * cap, 16), cap)
            xc_ref[rows, :] = _dot(onehot, hn_ref[...]).astype(BF16)
            yc_ref[rows, :] = jnp.zeros((cap, D_MODEL), F32)
            return 0

        lax.fori_loop(0, n_pass, compact, 0)

    def expert(start, size):
        rows = pl.ds(pl.multiple_of(start, MOE_GRAIN), size)
        yc_ref[rows, :] += _swiglu(xc_ref[rows, :], wg_ref[0], wu_ref[0], wd_ref[0])

    n_tok = cnt_ref[e]
    n_full = n_tok // cap
    rem_blocks = (n_tok - n_full * cap + (MOE_GRAIN - 1)) // MOE_GRAIN
    n_full = n_full + (rem_blocks == cap // MOE_GRAIN).astype(jnp.int32)

    def full_pass(p, _):
        expert(p * cap, cap)
        return 0

    lax.fori_loop(0, n_full, full_pass, 0)
    for blocks in range(1, cap // MOE_GRAIN):
        @pl.when(rem_blocks == blocks)
        def _(blocks=blocks):
            expert(n_full * cap, blocks * MOE_GRAIN)

    @pl.when(j == pl.num_programs(2) - 1)
    def _():
        lane = lax.broadcasted_iota(jnp.int32, (tm, LANES), 1)
        pick = lane == e
        rank_col = jnp.sum(jnp.where(pick, rank_c_ref[...], 0.0), axis=-1, keepdims=True)
        w_col = jnp.sum(jnp.where(pick, comb_ref[...], 0.0), axis=-1, keepdims=True)
        slot = lax.broadcasted_iota(jnp.int32, (tm, cap), 1).astype(F32)

        def expand(p, _):
            base = (p * cap).astype(F32)
            onehot = jnp.where(rank_col - base == slot, 1.0, 0.0).astype(BF16)
            rows = pl.ds(pl.multiple_of(p * cap, 16), cap)
            acc_ref[...] += w_col * _dot(onehot, yc_ref[rows, :].astype(BF16))
            return 0

        lax.fori_loop(0, n_pass, expand, 0)

    @pl.when((e == N_EXPERTS - 1) & (j == pl.num_programs(2) - 1))
    def _():
        o_ref[...] = x_ref[...] + gate_ref[0] * _rms(acc_ref[...], npost_ref[...])


def _moe(x, g, shift, scale, r_hi, r_lo, w_gate, w_up, w_down, gate, npost, rows_per_mod):
    t = x.shape[0]
    n_e, _, ff = w_gate.shape
    assert n_e == N_EXPERTS
    ff_tile = ff // 4
    tm = min(MOE_TILE, rows_per_mod)
    per = rows_per_mod // tm
    passes = -(-tm // MOE_CAP)
    assert ff_tile % LANES == 0 and MOE_CAP % MOE_GRAIN == 0 and MOE_GRAIN % 16 == 0
    return pl.pallas_call(
        _moe_kernel,
        grid=(t // tm, n_e, ff // ff_tile),
        in_specs=[
            pl.BlockSpec((tm, D_MODEL), lambda i, e, j: (i, 0)),
            pl.BlockSpec((1, D_MODEL), lambda i, e, j: (0, 0)),
            pl.BlockSpec((1, 1, D_MODEL), lambda i, e, j: (i // per, 0, 0)),
            pl.BlockSpec((1, 1, D_MODEL), lambda i, e, j: (i // per, 0, 0)),
            pl.BlockSpec((D_MODEL, LANES), lambda i, e, j: (0, 0)),
            pl.BlockSpec((D_MODEL, LANES), lambda i, e, j: (0, 0)),
            pl.BlockSpec((1, D_MODEL, ff_tile), lambda i, e, j: (e, 0, j)),
            pl.BlockSpec((1, D_MODEL, ff_tile), lambda i, e, j: (e, 0, j)),
            pl.BlockSpec((1, ff_tile, D_MODEL), lambda i, e, j: (e, j, 0)),
            pl.BlockSpec((1, 1, D_MODEL), lambda i, e, j: (i // per, 0, 0)),
            pl.BlockSpec((1, D_MODEL), lambda i, e, j: (0, 0)),
        ],
        out_specs=pl.BlockSpec((tm, D_MODEL), lambda i, e, j: (i, 0)),
        out_shape=jax.ShapeDtypeStruct((t, D_MODEL), F32),
        scratch_shapes=[
            pltpu.VMEM((tm, D_MODEL), BF16),
            pltpu.VMEM((tm, D_MODEL), F32),
            pltpu.VMEM((tm, LANES), F32),
            pltpu.VMEM((tm, LANES), F32),
            pltpu.VMEM((LANES, tm), F32),
            pltpu.VMEM((LANES, tm), F32),
            pltpu.VMEM((passes * MOE_CAP, D_MODEL), BF16),
            pltpu.VMEM((passes * MOE_CAP, D_MODEL), F32),
            pltpu.SMEM((N_EXPERTS,), jnp.int32),
        ],
        compiler_params=_params(("parallel", "arbitrary", "arbitrary")),
        name="moe",
    )(x, g, shift, scale, r_hi, r_lo, w_gate, w_up, w_down, gate, npost)


MOE_ROWS = MOE_TILE * TOP_K + N_EXPERTS * MOE_GRAIN
MOE_META = 2 * N_EXPERTS
MOE_PASS_ROWS = 256


def _moe_dispatch_kernel(x_ref, g_ref, sh_ref, sc_ref, rh_ref, rl_ref, xs_ref, route_ref, meta_ref):
    tm = x_ref.shape[0]
    h = _rms(x_ref[...], g_ref[...]) * (1.0 + sc_ref[0]) + sh_ref[0]
    hi = h.astype(BF16)
    lo = (h - hi.astype(F32)).astype(BF16)
    logits = _dot(hi, rh_ref[...]) + _dot(lo, rh_ref[...]) + _dot(hi, rl_ref[...])
    lane = lax.broadcasted_iota(jnp.int32, logits.shape, 1)
    logits = jnp.where(lane < N_EXPERTS, logits, -jnp.inf)
    m1 = jnp.max(logits, axis=-1, keepdims=True)
    i1 = jnp.min(jnp.where(logits == m1, lane, LANES), axis=-1, keepdims=True)
    rest = jnp.where(lane == i1, -jnp.inf, logits)
    m2 = jnp.max(rest, axis=-1, keepdims=True)
    i2 = jnp.min(jnp.where(rest == m2, lane, LANES), axis=-1, keepdims=True)
    e2 = jnp.exp(m2 - m1)
    sel1 = jnp.where(lane == i1, 1.0, 0.0)
    sel2 = jnp.where(lane == i2, 1.0, 0.0)
    sel = sel1 + sel2
    sel1_t, sel2_t = sel1.T, sel2.T
    sel_t = sel1_t + sel2_t

    ti = lax.broadcasted_iota(jnp.int32, (tm, tm), 0)
    si = lax.broadcasted_iota(jnp.int32, (tm, tm), 1)
    rank_c = _dot(jnp.where(si < ti, 1.0, 0.0).astype(BF16), sel.astype(BF16))
    rank_r = _dot(sel_t.astype(BF16), jnp.where(ti < si, 1.0, 0.0).astype(BF16))
    blk_row = jnp.ceil(jnp.sum(sel, axis=0, keepdims=True) * (1.0 / MOE_GRAIN))
    blk_col = jnp.ceil(jnp.sum(sel_t, axis=1, keepdims=True) * (1.0 / MOE_GRAIN))
    ea = lax.broadcasted_iota(jnp.int32, (LANES, LANES), 0)
    eb = lax.broadcasted_iota(jnp.int32, (LANES, LANES), 1)
    off_row = _dot(jnp.broadcast_to(blk_row, (8, LANES)).astype(BF16),
                   jnp.where(ea < eb, 1.0, 0.0).astype(BF16))[0:1, :]
    off_col = _dot(jnp.where(eb < ea, 1.0, 0.0).astype(BF16),
                   jnp.broadcast_to(blk_col, (LANES, LANES)).astype(BF16))[:, 0:1]

    pos_c = off_row * MOE_GRAIN + rank_c
    pos1_c = jnp.sum(jnp.where(lane == i1, pos_c, 0.0), axis=-1, keepdims=True)
    pos2_c = jnp.sum(jnp.where(lane == i2, pos_c, 0.0), axis=-1, keepdims=True)
    route_ref[...] = (jnp.where(lane == 0, pos1_c, 0.0) + jnp.where(lane == 1, pos2_c, 0.0)
                      + jnp.where(lane == 2, 1.0 / (1.0 + e2), 0.0) + jnp.where(lane == 3, e2 / (1.0 + e2), 0.0))
    pos_r = off_col * MOE_GRAIN + rank_r
    pos1_r = jnp.sum(sel1_t * pos_r, axis=0, keepdims=True)
    pos2_r = jnp.sum(sel2_t * pos_r, axis=0, keepdims=True)
    half = MOE_ROWS // 2
    for c in range(2):
        slot = (lax.broadcasted_iota(jnp.int32, (half, tm), 0) + c * half).astype(F32)
        onehot = jnp.where(slot == pos1_r, 1.0, jnp.where(slot == pos2_r, 1.0, 0.0)).astype(BF16)
        xs_ref[c * half:(c + 1) * half, :] = _dot(onehot, hi).astype(BF16)

    lane8 = lax.broadcasted_iota(jnp.int32, (8, LANES), 1)
    blk8 = jnp.broadcast_to(blk_row, (8, LANES))
    off8 = pltpu.roll(jnp.broadcast_to(off_row, (8, LANES)), N_EXPERTS, 1)
    meta = jnp.where(lane8 < N_EXPERTS, blk8, jnp.where(lane8 < MOE_META, off8, 0.0))
    meta_ref[0] = meta.astype(jnp.int32)


def _moe_dispatch(x, g, shift, scale, r_hi, r_lo, rows_per_mod):
    t = x.shape[0]
    tm = MOE_TILE
    assert rows_per_mod % tm == 0
    per = rows_per_mod // tm
    n_t = t // tm
    return pl.pallas_call(
        _moe_dispatch_kernel,
        grid=(n_t,),
        in_specs=[
            pl.BlockSpec((tm, D_MODEL), lambda i: (i, 0)),
            pl.BlockSpec((1, D_MODEL), lambda i: (0, 0)),
            pl.BlockSpec((1, 1, D_MODEL), lambda i: (i // per, 0, 0)),
            pl.BlockSpec((1, 1, D_MODEL), lambda i: (i // per, 0, 0)),
            pl.BlockSpec((D_MODEL, LANES), lambda i: (0, 0)),
            pl.BlockSpec((D_MODEL, LANES), lambda i: (0, 0)),
        ],
        out_specs=[
            pl.BlockSpec((MOE_ROWS, D_MODEL), lambda i: (i, 0)),
            pl.BlockSpec((tm, LANES), lambda i: (i, 0)),
            pl.BlockSpec((1, 8, LANES), lambda i: (i, 0, 0)),
        ],
        out_shape=[
            jax.ShapeDtypeStruct((n_t * MOE_ROWS, D_MODEL), BF16),
            jax.ShapeDtypeStruct((t, LANES), F32),
            jax.ShapeDtypeStruct((n_t, 8, LANES), jnp.int32),
        ],
        compiler_params=_params(("parallel",)),
        name="moe_dispatch",
    )(x, g, shift, scale, r_hi, r_lo)


def _moe_experts_kernel(meta_ref, xs_ref, wg_ref, wu_ref, wd_ref, out_ref, *, ff_tile):
    e = pl.program_id(0)
    i = pl.program_id(1)
    out_ref[...] = xs_ref[...]
    n_blk = meta_ref[i * MOE_META + e]
    first = meta_ref[i * MOE_META + N_EXPERTS + e] * MOE_GRAIN
    per_pass = MOE_PASS_ROWS // MOE_GRAIN
    n_full = n_blk // per_pass
    rem = n_blk - n_full * per_pass
    ff = wg_ref.shape[2]

    def run(start, size):
        rows = pl.ds(pl.multiple_of(start, MOE_GRAIN), size)
        xin = xs_ref[rows, :]
        y = jnp.zeros((size, D_MODEL), F32)
        for j in range(ff // ff_tile):
            cols = slice(j * ff_tile, (j + 1) * ff_tile)
            y += _swiglu(xin, wg_ref[0, :, cols], wu_ref[0, :, cols], wd_ref[0, cols, :])
        out_ref[rows, :] = y.astype(out_ref.dtype)

    def full_pass(p, _):
        run(first + p * MOE_PASS_ROWS, MOE_PASS_ROWS)
        return 0

    lax.fori_loop(0, n_full, full_pass, 0)
    for blocks in range(1, per_pass):
        @pl.when(rem == blocks)
        def _(blocks=blocks):
            run(first + n_full * MOE_PASS_ROWS, blocks * MOE_GRAIN)


MOE_FF_TILE = 512


def _moe_experts(xs, meta, w_gate, w_up, w_down):
    n_e, _, ff = w_gate.shape
    n_t = xs.shape[0] // MOE_ROWS
    assert ff % MOE_FF_TILE == 0
    resident = pl.Buffered(1)
    grid_spec = pltpu.PrefetchScalarGridSpec(
        num_scalar_prefetch=1,
        grid=(n_e, n_t),
        in_specs=[
            pl.BlockSpec((MOE_ROWS, D_MODEL), lambda e, i, m: (i, 0)),
            pl.BlockSpec((1, D_MODEL, ff), lambda e, i, m: (e, 0, 0), pipeline_mode=resident),
            pl.BlockSpec((1, D_MODEL, ff), lambda e, i, m: (e, 0, 0), pipeline_mode=resident),
            pl.BlockSpec((1, ff, D_MODEL), lambda e, i, m: (e, 0, 0), pipeline_mode=resident),
        ],
        out_specs=pl.BlockSpec((MOE_ROWS, D_MODEL), lambda e, i, m: (i, 0)),
    )
    return pl.pallas_call(
        functools.partial(_moe_experts_kernel, ff_tile=MOE_FF_TILE),
        grid_spec=grid_spec,
        out_shape=jax.ShapeDtypeStruct(xs.shape, xs.dtype),
        input_output_aliases={1: 0},
        compiler_params=_params(("arbitrary", "arbitrary")),
        name="moe_experts",
    )(meta, xs, w_gate, w_up, w_down)


def _moe_combine_kernel(ys_ref, route_ref, x_ref, gate_ref, npost_ref, o_ref):
    tm = x_ref.shape[0]
    lane = lax.broadcasted_iota(jnp.int32, (tm, LANES), 1)
    route = route_ref[...]
    pos1, pos2, w1, w2 = (jnp.sum(jnp.where(lane == k, route, 0.0), axis=-1, keepdims=True) for k in range(4))
    width = 512
    acc = jnp.zeros((tm, D_MODEL), F32)
    for c in range(MOE_ROWS // width):
        slot = (lax.broadcasted_iota(jnp.int32, (tm, width), 1) + c * width).astype(F32)
        wmat = jnp.where(slot == pos1, w1, 0.0) + jnp.where(slot == pos2, w2, 0.0)
        acc += _dot(wmat.astype(BF16), ys_ref[c * width:(c + 1) * width, :])
    o_ref[...] = x_ref[...] + gate_ref[0] * _rms(acc, npost_ref[...])


def _moe_combine(ys, route, x, gate, npost, rows_per_mod):
    t = x.shape[0]
    tm = MOE_TILE
    per = rows_per_mod // tm
    return pl.pallas_call(
        _moe_combine_kernel,
        grid=(t // tm,),
        in_specs=[
            pl.BlockSpec((MOE_ROWS, D_MODEL), lambda i: (i, 0)),
            pl.BlockSpec((tm, LANES), lambda i: (i, 0)),
            pl.BlockSpec((tm, D_MODEL), lambda i: (i, 0)),
            pl.BlockSpec((1, 1, D_MODEL), lambda i: (i // per, 0, 0)),
            pl.BlockSpec((1, D_MODEL), lambda i: (0, 0)),
        ],
        out_specs=pl.BlockSpec((tm, D_MODEL), lambda i: (i, 0)),
        out_shape=jax.ShapeDtypeStruct((t, D_MODEL), F32),
        compiler_params=_params(("parallel",)),
        name="moe_combine",
    )(ys, route, x, gate, npost)


MOE_BLOCK = 512
MOE_DMA_WINDOW = 64


def _moe_plan(meta, n_t):
    per_tile = MOE_ROWS // MOE_GRAIN
    per_blk = MOE_BLOCK // MOE_GRAIN
    nblk = meta[:, 0, :N_EXPERTS]
    first = meta[:, 0, N_EXPERTS:MOE_META]
    tot = jnp.sum(nblk, axis=0)
    tot_pad = ((tot + per_blk - 1) // per_blk) * per_blk
    ends = jnp.cumsum(tot_pad)
    seg_dst = (ends - tot_pad)[None, :] + jnp.cumsum(nblk, axis=0) - nblk
    k = jnp.arange(per_tile, dtype=jnp.int32)[None, :, None]
    inside = (k >= first[:, None, :]) & (k < (first + nblk)[:, None, :])
    dst = jnp.sum(jnp.where(inside, seg_dst[:, None, :] + k - first[:, None, :], 0), axis=-1)
    used = jnp.any(inside, axis=-1).reshape(-1)
    dst = dst.reshape(-1).astype(jnp.int32)
    n_big = (n_t * per_tile + N_EXPERTS * per_blk) // per_blk
    big = jnp.arange(n_big, dtype=jnp.int32)[:, None] * per_blk
    expert = jnp.minimum(jnp.sum(big >= ends[None, :], axis=-1), N_EXPERTS - 1).astype(jnp.int32)
    valid = (big[:, 0] < ends[-1]).astype(jnp.int32)
    src = jnp.zeros((n_big * per_blk,), jnp.int32).at[jnp.where(used, dst, n_big * per_blk)].set(
        jnp.arange(dst.shape[0], dtype=jnp.int32), mode='drop')
    return jnp.where(used, dst, 0), jnp.concatenate([expert, valid, src]), n_big


def _moe_global_kernel(plan_ref, *refs, ff_tile):
    per_blk = MOE_BLOCK // MOE_GRAIN
    x_refs = refs[:per_blk]
    wg_ref, wu_ref, wd_ref, out_ref, xin_scr = refs[per_blk:]
    b = pl.program_id(0)
    valid = plan_ref[pl.num_programs(0) + b]
    ff = wg_ref.shape[2]

    @pl.when(valid == 1)
    def _():
        for k in range(per_blk):
            xin_scr[k * MOE_GRAIN:(k + 1) * MOE_GRAIN, :] = x_refs[k][...]
        xin = xin_scr[...]
        y = jnp.zeros((MOE_BLOCK, D_MODEL), F32)
        for j in range(ff // ff_tile):
            cols = slice(j * ff_tile, (j + 1) * ff_tile)
            y += _swiglu(xin, wg_ref[0, :, cols], wu_ref[0, :, cols], wd_ref[0, cols, :])
        out_ref[...] = y.astype(out_ref.dtype)

    @pl.when(valid == 0)
    def _():
        out_ref[...] = jnp.zeros_like(out_ref)


def _moe_experts_global(xs, plan, n_big, w_gate, w_up, w_down):
    n_e, _, ff = w_gate.shape
    per_blk = MOE_BLOCK // MOE_GRAIN
    resident = pl.Buffered(1)

    def src_spec(k):
        return pl.BlockSpec((MOE_GRAIN, D_MODEL), lambda b, p: (p[2 * n_big + b * per_blk + k], 0))

    grid_spec = pltpu.PrefetchScalarGridSpec(
        num_scalar_prefetch=1,
        grid=(n_big,),
        in_specs=[src_spec(k) for k in range(per_blk)] + [
            pl.BlockSpec((1, D_MODEL, ff), lambda b, p: (p[b], 0, 0), pipeline_mode=resident),
            pl.BlockSpec((1, D_MODEL, ff), lambda b, p: (p[b], 0, 0), pipeline_mode=resident),
            pl.BlockSpec((1, ff, D_MODEL), lambda b, p: (p[b], 0, 0), pipeline_mode=resident),
        ],
        out_specs=pl.BlockSpec((MOE_BLOCK, D_MODEL), lambda b, p: (b, 0)),
        scratch_shapes=[pltpu.VMEM((MOE_BLOCK, D_MODEL), BF16)],
    )
    return pl.pallas_call(
        functools.partial(_moe_global_kernel, ff_tile=MOE_FF_TILE),
        grid_spec=grid_spec,
        out_shape=jax.ShapeDtypeStruct((n_big * MOE_BLOCK, D_MODEL), BF16),
        compiler_params=_params(("arbitrary",)),
        name="moe_experts",
    )(plan, *([xs] * per_blk), w_gate, w_up, w_down)


def _moe_global(x, g, shift, scale, r_hi, r_lo, w_gate, w_up, w_down, gate, npost, rows_per_mod):
    xs, route, meta = _moe_dispatch(x, g, shift, scale, r_hi, r_lo, rows_per_mod)
    n_t = xs.shape[0] // MOE_ROWS
    block_map, plan, n_big = _moe_plan(meta, n_t)
    yg = _moe_experts_global(xs, plan, n_big, w_gate, w_up, w_down)
    return _moe_gather_combine(yg, block_map, route, x, gate, npost, rows_per_mod)


def _moe_gather_combine_kernel(map_ref, *refs):
    per_tile = MOE_ROWS // MOE_GRAIN
    y_refs = refs[:per_tile]
    route_ref, x_ref, gate_ref, npost_ref, o_ref, ys_scr = refs[per_tile:]
    del map_ref
    for k in range(per_tile):
        ys_scr[k * MOE_GRAIN:(k + 1) * MOE_GRAIN, :] = y_refs[k][...]
    _moe_combine_kernel(ys_scr, route_ref, x_ref, gate_ref, npost_ref, o_ref)


def _moe_gather_combine(yg, block_map, route, x, gate, npost, rows_per_mod):
    t = x.shape[0]
    tm = MOE_TILE
    per = rows_per_mod // tm
    per_tile = MOE_ROWS // MOE_GRAIN

    def y_spec(k):
        return pl.BlockSpec((MOE_GRAIN, D_MODEL), lambda i, m: (m[i * per_tile + k], 0))

    grid_spec = pltpu.PrefetchScalarGridSpec(
        num_scalar_prefetch=1,
        grid=(t // tm,),
        in_specs=[y_spec(k) for k in range(per_tile)] + [
            pl.BlockSpec((tm, LANES), lambda i, m: (i, 0)),
            pl.BlockSpec((tm, D_MODEL), lambda i, m: (i, 0)),
            pl.BlockSpec((1, 1, D_MODEL), lambda i, m: (i // per, 0, 0)),
            pl.BlockSpec((1, D_MODEL), lambda i, m: (0, 0)),
        ],
        out_specs=pl.BlockSpec((tm, D_MODEL), lambda i, m: (i, 0)),
        scratch_shapes=[pltpu.VMEM((MOE_ROWS, D_MODEL), BF16)],
    )
    return pl.pallas_call(
        _moe_gather_combine_kernel,
        grid_spec=grid_spec,
        out_shape=jax.ShapeDtypeStruct((t, D_MODEL), F32),
        compiler_params=_params(("arbitrary",)),
        name="moe_combine",
    )(block_map, *([yg] * per_tile), route, x, gate, npost)


def _moe_sorted(x, g, shift, scale, r_hi, r_lo, w_gate, w_up, w_down, gate, npost, rows_per_mod):
    xs, route, meta = _moe_dispatch(x, g, shift, scale, r_hi, r_lo, rows_per_mod)
    meta = meta[:, 0, :MOE_META].reshape(-1)
    ys = _moe_experts(xs, meta, w_gate, w_up, w_down)
    return _moe_combine(ys, route, x, gate, npost, rows_per_mod)


def _reorder_w_in(w):
    o = 4 * M_WIDTH
    gates = jnp.pad(w[:, o:o + M_GATES], ((0, 0), (0, LANES - M_GATES)))
    na = w[:, o + M_GATES:o + M_GATES + 3 * NA_WIDTH]
    u0 = o + M_GATES + 3 * NA_WIDTH
    main = jnp.concatenate([w[:, :o], w[:, u0 + S5_WIDTH:], na], axis=1)
    return main.astype(BF16), gates.astype(BF16), w[:, u0:u0 + S5_WIDTH].astype(BF16)


def _row(v):
    return v.reshape(1, -1).astype(F32)


def _token_mixers(x, xc, mx, mc, lp, rope, seq_len, ctx_len, ctx_out):
    bsz = x.shape[0] // seq_len
    w_main, w_gates, w_u = _reorder_w_in(lp['w_in'])
    g_pre = _row(lp['norm_mix_pre'])
    px, gx, ux = _inproj(x, g_pre, mx[0], mx[1], w_main, w_gates, w_u, seq_len)
    pc, gc, uc = _inproj(xc, g_pre, mc[0], mc[1], w_main, w_gates, w_u, xc.shape[0])

    conv_w = lp['m_conv_w'].astype(F32)
    conv_b = _row(lp['m_conv_b'])
    gate_b = jnp.pad(_row(lp['m_gate_b']), ((0, 0), (0, LANES - M_GATES)))
    m_norm = _row(lp['m_norm'])
    qk_c = _qkconv(pc, conv_w, conv_b, rope[0][:ctx_len], rope[1][:ctx_len], ctx_len, False)
    qk_x = _qkconv(px, conv_w, conv_b, rope[0], rope[1], seq_len, True)
    zero = _mlstm_zero_state(bsz)
    ym_c, fin_f, fin_b = _mlstm_bidir(qk_c, pc, gc, gate_b, m_norm, zero, zero, ctx_len)
    ym_x, _, _ = _mlstm_bidir(qk_x, px, gx, gate_b, m_norm, fin_f, fin_b, seq_len)

    yn_x = _na_attention(px, pc, _na_bias_table(lp['na_rpb']), seq_len, ctx_len)

    tables = _s5_tables(lp['s5_lam_re'], lp['s5_lam_im'], lp['s5_log_dt'], lp['s5_b_re'], lp['s5_b_im'],
                        lp['s5_c_re'], lp['s5_c_im'])
    zs = jnp.zeros((bsz, S5_GROUPS * S5_STATE), F32)
    ys_c, fin_s = _s5_branch(uc, tables, ctx_len, (zs, zs, zs, zs), ctx_out)
    ys_x, _ = _s5_branch(ux, tables, seq_len, fin_s, True)

    wts = (_row(lp['s5_d']), lp['s5_glu_w'].astype(BF16), _row(lp['s5_glu_b']),
           lp['w_branch_m'].astype(BF16), lp['w_branch_na'].astype(BF16), lp['w_branch_s5'].astype(BF16),
           lp['w_out'].astype(BF16))
    g_post = _row(lp['norm_mix_post'])
    x = _merge(x, ym_x, yn_x, ys_x, ux, px, wts, mx[2], g_post, seq_len)
    if ctx_out:
        yn_c = _ctx_attention(pc, ctx_len)
        xc = _merge(xc, ym_c, yn_c, ys_c, uc, pc, wts, mc[2], g_post, xc.shape[0])
    return x, xc


def kernel(x, c, ctx, c_ctx, ada_w, ada_b, norm_mix_pre, norm_mix_post, norm_ffn_pre, norm_ffn_post, w_in, m_gate_b, m_conv_w, m_conv_b, m_norm, na_rpb, s5_lam_re, s5_lam_im, s5_log_dt, s5_b_re, s5_b_im, s5_c_re, s5_c_im, s5_d, s5_glu_w, s5_glu_b, w_branch_m, w_branch_na, w_branch_s5, w_out, ffn_w_gate, ffn_w_up, ffn_w_down, moe_router, moe_w_gate, moe_w_up, moe_w_down):
    bsz, seq_len, d = x.shape
    ctx_len = ctx.shape[1]
    depth = w_in.shape[0]
    assert d == D_MODEL and seq_len % (GRID_W * NA_ROWS) == 0 and ctx_len % M_CHUNK == 0
    rope = _rope_tables(seq_len)
    xf = x.reshape(bsz * seq_len, d)
    xc = ctx.reshape(bsz * ctx_len, d)
    cvecs = jnp.zeros((16, d), F32).at[:bsz].set(c).at[bsz].set(c_ctx)
    for l in range(depth):
        last = l == depth - 1
        mod = _adaln(cvecs, ada_w[l].astype(BF16), _row(ada_b[l]))
        mx = [mod[:bsz, k * d:(k + 1) * d].reshape(bsz, 1, d) for k in range(N_MOD)]
        mc = [mod[bsz:bsz + 1, k * d:(k + 1) * d].reshape(1, 1, d) for k in range(N_MOD)]
        lp = {
            'norm_mix_pre': norm_mix_pre[l], 'norm_mix_post': norm_mix_post[l],
            'w_in': w_in[l], 'm_gate_b': m_gate_b[l], 'm_conv_w': m_conv_w[l], 'm_conv_b': m_conv_b[l],
            'm_norm': m_norm[l], 'na_rpb': na_rpb[l], 's5_lam_re': s5_lam_re[l], 's5_lam_im': s5_lam_im[l],
            's5_log_dt': s5_log_dt[l], 's5_b_re': s5_b_re[l], 's5_b_im': s5_b_im[l], 's5_c_re': s5_c_re[l],
            's5_c_im': s5_c_im[l], 's5_d': s5_d[l], 's5_glu_w': s5_glu_w[l], 's5_glu_b': s5_glu_b[l],
            'w_branch_m': w_branch_m[l], 'w_branch_na': w_branch_na[l], 'w_branch_s5': w_branch_s5[l],
            'w_out': w_out[l],
        }
        xf, xc = _token_mixers(xf, xc, mx, mc, lp, rope, seq_len, ctx_len, not last)
        j = l // 2
        g_pre, g_post = _row(norm_ffn_pre[l]), _row(norm_ffn_post[l])
        if l % 2 == 0:
            wg, wu, wd = ffn_w_gate[j].astype(BF16), ffn_w_up[j].astype(BF16), ffn_w_down[j].astype(BF16)

            def channel(h, m, rows):
                return _ffn(h, g_pre, m[3], m[4], wg, wu, wd, m[5], g_post, rows)
        else:
            wg, wu, wd = moe_w_gate[j].astype(BF16), moe_w_up[j].astype(BF16), moe_w_down[j].astype(BF16)
            router = jnp.pad(moe_router[j].astype(F32), ((0, 0), (0, LANES - N_EXPERTS)))
            r_hi = router.astype(BF16)
            r_lo = (router - r_hi.astype(F32)).astype(BF16)

            def channel(h, m, rows):
                return _moe_global(h, g_pre, m[3], m[4], r_hi, r_lo, wg, wu, wd, m[5], g_post, rows)
        xf = channel(xf, mx, seq_len)
        if not last:
            xc = channel(xc, mc, xc.shape[0])
    return xf.reshape(bsz, seq_len, d)
```

```python
import functools
import math

import jax
import jax.numpy as jnp
from jax import lax
from jax.experimental import pallas as pl
from jax.experimental.pallas import tpu as pltpu

F32 = jnp.float32
BF16 = jnp.bfloat16

D_MODEL = 1024
EPS = 1e-6
N_MOD = 6
GRID_W = 64

M_HEADS = 4
M_HEAD_DIM = 128
M_WIDTH = M_HEADS * M_HEAD_DIM
M_GATES = 4 * M_HEADS
M_CHUNK = 128
ROPE_BASE = 10000.0

NA_HEADS = 8
NA_HEAD_DIM = 64
NA_WIDTH = NA_HEADS * NA_HEAD_DIM
WIN_R = 8
WIN_C = 16
NEG_BIG = -1e30
LOG2E = math.log2(math.e)

S5_GROUP = 16
S5_WIDTH = 512
S5_GROUPS = S5_WIDTH // S5_GROUP
S5_STATE = 64
S5_MAX_RE = -1e-4

LANES = 128
VMEM_LIMIT = 52 * 1024 * 1024

S5_CHUNK = 16
S5_QUAD_GROUPS = LANES // S5_GROUP
S5_QUADS = S5_GROUPS // S5_QUAD_GROUPS
S5_QCOLS = S5_CHUNK * LANES
S5_QSTATE = S5_QUAD_GROUPS * S5_STATE

N_EXPERTS = 8
TOP_K = 2
MOE_TILE = 1024
MOE_CAP = 256
MOE_GRAIN = 64

COL_QK, COL_V, COL_O, COL_NAQ, COL_NAK, COL_NAV = 0, 2, 3, 10, 11, 12
COL_GATE = 2
IN_MAIN = 6656


def _params(sem):
    return pltpu.CompilerParams(dimension_semantics=sem, vmem_limit_bytes=VMEM_LIMIT)


def _dot(a, b):
    return jnp.dot(a, b, preferred_element_type=F32)


def _dot_nt(a, b):
    return lax.dot_general(a, b, (((1,), (1,)), ((), ())), preferred_element_type=F32)


def _split3(x):
    hi = x.astype(BF16)
    r1 = x - hi.astype(F32)
    mid = r1.astype(BF16)
    lo = (r1 - mid.astype(F32)).astype(BF16)
    return hi, mid, lo


def _rms(x, g):
    return x * lax.rsqrt(jnp.mean(x * x, axis=-1, keepdims=True) + EPS) * g


def _inproj_kernel(x_ref, g_ref, sh_ref, sc_ref, w_ref, wg_ref, wu_ref, o_ref, og_ref, ou_ref, *, tn):
    h = _rms(x_ref[...], g_ref[...]) * (1.0 + sc_ref[0]) + sh_ref[0]
    hb = h.astype(BF16)
    og_ref[...] = _dot(hb, wg_ref[...])
    ou_ref[...] = _dot(hb, wu_ref[...])
    for c in range(IN_MAIN // tn):
        cols = slice(c * tn, (c + 1) * tn)
        o_ref[:, cols] = _dot(hb, w_ref[:, cols]).astype(o_ref.dtype)


def _inproj(x, g, shift, scale, w_main, w_gates, w_u, rows_per_mod):
    t = x.shape[0]
    tm = min(512, rows_per_mod)
    tn = IN_MAIN // 4
    assert rows_per_mod % tm == 0 and t % tm == 0 and tn % LANES == 0
    per = rows_per_mod // tm
    resident = pl.Buffered(1)
    return pl.pallas_call(
        functools.partial(_inproj_kernel, tn=tn),
        grid=(t // tm,),
        in_specs=[
            pl.BlockSpec((tm, D_MODEL), lambda i: (i, 0)),
            pl.BlockSpec((1, D_MODEL), lambda i: (0, 0)),
            pl.BlockSpec((1, 1, D_MODEL), lambda i: (i // per, 0, 0)),
            pl.BlockSpec((1, 1, D_MODEL), lambda i: (i // per, 0, 0)),
            pl.BlockSpec((D_MODEL, IN_MAIN), lambda i: (0, 0), pipeline_mode=resident),
            pl.BlockSpec((D_MODEL, LANES), lambda i: (0, 0), pipeline_mode=resident),
            pl.BlockSpec((D_MODEL, S5_WIDTH), lambda i: (0, 0), pipeline_mode=resident),
        ],
        out_specs=[
            pl.BlockSpec((tm, IN_MAIN), lambda i: (i, 0)),
            pl.BlockSpec((tm, LANES), lambda i: (i, 0)),
            pl.BlockSpec((tm, S5_WIDTH), lambda i: (i, 0)),
        ],
        out_shape=[jax.ShapeDtypeStruct((t, IN_MAIN), BF16), jax.ShapeDtypeStruct((t, LANES), F32),
                   jax.ShapeDtypeStruct((t, S5_WIDTH), F32)],
        compiler_params=_params(("parallel",)),
        name="inproj",
    )(x, g, shift, scale, w_main, w_gates, w_u)


def _adaln_kernel(c_ref, w_ref, b_ref, o_ref):
    c = c_ref[...]
    s = c * jax.nn.sigmoid(c)
    o_ref[...] = _dot(s.astype(BF16), w_ref[...]) + b_ref[...]


def _adaln(cvecs, w, b):
    n = w.shape[1]
    tn = 1024
    return pl.pallas_call(
        _adaln_kernel,
        grid=(n // tn,),
        in_specs=[
            pl.BlockSpec(cvecs.shape, lambda j: (0, 0)),
            pl.BlockSpec((D_MODEL, tn), lambda j: (0, j)),
            pl.BlockSpec((1, tn), lambda j: (0, j)),
        ],
        out_specs=pl.BlockSpec((cvecs.shape[0], tn), lambda j: (0, j)),
        out_shape=jax.ShapeDtypeStruct((cvecs.shape[0], n), F32),
        compiler_params=_params(("parallel",)),
        name="adaln",
    )(cvecs, w, b)


HALO = 16


def _swap32(x):
    lane = lax.broadcasted_iota(jnp.int32, x.shape, 1)
    fwd = pltpu.roll(x, 96, 1)
    bwd = pltpu.roll(x, 32, 1)
    return jnp.where((lane % 64) < 32, fwd, bwd)


def _qkconv_kernel(x_ref, pv_ref, nx_ref, w_ref, b_ref, cos_ref, sin_ref, o_ref, *, tiles_per_seq, rope):
    i = pl.program_id(0)
    tm = x_ref.shape[0]
    x = x_ref[...].astype(F32)
    first = (i % tiles_per_seq) == 0
    last = (i % tiles_per_seq) == tiles_per_seq - 1
    prev_row = jnp.where(first, 0.0, pv_ref[HALO - 1:HALO, :].astype(F32))
    next_row = jnp.where(last, 0.0, nx_ref[0:1, :].astype(F32))
    row = lax.broadcasted_iota(jnp.int32, x.shape, 0)
    xp = jnp.where(row == 0, prev_row, pltpu.roll(x, 1, 0))
    xn = jnp.where(row == tm - 1, next_row, pltpu.roll(x, tm - 1, 0))
    y = xp * w_ref[0:1, :] + x * w_ref[1:2, :] + xn * w_ref[2:3, :] + b_ref[...]
    y = y * jax.nn.sigmoid(y)
    kscale = M_HEAD_DIM ** -0.5
    for hh in range(2 * M_HEADS):
        ys = y[:, hh * LANES:(hh + 1) * LANES]
        if rope:
            ys = ys * cos_ref[...] + _swap32(ys) * sin_ref[...]
        if hh >= M_HEADS:
            ys = ys * kscale
        o_ref[:, hh * LANES:(hh + 1) * LANES] = ys.astype(o_ref.dtype)


def _qkconv(px, conv_w, conv_b, cos_t, sin_t, seq_len, rope):
    t = px.shape[0]
    tm = min(512, seq_len)
    assert seq_len % tm == 0
    tiles = seq_len // tm
    hb = tm // HALO
    nh = t // HALO
    return pl.pallas_call(
        functools.partial(_qkconv_kernel, tiles_per_seq=tiles, rope=rope),
        grid=(t // tm,),
        in_specs=[
            pl.BlockSpec((tm, 2 * M_WIDTH), lambda i: (i, 0)),
            pl.BlockSpec((HALO, 2 * M_WIDTH), lambda i: (jnp.maximum(i * hb - 1, 0), 0)),
            pl.BlockSpec((HALO, 2 * M_WIDTH), lambda i: (jnp.minimum((i + 1) * hb, nh - 1), 0)),
            pl.BlockSpec((3, 2 * M_WIDTH), lambda i: (0, 0)),
            pl.BlockSpec((1, 2 * M_WIDTH), lambda i: (0, 0)),
            pl.BlockSpec((tm, LANES), lambda i: (i % tiles, 0)),
            pl.BlockSpec((tm, LANES), lambda i: (i % tiles, 0)),
        ],
        out_specs=pl.BlockSpec((tm, 2 * M_WIDTH), lambda i: (i, 0)),
        out_shape=jax.ShapeDtypeStruct((t, 2 * M_WIDTH), BF16),
        compiler_params=_params(("parallel",)),
        name="qkconv",
    )(px, px, px, conv_w, conv_b, cos_t, sin_t)


def _rope_tables(n):
    pos = jnp.arange(n, dtype=jnp.int32)
    row = (pos // GRID_W).astype(F32)
    col = (pos % GRID_W).astype(F32)
    n_freq = M_HEAD_DIM // 4
    inv = ROPE_BASE ** (-jnp.arange(n_freq, dtype=F32) / n_freq)
    ar, ac = row[:, None] * inv, col[:, None] * inv
    cos_t = jnp.concatenate([jnp.cos(ar), jnp.cos(ar), jnp.cos(ac), jnp.cos(ac)], axis=-1)
    sin_t = jnp.concatenate([-jnp.sin(ar), jnp.sin(ar), -jnp.sin(ac), jnp.sin(ac)], axis=-1)
    return cos_t, sin_t


def _mlstm_kernel(*refs, reverse, final, nb):
    if final:
        (qk_ref, v_ref, g_ref, gb_ref, c0_ref, n0_ref, m0_ref, hp_ref, op_ref, nw_ref,
         out_ref, cf_ref, nf_ref, mf_ref, c_scr, n_scr, m_scr, qk_scr, qc_scr, qn_scr, s_scr, vt_scr, vu_scr, dec_scr) = refs
    else:
        (qk_ref, v_ref, g_ref, gb_ref, c0_ref, n0_ref, m0_ref,
         out_ref, cf_ref, nf_ref, mf_ref, c_scr, n_scr, m_scr, qk_scr, qc_scr, qn_scr, s_scr, vt_scr, vu_scr, dec_scr) = refs
    c_idx = pl.program_id(1)
    n_c = pl.num_programs(1)
    L = M_CHUNK

    @pl.when(c_idx == 0)
    def _():
        c_scr[...] = c0_ref[...]
        n_scr[...] = n0_ref[...]
        m_scr[...] = m0_ref[...]

    ti = lax.broadcasted_iota(jnp.int32, (L, L), 0)
    si = lax.broadcasted_iota(jnp.int32, (L, L), 1)
    keep = (si >= ti) if reverse else (si <= ti)
    tri = jnp.where(keep, 1.0, 0.0).astype(BF16)
    i_off = 2 * M_HEADS if reverse else 0
    f_off = i_off + M_HEADS
    end = 0 if reverse else L - 1

    units = [(a, b) for a in range(nb) for b in range(M_HEADS)]

    def head(ref, gi, h, base=0):
        return ref[gi, :, base + h * LANES:base + (h + 1) * LANES]

    keep_t = (ti >= si) if reverse else (ti <= si)

    for u, (gi, h) in enumerate(units):
        q = head(qk_ref, gi, h)
        qk_scr[u] = _dot_nt(head(qk_ref, gi, h, M_WIDTH), q)
        qc_scr[u] = _dot_nt(c_scr[gi, h].astype(BF16), q)
        qn_scr[u] = _dot_nt(jnp.broadcast_to(n_scr[gi, h], (8, LANES)).astype(BF16), q)

    for u, (gi, h) in enumerate(units):
        if h == 0:
            g = g_ref[gi] + gb_ref[...]
            f_hi, f_mid, f_lo = _split3(jax.nn.log_sigmoid(g))
            bsum = _dot(tri, f_hi) + _dot(tri, f_mid) + _dot(tri, f_lo)
            g_t = g.T
            b_t = bsum.T
        m_prev = m_scr[gi, h][:, 0:1]
        b_row = b_t[f_off + h:f_off + h + 1, :]
        i_row = g_t[i_off + h:i_off + h + 1, :]
        b_end = b_row[:, end:end + 1]
        d_col = g[:, i_off + h:i_off + h + 1] - bsum[:, f_off + h:f_off + h + 1]

        log_w = jnp.where(keep_t, b_row + d_col, -jnp.inf)
        carry_log = b_row + m_prev
        m_t = jnp.maximum(carry_log, jnp.max(log_w, axis=0, keepdims=True))
        s_t = qk_scr[u] * jnp.exp(log_w - m_t)
        c_scale = jnp.exp(carry_log - m_t)
        den = jnp.sum(s_t, axis=0, keepdims=True) + c_scale * qn_scr[u][0:1, :]
        inv = 1.0 / jnp.maximum(jnp.abs(den), jnp.exp(-m_t))
        s_scr[u] = (s_t * inv).astype(BF16)
        qc_scr[u] = qc_scr[u] * (c_scale * inv)

        m_new = jnp.maximum(b_end + m_prev, jnp.max(b_end - b_row + i_row, axis=-1, keepdims=True))
        u_row = jnp.exp(b_end - b_row + i_row - m_new)
        decay = jnp.exp(b_end + m_prev - m_new)
        v_t = head(v_ref, gi, h).astype(F32).T
        vt_scr[u] = v_t.astype(BF16)
        vu_scr[u] = (v_t * u_row).astype(BF16)
        u8 = jnp.broadcast_to(u_row, (8, L)).astype(BF16)
        n_scr[gi, h] = decay * n_scr[gi, h] + _dot(u8, head(qk_ref, gi, h, M_WIDTH))[0:1, :]
        m_scr[gi, h] = jnp.broadcast_to(m_new, (1, LANES))
        dec_scr[u] = jnp.broadcast_to(decay, (1, LANES))

    for u, (gi, h) in enumerate(units):
        qc_scr[u] = _dot(vt_scr[u], s_scr[u]) + qc_scr[u]
        c_scr[gi, h] = dec_scr[u][:, 0:1] * c_scr[gi, h] + _dot(vu_scr[u], head(qk_ref, gi, h, M_WIDTH))

    for u, (gi, h) in enumerate(units):
        rows = slice(h * M_HEAD_DIM, (h + 1) * M_HEAD_DIM)
        if final:
            hs = qc_scr[u] + hp_ref[gi, rows, :]
            mu = jnp.mean(hs, axis=0, keepdims=True)
            var = jnp.mean(jnp.square(hs - mu), axis=0, keepdims=True)
            hn = ((hs - mu) * lax.rsqrt(var + EPS)).T
            y = hn * nw_ref[:, h * LANES:(h + 1) * LANES] * jax.nn.sigmoid(head(op_ref, gi, h).astype(F32))
            out_ref[gi, :, h * LANES:(h + 1) * LANES] = y.astype(out_ref.dtype)
        else:
            out_ref[gi, rows, :] = qc_scr[u]

    @pl.when(c_idx == n_c - 1)
    def _():
        cf_ref[...] = c_scr[...]
        nf_ref[...] = n_scr[...]
        mf_ref[...] = m_scr[...]


def _mlstm_scan(qk, px, gates, gate_b, state, seq_len, reverse, hprev=None, norm_w=None):
    t = qk.shape[0]
    bsz = t // seq_len
    n_c = seq_len // M_CHUNK
    final = hprev is not None
    nb = 4 if bsz % 4 == 0 else (2 if bsz % 2 == 0 else 1)
    qk, px, gates = (a.reshape(bsz, seq_len, a.shape[1]) for a in (qk, px, gates))

    def chunk(c):
        return (n_c - 1 - c) if reverse else c

    state_specs = [
        pl.BlockSpec((nb, M_HEADS, M_HEAD_DIM, M_HEAD_DIM), lambda b, c: (b, 0, 0, 0)),
        pl.BlockSpec((nb, M_HEADS, 1, LANES), lambda b, c: (b, 0, 0, 0)),
        pl.BlockSpec((nb, M_HEADS, 1, LANES), lambda b, c: (b, 0, 0, 0)),
    ]
    in_specs = [
        pl.BlockSpec((nb, M_CHUNK, 2 * M_WIDTH), lambda b, c: (b, chunk(c), 0)),
        pl.BlockSpec((nb, M_CHUNK, M_WIDTH), lambda b, c: (b, chunk(c), COL_V)),
        pl.BlockSpec((nb, M_CHUNK, LANES), lambda b, c: (b, chunk(c), 0)),
        pl.BlockSpec((1, LANES), lambda b, c: (0, 0)),
    ] + state_specs
    args = [qk, px, gates, gate_b, *state]
    t_spec = pl.BlockSpec((nb, M_WIDTH, M_CHUNK), lambda b, c: (b, 0, chunk(c)))
    if final:
        in_specs += [
            t_spec,
            pl.BlockSpec((nb, M_CHUNK, M_WIDTH), lambda b, c: (b, chunk(c), COL_O)),
            pl.BlockSpec((1, M_WIDTH), lambda b, c: (0, 0)),
        ]
        args += [hprev, px, norm_w]
        out_spec = pl.BlockSpec((nb, M_CHUNK, M_WIDTH), lambda b, c: (b, chunk(c), 0))
        out_shape = jax.ShapeDtypeStruct((bsz, seq_len, M_WIDTH), BF16)
    else:
        out_spec = t_spec
        out_shape = jax.ShapeDtypeStruct((bsz, M_WIDTH, seq_len), F32)
    units = nb * M_HEADS
    out, *fin = pl.pallas_call(
        functools.partial(_mlstm_kernel, reverse=reverse, final=final, nb=nb),
        grid=(bsz // nb, n_c),
        in_specs=in_specs,
        out_specs=[out_spec] + state_specs,
        out_shape=[
            out_shape,
            jax.ShapeDtypeStruct((bsz, M_HEADS, M_HEAD_DIM, M_HEAD_DIM), F32),
            jax.ShapeDtypeStruct((bsz, M_HEADS, 1, LANES), F32),
            jax.ShapeDtypeStruct((bsz, M_HEADS, 1, LANES), F32),
        ],
        scratch_shapes=[
            pltpu.VMEM((nb, M_HEADS, M_HEAD_DIM, M_HEAD_DIM), F32),
            pltpu.VMEM((nb, M_HEADS, 1, LANES), F32),
            pltpu.VMEM((nb, M_HEADS, 1, LANES), F32),
            pltpu.VMEM((units, M_CHUNK, M_CHUNK), F32),
            pltpu.VMEM((units, M_HEAD_DIM, M_CHUNK), F32),
            pltpu.VMEM((units, 8, M_CHUNK), F32),
            pltpu.VMEM((units, M_CHUNK, M_CHUNK), BF16),
            pltpu.VMEM((units, M_HEAD_DIM, M_CHUNK), BF16),
            pltpu.VMEM((units, M_HEAD_DIM, M_CHUNK), BF16),
            pltpu.VMEM((units, 1, LANES), F32),
        ],
        compiler_params=_params(("parallel", "arbitrary")),
        name="mlstm_rev" if reverse else "mlstm_fwd",
    )(*args)
    return [out.reshape(t, M_WIDTH) if final else out] + fin


def _mlstm_zero_state(bsz):
    return (jnp.zeros((bsz, M_HEADS, M_HEAD_DIM, M_HEAD_DIM), F32),
            jnp.zeros((bsz, M_HEADS, 1, LANES), F32),
            jnp.zeros((bsz, M_HEADS, 1, LANES), F32))


def _mlstm_bidir(qk, px, gates, gate_b, norm_w, state_f, state_b, seq_len):
    h_f, *fin_f = _mlstm_scan(qk, px, gates, gate_b, state_f, seq_len, False)
    y, *fin_b = _mlstm_scan(qk, px, gates, gate_b, state_b, seq_len, True, hprev=h_f, norm_w=norm_w)
    return y, tuple(fin_f), tuple(fin_b)


NA_ROWS = 8
NA_STEP = 4


def _na_kernel(q_ref, kp_ref, kc_ref, kn_ref, vp_ref, vc_ref, vn_ref, kx_ref, vx_ref, *rest, rows):
    bias_refs = rest[:NA_STEP]
    o_ref, kwin, vwin, s_scr, p_scr, l_scr = rest[NA_STEP:]
    step = pl.program_id(1)
    blk = NA_ROWS * GRID_W

    @pl.when(step % (NA_ROWS // NA_STEP) == 0)
    def _():
        kwin[0:blk, :] = kp_ref[...]
        kwin[blk:2 * blk, :] = kc_ref[...]
        kwin[2 * blk:3 * blk, :] = kn_ref[...]
        vwin[0:blk, :] = vp_ref[...]
        vwin[blk:2 * blk, :] = vc_ref[...]
        vwin[2 * blk:3 * blk, :] = vn_ref[...]

    g = step // (NA_ROWS // NA_STEP)
    n_loc = WIN_R * GRID_W
    lane = lax.broadcasted_iota(jnp.int32, (GRID_W, LANES), 1)
    scale = NA_HEAD_DIM ** -0.5
    n_ctx = kx_ref.shape[0]
    units = [(kk, hp) for kk in range(NA_STEP) for hp in range(NA_HEADS // 2)]
    offs = []
    for kk in range(NA_STEP):
        r = step * NA_STEP + kk
        r0 = jnp.clip(r - WIN_R // 2, 0, rows - WIN_R)
        offs.append(pl.multiple_of((r0 - NA_ROWS * g + NA_ROWS) * GRID_W, GRID_W))

    for u, (kk, hp) in enumerate(units):
        cols = slice(hp * LANES, (hp + 1) * LANES)
        q2 = (q_ref[kk * GRID_W:(kk + 1) * GRID_W, cols].astype(F32) * (scale * LOG2E)).astype(BF16)
        zero = jnp.zeros_like(q2)
        qs = jnp.concatenate([jnp.where(lane < NA_HEAD_DIM, q2, zero), jnp.where(lane < NA_HEAD_DIM, zero, q2)],
                             axis=0)
        s_scr[u, :, 0:n_loc] = _dot_nt(qs, kwin[pl.ds(offs[kk], n_loc), cols])
        s_scr[u, :, n_loc:n_loc + n_ctx] = _dot_nt(qs, kx_ref[:, cols])

    for u, (kk, hp) in enumerate(units):
        s_loc = s_scr[u, :, 0:n_loc] + bias_refs[kk][0, hp]
        s_ctx = s_scr[u, :, n_loc:n_loc + n_ctx]
        m = jnp.maximum(jnp.max(s_loc, axis=-1, keepdims=True), jnp.max(s_ctx, axis=-1, keepdims=True))
        p_loc = jnp.exp2(s_loc - m)
        p_ctx = jnp.exp2(s_ctx - m)
        inv = 1.0 / (jnp.sum(p_loc, axis=-1, keepdims=True) + jnp.sum(p_ctx, axis=-1, keepdims=True))
        p_scr[u, :, 0:n_loc] = p_loc.astype(BF16)
        p_scr[u, :, n_loc:n_loc + n_ctx] = p_ctx.astype(BF16)
        l_scr[u] = jnp.broadcast_to(inv, (2 * GRID_W, LANES))

    for u, (kk, hp) in enumerate(units):
        cols = slice(hp * LANES, (hp + 1) * LANES)
        o2 = _dot(p_scr[u, :, 0:n_loc], vwin[pl.ds(offs[kk], n_loc), cols])
        o2 = (o2 + _dot(p_scr[u, :, n_loc:n_loc + n_ctx], vx_ref[:, cols])) * l_scr[u]
        out = jnp.where(lane < NA_HEAD_DIM, o2[0:GRID_W], o2[GRID_W:2 * GRID_W])
        o_ref[kk * GRID_W:(kk + 1) * GRID_W, cols] = out.astype(o_ref.dtype)


def _na_bias_table(rpb):
    qc = jnp.arange(GRID_W)[:, None]
    kc = jnp.arange(GRID_W)[None, :]
    cs = jnp.clip(qc - WIN_C // 2, 0, GRID_W - WIN_C)
    ok = (kc >= cs) & (kc < cs + WIN_C)
    dc = jnp.clip(kc - qc + WIN_C - 1, 0, 2 * WIN_C - 2)
    dr = jnp.arange(WIN_R)[:, None] + jnp.arange(WIN_R)[None, :]
    tab = (rpb.astype(F32) * LOG2E)[:, dr][:, :, :, dc]
    tab = jnp.where(ok[None, None, None], tab, NEG_BIG)
    tab = jnp.transpose(tab, (1, 0, 3, 2, 4))
    return tab.reshape(WIN_R, NA_HEADS // 2, 2 * GRID_W, WIN_R * GRID_W)


def _na_attention(px, pc, bias, seq_len, ctx_len):
    t = px.shape[0]
    bsz = t // seq_len
    rows = seq_len // GRID_W
    n_g = rows // NA_ROWS
    blk = NA_ROWS * GRID_W

    steps = rows // NA_STEP
    per_blk = NA_ROWS // NA_STEP

    def kv_spec(col, shift):
        def imap(b, s):
            g = jnp.clip(s // per_blk + shift, 0, n_g - 1)
            return (b * n_g + g, col)
        return pl.BlockSpec((blk, NA_WIDTH), imap)

    def bias_spec(kk):
        def imap(b, s):
            r = s * NA_STEP + kk
            r0 = jnp.clip(r - WIN_R // 2, 0, rows - WIN_R)
            return (r0 - r + WIN_R - 1, 0, 0, 0)
        return pl.BlockSpec((1, NA_HEADS // 2, 2 * GRID_W, WIN_R * GRID_W), imap)

    return pl.pallas_call(
        functools.partial(_na_kernel, rows=rows),
        grid=(bsz, steps),
        in_specs=[
            pl.BlockSpec((NA_STEP * GRID_W, NA_WIDTH), lambda b, s: (b * steps + s, COL_NAQ)),
            kv_spec(COL_NAK, -1), kv_spec(COL_NAK, 0), kv_spec(COL_NAK, 1),
            kv_spec(COL_NAV, -1), kv_spec(COL_NAV, 0), kv_spec(COL_NAV, 1),
            pl.BlockSpec((ctx_len, NA_WIDTH), lambda b, s: (b, COL_NAK)),
            pl.BlockSpec((ctx_len, NA_WIDTH), lambda b, s: (b, COL_NAV)),
        ] + [bias_spec(kk) for kk in range(NA_STEP)],
        out_specs=pl.BlockSpec((NA_STEP * GRID_W, NA_WIDTH), lambda b, s: (b * steps + s, 0)),
        out_shape=jax.ShapeDtypeStruct((t, NA_WIDTH), BF16),
        scratch_shapes=[
            pltpu.VMEM((3 * blk, NA_WIDTH), BF16), pltpu.VMEM((3 * blk, NA_WIDTH), BF16),
            pltpu.VMEM((NA_STEP * NA_HEADS // 2, 2 * GRID_W, WIN_R * GRID_W + ctx_len), F32),
            pltpu.VMEM((NA_STEP * NA_HEADS // 2, 2 * GRID_W, WIN_R * GRID_W + ctx_len), BF16),
            pltpu.VMEM((NA_STEP * NA_HEADS // 2, 2 * GRID_W, LANES), F32),
        ],
        compiler_params=_params(("parallel", "arbitrary")),
        name="na_attn",
    )(px, px, px, px, px, px, px, pc, pc, *([bias] * NA_STEP))


def _ctx_attn_kernel(q_ref, k_ref, v_ref, o_ref):
    n = q_ref.shape[0]
    lane = lax.broadcasted_iota(jnp.int32, (n, LANES), 1)
    scale = NA_HEAD_DIM ** -0.5
    for hp in range(NA_HEADS // 2):
        cols = slice(hp * LANES, (hp + 1) * LANES)
        q2 = q_ref[:, cols] * scale
        k2 = k_ref[:, cols]
        v2 = v_ref[:, cols]
        outs = []
        for sub in range(2):
            sel = (lane < NA_HEAD_DIM) if sub == 0 else (lane >= NA_HEAD_DIM)
            s = _dot_nt(jnp.where(sel, q2, jnp.zeros_like(q2)), k2)
            p = jnp.exp(s - jnp.max(s, axis=-1, keepdims=True))
            outs.append(_dot(p.astype(BF16), v2) / jnp.sum(p, axis=-1, keepdims=True))
        o_ref[:, cols] = jnp.where(lane < NA_HEAD_DIM, outs[0], outs[1]).astype(o_ref.dtype)


def _ctx_attention(pc, ctx_len):
    t = pc.shape[0]
    return pl.pallas_call(
        _ctx_attn_kernel,
        grid=(t // ctx_len,),
        in_specs=[pl.BlockSpec((ctx_len, NA_WIDTH), lambda b, col=col: (b, col))
                  for col in (COL_NAQ, COL_NAK, COL_NAV)],
        out_specs=pl.BlockSpec((ctx_len, NA_WIDTH), lambda b: (b, 0)),
        out_shape=jax.ShapeDtypeStruct((t, NA_WIDTH), BF16),
        compiler_params=_params(("parallel",)),
        name="ctx_attn",
    )(pc, pc, pc)


def _cmul(a, b):
    return a[0] * b[0] - a[1] * b[1], a[0] * b[1] + a[1] * b[0]


def _s5_tables(lam_re, lam_im, log_dt, b_re, b_im, c_re, c_im):
    L, G, S, H = S5_CHUNK, S5_GROUPS, S5_STATE, S5_GROUP
    lr = jnp.minimum(lam_re.astype(F32), S5_MAX_RE)
    li = lam_im.astype(F32)
    dt = jnp.exp(log_dt.astype(F32))[..., None]
    zr, zi = lr * dt, li * dt
    d = jnp.arange(L + 1, dtype=F32)[None, :, None, None]
    mag = jnp.exp(zr[:, None] * d)
    pw = (mag * jnp.cos(zi[:, None] * d), mag * jnp.sin(zi[:, None] * d))
    lb = (pw[0][:, 1], pw[1][:, 1])
    den = lr * lr + li * li
    ratio = (((lb[0] - 1.0) * lr + lb[1] * li) / den, (lb[1] * lr - (lb[0] - 1.0) * li) / den)
    bb = _cmul((ratio[0][..., None], ratio[1][..., None]), (b_re.astype(F32), b_im.astype(F32)))
    cc = (c_re.astype(F32), c_im.astype(F32))

    def kern(x):
        cl = _cmul((cc[0][x][None], cc[1][x][None]), (pw[0][x, :L, :, None, :], pw[1][x, :L, :, None, :]))
        return jnp.einsum('dghp,gpj->dghj', cl[0], bb[0][x]) - jnp.einsum('dghp,gpj->dghj', cl[1], bb[1][x])

    def state_in(x, steps):
        p = (pw[0][x][steps][:, :, None, :], pw[1][x][steps][:, :, None, :])
        r = _cmul(p, (jnp.swapaxes(bb[0][x], 1, 2)[None], jnp.swapaxes(bb[1][x], 1, 2)[None]))
        return jnp.transpose(r[0], (1, 0, 2, 3)), jnp.transpose(r[1], (1, 0, 2, 3))

    def state_out(x, steps):
        p = (pw[0][x][steps][:, :, None, :], pw[1][x][steps][:, :, None, :])
        r = _cmul(p, (cc[0][x][None], cc[1][x][None]))
        return jnp.transpose(r[0], (1, 3, 0, 2)), jnp.transpose(r[1], (1, 3, 0, 2))

    steps = jnp.arange(L)
    wf = state_in(0, L - 1 - steps)
    wb = state_in(1, steps)
    vf = state_out(0, steps + 1)
    vb = state_out(1, L - steps)
    Q, A = S5_QUADS, S5_QUAD_GROUPS
    kd = jnp.stack([kern(0), kern(1)]).reshape(2, L, Q, A, H, H)
    kd = jnp.transpose(kd, (0, 2, 1, 5, 3, 4)).reshape(2, Q, L, H, A * H)
    wk = jnp.stack([wf[0], wf[1], wb[0], wb[1]]).reshape(4, Q, A // 2, 2, L, H, S)
    wd = jnp.transpose(wk, (0, 1, 4, 2, 5, 3, 6)).reshape(4, Q, L, (A // 2) * H, 2 * S)
    vk = jnp.stack([vf[0], -vf[1], vb[0], -vb[1]]).reshape(4, Q, A, S, L, H)
    vd = jnp.transpose(vk, (0, 1, 4, 3, 2, 5)).reshape(4, Q, L, S, A * H)
    a_re = pw[0][:, L].reshape(2, 1, G * S)
    a_im = pw[1][:, L].reshape(2, 1, G * S)
    return wd, kd, vd, a_re, a_im


def _s5_operand(u_ref):
    return jnp.concatenate([u_ref[:, s, :] for s in range(S5_CHUNK)], axis=1).astype(BF16)


def _s5_expand_w(wd_ref, w_scr):
    H, half = S5_GROUP, S5_STATE
    w_scr[...] = jnp.zeros_like(w_scr)
    lane = lax.broadcasted_iota(jnp.int32, (H, LANES), 1)
    for k in range(4):
        for s in range(S5_CHUNK):
            for c in range(S5_QUAD_GROUPS // 2):
                blk = wd_ref[k, 0, s, c * H:(c + 1) * H, :]
                cols = slice(k * S5_QSTATE + c * LANES, k * S5_QSTATE + (c + 1) * LANES)
                row = s * LANES + 2 * c * H
                w_scr[row:row + H, cols] = jnp.where(lane < half, blk, 0.0).astype(BF16)
                w_scr[row + H:row + 2 * H, cols] = jnp.where(lane >= half, blk, 0.0).astype(BF16)


def _s5_expand_mv(kd_ref, vd_ref, m_scr, v_scr):
    H, L = S5_GROUP, S5_CHUNK
    rg = lax.broadcasted_iota(jnp.int32, (LANES, LANES), 0) // H
    lg = lax.broadcasted_iota(jnp.int32, (LANES, LANES), 1) // H
    tiles = []
    for x in range(2):
        tiles.append([jnp.where(rg == lg, jnp.concatenate([kd_ref[x, 0, d]] * S5_QUAD_GROUPS, axis=0), 0.0)
                      for d in range(L)])
    for s in range(L):
        for t in range(L):
            blk = tiles[0][t - s] if t > s else (tiles[1][s - t] if s > t else tiles[0][0] + tiles[1][0])
            m_scr[s * LANES:(s + 1) * LANES, t * LANES:(t + 1) * LANES] = blk.astype(BF16)
    lg2 = lax.broadcasted_iota(jnp.int32, (S5_STATE, LANES), 1) // H
    for k in range(4):
        for t in range(L):
            blk = vd_ref[k, 0, t]
            for b in range(S5_QUAD_GROUPS):
                row = k * S5_QSTATE + b * S5_STATE
                v_scr[row:row + S5_STATE, t * LANES:(t + 1) * LANES] = jnp.where(lg2 == b, blk, 0.0).astype(BF16)


def _s5_local_kernel(u_ref, wd_ref, sfr_ref, sfi_ref, sbr_ref, sbi_ref, w_scr):
    @pl.when(pl.program_id(1) == 0)
    def _():
        _s5_expand_w(wd_ref, w_scr)

    s = _dot(_s5_operand(u_ref), w_scr[...])
    n = S5_QSTATE
    sfr_ref[...] = s[:, 0:n]
    sfi_ref[...] = s[:, n:2 * n]
    sbr_ref[...] = s[:, 2 * n:3 * n]
    sbi_ref[...] = s[:, 3 * n:4 * n]


def _s5_local_states(u3, wd):
    rows = u3.shape[0]
    tr = min(256, rows)
    assert rows % tr == 0
    out = jax.ShapeDtypeStruct((rows, S5_GROUPS * S5_STATE), F32)
    return pl.pallas_call(
        _s5_local_kernel,
        grid=(S5_QUADS, rows // tr),
        in_specs=[
            pl.BlockSpec((tr, S5_CHUNK, LANES), lambda q, i: (i, 0, q)),
            pl.BlockSpec((4, 1) + wd.shape[2:], lambda q, i: (0, q, 0, 0, 0)),
        ],
        out_specs=[pl.BlockSpec((tr, S5_QSTATE), lambda q, i: (i, q))] * 4,
        out_shape=[out] * 4,
        scratch_shapes=[pltpu.VMEM((S5_QCOLS, 4 * S5_QSTATE), BF16)],
        compiler_params=_params(("arbitrary", "arbitrary")),
        name="s5_local",
    )(u3, wd)


def _s5_scan_kernel(sr_ref, si_ref, ar_ref, ai_ref, h0r_ref, h0i_ref, hr_ref, hi_ref, fr_ref, fi_ref,
                    hr_scr, hi_scr, *, reverse):
    i = pl.program_id(1)
    n_i = pl.num_programs(1)
    steps = sr_ref.shape[1]

    @pl.when(i == 0)
    def _():
        hr_scr[...] = h0r_ref[...]
        hi_scr[...] = h0i_ref[...]

    ar = ar_ref[...]
    ai = ai_ref[...]

    def body(j, carry):
        hr, hi = carry
        c = (steps - 1 - j) if reverse else j
        hr_ref[:, c, :] = hr
        hi_ref[:, c, :] = hi
        nr = ar * hr - ai * hi + sr_ref[:, c, :]
        ni = ar * hi + ai * hr + si_ref[:, c, :]
        return nr, ni

    hr, hi = lax.fori_loop(0, steps, body, (hr_scr[...], hi_scr[...]))
    hr_scr[...] = hr
    hi_scr[...] = hi

    @pl.when(i == n_i - 1)
    def _():
        fr_ref[...] = hr
        fi_ref[...] = hi


def _s5_chunk_scan(s_re, s_im, a_re, a_im, h0_re, h0_im, bsz, reverse):
    _, n_c, width = s_re.shape
    tr = min(64, n_c)
    tc = 512
    n_i = n_c // tr

    def rmap(j, i):
        return (0, (n_i - 1 - i) if reverse else i, j)

    big = jax.ShapeDtypeStruct((bsz, n_c, width), F32)
    small = jax.ShapeDtypeStruct((bsz, width), F32)
    return pl.pallas_call(
        functools.partial(_s5_scan_kernel, reverse=reverse),
        grid=(width // tc, n_i),
        in_specs=[
            pl.BlockSpec((bsz, tr, tc), rmap), pl.BlockSpec((bsz, tr, tc), rmap),
            pl.BlockSpec((1, tc), lambda j, i: (0, j)), pl.BlockSpec((1, tc), lambda j, i: (0, j)),
            pl.BlockSpec((bsz, tc), lambda j, i: (0, j)), pl.BlockSpec((bsz, tc), lambda j, i: (0, j)),
        ],
        out_specs=[
            pl.BlockSpec((bsz, tr, tc), rmap), pl.BlockSpec((bsz, tr, tc), rmap),
            pl.BlockSpec((bsz, tc), lambda j, i: (0, j)), pl.BlockSpec((bsz, tc), lambda j, i: (0, j)),
        ],
        out_shape=[big, big, small, small],
        scratch_shapes=[pltpu.VMEM((bsz, tc), F32), pltpu.VMEM((bsz, tc), F32)],
        compiler_params=_params(("parallel", "arbitrary")),
        name="s5_scan_rev" if reverse else "s5_scan_fwd",
    )(s_re, s_im, a_re, a_im, h0_re, h0_im)


def _s5_out_kernel(u_ref, kd_ref, vd_ref, hfr_ref, hfi_ref, hbr_ref, hbi_ref, y_ref, m_scr, v_scr):
    @pl.when(pl.program_id(1) == 0)
    def _():
        _s5_expand_mv(kd_ref, vd_ref, m_scr, v_scr)

    n = S5_QSTATE
    y = _dot(_s5_operand(u_ref), m_scr[...])
    for k, h_ref in enumerate((hfr_ref, hfi_ref, hbr_ref, hbi_ref)):
        y += _dot(h_ref[...].astype(BF16), v_scr[k * n:(k + 1) * n, :])
    for t in range(S5_CHUNK):
        y_ref[:, t, :] = y[:, t * LANES:(t + 1) * LANES]


def _s5_outputs(u3, kd, vd, h_states):
    rows = u3.shape[0]
    tr = min(256, rows)
    return pl.pallas_call(
        _s5_out_kernel,
        grid=(S5_QUADS, rows // tr),
        in_specs=[
            pl.BlockSpec((tr, S5_CHUNK, LANES), lambda q, i: (i, 0, q)),
            pl.BlockSpec((2, 1) + kd.shape[2:], lambda q, i: (0, q, 0, 0, 0)),
            pl.BlockSpec((4, 1) + vd.shape[2:], lambda q, i: (0, q, 0, 0, 0)),
        ] + [pl.BlockSpec((tr, S5_QSTATE), lambda q, i: (i, q))] * 4,
        out_specs=pl.BlockSpec((tr, S5_CHUNK, LANES), lambda q, i: (i, 0, q)),
        out_shape=jax.ShapeDtypeStruct((rows, S5_CHUNK, S5_WIDTH), F32),
        scratch_shapes=[pltpu.VMEM((S5_QCOLS, S5_QCOLS), BF16), pltpu.VMEM((4 * S5_QSTATE, S5_QCOLS), BF16)],
        compiler_params=_params(("arbitrary", "arbitrary")),
        name="s5_out",
    )(u3, kd, vd, *h_states)


def _s5_branch(u, tables, seq_len, init, want_y):
    w_q, m_q, v_q, a_re, a_im = tables
    t = u.shape[0]
    bsz = t // seq_len
    n_c = seq_len // S5_CHUNK
    width = S5_GROUPS * S5_STATE
    u3 = u.reshape(t // S5_CHUNK, S5_CHUNK, S5_WIDTH)
    s_loc = [s.reshape(bsz, n_c, width) for s in _s5_local_states(u3, w_q)]
    hfr, hfi, ffr, ffi = _s5_chunk_scan(s_loc[0], s_loc[1], a_re[0], a_im[0], init[0], init[1], bsz, False)
    hbr, hbi, fbr, fbi = _s5_chunk_scan(s_loc[2], s_loc[3], a_re[1], a_im[1], init[2], init[3], bsz, True)
    y = None
    if want_y:
        h_states = [h.reshape(bsz * n_c, width) for h in (hfr, hfi, hbr, hbi)]
        y = _s5_outputs(u3, m_q, v_q, h_states).reshape(t, S5_WIDTH)
    return y, (ffr, ffi, fbr, fbi)


def _sigmoid(x):
    return 0.5 * jnp.tanh(0.5 * x) + 0.5


def _gelu_tanh(x):
    return 0.5 * x * (1.0 + jnp.tanh(math.sqrt(2.0 / math.pi) * (x + 0.044715 * (x * x * x))))


def _merge_kernel(x_ref, ym_ref, yn_ref, ys_ref, u_ref, gm_ref, gn_ref, gs_ref, d_ref, gw_ref, gb_ref,
                  wm_ref, wn_ref, ws_ref, wo_ref, gate_ref, npost_ref, o_ref):
    ys = ys_ref[...] + d_ref[...] * u_ref[...]
    g = _gelu_tanh(ys)
    gg = g * _sigmoid(_dot(g.astype(BF16), gw_ref[...]) + gb_ref[...])
    y = _sigmoid(gm_ref[...].astype(F32)) * _dot(ym_ref[...], wm_ref[...])
    y += _sigmoid(gn_ref[...].astype(F32)) * _dot(yn_ref[...], wn_ref[...])
    y += _sigmoid(gs_ref[...].astype(F32)) * _dot(gg.astype(BF16), ws_ref[...])
    out = _dot(y.astype(BF16), wo_ref[...])
    o_ref[...] = x_ref[...] + gate_ref[0] * _rms(out, npost_ref[...])


def _merge(x, ym, yn, ys, u, px, wts, gate, npost, rows_per_mod):
    t = x.shape[0]
    tm = min(512, rows_per_mod)
    per = rows_per_mod // tm
    s5_d, glu_w, glu_b, w_m, w_n, w_s, w_o = wts
    row = lambda i: (i, 0)
    const = lambda i: (0, 0)
    return pl.pallas_call(
        _merge_kernel,
        grid=(t // tm,),
        in_specs=[
            pl.BlockSpec((tm, D_MODEL), row),
            pl.BlockSpec((tm, M_WIDTH), row),
            pl.BlockSpec((tm, NA_WIDTH), row),
            pl.BlockSpec((tm, S5_WIDTH), row),
            pl.BlockSpec((tm, S5_WIDTH), row),
            pl.BlockSpec((tm, D_MODEL), lambda i: (i, COL_GATE)),
            pl.BlockSpec((tm, D_MODEL), lambda i: (i, COL_GATE + 1)),
            pl.BlockSpec((tm, D_MODEL), lambda i: (i, COL_GATE + 2)),
            pl.BlockSpec((1, S5_WIDTH), const),
            pl.BlockSpec((S5_WIDTH, S5_WIDTH), const),
            pl.BlockSpec((1, S5_WIDTH), const),
            pl.BlockSpec((M_WIDTH, D_MODEL), const),
            pl.BlockSpec((NA_WIDTH, D_MODEL), const),
            pl.BlockSpec((S5_WIDTH, D_MODEL), const),
            pl.BlockSpec((D_MODEL, D_MODEL), const),
            pl.BlockSpec((1, 1, D_MODEL), lambda i: (i // per, 0, 0)),
            pl.BlockSpec((1, D_MODEL), const),
        ],
        out_specs=pl.BlockSpec((tm, D_MODEL), row),
        out_shape=jax.ShapeDtypeStruct((t, D_MODEL), F32),
        compiler_params=_params(("parallel",)),
        name="merge",
    )(x, ym, yn, ys, u, px, px, px, s5_d, glu_w, glu_b, w_m, w_n, w_s, w_o, gate, npost)


def _swiglu(h, wg, wu, wd):
    a = _dot(h, wg)
    a = a * jax.nn.sigmoid(a) * _dot(h, wu)
    return _dot(a.astype(BF16), wd)


def _ffn_kernel(x_ref, g_ref, sh_ref, sc_ref, wg_ref, wu_ref, wd_ref, gate_ref, npost_ref, o_ref, *, ff_tile):
    h = (_rms(x_ref[...], g_ref[...]) * (1.0 + sc_ref[0]) + sh_ref[0]).astype(BF16)
    acc = jnp.zeros(x_ref.shape, F32)
    for j in range(wg_ref.shape[1] // ff_tile):
        cols = slice(j * ff_tile, (j + 1) * ff_tile)
        acc += _swiglu(h, wg_ref[:, cols], wu_ref[:, cols], wd_ref[cols, :])
    o_ref[...] = x_ref[...] + gate_ref[0] * _rms(acc, npost_ref[...])


def _ffn(x, g, shift, scale, w_gate, w_up, w_down, gate, npost, rows_per_mod):
    t = x.shape[0]
    ff = w_gate.shape[1]
    ff_tile = ff // 2
    tm = min(512, rows_per_mod)
    per = rows_per_mod // tm
    assert ff_tile % LANES == 0
    resident = pl.Buffered(1)
    return pl.pallas_call(
        functools.partial(_ffn_kernel, ff_tile=ff_tile),
        grid=(t // tm,),
        in_specs=[
            pl.BlockSpec((tm, D_MODEL), lambda i: (i, 0)),
            pl.BlockSpec((1, D_MODEL), lambda i: (0, 0)),
            pl.BlockSpec((1, 1, D_MODEL), lambda i: (i // per, 0, 0)),
            pl.BlockSpec((1, 1, D_MODEL), lambda i: (i // per, 0, 0)),
            pl.BlockSpec((D_MODEL, ff), lambda i: (0, 0), pipeline_mode=resident),
            pl.BlockSpec((D_MODEL, ff), lambda i: (0, 0), pipeline_mode=resident),
            pl.BlockSpec((ff, D_MODEL), lambda i: (0, 0), pipeline_mode=resident),
            pl.BlockSpec((1, 1, D_MODEL), lambda i: (i // per, 0, 0)),
            pl.BlockSpec((1, D_MODEL), lambda i: (0, 0)),
        ],
        out_specs=pl.BlockSpec((tm, D_MODEL), lambda i: (i, 0)),
        out_shape=jax.ShapeDtypeStruct((t, D_MODEL), F32),
        compiler_params=_params(("parallel",)),
        name="ffn",
    )(x, g, shift, scale, w_gate, w_up, w_down, gate, npost)


def _moe_kernel(x_ref, g_ref, sh_ref, sc_ref, rh_ref, rl_ref, wg_ref, wu_ref, wd_ref, gate_ref, npost_ref,
                o_ref, hn_ref, acc_ref, comb_ref, rank_c_ref, rank_r_ref, mask_r_ref, xc_ref, yc_ref, cnt_ref):
    e = pl.program_id(1)
    j = pl.program_id(2)
    tm = x_ref.shape[0]
    cap = MOE_CAP

    @pl.when((e == 0) & (j == 0))
    def _():
        h = _rms(x_ref[...], g_ref[...]) * (1.0 + sc_ref[0]) + sh_ref[0]
        hi = h.astype(BF16)
        hn_ref[...] = hi
        acc_ref[...] = jnp.zeros_like(acc_ref)
        lo = (h - hi.astype(F32)).astype(BF16)
        logits = _dot(hi, rh_ref[...]) + _dot(lo, rh_ref[...]) + _dot(hi, rl_ref[...])
        lane = lax.broadcasted_iota(jnp.int32, logits.shape, 1)
        logits = jnp.where(lane < N_EXPERTS, logits, -jnp.inf)
        m1 = jnp.max(logits, axis=-1, keepdims=True)
        i1 = jnp.min(jnp.where(logits == m1, lane, LANES), axis=-1, keepdims=True)
        rest = jnp.where(lane == i1, -jnp.inf, logits)
        m2 = jnp.max(rest, axis=-1, keepdims=True)
        i2 = jnp.min(jnp.where(rest == m2, lane, LANES), axis=-1, keepdims=True)
        e2 = jnp.exp(m2 - m1)
        comb_ref[...] = jnp.where(lane == i1, 1.0 / (1.0 + e2), 0.0) + jnp.where(lane == i2, e2 / (1.0 + e2), 0.0)
        sel = jnp.where((lane == i1) | (lane == i2), 1.0, 0.0)
        ti = lax.broadcasted_iota(jnp.int32, (tm, tm), 0)
        si = lax.broadcasted_iota(jnp.int32, (tm, tm), 1)
        before = jnp.where(si < ti, 1.0, 0.0).astype(BF16)
        rank_c_ref[...] = _dot(before, sel.astype(BF16))
        sel_t = sel.T
        after = jnp.where(ti < si, 1.0, 0.0).astype(BF16)
        rank_r_ref[...] = _dot(sel_t.astype(BF16), after)
        mask_r_ref[...] = sel_t
        counts = jnp.sum(sel_t, axis=1, keepdims=True)
        for k in range(N_EXPERTS):
            cnt_ref[k] = jnp.sum(counts[k:k + 1, :]).astype(jnp.int32)

    n_pass = (cnt_ref[e] + (cap - 1)) // cap

    @pl.when(j == 0)
    def _():
        rank_row = rank_r_ref[pl.ds(e, 1), :]
        mask_row = mask_r_ref[pl.ds(e, 1), :]
        slot = lax.broadcasted_iota(jnp.int32, (cap, tm), 0).astype(F32)

        def compact(p, _):
            base = (p * cap).astype(F32)
            onehot = jnp.where((rank_row - base == slot) & (mask_row > 0.0), 1.0, 0.0).astype(BF16)
            rows = pl.ds(pl.multiple_of(p * cap, 16), cap)
            xc_ref[rows, :] = _dot(onehot, hn_ref[...]).astype(BF16)
            yc_ref[rows, :] = jnp.zeros((cap, D_MODEL), F32)
            return 0

        lax.fori_loop(0, n_pass, compact, 0)

    def expert(start, size):
        rows = pl.ds(pl.multiple_of(start, MOE_GRAIN), size)
        yc_ref[rows, :] += _swiglu(xc_ref[rows, :], wg_ref[0], wu_ref[0], wd_ref[0])

    n_tok = cnt_ref[e]
    n_full = n_tok // cap
    rem_blocks = (n_tok - n_full * cap + (MOE_GRAIN - 1)) // MOE_GRAIN
    n_full = n_full + (rem_blocks == cap // MOE_GRAIN).astype(jnp.int32)

    def full_pass(p, _):
        expert(p * cap, cap)
        return 0

    lax.fori_loop(0, n_full, full_pass, 0)
    for blocks in range(1, cap // MOE_GRAIN):
        @pl.when(rem_blocks == blocks)
        def _(blocks=blocks):
            expert(n_full * cap, blocks * MOE_GRAIN)

    @pl.when(j == pl.num_programs(2) - 1)
    def _():
        lane = lax.broadcasted_iota(jnp.int32, (tm, LANES), 1)
        pick = lane == e
        rank_col = jnp.sum(jnp.where(pick, rank_c_ref[...], 0.0), axis=-1, keepdims=True)
        w_col = jnp.sum(jnp.where(pick, comb_ref[...], 0.0), axis=-1, keepdims=True)
        slot = lax.broadcasted_iota(jnp.int32, (tm, cap), 1).astype(F32)

        def expand(p, _):
            base = (p * cap).astype(F32)
            onehot = jnp.where(rank_col - base == slot, 1.0, 0.0).astype(BF16)
            rows = pl.ds(pl.multiple_of(p * cap, 16), cap)
            acc_ref[...] += w_col * _dot(onehot, yc_ref[rows, :].astype(BF16))
            return 0

        lax.fori_loop(0, n_pass, expand, 0)

    @pl.when((e == N_EXPERTS - 1) & (j == pl.num_programs(2) - 1))
    def _():
        o_ref[...] = x_ref[...] + gate_ref[0] * _rms(acc_ref[...], npost_ref[...])


def _moe(x, g, shift, scale, r_hi, r_lo, w_gate, w_up, w_down, gate, npost, rows_per_mod):
    t = x.shape[0]
    n_e, _, ff = w_gate.shape
    assert n_e == N_EXPERTS
    ff_tile = ff // 4
    tm = min(MOE_TILE, rows_per_mod)
    per = rows_per_mod // tm
    passes = -(-tm // MOE_CAP)
    assert ff_tile % LANES == 0 and MOE_CAP % MOE_GRAIN == 0 and MOE_GRAIN % 16 == 0
    return pl.pallas_call(
        _moe_kernel,
        grid=(t // tm, n_e, ff // ff_tile),
        in_specs=[
            pl.BlockSpec((tm, D_MODEL), lambda i, e, j: (i, 0)),
            pl.BlockSpec((1, D_MODEL), lambda i, e, j: (0, 0)),
            pl.BlockSpec((1, 1, D_MODEL), lambda i, e, j: (i // per, 0, 0)),
            pl.BlockSpec((1, 1, D_MODEL), lambda i, e, j: (i // per, 0, 0)),
            pl.BlockSpec((D_MODEL, LANES), lambda i, e, j: (0, 0)),
            pl.BlockSpec((D_MODEL, LANES), lambda i, e, j: (0, 0)),
            pl.BlockSpec((1, D_MODEL, ff_tile), lambda i, e, j: (e, 0, j)),
            pl.BlockSpec((1, D_MODEL, ff_tile), lambda i, e, j: (e, 0, j)),
            pl.BlockSpec((1, ff_tile, D_MODEL), lambda i, e, j: (e, j, 0)),
            pl.BlockSpec((1, 1, D_MODEL), lambda i, e, j: (i // per, 0, 0)),
            pl.BlockSpec((1, D_MODEL), lambda i, e, j: (0, 0)),
        ],
        out_specs=pl.BlockSpec((tm, D_MODEL), lambda i, e, j: (i, 0)),
        out_shape=jax.ShapeDtypeStruct((t, D_MODEL), F32),
        scratch_shapes=[
            pltpu.VMEM((tm, D_MODEL), BF16),
            pltpu.VMEM((tm, D_MODEL), F32),
            pltpu.VMEM((tm, LANES), F32),
            pltpu.VMEM((tm, LANES), F32),
            pltpu.VMEM((LANES, tm), F32),
            pltpu.VMEM((LANES, tm), F32),
            pltpu.VMEM((passes * MOE_CAP, D_MODEL), BF16),
            pltpu.VMEM((passes * MOE_CAP, D_MODEL), F32),
            pltpu.SMEM((N_EXPERTS,), jnp.int32),
        ],
        compiler_params=_params(("parallel", "arbitrary", "arbitrary")),
        name="moe",
    )(x, g, shift, scale, r_hi, r_lo, w_gate, w_up, w_down, gate, npost)


MOE_ROWS = MOE_TILE * TOP_K + N_EXPERTS * MOE_GRAIN
MOE_META = 2 * N_EXPERTS
MOE_PASS_ROWS = 256


def _moe_dispatch_kernel(x_ref, g_ref, sh_ref, sc_ref, rh_ref, rl_ref, xs_ref, route_ref, meta_ref):
    tm = x_ref.shape[0]
    h = _rms(x_ref[...], g_ref[...]) * (1.0 + sc_ref[0]) + sh_ref[0]
    hi = h.astype(BF16)
    lo = (h - hi.astype(F32)).astype(BF16)
    logits = _dot(hi, rh_ref[...]) + _dot(lo, rh_ref[...]) + _dot(hi, rl_ref[...])
    lane = lax.broadcasted_iota(jnp.int32, logits.shape, 1)
    logits = jnp.where(lane < N_EXPERTS, logits, -jnp.inf)
    m1 = jnp.max(logits, axis=-1, keepdims=True)
    i1 = jnp.min(jnp.where(logits == m1, lane, LANES), axis=-1, keepdims=True)
    rest = jnp.where(lane == i1, -jnp.inf, logits)
    m2 = jnp.max(rest, axis=-1, keepdims=True)
    i2 = jnp.min(jnp.where(rest == m2, lane, LANES), axis=-1, keepdims=True)
    e2 = jnp.exp(m2 - m1)
    sel1 = jnp.where(lane == i1, 1.0, 0.0)
    sel2 = jnp.where(lane == i2, 1.0, 0.0)
    sel = sel1 + sel2
    sel1_t, sel2_t = sel1.T, sel2.T
    sel_t = sel1_t + sel2_t

    ti = lax.broadcasted_iota(jnp.int32, (tm, tm), 0)
    si = lax.broadcasted_iota(jnp.int32, (tm, tm), 1)
    rank_c = _dot(jnp.where(si < ti, 1.0, 0.0).astype(BF16), sel.astype(BF16))
    rank_r = _dot(sel_t.astype(BF16), jnp.where(ti < si, 1.0, 0.0).astype(BF16))
    blk_row = jnp.ceil(jnp.sum(sel, axis=0, keepdims=True) * (1.0 / MOE_GRAIN))
    blk_col = jnp.ceil(jnp.sum(sel_t, axis=1, keepdims=True) * (1.0 / MOE_GRAIN))
    ea = lax.broadcasted_iota(jnp.int32, (LANES, LANES), 0)
    eb = lax.broadcasted_iota(jnp.int32, (LANES, LANES), 1)
    off_row = _dot(jnp.broadcast_to(blk_row, (8, LANES)).astype(BF16),
                   jnp.where(ea < eb, 1.0, 0.0).astype(BF16))[0:1, :]
    off_col = _dot(jnp.where(eb < ea, 1.0, 0.0).astype(BF16),
                   jnp.broadcast_to(blk_col, (LANES, LANES)).astype(BF16))[:, 0:1]

    pos_c = off_row * MOE_GRAIN + rank_c
    pos1_c = jnp.sum(jnp.where(lane == i1, pos_c, 0.0), axis=-1, keepdims=True)
    pos2_c = jnp.sum(jnp.where(lane == i2, pos_c, 0.0), axis=-1, keepdims=True)
    route_ref[...] = (jnp.where(lane == 0, pos1_c, 0.0) + jnp.where(lane == 1, pos2_c, 0.0)
                      + jnp.where(lane == 2, 1.0 / (1.0 + e2), 0.0) + jnp.where(lane == 3, e2 / (1.0 + e2), 0.0))
    pos_r = off_col * MOE_GRAIN + rank_r
    pos1_r = jnp.sum(sel1_t * pos_r, axis=0, keepdims=True)
    pos2_r = jnp.sum(sel2_t * pos_r, axis=0, keepdims=True)
    half = MOE_ROWS // 2
    for c in range(2):
        slot = (lax.broadcasted_iota(jnp.int32, (half, tm), 0) + c * half).astype(F32)
        onehot = jnp.where(slot == pos1_r, 1.0, jnp.where(slot == pos2_r, 1.0, 0.0)).astype(BF16)
        xs_ref[c * half:(c + 1) * half, :] = _dot(onehot, hi).astype(BF16)

    lane8 = lax.broadcasted_iota(jnp.int32, (8, LANES), 1)
    blk8 = jnp.broadcast_to(blk_row, (8, LANES))
    off8 = pltpu.roll(jnp.broadcast_to(off_row, (8, LANES)), N_EXPERTS, 1)
    meta = jnp.where(lane8 < N_EXPERTS, blk8, jnp.where(lane8 < MOE_META, off8, 0.0))
    meta_ref[0] = meta.astype(jnp.int32)


def _moe_dispatch(x, g, shift, scale, r_hi, r_lo, rows_per_mod):
    t = x.shape[0]
    tm = MOE_TILE
    assert rows_per_mod % tm == 0
    per = rows_per_mod // tm
    n_t = t // tm
    return pl.pallas_call(
        _moe_dispatch_kernel,
        grid=(n_t,),
        in_specs=[
            pl.BlockSpec((tm, D_MODEL), lambda i: (i, 0)),
            pl.BlockSpec((1, D_MODEL), lambda i: (0, 0)),
            pl.BlockSpec((1, 1, D_MODEL), lambda i: (i // per, 0, 0)),
            pl.BlockSpec((1, 1, D_MODEL), lambda i: (i // per, 0, 0)),
            pl.BlockSpec((D_MODEL, LANES), lambda i: (0, 0)),
            pl.BlockSpec((D_MODEL, LANES), lambda i: (0, 0)),
        ],
        out_specs=[
            pl.BlockSpec((MOE_ROWS, D_MODEL), lambda i: (i, 0)),
            pl.BlockSpec((tm, LANES), lambda i: (i, 0)),
            pl.BlockSpec((1, 8, LANES), lambda i: (i, 0, 0)),
        ],
        out_shape=[
            jax.ShapeDtypeStruct((n_t * MOE_ROWS, D_MODEL), BF16),
            jax.ShapeDtypeStruct((t, LANES), F32),
            jax.ShapeDtypeStruct((n_t, 8, LANES), jnp.int32),
        ],
        compiler_params=_params(("parallel",)),
        name="moe_dispatch",
    )(x, g, shift, scale, r_hi, r_lo)


def _moe_experts_kernel(meta_ref, xs_ref, wg_ref, wu_ref, wd_ref, out_ref, *, ff_tile):
    e = pl.program_id(0)
    i = pl.program_id(1)
    out_ref[...] = xs_ref[...]
    n_blk = meta_ref[i * MOE_META + e]
    first = meta_ref[i * MOE_META + N_EXPERTS + e] * MOE_GRAIN
    per_pass = MOE_PASS_ROWS // MOE_GRAIN
    n_full = n_blk // per_pass
    rem = n_blk - n_full * per_pass
    ff = wg_ref.shape[2]

    def run(start, size):
        rows = pl.ds(pl.multiple_of(start, MOE_GRAIN), size)
        xin = xs_ref[rows, :]
        y = jnp.zeros((size, D_MODEL), F32)
        for j in range(ff // ff_tile):
            cols = slice(j * ff_tile, (j + 1) * ff_tile)
            y += _swiglu(xin, wg_ref[0, :, cols], wu_ref[0, :, cols], wd_ref[0, cols, :])
        out_ref[rows, :] = y.astype(out_ref.dtype)

    def full_pass(p, _):
        run(first + p * MOE_PASS_ROWS, MOE_PASS_ROWS)
        return 0

    lax.fori_loop(0, n_full, full_pass, 0)
    for blocks in range(1, per_pass):
        @pl.when(rem == blocks)
        def _(blocks=blocks):
            run(first + n_full * MOE_PASS_ROWS, blocks * MOE_GRAIN)


MOE_FF_TILE = 512


def _moe_experts(xs, meta, w_gate, w_up, w_down):
    n_e, _, ff = w_gate.shape
    n_t = xs.shape[0] // MOE_ROWS
    assert ff % MOE_FF_TILE == 0
    resident = pl.Buffered(1)
    grid_spec = pltpu.PrefetchScalarGridSpec(
        num_scalar_prefetch=1,
        grid=(n_e, n_t),
        in_specs=[
            pl.BlockSpec((MOE_ROWS, D_MODEL), lambda e, i, m: (i, 0)),
            pl.BlockSpec((1, D_MODEL, ff), lambda e, i, m: (e, 0, 0), pipeline_mode=resident),
            pl.BlockSpec((1, D_MODEL, ff), lambda e, i, m: (e, 0, 0), pipeline_mode=resident),
            pl.BlockSpec((1, ff, D_MODEL), lambda e, i, m: (e, 0, 0), pipeline_mode=resident),
        ],
        out_specs=pl.BlockSpec((MOE_ROWS, D_MODEL), lambda e, i, m: (i, 0)),
    )
    return pl.pallas_call(
        functools.partial(_moe_experts_kernel, ff_tile=MOE_FF_TILE),
        grid_spec=grid_spec,
        out_shape=jax.ShapeDtypeStruct(xs.shape, xs.dtype),
        input_output_aliases={1: 0},
        compiler_params=_params(("arbitrary", "arbitrary")),
        name="moe_experts",
    )(meta, xs, w_gate, w_up, w_down)


def _moe_combine_kernel(ys_ref, route_ref, x_ref, gate_ref, npost_ref, o_ref):
    tm = x_ref.shape[0]
    lane = lax.broadcasted_iota(jnp.int32, (tm, LANES), 1)
    route = route_ref[...]
    pos1, pos2, w1, w2 = (jnp.sum(jnp.where(lane == k, route, 0.0), axis=-1, keepdims=True) for k in range(4))
    width = 512
    acc = jnp.zeros((tm, D_MODEL), F32)
    for c in range(MOE_ROWS // width):
        slot = (lax.broadcasted_iota(jnp.int32, (tm, width), 1) + c * width).astype(F32)
        wmat = jnp.where(slot == pos1, w1, 0.0) + jnp.where(slot == pos2, w2, 0.0)
        acc += _dot(wmat.astype(BF16), ys_ref[c * width:(c + 1) * width, :])
    o_ref[...] = x_ref[...] + gate_ref[0] * _rms(acc, npost_ref[...])


def _moe_combine(ys, route, x, gate, npost, rows_per_mod):
    t = x.shape[0]
    tm = MOE_TILE
    per = rows_per_mod // tm
    return pl.pallas_call(
        _moe_combine_kernel,
        grid=(t // tm,),
        in_specs=[
            pl.BlockSpec((MOE_ROWS, D_MODEL), lambda i: (i, 0)),
            pl.BlockSpec((tm, LANES), lambda i: (i, 0)),
            pl.BlockSpec((tm, D_MODEL), lambda i: (i, 0)),
            pl.BlockSpec((1, 1, D_MODEL), lambda i: (i // per, 0, 0)),
            pl.BlockSpec((1, D_MODEL), lambda i: (0, 0)),
        ],
        out_specs=pl.BlockSpec((tm, D_MODEL), lambda i: (i, 0)),
        out_shape=jax.ShapeDtypeStruct((t, D_MODEL), F32),
        compiler_params=_params(("parallel",)),
        name="moe_combine",
    )(ys, route, x, gate, npost)


MOE_BLOCK = 512
MOE_DMA_WINDOW = 64


def _moe_plan(meta, n_t):
    per_tile = MOE_ROWS // MOE_GRAIN
    per_blk = MOE_BLOCK // MOE_GRAIN
    nblk = meta[:, 0, :N_EXPERTS]
    first = meta[:, 0, N_EXPERTS:MOE_META]
    tot = jnp.sum(nblk, axis=0)
    tot_pad = ((tot + per_blk - 1) // per_blk) * per_blk
    ends = jnp.cumsum(tot_pad)
    seg_dst = (ends - tot_pad)[None, :] + jnp.cumsum(nblk, axis=0) - nblk
    k = jnp.arange(per_tile, dtype=jnp.int32)[None, :, None]
    inside = (k >= first[:, None, :]) & (k < (first + nblk)[:, None, :])
    dst = jnp.sum(jnp.where(inside, seg_dst[:, None, :] + k - first[:, None, :], 0), axis=-1)
    used = jnp.any(inside, axis=-1).reshape(-1)
    dst = dst.reshape(-1).astype(jnp.int32)
    n_big = (n_t * per_tile + N_EXPERTS * per_blk) // per_blk
    big = jnp.arange(n_big, dtype=jnp.int32)[:, None] * per_blk
    expert = jnp.minimum(jnp.sum(big >= ends[None, :], axis=-1), N_EXPERTS - 1).astype(jnp.int32)
    valid = (big[:, 0] < ends[-1]).astype(jnp.int32)
    src = jnp.zeros((n_big * per_blk,), jnp.int32).at[jnp.where(used, dst, n_big * per_blk)].set(
        jnp.arange(dst.shape[0], dtype=jnp.int32), mode='drop')
    return jnp.where(used, dst, 0), jnp.concatenate([expert, valid, src]), n_big


def _moe_global_kernel(plan_ref, *refs, ff_tile):
    per_blk = MOE_BLOCK // MOE_GRAIN
    x_refs = refs[:per_blk]
    wg_ref, wu_ref, wd_ref, out_ref, xin_scr = refs[per_blk:]
    b = pl.program_id(0)
    valid = plan_ref[pl.num_programs(0) + b]
    ff = wg_ref.shape[2]

    @pl.when(valid == 1)
    def _():
        for k in range(per_blk):
            xin_scr[k * MOE_GRAIN:(k + 1) * MOE_GRAIN, :] = x_refs[k][...]
        xin = xin_scr[...]
        y = jnp.zeros((MOE_BLOCK, D_MODEL), F32)
        for j in range(ff // ff_tile):
            cols = slice(j * ff_tile, (j + 1) * ff_tile)
            y += _swiglu(xin, wg_ref[0, :, cols], wu_ref[0, :, cols], wd_ref[0, cols, :])
        out_ref[...] = y.astype(out_ref.dtype)

    @pl.when(valid == 0)
    def _():
        out_ref[...] = jnp.zeros_like(out_ref)


def _moe_experts_global(xs, plan, n_big, w_gate, w_up, w_down):
    n_e, _, ff = w_gate.shape
    per_blk = MOE_BLOCK // MOE_GRAIN
    resident = pl.Buffered(1)

    def src_spec(k):
        return pl.BlockSpec((MOE_GRAIN, D_MODEL), lambda b, p: (p[2 * n_big + b * per_blk + k], 0))

    grid_spec = pltpu.PrefetchScalarGridSpec(
        num_scalar_prefetch=1,
        grid=(n_big,),
        in_specs=[src_spec(k) for k in range(per_blk)] + [
            pl.BlockSpec((1, D_MODEL, ff), lambda b, p: (p[b], 0, 0), pipeline_mode=resident),
            pl.BlockSpec((1, D_MODEL, ff), lambda b, p: (p[b], 0, 0), pipeline_mode=resident),
            pl.BlockSpec((1, ff, D_MODEL), lambda b, p: (p[b], 0, 0), pipeline_mode=resident),
        ],
        out_specs=pl.BlockSpec((MOE_BLOCK, D_MODEL), lambda b, p: (b, 0)),
        scratch_shapes=[pltpu.VMEM((MOE_BLOCK, D_MODEL), BF16)],
    )
    return pl.pallas_call(
        functools.partial(_moe_global_kernel, ff_tile=MOE_FF_TILE),
        grid_spec=grid_spec,
        out_shape=jax.ShapeDtypeStruct((n_big * MOE_BLOCK, D_MODEL), BF16),
        compiler_params=_params(("arbitrary",)),
        name="moe_experts",
    )(plan, *([xs] * per_blk), w_gate, w_up, w_down)


def _moe_global(x, g, shift, scale, r_hi, r_lo, w_gate, w_up, w_down, gate, npost, rows_per_mod):
    xs, route, meta = _moe_dispatch(x, g, shift, scale, r_hi, r_lo, rows_per_mod)
    n_t = xs.shape[0] // MOE_ROWS
    block_map, plan, n_big = _moe_plan(meta, n_t)
    yg = _moe_experts_global(xs, plan, n_big, w_gate, w_up, w_down)
    return _moe_gather_combine(yg, block_map, route, x, gate, npost, rows_per_mod)


def _moe_gather_combine_kernel(map_ref, *refs):
    per_tile = MOE_ROWS // MOE_GRAIN
    y_refs = refs[:per_tile]
    route_ref, x_ref, gate_ref, npost_ref, o_ref, ys_scr = refs[per_tile:]
    del map_ref
    for k in range(per_tile):
        ys_scr[k * MOE_GRAIN:(k + 1) * MOE_GRAIN, :] = y_refs[k][...]
    _moe_combine_kernel(ys_scr, route_ref, x_ref, gate_ref, npost_ref, o_ref)


def _moe_gather_combine(yg, block_map, route, x, gate, npost, rows_per_mod):
    t = x.shape[0]
    tm = MOE_TILE
    per = rows_per_mod // tm
    per_tile = MOE_ROWS // MOE_GRAIN

    def y_spec(k):
        return pl.BlockSpec((MOE_GRAIN, D_MODEL), lambda i, m: (m[i * per_tile + k], 0))

    grid_spec = pltpu.PrefetchScalarGridSpec(
        num_scalar_prefetch=1,
        grid=(t // tm,),
        in_specs=[y_spec(k) for k in range(per_tile)] + [
            pl.BlockSpec((tm, LANES), lambda i, m: (i, 0)),
            pl.BlockSpec((tm, D_MODEL), lambda i, m: (i, 0)),
            pl.BlockSpec((1, 1, D_MODEL), lambda i, m: (i // per, 0, 0)),
            pl.BlockSpec((1, D_MODEL), lambda i, m: (0, 0)),
        ],
        out_specs=pl.BlockSpec((tm, D_MODEL), lambda i, m: (i, 0)),
        scratch_shapes=[pltpu.VMEM((MOE_ROWS, D_MODEL), BF16)],
    )
    return pl.pallas_call(
        _moe_gather_combine_kernel,
        grid_spec=grid_spec,
        out_shape=jax.ShapeDtypeStruct((t, D_MODEL), F32),
        compiler_params=_params(("arbitrary",)),
        name="moe_combine",
    )(block_map, *([yg] * per_tile), route, x, gate, npost)


def _moe_sorted(x, g, shift, scale, r_hi, r_lo, w_gate, w_up, w_down, gate, npost, rows_per_mod):
    xs, route, meta = _moe_dispatch(x, g, shift, scale, r_hi, r_lo, rows_per_mod)
    meta = meta[:, 0, :MOE_META].reshape(-1)
    ys = _moe_experts(xs, meta, w_gate, w_up, w_down)
    return _moe_combine(ys, route, x, gate, npost, rows_per_mod)


def _reorder_w_in(w):
    o = 4 * M_WIDTH
    gates = jnp.pad(w[:, o:o + M_GATES], ((0, 0), (0, LANES - M_GATES)))
    na = w[:, o + M_GATES:o + M_GATES + 3 * NA_WIDTH]
    u0 = o + M_GATES + 3 * NA_WIDTH
    main = jnp.concatenate([w[:, :o], w[:, u0 + S5_WIDTH:], na], axis=1)
    return main.astype(BF16), gates.astype(BF16), w[:, u0:u0 + S5_WIDTH].astype(BF16)


def _row(v):
    return v.reshape(1, -1).astype(F32)


def _token_mixers(x, xc, mx, mc, lp, rope, seq_len, ctx_len, ctx_out):
    bsz = x.shape[0] // seq_len
    w_main, w_gates, w_u = _reorder_w_in(lp['w_in'])
    g_pre = _row(lp['norm_mix_pre'])
    px, gx, ux = _inproj(x, g_pre, mx[0], mx[1], w_main, w_gates, w_u, seq_len)
    pc, gc, uc = _inproj(xc, g_pre, mc[0], mc[1], w_main, w_gates, w_u, xc.shape[0])

    conv_w = lp['m_conv_w'].astype(F32)
    conv_b = _row(lp['m_conv_b'])
    gate_b = jnp.pad(_row(lp['m_gate_b']), ((0, 0), (0, LANES - M_GATES)))
    m_norm = _row(lp['m_norm'])
    qk_c = _qkconv(pc, conv_w, conv_b, rope[0][:ctx_len], rope[1][:ctx_len], ctx_len, False)
    qk_x = _qkconv(px, conv_w, conv_b, rope[0], rope[1], seq_len, True)
    zero = _mlstm_zero_state(bsz)
    ym_c, fin_f, fin_b = _mlstm_bidir(qk_c, pc, gc, gate_b, m_norm, zero, zero, ctx_len)
    ym_x, _, _ = _mlstm_bidir(qk_x, px, gx, gate_b, m_norm, fin_f, fin_b, seq_len)

    yn_x = _na_attention(px, pc, _na_bias_table(lp['na_rpb']), seq_len, ctx_len)

    tables = _s5_tables(lp['s5_lam_re'], lp['s5_lam_im'], lp['s5_log_dt'], lp['s5_b_re'], lp['s5_b_im'],
                        lp['s5_c_re'], lp['s5_c_im'])
    zs = jnp.zeros((bsz, S5_GROUPS * S5_STATE), F32)
    ys_c, fin_s = _s5_branch(uc, tables, ctx_len, (zs, zs, zs, zs), ctx_out)
    ys_x, _ = _s5_branch(ux, tables, seq_len, fin_s, True)

    wts = (_row(lp['s5_d']), lp['s5_glu_w'].astype(BF16), _row(lp['s5_glu_b']),
           lp['w_branch_m'].astype(BF16), lp['w_branch_na'].astype(BF16), lp['w_branch_s5'].astype(BF16),
           lp['w_out'].astype(BF16))
    g_post = _row(lp['norm_mix_post'])
    x = _merge(x, ym_x, yn_x, ys_x, ux, px, wts, mx[2], g_post, seq_len)
    if ctx_out:
        yn_c = _ctx_attention(pc, ctx_len)
        xc = _merge(xc, ym_c, yn_c, ys_c, uc, pc, wts, mc[2], g_post, xc.shape[0])
    return x, xc


def kernel(x, c, ctx, c_ctx, ada_w, ada_b, norm_mix_pre, norm_mix_post, norm_ffn_pre, norm_ffn_post, w_in, m_gate_b, m_conv_w, m_conv_b, m_norm, na_rpb, s5_lam_re, s5_lam_im, s5_log_dt, s5_b_re, s5_b_im, s5_c_re, s5_c_im, s5_d, s5_glu_w, s5_glu_b, w_branch_m, w_branch_na, w_branch_s5, w_out, ffn_w_gate, ffn_w_up, ffn_w_down, moe_router, moe_w_gate, moe_w_up, moe_w_down):
    bsz, seq_len, d = x.shape
    ctx_len = ctx.shape[1]
    depth = w_in.shape[0]
    assert d == D_MODEL and seq_len % (GRID_W * NA_ROWS) == 0 and ctx_len % M_CHUNK == 0
    rope = _rope_tables(seq_len)
    xf = x.reshape(bsz * seq_len, d)
    xc = ctx.reshape(bsz * ctx_len, d)
    cvecs = jnp.zeros((16, d), F32).at[:bsz].set(c).at[bsz].set(c_ctx)
    for l in range(depth):
        last = l == depth - 1
        mod = _adaln(cvecs, ada_w[l].astype(BF16), _row(ada_b[l]))
        mx = [mod[:bsz, k * d:(k + 1) * d].reshape(bsz, 1, d) for k in range(N_MOD)]
        mc = [mod[bsz:bsz + 1, k * d:(k + 1) * d].reshape(1, 1, d) for k in range(N_MOD)]
        lp = {
            'norm_mix_pre': norm_mix_pre[l], 'norm_mix_post': norm_mix_post[l],
            'w_in': w_in[l], 'm_gate_b': m_gate_b[l], 'm_conv_w': m_conv_w[l], 'm_conv_b': m_conv_b[l],
            'm_norm': m_norm[l], 'na_rpb': na_rpb[l], 's5_lam_re': s5_lam_re[l], 's5_lam_im': s5_lam_im[l],
            's5_log_dt': s5_log_dt[l], 's5_b_re': s5_b_re[l], 's5_b_im': s5_b_im[l], 's5_c_re': s5_c_re[l],
            's5_c_im': s5_c_im[l], 's5_d': s5_d[l], 's5_glu_w': s5_glu_w[l], 's5_glu_b': s5_glu_b[l],
            'w_branch_m': w_branch_m[l], 'w_branch_na': w_branch_na[l], 'w_branch_s5': w_branch_s5[l],
            'w_out': w_out[l],
        }
        xf, xc = _token_mixers(xf, xc, mx, mc, lp, rope, seq_len, ctx_len, not last)
        j = l // 2
        g_pre, g_post = _row(norm_ffn_pre[l]), _row(norm_ffn_post[l])
        if l % 2 == 0:
            wg, wu, wd = ffn_w_gate[j].astype(BF16), ffn_w_up[j].astype(BF16), ffn_w_down[j].astype(BF16)

            def channel(h, m, rows):
                return _ffn(h, g_pre, m[3], m[4], wg, wu, wd, m[5], g_post, rows)
        else:
            wg, wu, wd = moe_w_gate[j].astype(BF16), moe_w_up[j].astype(BF16), moe_w_down[j].astype(BF16)
            router = jnp.pad(moe_router[j].astype(F32), ((0, 0), (0, LANES - N_EXPERTS)))
            r_hi = router.astype(BF16)
            r_lo = (router - r_hi.astype(F32)).astype(BF16)

            def channel(h, m, rows):
                return _moe_global(h, g_pre, m[3], m[4], r_hi, r_lo, wg, wu, wd, m[5], g_post, rows)
        xf = channel(xf, mx, seq_len)
        if not last:
            xc = channel(xc, mc, xc.shape[0])
    return xf.reshape(bsz, seq_len, d)
```

```python
import functools
import math

import jax
import jax.numpy as jnp
from jax import lax
from jax.experimental import pallas as pl
from jax.experimental.pallas import tpu as pltpu

F32 = jnp.float32
BF16 = jnp.bfloat16

D_MODEL = 1024
EPS = 1e-6
N_MOD = 6
GRID_W = 64

M_HEADS = 4
M_HEAD_DIM = 128
M_WIDTH = M_HEADS * M_HEAD_DIM
M_GATES = 4 * M_HEADS
M_CHUNK = 128
ROPE_BASE = 10000.0

NA_HEADS = 8
NA_HEAD_DIM = 64
NA_WIDTH = NA_HEADS * NA_HEAD_DIM
WIN_R = 8
WIN_C = 16
NEG_BIG = -1e30
LOG2E = math.log2(math.e)

S5_GROUP = 16
S5_WIDTH = 512
S5_GROUPS = S5_WIDTH // S5_GROUP
S5_STATE = 64
S5_MAX_RE = -1e-4

LANES = 128
VMEM_LIMIT = 52 * 1024 * 1024

S5_CHUNK = 16
S5_QUAD_GROUPS = LANES // S5_GROUP
S5_QUADS = S5_GROUPS // S5_QUAD_GROUPS
S5_QCOLS = S5_CHUNK * LANES
S5_QSTATE = S5_QUAD_GROUPS * S5_STATE

N_EXPERTS = 8
TOP_K = 2
MOE_TILE = 1024
MOE_GRAIN = 64

COL_QK, COL_V, COL_O, COL_NAQ, COL_NAK, COL_NAV = 0, 2, 3, 10, 11, 12
COL_GATE = 2
IN_MAIN = 6656


def _params(sem):
    return pltpu.CompilerParams(dimension_semantics=sem, vmem_limit_bytes=VMEM_LIMIT)


def _dot(a, b):
    return jnp.dot(a, b, preferred_element_type=F32)


def _dot_nt(a, b):
    return lax.dot_general(a, b, (((1,), (1,)), ((), ())), preferred_element_type=F32)


def _split3(x):
    hi = x.astype(BF16)
    r1 = x - hi.astype(F32)
    mid = r1.astype(BF16)
    lo = (r1 - mid.astype(F32)).astype(BF16)
    return hi, mid, lo


def _rms(x, g):
    return x * lax.rsqrt(jnp.mean(x * x, axis=-1, keepdims=True) + EPS) * g


def _inproj_kernel(x_ref, g_ref, sh_ref, sc_ref, w_ref, wg_ref, wu_ref, o_ref, og_ref, ou_ref, *, tn):
    h = _rms(x_ref[...], g_ref[...]) * (1.0 + sc_ref[0]) + sh_ref[0]
    hb = h.astype(BF16)
    og_ref[...] = _dot(hb, wg_ref[...])
    ou_ref[...] = _dot(hb, wu_ref[...])
    for c in range(IN_MAIN // tn):
        cols = slice(c * tn, (c + 1) * tn)
        o_ref[:, cols] = _dot(hb, w_ref[:, cols]).astype(o_ref.dtype)


def _inproj(x, g, shift, scale, w_main, w_gates, w_u, rows_per_mod):
    t = x.shape[0]
    tm = min(512, rows_per_mod)
    tn = IN_MAIN // 4
    assert rows_per_mod % tm == 0 and t % tm == 0 and tn % LANES == 0
    per = rows_per_mod // tm
    resident = pl.Buffered(1)
    return pl.pallas_call(
        functools.partial(_inproj_kernel, tn=tn),
        grid=(t // tm,),
        in_specs=[
            pl.BlockSpec((tm, D_MODEL), lambda i: (i, 0)),
            pl.BlockSpec((1, D_MODEL), lambda i: (0, 0)),
            pl.BlockSpec((1, 1, D_MODEL), lambda i: (i // per, 0, 0)),
            pl.BlockSpec((1, 1, D_MODEL), lambda i: (i // per, 0, 0)),
            pl.BlockSpec((D_MODEL, IN_MAIN), lambda i: (0, 0), pipeline_mode=resident),
            pl.BlockSpec((D_MODEL, LANES), lambda i: (0, 0), pipeline_mode=resident),
            pl.BlockSpec((D_MODEL, S5_WIDTH), lambda i: (0, 0), pipeline_mode=resident),
        ],
        out_specs=[
            pl.BlockSpec((tm, IN_MAIN), lambda i: (i, 0)),
            pl.BlockSpec((tm, LANES), lambda i: (i, 0)),
            pl.BlockSpec((tm, S5_WIDTH), lambda i: (i, 0)),
        ],
        out_shape=[jax.ShapeDtypeStruct((t, IN_MAIN), BF16), jax.ShapeDtypeStruct((t, LANES), F32),
                   jax.ShapeDtypeStruct((t, S5_WIDTH), F32)],
        compiler_params=_params(("parallel",)),
        name="inproj",
    )(x, g, shift, scale, w_main, w_gates, w_u)


def _adaln_kernel(c_ref, w_ref, b_ref, o_ref):
    c = c_ref[...]
    s = c * jax.nn.sigmoid(c)
    o_ref[...] = _dot(s.astype(BF16), w_ref[...]) + b_ref[...]


def _adaln(cvecs, w, b):
    n = w.shape[1]
    tn = 1024
    return pl.pallas_call(
        _adaln_kernel,
        grid=(n // tn,),
        in_specs=[
            pl.BlockSpec(cvecs.shape, lambda j: (0, 0)),
            pl.BlockSpec((D_MODEL, tn), lambda j: (0, j)),
            pl.BlockSpec((1, tn), lambda j: (0, j)),
        ],
        out_specs=pl.BlockSpec((cvecs.shape[0], tn), lambda j: (0, j)),
        out_shape=jax.ShapeDtypeStruct((cvecs.shape[0], n), F32),
        compiler_params=_params(("parallel",)),
        name="adaln",
    )(cvecs, w, b)


HALO = 16


def _swap32(x):
    lane = lax.broadcasted_iota(jnp.int32, x.shape, 1)
    fwd = pltpu.roll(x, 96, 1)
    bwd = pltpu.roll(x, 32, 1)
    return jnp.where((lane % 64) < 32, fwd, bwd)


def _qkconv_kernel(x_ref, pv_ref, nx_ref, w_ref, b_ref, cos_ref, sin_ref, o_ref, *, tiles_per_seq, rope):
    i = pl.program_id(0)
    tm = x_ref.shape[0]
    x = x_ref[...].astype(F32)
    first = (i % tiles_per_seq) == 0
    last = (i % tiles_per_seq) == tiles_per_seq - 1
    prev_row = jnp.where(first, 0.0, pv_ref[HALO - 1:HALO, :].astype(F32))
    next_row = jnp.where(last, 0.0, nx_ref[0:1, :].astype(F32))
    row = lax.broadcasted_iota(jnp.int32, x.shape, 0)
    xp = jnp.where(row == 0, prev_row, pltpu.roll(x, 1, 0))
    xn = jnp.where(row == tm - 1, next_row, pltpu.roll(x, tm - 1, 0))
    y = xp * w_ref[0:1, :] + x * w_ref[1:2, :] + xn * w_ref[2:3, :] + b_ref[...]
    y = y * _sigmoid(y)
    kscale = M_HEAD_DIM ** -0.5
    for hh in range(2 * M_HEADS):
        ys = y[:, hh * LANES:(hh + 1) * LANES]
        if rope:
            ys = ys * cos_ref[...] + _swap32(ys) * sin_ref[...]
        if hh >= M_HEADS:
            ys = ys * kscale
        o_ref[:, hh * LANES:(hh + 1) * LANES] = ys.astype(o_ref.dtype)


def _qkconv(px, conv_w, conv_b, cos_t, sin_t, seq_len, rope):
    t = px.shape[0]
    tm = min(512, seq_len)
    assert seq_len % tm == 0
    tiles = seq_len // tm
    hb = tm // HALO
    nh = t // HALO
    return pl.pallas_call(
        functools.partial(_qkconv_kernel, tiles_per_seq=tiles, rope=rope),
        grid=(t // tm,),
        in_specs=[
            pl.BlockSpec((tm, 2 * M_WIDTH), lambda i: (i, 0)),
            pl.BlockSpec((HALO, 2 * M_WIDTH), lambda i: (jnp.maximum(i * hb - 1, 0), 0)),
            pl.BlockSpec((HALO, 2 * M_WIDTH), lambda i: (jnp.minimum((i + 1) * hb, nh - 1), 0)),
            pl.BlockSpec((3, 2 * M_WIDTH), lambda i: (0, 0)),
            pl.BlockSpec((1, 2 * M_WIDTH), lambda i: (0, 0)),
            pl.BlockSpec((tm, LANES), lambda i: (i % tiles, 0)),
            pl.BlockSpec((tm, LANES), lambda i: (i % tiles, 0)),
        ],
        out_specs=pl.BlockSpec((tm, 2 * M_WIDTH), lambda i: (i, 0)),
        out_shape=jax.ShapeDtypeStruct((t, 2 * M_WIDTH), BF16),
        compiler_params=_params(("parallel",)),
        name="qkconv",
    )(px, px, px, conv_w, conv_b, cos_t, sin_t)


def _rope_tables(n):
    pos = jnp.arange(n, dtype=jnp.int32)
    row = (pos // GRID_W).astype(F32)
    col = (pos % GRID_W).astype(F32)
    n_freq = M_HEAD_DIM // 4
    inv = ROPE_BASE ** (-jnp.arange(n_freq, dtype=F32) / n_freq)
    ar, ac = row[:, None] * inv, col[:, None] * inv
    cos_t = jnp.concatenate([jnp.cos(ar), jnp.cos(ar), jnp.cos(ac), jnp.cos(ac)], axis=-1)
    sin_t = jnp.concatenate([-jnp.sin(ar), jnp.sin(ar), -jnp.sin(ac), jnp.sin(ac)], axis=-1)
    return cos_t, sin_t


def _mlstm_kernel(*refs, reverse, final, nb):
    if final:
        (qk_ref, v_ref, g_ref, gb_ref, c0_ref, n0_ref, m0_ref, hp_ref, op_ref, nw_ref,
         out_ref, cf_ref, nf_ref, mf_ref, c_scr, n_scr, m_scr, qk_scr, qc_scr, qn_scr, s_scr, vt_scr, vu_scr, dec_scr) = refs
    else:
        (qk_ref, v_ref, g_ref, gb_ref, c0_ref, n0_ref, m0_ref,
         out_ref, cf_ref, nf_ref, mf_ref, c_scr, n_scr, m_scr, qk_scr, qc_scr, qn_scr, s_scr, vt_scr, vu_scr, dec_scr) = refs
    c_idx = pl.program_id(1)
    n_c = pl.num_programs(1)
    L = M_CHUNK

    @pl.when(c_idx == 0)
    def _():
        c_scr[...] = c0_ref[...]
        n_scr[...] = n0_ref[...]
        m_scr[...] = m0_ref[...]

    ti = lax.broadcasted_iota(jnp.int32, (L, L), 0)
    si = lax.broadcasted_iota(jnp.int32, (L, L), 1)
    keep = (si >= ti) if reverse else (si <= ti)
    tri = jnp.where(keep, 1.0, 0.0).astype(BF16)
    i_off = 2 * M_HEADS if reverse else 0
    f_off = i_off + M_HEADS
    end = 0 if reverse else L - 1

    units = [(a, b) for a in range(nb) for b in range(M_HEADS)]

    def head(ref, gi, h, base=0):
        return ref[gi, :, base + h * LANES:base + (h + 1) * LANES]

    keep_t = (ti >= si) if reverse else (ti <= si)

    for u, (gi, h) in enumerate(units):
        q = head(qk_ref, gi, h)
        qk_scr[u] = _dot_nt(head(qk_ref, gi, h, M_WIDTH), q)
        qc_scr[u] = _dot_nt(c_scr[gi, h].astype(BF16), q)
        qn_scr[u] = _dot_nt(jnp.broadcast_to(n_scr[gi, h], (8, LANES)).astype(BF16), q)

    for u, (gi, h) in enumerate(units):
        if h == 0:
            g = g_ref[gi] + gb_ref[...]
            f_hi, f_mid, f_lo = _split3(jax.nn.log_sigmoid(g))
            bsum = _dot(tri, f_hi) + _dot(tri, f_mid) + _dot(tri, f_lo)
            g_t = g.T
            b_t = bsum.T
        m_prev = m_scr[gi, h][:, 0:1]
        b_row = b_t[f_off + h:f_off + h + 1, :]
        i_row = g_t[i_off + h:i_off + h + 1, :]
        b_end = b_row[:, end:end + 1]
        d_col = g[:, i_off + h:i_off + h + 1] - bsum[:, f_off + h:f_off + h + 1]

        log_w = jnp.where(keep_t, b_row + d_col, -jnp.inf)
        carry_log = b_row + m_prev
        m_t = jnp.maximum(carry_log, jnp.max(log_w, axis=0, keepdims=True))
        s_t = qk_scr[u] * jnp.exp(log_w - m_t)
        c_scale = jnp.exp(carry_log - m_t)
        den = jnp.sum(s_t, axis=0, keepdims=True) + c_scale * qn_scr[u][0:1, :]
        inv = 1.0 / jnp.maximum(jnp.abs(den), jnp.exp(-m_t))
        s_scr[u] = (s_t * inv).astype(BF16)
        qc_scr[u] = qc_scr[u] * (c_scale * inv)

        m_new = jnp.maximum(b_end + m_prev, jnp.max(b_end - b_row + i_row, axis=-1, keepdims=True))
        u_row = jnp.exp(b_end - b_row + i_row - m_new)
        decay = jnp.exp(b_end + m_prev - m_new)
        v_t = head(v_ref, gi, h).astype(F32).T
        vt_scr[u] = v_t.astype(BF16)
        vu_scr[u] = (v_t * u_row).astype(BF16)
        u8 = jnp.broadcast_to(u_row, (8, L)).astype(BF16)
        n_scr[gi, h] = decay * n_scr[gi, h] + _dot(u8, head(qk_ref, gi, h, M_WIDTH))[0:1, :]
        m_scr[gi, h] = jnp.broadcast_to(m_new, (1, LANES))
        dec_scr[u] = jnp.broadcast_to(decay, (1, LANES))

    for u, (gi, h) in enumerate(units):
        qc_scr[u] = _dot(vt_scr[u], s_scr[u]) + qc_scr[u]
        c_scr[gi, h] = dec_scr[u][:, 0:1] * c_scr[gi, h] + _dot(vu_scr[u], head(qk_ref, gi, h, M_WIDTH))

    for u, (gi, h) in enumerate(units):
        rows = slice(h * M_HEAD_DIM, (h + 1) * M_HEAD_DIM)
        if final:
            hs = qc_scr[u] + hp_ref[gi, rows, :]
            mu = jnp.mean(hs, axis=0, keepdims=True)
            var = jnp.mean(jnp.square(hs - mu), axis=0, keepdims=True)
            hn = ((hs - mu) * lax.rsqrt(var + EPS)).T
            y = hn * nw_ref[:, h * LANES:(h + 1) * LANES] * jax.nn.sigmoid(head(op_ref, gi, h).astype(F32))
            out_ref[gi, :, h * LANES:(h + 1) * LANES] = y.astype(out_ref.dtype)
        else:
            out_ref[gi, rows, :] = qc_scr[u]

    @pl.when(c_idx == n_c - 1)
    def _():
        cf_ref[...] = c_scr[...]
        nf_ref[...] = n_scr[...]
        mf_ref[...] = m_scr[...]


def _mlstm_scan(qk, px, gates, gate_b, state, seq_len, reverse, hprev=None, norm_w=None):
    t = qk.shape[0]
    bsz = t // seq_len
    n_c = seq_len // M_CHUNK
    final = hprev is not None
    nb = 4 if bsz % 4 == 0 else (2 if bsz % 2 == 0 else 1)
    qk, px, gates = (a.reshape(bsz, seq_len, a.shape[1]) for a in (qk, px, gates))

    def chunk(c):
        return (n_c - 1 - c) if reverse else c

    state_specs = [
        pl.BlockSpec((nb, M_HEADS, M_HEAD_DIM, M_HEAD_DIM), lambda b, c: (b, 0, 0, 0)),
        pl.BlockSpec((nb, M_HEADS, 1, LANES), lambda b, c: (b, 0, 0, 0)),
        pl.BlockSpec((nb, M_HEADS, 1, LANES), lambda b, c: (b, 0, 0, 0)),
    ]
    in_specs = [
        pl.BlockSpec((nb, M_CHUNK, 2 * M_WIDTH), lambda b, c: (b, chunk(c), 0)),
        pl.BlockSpec((nb, M_CHUNK, M_WIDTH), lambda b, c: (b, chunk(c), COL_V)),
        pl.BlockSpec((nb, M_CHUNK, LANES), lambda b, c: (b, chunk(c), 0)),
        pl.BlockSpec((1, LANES), lambda b, c: (0, 0)),
    ] + state_specs
    args = [qk, px, gates, gate_b, *state]
    t_spec = pl.BlockSpec((nb, M_WIDTH, M_CHUNK), lambda b, c: (b, 0, chunk(c)))
    if final:
        in_specs += [
            t_spec,
            pl.BlockSpec((nb, M_CHUNK, M_WIDTH), lambda b, c: (b, chunk(c), COL_O)),
            pl.BlockSpec((1, M_WIDTH), lambda b, c: (0, 0)),
        ]
        args += [hprev, px, norm_w]
        out_spec = pl.BlockSpec((nb, M_CHUNK, M_WIDTH), lambda b, c: (b, chunk(c), 0))
        out_shape = jax.ShapeDtypeStruct((bsz, seq_len, M_WIDTH), BF16)
    else:
        out_spec = t_spec
        out_shape = jax.ShapeDtypeStruct((bsz, M_WIDTH, seq_len), F32)
    units = nb * M_HEADS
    out, *fin = pl.pallas_call(
        functools.partial(_mlstm_kernel, reverse=reverse, final=final, nb=nb),
        grid=(bsz // nb, n_c),
        in_specs=in_specs,
        out_specs=[out_spec] + state_specs,
        out_shape=[
            out_shape,
            jax.ShapeDtypeStruct((bsz, M_HEADS, M_HEAD_DIM, M_HEAD_DIM), F32),
            jax.ShapeDtypeStruct((bsz, M_HEADS, 1, LANES), F32),
            jax.ShapeDtypeStruct((bsz, M_HEADS, 1, LANES), F32),
        ],
        scratch_shapes=[
            pltpu.VMEM((nb, M_HEADS, M_HEAD_DIM, M_HEAD_DIM), F32),
            pltpu.VMEM((nb, M_HEADS, 1, LANES), F32),
            pltpu.VMEM((nb, M_HEADS, 1, LANES), F32),
            pltpu.VMEM((units, M_CHUNK, M_CHUNK), F32),
            pltpu.VMEM((units, M_HEAD_DIM, M_CHUNK), F32),
            pltpu.VMEM((units, 8, M_CHUNK), F32),
            pltpu.VMEM((units, M_CHUNK, M_CHUNK), BF16),
            pltpu.VMEM((units, M_HEAD_DIM, M_CHUNK), BF16),
            pltpu.VMEM((units, M_HEAD_DIM, M_CHUNK), BF16),
            pltpu.VMEM((units, 1, LANES), F32),
        ],
        compiler_params=_params(("parallel", "arbitrary")),
        name="mlstm_rev" if reverse else "mlstm_fwd",
    )(*args)
    return [out.reshape(t, M_WIDTH) if final else out] + fin


def _mlstm_zero_state(bsz):
    return (jnp.zeros((bsz, M_HEADS, M_HEAD_DIM, M_HEAD_DIM), F32),
            jnp.zeros((bsz, M_HEADS, 1, LANES), F32),
            jnp.zeros((bsz, M_HEADS, 1, LANES), F32))


def _mlstm_bidir(qk, px, gates, gate_b, norm_w, state_f, state_b, seq_len):
    h_f, *fin_f = _mlstm_scan(qk, px, gates, gate_b, state_f, seq_len, False)
    y, *fin_b = _mlstm_scan(qk, px, gates, gate_b, state_b, seq_len, True, hprev=h_f, norm_w=norm_w)
    return y, tuple(fin_f), tuple(fin_b)


NA_ROWS = 8
NA_STEP = 4


def _na_kernel(q_ref, kp_ref, kc_ref, kn_ref, vp_ref, vc_ref, vn_ref, kx_ref, vx_ref, *rest, rows):
    bias_refs = rest[:NA_STEP]
    o_ref, kwin, vwin, s_scr, p_scr, l_scr = rest[NA_STEP:]
    step = pl.program_id(1)
    blk = NA_ROWS * GRID_W

    @pl.when(step % (NA_ROWS // NA_STEP) == 0)
    def _():
        kwin[0:blk, :] = kp_ref[...]
        kwin[blk:2 * blk, :] = kc_ref[...]
        kwin[2 * blk:3 * blk, :] = kn_ref[...]
        vwin[0:blk, :] = vp_ref[...]
        vwin[blk:2 * blk, :] = vc_ref[...]
        vwin[2 * blk:3 * blk, :] = vn_ref[...]

    g = step // (NA_ROWS // NA_STEP)
    n_loc = WIN_R * GRID_W
    lane = lax.broadcasted_iota(jnp.int32, (GRID_W, LANES), 1)
    scale = NA_HEAD_DIM ** -0.5
    n_ctx = kx_ref.shape[0]
    units = [(kk, hp) for kk in range(NA_STEP) for hp in range(NA_HEADS // 2)]
    offs = []
    for kk in range(NA_STEP):
        r = step * NA_STEP + kk
        r0 = jnp.clip(r - WIN_R // 2, 0, rows - WIN_R)
        offs.append(pl.multiple_of((r0 - NA_ROWS * g + NA_ROWS) * GRID_W, GRID_W))

    for u, (kk, hp) in enumerate(units):
        cols = slice(hp * LANES, (hp + 1) * LANES)
        q2 = (q_ref[kk * GRID_W:(kk + 1) * GRID_W, cols].astype(F32) * (scale * LOG2E)).astype(BF16)
        zero = jnp.zeros_like(q2)
        qs = jnp.concatenate([jnp.where(lane < NA_HEAD_DIM, q2, zero), jnp.where(lane < NA_HEAD_DIM, zero, q2)],
                             axis=0)
        s_scr[u, :, 0:n_loc] = _dot_nt(qs, kwin[pl.ds(offs[kk], n_loc), cols])
        s_scr[u, :, n_loc:n_loc + n_ctx] = _dot_nt(qs, kx_ref[:, cols])

    for u, (kk, hp) in enumerate(units):
        s_loc = s_scr[u, :, 0:n_loc] + bias_refs[kk][0, hp]
        s_ctx = s_scr[u, :, n_loc:n_loc + n_ctx]
        m = jnp.maximum(jnp.max(s_loc, axis=-1, keepdims=True), jnp.max(s_ctx, axis=-1, keepdims=True))
        p_loc = jnp.exp2(s_loc - m)
        p_ctx = jnp.exp2(s_ctx - m)
        inv = 1.0 / (jnp.sum(p_loc, axis=-1, keepdims=True) + jnp.sum(p_ctx, axis=-1, keepdims=True))
        p_scr[u, :, 0:n_loc] = p_loc.astype(BF16)
        p_scr[u, :, n_loc:n_loc + n_ctx] = p_ctx.astype(BF16)
        l_scr[u] = jnp.broadcast_to(inv, (2 * GRID_W, LANES))

    for u, (kk, hp) in enumerate(units):
        cols = slice(hp * LANES, (hp + 1) * LANES)
        o2 = _dot(p_scr[u, :, 0:n_loc], vwin[pl.ds(offs[kk], n_loc), cols])
        o2 = (o2 + _dot(p_scr[u, :, n_loc:n_loc + n_ctx], vx_ref[:, cols])) * l_scr[u]
        out = jnp.where(lane < NA_HEAD_DIM, o2[0:GRID_W], o2[GRID_W:2 * GRID_W])
        o_ref[kk * GRID_W:(kk + 1) * GRID_W, cols] = out.astype(o_ref.dtype)


def _na_bias_table(rpb):
    qc = jnp.arange(GRID_W)[:, None]
    kc = jnp.arange(GRID_W)[None, :]
    cs = jnp.clip(qc - WIN_C // 2, 0, GRID_W - WIN_C)
    ok = (kc >= cs) & (kc < cs + WIN_C)
    dc = jnp.clip(kc - qc + WIN_C - 1, 0, 2 * WIN_C - 2)
    pick = (dc[:, :, None] == jnp.arange(2 * WIN_C - 1)).astype(F32)
    cols = jnp.einsum('hrd,qkd->hrqk', rpb.astype(F32) * LOG2E, pick, precision=lax.Precision.HIGHEST)
    cols = jnp.where(ok[None, None], cols, NEG_BIG)
    tab = jnp.stack([cols[:, d0:d0 + WIN_R] for d0 in range(WIN_R)])
    tab = jnp.transpose(tab, (0, 1, 3, 2, 4))
    return tab.reshape(WIN_R, NA_HEADS // 2, 2 * GRID_W, WIN_R * GRID_W)


def _na_attention(px, pc, bias, seq_len, ctx_len):
    t = px.shape[0]
    bsz = t // seq_len
    rows = seq_len // GRID_W
    n_g = rows // NA_ROWS
    blk = NA_ROWS * GRID_W

    steps = rows // NA_STEP
    per_blk = NA_ROWS // NA_STEP

    def kv_spec(col, shift):
        def imap(b, s):
            g = jnp.clip(s // per_blk + shift, 0, n_g - 1)
            return (b * n_g + g, col)
        return pl.BlockSpec((blk, NA_WIDTH), imap)

    def bias_spec(kk):
        def imap(b, s):
            r = s * NA_STEP + kk
            r0 = jnp.clip(r - WIN_R // 2, 0, rows - WIN_R)
            return (r0 - r + WIN_R - 1, 0, 0, 0)
        return pl.BlockSpec((1, NA_HEADS // 2, 2 * GRID_W, WIN_R * GRID_W), imap)

    return pl.pallas_call(
        functools.partial(_na_kernel, rows=rows),
        grid=(bsz, steps),
        in_specs=[
            pl.BlockSpec((NA_STEP * GRID_W, NA_WIDTH), lambda b, s: (b * steps + s, COL_NAQ)),
            kv_spec(COL_NAK, -1), kv_spec(COL_NAK, 0), kv_spec(COL_NAK, 1),
            kv_spec(COL_NAV, -1), kv_spec(COL_NAV, 0), kv_spec(COL_NAV, 1),
            pl.BlockSpec((ctx_len, NA_WIDTH), lambda b, s: (b, COL_NAK)),
            pl.BlockSpec((ctx_len, NA_WIDTH), lambda b, s: (b, COL_NAV)),
        ] + [bias_spec(kk) for kk in range(NA_STEP)],
        out_specs=pl.BlockSpec((NA_STEP * GRID_W, NA_WIDTH), lambda b, s: (b * steps + s, 0)),
        out_shape=jax.ShapeDtypeStruct((t, NA_WIDTH), BF16),
        scratch_shapes=[
            pltpu.VMEM((3 * blk, NA_WIDTH), BF16), pltpu.VMEM((3 * blk, NA_WIDTH), BF16),
            pltpu.VMEM((NA_STEP * NA_HEADS // 2, 2 * GRID_W, WIN_R * GRID_W + ctx_len), F32),
            pltpu.VMEM((NA_STEP * NA_HEADS // 2, 2 * GRID_W, WIN_R * GRID_W + ctx_len), BF16),
            pltpu.VMEM((NA_STEP * NA_HEADS // 2, 2 * GRID_W, LANES), F32),
        ],
        compiler_params=_params(("parallel", "arbitrary")),
        name="na_attn",
    )(px, px, px, px, px, px, px, pc, pc, *([bias] * NA_STEP))


def _ctx_attn_kernel(q_ref, k_ref, v_ref, o_ref):
    n = q_ref.shape[0]
    lane = lax.broadcasted_iota(jnp.int32, (n, LANES), 1)
    scale = NA_HEAD_DIM ** -0.5
    for hp in range(NA_HEADS // 2):
        cols = slice(hp * LANES, (hp + 1) * LANES)
        q2 = q_ref[:, cols] * scale
        k2 = k_ref[:, cols]
        v2 = v_ref[:, cols]
        outs = []
        for sub in range(2):
            sel = (lane < NA_HEAD_DIM) if sub == 0 else (lane >= NA_HEAD_DIM)
            s = _dot_nt(jnp.where(sel, q2, jnp.zeros_like(q2)), k2)
            p = jnp.exp(s - jnp.max(s, axis=-1, keepdims=True))
            outs.append(_dot(p.astype(BF16), v2) / jnp.sum(p, axis=-1, keepdims=True))
        o_ref[:, cols] = jnp.where(lane < NA_HEAD_DIM, outs[0], outs[1]).astype(o_ref.dtype)


def _ctx_attention(pc, ctx_len):
    t = pc.shape[0]
    return pl.pallas_call(
        _ctx_attn_kernel,
        grid=(t // ctx_len,),
        in_specs=[pl.BlockSpec((ctx_len, NA_WIDTH), lambda b, col=col: (b, col))
                  for col in (COL_NAQ, COL_NAK, COL_NAV)],
        out_specs=pl.BlockSpec((ctx_len, NA_WIDTH), lambda b: (b, 0)),
        out_shape=jax.ShapeDtypeStruct((t, NA_WIDTH), BF16),
        compiler_params=_params(("parallel",)),
        name="ctx_attn",
    )(pc, pc, pc)


def _cmul(a, b):
    return a[0] * b[0] - a[1] * b[1], a[0] * b[1] + a[1] * b[0]


def _s5_tables(lam_re, lam_im, log_dt, b_re, b_im, c_re, c_im):
    L, G, S, H = S5_CHUNK, S5_GROUPS, S5_STATE, S5_GROUP
    lr = jnp.minimum(lam_re.astype(F32), S5_MAX_RE)
    li = lam_im.astype(F32)
    dt = jnp.exp(log_dt.astype(F32))[..., None]
    zr, zi = lr * dt, li * dt
    d = jnp.arange(L + 1, dtype=F32)[None, :, None, None]
    mag = jnp.exp(zr[:, None] * d)
    pw = (mag * jnp.cos(zi[:, None] * d), mag * jnp.sin(zi[:, None] * d))
    lb = (pw[0][:, 1], pw[1][:, 1])
    den = lr * lr + li * li
    ratio = (((lb[0] - 1.0) * lr + lb[1] * li) / den, (lb[1] * lr - (lb[0] - 1.0) * li) / den)
    bb = _cmul((ratio[0][..., None], ratio[1][..., None]), (b_re.astype(F32), b_im.astype(F32)))
    cc = (c_re.astype(F32), c_im.astype(F32))

    def kern(x):
        cl = _cmul((cc[0][x][None], cc[1][x][None]), (pw[0][x, :L, :, None, :], pw[1][x, :L, :, None, :]))
        return jnp.einsum('dghp,gpj->dghj', cl[0], bb[0][x]) - jnp.einsum('dghp,gpj->dghj', cl[1], bb[1][x])

    def state_in(x, steps):
        p = (pw[0][x][steps][:, :, None, :], pw[1][x][steps][:, :, None, :])
        r = _cmul(p, (jnp.swapaxes(bb[0][x], 1, 2)[None], jnp.swapaxes(bb[1][x], 1, 2)[None]))
        return jnp.transpose(r[0], (1, 0, 2, 3)), jnp.transpose(r[1], (1, 0, 2, 3))

    def state_out(x, steps):
        p = (pw[0][x][steps][:, :, None, :], pw[1][x][steps][:, :, None, :])
        r = _cmul(p, (cc[0][x][None], cc[1][x][None]))
        return jnp.transpose(r[0], (1, 3, 0, 2)), jnp.transpose(r[1], (1, 3, 0, 2))

    steps = jnp.arange(L)
    wf = state_in(0, L - 1 - steps)
    wb = state_in(1, steps)
    vf = state_out(0, steps + 1)
    vb = state_out(1, L - steps)
    Q, A = S5_QUADS, S5_QUAD_GROUPS
    kd = jnp.stack([kern(0), kern(1)]).reshape(2, L, Q, A, H, H)
    kd = jnp.transpose(kd, (0, 2, 1, 5, 3, 4)).reshape(2, Q, L, H, A * H)
    wk = jnp.stack([wf[0], wf[1], wb[0], wb[1]]).reshape(4, Q, A // 2, 2, L, H, S)
    wd = jnp.transpose(wk, (0, 1, 4, 2, 5, 3, 6)).reshape(4, Q, L, (A // 2) * H, 2 * S)
    vk = jnp.stack([vf[0], -vf[1], vb[0], -vb[1]]).reshape(4, Q, A, S, L, H)
    vd = jnp.transpose(vk, (0, 1, 4, 3, 2, 5)).reshape(4, Q, L, S, A * H)
    a_re = pw[0][:, L].reshape(2, 1, G * S)
    a_im = pw[1][:, L].reshape(2, 1, G * S)
    return wd, kd, vd, a_re, a_im


def _s5_operand(u_ref):
    return jnp.concatenate([u_ref[:, s, :] for s in range(S5_CHUNK)], axis=1).astype(BF16)


def _s5_expand_w(wd_ref, w_scr):
    H, half = S5_GROUP, S5_STATE
    w_scr[...] = jnp.zeros_like(w_scr)
    lane = lax.broadcasted_iota(jnp.int32, (H, LANES), 1)
    for k in range(4):
        for s in range(S5_CHUNK):
            for c in range(S5_QUAD_GROUPS // 2):
                blk = wd_ref[k, 0, s, c * H:(c + 1) * H, :]
                cols = slice(k * S5_QSTATE + c * LANES, k * S5_QSTATE + (c + 1) * LANES)
                row = s * LANES + 2 * c * H
                w_scr[row:row + H, cols] = jnp.where(lane < half, blk, 0.0).astype(BF16)
                w_scr[row + H:row + 2 * H, cols] = jnp.where(lane >= half, blk, 0.0).astype(BF16)


def _s5_expand_mv(kd_ref, vd_ref, m_scr, v_scr):
    H, L = S5_GROUP, S5_CHUNK
    rg = lax.broadcasted_iota(jnp.int32, (LANES, LANES), 0) // H
    lg = lax.broadcasted_iota(jnp.int32, (LANES, LANES), 1) // H
    tiles = []
    for x in range(2):
        tiles.append([jnp.where(rg == lg, jnp.concatenate([kd_ref[x, 0, d]] * S5_QUAD_GROUPS, axis=0), 0.0)
                      for d in range(L)])
    for s in range(L):
        for t in range(L):
            blk = tiles[0][t - s] if t > s else (tiles[1][s - t] if s > t else tiles[0][0] + tiles[1][0])
            m_scr[s * LANES:(s + 1) * LANES, t * LANES:(t + 1) * LANES] = blk.astype(BF16)
    lg2 = lax.broadcasted_iota(jnp.int32, (S5_STATE, LANES), 1) // H
    for k in range(4):
        for t in range(L):
            blk = vd_ref[k, 0, t]
            for b in range(S5_QUAD_GROUPS):
                row = k * S5_QSTATE + b * S5_STATE
                v_scr[row:row + S5_STATE, t * LANES:(t + 1) * LANES] = jnp.where(lg2 == b, blk, 0.0).astype(BF16)


def _s5_local_kernel(u_ref, wd_ref, sfr_ref, sfi_ref, sbr_ref, sbi_ref, w_scr):
    @pl.when(pl.program_id(1) == 0)
    def _():
        _s5_expand_w(wd_ref, w_scr)

    s = _dot(_s5_operand(u_ref), w_scr[...])
    n = S5_QSTATE
    sfr_ref[...] = s[:, 0:n]
    sfi_ref[...] = s[:, n:2 * n]
    sbr_ref[...] = s[:, 2 * n:3 * n]
    sbi_ref[...] = s[:, 3 * n:4 * n]


def _s5_local_states(u3, wd):
    rows = u3.shape[0]
    tr = min(256, rows)
    assert rows % tr == 0
    out = jax.ShapeDtypeStruct((rows, S5_GROUPS * S5_STATE), F32)
    return pl.pallas_call(
        _s5_local_kernel,
        grid=(S5_QUADS, rows // tr),
        in_specs=[
            pl.BlockSpec((tr, S5_CHUNK, LANES), lambda q, i: (i, 0, q)),
            pl.BlockSpec((4, 1) + wd.shape[2:], lambda q, i: (0, q, 0, 0, 0)),
        ],
        out_specs=[pl.BlockSpec((tr, S5_QSTATE), lambda q, i: (i, q))] * 4,
        out_shape=[out] * 4,
        scratch_shapes=[pltpu.VMEM((S5_QCOLS, 4 * S5_QSTATE), BF16)],
        compiler_params=_params(("arbitrary", "arbitrary")),
        name="s5_local",
    )(u3, wd)


def _s5_scan_kernel(sr_ref, si_ref, ar_ref, ai_ref, h0r_ref, h0i_ref, hr_ref, hi_ref, fr_ref, fi_ref,
                    hr_scr, hi_scr, *, reverse):
    i = pl.program_id(1)
    n_i = pl.num_programs(1)
    steps = sr_ref.shape[1]

    @pl.when(i == 0)
    def _():
        hr_scr[...] = h0r_ref[...]
        hi_scr[...] = h0i_ref[...]

    ar = ar_ref[...]
    ai = ai_ref[...]

    def body(j, carry):
        hr, hi = carry
        c = (steps - 1 - j) if reverse else j
        hr_ref[:, c, :] = hr
        hi_ref[:, c, :] = hi
        nr = ar * hr - ai * hi + sr_ref[:, c, :]
        ni = ar * hi + ai * hr + si_ref[:, c, :]
        return nr, ni

    hr, hi = lax.fori_loop(0, steps, body, (hr_scr[...], hi_scr[...]))
    hr_scr[...] = hr
    hi_scr[...] = hi

    @pl.when(i == n_i - 1)
    def _():
        fr_ref[...] = hr
        fi_ref[...] = hi


def _s5_chunk_scan(s_re, s_im, a_re, a_im, h0_re, h0_im, bsz, reverse):
    _, n_c, width = s_re.shape
    tr = min(64, n_c)
    tc = 512
    n_i = n_c // tr

    def rmap(j, i):
        return (0, (n_i - 1 - i) if reverse else i, j)

    big = jax.ShapeDtypeStruct((bsz, n_c, width), F32)
    small = jax.ShapeDtypeStruct((bsz, width), F32)
    return pl.pallas_call(
        functools.partial(_s5_scan_kernel, reverse=reverse),
        grid=(width // tc, n_i),
        in_specs=[
            pl.BlockSpec((bsz, tr, tc), rmap), pl.BlockSpec((bsz, tr, tc), rmap),
            pl.BlockSpec((1, tc), lambda j, i: (0, j)), pl.BlockSpec((1, tc), lambda j, i: (0, j)),
            pl.BlockSpec((bsz, tc), lambda j, i: (0, j)), pl.BlockSpec((bsz, tc), lambda j, i: (0, j)),
        ],
        out_specs=[
            pl.BlockSpec((bsz, tr, tc), rmap), pl.BlockSpec((bsz, tr, tc), rmap),
            pl.BlockSpec((bsz, tc), lambda j, i: (0, j)), pl.BlockSpec((bsz, tc), lambda j, i: (0, j)),
        ],
        out_shape=[big, big, small, small],
        scratch_shapes=[pltpu.VMEM((bsz, tc), F32), pltpu.VMEM((bsz, tc), F32)],
        compiler_params=_params(("parallel", "arbitrary")),
        name="s5_scan_rev" if reverse else "s5_scan_fwd",
    )(s_re, s_im, a_re, a_im, h0_re, h0_im)


def _s5_out_kernel(u_ref, kd_ref, vd_ref, hfr_ref, hfi_ref, hbr_ref, hbi_ref, y_ref, m_scr, v_scr):
    @pl.when(pl.program_id(1) == 0)
    def _():
        _s5_expand_mv(kd_ref, vd_ref, m_scr, v_scr)

    n = S5_QSTATE
    y = _dot(_s5_operand(u_ref), m_scr[...])
    for k, h_ref in enumerate((hfr_ref, hfi_ref, hbr_ref, hbi_ref)):
        y += _dot(h_ref[...].astype(BF16), v_scr[k * n:(k + 1) * n, :])
    for t in range(S5_CHUNK):
        y_ref[:, t, :] = y[:, t * LANES:(t + 1) * LANES]


def _s5_outputs(u3, kd, vd, h_states):
    rows = u3.shape[0]
    tr = min(256, rows)
    return pl.pallas_call(
        _s5_out_kernel,
        grid=(S5_QUADS, rows // tr),
        in_specs=[
            pl.BlockSpec((tr, S5_CHUNK, LANES), lambda q, i: (i, 0, q)),
            pl.BlockSpec((2, 1) + kd.shape[2:], lambda q, i: (0, q, 0, 0, 0)),
            pl.BlockSpec((4, 1) + vd.shape[2:], lambda q, i: (0, q, 0, 0, 0)),
        ] + [pl.BlockSpec((tr, S5_QSTATE), lambda q, i: (i, q))] * 4,
        out_specs=pl.BlockSpec((tr, S5_CHUNK, LANES), lambda q, i: (i, 0, q)),
        out_shape=jax.ShapeDtypeStruct((rows, S5_CHUNK, S5_WIDTH), F32),
        scratch_shapes=[pltpu.VMEM((S5_QCOLS, S5_QCOLS), BF16), pltpu.VMEM((4 * S5_QSTATE, S5_QCOLS), BF16)],
        compiler_params=_params(("arbitrary", "arbitrary")),
        name="s5_out",
    )(u3, kd, vd, *h_states)


def _s5_branch(u, tables, seq_len, init, want_y):
    wd, kd, vd, a_re, a_im = tables
    t = u.shape[0]
    bsz = t // seq_len
    n_c = seq_len // S5_CHUNK
    width = S5_GROUPS * S5_STATE
    u3 = u.reshape(t // S5_CHUNK, S5_CHUNK, S5_WIDTH)
    s_loc = [s.reshape(bsz, n_c, width) for s in _s5_local_states(u3, wd)]
    hfr, hfi, ffr, ffi = _s5_chunk_scan(s_loc[0], s_loc[1], a_re[0], a_im[0], init[0], init[1], bsz, False)
    hbr, hbi, fbr, fbi = _s5_chunk_scan(s_loc[2], s_loc[3], a_re[1], a_im[1], init[2], init[3], bsz, True)
    y = None
    if want_y:
        h_states = [h.reshape(bsz * n_c, width) for h in (hfr, hfi, hbr, hbi)]
        y = _s5_outputs(u3, kd, vd, h_states).reshape(t, S5_WIDTH)
    return y, (ffr, ffi, fbr, fbi)


def _sigmoid(x):
    return 0.5 * jnp.tanh(0.5 * x) + 0.5


def _gelu_tanh(x):
    return 0.5 * x * (1.0 + jnp.tanh(math.sqrt(2.0 / math.pi) * (x + 0.044715 * (x * x * x))))


def _merge_kernel(x_ref, ym_ref, yn_ref, ys_ref, u_ref, gm_ref, gn_ref, gs_ref, d_ref, gw_ref, gb_ref,
                  wm_ref, wn_ref, ws_ref, wo_ref, gate_ref, npost_ref, o_ref):
    ys = ys_ref[...] + d_ref[...] * u_ref[...]
    g = _gelu_tanh(ys)
    gg = g * _sigmoid(_dot(g.astype(BF16), gw_ref[...]) + gb_ref[...])
    y = _sigmoid(gm_ref[...].astype(F32)) * _dot(ym_ref[...], wm_ref[...])
    y += _sigmoid(gn_ref[...].astype(F32)) * _dot(yn_ref[...], wn_ref[...])
    y += _sigmoid(gs_ref[...].astype(F32)) * _dot(gg.astype(BF16), ws_ref[...])
    out = _dot(y.astype(BF16), wo_ref[...])
    o_ref[...] = x_ref[...] + gate_ref[0] * _rms(out, npost_ref[...])


def _merge(x, ym, yn, ys, u, px, wts, gate, npost, rows_per_mod):
    t = x.shape[0]
    tm = min(512, rows_per_mod)
    per = rows_per_mod // tm
    s5_d, glu_w, glu_b, w_m, w_n, w_s, w_o = wts
    row = lambda i: (i, 0)
    const = lambda i: (0, 0)
    return pl.pallas_call(
        _merge_kernel,
        grid=(t // tm,),
        in_specs=[
            pl.BlockSpec((tm, D_MODEL), row),
            pl.BlockSpec((tm, M_WIDTH), row),
            pl.BlockSpec((tm, NA_WIDTH), row),
            pl.BlockSpec((tm, S5_WIDTH), row),
            pl.BlockSpec((tm, S5_WIDTH), row),
            pl.BlockSpec((tm, D_MODEL), lambda i: (i, COL_GATE)),
            pl.BlockSpec((tm, D_MODEL), lambda i: (i, COL_GATE + 1)),
            pl.BlockSpec((tm, D_MODEL), lambda i: (i, COL_GATE + 2)),
            pl.BlockSpec((1, S5_WIDTH), const),
            pl.BlockSpec((S5_WIDTH, S5_WIDTH), const),
            pl.BlockSpec((1, S5_WIDTH), const),
            pl.BlockSpec((M_WIDTH, D_MODEL), const),
            pl.BlockSpec((NA_WIDTH, D_MODEL), const),
            pl.BlockSpec((S5_WIDTH, D_MODEL), const),
            pl.BlockSpec((D_MODEL, D_MODEL), const),
            pl.BlockSpec((1, 1, D_MODEL), lambda i: (i // per, 0, 0)),
            pl.BlockSpec((1, D_MODEL), const),
        ],
        out_specs=pl.BlockSpec((tm, D_MODEL), row),
        out_shape=jax.ShapeDtypeStruct((t, D_MODEL), F32),
        compiler_params=_params(("parallel",)),
        name="merge",
    )(x, ym, yn, ys, u, px, px, px, s5_d, glu_w, glu_b, w_m, w_n, w_s, w_o, gate, npost)


def _swiglu(h, wg, wu, wd):
    a = _dot(h, wg)
    a = a * jax.nn.sigmoid(a) * _dot(h, wu)
    return _dot(a.astype(BF16), wd)


def _ffn_kernel(x_ref, g_ref, sh_ref, sc_ref, wg_ref, wu_ref, wd_ref, gate_ref, npost_ref, o_ref, *, ff_tile):
    h = (_rms(x_ref[...], g_ref[...]) * (1.0 + sc_ref[0]) + sh_ref[0]).astype(BF16)
    acc = jnp.zeros(x_ref.shape, F32)
    for j in range(wg_ref.shape[1] // ff_tile):
        cols = slice(j * ff_tile, (j + 1) * ff_tile)
        acc += _swiglu(h, wg_ref[:, cols], wu_ref[:, cols], wd_ref[cols, :])
    o_ref[...] = x_ref[...] + gate_ref[0] * _rms(acc, npost_ref[...])


def _ffn(x, g, shift, scale, w_gate, w_up, w_down, gate, npost, rows_per_mod):
    t = x.shape[0]
    ff = w_gate.shape[1]
    ff_tile = ff // 2
    tm = min(512, rows_per_mod)
    per = rows_per_mod // tm
    assert ff_tile % LANES == 0
    resident = pl.Buffered(1)
    return pl.pallas_call(
        functools.partial(_ffn_kernel, ff_tile=ff_tile),
        grid=(t // tm,),
        in_specs=[
            pl.BlockSpec((tm, D_MODEL), lambda i: (i, 0)),
            pl.BlockSpec((1, D_MODEL), lambda i: (0, 0)),
            pl.BlockSpec((1, 1, D_MODEL), lambda i: (i // per, 0, 0)),
            pl.BlockSpec((1, 1, D_MODEL), lambda i: (i // per, 0, 0)),
            pl.BlockSpec((D_MODEL, ff), lambda i: (0, 0), pipeline_mode=resident),
            pl.BlockSpec((D_MODEL, ff), lambda i: (0, 0), pipeline_mode=resident),
            pl.BlockSpec((ff, D_MODEL), lambda i: (0, 0), pipeline_mode=resident),
            pl.BlockSpec((1, 1, D_MODEL), lambda i: (i // per, 0, 0)),
            pl.BlockSpec((1, D_MODEL), lambda i: (0, 0)),
        ],
        out_specs=pl.BlockSpec((tm, D_MODEL), lambda i: (i, 0)),
        out_shape=jax.ShapeDtypeStruct((t, D_MODEL), F32),
        compiler_params=_params(("parallel",)),
        name="ffn",
    )(x, g, shift, scale, w_gate, w_up, w_down, gate, npost)


MOE_ROWS = MOE_TILE * TOP_K + N_EXPERTS * MOE_GRAIN
MOE_META = 2 * N_EXPERTS


def _moe_dispatch_kernel(x_ref, g_ref, sh_ref, sc_ref, rh_ref, rl_ref, xs_ref, route_ref, meta_ref):
    tm = x_ref.shape[0]
    h = _rms(x_ref[...], g_ref[...]) * (1.0 + sc_ref[0]) + sh_ref[0]
    hi = h.astype(BF16)
    lo = (h - hi.astype(F32)).astype(BF16)
    logits = _dot(hi, rh_ref[...]) + _dot(lo, rh_ref[...]) + _dot(hi, rl_ref[...])
    lane = lax.broadcasted_iota(jnp.int32, logits.shape, 1)
    logits = jnp.where(lane < N_EXPERTS, logits, -jnp.inf)
    m1 = jnp.max(logits, axis=-1, keepdims=True)
    i1 = jnp.min(jnp.where(logits == m1, lane, LANES), axis=-1, keepdims=True)
    rest = jnp.where(lane == i1, -jnp.inf, logits)
    m2 = jnp.max(rest, axis=-1, keepdims=True)
    i2 = jnp.min(jnp.where(rest == m2, lane, LANES), axis=-1, keepdims=True)
    e2 = jnp.exp(m2 - m1)
    sel1 = jnp.where(lane == i1, 1.0, 0.0)
    sel2 = jnp.where(lane == i2, 1.0, 0.0)
    sel = sel1 + sel2
    sel1_t, sel2_t = sel1.T, sel2.T
    sel_t = sel1_t + sel2_t

    ti = lax.broadcasted_iota(jnp.int32, (tm, tm), 0)
    si = lax.broadcasted_iota(jnp.int32, (tm, tm), 1)
    rank_c = _dot(jnp.where(si < ti, 1.0, 0.0).astype(BF16), sel.astype(BF16))
    rank_r = _dot(sel_t.astype(BF16), jnp.where(ti < si, 1.0, 0.0).astype(BF16))
    blk_row = jnp.ceil(jnp.sum(sel, axis=0, keepdims=True) * (1.0 / MOE_GRAIN))
    blk_col = jnp.ceil(jnp.sum(sel_t, axis=1, keepdims=True) * (1.0 / MOE_GRAIN))
    ea = lax.broadcasted_iota(jnp.int32, (LANES, LANES), 0)
    eb = lax.broadcasted_iota(jnp.int32, (LANES, LANES), 1)
    off_row = _dot(jnp.broadcast_to(blk_row, (8, LANES)).astype(BF16),
                   jnp.where(ea < eb, 1.0, 0.0).astype(BF16))[0:1, :]
    off_col = _dot(jnp.where(eb < ea, 1.0, 0.0).astype(BF16),
                   jnp.broadcast_to(blk_col, (LANES, LANES)).astype(BF16))[:, 0:1]

    pos_c = off_row * MOE_GRAIN + rank_c
    pos1_c = jnp.sum(jnp.where(lane == i1, pos_c, 0.0), axis=-1, keepdims=True)
    pos2_c = jnp.sum(jnp.where(lane == i2, pos_c, 0.0), axis=-1, keepdims=True)
    route_ref[...] = (jnp.where(lane == 0, pos1_c, 0.0) + jnp.where(lane == 1, pos2_c, 0.0)
                      + jnp.where(lane == 2, 1.0 / (1.0 + e2), 0.0) + jnp.where(lane == 3, e2 / (1.0 + e2), 0.0))
    pos_r = off_col * MOE_GRAIN + rank_r
    pos1_r = jnp.sum(sel1_t * pos_r, axis=0, keepdims=True)
    pos2_r = jnp.sum(sel2_t * pos_r, axis=0, keepdims=True)
    half = MOE_ROWS // 2
    for c in range(2):
        slot = (lax.broadcasted_iota(jnp.int32, (half, tm), 0) + c * half).astype(F32)
        onehot = jnp.where(slot == pos1_r, 1.0, jnp.where(slot == pos2_r, 1.0, 0.0)).astype(BF16)
        xs_ref[c * half:(c + 1) * half, :] = _dot(onehot, hi).astype(BF16)

    lane8 = lax.broadcasted_iota(jnp.int32, (8, LANES), 1)
    blk8 = jnp.broadcast_to(blk_row, (8, LANES))
    off8 = pltpu.roll(jnp.broadcast_to(off_row, (8, LANES)), N_EXPERTS, 1)
    meta = jnp.where(lane8 < N_EXPERTS, blk8, jnp.where(lane8 < MOE_META, off8, 0.0))
    meta_ref[0] = meta.astype(jnp.int32)


def _moe_dispatch(x, g, shift, scale, r_hi, r_lo, rows_per_mod):
    t = x.shape[0]
    tm = MOE_TILE
    assert rows_per_mod % tm == 0
    per = rows_per_mod // tm
    n_t = t // tm
    return pl.pallas_call(
        _moe_dispatch_kernel,
        grid=(n_t,),
        in_specs=[
            pl.BlockSpec((tm, D_MODEL), lambda i: (i, 0)),
            pl.BlockSpec((1, D_MODEL), lambda i: (0, 0)),
            pl.BlockSpec((1, 1, D_MODEL), lambda i: (i // per, 0, 0)),
            pl.BlockSpec((1, 1, D_MODEL), lambda i: (i // per, 0, 0)),
            pl.BlockSpec((D_MODEL, LANES), lambda i: (0, 0)),
            pl.BlockSpec((D_MODEL, LANES), lambda i: (0, 0)),
        ],
        out_specs=[
            pl.BlockSpec((MOE_ROWS, D_MODEL), lambda i: (i, 0)),
            pl.BlockSpec((tm, LANES), lambda i: (i, 0)),
            pl.BlockSpec((1, 8, LANES), lambda i: (i, 0, 0)),
        ],
        out_shape=[
            jax.ShapeDtypeStruct((n_t * MOE_ROWS, D_MODEL), BF16),
            jax.ShapeDtypeStruct((t, LANES), F32),
            jax.ShapeDtypeStruct((n_t, 8, LANES), jnp.int32),
        ],
        compiler_params=_params(("parallel",)),
        name="moe_dispatch",
    )(x, g, shift, scale, r_hi, r_lo)


MOE_FF_TILE = 512


def _moe_combine_kernel(ys_ref, route_ref, x_ref, gate_ref, npost_ref, o_ref):
    tm = x_ref.shape[0]
    lane = lax.broadcasted_iota(jnp.int32, (tm, LANES), 1)
    route = route_ref[...]
    pos1, pos2, w1, w2 = (jnp.sum(jnp.where(lane == k, route, 0.0), axis=-1, keepdims=True) for k in range(4))
    width = 512
    acc = jnp.zeros((tm, D_MODEL), F32)
    for c in range(MOE_ROWS // width):
        slot = (lax.broadcasted_iota(jnp.int32, (tm, width), 1) + c * width).astype(F32)
        wmat = jnp.where(slot == pos1, w1, 0.0) + jnp.where(slot == pos2, w2, 0.0)
        acc += _dot(wmat.astype(BF16), ys_ref[c * width:(c + 1) * width, :])
    o_ref[...] = x_ref[...] + gate_ref[0] * _rms(acc, npost_ref[...])


MOE_BLOCK = 512


def _moe_plan(meta, n_t):
    per_tile = MOE_ROWS // MOE_GRAIN
    per_blk = MOE_BLOCK // MOE_GRAIN
    nblk = meta[:, 0, :N_EXPERTS]
    first = meta[:, 0, N_EXPERTS:MOE_META]
    tot = jnp.sum(nblk, axis=0)
    tot_pad = ((tot + per_blk - 1) // per_blk) * per_blk
    ends = jnp.cumsum(tot_pad)
    seg_dst = (ends - tot_pad)[None, :] + jnp.cumsum(nblk, axis=0) - nblk
    k = jnp.arange(per_tile, dtype=jnp.int32)[None, :, None]
    inside = (k >= first[:, None, :]) & (k < (first + nblk)[:, None, :])
    dst = jnp.sum(jnp.where(inside, seg_dst[:, None, :] + k - first[:, None, :], 0), axis=-1)
    used = jnp.any(inside, axis=-1).reshape(-1)
    dst = dst.reshape(-1).astype(jnp.int32)
    n_big = (n_t * per_tile + N_EXPERTS * per_blk) // per_blk
    big = jnp.arange(n_big, dtype=jnp.int32)[:, None] * per_blk
    expert = jnp.minimum(jnp.sum(big >= ends[None, :], axis=-1), N_EXPERTS - 1).astype(jnp.int32)
    valid = (big[:, 0] < ends[-1]).astype(jnp.int32)
    src = jnp.zeros((n_big * per_blk,), jnp.int32).at[jnp.where(used, dst, n_big * per_blk)].set(
        jnp.arange(dst.shape[0], dtype=jnp.int32), mode='drop')
    return jnp.where(used, dst, 0), jnp.concatenate([expert, valid, src]), n_big


def _moe_global_kernel(plan_ref, *refs, ff_tile):
    per_blk = MOE_BLOCK // MOE_GRAIN
    x_refs = refs[:per_blk]
    wg_ref, wu_ref, wd_ref, out_ref, xin_scr = refs[per_blk:]
    b = pl.program_id(0)
    valid = plan_ref[pl.num_programs(0) + b]
    ff = wg_ref.shape[2]

    @pl.when(valid == 1)
    def _():
        for k in range(per_blk):
            xin_scr[k * MOE_GRAIN:(k + 1) * MOE_GRAIN, :] = x_refs[k][...]
        xin = xin_scr[...]
        y = jnp.zeros((MOE_BLOCK, D_MODEL), F32)
        for j in range(ff // ff_tile):
            cols = slice(j * ff_tile, (j + 1) * ff_tile)
            y += _swiglu(xin, wg_ref[0, :, cols], wu_ref[0, :, cols], wd_ref[0, cols, :])
        out_ref[...] = y.astype(out_ref.dtype)

    @pl.when(valid == 0)
    def _():
        out_ref[...] = jnp.zeros_like(out_ref)


def _moe_experts_global(xs, plan, n_big, w_gate, w_up, w_down):
    n_e, _, ff = w_gate.shape
    per_blk = MOE_BLOCK // MOE_GRAIN
    resident = pl.Buffered(1)

    def src_spec(k):
        return pl.BlockSpec((MOE_GRAIN, D_MODEL), lambda b, p: (p[2 * n_big + b * per_blk + k], 0))

    grid_spec = pltpu.PrefetchScalarGridSpec(
        num_scalar_prefetch=1,
        grid=(n_big,),
        in_specs=[src_spec(k) for k in range(per_blk)] + [
            pl.BlockSpec((1, D_MODEL, ff), lambda b, p: (p[b], 0, 0), pipeline_mode=resident),
            pl.BlockSpec((1, D_MODEL, ff), lambda b, p: (p[b], 0, 0), pipeline_mode=resident),
            pl.BlockSpec((1, ff, D_MODEL), lambda b, p: (p[b], 0, 0), pipeline_mode=resident),
        ],
        out_specs=pl.BlockSpec((MOE_BLOCK, D_MODEL), lambda b, p: (b, 0)),
        scratch_shapes=[pltpu.VMEM((MOE_BLOCK, D_MODEL), BF16)],
    )
    return pl.pallas_call(
        functools.partial(_moe_global_kernel, ff_tile=MOE_FF_TILE),
        grid_spec=grid_spec,
        out_shape=jax.ShapeDtypeStruct((n_big * MOE_BLOCK, D_MODEL), BF16),
        compiler_params=_params(("arbitrary",)),
        name="moe_experts",
    )(plan, *([xs] * per_blk), w_gate, w_up, w_down)


def _moe_global(x, g, shift, scale, r_hi, r_lo, w_gate, w_up, w_down, gate, npost, rows_per_mod):
    xs, route, meta = _moe_dispatch(x, g, shift, scale, r_hi, r_lo, rows_per_mod)
    n_t = xs.shape[0] // MOE_ROWS
    block_map, plan, n_big = _moe_plan(meta, n_t)
    yg = _moe_experts_global(xs, plan, n_big, w_gate, w_up, w_down)
    return _moe_gather_combine(yg, block_map, route, x, gate, npost, rows_per_mod)


def _moe_gather_combine_kernel(map_ref, *refs):
    per_tile = MOE_ROWS // MOE_GRAIN
    y_refs = refs[:per_tile]
    route_ref, x_ref, gate_ref, npost_ref, o_ref, ys_scr = refs[per_tile:]
    del map_ref
    for k in range(per_tile):
        ys_scr[k * MOE_GRAIN:(k + 1) * MOE_GRAIN, :] = y_refs[k][...]
    _moe_combine_kernel(ys_scr, route_ref, x_ref, gate_ref, npost_ref, o_ref)


def _moe_gather_combine(yg, block_map, route, x, gate, npost, rows_per_mod):
    t = x.shape[0]
    tm = MOE_TILE
    per = rows_per_mod // tm
    per_tile = MOE_ROWS // MOE_GRAIN

    def y_spec(k):
        return pl.BlockSpec((MOE_GRAIN, D_MODEL), lambda i, m: (m[i * per_tile + k], 0))

    grid_spec = pltpu.PrefetchScalarGridSpec(
        num_scalar_prefetch=1,
        grid=(t // tm,),
        in_specs=[y_spec(k) for k in range(per_tile)] + [
            pl.BlockSpec((tm, LANES), lambda i, m: (i, 0)),
            pl.BlockSpec((tm, D_MODEL), lambda i, m: (i, 0)),
            pl.BlockSpec((1, 1, D_MODEL), lambda i, m: (i // per, 0, 0)),
            pl.BlockSpec((1, D_MODEL), lambda i, m: (0, 0)),
        ],
        out_specs=pl.BlockSpec((tm, D_MODEL), lambda i, m: (i, 0)),
        scratch_shapes=[pltpu.VMEM((MOE_ROWS, D_MODEL), BF16)],
    )
    return pl.pallas_call(
        _moe_gather_combine_kernel,
        grid_spec=grid_spec,
        out_shape=jax.ShapeDtypeStruct((t, D_MODEL), F32),
        compiler_params=_params(("arbitrary",)),
        name="moe_combine",
    )(block_map, *([yg] * per_tile), route, x, gate, npost)


def _reorder_w_in(w):
    o = 4 * M_WIDTH
    gates = jnp.pad(w[:, o:o + M_GATES], ((0, 0), (0, LANES - M_GATES)))
    na = w[:, o + M_GATES:o + M_GATES + 3 * NA_WIDTH]
    u0 = o + M_GATES + 3 * NA_WIDTH
    main = jnp.concatenate([w[:, :o], w[:, u0 + S5_WIDTH:], na], axis=1)
    return main.astype(BF16), gates.astype(BF16), w[:, u0:u0 + S5_WIDTH].astype(BF16)


def _row(v):
    return v.reshape(1, -1).astype(F32)


def _token_mixers(x, xc, mx, mc, lp, rope, seq_len, ctx_len, ctx_out):
    bsz = x.shape[0] // seq_len
    w_main, w_gates, w_u = _reorder_w_in(lp['w_in'])
    g_pre = _row(lp['norm_mix_pre'])
    px, gx, ux = _inproj(x, g_pre, mx[0], mx[1], w_main, w_gates, w_u, seq_len)
    pc, gc, uc = _inproj(xc, g_pre, mc[0], mc[1], w_main, w_gates, w_u, xc.shape[0])

    conv_w = lp['m_conv_w'].astype(F32)
    conv_b = _row(lp['m_conv_b'])
    gate_b = jnp.pad(_row(lp['m_gate_b']), ((0, 0), (0, LANES - M_GATES)))
    m_norm = _row(lp['m_norm'])
    qk_c = _qkconv(pc, conv_w, conv_b, rope[0][:ctx_len], rope[1][:ctx_len], ctx_len, False)
    qk_x = _qkconv(px, conv_w, conv_b, rope[0], rope[1], seq_len, True)
    zero = _mlstm_zero_state(bsz)
    ym_c, fin_f, fin_b = _mlstm_bidir(qk_c, pc, gc, gate_b, m_norm, zero, zero, ctx_len)
    ym_x, _, _ = _mlstm_bidir(qk_x, px, gx, gate_b, m_norm, fin_f, fin_b, seq_len)

    yn_x = _na_attention(px, pc, _na_bias_table(lp['na_rpb']), seq_len, ctx_len)

    tables = _s5_tables(lp['s5_lam_re'], lp['s5_lam_im'], lp['s5_log_dt'], lp['s5_b_re'], lp['s5_b_im'],
                        lp['s5_c_re'], lp['s5_c_im'])
    zs = jnp.zeros((bsz, S5_GROUPS * S5_STATE), F32)
    ys_c, fin_s = _s5_branch(uc, tables, ctx_len, (zs, zs, zs, zs), ctx_out)
    ys_x, _ = _s5_branch(ux, tables, seq_len, fin_s, True)

    wts = (_row(lp['s5_d']), lp['s5_glu_w'].astype(BF16), _row(lp['s5_glu_b']),
           lp['w_branch_m'].astype(BF16), lp['w_branch_na'].astype(BF16), lp['w_branch_s5'].astype(BF16),
           lp['w_out'].astype(BF16))
    g_post = _row(lp['norm_mix_post'])
    x = _merge(x, ym_x, yn_x, ys_x, ux, px, wts, mx[2], g_post, seq_len)
    if ctx_out:
        yn_c = _ctx_attention(pc, ctx_len)
        xc = _merge(xc, ym_c, yn_c, ys_c, uc, pc, wts, mc[2], g_post, xc.shape[0])
    return x, xc


def kernel(x, c, ctx, c_ctx, ada_w, ada_b, norm_mix_pre, norm_mix_post, norm_ffn_pre, norm_ffn_post, w_in, m_gate_b, m_conv_w, m_conv_b, m_norm, na_rpb, s5_lam_re, s5_lam_im, s5_log_dt, s5_b_re, s5_b_im, s5_c_re, s5_c_im, s5_d, s5_glu_w, s5_glu_b, w_branch_m, w_branch_na, w_branch_s5, w_out, ffn_w_gate, ffn_w_up, ffn_w_down, moe_router, moe_w_gate, moe_w_up, moe_w_down):
    bsz, seq_len, d = x.shape
    ctx_len = ctx.shape[1]
    depth = w_in.shape[0]
    assert d == D_MODEL and seq_len % (GRID_W * NA_ROWS) == 0 and ctx_len % M_CHUNK == 0
    rope = _rope_tables(seq_len)
    xf = x.reshape(bsz * seq_len, d)
    xc = ctx.reshape(bsz * ctx_len, d)
    cvecs = jnp.zeros((16, d), F32).at[:bsz].set(c).at[bsz].set(c_ctx)
    for l in range(depth):
        last = l == depth - 1
        mod = _adaln(cvecs, ada_w[l].astype(BF16), _row(ada_b[l]))
        mx = [mod[:bsz, k * d:(k + 1) * d].reshape(bsz, 1, d) for k in range(N_MOD)]
        mc = [mod[bsz:bsz + 1, k * d:(k + 1) * d].reshape(1, 1, d) for k in range(N_MOD)]
        lp = {
            'norm_mix_pre': norm_mix_pre[l], 'norm_mix_post': norm_mix_post[l],
            'w_in': w_in[l], 'm_gate_b': m_gate_b[l], 'm_conv_w': m_conv_w[l], 'm_conv_b': m_conv_b[l],
            'm_norm': m_norm[l], 'na_rpb': na_rpb[l], 's5_lam_re': s5_lam_re[l], 's5_lam_im': s5_lam_im[l],
            's5_log_dt': s5_log_dt[l], 's5_b_re': s5_b_re[l], 's5_b_im': s5_b_im[l], 's5_c_re': s5_c_re[l],
            's5_c_im': s5_c_im[l], 's5_d': s5_d[l], 's5_glu_w': s5_glu_w[l], 's5_glu_b': s5_glu_b[l],
            'w_branch_m': w_branch_m[l], 'w_branch_na': w_branch_na[l], 'w_branch_s5': w_branch_s5[l],
            'w_out': w_out[l],
        }
        xf, xc = _token_mixers(xf, xc, mx, mc, lp, rope, seq_len, ctx_len, not last)
        j = l // 2
        g_pre, g_post = _row(norm_ffn_pre[l]), _row(norm_ffn_post[l])
        if l % 2 == 0:
            wg, wu, wd = ffn_w_gate[j].astype(BF16), ffn_w_up[j].astype(BF16), ffn_w_down[j].astype(BF16)

            def channel(h, m, rows):
                return _ffn(h, g_pre, m[3], m[4], wg, wu, wd, m[5], g_post, rows)
        else:
            wg, wu, wd = moe_w_gate[j].astype(BF16), moe_w_up[j].astype(BF16), moe_w_down[j].astype(BF16)
            router = jnp.pad(moe_router[j].astype(F32), ((0, 0), (0, LANES - N_EXPERTS)))
            r_hi = router.astype(BF16)
            r_lo = (router - r_hi.astype(F32)).astype(BF16)

            def channel(h, m, rows):
                return _moe_global(h, g_pre, m[3], m[4], r_hi, r_lo, wg, wu, wd, m[5], g_post, rows)
        xf = channel(xf, mx, seq_len)
        if not last:
            xc = channel(xc, mc, xc.shape[0])
    return xf.reshape(bsz, seq_len, d)
```

```python
import functools
import math

import jax
import jax.numpy as jnp
from jax import lax
from jax.experimental import pallas as pl
from jax.experimental.pallas import tpu as pltpu

F32 = jnp.float32
BF16 = jnp.bfloat16

D_MODEL = 1024
EPS = 1e-6
N_MOD = 6
GRID_W = 64

M_HEADS = 4
M_HEAD_DIM = 128
M_WIDTH = M_HEADS * M_HEAD_DIM
M_GATES = 4 * M_HEADS
M_CHUNK = 128
ROPE_BASE = 10000.0

NA_HEADS = 8
NA_HEAD_DIM = 64
NA_WIDTH = NA_HEADS * NA_HEAD_DIM
WIN_R = 8
WIN_C = 16
NEG_BIG = -1e30
LOG2E = math.log2(math.e)

S5_GROUP = 16
S5_WIDTH = 512
S5_GROUPS = S5_WIDTH // S5_GROUP
S5_STATE = 64
S5_MAX_RE = -1e-4

LANES = 128
VMEM_LIMIT = 52 * 1024 * 1024

S5_CHUNK = 16
S5_QUAD_GROUPS = LANES // S5_GROUP
S5_QUADS = S5_GROUPS // S5_QUAD_GROUPS
S5_QCOLS = S5_CHUNK * LANES
S5_QSTATE = S5_QUAD_GROUPS * S5_STATE

N_EXPERTS = 8
TOP_K = 2
MOE_TILE = 1024
MOE_GRAIN = 64

COL_QK, COL_V, COL_O, COL_NAQ, COL_NAK, COL_NAV = 0, 2, 3, 10, 11, 12
COL_GATE = 2
IN_MAIN = 6656


def _params(sem):
    return pltpu.CompilerParams(dimension_semantics=sem, vmem_limit_bytes=VMEM_LIMIT)


def _dot(a, b):
    return jnp.dot(a, b, preferred_element_type=F32)


def _dot_nt(a, b):
    return lax.dot_general(a, b, (((1,), (1,)), ((), ())), preferred_element_type=F32)


def _split3(x):
    hi = x.astype(BF16)
    r1 = x - hi.astype(F32)
    mid = r1.astype(BF16)
    lo = (r1 - mid.astype(F32)).astype(BF16)
    return hi, mid, lo


def _rms(x, g):
    return x * lax.rsqrt(jnp.mean(x * x, axis=-1, keepdims=True) + EPS) * g


def _inproj_kernel(x_ref, g_ref, sh_ref, sc_ref, w_ref, wg_ref, wu_ref, o_ref, og_ref, ou_ref, *, tn):
    h = _rms(x_ref[...], g_ref[...]) * (1.0 + sc_ref[0]) + sh_ref[0]
    hb = h.astype(BF16)
    og_ref[...] = _dot(hb, wg_ref[...])
    ou_ref[...] = _dot(hb, wu_ref[...])
    for c in range(IN_MAIN // tn):
        cols = slice(c * tn, (c + 1) * tn)
        o_ref[:, cols] = _dot(hb, w_ref[:, cols]).astype(o_ref.dtype)


def _inproj(x, g, shift, scale, w_main, w_gates, w_u, rows_per_mod):
    t = x.shape[0]
    tm = min(512, rows_per_mod)
    tn = IN_MAIN // 4
    assert rows_per_mod % tm == 0 and t % tm == 0 and tn % LANES == 0
    per = rows_per_mod // tm
    resident = pl.Buffered(1)
    return pl.pallas_call(
        functools.partial(_inproj_kernel, tn=tn),
        grid=(t // tm,),
        in_specs=[
            pl.BlockSpec((tm, D_MODEL), lambda i: (i, 0)),
            pl.BlockSpec((1, D_MODEL), lambda i: (0, 0)),
            pl.BlockSpec((1, 1, D_MODEL), lambda i: (i // per, 0, 0)),
            pl.BlockSpec((1, 1, D_MODEL), lambda i: (i // per, 0, 0)),
            pl.BlockSpec((D_MODEL, IN_MAIN), lambda i: (0, 0), pipeline_mode=resident),
            pl.BlockSpec((D_MODEL, LANES), lambda i: (0, 0), pipeline_mode=resident),
            pl.BlockSpec((D_MODEL, S5_WIDTH), lambda i: (0, 0), pipeline_mode=resident),
        ],
        out_specs=[
            pl.BlockSpec((tm, IN_MAIN), lambda i: (i, 0)),
            pl.BlockSpec((tm, LANES), lambda i: (i, 0)),
            pl.BlockSpec((tm, S5_WIDTH), lambda i: (i, 0)),
        ],
        out_shape=[jax.ShapeDtypeStruct((t, IN_MAIN), BF16), jax.ShapeDtypeStruct((t, LANES), F32),
                   jax.ShapeDtypeStruct((t, S5_WIDTH), F32)],
        compiler_params=_params(("parallel",)),
        name="inproj",
    )(x, g, shift, scale, w_main, w_gates, w_u)


def _adaln_kernel(c_ref, w_ref, b_ref, o_ref):
    c = c_ref[...]
    s = c * jax.nn.sigmoid(c)
    o_ref[...] = _dot(s.astype(BF16), w_ref[...]) + b_ref[...]


def _adaln(cvecs, w, b):
    n = w.shape[1]
    tn = 1024
    return pl.pallas_call(
        _adaln_kernel,
        grid=(n // tn,),
        in_specs=[
            pl.BlockSpec(cvecs.shape, lambda j: (0, 0)),
            pl.BlockSpec((D_MODEL, tn), lambda j: (0, j)),
            pl.BlockSpec((1, tn), lambda j: (0, j)),
        ],
        out_specs=pl.BlockSpec((cvecs.shape[0], tn), lambda j: (0, j)),
        out_shape=jax.ShapeDtypeStruct((cvecs.shape[0], n), F32),
        compiler_params=_params(("parallel",)),
        name="adaln",
    )(cvecs, w, b)


HALO = 16


def _swap32(x):
    lane = lax.broadcasted_iota(jnp.int32, x.shape, 1)
    fwd = pltpu.roll(x, 96, 1)
    bwd = pltpu.roll(x, 32, 1)
    return jnp.where((lane % 64) < 32, fwd, bwd)


def _qkconv_kernel(x_ref, pv_ref, nx_ref, w_ref, b_ref, cos_ref, sin_ref, o_ref, *, tiles_per_seq, rope):
    i = pl.program_id(0)
    tm = x_ref.shape[0]
    x = x_ref[...].astype(F32)
    first = (i % tiles_per_seq) == 0
    last = (i % tiles_per_seq) == tiles_per_seq - 1
    prev_row = jnp.where(first, 0.0, pv_ref[HALO - 1:HALO, :].astype(F32))
    next_row = jnp.where(last, 0.0, nx_ref[0:1, :].astype(F32))
    row = lax.broadcasted_iota(jnp.int32, x.shape, 0)
    xp = jnp.where(row == 0, prev_row, pltpu.roll(x, 1, 0))
    xn = jnp.where(row == tm - 1, next_row, pltpu.roll(x, tm - 1, 0))
    y = xp * w_ref[0:1, :] + x * w_ref[1:2, :] + xn * w_ref[2:3, :] + b_ref[...]
    y = y * _sigmoid(y)
    kscale = M_HEAD_DIM ** -0.5
    for hh in range(2 * M_HEADS):
        ys = y[:, hh * LANES:(hh + 1) * LANES]
        if rope:
            ys = ys * cos_ref[...] + _swap32(ys) * sin_ref[...]
        if hh >= M_HEADS:
            ys = ys * kscale
        o_ref[:, hh * LANES:(hh + 1) * LANES] = ys.astype(o_ref.dtype)


def _qkconv(px, conv_w, conv_b, cos_t, sin_t, seq_len, rope):
    t = px.shape[0]
    tm = min(512, seq_len)
    assert seq_len % tm == 0
    tiles = seq_len // tm
    hb = tm // HALO
    nh = t // HALO
    return pl.pallas_call(
        functools.partial(_qkconv_kernel, tiles_per_seq=tiles, rope=rope),
        grid=(t // tm,),
        in_specs=[
            pl.BlockSpec((tm, 2 * M_WIDTH), lambda i: (i, 0)),
            pl.BlockSpec((HALO, 2 * M_WIDTH), lambda i: (jnp.maximum(i * hb - 1, 0), 0)),
            pl.BlockSpec((HALO, 2 * M_WIDTH), lambda i: (jnp.minimum((i + 1) * hb, nh - 1), 0)),
            pl.BlockSpec((3, 2 * M_WIDTH), lambda i: (0, 0)),
            pl.BlockSpec((1, 2 * M_WIDTH), lambda i: (0, 0)),
            pl.BlockSpec((tm, LANES), lambda i: (i % tiles, 0)),
            pl.BlockSpec((tm, LANES), lambda i: (i % tiles, 0)),
        ],
        out_specs=pl.BlockSpec((tm, 2 * M_WIDTH), lambda i: (i, 0)),
        out_shape=jax.ShapeDtypeStruct((t, 2 * M_WIDTH), BF16),
        compiler_params=_params(("parallel",)),
        name="qkconv",
    )(px, px, px, conv_w, conv_b, cos_t, sin_t)


def _rope_tables(n):
    pos = jnp.arange(n, dtype=jnp.int32)
    row = (pos // GRID_W).astype(F32)
    col = (pos % GRID_W).astype(F32)
    n_freq = M_HEAD_DIM // 4
    inv = ROPE_BASE ** (-jnp.arange(n_freq, dtype=F32) / n_freq)
    ar, ac = row[:, None] * inv, col[:, None] * inv
    cos_t = jnp.concatenate([jnp.cos(ar), jnp.cos(ar), jnp.cos(ac), jnp.cos(ac)], axis=-1)
    sin_t = jnp.concatenate([-jnp.sin(ar), jnp.sin(ar), -jnp.sin(ac), jnp.sin(ac)], axis=-1)
    return cos_t, sin_t


def _mlstm_kernel(*refs, reverse, final, nb):
    if final:
        (qk_ref, v_ref, g_ref, gb_ref, c0_ref, n0_ref, m0_ref, hp_ref, op_ref, nw_ref,
         out_ref, cf_ref, nf_ref, mf_ref, c_scr, n_scr, m_scr, qk_scr, qc_scr, qn_scr, s_scr, vt_scr, vu_scr, dec_scr) = refs
    else:
        (qk_ref, v_ref, g_ref, gb_ref, c0_ref, n0_ref, m0_ref,
         out_ref, cf_ref, nf_ref, mf_ref, c_scr, n_scr, m_scr, qk_scr, qc_scr, qn_scr, s_scr, vt_scr, vu_scr, dec_scr) = refs
    c_idx = pl.program_id(1)
    n_c = pl.num_programs(1)
    L = M_CHUNK

    @pl.when(c_idx == 0)
    def _():
        c_scr[...] = c0_ref[...]
        n_scr[...] = n0_ref[...]
        m_scr[...] = m0_ref[...]

    ti = lax.broadcasted_iota(jnp.int32, (L, L), 0)
    si = lax.broadcasted_iota(jnp.int32, (L, L), 1)
    keep = (si >= ti) if reverse else (si <= ti)
    tri = jnp.where(keep, 1.0, 0.0).astype(BF16)
    i_off = 2 * M_HEADS if reverse else 0
    f_off = i_off + M_HEADS
    end = 0 if reverse else L - 1

    units = [(a, b) for a in range(nb) for b in range(M_HEADS)]

    def head(ref, gi, h, base=0):
        return ref[gi, :, base + h * LANES:base + (h + 1) * LANES]

    keep_t = (ti >= si) if reverse else (ti <= si)

    for u, (gi, h) in enumerate(units):
        q = head(qk_ref, gi, h)
        qk_scr[u] = _dot_nt(head(qk_ref, gi, h, M_WIDTH), q)
        qc_scr[u] = _dot_nt(c_scr[gi, h].astype(BF16), q)
        qn_scr[u] = _dot_nt(jnp.broadcast_to(n_scr[gi, h], (8, LANES)).astype(BF16), q)

    for u, (gi, h) in enumerate(units):
        if h == 0:
            g = g_ref[gi] + gb_ref[...]
            f_hi, f_mid, f_lo = _split3(jax.nn.log_sigmoid(g))
            bsum = _dot(tri, f_hi) + _dot(tri, f_mid) + _dot(tri, f_lo)
            g_t = g.T
            b_t = bsum.T
        m_prev = m_scr[gi, h][:, 0:1]
        b_row = b_t[f_off + h:f_off + h + 1, :]
        i_row = g_t[i_off + h:i_off + h + 1, :]
        b_end = b_row[:, end:end + 1]
        d_col = g[:, i_off + h:i_off + h + 1] - bsum[:, f_off + h:f_off + h + 1]

        log_w = jnp.where(keep_t, b_row + d_col, -jnp.inf)
        carry_log = b_row + m_prev
        m_t = jnp.maximum(carry_log, jnp.max(log_w, axis=0, keepdims=True))
        s_t = qk_scr[u] * jnp.exp(log_w - m_t)
        c_scale = jnp.exp(carry_log - m_t)
        den = jnp.sum(s_t, axis=0, keepdims=True) + c_scale * qn_scr[u][0:1, :]
        inv = 1.0 / jnp.maximum(jnp.abs(den), jnp.exp(-m_t))
        s_scr[u] = (s_t * inv).astype(BF16)
        qc_scr[u] = qc_scr[u] * (c_scale * inv)

        m_new = jnp.maximum(b_end + m_prev, jnp.max(b_end - b_row + i_row, axis=-1, keepdims=True))
        u_row = jnp.exp(b_end - b_row + i_row - m_new)
        decay = jnp.exp(b_end + m_prev - m_new)
        v_t = head(v_ref, gi, h).astype(F32).T
        vt_scr[u] = v_t.astype(BF16)
        vu_scr[u] = (v_t * u_row).astype(BF16)
        u8 = jnp.broadcast_to(u_row, (8, L)).astype(BF16)
        n_scr[gi, h] = decay * n_scr[gi, h] + _dot(u8, head(qk_ref, gi, h, M_WIDTH))[0:1, :]
        m_scr[gi, h] = jnp.broadcast_to(m_new, (1, LANES))
        dec_scr[u] = jnp.broadcast_to(decay, (1, LANES))

    for u, (gi, h) in enumerate(units):
        qc_scr[u] = _dot(vt_scr[u], s_scr[u]) + qc_scr[u]
        c_scr[gi, h] = dec_scr[u][:, 0:1] * c_scr[gi, h] + _dot(vu_scr[u], head(qk_ref, gi, h, M_WIDTH))

    for u, (gi, h) in enumerate(units):
        rows = slice(h * M_HEAD_DIM, (h + 1) * M_HEAD_DIM)
        if final:
            hs = qc_scr[u] + hp_ref[gi, rows, :]
            mu = jnp.mean(hs, axis=0, keepdims=True)
            var = jnp.mean(jnp.square(hs - mu), axis=0, keepdims=True)
            hn = ((hs - mu) * lax.rsqrt(var + EPS)).T
            y = hn * nw_ref[:, h * LANES:(h + 1) * LANES] * jax.nn.sigmoid(head(op_ref, gi, h).astype(F32))
            out_ref[gi, :, h * LANES:(h + 1) * LANES] = y.astype(out_ref.dtype)
        else:
            out_ref[gi, rows, :] = qc_scr[u]

    @pl.when(c_idx == n_c - 1)
    def _():
        cf_ref[...] = c_scr[...]
        nf_ref[...] = n_scr[...]
        mf_ref[...] = m_scr[...]


def _mlstm_scan(qk, px, gates, gate_b, state, seq_len, reverse, hprev=None, norm_w=None):
    t = qk.shape[0]
    bsz = t // seq_len
    n_c = seq_len // M_CHUNK
    final = hprev is not None
    nb = next(n for n in (8, 4, 2, 1) if bsz % n == 0)
    qk, px, gates = (a.reshape(bsz, seq_len, a.shape[1]) for a in (qk, px, gates))

    def chunk(c):
        return (n_c - 1 - c) if reverse else c

    state_specs = [
        pl.BlockSpec((nb, M_HEADS, M_HEAD_DIM, M_HEAD_DIM), lambda b, c: (b, 0, 0, 0)),
        pl.BlockSpec((nb, M_HEADS, 1, LANES), lambda b, c: (b, 0, 0, 0)),
        pl.BlockSpec((nb, M_HEADS, 1, LANES), lambda b, c: (b, 0, 0, 0)),
    ]
    in_specs = [
        pl.BlockSpec((nb, M_CHUNK, 2 * M_WIDTH), lambda b, c: (b, chunk(c), 0)),
        pl.BlockSpec((nb, M_CHUNK, M_WIDTH), lambda b, c: (b, chunk(c), COL_V)),
        pl.BlockSpec((nb, M_CHUNK, LANES), lambda b, c: (b, chunk(c), 0)),
        pl.BlockSpec((1, LANES), lambda b, c: (0, 0)),
    ] + state_specs
    args = [qk, px, gates, gate_b, *state]
    t_spec = pl.BlockSpec((nb, M_WIDTH, M_CHUNK), lambda b, c: (b, 0, chunk(c)))
    if final:
        in_specs += [
            t_spec,
            pl.BlockSpec((nb, M_CHUNK, M_WIDTH), lambda b, c: (b, chunk(c), COL_O)),
            pl.BlockSpec((1, M_WIDTH), lambda b, c: (0, 0)),
        ]
        args += [hprev, px, norm_w]
        out_spec = pl.BlockSpec((nb, M_CHUNK, M_WIDTH), lambda b, c: (b, chunk(c), 0))
        out_shape = jax.ShapeDtypeStruct((bsz, seq_len, M_WIDTH), BF16)
    else:
        out_spec = t_spec
        out_shape = jax.ShapeDtypeStruct((bsz, M_WIDTH, seq_len), F32)
    units = nb * M_HEADS
    out, *fin = pl.pallas_call(
        functools.partial(_mlstm_kernel, reverse=reverse, final=final, nb=nb),
        grid=(bsz // nb, n_c),
        in_specs=in_specs,
        out_specs=[out_spec] + state_specs,
        out_shape=[
            out_shape,
            jax.ShapeDtypeStruct((bsz, M_HEADS, M_HEAD_DIM, M_HEAD_DIM), F32),
            jax.ShapeDtypeStruct((bsz, M_HEADS, 1, LANES), F32),
            jax.ShapeDtypeStruct((bsz, M_HEADS, 1, LANES), F32),
        ],
        scratch_shapes=[
            pltpu.VMEM((nb, M_HEADS, M_HEAD_DIM, M_HEAD_DIM), F32),
            pltpu.VMEM((nb, M_HEADS, 1, LANES), F32),
            pltpu.VMEM((nb, M_HEADS, 1, LANES), F32),
            pltpu.VMEM((units, M_CHUNK, M_CHUNK), F32),
            pltpu.VMEM((units, M_HEAD_DIM, M_CHUNK), F32),
            pltpu.VMEM((units, 8, M_CHUNK), F32),
            pltpu.VMEM((units, M_CHUNK, M_CHUNK), BF16),
            pltpu.VMEM((units, M_HEAD_DIM, M_CHUNK), BF16),
            pltpu.VMEM((units, M_HEAD_DIM, M_CHUNK), BF16),
            pltpu.VMEM((units, 1, LANES), F32),
        ],
        compiler_params=_params(("parallel", "arbitrary")),
        name="mlstm_rev" if reverse else "mlstm_fwd",
    )(*args)
    return [out.reshape(t, M_WIDTH) if final else out] + fin


def _mlstm_zero_state(bsz):
    return (jnp.zeros((bsz, M_HEADS, M_HEAD_DIM, M_HEAD_DIM), F32),
            jnp.zeros((bsz, M_HEADS, 1, LANES), F32),
            jnp.zeros((bsz, M_HEADS, 1, LANES), F32))


def _mlstm_bidir(qk, px, gates, gate_b, norm_w, state_f, state_b, seq_len):
    h_f, *fin_f = _mlstm_scan(qk, px, gates, gate_b, state_f, seq_len, False)
    y, *fin_b = _mlstm_scan(qk, px, gates, gate_b, state_b, seq_len, True, hprev=h_f, norm_w=norm_w)
    return y, tuple(fin_f), tuple(fin_b)


NA_ROWS = 8
NA_STEP = 8


def _na_kernel(q_ref, kp_ref, kc_ref, kn_ref, vp_ref, vc_ref, vn_ref, kx_ref, vx_ref, *rest, rows):
    bias_refs = rest[:NA_STEP]
    o_ref, kwin, vwin, s_scr, p_scr, l_scr = rest[NA_STEP:]
    step = pl.program_id(1)
    blk = NA_ROWS * GRID_W

    @pl.when(step % (NA_ROWS // NA_STEP) == 0)
    def _():
        kwin[0:blk, :] = kp_ref[...]
        kwin[blk:2 * blk, :] = kc_ref[...]
        kwin[2 * blk:3 * blk, :] = kn_ref[...]
        vwin[0:blk, :] = vp_ref[...]
        vwin[blk:2 * blk, :] = vc_ref[...]
        vwin[2 * blk:3 * blk, :] = vn_ref[...]

    g = step // (NA_ROWS // NA_STEP)
    n_loc = WIN_R * GRID_W
    lane = lax.broadcasted_iota(jnp.int32, (GRID_W, LANES), 1)
    scale = NA_HEAD_DIM ** -0.5
    n_ctx = kx_ref.shape[0]
    units = [(kk, hp) for kk in range(NA_STEP) for hp in range(NA_HEADS // 2)]
    offs = []
    for kk in range(NA_STEP):
        r = step * NA_STEP + kk
        r0 = jnp.clip(r - WIN_R // 2, 0, rows - WIN_R)
        offs.append(pl.multiple_of((r0 - NA_ROWS * g + NA_ROWS) * GRID_W, GRID_W))

    for u, (kk, hp) in enumerate(units):
        cols = slice(hp * LANES, (hp + 1) * LANES)
        q2 = (q_ref[kk * GRID_W:(kk + 1) * GRID_W, cols].astype(F32) * (scale * LOG2E)).astype(BF16)
        zero = jnp.zeros_like(q2)
        qs = jnp.concatenate([jnp.where(lane < NA_HEAD_DIM, q2, zero), jnp.where(lane < NA_HEAD_DIM, zero, q2)],
                             axis=0)
        s_scr[u, :, 0:n_loc] = _dot_nt(qs, kwin[pl.ds(offs[kk], n_loc), cols])
        s_scr[u, :, n_loc:n_loc + n_ctx] = _dot_nt(qs, kx_ref[:, cols])

    for u, (kk, hp) in enumerate(units):
        s_loc = s_scr[u, :, 0:n_loc] + bias_refs[kk][0, hp]
        s_ctx = s_scr[u, :, n_loc:n_loc + n_ctx]
        m = jnp.maximum(jnp.max(s_loc, axis=-1, keepdims=True), jnp.max(s_ctx, axis=-1, keepdims=True))
        p_loc = jnp.exp2(s_loc - m)
        p_ctx = jnp.exp2(s_ctx - m)
        inv = 1.0 / (jnp.sum(p_loc, axis=-1, keepdims=True) + jnp.sum(p_ctx, axis=-1, keepdims=True))
        p_scr[u, :, 0:n_loc] = p_loc.astype(BF16)
        p_scr[u, :, n_loc:n_loc + n_ctx] = p_ctx.astype(BF16)
        l_scr[u] = jnp.broadcast_to(inv, (2 * GRID_W, LANES))

    for u, (kk, hp) in enumerate(units):
        cols = slice(hp * LANES, (hp + 1) * LANES)
        o2 = _dot(p_scr[u, :, 0:n_loc], vwin[pl.ds(offs[kk], n_loc), cols])
        o2 = (o2 + _dot(p_scr[u, :, n_loc:n_loc + n_ctx], vx_ref[:, cols])) * l_scr[u]
        out = jnp.where(lane < NA_HEAD_DIM, o2[0:GRID_W], o2[GRID_W:2 * GRID_W])
        o_ref[kk * GRID_W:(kk + 1) * GRID_W, cols] = out.astype(o_ref.dtype)


def _na_bias_table(rpb):
    qc = jnp.arange(GRID_W)[:, None]
    kc = jnp.arange(GRID_W)[None, :]
    cs = jnp.clip(qc - WIN_C // 2, 0, GRID_W - WIN_C)
    ok = (kc >= cs) & (kc < cs + WIN_C)
    dc = jnp.clip(kc - qc + WIN_C - 1, 0, 2 * WIN_C - 2)
    pick = (dc[:, :, None] == jnp.arange(2 * WIN_C - 1)).astype(F32)
    cols = jnp.einsum('hrd,qkd->hrqk', rpb.astype(F32) * LOG2E, pick, precision=lax.Precision.HIGHEST)
    cols = jnp.where(ok[None, None], cols, NEG_BIG)
    tab = jnp.stack([cols[:, d0:d0 + WIN_R] for d0 in range(WIN_R)])
    tab = jnp.transpose(tab, (0, 1, 3, 2, 4))
    return tab.reshape(WIN_R, NA_HEADS // 2, 2 * GRID_W, WIN_R * GRID_W)


def _na_attention(px, pc, bias, seq_len, ctx_len):
    t = px.shape[0]
    bsz = t // seq_len
    rows = seq_len // GRID_W
    n_g = rows // NA_ROWS
    blk = NA_ROWS * GRID_W

    steps = rows // NA_STEP
    per_blk = NA_ROWS // NA_STEP

    def kv_spec(col, shift):
        def imap(b, s):
            g = jnp.clip(s // per_blk + shift, 0, n_g - 1)
            return (b * n_g + g, col)
        return pl.BlockSpec((blk, NA_WIDTH), imap)

    def bias_spec(kk):
        def imap(b, s):
            r = s * NA_STEP + kk
            r0 = jnp.clip(r - WIN_R // 2, 0, rows - WIN_R)
            return (r0 - r + WIN_R - 1, 0, 0, 0)
        return pl.BlockSpec((1, NA_HEADS // 2, 2 * GRID_W, WIN_R * GRID_W), imap)

    return pl.pallas_call(
        functools.partial(_na_kernel, rows=rows),
        grid=(bsz, steps),
        in_specs=[
            pl.BlockSpec((NA_STEP * GRID_W, NA_WIDTH), lambda b, s: (b * steps + s, COL_NAQ)),
            kv_spec(COL_NAK, -1), kv_spec(COL_NAK, 0), kv_spec(COL_NAK, 1),
            kv_spec(COL_NAV, -1), kv_spec(COL_NAV, 0), kv_spec(COL_NAV, 1),
            pl.BlockSpec((ctx_len, NA_WIDTH), lambda b, s: (b, COL_NAK)),
            pl.BlockSpec((ctx_len, NA_WIDTH), lambda b, s: (b, COL_NAV)),
        ] + [bias_spec(kk) for kk in range(NA_STEP)],
        out_specs=pl.BlockSpec((NA_STEP * GRID_W, NA_WIDTH), lambda b, s: (b * steps + s, 0)),
        out_shape=jax.ShapeDtypeStruct((t, NA_WIDTH), BF16),
        scratch_shapes=[
            pltpu.VMEM((3 * blk, NA_WIDTH), BF16), pltpu.VMEM((3 * blk, NA_WIDTH), BF16),
            pltpu.VMEM((NA_STEP * NA_HEADS // 2, 2 * GRID_W, WIN_R * GRID_W + ctx_len), F32),
            pltpu.VMEM((NA_STEP * NA_HEADS // 2, 2 * GRID_W, WIN_R * GRID_W + ctx_len), BF16),
            pltpu.VMEM((NA_STEP * NA_HEADS // 2, 2 * GRID_W, LANES), F32),
        ],
        compiler_params=_params(("parallel", "arbitrary")),
        name="na_attn",
    )(px, px, px, px, px, px, px, pc, pc, *([bias] * NA_STEP))


def _ctx_attn_kernel(q_ref, k_ref, v_ref, o_ref):
    n = q_ref.shape[0]
    lane = lax.broadcasted_iota(jnp.int32, (n, LANES), 1)
    scale = NA_HEAD_DIM ** -0.5
    for hp in range(NA_HEADS // 2):
        cols = slice(hp * LANES, (hp + 1) * LANES)
        q2 = q_ref[:, cols] * scale
        k2 = k_ref[:, cols]
        v2 = v_ref[:, cols]
        outs = []
        for sub in range(2):
            sel = (lane < NA_HEAD_DIM) if sub == 0 else (lane >= NA_HEAD_DIM)
            s = _dot_nt(jnp.where(sel, q2, jnp.zeros_like(q2)), k2)
            p = jnp.exp(s - jnp.max(s, axis=-1, keepdims=True))
            outs.append(_dot(p.astype(BF16), v2) / jnp.sum(p, axis=-1, keepdims=True))
        o_ref[:, cols] = jnp.where(lane < NA_HEAD_DIM, outs[0], outs[1]).astype(o_ref.dtype)


def _ctx_attention(pc, ctx_len):
    t = pc.shape[0]
    return pl.pallas_call(
        _ctx_attn_kernel,
        grid=(t // ctx_len,),
        in_specs=[pl.BlockSpec((ctx_len, NA_WIDTH), lambda b, col=col: (b, col))
                  for col in (COL_NAQ, COL_NAK, COL_NAV)],
        out_specs=pl.BlockSpec((ctx_len, NA_WIDTH), lambda b: (b, 0)),
        out_shape=jax.ShapeDtypeStruct((t, NA_WIDTH), BF16),
        compiler_params=_params(("parallel",)),
        name="ctx_attn",
    )(pc, pc, pc)


def _cmul(a, b):
    return a[0] * b[0] - a[1] * b[1], a[0] * b[1] + a[1] * b[0]


def _s5_tables(lam_re, lam_im, log_dt, b_re, b_im, c_re, c_im):
    L, G, S, H = S5_CHUNK, S5_GROUPS, S5_STATE, S5_GROUP
    lr = jnp.minimum(lam_re.astype(F32), S5_MAX_RE)
    li = lam_im.astype(F32)
    dt = jnp.exp(log_dt.astype(F32))[..., None]
    zr, zi = lr * dt, li * dt
    d = jnp.arange(L + 1, dtype=F32)[None, :, None, None]
    mag = jnp.exp(zr[:, None] * d)
    pw = (mag * jnp.cos(zi[:, None] * d), mag * jnp.sin(zi[:, None] * d))
    lb = (pw[0][:, 1], pw[1][:, 1])
    den = lr * lr + li * li
    ratio = (((lb[0] - 1.0) * lr + lb[1] * li) / den, (lb[1] * lr - (lb[0] - 1.0) * li) / den)
    bb = _cmul((ratio[0][..., None], ratio[1][..., None]), (b_re.astype(F32), b_im.astype(F32)))
    cc = (c_re.astype(F32), c_im.astype(F32))

    def kern(x):
        cl = _cmul((cc[0][x][None], cc[1][x][None]), (pw[0][x, :L, :, None, :], pw[1][x, :L, :, None, :]))
        return jnp.einsum('dghp,gpj->dghj', cl[0], bb[0][x]) - jnp.einsum('dghp,gpj->dghj', cl[1], bb[1][x])

    def state_in(x, steps):
        p = (pw[0][x][steps][:, :, None, :], pw[1][x][steps][:, :, None, :])
        r = _cmul(p, (jnp.swapaxes(bb[0][x], 1, 2)[None], jnp.swapaxes(bb[1][x], 1, 2)[None]))
        return jnp.transpose(r[0], (1, 0, 2, 3)), jnp.transpose(r[1], (1, 0, 2, 3))

    def state_out(x, steps):
        p = (pw[0][x][steps][:, :, None, :], pw[1][x][steps][:, :, None, :])
        r = _cmul(p, (cc[0][x][None], cc[1][x][None]))
        return jnp.transpose(r[0], (1, 3, 0, 2)), jnp.transpose(r[1], (1, 3, 0, 2))

    steps = jnp.arange(L)
    wf = state_in(0, L - 1 - steps)
    wb = state_in(1, steps)
    vf = state_out(0, steps + 1)
    vb = state_out(1, L - steps)
    Q, A = S5_QUADS, S5_QUAD_GROUPS
    kd = jnp.stack([kern(0), kern(1)]).reshape(2, L, Q, A, H, H)
    kd = jnp.transpose(kd, (0, 2, 1, 5, 3, 4)).reshape(2, Q, L, H, A * H)
    wk = jnp.stack([wf[0], wf[1], wb[0], wb[1]]).reshape(4, Q, A // 2, 2, L, H, S)
    wd = jnp.transpose(wk, (0, 1, 4, 2, 5, 3, 6)).reshape(4, Q, L, (A // 2) * H, 2 * S)
    vk = jnp.stack([vf[0], -vf[1], vb[0], -vb[1]]).reshape(4, Q, A, S, L, H)
    vd = jnp.transpose(vk, (0, 1, 4, 3, 2, 5)).reshape(4, Q, L, S, A * H)
    a_re = pw[0][:, L].reshape(2, 1, G * S)
    a_im = pw[1][:, L].reshape(2, 1, G * S)
    return wd, kd, vd, a_re, a_im


def _s5_operand(u_ref):
    return jnp.concatenate([u_ref[:, s, :] for s in range(S5_CHUNK)], axis=1).astype(BF16)


def _s5_expand_w(wd_ref, w_scr):
    H, half = S5_GROUP, S5_STATE
    w_scr[...] = jnp.zeros_like(w_scr)
    lane = lax.broadcasted_iota(jnp.int32, (H, LANES), 1)
    for k in range(4):
        for s in range(S5_CHUNK):
            for c in range(S5_QUAD_GROUPS // 2):
                blk = wd_ref[k, 0, s, c * H:(c + 1) * H, :]
                cols = slice(k * S5_QSTATE + c * LANES, k * S5_QSTATE + (c + 1) * LANES)
                row = s * LANES + 2 * c * H
                w_scr[row:row + H, cols] = jnp.where(lane < half, blk, 0.0).astype(BF16)
                w_scr[row + H:row + 2 * H, cols] = jnp.where(lane >= half, blk, 0.0).astype(BF16)


def _s5_expand_mv(kd_ref, vd_ref, m_scr, v_scr):
    H, L = S5_GROUP, S5_CHUNK
    rg = lax.broadcasted_iota(jnp.int32, (LANES, LANES), 0) // H
    lg = lax.broadcasted_iota(jnp.int32, (LANES, LANES), 1) // H
    tiles = []
    for x in range(2):
        tiles.append([jnp.where(rg == lg, jnp.concatenate([kd_ref[x, 0, d]] * S5_QUAD_GROUPS, axis=0), 0.0)
                      for d in range(L)])
    for s in range(L):
        for t in range(L):
            blk = tiles[0][t - s] if t > s else (tiles[1][s - t] if s > t else tiles[0][0] + tiles[1][0])
            m_scr[s * LANES:(s + 1) * LANES, t * LANES:(t + 1) * LANES] = blk.astype(BF16)
    lg2 = lax.broadcasted_iota(jnp.int32, (S5_STATE, LANES), 1) // H
    for k in range(4):
        for t in range(L):
            blk = vd_ref[k, 0, t]
            for b in range(S5_QUAD_GROUPS):
                row = k * S5_QSTATE + b * S5_STATE
                v_scr[row:row + S5_STATE, t * LANES:(t + 1) * LANES] = jnp.where(lg2 == b, blk, 0.0).astype(BF16)


def _s5_local_kernel(u_ref, wd_ref, sfr_ref, sfi_ref, sbr_ref, sbi_ref, w_scr):
    @pl.when(pl.program_id(1) == 0)
    def _():
        _s5_expand_w(wd_ref, w_scr)

    s = _dot(_s5_operand(u_ref), w_scr[...])
    n = S5_QSTATE
    sfr_ref[...] = s[:, 0:n]
    sfi_ref[...] = s[:, n:2 * n]
    sbr_ref[...] = s[:, 2 * n:3 * n]
    sbi_ref[...] = s[:, 3 * n:4 * n]


def _s5_local_states(u3, wd):
    rows = u3.shape[0]
    tr = min(256, rows)
    assert rows % tr == 0
    out = jax.ShapeDtypeStruct((rows, S5_GROUPS * S5_STATE), F32)
    return pl.pallas_call(
        _s5_local_kernel,
        grid=(S5_QUADS, rows // tr),
        in_specs=[
            pl.BlockSpec((tr, S5_CHUNK, LANES), lambda q, i: (i, 0, q)),
            pl.BlockSpec((4, 1) + wd.shape[2:], lambda q, i: (0, q, 0, 0, 0)),
        ],
        out_specs=[pl.BlockSpec((tr, S5_QSTATE), lambda q, i: (i, q))] * 4,
        out_shape=[out] * 4,
        scratch_shapes=[pltpu.VMEM((S5_QCOLS, 4 * S5_QSTATE), BF16)],
        compiler_params=_params(("arbitrary", "arbitrary")),
        name="s5_local",
    )(u3, wd)


def _s5_scan_kernel(sr_ref, si_ref, ar_ref, ai_ref, h0r_ref, h0i_ref, hr_ref, hi_ref, fr_ref, fi_ref,
                    hr_scr, hi_scr, *, reverse):
    i = pl.program_id(1)
    n_i = pl.num_programs(1)
    steps = sr_ref.shape[1]

    @pl.when(i == 0)
    def _():
        hr_scr[...] = h0r_ref[...]
        hi_scr[...] = h0i_ref[...]

    ar = ar_ref[...]
    ai = ai_ref[...]

    def body(j, carry):
        hr, hi = carry
        c = (steps - 1 - j) if reverse else j
        hr_ref[:, c, :] = hr
        hi_ref[:, c, :] = hi
        nr = ar * hr - ai * hi + sr_ref[:, c, :]
        ni = ar * hi + ai * hr + si_ref[:, c, :]
        return nr, ni

    hr, hi = lax.fori_loop(0, steps, body, (hr_scr[...], hi_scr[...]))
    hr_scr[...] = hr
    hi_scr[...] = hi

    @pl.when(i == n_i - 1)
    def _():
        fr_ref[...] = hr
        fi_ref[...] = hi


def _s5_chunk_scan(s_re, s_im, a_re, a_im, h0_re, h0_im, bsz, reverse):
    _, n_c, width = s_re.shape
    tr = min(64, n_c)
    tc = 512
    n_i = n_c // tr

    def rmap(j, i):
        return (0, (n_i - 1 - i) if reverse else i, j)

    big = jax.ShapeDtypeStruct((bsz, n_c, width), F32)
    small = jax.ShapeDtypeStruct((bsz, width), F32)
    return pl.pallas_call(
        functools.partial(_s5_scan_kernel, reverse=reverse),
        grid=(width // tc, n_i),
        in_specs=[
            pl.BlockSpec((bsz, tr, tc), rmap), pl.BlockSpec((bsz, tr, tc), rmap),
            pl.BlockSpec((1, tc), lambda j, i: (0, j)), pl.BlockSpec((1, tc), lambda j, i: (0, j)),
            pl.BlockSpec((bsz, tc), lambda j, i: (0, j)), pl.BlockSpec((bsz, tc), lambda j, i: (0, j)),
        ],
        out_specs=[
            pl.BlockSpec((bsz, tr, tc), rmap), pl.BlockSpec((bsz, tr, tc), rmap),
            pl.BlockSpec((bsz, tc), lambda j, i: (0, j)), pl.BlockSpec((bsz, tc), lambda j, i: (0, j)),
        ],
        out_shape=[big, big, small, small],
        scratch_shapes=[pltpu.VMEM((bsz, tc), F32), pltpu.VMEM((bsz, tc), F32)],
        compiler_params=_params(("parallel", "arbitrary")),
        name="s5_scan_rev" if reverse else "s5_scan_fwd",
    )(s_re, s_im, a_re, a_im, h0_re, h0_im)


def _s5_out_kernel(u_ref, kd_ref, vd_ref, hfr_ref, hfi_ref, hbr_ref, hbi_ref, y_ref, m_scr, v_scr):
    @pl.when(pl.program_id(1) == 0)
    def _():
        _s5_expand_mv(kd_ref, vd_ref, m_scr, v_scr)

    n = S5_QSTATE
    y = _dot(_s5_operand(u_ref), m_scr[...])
    for k, h_ref in enumerate((hfr_ref, hfi_ref, hbr_ref, hbi_ref)):
        y += _dot(h_ref[...].astype(BF16), v_scr[k * n:(k + 1) * n, :])
    for t in range(S5_CHUNK):
        y_ref[:, t, :] = y[:, t * LANES:(t + 1) * LANES]


def _s5_outputs(u3, kd, vd, h_states):
    rows = u3.shape[0]
    tr = min(256, rows)
    return pl.pallas_call(
        _s5_out_kernel,
        grid=(S5_QUADS, rows // tr),
        in_specs=[
            pl.BlockSpec((tr, S5_CHUNK, LANES), lambda q, i: (i, 0, q)),
            pl.BlockSpec((2, 1) + kd.shape[2:], lambda q, i: (0, q, 0, 0, 0)),
            pl.BlockSpec((4, 1) + vd.shape[2:], lambda q, i: (0, q, 0, 0, 0)),
        ] + [pl.BlockSpec((tr, S5_QSTATE), lambda q, i: (i, q))] * 4,
        out_specs=pl.BlockSpec((tr, S5_CHUNK, LANES), lambda q, i: (i, 0, q)),
        out_shape=jax.ShapeDtypeStruct((rows, S5_CHUNK, S5_WIDTH), F32),
        scratch_shapes=[pltpu.VMEM((S5_QCOLS, S5_QCOLS), BF16), pltpu.VMEM((4 * S5_QSTATE, S5_QCOLS), BF16)],
        compiler_params=_params(("arbitrary", "arbitrary")),
        name="s5_out",
    )(u3, kd, vd, *h_states)


def _s5_branch(u, tables, seq_len, init, want_y):
    wd, kd, vd, a_re, a_im = tables
    t = u.shape[0]
    bsz = t // seq_len
    n_c = seq_len // S5_CHUNK
    width = S5_GROUPS * S5_STATE
    u3 = u.reshape(t // S5_CHUNK, S5_CHUNK, S5_WIDTH)
    s_loc = [s.reshape(bsz, n_c, width) for s in _s5_local_states(u3, wd)]
    hfr, hfi, ffr, ffi = _s5_chunk_scan(s_loc[0], s_loc[1], a_re[0], a_im[0], init[0], init[1], bsz, False)
    hbr, hbi, fbr, fbi = _s5_chunk_scan(s_loc[2], s_loc[3], a_re[1], a_im[1], init[2], init[3], bsz, True)
    y = None
    if want_y:
        h_states = [h.reshape(bsz * n_c, width) for h in (hfr, hfi, hbr, hbi)]
        y = _s5_outputs(u3, kd, vd, h_states).reshape(t, S5_WIDTH)
    return y, (ffr, ffi, fbr, fbi)


def _sigmoid(x):
    return 0.5 * jnp.tanh(0.5 * x) + 0.5


def _gelu_tanh(x):
    return 0.5 * x * (1.0 + jnp.tanh(math.sqrt(2.0 / math.pi) * (x + 0.044715 * (x * x * x))))


def _merge_kernel(x_ref, ym_ref, yn_ref, ys_ref, u_ref, gm_ref, gn_ref, gs_ref, d_ref, gw_ref, gb_ref,
                  wm_ref, wn_ref, ws_ref, wo_ref, gate_ref, npost_ref, o_ref):
    ys = ys_ref[...] + d_ref[...] * u_ref[...]
    g = _gelu_tanh(ys)
    gg = g * _sigmoid(_dot(g.astype(BF16), gw_ref[...]) + gb_ref[...])
    y = _sigmoid(gm_ref[...].astype(F32)) * _dot(ym_ref[...], wm_ref[...])
    y += _sigmoid(gn_ref[...].astype(F32)) * _dot(yn_ref[...], wn_ref[...])
    y += _sigmoid(gs_ref[...].astype(F32)) * _dot(gg.astype(BF16), ws_ref[...])
    out = _dot(y.astype(BF16), wo_ref[...])
    o_ref[...] = x_ref[...] + gate_ref[0] * _rms(out, npost_ref[...])


def _merge(x, ym, yn, ys, u, px, wts, gate, npost, rows_per_mod):
    t = x.shape[0]
    tm = min(512, rows_per_mod)
    per = rows_per_mod // tm
    s5_d, glu_w, glu_b, w_m, w_n, w_s, w_o = wts
    row = lambda i: (i, 0)
    const = lambda i: (0, 0)
    return pl.pallas_call(
        _merge_kernel,
        grid=(t // tm,),
        in_specs=[
            pl.BlockSpec((tm, D_MODEL), row),
            pl.BlockSpec((tm, M_WIDTH), row),
            pl.BlockSpec((tm, NA_WIDTH), row),
            pl.BlockSpec((tm, S5_WIDTH), row),
            pl.BlockSpec((tm, S5_WIDTH), row),
            pl.BlockSpec((tm, D_MODEL), lambda i: (i, COL_GATE)),
            pl.BlockSpec((tm, D_MODEL), lambda i: (i, COL_GATE + 1)),
            pl.BlockSpec((tm, D_MODEL), lambda i: (i, COL_GATE + 2)),
            pl.BlockSpec((1, S5_WIDTH), const),
            pl.BlockSpec((S5_WIDTH, S5_WIDTH), const),
            pl.BlockSpec((1, S5_WIDTH), const),
            pl.BlockSpec((M_WIDTH, D_MODEL), const),
            pl.BlockSpec((NA_WIDTH, D_MODEL), const),
            pl.BlockSpec((S5_WIDTH, D_MODEL), const),
            pl.BlockSpec((D_MODEL, D_MODEL), const),
            pl.BlockSpec((1, 1, D_MODEL), lambda i: (i // per, 0, 0)),
            pl.BlockSpec((1, D_MODEL), const),
        ],
        out_specs=pl.BlockSpec((tm, D_MODEL), row),
        out_shape=jax.ShapeDtypeStruct((t, D_MODEL), F32),
        compiler_params=_params(("parallel",)),
        name="merge",
    )(x, ym, yn, ys, u, px, px, px, s5_d, glu_w, glu_b, w_m, w_n, w_s, w_o, gate, npost)


def _swiglu(h, wg, wu, wd):
    a = _dot(h, wg)
    a = a * jax.nn.sigmoid(a) * _dot(h, wu)
    return _dot(a.astype(BF16), wd)


def _ffn_kernel(x_ref, g_ref, sh_ref, sc_ref, wg_ref, wu_ref, wd_ref, gate_ref, npost_ref, o_ref, *, ff_tile):
    h = (_rms(x_ref[...], g_ref[...]) * (1.0 + sc_ref[0]) + sh_ref[0]).astype(BF16)
    acc = jnp.zeros(x_ref.shape, F32)
    for j in range(wg_ref.shape[1] // ff_tile):
        cols = slice(j * ff_tile, (j + 1) * ff_tile)
        acc += _swiglu(h, wg_ref[:, cols], wu_ref[:, cols], wd_ref[cols, :])
    o_ref[...] = x_ref[...] + gate_ref[0] * _rms(acc, npost_ref[...])


def _ffn(x, g, shift, scale, w_gate, w_up, w_down, gate, npost, rows_per_mod):
    t = x.shape[0]
    ff = w_gate.shape[1]
    ff_tile = ff // 2
    tm = min(512, rows_per_mod)
    per = rows_per_mod // tm
    assert ff_tile % LANES == 0
    resident = pl.Buffered(1)
    return pl.pallas_call(
        functools.partial(_ffn_kernel, ff_tile=ff_tile),
        grid=(t // tm,),
        in_specs=[
            pl.BlockSpec((tm, D_MODEL), lambda i: (i, 0)),
            pl.BlockSpec((1, D_MODEL), lambda i: (0, 0)),
            pl.BlockSpec((1, 1, D_MODEL), lambda i: (i // per, 0, 0)),
            pl.BlockSpec((1, 1, D_MODEL), lambda i: (i // per, 0, 0)),
            pl.BlockSpec((D_MODEL, ff), lambda i: (0, 0), pipeline_mode=resident),
            pl.BlockSpec((D_MODEL, ff), lambda i: (0, 0), pipeline_mode=resident),
            pl.BlockSpec((ff, D_MODEL), lambda i: (0, 0), pipeline_mode=resident),
            pl.BlockSpec((1, 1, D_MODEL), lambda i: (i // per, 0, 0)),
            pl.BlockSpec((1, D_MODEL), lambda i: (0, 0)),
        ],
        out_specs=pl.BlockSpec((tm, D_MODEL), lambda i: (i, 0)),
        out_shape=jax.ShapeDtypeStruct((t, D_MODEL), F32),
        compiler_params=_params(("parallel",)),
        name="ffn",
    )(x, g, shift, scale, w_gate, w_up, w_down, gate, npost)


MOE_ROWS = MOE_TILE * TOP_K + N_EXPERTS * MOE_GRAIN
MOE_META = 2 * N_EXPERTS


def _moe_dispatch_kernel(x_ref, g_ref, sh_ref, sc_ref, rh_ref, rl_ref, xs_ref, route_ref, meta_ref):
    tm = x_ref.shape[0]
    h = _rms(x_ref[...], g_ref[...]) * (1.0 + sc_ref[0]) + sh_ref[0]
    hi = h.astype(BF16)
    lo = (h - hi.astype(F32)).astype(BF16)
    logits = _dot(hi, rh_ref[...]) + _dot(lo, rh_ref[...]) + _dot(hi, rl_ref[...])
    lane = lax.broadcasted_iota(jnp.int32, logits.shape, 1)
    logits = jnp.where(lane < N_EXPERTS, logits, -jnp.inf)
    m1 = jnp.max(logits, axis=-1, keepdims=True)
    i1 = jnp.min(jnp.where(logits == m1, lane, LANES), axis=-1, keepdims=True)
    rest = jnp.where(lane == i1, -jnp.inf, logits)
    m2 = jnp.max(rest, axis=-1, keepdims=True)
    i2 = jnp.min(jnp.where(rest == m2, lane, LANES), axis=-1, keepdims=True)
    e2 = jnp.exp(m2 - m1)
    sel1 = jnp.where(lane == i1, 1.0, 0.0)
    sel2 = jnp.where(lane == i2, 1.0, 0.0)
    sel = sel1 + sel2
    sel1_t, sel2_t = sel1.T, sel2.T
    sel_t = sel1_t + sel2_t

    ti = lax.broadcasted_iota(jnp.int32, (tm, tm), 0)
    si = lax.broadcasted_iota(jnp.int32, (tm, tm), 1)
    rank_c = _dot(jnp.where(si < ti, 1.0, 0.0).astype(BF16), sel.astype(BF16))
    rank_r = _dot(sel_t.astype(BF16), jnp.where(ti < si, 1.0, 0.0).astype(BF16))
    blk_row = jnp.ceil(jnp.sum(sel, axis=0, keepdims=True) * (1.0 / MOE_GRAIN))
    blk_col = jnp.ceil(jnp.sum(sel_t, axis=1, keepdims=True) * (1.0 / MOE_GRAIN))
    ea = lax.broadcasted_iota(jnp.int32, (LANES, LANES), 0)
    eb = lax.broadcasted_iota(jnp.int32, (LANES, LANES), 1)
    off_row = _dot(jnp.broadcast_to(blk_row, (8, LANES)).astype(BF16),
                   jnp.where(ea < eb, 1.0, 0.0).astype(BF16))[0:1, :]
    off_col = _dot(jnp.where(eb < ea, 1.0, 0.0).astype(BF16),
                   jnp.broadcast_to(blk_col, (LANES, LANES)).astype(BF16))[:, 0:1]

    pos_c = off_row * MOE_GRAIN + rank_c
    pos1_c = jnp.sum(jnp.where(lane == i1, pos_c, 0.0), axis=-1, keepdims=True)
    pos2_c = jnp.sum(jnp.where(lane == i2, pos_c, 0.0), axis=-1, keepdims=True)
    route_ref[...] = (jnp.where(lane == 0, pos1_c, 0.0) + jnp.where(lane == 1, pos2_c, 0.0)
                      + jnp.where(lane == 2, 1.0 / (1.0 + e2), 0.0) + jnp.where(lane == 3, e2 / (1.0 + e2), 0.0))
    pos_r = off_col * MOE_GRAIN + rank_r
    pos1_r = jnp.sum(sel1_t * pos_r, axis=0, keepdims=True)
    pos2_r = jnp.sum(sel2_t * pos_r, axis=0, keepdims=True)
    half = MOE_ROWS // 2
    for c in range(2):
        slot = (lax.broadcasted_iota(jnp.int32, (half, tm), 0) + c * half).astype(F32)
        onehot = jnp.where(slot == pos1_r, 1.0, jnp.where(slot == pos2_r, 1.0, 0.0)).astype(BF16)
        xs_ref[c * half:(c + 1) * half, :] = _dot(onehot, hi).astype(BF16)

    lane8 = lax.broadcasted_iota(jnp.int32, (8, LANES), 1)
    blk8 = jnp.broadcast_to(blk_row, (8, LANES))
    off8 = pltpu.roll(jnp.broadcast_to(off_row, (8, LANES)), N_EXPERTS, 1)
    meta = jnp.where(lane8 < N_EXPERTS, blk8, jnp.where(lane8 < MOE_META, off8, 0.0))
    meta_ref[0] = meta.astype(jnp.int32)


def _moe_dispatch(x, g, shift, scale, r_hi, r_lo, rows_per_mod):
    t = x.shape[0]
    tm = MOE_TILE
    assert rows_per_mod % tm == 0
    per = rows_per_mod // tm
    n_t = t // tm
    return pl.pallas_call(
        _moe_dispatch_kernel,
        grid=(n_t,),
        in_specs=[
            pl.BlockSpec((tm, D_MODEL), lambda i: (i, 0)),
            pl.BlockSpec((1, D_MODEL), lambda i: (0, 0)),
            pl.BlockSpec((1, 1, D_MODEL), lambda i: (i // per, 0, 0)),
            pl.BlockSpec((1, 1, D_MODEL), lambda i: (i // per, 0, 0)),
            pl.BlockSpec((D_MODEL, LANES), lambda i: (0, 0)),
            pl.BlockSpec((D_MODEL, LANES), lambda i: (0, 0)),
        ],
        out_specs=[
            pl.BlockSpec((MOE_ROWS, D_MODEL), lambda i: (i, 0)),
            pl.BlockSpec((tm, LANES), lambda i: (i, 0)),
            pl.BlockSpec((1, 8, LANES), lambda i: (i, 0, 0)),
        ],
        out_shape=[
            jax.ShapeDtypeStruct((n_t * MOE_ROWS, D_MODEL), BF16),
            jax.ShapeDtypeStruct((t, LANES), F32),
            jax.ShapeDtypeStruct((n_t, 8, LANES), jnp.int32),
        ],
        compiler_params=_params(("parallel",)),
        name="moe_dispatch",
    )(x, g, shift, scale, r_hi, r_lo)


MOE_FF_TILE = 512


def _moe_combine_kernel(ys_ref, route_ref, x_ref, gate_ref, npost_ref, o_ref):
    tm = x_ref.shape[0]
    lane = lax.broadcasted_iota(jnp.int32, (tm, LANES), 1)
    route = route_ref[...]
    pos1, pos2, w1, w2 = (jnp.sum(jnp.where(lane == k, route, 0.0), axis=-1, keepdims=True) for k in range(4))
    width = 512
    acc = jnp.zeros((tm, D_MODEL), F32)
    for c in range(MOE_ROWS // width):
        slot = (lax.broadcasted_iota(jnp.int32, (tm, width), 1) + c * width).astype(F32)
        wmat = jnp.where(slot == pos1, w1, 0.0) + jnp.where(slot == pos2, w2, 0.0)
        acc += _dot(wmat.astype(BF16), ys_ref[c * width:(c + 1) * width, :])
    o_ref[...] = x_ref[...] + gate_ref[0] * _rms(acc, npost_ref[...])


MOE_BLOCK = 512


def _moe_plan(meta, n_t):
    per_tile = MOE_ROWS // MOE_GRAIN
    per_blk = MOE_BLOCK // MOE_GRAIN
    nblk = meta[:, 0, :N_EXPERTS]
    first = meta[:, 0, N_EXPERTS:MOE_META]
    tot = jnp.sum(nblk, axis=0)
    tot_pad = ((tot + per_blk - 1) // per_blk) * per_blk
    ends = jnp.cumsum(tot_pad)
    seg_dst = (ends - tot_pad)[None, :] + jnp.cumsum(nblk, axis=0) - nblk
    k = jnp.arange(per_tile, dtype=jnp.int32)[None, :, None]
    inside = (k >= first[:, None, :]) & (k < (first + nblk)[:, None, :])
    dst = jnp.sum(jnp.where(inside, seg_dst[:, None, :] + k - first[:, None, :], 0), axis=-1)
    used = jnp.any(inside, axis=-1).reshape(-1)
    dst = dst.reshape(-1).astype(jnp.int32)
    n_big = (n_t * per_tile + N_EXPERTS * per_blk) // per_blk
    big = jnp.arange(n_big, dtype=jnp.int32)[:, None] * per_blk
    expert = jnp.minimum(jnp.sum(big >= ends[None, :], axis=-1), N_EXPERTS - 1).astype(jnp.int32)
    valid = (big[:, 0] < ends[-1]).astype(jnp.int32)
    src = jnp.zeros((n_big * per_blk,), jnp.int32).at[jnp.where(used, dst, n_big * per_blk)].set(
        jnp.arange(dst.shape[0], dtype=jnp.int32), mode='drop')
    return jnp.where(used, dst, 0), jnp.concatenate([expert, valid, src]), n_big


def _moe_global_kernel(plan_ref, *refs, ff_tile):
    per_blk = MOE_BLOCK // MOE_GRAIN
    x_refs = refs[:per_blk]
    wg_ref, wu_ref, wd_ref, out_ref, xin_scr = refs[per_blk:]
    b = pl.program_id(0)
    valid = plan_ref[pl.num_programs(0) + b]
    ff = wg_ref.shape[2]

    @pl.when(valid == 1)
    def _():
        for k in range(per_blk):
            xin_scr[k * MOE_GRAIN:(k + 1) * MOE_GRAIN, :] = x_refs[k][...]
        xin = xin_scr[...]
        y = jnp.zeros((MOE_BLOCK, D_MODEL), F32)
        for j in range(ff // ff_tile):
            cols = slice(j * ff_tile, (j + 1) * ff_tile)
            y += _swiglu(xin, wg_ref[0, :, cols], wu_ref[0, :, cols], wd_ref[0, cols, :])
        out_ref[...] = y.astype(out_ref.dtype)

    @pl.when(valid == 0)
    def _():
        out_ref[...] = jnp.zeros_like(out_ref)


def _moe_experts_global(xs, plan, n_big, w_gate, w_up, w_down):
    n_e, _, ff = w_gate.shape
    per_blk = MOE_BLOCK // MOE_GRAIN
    resident = pl.Buffered(1)

    def src_spec(k):
        return pl.BlockSpec((MOE_GRAIN, D_MODEL), lambda b, p: (p[2 * n_big + b * per_blk + k], 0))

    grid_spec = pltpu.PrefetchScalarGridSpec(
        num_scalar_prefetch=1,
        grid=(n_big,),
        in_specs=[src_spec(k) for k in range(per_blk)] + [
            pl.BlockSpec((1, D_MODEL, ff), lambda b, p: (p[b], 0, 0), pipeline_mode=resident),
            pl.BlockSpec((1, D_MODEL, ff), lambda b, p: (p[b], 0, 0), pipeline_mode=resident),
            pl.BlockSpec((1, ff, D_MODEL), lambda b, p: (p[b], 0, 0), pipeline_mode=resident),
        ],
        out_specs=pl.BlockSpec((MOE_BLOCK, D_MODEL), lambda b, p: (b, 0)),
        scratch_shapes=[pltpu.VMEM((MOE_BLOCK, D_MODEL), BF16)],
    )
    return pl.pallas_call(
        functools.partial(_moe_global_kernel, ff_tile=MOE_FF_TILE),
        grid_spec=grid_spec,
        out_shape=jax.ShapeDtypeStruct((n_big * MOE_BLOCK, D_MODEL), BF16),
        compiler_params=_params(("arbitrary",)),
        name="moe_experts",
    )(plan, *([xs] * per_blk), w_gate, w_up, w_down)


def _moe_global(x, g, shift, scale, r_hi, r_lo, w_gate, w_up, w_down, gate, npost, rows_per_mod):
    xs, route, meta = _moe_dispatch(x, g, shift, scale, r_hi, r_lo, rows_per_mod)
    n_t = xs.shape[0] // MOE_ROWS
    block_map, plan, n_big = _moe_plan(meta, n_t)
    yg = _moe_experts_global(xs, plan, n_big, w_gate, w_up, w_down)
    return _moe_gather_combine(yg, block_map, route, x, gate, npost, rows_per_mod)


def _moe_gather_combine_kernel(map_ref, *refs):
    per_tile = MOE_ROWS // MOE_GRAIN
    y_refs = refs[:per_tile]
    route_ref, x_ref, gate_ref, npost_ref, o_ref, ys_scr = refs[per_tile:]
    del map_ref
    for k in range(per_tile):
        ys_scr[k * MOE_GRAIN:(k + 1) * MOE_GRAIN, :] = y_refs[k][...]
    _moe_combine_kernel(ys_scr, route_ref, x_ref, gate_ref, npost_ref, o_ref)


def _moe_gather_combine(yg, block_map, route, x, gate, npost, rows_per_mod):
    t = x.shape[0]
    tm = MOE_TILE
    per = rows_per_mod // tm
    per_tile = MOE_ROWS // MOE_GRAIN

    def y_spec(k):
        return pl.BlockSpec((MOE_GRAIN, D_MODEL), lambda i, m: (m[i * per_tile + k], 0))

    grid_spec = pltpu.PrefetchScalarGridSpec(
        num_scalar_prefetch=1,
        grid=(t // tm,),
        in_specs=[y_spec(k) for k in range(per_tile)] + [
            pl.BlockSpec((tm, LANES), lambda i, m: (i, 0)),
            pl.BlockSpec((tm, D_MODEL), lambda i, m: (i, 0)),
            pl.BlockSpec((1, 1, D_MODEL), lambda i, m: (i // per, 0, 0)),
            pl.BlockSpec((1, D_MODEL), lambda i, m: (0, 0)),
        ],
        out_specs=pl.BlockSpec((tm, D_MODEL), lambda i, m: (i, 0)),
        scratch_shapes=[pltpu.VMEM((MOE_ROWS, D_MODEL), BF16)],
    )
    return pl.pallas_call(
        _moe_gather_combine_kernel,
        grid_spec=grid_spec,
        out_shape=jax.ShapeDtypeStruct((t, D_MODEL), F32),
        compiler_params=_params(("arbitrary",)),
        name="moe_combine",
    )(block_map, *([yg] * per_tile), route, x, gate, npost)


def _reorder_w_in(w):
    o = 4 * M_WIDTH
    gates = jnp.pad(w[:, o:o + M_GATES], ((0, 0), (0, LANES - M_GATES)))
    na = w[:, o + M_GATES:o + M_GATES + 3 * NA_WIDTH]
    u0 = o + M_GATES + 3 * NA_WIDTH
    main = jnp.concatenate([w[:, :o], w[:, u0 + S5_WIDTH:], na], axis=1)
    return main.astype(BF16), gates.astype(BF16), w[:, u0:u0 + S5_WIDTH].astype(BF16)


def _row(v):
    return v.reshape(1, -1).astype(F32)


def _token_mixers(x, xc, mx, mc, lp, rope, seq_len, ctx_len, ctx_out):
    bsz = x.shape[0] // seq_len
    w_main, w_gates, w_u = _reorder_w_in(lp['w_in'])
    g_pre = _row(lp['norm_mix_pre'])
    px, gx, ux = _inproj(x, g_pre, mx[0], mx[1], w_main, w_gates, w_u, seq_len)
    pc, gc, uc = _inproj(xc, g_pre, mc[0], mc[1], w_main, w_gates, w_u, xc.shape[0])

    conv_w = lp['m_conv_w'].astype(F32)
    conv_b = _row(lp['m_conv_b'])
    gate_b = jnp.pad(_row(lp['m_gate_b']), ((0, 0), (0, LANES - M_GATES)))
    m_norm = _row(lp['m_norm'])
    qk_c = _qkconv(pc, conv_w, conv_b, rope[0][:ctx_len], rope[1][:ctx_len], ctx_len, False)
    qk_x = _qkconv(px, conv_w, conv_b, rope[0], rope[1], seq_len, True)
    zero = _mlstm_zero_state(bsz)
    ym_c, fin_f, fin_b = _mlstm_bidir(qk_c, pc, gc, gate_b, m_norm, zero, zero, ctx_len)
    ym_x, _, _ = _mlstm_bidir(qk_x, px, gx, gate_b, m_norm, fin_f, fin_b, seq_len)

    yn_x = _na_attention(px, pc, _na_bias_table(lp['na_rpb']), seq_len, ctx_len)

    tables = _s5_tables(lp['s5_lam_re'], lp['s5_lam_im'], lp['s5_log_dt'], lp['s5_b_re'], lp['s5_b_im'],
                        lp['s5_c_re'], lp['s5_c_im'])
    zs = jnp.zeros((bsz, S5_GROUPS * S5_STATE), F32)
    ys_c, fin_s = _s5_branch(uc, tables, ctx_len, (zs, zs, zs, zs), ctx_out)
    ys_x, _ = _s5_branch(ux, tables, seq_len, fin_s, True)

    wts = (_row(lp['s5_d']), lp['s5_glu_w'].astype(BF16), _row(lp['s5_glu_b']),
           lp['w_branch_m'].astype(BF16), lp['w_branch_na'].astype(BF16), lp['w_branch_s5'].astype(BF16),
           lp['w_out'].astype(BF16))
    g_post = _row(lp['norm_mix_post'])
    x = _merge(x, ym_x, yn_x, ys_x, ux, px, wts, mx[2], g_post, seq_len)
    if ctx_out:
        yn_c = _ctx_attention(pc, ctx_len)
        xc = _merge(xc, ym_c, yn_c, ys_c, uc, pc, wts, mc[2], g_post, xc.shape[0])
    return x, xc


def kernel(x, c, ctx, c_ctx, ada_w, ada_b, norm_mix_pre, norm_mix_post, norm_ffn_pre, norm_ffn_post, w_in, m_gate_b, m_conv_w, m_conv_b, m_norm, na_rpb, s5_lam_re, s5_lam_im, s5_log_dt, s5_b_re, s5_b_im, s5_c_re, s5_c_im, s5_d, s5_glu_w, s5_glu_b, w_branch_m, w_branch_na, w_branch_s5, w_out, ffn_w_gate, ffn_w_up, ffn_w_down, moe_router, moe_w_gate, moe_w_up, moe_w_down):
    bsz, seq_len, d = x.shape
    ctx_len = ctx.shape[1]
    depth = w_in.shape[0]
    assert d == D_MODEL and seq_len % (GRID_W * NA_ROWS) == 0 and ctx_len % M_CHUNK == 0
    rope = _rope_tables(seq_len)
    xf = x.reshape(bsz * seq_len, d)
    xc = ctx.reshape(bsz * ctx_len, d)
    cvecs = jnp.zeros((16, d), F32).at[:bsz].set(c).at[bsz].set(c_ctx)
    for l in range(depth):
        last = l == depth - 1
        mod = _adaln(cvecs, ada_w[l].astype(BF16), _row(ada_b[l]))
        mx = [mod[:bsz, k * d:(k + 1) * d].reshape(bsz, 1, d) for k in range(N_MOD)]
        mc = [mod[bsz:bsz + 1, k * d:(k + 1) * d].reshape(1, 1, d) for k in range(N_MOD)]
        lp = {
            'norm_mix_pre': norm_mix_pre[l], 'norm_mix_post': norm_mix_post[l],
            'w_in': w_in[l], 'm_gate_b': m_gate_b[l], 'm_conv_w': m_conv_w[l], 'm_conv_b': m_conv_b[l],
            'm_norm': m_norm[l], 'na_rpb': na_rpb[l], 's5_lam_re': s5_lam_re[l], 's5_lam_im': s5_lam_im[l],
            's5_log_dt': s5_log_dt[l], 's5_b_re': s5_b_re[l], 's5_b_im': s5_b_im[l], 's5_c_re': s5_c_re[l],
            's5_c_im': s5_c_im[l], 's5_d': s5_d[l], 's5_glu_w': s5_glu_w[l], 's5_glu_b': s5_glu_b[l],
            'w_branch_m': w_branch_m[l], 'w_branch_na': w_branch_na[l], 'w_branch_s5': w_branch_s5[l],
            'w_out': w_out[l],
        }
        xf, xc = _token_mixers(xf, xc, mx, mc, lp, rope, seq_len, ctx_len, not last)
        j = l // 2
        g_pre, g_post = _row(norm_ffn_pre[l]), _row(norm_ffn_post[l])
        if l % 2 == 0:
            wg, wu, wd = ffn_w_gate[j].astype(BF16), ffn_w_up[j].astype(BF16), ffn_w_down[j].astype(BF16)

            def channel(h, m, rows):
                return _ffn(h, g_pre, m[3], m[4], wg, wu, wd, m[5], g_post, rows)
        else:
            wg, wu, wd = moe_w_gate[j].astype(BF16), moe_w_up[j].astype(BF16), moe_w_down[j].astype(BF16)
            router = jnp.pad(moe_router[j].astype(F32), ((0, 0), (0, LANES - N_EXPERTS)))
            r_hi = router.astype(BF16)
            r_lo = (router - r_hi.astype(F32)).astype(BF16)

            def channel(h, m, rows):
                return _moe_global(h, g_pre, m[3], m[4], r_hi, r_lo, wg, wu, wd, m[5], g_post, rows)
        xf = channel(xf, mx, seq_len)
        if not last:
            xc = channel(xc, mc, xc.shape[0])
    return xf.reshape(bsz, seq_len, d)
```

```python
import functools
import math

import jax
import jax.numpy as jnp
from jax import lax
from jax.experimental import pallas as pl
from jax.experimental.pallas import tpu as pltpu

F32 = jnp.float32
BF16 = jnp.bfloat16

D_MODEL = 1024
EPS = 1e-6
N_MOD = 6
GRID_W = 64

M_HEADS = 4
M_HEAD_DIM = 128
M_WIDTH = M_HEADS * M_HEAD_DIM
M_GATES = 4 * M_HEADS
M_CHUNK = 128
ROPE_BASE = 10000.0

NA_HEADS = 8
NA_HEAD_DIM = 64
NA_WIDTH = NA_HEADS * NA_HEAD_DIM
WIN_R = 8
WIN_C = 16
NEG_BIG = -1e30
LOG2E = math.log2(math.e)

S5_GROUP = 16
S5_WIDTH = 512
S5_GROUPS = S5_WIDTH // S5_GROUP
S5_STATE = 64
S5_MAX_RE = -1e-4

LANES = 128
VMEM_LIMIT = 52 * 1024 * 1024

S5_CHUNK = 16
S5_QUAD_GROUPS = LANES // S5_GROUP
S5_QUADS = S5_GROUPS // S5_QUAD_GROUPS
S5_QCOLS = S5_CHUNK * LANES
S5_QSTATE = S5_QUAD_GROUPS * S5_STATE

N_EXPERTS = 8
TOP_K = 2
MOE_TILE = 1024
MOE_GRAIN = 64

COL_QK, COL_V, COL_O, COL_NAQ, COL_NAK, COL_NAV = 0, 2, 3, 10, 11, 12
COL_GATE = 2
IN_MAIN = 6656


def _params(sem):
    return pltpu.CompilerParams(dimension_semantics=sem, vmem_limit_bytes=VMEM_LIMIT)


def _dot(a, b):
    return jnp.dot(a, b, preferred_element_type=F32)


def _dot_nt(a, b):
    return lax.dot_general(a, b, (((1,), (1,)), ((), ())), preferred_element_type=F32)


def _split3(x):
    hi = x.astype(BF16)
    r1 = x - hi.astype(F32)
    mid = r1.astype(BF16)
    lo = (r1 - mid.astype(F32)).astype(BF16)
    return hi, mid, lo


def _rms(x, g):
    return x * lax.rsqrt(jnp.mean(x * x, axis=-1, keepdims=True) + EPS) * g


X_HALO = 8


def _inproj_kernel(x_ref, xp_ref, xn_ref, g_ref, sh_ref, sc_ref, w_ref, wg_ref, wu_ref, cw_ref, cb_ref,
                   cos_ref, sin_ref, o_ref, og_ref, ou_ref, *, tn, seq_len, rope):
    tm = x_ref.shape[0]
    qk_w = 2 * M_WIDTH

    def prep(xv):
        return (_rms(xv, g_ref[...]) * (1.0 + sc_ref[0]) + sh_ref[0]).astype(BF16)

    hb = prep(x_ref[...])
    og_ref[...] = _dot(hb, wg_ref[...])
    ou_ref[...] = _dot(hb, wu_ref[...])
    for c in range((IN_MAIN - qk_w) // tn):
        cols = slice(qk_w + c * tn, qk_w + (c + 1) * tn)
        o_ref[:, cols] = _dot(hb, w_ref[:, cols]).astype(o_ref.dtype)

    x = _dot(hb, w_ref[:, 0:qk_w])
    halo = _dot(prep(jnp.concatenate([xp_ref[...], xn_ref[...]], axis=0)), w_ref[:, 0:qk_w])
    row = lax.broadcasted_iota(jnp.int32, x.shape, 0)
    pos = (row + pl.program_id(0) * tm) % seq_len
    xp = jnp.where(row == 0, halo[X_HALO - 1:X_HALO, :], pltpu.roll(x, 1, 0))
    xp = jnp.where(pos == 0, 0.0, xp)
    xn = jnp.where(row == tm - 1, halo[X_HALO:X_HALO + 1, :], pltpu.roll(x, tm - 1, 0))
    xn = jnp.where(pos == seq_len - 1, 0.0, xn)
    y = xp * cw_ref[0:1, :] + x * cw_ref[1:2, :] + xn * cw_ref[2:3, :] + cb_ref[...]
    y = y * _sigmoid(y)
    kscale = M_HEAD_DIM ** -0.5
    for hh in range(2 * M_HEADS):
        ys = y[:, hh * LANES:(hh + 1) * LANES]
        if rope:
            ys = ys * cos_ref[...] + _swap32(ys) * sin_ref[...]
        if hh >= M_HEADS:
            ys = ys * kscale
        o_ref[:, hh * LANES:(hh + 1) * LANES] = ys.astype(o_ref.dtype)


def _inproj(x, g, shift, scale, w_main, w_gates, w_u, conv_w, conv_b, cos_t, sin_t, rows_per_mod, seq_len, rope):
    t = x.shape[0]
    tm = min(512, rows_per_mod)
    tn = (IN_MAIN - 2 * M_WIDTH) // 4
    assert rows_per_mod % tm == 0 and t % tm == 0 and tn % LANES == 0
    assert seq_len % tm == 0 or tm % seq_len == 0
    per = rows_per_mod // tm
    hb = tm // X_HALO
    nh = t // X_HALO
    tiles = max(seq_len // tm, 1)
    tab_rows = min(tm, seq_len)
    if tab_rows < tm:
        cos_t, sin_t = (jnp.tile(a[:seq_len], (tm // seq_len, 1)) for a in (cos_t, sin_t))
    resident = pl.Buffered(1)
    return pl.pallas_call(
        functools.partial(_inproj_kernel, tn=tn, seq_len=seq_len, rope=rope),
        grid=(t // tm,),
        in_specs=[
            pl.BlockSpec((tm, D_MODEL), lambda i: (i, 0)),
            pl.BlockSpec((X_HALO, D_MODEL), lambda i: (jnp.maximum(i * hb - 1, 0), 0)),
            pl.BlockSpec((X_HALO, D_MODEL), lambda i: (jnp.minimum((i + 1) * hb, nh - 1), 0)),
            pl.BlockSpec((1, D_MODEL), lambda i: (0, 0)),
            pl.BlockSpec((1, 1, D_MODEL), lambda i: (i // per, 0, 0)),
            pl.BlockSpec((1, 1, D_MODEL), lambda i: (i // per, 0, 0)),
            pl.BlockSpec((D_MODEL, IN_MAIN), lambda i: (0, 0), pipeline_mode=resident),
            pl.BlockSpec((D_MODEL, LANES), lambda i: (0, 0), pipeline_mode=resident),
            pl.BlockSpec((D_MODEL, S5_WIDTH), lambda i: (0, 0), pipeline_mode=resident),
            pl.BlockSpec((3, 2 * M_WIDTH), lambda i: (0, 0)),
            pl.BlockSpec((1, 2 * M_WIDTH), lambda i: (0, 0)),
            pl.BlockSpec((tm, LANES), lambda i: (i % tiles, 0)),
            pl.BlockSpec((tm, LANES), lambda i: (i % tiles, 0)),
        ],
        out_specs=[
            pl.BlockSpec((tm, IN_MAIN), lambda i: (i, 0)),
            pl.BlockSpec((tm, LANES), lambda i: (i, 0)),
            pl.BlockSpec((tm, S5_WIDTH), lambda i: (i, 0)),
        ],
        out_shape=[jax.ShapeDtypeStruct((t, IN_MAIN), BF16), jax.ShapeDtypeStruct((t, LANES), F32),
                   jax.ShapeDtypeStruct((t, S5_WIDTH), F32)],
        compiler_params=_params(("parallel",)),
        name="inproj",
    )(x, x, x, g, shift, scale, w_main, w_gates, w_u, conv_w, conv_b, cos_t, sin_t)


def _adaln_kernel(c_ref, w_ref, b_ref, o_ref):
    c = c_ref[...]
    s = c * jax.nn.sigmoid(c)
    o_ref[...] = _dot(s.astype(BF16), w_ref[...]) + b_ref[...]


def _adaln(cvecs, w, b):
    n = w.shape[1]
    tn = 1024
    return pl.pallas_call(
        _adaln_kernel,
        grid=(n // tn,),
        in_specs=[
            pl.BlockSpec(cvecs.shape, lambda j: (0, 0)),
            pl.BlockSpec((D_MODEL, tn), lambda j: (0, j)),
            pl.BlockSpec((1, tn), lambda j: (0, j)),
        ],
        out_specs=pl.BlockSpec((cvecs.shape[0], tn), lambda j: (0, j)),
        out_shape=jax.ShapeDtypeStruct((cvecs.shape[0], n), F32),
        compiler_params=_params(("parallel",)),
        name="adaln",
    )(cvecs, w, b)


def _swap32(x):
    lane = lax.broadcasted_iota(jnp.int32, x.shape, 1)
    fwd = pltpu.roll(x, 96, 1)
    bwd = pltpu.roll(x, 32, 1)
    return jnp.where((lane % 64) < 32, fwd, bwd)


def _rope_tables(n):
    pos = jnp.arange(n, dtype=jnp.int32)
    row = (pos // GRID_W).astype(F32)
    col = (pos % GRID_W).astype(F32)
    n_freq = M_HEAD_DIM // 4
    inv = ROPE_BASE ** (-jnp.arange(n_freq, dtype=F32) / n_freq)
    ar, ac = row[:, None] * inv, col[:, None] * inv
    cos_t = jnp.concatenate([jnp.cos(ar), jnp.cos(ar), jnp.cos(ac), jnp.cos(ac)], axis=-1)
    sin_t = jnp.concatenate([-jnp.sin(ar), jnp.sin(ar), -jnp.sin(ac), jnp.sin(ac)], axis=-1)
    return cos_t, sin_t


def _mlstm_kernel(*refs, reverse, final, nb):
    if final:
        (qk_ref, v_ref, g_ref, gb_ref, c0_ref, n0_ref, m0_ref, hp_ref, op_ref, nw_ref,
         out_ref, cf_ref, nf_ref, mf_ref, c_scr, n_scr, m_scr, qk_scr, qc_scr, qn_scr, s_scr, vt_scr, vu_scr, dec_scr) = refs
    else:
        (qk_ref, v_ref, g_ref, gb_ref, c0_ref, n0_ref, m0_ref,
         out_ref, cf_ref, nf_ref, mf_ref, c_scr, n_scr, m_scr, qk_scr, qc_scr, qn_scr, s_scr, vt_scr, vu_scr, dec_scr) = refs
    c_idx = pl.program_id(1)
    n_c = pl.num_programs(1)
    L = M_CHUNK

    @pl.when(c_idx == 0)
    def _():
        c_scr[...] = c0_ref[...]
        n_scr[...] = n0_ref[...]
        m_scr[...] = m0_ref[...]

    ti = lax.broadcasted_iota(jnp.int32, (L, L), 0)
    si = lax.broadcasted_iota(jnp.int32, (L, L), 1)
    keep = (si >= ti) if reverse else (si <= ti)
    tri = jnp.where(keep, 1.0, 0.0).astype(BF16)
    i_off = 2 * M_HEADS if reverse else 0
    f_off = i_off + M_HEADS
    end = 0 if reverse else L - 1

    units = [(a, b) for a in range(nb) for b in range(M_HEADS)]

    def head(ref, gi, h, base=0):
        return ref[gi, :, base + h * LANES:base + (h + 1) * LANES]

    keep_t = (ti >= si) if reverse else (ti <= si)

    for u, (gi, h) in enumerate(units):
        q = head(qk_ref, gi, h)
        qk_scr[u] = _dot_nt(head(qk_ref, gi, h, M_WIDTH), q)
        qc_scr[u] = _dot_nt(c_scr[gi, h].astype(BF16), q)
        qn_scr[u] = _dot_nt(jnp.broadcast_to(n_scr[gi, h], (8, LANES)).astype(BF16), q)

    for u, (gi, h) in enumerate(units):
        if h == 0:
            g = g_ref[gi] + gb_ref[...]
            f_hi, f_mid, f_lo = _split3(jax.nn.log_sigmoid(g))
            bsum = _dot(tri, f_hi) + _dot(tri, f_mid) + _dot(tri, f_lo)
            g_t = g.T
            b_t = bsum.T
        m_prev = m_scr[gi, h][:, 0:1]
        b_row = b_t[f_off + h:f_off + h + 1, :]
        i_row = g_t[i_off + h:i_off + h + 1, :]
        b_end = b_row[:, end:end + 1]
        d_col = g[:, i_off + h:i_off + h + 1] - bsum[:, f_off + h:f_off + h + 1]

        log_w = jnp.where(keep_t, b_row + d_col, -jnp.inf)
        carry_log = b_row + m_prev
        m_t = jnp.maximum(carry_log, jnp.max(log_w, axis=0, keepdims=True))
        s_t = qk_scr[u] * jnp.exp(log_w - m_t)
        c_scale = jnp.exp(carry_log - m_t)
        den = jnp.sum(s_t, axis=0, keepdims=True) + c_scale * qn_scr[u][0:1, :]
        inv = 1.0 / jnp.maximum(jnp.abs(den), jnp.exp(-m_t))
        s_scr[u] = (s_t * inv).astype(BF16)
        qc_scr[u] = qc_scr[u] * (c_scale * inv)

        m_new = jnp.maximum(b_end + m_prev, jnp.max(b_end - b_row + i_row, axis=-1, keepdims=True))
        u_row = jnp.exp(b_end - b_row + i_row - m_new)
        decay = jnp.exp(b_end + m_prev - m_new)
        v_t = head(v_ref, gi, h).astype(F32).T
        vt_scr[u] = v_t.astype(BF16)
        vu_scr[u] = (v_t * u_row).astype(BF16)
        u8 = jnp.broadcast_to(u_row, (8, L)).astype(BF16)
        n_scr[gi, h] = decay * n_scr[gi, h] + _dot(u8, head(qk_ref, gi, h, M_WIDTH))[0:1, :]
        m_scr[gi, h] = jnp.broadcast_to(m_new, (1, LANES))
        dec_scr[u] = jnp.broadcast_to(decay, (1, LANES))

    for u, (gi, h) in enumerate(units):
        qc_scr[u] = _dot(vt_scr[u], s_scr[u]) + qc_scr[u]
        c_scr[gi, h] = dec_scr[u][:, 0:1] * c_scr[gi, h] + _dot(vu_scr[u], head(qk_ref, gi, h, M_WIDTH))

    for u, (gi, h) in enumerate(units):
        rows = slice(h * M_HEAD_DIM, (h + 1) * M_HEAD_DIM)
        if final:
            hs = qc_scr[u] + hp_ref[gi, rows, :]
            mu = jnp.mean(hs, axis=0, keepdims=True)
            var = jnp.mean(jnp.square(hs - mu), axis=0, keepdims=True)
            hn = ((hs - mu) * lax.rsqrt(var + EPS)).T
            y = hn * nw_ref[:, h * LANES:(h + 1) * LANES] * jax.nn.sigmoid(head(op_ref, gi, h).astype(F32))
            out_ref[gi, :, h * LANES:(h + 1) * LANES] = y.astype(out_ref.dtype)
        else:
            out_ref[gi, rows, :] = qc_scr[u]

    @pl.when(c_idx == n_c - 1)
    def _():
        cf_ref[...] = c_scr[...]
        nf_ref[...] = n_scr[...]
        mf_ref[...] = m_scr[...]


def _mlstm_scan(qk, px, gates, gate_b, state, seq_len, reverse, hprev=None, norm_w=None):
    t = qk.shape[0]
    bsz = t // seq_len
    n_c = seq_len // M_CHUNK
    final = hprev is not None
    nb = next(n for n in (8, 4, 2, 1) if bsz % n == 0)
    qk, px, gates = (a.reshape(bsz, seq_len, a.shape[1]) for a in (qk, px, gates))

    def chunk(c):
        return (n_c - 1 - c) if reverse else c

    state_specs = [
        pl.BlockSpec((nb, M_HEADS, M_HEAD_DIM, M_HEAD_DIM), lambda b, c: (b, 0, 0, 0)),
        pl.BlockSpec((nb, M_HEADS, 1, LANES), lambda b, c: (b, 0, 0, 0)),
        pl.BlockSpec((nb, M_HEADS, 1, LANES), lambda b, c: (b, 0, 0, 0)),
    ]
    in_specs = [
        pl.BlockSpec((nb, M_CHUNK, 2 * M_WIDTH), lambda b, c: (b, chunk(c), 0)),
        pl.BlockSpec((nb, M_CHUNK, M_WIDTH), lambda b, c: (b, chunk(c), COL_V)),
        pl.BlockSpec((nb, M_CHUNK, LANES), lambda b, c: (b, chunk(c), 0)),
        pl.BlockSpec((1, LANES), lambda b, c: (0, 0)),
    ] + state_specs
    args = [qk, px, gates, gate_b, *state]
    t_spec = pl.BlockSpec((nb, M_WIDTH, M_CHUNK), lambda b, c: (b, 0, chunk(c)))
    if final:
        in_specs += [
            t_spec,
            pl.BlockSpec((nb, M_CHUNK, M_WIDTH), lambda b, c: (b, chunk(c), COL_O)),
            pl.BlockSpec((1, M_WIDTH), lambda b, c: (0, 0)),
        ]
        args += [hprev, px, norm_w]
        out_spec = pl.BlockSpec((nb, M_CHUNK, M_WIDTH), lambda b, c: (b, chunk(c), 0))
        out_shape = jax.ShapeDtypeStruct((bsz, seq_len, M_WIDTH), BF16)
    else:
        out_spec = t_spec
        out_shape = jax.ShapeDtypeStruct((bsz, M_WIDTH, seq_len), F32)
    units = nb * M_HEADS
    out, *fin = pl.pallas_call(
        functools.partial(_mlstm_kernel, reverse=reverse, final=final, nb=nb),
        grid=(bsz // nb, n_c),
        in_specs=in_specs,
        out_specs=[out_spec] + state_specs,
        out_shape=[
            out_shape,
            jax.ShapeDtypeStruct((bsz, M_HEADS, M_HEAD_DIM, M_HEAD_DIM), F32),
            jax.ShapeDtypeStruct((bsz, M_HEADS, 1, LANES), F32),
            jax.ShapeDtypeStruct((bsz, M_HEADS, 1, LANES), F32),
        ],
        scratch_shapes=[
            pltpu.VMEM((nb, M_HEADS, M_HEAD_DIM, M_HEAD_DIM), F32),
            pltpu.VMEM((nb, M_HEADS, 1, LANES), F32),
            pltpu.VMEM((nb, M_HEADS, 1, LANES), F32),
            pltpu.VMEM((units, M_CHUNK, M_CHUNK), F32),
            pltpu.VMEM((units, M_HEAD_DIM, M_CHUNK), F32),
            pltpu.VMEM((units, 8, M_CHUNK), F32),
            pltpu.VMEM((units, M_CHUNK, M_CHUNK), BF16),
            pltpu.VMEM((units, M_HEAD_DIM, M_CHUNK), BF16),
            pltpu.VMEM((units, M_HEAD_DIM, M_CHUNK), BF16),
            pltpu.VMEM((units, 1, LANES), F32),
        ],
        compiler_params=_params(("parallel", "arbitrary")),
        name="mlstm_rev" if reverse else "mlstm_fwd",
    )(*args)
    return [out.reshape(t, M_WIDTH) if final else out] + fin


def _mlstm_zero_state(bsz):
    return (jnp.zeros((bsz, M_HEADS, M_HEAD_DIM, M_HEAD_DIM), F32),
            jnp.zeros((bsz, M_HEADS, 1, LANES), F32),
            jnp.zeros((bsz, M_HEADS, 1, LANES), F32))


def _mlstm_bidir(qk, px, gates, gate_b, norm_w, state_f, state_b, seq_len):
    h_f, *fin_f = _mlstm_scan(qk, px, gates, gate_b, state_f, seq_len, False)
    y, *fin_b = _mlstm_scan(qk, px, gates, gate_b, state_b, seq_len, True, hprev=h_f, norm_w=norm_w)
    return y, tuple(fin_f), tuple(fin_b)


NA_ROWS = 8
NA_STEP = 8


def _na_kernel(q_ref, kp_ref, kc_ref, kn_ref, vp_ref, vc_ref, vn_ref, kx_ref, vx_ref, *rest, rows):
    bias_refs = rest[:NA_STEP]
    o_ref, kwin, vwin, s_scr, p_scr, l_scr = rest[NA_STEP:]
    step = pl.program_id(1)
    blk = NA_ROWS * GRID_W

    @pl.when(step % (NA_ROWS // NA_STEP) == 0)
    def _():
        kwin[0:blk, :] = kp_ref[...]
        kwin[blk:2 * blk, :] = kc_ref[...]
        kwin[2 * blk:3 * blk, :] = kn_ref[...]
        vwin[0:blk, :] = vp_ref[...]
        vwin[blk:2 * blk, :] = vc_ref[...]
        vwin[2 * blk:3 * blk, :] = vn_ref[...]

    g = step // (NA_ROWS // NA_STEP)
    n_loc = WIN_R * GRID_W
    lane = lax.broadcasted_iota(jnp.int32, (GRID_W, LANES), 1)
    scale = NA_HEAD_DIM ** -0.5
    n_ctx = kx_ref.shape[0]
    units = [(kk, hp) for kk in range(NA_STEP) for hp in range(NA_HEADS // 2)]
    offs = []
    for kk in range(NA_STEP):
        r = step * NA_STEP + kk
        r0 = jnp.clip(r - WIN_R // 2, 0, rows - WIN_R)
        offs.append(pl.multiple_of((r0 - NA_ROWS * g + NA_ROWS) * GRID_W, GRID_W))

    for u, (kk, hp) in enumerate(units):
        cols = slice(hp * LANES, (hp + 1) * LANES)
        q2 = (q_ref[kk * GRID_W:(kk + 1) * GRID_W, cols].astype(F32) * (scale * LOG2E)).astype(BF16)
        zero = jnp.zeros_like(q2)
        qs = jnp.concatenate([jnp.where(lane < NA_HEAD_DIM, q2, zero), jnp.where(lane < NA_HEAD_DIM, zero, q2)],
                             axis=0)
        s_scr[u, :, 0:n_loc] = _dot_nt(qs, kwin[pl.ds(offs[kk], n_loc), cols])
        s_scr[u, :, n_loc:n_loc + n_ctx] = _dot_nt(qs, kx_ref[:, cols])

    for u, (kk, hp) in enumerate(units):
        s_loc = s_scr[u, :, 0:n_loc] + bias_refs[kk][0, hp]
        s_ctx = s_scr[u, :, n_loc:n_loc + n_ctx]
        m = jnp.maximum(jnp.max(s_loc, axis=-1, keepdims=True), jnp.max(s_ctx, axis=-1, keepdims=True))
        p_loc = jnp.exp2(s_loc - m)
        p_ctx = jnp.exp2(s_ctx - m)
        inv = 1.0 / (jnp.sum(p_loc, axis=-1, keepdims=True) + jnp.sum(p_ctx, axis=-1, keepdims=True))
        p_scr[u, :, 0:n_loc] = p_loc.astype(BF16)
        p_scr[u, :, n_loc:n_loc + n_ctx] = p_ctx.astype(BF16)
        l_scr[u] = jnp.broadcast_to(inv, (2 * GRID_W, LANES))

    for u, (kk, hp) in enumerate(units):
        cols = slice(hp * LANES, (hp + 1) * LANES)
        o2 = _dot(p_scr[u, :, 0:n_loc], vwin[pl.ds(offs[kk], n_loc), cols])
        o2 = (o2 + _dot(p_scr[u, :, n_loc:n_loc + n_ctx], vx_ref[:, cols])) * l_scr[u]
        out = jnp.where(lane < NA_HEAD_DIM, o2[0:GRID_W], o2[GRID_W:2 * GRID_W])
        o_ref[kk * GRID_W:(kk + 1) * GRID_W, cols] = out.astype(o_ref.dtype)


def _na_bias_table(rpb):
    qc = jnp.arange(GRID_W)[:, None]
    kc = jnp.arange(GRID_W)[None, :]
    cs = jnp.clip(qc - WIN_C // 2, 0, GRID_W - WIN_C)
    ok = (kc >= cs) & (kc < cs + WIN_C)
    dc = jnp.clip(kc - qc + WIN_C - 1, 0, 2 * WIN_C - 2)
    pick = (dc[:, :, None] == jnp.arange(2 * WIN_C - 1)).astype(F32)
    cols = jnp.einsum('hrd,qkd->hrqk', rpb.astype(F32) * LOG2E, pick, precision=lax.Precision.HIGHEST)
    cols = jnp.where(ok[None, None], cols, NEG_BIG)
    tab = jnp.stack([cols[:, d0:d0 + WIN_R] for d0 in range(WIN_R)])
    tab = jnp.transpose(tab, (0, 1, 3, 2, 4))
    return tab.reshape(WIN_R, NA_HEADS // 2, 2 * GRID_W, WIN_R * GRID_W)


def _na_attention(px, pc, bias, seq_len, ctx_len):
    t = px.shape[0]
    bsz = t // seq_len
    rows = seq_len // GRID_W
    n_g = rows // NA_ROWS
    blk = NA_ROWS * GRID_W

    steps = rows // NA_STEP
    per_blk = NA_ROWS // NA_STEP

    def kv_spec(col, shift):
        def imap(b, s):
            g = jnp.clip(s // per_blk + shift, 0, n_g - 1)
            return (b * n_g + g, col)
        return pl.BlockSpec((blk, NA_WIDTH), imap)

    def bias_spec(kk):
        def imap(b, s):
            r = s * NA_STEP + kk
            r0 = jnp.clip(r - WIN_R // 2, 0, rows - WIN_R)
            return (r0 - r + WIN_R - 1, 0, 0, 0)
        return pl.BlockSpec((1, NA_HEADS // 2, 2 * GRID_W, WIN_R * GRID_W), imap)

    return pl.pallas_call(
        functools.partial(_na_kernel, rows=rows),
        grid=(bsz, steps),
        in_specs=[
            pl.BlockSpec((NA_STEP * GRID_W, NA_WIDTH), lambda b, s: (b * steps + s, COL_NAQ)),
            kv_spec(COL_NAK, -1), kv_spec(COL_NAK, 0), kv_spec(COL_NAK, 1),
            kv_spec(COL_NAV, -1), kv_spec(COL_NAV, 0), kv_spec(COL_NAV, 1),
            pl.BlockSpec((ctx_len, NA_WIDTH), lambda b, s: (b, COL_NAK)),
            pl.BlockSpec((ctx_len, NA_WIDTH), lambda b, s: (b, COL_NAV)),
        ] + [bias_spec(kk) for kk in range(NA_STEP)],
        out_specs=pl.BlockSpec((NA_STEP * GRID_W, NA_WIDTH), lambda b, s: (b * steps + s, 0)),
        out_shape=jax.ShapeDtypeStruct((t, NA_WIDTH), BF16),
        scratch_shapes=[
            pltpu.VMEM((3 * blk, NA_WIDTH), BF16), pltpu.VMEM((3 * blk, NA_WIDTH), BF16),
            pltpu.VMEM((NA_STEP * NA_HEADS // 2, 2 * GRID_W, WIN_R * GRID_W + ctx_len), F32),
            pltpu.VMEM((NA_STEP * NA_HEADS // 2, 2 * GRID_W, WIN_R * GRID_W + ctx_len), BF16),
            pltpu.VMEM((NA_STEP * NA_HEADS // 2, 2 * GRID_W, LANES), F32),
        ],
        compiler_params=_params(("parallel", "arbitrary")),
        name="na_attn",
    )(px, px, px, px, px, px, px, pc, pc, *([bias] * NA_STEP))


def _ctx_attn_kernel(q_ref, k_ref, v_ref, o_ref):
    n = q_ref.shape[0]
    lane = lax.broadcasted_iota(jnp.int32, (n, LANES), 1)
    scale = NA_HEAD_DIM ** -0.5
    for hp in range(NA_HEADS // 2):
        cols = slice(hp * LANES, (hp + 1) * LANES)
        q2 = q_ref[:, cols] * scale
        k2 = k_ref[:, cols]
        v2 = v_ref[:, cols]
        outs = []
        for sub in range(2):
            sel = (lane < NA_HEAD_DIM) if sub == 0 else (lane >= NA_HEAD_DIM)
            s = _dot_nt(jnp.where(sel, q2, jnp.zeros_like(q2)), k2)
            p = jnp.exp(s - jnp.max(s, axis=-1, keepdims=True))
            outs.append(_dot(p.astype(BF16), v2) / jnp.sum(p, axis=-1, keepdims=True))
        o_ref[:, cols] = jnp.where(lane < NA_HEAD_DIM, outs[0], outs[1]).astype(o_ref.dtype)


def _ctx_attention(pc, ctx_len):
    t = pc.shape[0]
    return pl.pallas_call(
        _ctx_attn_kernel,
        grid=(t // ctx_len,),
        in_specs=[pl.BlockSpec((ctx_len, NA_WIDTH), lambda b, col=col: (b, col))
                  for col in (COL_NAQ, COL_NAK, COL_NAV)],
        out_specs=pl.BlockSpec((ctx_len, NA_WIDTH), lambda b: (b, 0)),
        out_shape=jax.ShapeDtypeStruct((t, NA_WIDTH), BF16),
        compiler_params=_params(("parallel",)),
        name="ctx_attn",
    )(pc, pc, pc)


def _cmul(a, b):
    return a[0] * b[0] - a[1] * b[1], a[0] * b[1] + a[1] * b[0]


def _s5_tables(lam_re, lam_im, log_dt, b_re, b_im, c_re, c_im):
    L, G, S, H = S5_CHUNK, S5_GROUPS, S5_STATE, S5_GROUP
    lr = jnp.minimum(lam_re.astype(F32), S5_MAX_RE)
    li = lam_im.astype(F32)
    dt = jnp.exp(log_dt.astype(F32))[..., None]
    zr, zi = lr * dt, li * dt
    d = jnp.arange(L + 1, dtype=F32)[None, :, None, None]
    mag = jnp.exp(zr[:, None] * d)
    pw = (mag * jnp.cos(zi[:, None] * d), mag * jnp.sin(zi[:, None] * d))
    lb = (pw[0][:, 1], pw[1][:, 1])
    den = lr * lr + li * li
    ratio = (((lb[0] - 1.0) * lr + lb[1] * li) / den, (lb[1] * lr - (lb[0] - 1.0) * li) / den)
    bb = _cmul((ratio[0][..., None], ratio[1][..., None]), (b_re.astype(F32), b_im.astype(F32)))
    cc = (c_re.astype(F32), c_im.astype(F32))

    def kern(x):
        cl = _cmul((cc[0][x][None], cc[1][x][None]), (pw[0][x, :L, :, None, :], pw[1][x, :L, :, None, :]))
        return jnp.einsum('dghp,gpj->dghj', cl[0], bb[0][x]) - jnp.einsum('dghp,gpj->dghj', cl[1], bb[1][x])

    def state_in(x, steps):
        p = (pw[0][x][steps][:, :, None, :], pw[1][x][steps][:, :, None, :])
        r = _cmul(p, (jnp.swapaxes(bb[0][x], 1, 2)[None], jnp.swapaxes(bb[1][x], 1, 2)[None]))
        return jnp.transpose(r[0], (1, 0, 2, 3)), jnp.transpose(r[1], (1, 0, 2, 3))

    def state_out(x, steps):
        p = (pw[0][x][steps][:, :, None, :], pw[1][x][steps][:, :, None, :])
        r = _cmul(p, (cc[0][x][None], cc[1][x][None]))
        return jnp.transpose(r[0], (1, 3, 0, 2)), jnp.transpose(r[1], (1, 3, 0, 2))

    steps = jnp.arange(L)
    wf = state_in(0, L - 1 - steps)
    wb = state_in(1, steps)
    vf = state_out(0, steps + 1)
    vb = state_out(1, L - steps)
    Q, A = S5_QUADS, S5_QUAD_GROUPS
    kd = jnp.stack([kern(0), kern(1)]).reshape(2, L, Q, A, H, H)
    kd = jnp.transpose(kd, (0, 2, 1, 5, 3, 4)).reshape(2, Q, L, H, A * H)
    wk = jnp.stack([wf[0], wf[1], wb[0], wb[1]]).reshape(4, Q, A // 2, 2, L, H, S)
    wd = jnp.transpose(wk, (0, 1, 4, 2, 5, 3, 6)).reshape(4, Q, L, (A // 2) * H, 2 * S)
    vk = jnp.stack([vf[0], -vf[1], vb[0], -vb[1]]).reshape(4, Q, A, S, L, H)
    vd = jnp.transpose(vk, (0, 1, 4, 3, 2, 5)).reshape(4, Q, L, S, A * H)
    a_re = pw[0][:, L].reshape(2, 1, G * S)
    a_im = pw[1][:, L].reshape(2, 1, G * S)
    return wd, kd, vd, a_re, a_im


def _s5_operand(u_ref):
    return jnp.concatenate([u_ref[:, s, :] for s in range(S5_CHUNK)], axis=1).astype(BF16)


def _s5_expand_w(wd_ref, w_scr):
    H, half = S5_GROUP, S5_STATE
    w_scr[...] = jnp.zeros_like(w_scr)
    lane = lax.broadcasted_iota(jnp.int32, (H, LANES), 1)
    for k in range(4):
        for s in range(S5_CHUNK):
            for c in range(S5_QUAD_GROUPS // 2):
                blk = wd_ref[k, 0, s, c * H:(c + 1) * H, :]
                cols = slice(k * S5_QSTATE + c * LANES, k * S5_QSTATE + (c + 1) * LANES)
                row = s * LANES + 2 * c * H
                w_scr[row:row + H, cols] = jnp.where(lane < half, blk, 0.0).astype(BF16)
                w_scr[row + H:row + 2 * H, cols] = jnp.where(lane >= half, blk, 0.0).astype(BF16)


def _s5_expand_mv(kd_ref, vd_ref, m_scr, v_scr):
    H, L = S5_GROUP, S5_CHUNK
    rg = lax.broadcasted_iota(jnp.int32, (LANES, LANES), 0) // H
    lg = lax.broadcasted_iota(jnp.int32, (LANES, LANES), 1) // H
    tiles = []
    for x in range(2):
        tiles.append([jnp.where(rg == lg, jnp.concatenate([kd_ref[x, 0, d]] * S5_QUAD_GROUPS, axis=0), 0.0)
                      for d in range(L)])
    for s in range(L):
        for t in range(L):
            blk = tiles[0][t - s] if t > s else (tiles[1][s - t] if s > t else tiles[0][0] + tiles[1][0])
            m_scr[s * LANES:(s + 1) * LANES, t * LANES:(t + 1) * LANES] = blk.astype(BF16)
    lg2 = lax.broadcasted_iota(jnp.int32, (S5_STATE, LANES), 1) // H
    for k in range(4):
        for t in range(L):
            blk = vd_ref[k, 0, t]
            for b in range(S5_QUAD_GROUPS):
                row = k * S5_QSTATE + b * S5_STATE
                v_scr[row:row + S5_STATE, t * LANES:(t + 1) * LANES] = jnp.where(lg2 == b, blk, 0.0).astype(BF16)


def _s5_local_kernel(u_ref, wd_ref, sfr_ref, sfi_ref, sbr_ref, sbi_ref, w_scr):
    @pl.when(pl.program_id(1) == 0)
    def _():
        _s5_expand_w(wd_ref, w_scr)

    s = _dot(_s5_operand(u_ref), w_scr[...])
    n = S5_QSTATE
    sfr_ref[...] = s[:, 0:n]
    sfi_ref[...] = s[:, n:2 * n]
    sbr_ref[...] = s[:, 2 * n:3 * n]
    sbi_ref[...] = s[:, 3 * n:4 * n]


def _s5_local_states(u3, wd):
    rows = u3.shape[0]
    tr = min(256, rows)
    assert rows % tr == 0
    out = jax.ShapeDtypeStruct((rows, S5_GROUPS * S5_STATE), F32)
    return pl.pallas_call(
        _s5_local_kernel,
        grid=(S5_QUADS, rows // tr),
        in_specs=[
            pl.BlockSpec((tr, S5_CHUNK, LANES), lambda q, i: (i, 0, q)),
            pl.BlockSpec((4, 1) + wd.shape[2:], lambda q, i: (0, q, 0, 0, 0)),
        ],
        out_specs=[pl.BlockSpec((tr, S5_QSTATE), lambda q, i: (i, q))] * 4,
        out_shape=[out] * 4,
        scratch_shapes=[pltpu.VMEM((S5_QCOLS, 4 * S5_QSTATE), BF16)],
        compiler_params=_params(("arbitrary", "arbitrary")),
        name="s5_local",
    )(u3, wd)


def _s5_scan_kernel(sr_ref, si_ref, ar_ref, ai_ref, h0r_ref, h0i_ref, hr_ref, hi_ref, fr_ref, fi_ref,
                    hr_scr, hi_scr, *, reverse):
    i = pl.program_id(1)
    n_i = pl.num_programs(1)
    steps = sr_ref.shape[1]

    @pl.when(i == 0)
    def _():
        hr_scr[...] = h0r_ref[...]
        hi_scr[...] = h0i_ref[...]

    ar = ar_ref[...]
    ai = ai_ref[...]

    def body(j, carry):
        hr, hi = carry
        c = (steps - 1 - j) if reverse else j
        hr_ref[:, c, :] = hr
        hi_ref[:, c, :] = hi
        nr = ar * hr - ai * hi + sr_ref[:, c, :]
        ni = ar * hi + ai * hr + si_ref[:, c, :]
        return nr, ni

    hr, hi = lax.fori_loop(0, steps, body, (hr_scr[...], hi_scr[...]))
    hr_scr[...] = hr
    hi_scr[...] = hi

    @pl.when(i == n_i - 1)
    def _():
        fr_ref[...] = hr
        fi_ref[...] = hi


def _s5_chunk_scan(s_re, s_im, a_re, a_im, h0_re, h0_im, bsz, reverse):
    _, n_c, width = s_re.shape
    tr = min(64, n_c)
    tc = 512
    n_i = n_c // tr

    def rmap(j, i):
        return (0, (n_i - 1 - i) if reverse else i, j)

    big = jax.ShapeDtypeStruct((bsz, n_c, width), F32)
    small = jax.ShapeDtypeStruct((bsz, width), F32)
    return pl.pallas_call(
        functools.partial(_s5_scan_kernel, reverse=reverse),
        grid=(width // tc, n_i),
        in_specs=[
            pl.BlockSpec((bsz, tr, tc), rmap), pl.BlockSpec((bsz, tr, tc), rmap),
            pl.BlockSpec((1, tc), lambda j, i: (0, j)), pl.BlockSpec((1, tc), lambda j, i: (0, j)),
            pl.BlockSpec((bsz, tc), lambda j, i: (0, j)), pl.BlockSpec((bsz, tc), lambda j, i: (0, j)),
        ],
        out_specs=[
            pl.BlockSpec((bsz, tr, tc), rmap), pl.BlockSpec((bsz, tr, tc), rmap),
            pl.BlockSpec((bsz, tc), lambda j, i: (0, j)), pl.BlockSpec((bsz, tc), lambda j, i: (0, j)),
        ],
        out_shape=[big, big, small, small],
        scratch_shapes=[pltpu.VMEM((bsz, tc), F32), pltpu.VMEM((bsz, tc), F32)],
        compiler_params=_params(("parallel", "arbitrary")),
        name="s5_scan_rev" if reverse else "s5_scan_fwd",
    )(s_re, s_im, a_re, a_im, h0_re, h0_im)


def _s5_out_kernel(u_ref, kd_ref, vd_ref, hfr_ref, hfi_ref, hbr_ref, hbi_ref, y_ref, m_scr, v_scr):
    @pl.when(pl.program_id(1) == 0)
    def _():
        _s5_expand_mv(kd_ref, vd_ref, m_scr, v_scr)

    n = S5_QSTATE
    y = _dot(_s5_operand(u_ref), m_scr[...])
    for k, h_ref in enumerate((hfr_ref, hfi_ref, hbr_ref, hbi_ref)):
        y += _dot(h_ref[...].astype(BF16), v_scr[k * n:(k + 1) * n, :])
    for t in range(S5_CHUNK):
        y_ref[:, t, :] = y[:, t * LANES:(t + 1) * LANES]


def _s5_outputs(u3, kd, vd, h_states):
    rows = u3.shape[0]
    tr = min(256, rows)
    return pl.pallas_call(
        _s5_out_kernel,
        grid=(S5_QUADS, rows // tr),
        in_specs=[
            pl.BlockSpec((tr, S5_CHUNK, LANES), lambda q, i: (i, 0, q)),
            pl.BlockSpec((2, 1) + kd.shape[2:], lambda q, i: (0, q, 0, 0, 0)),
            pl.BlockSpec((4, 1) + vd.shape[2:], lambda q, i: (0, q, 0, 0, 0)),
        ] + [pl.BlockSpec((tr, S5_QSTATE), lambda q, i: (i, q))] * 4,
        out_specs=pl.BlockSpec((tr, S5_CHUNK, LANES), lambda q, i: (i, 0, q)),
        out_shape=jax.ShapeDtypeStruct((rows, S5_CHUNK, S5_WIDTH), F32),
        scratch_shapes=[pltpu.VMEM((S5_QCOLS, S5_QCOLS), BF16), pltpu.VMEM((4 * S5_QSTATE, S5_QCOLS), BF16)],
        compiler_params=_params(("arbitrary", "arbitrary")),
        name="s5_out",
    )(u3, kd, vd, *h_states)


def _s5_branch(u, tables, seq_len, init, want_y):
    wd, kd, vd, a_re, a_im = tables
    t = u.shape[0]
    bsz = t // seq_len
    n_c = seq_len // S5_CHUNK
    width = S5_GROUPS * S5_STATE
    u3 = u.reshape(t // S5_CHUNK, S5_CHUNK, S5_WIDTH)
    s_loc = [s.reshape(bsz, n_c, width) for s in _s5_local_states(u3, wd)]
    hfr, hfi, ffr, ffi = _s5_chunk_scan(s_loc[0], s_loc[1], a_re[0], a_im[0], init[0], init[1], bsz, False)
    hbr, hbi, fbr, fbi = _s5_chunk_scan(s_loc[2], s_loc[3], a_re[1], a_im[1], init[2], init[3], bsz, True)
    y = None
    if want_y:
        h_states = [h.reshape(bsz * n_c, width) for h in (hfr, hfi, hbr, hbi)]
        y = _s5_outputs(u3, kd, vd, h_states).reshape(t, S5_WIDTH)
    return y, (ffr, ffi, fbr, fbi)


def _sigmoid(x):
    return 0.5 * jnp.tanh(0.5 * x) + 0.5


def _gelu_tanh(x):
    return 0.5 * x * (1.0 + jnp.tanh(math.sqrt(2.0 / math.pi) * (x + 0.044715 * (x * x * x))))


def _merge_kernel(x_ref, ym_ref, yn_ref, ys_ref, u_ref, gm_ref, gn_ref, gs_ref, d_ref, gw_ref, gb_ref,
                  wm_ref, wn_ref, ws_ref, wo_ref, gate_ref, npost_ref, o_ref):
    ys = ys_ref[...] + d_ref[...] * u_ref[...]
    g = _gelu_tanh(ys)
    gg = g * _sigmoid(_dot(g.astype(BF16), gw_ref[...]) + gb_ref[...])
    y = _sigmoid(gm_ref[...].astype(F32)) * _dot(ym_ref[...], wm_ref[...])
    y += _sigmoid(gn_ref[...].astype(F32)) * _dot(yn_ref[...], wn_ref[...])
    y += _sigmoid(gs_ref[...].astype(F32)) * _dot(gg.astype(BF16), ws_ref[...])
    out = _dot(y.astype(BF16), wo_ref[...])
    o_ref[...] = x_ref[...] + gate_ref[0] * _rms(out, npost_ref[...])


def _merge(x, ym, yn, ys, u, px, wts, gate, npost, rows_per_mod):
    t = x.shape[0]
    tm = min(512, rows_per_mod)
    per = rows_per_mod // tm
    s5_d, glu_w, glu_b, w_m, w_n, w_s, w_o = wts
    row = lambda i: (i, 0)
    const = lambda i: (0, 0)
    return pl.pallas_call(
        _merge_kernel,
        grid=(t // tm,),
        in_specs=[
            pl.BlockSpec((tm, D_MODEL), row),
            pl.BlockSpec((tm, M_WIDTH), row),
            pl.BlockSpec((tm, NA_WIDTH), row),
            pl.BlockSpec((tm, S5_WIDTH), row),
            pl.BlockSpec((tm, S5_WIDTH), row),
            pl.BlockSpec((tm, D_MODEL), lambda i: (i, COL_GATE)),
            pl.BlockSpec((tm, D_MODEL), lambda i: (i, COL_GATE + 1)),
            pl.BlockSpec((tm, D_MODEL), lambda i: (i, COL_GATE + 2)),
            pl.BlockSpec((1, S5_WIDTH), const),
            pl.BlockSpec((S5_WIDTH, S5_WIDTH), const),
            pl.BlockSpec((1, S5_WIDTH), const),
            pl.BlockSpec((M_WIDTH, D_MODEL), const),
            pl.BlockSpec((NA_WIDTH, D_MODEL), const),
            pl.BlockSpec((S5_WIDTH, D_MODEL), const),
            pl.BlockSpec((D_MODEL, D_MODEL), const),
            pl.BlockSpec((1, 1, D_MODEL), lambda i: (i // per, 0, 0)),
            pl.BlockSpec((1, D_MODEL), const),
        ],
        out_specs=pl.BlockSpec((tm, D_MODEL), row),
        out_shape=jax.ShapeDtypeStruct((t, D_MODEL), F32),
        compiler_params=_params(("parallel",)),
        name="merge",
    )(x, ym, yn, ys, u, px, px, px, s5_d, glu_w, glu_b, w_m, w_n, w_s, w_o, gate, npost)


def _swiglu(h, wg, wu, wd):
    a = _dot(h, wg)
    a = a * jax.nn.sigmoid(a) * _dot(h, wu)
    return _dot(a.astype(BF16), wd)


def _ffn_kernel(x_ref, g_ref, sh_ref, sc_ref, wg_ref, wu_ref, wd_ref, gate_ref, npost_ref, o_ref, *, ff_tile):
    h = (_rms(x_ref[...], g_ref[...]) * (1.0 + sc_ref[0]) + sh_ref[0]).astype(BF16)
    acc = jnp.zeros(x_ref.shape, F32)
    for j in range(wg_ref.shape[1] // ff_tile):
        cols = slice(j * ff_tile, (j + 1) * ff_tile)
        acc += _swiglu(h, wg_ref[:, cols], wu_ref[:, cols], wd_ref[cols, :])
    o_ref[...] = x_ref[...] + gate_ref[0] * _rms(acc, npost_ref[...])


def _ffn(x, g, shift, scale, w_gate, w_up, w_down, gate, npost, rows_per_mod):
    t = x.shape[0]
    ff = w_gate.shape[1]
    ff_tile = ff // 2
    tm = min(512, rows_per_mod)
    per = rows_per_mod // tm
    assert ff_tile % LANES == 0
    resident = pl.Buffered(1)
    return pl.pallas_call(
        functools.partial(_ffn_kernel, ff_tile=ff_tile),
        grid=(t // tm,),
        in_specs=[
            pl.BlockSpec((tm, D_MODEL), lambda i: (i, 0)),
            pl.BlockSpec((1, D_MODEL), lambda i: (0, 0)),
            pl.BlockSpec((1, 1, D_MODEL), lambda i: (i // per, 0, 0)),
            pl.BlockSpec((1, 1, D_MODEL), lambda i: (i // per, 0, 0)),
            pl.BlockSpec((D_MODEL, ff), lambda i: (0, 0), pipeline_mode=resident),
            pl.BlockSpec((D_MODEL, ff), lambda i: (0, 0), pipeline_mode=resident),
            pl.BlockSpec((ff, D_MODEL), lambda i: (0, 0), pipeline_mode=resident),
            pl.BlockSpec((1, 1, D_MODEL), lambda i: (i // per, 0, 0)),
            pl.BlockSpec((1, D_MODEL), lambda i: (0, 0)),
        ],
        out_specs=pl.BlockSpec((tm, D_MODEL), lambda i: (i, 0)),
        out_shape=jax.ShapeDtypeStruct((t, D_MODEL), F32),
        compiler_params=_params(("parallel",)),
        name="ffn",
    )(x, g, shift, scale, w_gate, w_up, w_down, gate, npost)


MOE_ROWS = MOE_TILE * TOP_K + N_EXPERTS * MOE_GRAIN
MOE_META = 2 * N_EXPERTS


def _moe_dispatch_kernel(x_ref, g_ref, sh_ref, sc_ref, rh_ref, rl_ref, xs_ref, route_ref, meta_ref):
    tm = x_ref.shape[0]
    h = _rms(x_ref[...], g_ref[...]) * (1.0 + sc_ref[0]) + sh_ref[0]
    hi = h.astype(BF16)
    lo = (h - hi.astype(F32)).astype(BF16)
    logits = _dot(hi, rh_ref[...]) + _dot(lo, rh_ref[...]) + _dot(hi, rl_ref[...])
    lane = lax.broadcasted_iota(jnp.int32, logits.shape, 1)
    logits = jnp.where(lane < N_EXPERTS, logits, -jnp.inf)
    m1 = jnp.max(logits, axis=-1, keepdims=True)
    i1 = jnp.min(jnp.where(logits == m1, lane, LANES), axis=-1, keepdims=True)
    rest = jnp.where(lane == i1, -jnp.inf, logits)
    m2 = jnp.max(rest, axis=-1, keepdims=True)
    i2 = jnp.min(jnp.where(rest == m2, lane, LANES), axis=-1, keepdims=True)
    e2 = jnp.exp(m2 - m1)
    sel1 = jnp.where(lane == i1, 1.0, 0.0)
    sel2 = jnp.where(lane == i2, 1.0, 0.0)
    sel = sel1 + sel2
    sel1_t, sel2_t = sel1.T, sel2.T
    sel_t = sel1_t + sel2_t

    ti = lax.broadcasted_iota(jnp.int32, (tm, tm), 0)
    si = lax.broadcasted_iota(jnp.int32, (tm, tm), 1)
    rank_c = _dot(jnp.where(si < ti, 1.0, 0.0).astype(BF16), sel.astype(BF16))
    rank_r = _dot(sel_t.astype(BF16), jnp.where(ti < si, 1.0, 0.0).astype(BF16))
    blk_row = jnp.ceil(jnp.sum(sel, axis=0, keepdims=True) * (1.0 / MOE_GRAIN))
    blk_col = jnp.ceil(jnp.sum(sel_t, axis=1, keepdims=True) * (1.0 / MOE_GRAIN))
    ea = lax.broadcasted_iota(jnp.int32, (LANES, LANES), 0)
    eb = lax.broadcasted_iota(jnp.int32, (LANES, LANES), 1)
    off_row = _dot(jnp.broadcast_to(blk_row, (8, LANES)).astype(BF16),
                   jnp.where(ea < eb, 1.0, 0.0).astype(BF16))[0:1, :]
    off_col = _dot(jnp.where(eb < ea, 1.0, 0.0).astype(BF16),
                   jnp.broadcast_to(blk_col, (LANES, LANES)).astype(BF16))[:, 0:1]

    pos_c = off_row * MOE_GRAIN + rank_c
    pos1_c = jnp.sum(jnp.where(lane == i1, pos_c, 0.0), axis=-1, keepdims=True)
    pos2_c = jnp.sum(jnp.where(lane == i2, pos_c, 0.0), axis=-1, keepdims=True)
    route_ref[...] = (jnp.where(lane == 0, pos1_c, 0.0) + jnp.where(lane == 1, pos2_c, 0.0)
                      + jnp.where(lane == 2, 1.0 / (1.0 + e2), 0.0) + jnp.where(lane == 3, e2 / (1.0 + e2), 0.0))
    pos_r = off_col * MOE_GRAIN + rank_r
    pos1_r = jnp.sum(sel1_t * pos_r, axis=0, keepdims=True)
    pos2_r = jnp.sum(sel2_t * pos_r, axis=0, keepdims=True)
    half = MOE_ROWS // 2
    for c in range(2):
        slot = (lax.broadcasted_iota(jnp.int32, (half, tm), 0) + c * half).astype(F32)
        onehot = jnp.where(slot == pos1_r, 1.0, jnp.where(slot == pos2_r, 1.0, 0.0)).astype(BF16)
        xs_ref[c * half:(c + 1) * half, :] = _dot(onehot, hi).astype(BF16)

    lane8 = lax.broadcasted_iota(jnp.int32, (8, LANES), 1)
    blk8 = jnp.broadcast_to(blk_row, (8, LANES))
    off8 = pltpu.roll(jnp.broadcast_to(off_row, (8, LANES)), N_EXPERTS, 1)
    meta = jnp.where(lane8 < N_EXPERTS, blk8, jnp.where(lane8 < MOE_META, off8, 0.0))
    meta_ref[0] = meta.astype(jnp.int32)


def _moe_dispatch(x, g, shift, scale, r_hi, r_lo, rows_per_mod):
    t = x.shape[0]
    tm = MOE_TILE
    assert rows_per_mod % tm == 0
    per = rows_per_mod // tm
    n_t = t // tm
    return pl.pallas_call(
        _moe_dispatch_kernel,
        grid=(n_t,),
        in_specs=[
            pl.BlockSpec((tm, D_MODEL), lambda i: (i, 0)),
            pl.BlockSpec((1, D_MODEL), lambda i: (0, 0)),
            pl.BlockSpec((1, 1, D_MODEL), lambda i: (i // per, 0, 0)),
            pl.BlockSpec((1, 1, D_MODEL), lambda i: (i // per, 0, 0)),
            pl.BlockSpec((D_MODEL, LANES), lambda i: (0, 0)),
            pl.BlockSpec((D_MODEL, LANES), lambda i: (0, 0)),
        ],
        out_specs=[
            pl.BlockSpec((MOE_ROWS, D_MODEL), lambda i: (i, 0)),
            pl.BlockSpec((tm, LANES), lambda i: (i, 0)),
            pl.BlockSpec((1, 8, LANES), lambda i: (i, 0, 0)),
        ],
        out_shape=[
            jax.ShapeDtypeStruct((n_t * MOE_ROWS, D_MODEL), BF16),
            jax.ShapeDtypeStruct((t, LANES), F32),
            jax.ShapeDtypeStruct((n_t, 8, LANES), jnp.int32),
        ],
        compiler_params=_params(("parallel",)),
        name="moe_dispatch",
    )(x, g, shift, scale, r_hi, r_lo)


MOE_FF_TILE = 512


def _moe_combine_kernel(ys_ref, route_ref, x_ref, gate_ref, npost_ref, o_ref):
    tm = x_ref.shape[0]
    lane = lax.broadcasted_iota(jnp.int32, (tm, LANES), 1)
    route = route_ref[...]
    pos1, pos2, w1, w2 = (jnp.sum(jnp.where(lane == k, route, 0.0), axis=-1, keepdims=True) for k in range(4))
    width = 512
    acc = jnp.zeros((tm, D_MODEL), F32)
    for c in range(MOE_ROWS // width):
        slot = (lax.broadcasted_iota(jnp.int32, (tm, width), 1) + c * width).astype(F32)
        wmat = jnp.where(slot == pos1, w1, 0.0) + jnp.where(slot == pos2, w2, 0.0)
        acc += _dot(wmat.astype(BF16), ys_ref[c * width:(c + 1) * width, :])
    o_ref[...] = x_ref[...] + gate_ref[0] * _rms(acc, npost_ref[...])


MOE_BLOCK = 512


def _moe_plan(meta, n_t):
    per_tile = MOE_ROWS // MOE_GRAIN
    per_blk = MOE_BLOCK // MOE_GRAIN
    nblk = meta[:, 0, :N_EXPERTS]
    first = meta[:, 0, N_EXPERTS:MOE_META]
    tot = jnp.sum(nblk, axis=0)
    tot_pad = ((tot + per_blk - 1) // per_blk) * per_blk
    ends = jnp.cumsum(tot_pad)
    seg_dst = (ends - tot_pad)[None, :] + jnp.cumsum(nblk, axis=0) - nblk
    k = jnp.arange(per_tile, dtype=jnp.int32)[None, :, None]
    inside = (k >= first[:, None, :]) & (k < (first + nblk)[:, None, :])
    dst = jnp.sum(jnp.where(inside, seg_dst[:, None, :] + k - first[:, None, :], 0), axis=-1)
    used = jnp.any(inside, axis=-1).reshape(-1)
    dst = dst.reshape(-1).astype(jnp.int32)
    n_big = (n_t * per_tile + N_EXPERTS * per_blk) // per_blk
    big = jnp.arange(n_big, dtype=jnp.int32)[:, None] * per_blk
    expert = jnp.minimum(jnp.sum(big >= ends[None, :], axis=-1), N_EXPERTS - 1).astype(jnp.int32)
    valid = (big[:, 0] < ends[-1]).astype(jnp.int32)
    src = jnp.zeros((n_big * per_blk,), jnp.int32).at[jnp.where(used, dst, n_big * per_blk)].set(
        jnp.arange(dst.shape[0], dtype=jnp.int32), mode='drop')
    return jnp.where(used, dst, 0), jnp.concatenate([expert, valid, src]), n_big


def _moe_global_kernel(plan_ref, *refs, ff_tile):
    per_blk = MOE_BLOCK // MOE_GRAIN
    x_refs = refs[:per_blk]
    wg_ref, wu_ref, wd_ref, out_ref, xin_scr = refs[per_blk:]
    b = pl.program_id(0)
    valid = plan_ref[pl.num_programs(0) + b]
    ff = wg_ref.shape[2]

    @pl.when(valid == 1)
    def _():
        for k in range(per_blk):
            xin_scr[k * MOE_GRAIN:(k + 1) * MOE_GRAIN, :] = x_refs[k][...]
        xin = xin_scr[...]
        y = jnp.zeros((MOE_BLOCK, D_MODEL), F32)
        for j in range(ff // ff_tile):
            cols = slice(j * ff_tile, (j + 1) * ff_tile)
            y += _swiglu(xin, wg_ref[0, :, cols], wu_ref[0, :, cols], wd_ref[0, cols, :])
        out_ref[...] = y.astype(out_ref.dtype)

    @pl.when(valid == 0)
    def _():
        out_ref[...] = jnp.zeros_like(out_ref)


def _moe_experts_global(xs, plan, n_big, w_gate, w_up, w_down):
    n_e, _, ff = w_gate.shape
    per_blk = MOE_BLOCK // MOE_GRAIN
    resident = pl.Buffered(1)

    def src_spec(k):
        return pl.BlockSpec((MOE_GRAIN, D_MODEL), lambda b, p: (p[2 * n_big + b * per_blk + k], 0))

    grid_spec = pltpu.PrefetchScalarGridSpec(
        num_scalar_prefetch=1,
        grid=(n_big,),
        in_specs=[src_spec(k) for k in range(per_blk)] + [
            pl.BlockSpec((1, D_MODEL, ff), lambda b, p: (p[b], 0, 0), pipeline_mode=resident),
            pl.BlockSpec((1, D_MODEL, ff), lambda b, p: (p[b], 0, 0), pipeline_mode=resident),
            pl.BlockSpec((1, ff, D_MODEL), lambda b, p: (p[b], 0, 0), pipeline_mode=resident),
        ],
        out_specs=pl.BlockSpec((MOE_BLOCK, D_MODEL), lambda b, p: (b, 0)),
        scratch_shapes=[pltpu.VMEM((MOE_BLOCK, D_MODEL), BF16)],
    )
    return pl.pallas_call(
        functools.partial(_moe_global_kernel, ff_tile=MOE_FF_TILE),
        grid_spec=grid_spec,
        out_shape=jax.ShapeDtypeStruct((n_big * MOE_BLOCK, D_MODEL), BF16),
        compiler_params=_params(("arbitrary",)),
        name="moe_experts",
    )(plan, *([xs] * per_blk), w_gate, w_up, w_down)


def _moe_global(x, g, shift, scale, r_hi, r_lo, w_gate, w_up, w_down, gate, npost, rows_per_mod):
    xs, route, meta = _moe_dispatch(x, g, shift, scale, r_hi, r_lo, rows_per_mod)
    n_t = xs.shape[0] // MOE_ROWS
    block_map, plan, n_big = _moe_plan(meta, n_t)
    yg = _moe_experts_global(xs, plan, n_big, w_gate, w_up, w_down)
    return _moe_gather_combine(yg, block_map, route, x, gate, npost, rows_per_mod)


def _moe_gather_combine_kernel(map_ref, *refs):
    per_tile = MOE_ROWS // MOE_GRAIN
    y_refs = refs[:per_tile]
    route_ref, x_ref, gate_ref, npost_ref, o_ref, ys_scr = refs[per_tile:]
    del map_ref
    for k in range(per_tile):
        ys_scr[k * MOE_GRAIN:(k + 1) * MOE_GRAIN, :] = y_refs[k][...]
    _moe_combine_kernel(ys_scr, route_ref, x_ref, gate_ref, npost_ref, o_ref)


def _moe_gather_combine(yg, block_map, route, x, gate, npost, rows_per_mod):
    t = x.shape[0]
    tm = MOE_TILE
    per = rows_per_mod // tm
    per_tile = MOE_ROWS // MOE_GRAIN

    def y_spec(k):
        return pl.BlockSpec((MOE_GRAIN, D_MODEL), lambda i, m: (m[i * per_tile + k], 0))

    grid_spec = pltpu.PrefetchScalarGridSpec(
        num_scalar_prefetch=1,
        grid=(t // tm,),
        in_specs=[y_spec(k) for k in range(per_tile)] + [
            pl.BlockSpec((tm, LANES), lambda i, m: (i, 0)),
            pl.BlockSpec((tm, D_MODEL), lambda i, m: (i, 0)),
            pl.BlockSpec((1, 1, D_MODEL), lambda i, m: (i // per, 0, 0)),
            pl.BlockSpec((1, D_MODEL), lambda i, m: (0, 0)),
        ],
        out_specs=pl.BlockSpec((tm, D_MODEL), lambda i, m: (i, 0)),
        scratch_shapes=[pltpu.VMEM((MOE_ROWS, D_MODEL), BF16)],
    )
    return pl.pallas_call(
        _moe_gather_combine_kernel,
        grid_spec=grid_spec,
        out_shape=jax.ShapeDtypeStruct((t, D_MODEL), F32),
        compiler_params=_params(("arbitrary",)),
        name="moe_combine",
    )(block_map, *([yg] * per_tile), route, x, gate, npost)


def _reorder_w_in(w):
    o = 4 * M_WIDTH
    gates = jnp.pad(w[:, o:o + M_GATES], ((0, 0), (0, LANES - M_GATES)))
    na = w[:, o + M_GATES:o + M_GATES + 3 * NA_WIDTH]
    u0 = o + M_GATES + 3 * NA_WIDTH
    main = jnp.concatenate([w[:, :o], w[:, u0 + S5_WIDTH:], na], axis=1)
    return main.astype(BF16), gates.astype(BF16), w[:, u0:u0 + S5_WIDTH].astype(BF16)


def _row(v):
    return v.reshape(1, -1).astype(F32)


def _token_mixers(x, xc, mx, mc, lp, rope, seq_len, ctx_len, ctx_out):
    bsz = x.shape[0] // seq_len
    w_main, w_gates, w_u = _reorder_w_in(lp['w_in'])
    g_pre = _row(lp['norm_mix_pre'])
    conv_w = lp['m_conv_w'].astype(F32)
    conv_b = _row(lp['m_conv_b'])
    px, gx, ux = _inproj(x, g_pre, mx[0], mx[1], w_main, w_gates, w_u, conv_w, conv_b, rope[0], rope[1],
                         seq_len, seq_len, True)
    pc, gc, uc = _inproj(xc, g_pre, mc[0], mc[1], w_main, w_gates, w_u, conv_w, conv_b, rope[0], rope[1],
                         xc.shape[0], ctx_len, False)

    gate_b = jnp.pad(_row(lp['m_gate_b']), ((0, 0), (0, LANES - M_GATES)))
    m_norm = _row(lp['m_norm'])
    zero = _mlstm_zero_state(bsz)
    ym_c, fin_f, fin_b = _mlstm_bidir(pc, pc, gc, gate_b, m_norm, zero, zero, ctx_len)
    ym_x, _, _ = _mlstm_bidir(px, px, gx, gate_b, m_norm, fin_f, fin_b, seq_len)

    yn_x = _na_attention(px, pc, _na_bias_table(lp['na_rpb']), seq_len, ctx_len)

    tables = _s5_tables(lp['s5_lam_re'], lp['s5_lam_im'], lp['s5_log_dt'], lp['s5_b_re'], lp['s5_b_im'],
                        lp['s5_c_re'], lp['s5_c_im'])
    zs = jnp.zeros((bsz, S5_GROUPS * S5_STATE), F32)
    ys_c, fin_s = _s5_branch(uc, tables, ctx_len, (zs, zs, zs, zs), ctx_out)
    ys_x, _ = _s5_branch(ux, tables, seq_len, fin_s, True)

    wts = (_row(lp['s5_d']), lp['s5_glu_w'].astype(BF16), _row(lp['s5_glu_b']),
           lp['w_branch_m'].astype(BF16), lp['w_branch_na'].astype(BF16), lp['w_branch_s5'].astype(BF16),
           lp['w_out'].astype(BF16))
    g_post = _row(lp['norm_mix_post'])
    x = _merge(x, ym_x, yn_x, ys_x, ux, px, wts, mx[2], g_post, seq_len)
    if ctx_out:
        yn_c = _ctx_attention(pc, ctx_len)
        xc = _merge(xc, ym_c, yn_c, ys_c, uc, pc, wts, mc[2], g_post, xc.shape[0])
    return x, xc


def kernel(x, c, ctx, c_ctx, ada_w, ada_b, norm_mix_pre, norm_mix_post, norm_ffn_pre, norm_ffn_post, w_in, m_gate_b, m_conv_w, m_conv_b, m_norm, na_rpb, s5_lam_re, s5_lam_im, s5_log_dt, s5_b_re, s5_b_im, s5_c_re, s5_c_im, s5_d, s5_glu_w, s5_glu_b, w_branch_m, w_branch_na, w_branch_s5, w_out, ffn_w_gate, ffn_w_up, ffn_w_down, moe_router, moe_w_gate, moe_w_up, moe_w_down):
    bsz, seq_len, d = x.shape
    ctx_len = ctx.shape[1]
    depth = w_in.shape[0]
    assert d == D_MODEL and seq_len % (GRID_W * NA_ROWS) == 0 and ctx_len % M_CHUNK == 0
    rope = _rope_tables(seq_len)
    xf = x.reshape(bsz * seq_len, d)
    xc = ctx.reshape(bsz * ctx_len, d)
    cvecs = jnp.zeros((16, d), F32).at[:bsz].set(c).at[bsz].set(c_ctx)
    for l in range(depth):
        last = l == depth - 1
        mod = _adaln(cvecs, ada_w[l].astype(BF16), _row(ada_b[l]))
        mx = [mod[:bsz, k * d:(k + 1) * d].reshape(bsz, 1, d) for k in range(N_MOD)]
        mc = [mod[bsz:bsz + 1, k * d:(k + 1) * d].reshape(1, 1, d) for k in range(N_MOD)]
        lp = {
            'norm_mix_pre': norm_mix_pre[l], 'norm_mix_post': norm_mix_post[l],
            'w_in': w_in[l], 'm_gate_b': m_gate_b[l], 'm_conv_w': m_conv_w[l], 'm_conv_b': m_conv_b[l],
            'm_norm': m_norm[l], 'na_rpb': na_rpb[l], 's5_lam_re': s5_lam_re[l], 's5_lam_im': s5_lam_im[l],
            's5_log_dt': s5_log_dt[l], 's5_b_re': s5_b_re[l], 's5_b_im': s5_b_im[l], 's5_c_re': s5_c_re[l],
            's5_c_im': s5_c_im[l], 's5_d': s5_d[l], 's5_glu_w': s5_glu_w[l], 's5_glu_b': s5_glu_b[l],
            'w_branch_m': w_branch_m[l], 'w_branch_na': w_branch_na[l], 'w_branch_s5': w_branch_s5[l],
            'w_out': w_out[l],
        }
        xf, xc = _token_mixers(xf, xc, mx, mc, lp, rope, seq_len, ctx_len, not last)
        j = l // 2
        g_pre, g_post = _row(norm_ffn_pre[l]), _row(norm_ffn_post[l])
        if l % 2 == 0:
            wg, wu, wd = ffn_w_gate[j].astype(BF16), ffn_w_up[j].astype(BF16), ffn_w_down[j].astype(BF16)

            def channel(h, m, rows):
                return _ffn(h, g_pre, m[3], m[4], wg, wu, wd, m[5], g_post, rows)
        else:
            wg, wu, wd = moe_w_gate[j].astype(BF16), moe_w_up[j].astype(BF16), moe_w_down[j].astype(BF16)
            router = jnp.pad(moe_router[j].astype(F32), ((0, 0), (0, LANES - N_EXPERTS)))
            r_hi = router.astype(BF16)
            r_lo = (router - r_hi.astype(F32)).astype(BF16)

            def channel(h, m, rows):
                return _moe_global(h, g_pre, m[3], m[4], r_hi, r_lo, wg, wu, wd, m[5], g_post, rows)
        xf = channel(xf, mx, seq_len)
        if not last:
            xc = channel(xc, mc, xc.shape[0])
    return xf.reshape(bsz, seq_len, d)
```

```python
import functools
import math

import jax
import jax.numpy as jnp
from jax import lax
from jax.experimental import pallas as pl
from jax.experimental.pallas import tpu as pltpu

F32 = jnp.float32
BF16 = jnp.bfloat16

D_MODEL = 1024
EPS = 1e-6
N_MOD = 6
GRID_W = 64

M_HEADS = 4
M_HEAD_DIM = 128
M_WIDTH = M_HEADS * M_HEAD_DIM
M_GATES = 4 * M_HEADS
M_CHUNK = 128
ROPE_BASE = 10000.0

NA_HEADS = 8
NA_HEAD_DIM = 64
NA_WIDTH = NA_HEADS * NA_HEAD_DIM
WIN_R = 8
WIN_C = 16
NEG_BIG = -1e30
LOG2E = math.log2(math.e)

S5_GROUP = 16
S5_WIDTH = 512
S5_GROUPS = S5_WIDTH // S5_GROUP
S5_STATE = 64
S5_MAX_RE = -1e-4

LANES = 128
VMEM_LIMIT = 52 * 1024 * 1024

S5_CHUNK = 16
S5_QUAD_GROUPS = LANES // S5_GROUP
S5_QUADS = S5_GROUPS // S5_QUAD_GROUPS
S5_QCOLS = S5_CHUNK * LANES
S5_QSTATE = S5_QUAD_GROUPS * S5_STATE

N_EXPERTS = 8
TOP_K = 2
MOE_TILE = 1024
MOE_GRAIN = 64

COL_QK, COL_V, COL_O, COL_NAQ, COL_NAK, COL_NAV = 0, 2, 3, 10, 11, 12
COL_GATE = 2
IN_MAIN = 6656


def _params(sem):
    return pltpu.CompilerParams(dimension_semantics=sem, vmem_limit_bytes=VMEM_LIMIT)


def _dot(a, b):
    return jnp.dot(a, b, preferred_element_type=F32)


def _dot_nt(a, b):
    return lax.dot_general(a, b, (((1,), (1,)), ((), ())), preferred_element_type=F32)


def _split3(x):
    hi = x.astype(BF16)
    r1 = x - hi.astype(F32)
    mid = r1.astype(BF16)
    lo = (r1 - mid.astype(F32)).astype(BF16)
    return hi, mid, lo


def _rms(x, g):
    return x * lax.rsqrt(jnp.mean(x * x, axis=-1, keepdims=True) + EPS) * g


X_HALO = 8


def _inproj_kernel(x_ref, xp_ref, xn_ref, g_ref, sh_ref, sc_ref, w_ref, wg_ref, wu_ref, cw_ref, cb_ref,
                   cos_ref, sin_ref, o_ref, og_ref, ou_ref, *, tn, seq_len, rope):
    tm = x_ref.shape[0]
    qk_w = 2 * M_WIDTH

    def prep(xv):
        return (_rms(xv, g_ref[...]) * (1.0 + sc_ref[0]) + sh_ref[0]).astype(BF16)

    hb = prep(x_ref[...])
    og_ref[...] = _dot(hb, wg_ref[...])
    ou_ref[...] = _dot(hb, wu_ref[...])
    for c in range((IN_MAIN - qk_w) // tn):
        cols = slice(qk_w + c * tn, qk_w + (c + 1) * tn)
        o_ref[:, cols] = _dot(hb, w_ref[:, cols]).astype(o_ref.dtype)

    x = _dot(hb, w_ref[:, 0:qk_w])
    halo = _dot(prep(jnp.concatenate([xp_ref[...], xn_ref[...]], axis=0)), w_ref[:, 0:qk_w])
    row = lax.broadcasted_iota(jnp.int32, x.shape, 0)
    pos = (row + pl.program_id(0) * tm) % seq_len
    xp = jnp.where(row == 0, halo[X_HALO - 1:X_HALO, :], pltpu.roll(x, 1, 0))
    xp = jnp.where(pos == 0, 0.0, xp)
    xn = jnp.where(row == tm - 1, halo[X_HALO:X_HALO + 1, :], pltpu.roll(x, tm - 1, 0))
    xn = jnp.where(pos == seq_len - 1, 0.0, xn)
    y = xp * cw_ref[0:1, :] + x * cw_ref[1:2, :] + xn * cw_ref[2:3, :] + cb_ref[...]
    y = y * _sigmoid(y)
    kscale = M_HEAD_DIM ** -0.5
    for hh in range(2 * M_HEADS):
        ys = y[:, hh * LANES:(hh + 1) * LANES]
        if rope:
            ys = ys * cos_ref[...] + _swap32(ys) * sin_ref[...]
        if hh >= M_HEADS:
            ys = ys * kscale
        o_ref[:, hh * LANES:(hh + 1) * LANES] = ys.astype(o_ref.dtype)


def _inproj(x, g, shift, scale, w_main, w_gates, w_u, conv_w, conv_b, cos_t, sin_t, rows_per_mod, seq_len, rope):
    t = x.shape[0]
    tm = min(512, rows_per_mod)
    tn = (IN_MAIN - 2 * M_WIDTH) // 4
    assert rows_per_mod % tm == 0 and t % tm == 0 and tn % LANES == 0
    assert seq_len % tm == 0 or tm % seq_len == 0
    per = rows_per_mod // tm
    hb = tm // X_HALO
    nh = t // X_HALO
    tiles = max(seq_len // tm, 1)
    tab_rows = min(tm, seq_len)
    if tab_rows < tm:
        cos_t, sin_t = (jnp.tile(a[:seq_len], (tm // seq_len, 1)) for a in (cos_t, sin_t))
    resident = pl.Buffered(1)
    return pl.pallas_call(
        functools.partial(_inproj_kernel, tn=tn, seq_len=seq_len, rope=rope),
        grid=(t // tm,),
        in_specs=[
            pl.BlockSpec((tm, D_MODEL), lambda i: (i, 0)),
            pl.BlockSpec((X_HALO, D_MODEL), lambda i: (jnp.maximum(i * hb - 1, 0), 0)),
            pl.BlockSpec((X_HALO, D_MODEL), lambda i: (jnp.minimum((i + 1) * hb, nh - 1), 0)),
            pl.BlockSpec((1, D_MODEL), lambda i: (0, 0)),
            pl.BlockSpec((1, 1, D_MODEL), lambda i: (i // per, 0, 0)),
            pl.BlockSpec((1, 1, D_MODEL), lambda i: (i // per, 0, 0)),
            pl.BlockSpec((D_MODEL, IN_MAIN), lambda i: (0, 0), pipeline_mode=resident),
            pl.BlockSpec((D_MODEL, LANES), lambda i: (0, 0), pipeline_mode=resident),
            pl.BlockSpec((D_MODEL, S5_WIDTH), lambda i: (0, 0), pipeline_mode=resident),
            pl.BlockSpec((3, 2 * M_WIDTH), lambda i: (0, 0)),
            pl.BlockSpec((1, 2 * M_WIDTH), lambda i: (0, 0)),
            pl.BlockSpec((tm, LANES), lambda i: (i % tiles, 0)),
            pl.BlockSpec((tm, LANES), lambda i: (i % tiles, 0)),
        ],
        out_specs=[
            pl.BlockSpec((tm, IN_MAIN), lambda i: (i, 0)),
            pl.BlockSpec((tm, LANES), lambda i: (i, 0)),
            pl.BlockSpec((tm, S5_WIDTH), lambda i: (i, 0)),
        ],
        out_shape=[jax.ShapeDtypeStruct((t, IN_MAIN), BF16), jax.ShapeDtypeStruct((t, LANES), F32),
                   jax.ShapeDtypeStruct((t, S5_WIDTH), F32)],
        compiler_params=_params(("parallel",)),
        name="inproj",
    )(x, x, x, g, shift, scale, w_main, w_gates, w_u, conv_w, conv_b, cos_t, sin_t)


def _adaln_kernel(c_ref, w_ref, b_ref, o_ref):
    c = c_ref[...]
    s = c * jax.nn.sigmoid(c)
    o_ref[...] = _dot(s.astype(BF16), w_ref[...]) + b_ref[...]


def _adaln(cvecs, w, b):
    n = w.shape[1]
    tn = 1024
    return pl.pallas_call(
        _adaln_kernel,
        grid=(n // tn,),
        in_specs=[
            pl.BlockSpec(cvecs.shape, lambda j: (0, 0)),
            pl.BlockSpec((D_MODEL, tn), lambda j: (0, j)),
            pl.BlockSpec((1, tn), lambda j: (0, j)),
        ],
        out_specs=pl.BlockSpec((cvecs.shape[0], tn), lambda j: (0, j)),
        out_shape=jax.ShapeDtypeStruct((cvecs.shape[0], n), F32),
        compiler_params=_params(("parallel",)),
        name="adaln",
    )(cvecs, w, b)


def _swap32(x):
    lane = lax.broadcasted_iota(jnp.int32, x.shape, 1)
    fwd = pltpu.roll(x, 96, 1)
    bwd = pltpu.roll(x, 32, 1)
    return jnp.where((lane % 64) < 32, fwd, bwd)


def _rope_tables(n):
    pos = jnp.arange(n, dtype=jnp.int32)
    row = (pos // GRID_W).astype(F32)
    col = (pos % GRID_W).astype(F32)
    n_freq = M_HEAD_DIM // 4
    inv = ROPE_BASE ** (-jnp.arange(n_freq, dtype=F32) / n_freq)
    ar, ac = row[:, None] * inv, col[:, None] * inv
    cos_t = jnp.concatenate([jnp.cos(ar), jnp.cos(ar), jnp.cos(ac), jnp.cos(ac)], axis=-1)
    sin_t = jnp.concatenate([-jnp.sin(ar), jnp.sin(ar), -jnp.sin(ac), jnp.sin(ac)], axis=-1)
    return cos_t, sin_t


def _mlstm_kernel(*refs, reverse, final, nb):
    if final:
        (qk_ref, v_ref, g_ref, gb_ref, c0_ref, n0_ref, m0_ref, hp_ref, op_ref, nw_ref,
         out_ref, cf_ref, nf_ref, mf_ref, c_scr, n_scr, m_scr, qk_scr, qc_scr, qn_scr, s_scr, vt_scr, vu_scr, dec_scr) = refs
    else:
        (qk_ref, v_ref, g_ref, gb_ref, c0_ref, n0_ref, m0_ref,
         out_ref, cf_ref, nf_ref, mf_ref, c_scr, n_scr, m_scr, qk_scr, qc_scr, qn_scr, s_scr, vt_scr, vu_scr, dec_scr) = refs
    c_idx = pl.program_id(1)
    n_c = pl.num_programs(1)
    L = M_CHUNK

    @pl.when(c_idx == 0)
    def _():
        c_scr[...] = c0_ref[...]
        n_scr[...] = n0_ref[...]
        m_scr[...] = m0_ref[...]

    ti = lax.broadcasted_iota(jnp.int32, (L, L), 0)
    si = lax.broadcasted_iota(jnp.int32, (L, L), 1)
    keep = (si >= ti) if reverse else (si <= ti)
    tri = jnp.where(keep, 1.0, 0.0).astype(BF16)
    i_off = 2 * M_HEADS if reverse else 0
    f_off = i_off + M_HEADS
    end = 0 if reverse else L - 1

    units = [(a, b) for a in range(nb) for b in range(M_HEADS)]

    def head(ref, gi, h, base=0):
        return ref[gi, :, base + h * LANES:base + (h + 1) * LANES]

    keep_t = (ti >= si) if reverse else (ti <= si)

    for u, (gi, h) in enumerate(units):
        q = head(qk_ref, gi, h)
        qk_scr[u] = _dot_nt(head(qk_ref, gi, h, M_WIDTH), q)
        qc_scr[u] = _dot_nt(c_scr[gi, h].astype(BF16), q)
        qn_scr[u] = _dot_nt(jnp.broadcast_to(n_scr[gi, h], (8, LANES)).astype(BF16), q)

    for u, (gi, h) in enumerate(units):
        if h == 0:
            g = g_ref[gi] + gb_ref[...]
            f_hi, f_mid, f_lo = _split3(jax.nn.log_sigmoid(g))
            bsum = _dot(tri, f_hi) + _dot(tri, f_mid) + _dot(tri, f_lo)
            g_t = g.T
            b_t = bsum.T
        m_prev = m_scr[gi, h][:, 0:1]
        b_row = b_t[f_off + h:f_off + h + 1, :]
        i_row = g_t[i_off + h:i_off + h + 1, :]
        b_end = b_row[:, end:end + 1]
        d_col = g[:, i_off + h:i_off + h + 1] - bsum[:, f_off + h:f_off + h + 1]

        log_w = jnp.where(keep_t, b_row + d_col, -jnp.inf)
        carry_log = b_row + m_prev
        m_t = jnp.maximum(carry_log, jnp.max(log_w, axis=0, keepdims=True))
        s_t = qk_scr[u] * jnp.exp(log_w - m_t)
        c_scale = jnp.exp(carry_log - m_t)
        den = jnp.sum(s_t, axis=0, keepdims=True) + c_scale * qn_scr[u][0:1, :]
        inv = 1.0 / jnp.maximum(jnp.abs(den), jnp.exp(-m_t))
        s_scr[u] = (s_t * inv).astype(BF16)
        qc_scr[u] = qc_scr[u] * (c_scale * inv)

        m_new = jnp.maximum(b_end + m_prev, jnp.max(b_end - b_row + i_row, axis=-1, keepdims=True))
        u_row = jnp.exp(b_end - b_row + i_row - m_new)
        decay = jnp.exp(b_end + m_prev - m_new)
        v_t = head(v_ref, gi, h).astype(F32).T
        vt_scr[u] = v_t.astype(BF16)
        vu_scr[u] = (v_t * u_row).astype(BF16)
        u8 = jnp.broadcast_to(u_row, (8, L)).astype(BF16)
        n_scr[gi, h] = decay * n_scr[gi, h] + _dot(u8, head(qk_ref, gi, h, M_WIDTH))[0:1, :]
        m_scr[gi, h] = jnp.broadcast_to(m_new, (1, LANES))
        dec_scr[u] = jnp.broadcast_to(decay, (1, LANES))

    for u, (gi, h) in enumerate(units):
        qc_scr[u] = _dot(vt_scr[u], s_scr[u]) + qc_scr[u]
        c_scr[gi, h] = dec_scr[u][:, 0:1] * c_scr[gi, h] + _dot(vu_scr[u], head(qk_ref, gi, h, M_WIDTH))

    for u, (gi, h) in enumerate(units):
        rows = slice(h * M_HEAD_DIM, (h + 1) * M_HEAD_DIM)
        if final:
            hs = qc_scr[u] + hp_ref[gi, rows, :]
            mu = jnp.mean(hs, axis=0, keepdims=True)
            var = jnp.mean(jnp.square(hs - mu), axis=0, keepdims=True)
            hn = ((hs - mu) * lax.rsqrt(var + EPS)).T
            y = hn * nw_ref[:, h * LANES:(h + 1) * LANES] * jax.nn.sigmoid(head(op_ref, gi, h).astype(F32))
            out_ref[gi, :, h * LANES:(h + 1) * LANES] = y.astype(out_ref.dtype)
        else:
            out_ref[gi, rows, :] = qc_scr[u]

    @pl.when(c_idx == n_c - 1)
    def _():
        cf_ref[...] = c_scr[...]
        nf_ref[...] = n_scr[...]
        mf_ref[...] = m_scr[...]


def _mlstm_scan(qk, px, gates, gate_b, state, seq_len, reverse, hprev=None, norm_w=None):
    t = qk.shape[0]
    bsz = t // seq_len
    n_c = seq_len // M_CHUNK
    final = hprev is not None
    nb = next(n for n in (8, 4, 2, 1) if bsz % n == 0)
    qk, px, gates = (a.reshape(bsz, seq_len, a.shape[1]) for a in (qk, px, gates))

    def chunk(c):
        return (n_c - 1 - c) if reverse else c

    state_specs = [
        pl.BlockSpec((nb, M_HEADS, M_HEAD_DIM, M_HEAD_DIM), lambda b, c: (b, 0, 0, 0)),
        pl.BlockSpec((nb, M_HEADS, 1, LANES), lambda b, c: (b, 0, 0, 0)),
        pl.BlockSpec((nb, M_HEADS, 1, LANES), lambda b, c: (b, 0, 0, 0)),
    ]
    in_specs = [
        pl.BlockSpec((nb, M_CHUNK, 2 * M_WIDTH), lambda b, c: (b, chunk(c), 0)),
        pl.BlockSpec((nb, M_CHUNK, M_WIDTH), lambda b, c: (b, chunk(c), COL_V)),
        pl.BlockSpec((nb, M_CHUNK, LANES), lambda b, c: (b, chunk(c), 0)),
        pl.BlockSpec((1, LANES), lambda b, c: (0, 0)),
    ] + state_specs
    args = [qk, px, gates, gate_b, *state]
    t_spec = pl.BlockSpec((nb, M_WIDTH, M_CHUNK), lambda b, c: (b, 0, chunk(c)))
    if final:
        in_specs += [
            t_spec,
            pl.BlockSpec((nb, M_CHUNK, M_WIDTH), lambda b, c: (b, chunk(c), COL_O)),
            pl.BlockSpec((1, M_WIDTH), lambda b, c: (0, 0)),
        ]
        args += [hprev, px, norm_w]
        out_spec = pl.BlockSpec((nb, M_CHUNK, M_WIDTH), lambda b, c: (b, chunk(c), 0))
        out_shape = jax.ShapeDtypeStruct((bsz, seq_len, M_WIDTH), BF16)
    else:
        out_spec = t_spec
        out_shape = jax.ShapeDtypeStruct((bsz, M_WIDTH, seq_len), F32)
    units = nb * M_HEADS
    out, *fin = pl.pallas_call(
        functools.partial(_mlstm_kernel, reverse=reverse, final=final, nb=nb),
        grid=(bsz // nb, n_c),
        in_specs=in_specs,
        out_specs=[out_spec] + state_specs,
        out_shape=[
            out_shape,
            jax.ShapeDtypeStruct((bsz, M_HEADS, M_HEAD_DIM, M_HEAD_DIM), F32),
            jax.ShapeDtypeStruct((bsz, M_HEADS, 1, LANES), F32),
            jax.ShapeDtypeStruct((bsz, M_HEADS, 1, LANES), F32),
        ],
        scratch_shapes=[
            pltpu.VMEM((nb, M_HEADS, M_HEAD_DIM, M_HEAD_DIM), F32),
            pltpu.VMEM((nb, M_HEADS, 1, LANES), F32),
            pltpu.VMEM((nb, M_HEADS, 1, LANES), F32),
            pltpu.VMEM((units, M_CHUNK, M_CHUNK), F32),
            pltpu.VMEM((units, M_HEAD_DIM, M_CHUNK), F32),
            pltpu.VMEM((units, 8, M_CHUNK), F32),
            pltpu.VMEM((units, M_CHUNK, M_CHUNK), BF16),
            pltpu.VMEM((units, M_HEAD_DIM, M_CHUNK), BF16),
            pltpu.VMEM((units, M_HEAD_DIM, M_CHUNK), BF16),
            pltpu.VMEM((units, 1, LANES), F32),
        ],
        compiler_params=_params(("parallel", "arbitrary")),
        name="mlstm_rev" if reverse else "mlstm_fwd",
    )(*args)
    return [out.reshape(t, M_WIDTH) if final else out] + fin


def _mlstm_zero_state(bsz):
    return (jnp.zeros((bsz, M_HEADS, M_HEAD_DIM, M_HEAD_DIM), F32),
            jnp.zeros((bsz, M_HEADS, 1, LANES), F32),
            jnp.zeros((bsz, M_HEADS, 1, LANES), F32))


def _mlstm_bidir(qk, px, gates, gate_b, norm_w, state_f, state_b, seq_len):
    h_f, *fin_f = _mlstm_scan(qk, px, gates, gate_b, state_f, seq_len, False)
    y, *fin_b = _mlstm_scan(qk, px, gates, gate_b, state_b, seq_len, True, hprev=h_f, norm_w=norm_w)
    return y, tuple(fin_f), tuple(fin_b)


NA_ROWS = 8
NA_STEP = 8


def _na_kernel(q_ref, kp_ref, kc_ref, kn_ref, vp_ref, vc_ref, vn_ref, kx_ref, vx_ref, *rest, rows):
    bias_refs = rest[:NA_STEP]
    o_ref, kwin, vwin, s_scr, p_scr, l_scr = rest[NA_STEP:]
    step = pl.program_id(1)
    blk = NA_ROWS * GRID_W

    @pl.when(step % (NA_ROWS // NA_STEP) == 0)
    def _():
        kwin[0:blk, :] = kp_ref[...]
        kwin[blk:2 * blk, :] = kc_ref[...]
        kwin[2 * blk:3 * blk, :] = kn_ref[...]
        vwin[0:blk, :] = vp_ref[...]
        vwin[blk:2 * blk, :] = vc_ref[...]
        vwin[2 * blk:3 * blk, :] = vn_ref[...]

    g = step // (NA_ROWS // NA_STEP)
    n_loc = WIN_R * GRID_W
    lane = lax.broadcasted_iota(jnp.int32, (GRID_W, LANES), 1)
    scale = NA_HEAD_DIM ** -0.5
    n_ctx = kx_ref.shape[0]
    units = [(kk, hp) for kk in range(NA_STEP) for hp in range(NA_HEADS // 2)]
    offs = []
    for kk in range(NA_STEP):
        r = step * NA_STEP + kk
        r0 = jnp.clip(r - WIN_R // 2, 0, rows - WIN_R)
        offs.append(pl.multiple_of((r0 - NA_ROWS * g + NA_ROWS) * GRID_W, GRID_W))

    for u, (kk, hp) in enumerate(units):
        cols = slice(hp * LANES, (hp + 1) * LANES)
        q2 = (q_ref[kk * GRID_W:(kk + 1) * GRID_W, cols].astype(F32) * (scale * LOG2E)).astype(BF16)
        zero = jnp.zeros_like(q2)
        qs = jnp.concatenate([jnp.where(lane < NA_HEAD_DIM, q2, zero), jnp.where(lane < NA_HEAD_DIM, zero, q2)],
                             axis=0)
        s_scr[u, :, 0:n_loc] = _dot_nt(qs, kwin[pl.ds(offs[kk], n_loc), cols])
        s_scr[u, :, n_loc:n_loc + n_ctx] = _dot_nt(qs, kx_ref[:, cols])

    for u, (kk, hp) in enumerate(units):
        s_loc = s_scr[u, :, 0:n_loc] + bias_refs[kk][0, hp]
        s_ctx = s_scr[u, :, n_loc:n_loc + n_ctx]
        m = jnp.maximum(jnp.max(s_loc, axis=-1, keepdims=True), jnp.max(s_ctx, axis=-1, keepdims=True))
        p_loc = jnp.exp2(s_loc - m)
        p_ctx = jnp.exp2(s_ctx - m)
        inv = 1.0 / (jnp.sum(p_loc, axis=-1, keepdims=True) + jnp.sum(p_ctx, axis=-1, keepdims=True))
        p_scr[u, :, 0:n_loc] = p_loc.astype(BF16)
        p_scr[u, :, n_loc:n_loc + n_ctx] = p_ctx.astype(BF16)
        l_scr[u] = jnp.broadcast_to(inv, (2 * GRID_W, LANES))

    for u, (kk, hp) in enumerate(units):
        cols = slice(hp * LANES, (hp + 1) * LANES)
        o2 = _dot(p_scr[u, :, 0:n_loc], vwin[pl.ds(offs[kk], n_loc), cols])
        o2 = (o2 + _dot(p_scr[u, :, n_loc:n_loc + n_ctx], vx_ref[:, cols])) * l_scr[u]
        out = jnp.where(lane < NA_HEAD_DIM, o2[0:GRID_W], o2[GRID_W:2 * GRID_W])
        o_ref[kk * GRID_W:(kk + 1) * GRID_W, cols] = out.astype(o_ref.dtype)


def _na_bias_table(rpb):
    qc = jnp.arange(GRID_W)[:, None]
    kc = jnp.arange(GRID_W)[None, :]
    cs = jnp.clip(qc - WIN_C // 2, 0, GRID_W - WIN_C)
    ok = (kc >= cs) & (kc < cs + WIN_C)
    dc = jnp.clip(kc - qc + WIN_C - 1, 0, 2 * WIN_C - 2)
    pick = (dc[:, :, None] == jnp.arange(2 * WIN_C - 1)).astype(F32)
    cols = jnp.einsum('hrd,qkd->hrqk', rpb.astype(F32) * LOG2E, pick, precision=lax.Precision.HIGHEST)
    cols = jnp.where(ok[None, None], cols, NEG_BIG)
    tab = jnp.stack([cols[:, d0:d0 + WIN_R] for d0 in range(WIN_R)])
    tab = jnp.transpose(tab, (0, 1, 3, 2, 4))
    return tab.reshape(WIN_R, NA_HEADS // 2, 2 * GRID_W, WIN_R * GRID_W)


def _na_attention(px, pc, bias, seq_len, ctx_len):
    t = px.shape[0]
    bsz = t // seq_len
    rows = seq_len // GRID_W
    n_g = rows // NA_ROWS
    blk = NA_ROWS * GRID_W

    steps = rows // NA_STEP
    per_blk = NA_ROWS // NA_STEP

    def kv_spec(col, shift):
        def imap(b, s):
            g = jnp.clip(s // per_blk + shift, 0, n_g - 1)
            return (b * n_g + g, col)
        return pl.BlockSpec((blk, NA_WIDTH), imap)

    def bias_spec(kk):
        def imap(b, s):
            r = s * NA_STEP + kk
            r0 = jnp.clip(r - WIN_R // 2, 0, rows - WIN_R)
            return (r0 - r + WIN_R - 1, 0, 0, 0)
        return pl.BlockSpec((1, NA_HEADS // 2, 2 * GRID_W, WIN_R * GRID_W), imap)

    return pl.pallas_call(
        functools.partial(_na_kernel, rows=rows),
        grid=(bsz, steps),
        in_specs=[
            pl.BlockSpec((NA_STEP * GRID_W, NA_WIDTH), lambda b, s: (b * steps + s, COL_NAQ)),
            kv_spec(COL_NAK, -1), kv_spec(COL_NAK, 0), kv_spec(COL_NAK, 1),
            kv_spec(COL_NAV, -1), kv_spec(COL_NAV, 0), kv_spec(COL_NAV, 1),
            pl.BlockSpec((ctx_len, NA_WIDTH), lambda b, s: (b, COL_NAK)),
            pl.BlockSpec((ctx_len, NA_WIDTH), lambda b, s: (b, COL_NAV)),
        ] + [bias_spec(kk) for kk in range(NA_STEP)],
        out_specs=pl.BlockSpec((NA_STEP * GRID_W, NA_WIDTH), lambda b, s: (b * steps + s, 0)),
        out_shape=jax.ShapeDtypeStruct((t, NA_WIDTH), BF16),
        scratch_shapes=[
            pltpu.VMEM((3 * blk, NA_WIDTH), BF16), pltpu.VMEM((3 * blk, NA_WIDTH), BF16),
            pltpu.VMEM((NA_STEP * NA_HEADS // 2, 2 * GRID_W, WIN_R * GRID_W + ctx_len), F32),
            pltpu.VMEM((NA_STEP * NA_HEADS // 2, 2 * GRID_W, WIN_R * GRID_W + ctx_len), BF16),
            pltpu.VMEM((NA_STEP * NA_HEADS // 2, 2 * GRID_W, LANES), F32),
        ],
        compiler_params=_params(("parallel", "arbitrary")),
        name="na_attn",
    )(px, px, px, px, px, px, px, pc, pc, *([bias] * NA_STEP))


def _ctx_attn_kernel(q_ref, k_ref, v_ref, o_ref):
    n = q_ref.shape[0]
    lane = lax.broadcasted_iota(jnp.int32, (n, LANES), 1)
    scale = NA_HEAD_DIM ** -0.5
    for hp in range(NA_HEADS // 2):
        cols = slice(hp * LANES, (hp + 1) * LANES)
        q2 = q_ref[:, cols] * scale
        k2 = k_ref[:, cols]
        v2 = v_ref[:, cols]
        outs = []
        for sub in range(2):
            sel = (lane < NA_HEAD_DIM) if sub == 0 else (lane >= NA_HEAD_DIM)
            s = _dot_nt(jnp.where(sel, q2, jnp.zeros_like(q2)), k2)
            p = jnp.exp(s - jnp.max(s, axis=-1, keepdims=True))
            outs.append(_dot(p.astype(BF16), v2) / jnp.sum(p, axis=-1, keepdims=True))
        o_ref[:, cols] = jnp.where(lane < NA_HEAD_DIM, outs[0], outs[1]).astype(o_ref.dtype)


def _ctx_attention(pc, ctx_len):
    t = pc.shape[0]
    return pl.pallas_call(
        _ctx_attn_kernel,
        grid=(t // ctx_len,),
        in_specs=[pl.BlockSpec((ctx_len, NA_WIDTH), lambda b, col=col: (b, col))
                  for col in (COL_NAQ, COL_NAK, COL_NAV)],
        out_specs=pl.BlockSpec((ctx_len, NA_WIDTH), lambda b: (b, 0)),
        out_shape=jax.ShapeDtypeStruct((t, NA_WIDTH), BF16),
        compiler_params=_params(("parallel",)),
        name="ctx_attn",
    )(pc, pc, pc)


def _cmul(a, b):
    return a[0] * b[0] - a[1] * b[1], a[0] * b[1] + a[1] * b[0]


def _s5_tables(lam_re, lam_im, log_dt, b_re, b_im, c_re, c_im):
    L, G, S, H = S5_CHUNK, S5_GROUPS, S5_STATE, S5_GROUP
    lr = jnp.minimum(lam_re.astype(F32), S5_MAX_RE)
    li = lam_im.astype(F32)
    dt = jnp.exp(log_dt.astype(F32))[..., None]
    zr, zi = lr * dt, li * dt
    d = jnp.arange(L + 1, dtype=F32)[None, :, None, None]
    mag = jnp.exp(zr[:, None] * d)
    pw = (mag * jnp.cos(zi[:, None] * d), mag * jnp.sin(zi[:, None] * d))
    lb = (pw[0][:, 1], pw[1][:, 1])
    den = lr * lr + li * li
    ratio = (((lb[0] - 1.0) * lr + lb[1] * li) / den, (lb[1] * lr - (lb[0] - 1.0) * li) / den)
    bb = _cmul((ratio[0][..., None], ratio[1][..., None]), (b_re.astype(F32), b_im.astype(F32)))
    cc = (c_re.astype(F32), c_im.astype(F32))

    def kern(x):
        cl = _cmul((cc[0][x][None], cc[1][x][None]), (pw[0][x, :L, :, None, :], pw[1][x, :L, :, None, :]))
        return jnp.einsum('dghp,gpj->dghj', cl[0], bb[0][x]) - jnp.einsum('dghp,gpj->dghj', cl[1], bb[1][x])

    def state_in(x, steps):
        p = (pw[0][x][steps][:, :, None, :], pw[1][x][steps][:, :, None, :])
        r = _cmul(p, (jnp.swapaxes(bb[0][x], 1, 2)[None], jnp.swapaxes(bb[1][x], 1, 2)[None]))
        return jnp.transpose(r[0], (1, 0, 2, 3)), jnp.transpose(r[1], (1, 0, 2, 3))

    def state_out(x, steps):
        p = (pw[0][x][steps][:, :, None, :], pw[1][x][steps][:, :, None, :])
        r = _cmul(p, (cc[0][x][None], cc[1][x][None]))
        return jnp.transpose(r[0], (1, 3, 0, 2)), jnp.transpose(r[1], (1, 3, 0, 2))

    steps = jnp.arange(L)
    wf = state_in(0, L - 1 - steps)
    wb = state_in(1, steps)
    vf = state_out(0, steps + 1)
    vb = state_out(1, L - steps)
    Q, A = S5_QUADS, S5_QUAD_GROUPS
    kd = jnp.stack([kern(0), kern(1)]).reshape(2, L, Q, A, H, H)
    kd = jnp.transpose(kd, (0, 2, 1, 5, 3, 4)).reshape(2, Q, L, H, A * H)
    wk = jnp.stack([wf[0], wf[1], wb[0], wb[1]]).reshape(4, Q, A // 2, 2, L, H, S)
    wd = jnp.transpose(wk, (0, 1, 4, 2, 5, 3, 6)).reshape(4, Q, L, (A // 2) * H, 2 * S)
    vk = jnp.stack([vf[0], -vf[1], vb[0], -vb[1]]).reshape(4, Q, A, S, L, H)
    vd = jnp.transpose(vk, (0, 1, 4, 3, 2, 5)).reshape(4, Q, L, S, A * H)
    a_re = pw[0][:, L].reshape(2, 1, G * S)
    a_im = pw[1][:, L].reshape(2, 1, G * S)
    return wd, kd, vd, a_re, a_im


def _s5_operand(u_ref):
    return jnp.concatenate([u_ref[:, s, :] for s in range(S5_CHUNK)], axis=1).astype(BF16)


def _s5_expand_w(wd_ref, w_scr):
    H, half = S5_GROUP, S5_STATE
    w_scr[...] = jnp.zeros_like(w_scr)
    lane = lax.broadcasted_iota(jnp.int32, (H, LANES), 1)
    for k in range(4):
        for s in range(S5_CHUNK):
            for c in range(S5_QUAD_GROUPS // 2):
                blk = wd_ref[k, 0, s, c * H:(c + 1) * H, :]
                cols = slice(k * S5_QSTATE + c * LANES, k * S5_QSTATE + (c + 1) * LANES)
                row = s * LANES + 2 * c * H
                w_scr[row:row + H, cols] = jnp.where(lane < half, blk, 0.0).astype(BF16)
                w_scr[row + H:row + 2 * H, cols] = jnp.where(lane >= half, blk, 0.0).astype(BF16)


def _s5_expand_mv(kd_ref, vd_ref, m_scr, v_scr):
    H, L = S5_GROUP, S5_CHUNK
    rg = lax.broadcasted_iota(jnp.int32, (LANES, LANES), 0) // H
    lg = lax.broadcasted_iota(jnp.int32, (LANES, LANES), 1) // H
    tiles = []
    for x in range(2):
        tiles.append([jnp.where(rg == lg, jnp.concatenate([kd_ref[x, 0, d]] * S5_QUAD_GROUPS, axis=0), 0.0)
                      for d in range(L)])
    for s in range(L):
        for t in range(L):
            blk = tiles[0][t - s] if t > s else (tiles[1][s - t] if s > t else tiles[0][0] + tiles[1][0])
            m_scr[s * LANES:(s + 1) * LANES, t * LANES:(t + 1) * LANES] = blk.astype(BF16)
    lg2 = lax.broadcasted_iota(jnp.int32, (S5_STATE, LANES), 1) // H
    for k in range(4):
        for t in range(L):
            blk = vd_ref[k, 0, t]
            for b in range(S5_QUAD_GROUPS):
                row = k * S5_QSTATE + b * S5_STATE
                v_scr[row:row + S5_STATE, t * LANES:(t + 1) * LANES] = jnp.where(lg2 == b, blk, 0.0).astype(BF16)


def _s5_local_kernel(u_ref, wd_ref, sfr_ref, sfi_ref, sbr_ref, sbi_ref, w_scr):
    @pl.when(pl.program_id(1) == 0)
    def _():
        _s5_expand_w(wd_ref, w_scr)

    s = _dot(_s5_operand(u_ref), w_scr[...])
    n = S5_QSTATE
    sfr_ref[...] = s[:, 0:n]
    sfi_ref[...] = s[:, n:2 * n]
    sbr_ref[...] = s[:, 2 * n:3 * n]
    sbi_ref[...] = s[:, 3 * n:4 * n]


def _s5_local_states(u3, wd):
    rows = u3.shape[0]
    tr = min(256, rows)
    assert rows % tr == 0
    out = jax.ShapeDtypeStruct((rows, S5_GROUPS * S5_STATE), F32)
    return pl.pallas_call(
        _s5_local_kernel,
        grid=(S5_QUADS, rows // tr),
        in_specs=[
            pl.BlockSpec((tr, S5_CHUNK, LANES), lambda q, i: (i, 0, q)),
            pl.BlockSpec((4, 1) + wd.shape[2:], lambda q, i: (0, q, 0, 0, 0)),
        ],
        out_specs=[pl.BlockSpec((tr, S5_QSTATE), lambda q, i: (i, q))] * 4,
        out_shape=[out] * 4,
        scratch_shapes=[pltpu.VMEM((S5_QCOLS, 4 * S5_QSTATE), BF16)],
        compiler_params=_params(("arbitrary", "arbitrary")),
        name="s5_local",
    )(u3, wd)


def _s5_scan_kernel(sr_ref, si_ref, ar_ref, ai_ref, h0r_ref, h0i_ref, hr_ref, hi_ref, fr_ref, fi_ref,
                    hr_scr, hi_scr, *, reverse):
    i = pl.program_id(1)
    n_i = pl.num_programs(1)
    steps = sr_ref.shape[1]

    @pl.when(i == 0)
    def _():
        hr_scr[...] = h0r_ref[...]
        hi_scr[...] = h0i_ref[...]

    ar = ar_ref[...]
    ai = ai_ref[...]

    def body(j, carry):
        hr, hi = carry
        c = (steps - 1 - j) if reverse else j
        hr_ref[:, c, :] = hr
        hi_ref[:, c, :] = hi
        nr = ar * hr - ai * hi + sr_ref[:, c, :]
        ni = ar * hi + ai * hr + si_ref[:, c, :]
        return nr, ni

    hr, hi = lax.fori_loop(0, steps, body, (hr_scr[...], hi_scr[...]))
    hr_scr[...] = hr
    hi_scr[...] = hi

    @pl.when(i == n_i - 1)
    def _():
        fr_ref[...] = hr
        fi_ref[...] = hi


def _s5_chunk_scan(s_re, s_im, a_re, a_im, h0_re, h0_im, bsz, reverse):
    _, n_c, width = s_re.shape
    tr = min(64, n_c)
    tc = 512
    n_i = n_c // tr

    def rmap(j, i):
        return (0, (n_i - 1 - i) if reverse else i, j)

    big = jax.ShapeDtypeStruct((bsz, n_c, width), F32)
    small = jax.ShapeDtypeStruct((bsz, width), F32)
    return pl.pallas_call(
        functools.partial(_s5_scan_kernel, reverse=reverse),
        grid=(width // tc, n_i),
        in_specs=[
            pl.BlockSpec((bsz, tr, tc), rmap), pl.BlockSpec((bsz, tr, tc), rmap),
            pl.BlockSpec((1, tc), lambda j, i: (0, j)), pl.BlockSpec((1, tc), lambda j, i: (0, j)),
            pl.BlockSpec((bsz, tc), lambda j, i: (0, j)), pl.BlockSpec((bsz, tc), lambda j, i: (0, j)),
        ],
        out_specs=[
            pl.BlockSpec((bsz, tr, tc), rmap), pl.BlockSpec((bsz, tr, tc), rmap),
            pl.BlockSpec((bsz, tc), lambda j, i: (0, j)), pl.BlockSpec((bsz, tc), lambda j, i: (0, j)),
        ],
        out_shape=[big, big, small, small],
        scratch_shapes=[pltpu.VMEM((bsz, tc), F32), pltpu.VMEM((bsz, tc), F32)],
        compiler_params=_params(("parallel", "arbitrary")),
        name="s5_scan_rev" if reverse else "s5_scan_fwd",
    )(s_re, s_im, a_re, a_im, h0_re, h0_im)


def _s5_out_kernel(u_ref, kd_ref, vd_ref, hfr_ref, hfi_ref, hbr_ref, hbi_ref, y_ref, m_scr, v_scr):
    @pl.when(pl.program_id(1) == 0)
    def _():
        _s5_expand_mv(kd_ref, vd_ref, m_scr, v_scr)

    n = S5_QSTATE
    y = _dot(_s5_operand(u_ref), m_scr[...])
    for k, h_ref in enumerate((hfr_ref, hfi_ref, hbr_ref, hbi_ref)):
        y += _dot(h_ref[...].astype(BF16), v_scr[k * n:(k + 1) * n, :])
    for t in range(S5_CHUNK):
        y_ref[:, t, :] = y[:, t * LANES:(t + 1) * LANES]


def _s5_outputs(u3, kd, vd, h_states):
    rows = u3.shape[0]
    tr = min(256, rows)
    return pl.pallas_call(
        _s5_out_kernel,
        grid=(S5_QUADS, rows // tr),
        in_specs=[
            pl.BlockSpec((tr, S5_CHUNK, LANES), lambda q, i: (i, 0, q)),
            pl.BlockSpec((2, 1) + kd.shape[2:], lambda q, i: (0, q, 0, 0, 0)),
            pl.BlockSpec((4, 1) + vd.shape[2:], lambda q, i: (0, q, 0, 0, 0)),
        ] + [pl.BlockSpec((tr, S5_QSTATE), lambda q, i: (i, q))] * 4,
        out_specs=pl.BlockSpec((tr, S5_CHUNK, LANES), lambda q, i: (i, 0, q)),
        out_shape=jax.ShapeDtypeStruct((rows, S5_CHUNK, S5_WIDTH), F32),
        scratch_shapes=[pltpu.VMEM((S5_QCOLS, S5_QCOLS), BF16), pltpu.VMEM((4 * S5_QSTATE, S5_QCOLS), BF16)],
        compiler_params=_params(("arbitrary", "arbitrary")),
        name="s5_out",
    )(u3, kd, vd, *h_states)


def _s5_branch(u, tables, seq_len, init, want_y):
    wd, kd, vd, a_re, a_im = tables
    t = u.shape[0]
    bsz = t // seq_len
    n_c = seq_len // S5_CHUNK
    width = S5_GROUPS * S5_STATE
    u3 = u.reshape(t // S5_CHUNK, S5_CHUNK, S5_WIDTH)
    s_loc = [s.reshape(bsz, n_c, width) for s in _s5_local_states(u3, wd)]
    hfr, hfi, ffr, ffi = _s5_chunk_scan(s_loc[0], s_loc[1], a_re[0], a_im[0], init[0], init[1], bsz, False)
    hbr, hbi, fbr, fbi = _s5_chunk_scan(s_loc[2], s_loc[3], a_re[1], a_im[1], init[2], init[3], bsz, True)
    y = None
    if want_y:
        h_states = [h.reshape(bsz * n_c, width) for h in (hfr, hfi, hbr, hbi)]
        y = _s5_outputs(u3, kd, vd, h_states).reshape(t, S5_WIDTH)
    return y, (ffr, ffi, fbr, fbi)


def _sigmoid(x):
    return 0.5 * jnp.tanh(0.5 * x) + 0.5


def _gelu_tanh(x):
    return 0.5 * x * (1.0 + jnp.tanh(math.sqrt(2.0 / math.pi) * (x + 0.044715 * (x * x * x))))


def _merge_kernel(x_ref, ym_ref, yn_ref, ys_ref, u_ref, gm_ref, gn_ref, gs_ref, d_ref, gw_ref, gb_ref,
                  wm_ref, wn_ref, ws_ref, wo_ref, gate_ref, npost_ref, o_ref):
    ys = ys_ref[...] + d_ref[...] * u_ref[...]
    g = _gelu_tanh(ys)
    gg = g * _sigmoid(_dot(g.astype(BF16), gw_ref[...]) + gb_ref[...])
    y = _sigmoid(gm_ref[...].astype(F32)) * _dot(ym_ref[...], wm_ref[...])
    y += _sigmoid(gn_ref[...].astype(F32)) * _dot(yn_ref[...], wn_ref[...])
    y += _sigmoid(gs_ref[...].astype(F32)) * _dot(gg.astype(BF16), ws_ref[...])
    out = _dot(y.astype(BF16), wo_ref[...])
    o_ref[...] = x_ref[...] + gate_ref[0] * _rms(out, npost_ref[...])


def _merge(x, ym, yn, ys, u, px, wts, gate, npost, rows_per_mod):
    t = x.shape[0]
    tm = min(512, rows_per_mod)
    per = rows_per_mod // tm
    s5_d, glu_w, glu_b, w_m, w_n, w_s, w_o = wts
    row = lambda i: (i, 0)
    const = lambda i: (0, 0)
    return pl.pallas_call(
        _merge_kernel,
        grid=(t // tm,),
        in_specs=[
            pl.BlockSpec((tm, D_MODEL), row),
            pl.BlockSpec((tm, M_WIDTH), row),
            pl.BlockSpec((tm, NA_WIDTH), row),
            pl.BlockSpec((tm, S5_WIDTH), row),
            pl.BlockSpec((tm, S5_WIDTH), row),
            pl.BlockSpec((tm, D_MODEL), lambda i: (i, COL_GATE)),
            pl.BlockSpec((tm, D_MODEL), lambda i: (i, COL_GATE + 1)),
            pl.BlockSpec((tm, D_MODEL), lambda i: (i, COL_GATE + 2)),
            pl.BlockSpec((1, S5_WIDTH), const),
            pl.BlockSpec((S5_WIDTH, S5_WIDTH), const),
            pl.BlockSpec((1, S5_WIDTH), const),
            pl.BlockSpec((M_WIDTH, D_MODEL), const),
            pl.BlockSpec((NA_WIDTH, D_MODEL), const),
            pl.BlockSpec((S5_WIDTH, D_MODEL), const),
            pl.BlockSpec((D_MODEL, D_MODEL), const),
            pl.BlockSpec((1, 1, D_MODEL), lambda i: (i // per, 0, 0)),
            pl.BlockSpec((1, D_MODEL), const),
        ],
        out_specs=pl.BlockSpec((tm, D_MODEL), row),
        out_shape=jax.ShapeDtypeStruct((t, D_MODEL), F32),
        compiler_params=_params(("parallel",)),
        name="merge",
    )(x, ym, yn, ys, u, px, px, px, s5_d, glu_w, glu_b, w_m, w_n, w_s, w_o, gate, npost)


def _swiglu(h, wg, wu, wd):
    a = _dot(h, wg)
    a = a * jax.nn.sigmoid(a) * _dot(h, wu)
    return _dot(a.astype(BF16), wd)


def _ffn_kernel(x_ref, g_ref, sh_ref, sc_ref, wg_ref, wu_ref, wd_ref, gate_ref, npost_ref, o_ref, *, ff_tile):
    h = (_rms(x_ref[...], g_ref[...]) * (1.0 + sc_ref[0]) + sh_ref[0]).astype(BF16)
    acc = jnp.zeros(x_ref.shape, F32)
    for j in range(wg_ref.shape[1] // ff_tile):
        cols = slice(j * ff_tile, (j + 1) * ff_tile)
        acc += _swiglu(h, wg_ref[:, cols], wu_ref[:, cols], wd_ref[cols, :])
    o_ref[...] = x_ref[...] + gate_ref[0] * _rms(acc, npost_ref[...])


def _ffn(x, g, shift, scale, w_gate, w_up, w_down, gate, npost, rows_per_mod):
    t = x.shape[0]
    ff = w_gate.shape[1]
    ff_tile = ff // 2
    tm = min(512, rows_per_mod)
    per = rows_per_mod // tm
    assert ff_tile % LANES == 0
    resident = pl.Buffered(1)
    return pl.pallas_call(
        functools.partial(_ffn_kernel, ff_tile=ff_tile),
        grid=(t // tm,),
        in_specs=[
            pl.BlockSpec((tm, D_MODEL), lambda i: (i, 0)),
            pl.BlockSpec((1, D_MODEL), lambda i: (0, 0)),
            pl.BlockSpec((1, 1, D_MODEL), lambda i: (i // per, 0, 0)),
            pl.BlockSpec((1, 1, D_MODEL), lambda i: (i // per, 0, 0)),
            pl.BlockSpec((D_MODEL, ff), lambda i: (0, 0), pipeline_mode=resident),
            pl.BlockSpec((D_MODEL, ff), lambda i: (0, 0), pipeline_mode=resident),
            pl.BlockSpec((ff, D_MODEL), lambda i: (0, 0), pipeline_mode=resident),
            pl.BlockSpec((1, 1, D_MODEL), lambda i: (i // per, 0, 0)),
            pl.BlockSpec((1, D_MODEL), lambda i: (0, 0)),
        ],
        out_specs=pl.BlockSpec((tm, D_MODEL), lambda i: (i, 0)),
        out_shape=jax.ShapeDtypeStruct((t, D_MODEL), F32),
        compiler_params=_params(("parallel",)),
        name="ffn",
    )(x, g, shift, scale, w_gate, w_up, w_down, gate, npost)


MOE_ROWS = MOE_TILE * TOP_K + N_EXPERTS * MOE_GRAIN
MOE_META = 2 * N_EXPERTS


def _moe_dispatch_kernel(x_ref, g_ref, sh_ref, sc_ref, rh_ref, rl_ref, xs_ref, route_ref, meta_ref):
    tm = x_ref.shape[0]
    h = _rms(x_ref[...], g_ref[...]) * (1.0 + sc_ref[0]) + sh_ref[0]
    hi = h.astype(BF16)
    lo = (h - hi.astype(F32)).astype(BF16)
    logits = _dot(hi, rh_ref[...]) + _dot(lo, rh_ref[...]) + _dot(hi, rl_ref[...])
    lane = lax.broadcasted_iota(jnp.int32, logits.shape, 1)
    logits = jnp.where(lane < N_EXPERTS, logits, -jnp.inf)
    m1 = jnp.max(logits, axis=-1, keepdims=True)
    i1 = jnp.min(jnp.where(logits == m1, lane, LANES), axis=-1, keepdims=True)
    rest = jnp.where(lane == i1, -jnp.inf, logits)
    m2 = jnp.max(rest, axis=-1, keepdims=True)
    i2 = jnp.min(jnp.where(rest == m2, lane, LANES), axis=-1, keepdims=True)
    e2 = jnp.exp(m2 - m1)
    sel1 = jnp.where(lane == i1, 1.0, 0.0)
    sel2 = jnp.where(lane == i2, 1.0, 0.0)
    sel = sel1 + sel2
    sel1_t, sel2_t = sel1.T, sel2.T
    sel_t = sel1_t + sel2_t

    ti = lax.broadcasted_iota(jnp.int32, (tm, tm), 0)
    si = lax.broadcasted_iota(jnp.int32, (tm, tm), 1)
    rank_c = _dot(jnp.where(si < ti, 1.0, 0.0).astype(BF16), sel.astype(BF16))
    rank_r = _dot(sel_t.astype(BF16), jnp.where(ti < si, 1.0, 0.0).astype(BF16))
    blk_row = jnp.ceil(jnp.sum(sel, axis=0, keepdims=True) * (1.0 / MOE_GRAIN))
    blk_col = jnp.ceil(jnp.sum(sel_t, axis=1, keepdims=True) * (1.0 / MOE_GRAIN))
    ea = lax.broadcasted_iota(jnp.int32, (LANES, LANES), 0)
    eb = lax.broadcasted_iota(jnp.int32, (LANES, LANES), 1)
    off_row = _dot(jnp.broadcast_to(blk_row, (8, LANES)).astype(BF16),
                   jnp.where(ea < eb, 1.0, 0.0).astype(BF16))[0:1, :]
    off_col = _dot(jnp.where(eb < ea, 1.0, 0.0).astype(BF16),
                   jnp.broadcast_to(blk_col, (LANES, LANES)).astype(BF16))[:, 0:1]

    pos_c = off_row * MOE_GRAIN + rank_c
    pos1_c = jnp.sum(jnp.where(lane == i1, pos_c, 0.0), axis=-1, keepdims=True)
    pos2_c = jnp.sum(jnp.where(lane == i2, pos_c, 0.0), axis=-1, keepdims=True)
    route_ref[...] = (jnp.where(lane == 0, pos1_c, 0.0) + jnp.where(lane == 1, pos2_c, 0.0)
                      + jnp.where(lane == 2, 1.0 / (1.0 + e2), 0.0) + jnp.where(lane == 3, e2 / (1.0 + e2), 0.0))
    pos_r = off_col * MOE_GRAIN + rank_r
    pos1_r = jnp.sum(sel1_t * pos_r, axis=0, keepdims=True)
    pos2_r = jnp.sum(sel2_t * pos_r, axis=0, keepdims=True)
    half = MOE_ROWS // 2
    for c in range(2):
        slot = (lax.broadcasted_iota(jnp.int32, (half, tm), 0) + c * half).astype(F32)
        onehot = jnp.where(slot == pos1_r, 1.0, jnp.where(slot == pos2_r, 1.0, 0.0)).astype(BF16)
        for n in range(2):
            cols = slice(n * (D_MODEL // 2), (n + 1) * (D_MODEL // 2))
            xs_ref[c * half:(c + 1) * half, cols] = _dot(onehot, hi[:, cols]).astype(BF16)

    lane8 = lax.broadcasted_iota(jnp.int32, (8, LANES), 1)
    blk8 = jnp.broadcast_to(blk_row, (8, LANES))
    off8 = pltpu.roll(jnp.broadcast_to(off_row, (8, LANES)), N_EXPERTS, 1)
    meta = jnp.where(lane8 < N_EXPERTS, blk8, jnp.where(lane8 < MOE_META, off8, 0.0))
    meta_ref[0] = meta.astype(jnp.int32)


def _moe_dispatch(x, g, shift, scale, r_hi, r_lo, rows_per_mod):
    t = x.shape[0]
    tm = MOE_TILE
    assert rows_per_mod % tm == 0
    per = rows_per_mod // tm
    n_t = t // tm
    return pl.pallas_call(
        _moe_dispatch_kernel,
        grid=(n_t,),
        in_specs=[
            pl.BlockSpec((tm, D_MODEL), lambda i: (i, 0)),
            pl.BlockSpec((1, D_MODEL), lambda i: (0, 0)),
            pl.BlockSpec((1, 1, D_MODEL), lambda i: (i // per, 0, 0)),
            pl.BlockSpec((1, 1, D_MODEL), lambda i: (i // per, 0, 0)),
            pl.BlockSpec((D_MODEL, LANES), lambda i: (0, 0)),
            pl.BlockSpec((D_MODEL, LANES), lambda i: (0, 0)),
        ],
        out_specs=[
            pl.BlockSpec((MOE_ROWS, D_MODEL), lambda i: (i, 0)),
            pl.BlockSpec((tm, LANES), lambda i: (i, 0)),
            pl.BlockSpec((1, 8, LANES), lambda i: (i, 0, 0)),
        ],
        out_shape=[
            jax.ShapeDtypeStruct((n_t * MOE_ROWS, D_MODEL), BF16),
            jax.ShapeDtypeStruct((t, LANES), F32),
            jax.ShapeDtypeStruct((n_t, 8, LANES), jnp.int32),
        ],
        compiler_params=_params(("parallel",)),
        name="moe_dispatch",
    )(x, g, shift, scale, r_hi, r_lo)


MOE_FF_TILE = 512


def _moe_combine_kernel(ys_ref, route_ref, x_ref, gate_ref, npost_ref, o_ref):
    tm = x_ref.shape[0]
    lane = lax.broadcasted_iota(jnp.int32, (tm, LANES), 1)
    route = route_ref[...]
    pos1, pos2, w1, w2 = (jnp.sum(jnp.where(lane == k, route, 0.0), axis=-1, keepdims=True) for k in range(4))
    width = 512
    chunks = []
    for c in range(MOE_ROWS // width):
        slot = (lax.broadcasted_iota(jnp.int32, (tm, width), 1) + c * width).astype(F32)
        chunks.append((jnp.where(slot == pos1, w1, 0.0) + jnp.where(slot == pos2, w2, 0.0)).astype(BF16))
    wmat = jnp.concatenate(chunks, axis=1)
    half = D_MODEL // 2
    acc = jnp.concatenate([_dot(wmat, ys_ref[:, n * half:(n + 1) * half]) for n in range(2)], axis=1)
    o_ref[...] = x_ref[...] + gate_ref[0] * _rms(acc, npost_ref[...])


MOE_BLOCK = 512


def _moe_plan(meta, n_t):
    per_tile = MOE_ROWS // MOE_GRAIN
    per_blk = MOE_BLOCK // MOE_GRAIN
    nblk = meta[:, 0, :N_EXPERTS]
    first = meta[:, 0, N_EXPERTS:MOE_META]
    tot = jnp.sum(nblk, axis=0)
    tot_pad = ((tot + per_blk - 1) // per_blk) * per_blk
    ends = jnp.cumsum(tot_pad)
    seg_dst = (ends - tot_pad)[None, :] + jnp.cumsum(nblk, axis=0) - nblk
    k = jnp.arange(per_tile, dtype=jnp.int32)[None, :, None]
    inside = (k >= first[:, None, :]) & (k < (first + nblk)[:, None, :])
    dst = jnp.sum(jnp.where(inside, seg_dst[:, None, :] + k - first[:, None, :], 0), axis=-1)
    used = jnp.any(inside, axis=-1).reshape(-1)
    dst = dst.reshape(-1).astype(jnp.int32)
    n_big = (n_t * per_tile + N_EXPERTS * per_blk) // per_blk
    big = jnp.arange(n_big, dtype=jnp.int32)[:, None] * per_blk
    expert = jnp.minimum(jnp.sum(big >= ends[None, :], axis=-1), N_EXPERTS - 1).astype(jnp.int32)
    valid = (big[:, 0] < ends[-1]).astype(jnp.int32)
    src = jnp.zeros((n_big * per_blk,), jnp.int32).at[jnp.where(used, dst, n_big * per_blk)].set(
        jnp.arange(dst.shape[0], dtype=jnp.int32), mode='drop')
    return jnp.where(used, dst, 0), jnp.concatenate([expert, valid, src]), n_big


def _moe_global_kernel(plan_ref, *refs, ff_tile):
    per_blk = MOE_BLOCK // MOE_GRAIN
    x_refs = refs[:per_blk]
    wg_ref, wu_ref, wd_ref, out_ref, xin_scr = refs[per_blk:]
    b = pl.program_id(0)
    valid = plan_ref[pl.num_programs(0) + b]
    ff = wg_ref.shape[2]

    @pl.when(valid == 1)
    def _():
        for k in range(per_blk):
            xin_scr[k * MOE_GRAIN:(k + 1) * MOE_GRAIN, :] = x_refs[k][...]
        xin = xin_scr[...]
        y = jnp.zeros((MOE_BLOCK, D_MODEL), F32)
        for j in range(ff // ff_tile):
            cols = slice(j * ff_tile, (j + 1) * ff_tile)
            y += _swiglu(xin, wg_ref[0, :, cols], wu_ref[0, :, cols], wd_ref[0, cols, :])
        out_ref[...] = y.astype(out_ref.dtype)

    @pl.when(valid == 0)
    def _():
        out_ref[...] = jnp.zeros_like(out_ref)


def _moe_experts_global(xs, plan, n_big, w_gate, w_up, w_down):
    n_e, _, ff = w_gate.shape
    per_blk = MOE_BLOCK // MOE_GRAIN
    resident = pl.Buffered(1)

    def src_spec(k):
        return pl.BlockSpec((MOE_GRAIN, D_MODEL), lambda b, p: (p[2 * n_big + b * per_blk + k], 0))

    grid_spec = pltpu.PrefetchScalarGridSpec(
        num_scalar_prefetch=1,
        grid=(n_big,),
        in_specs=[src_spec(k) for k in range(per_blk)] + [
            pl.BlockSpec((1, D_MODEL, ff), lambda b, p: (p[b], 0, 0), pipeline_mode=resident),
            pl.BlockSpec((1, D_MODEL, ff), lambda b, p: (p[b], 0, 0), pipeline_mode=resident),
            pl.BlockSpec((1, ff, D_MODEL), lambda b, p: (p[b], 0, 0), pipeline_mode=resident),
        ],
        out_specs=pl.BlockSpec((MOE_BLOCK, D_MODEL), lambda b, p: (b, 0)),
        scratch_shapes=[pltpu.VMEM((MOE_BLOCK, D_MODEL), BF16)],
    )
    return pl.pallas_call(
        functools.partial(_moe_global_kernel, ff_tile=MOE_FF_TILE),
        grid_spec=grid_spec,
        out_shape=jax.ShapeDtypeStruct((n_big * MOE_BLOCK, D_MODEL), BF16),
        compiler_params=_params(("arbitrary",)),
        name="moe_experts",
    )(plan, *([xs] * per_blk), w_gate, w_up, w_down)


def _moe_global(x, g, shift, scale, r_hi, r_lo, w_gate, w_up, w_down, gate, npost, rows_per_mod):
    xs, route, meta = _moe_dispatch(x, g, shift, scale, r_hi, r_lo, rows_per_mod)
    n_t = xs.shape[0] // MOE_ROWS
    block_map, plan, n_big = _moe_plan(meta, n_t)
    yg = _moe_experts_global(xs, plan, n_big, w_gate, w_up, w_down)
    return _moe_gather_combine(yg, block_map, route, x, gate, npost, rows_per_mod)


def _moe_gather_combine_kernel(map_ref, *refs):
    per_tile = MOE_ROWS // MOE_GRAIN
    y_refs = refs[:per_tile]
    route_ref, x_ref, gate_ref, npost_ref, o_ref, ys_scr = refs[per_tile:]
    del map_ref
    for k in range(per_tile):
        ys_scr[k * MOE_GRAIN:(k + 1) * MOE_GRAIN, :] = y_refs[k][...]
    _moe_combine_kernel(ys_scr, route_ref, x_ref, gate_ref, npost_ref, o_ref)


def _moe_gather_combine(yg, block_map, route, x, gate, npost, rows_per_mod):
    t = x.shape[0]
    tm = MOE_TILE
    per = rows_per_mod // tm
    per_tile = MOE_ROWS // MOE_GRAIN

    def y_spec(k):
        return pl.BlockSpec((MOE_GRAIN, D_MODEL), lambda i, m: (m[i * per_tile + k], 0))

    grid_spec = pltpu.PrefetchScalarGridSpec(
        num_scalar_prefetch=1,
        grid=(t // tm,),
        in_specs=[y_spec(k) for k in range(per_tile)] + [
            pl.BlockSpec((tm, LANES), lambda i, m: (i, 0)),
            pl.BlockSpec((tm, D_MODEL), lambda i, m: (i, 0)),
            pl.BlockSpec((1, 1, D_MODEL), lambda i, m: (i // per, 0, 0)),
            pl.BlockSpec((1, D_MODEL), lambda i, m: (0, 0)),
        ],
        out_specs=pl.BlockSpec((tm, D_MODEL), lambda i, m: (i, 0)),
        scratch_shapes=[pltpu.VMEM((MOE_ROWS, D_MODEL), BF16)],
    )
    return pl.pallas_call(
        _moe_gather_combine_kernel,
        grid_spec=grid_spec,
        out_shape=jax.ShapeDtypeStruct((t, D_MODEL), F32),
        compiler_params=_params(("arbitrary",)),
        name="moe_combine",
    )(block_map, *([yg] * per_tile), route, x, gate, npost)


def _reorder_w_in(w):
    o = 4 * M_WIDTH
    gates = jnp.pad(w[:, o:o + M_GATES], ((0, 0), (0, LANES - M_GATES)))
    na = w[:, o + M_GATES:o + M_GATES + 3 * NA_WIDTH]
    u0 = o + M_GATES + 3 * NA_WIDTH
    main = jnp.concatenate([w[:, :o], w[:, u0 + S5_WIDTH:], na], axis=1)
    return main.astype(BF16), gates.astype(BF16), w[:, u0:u0 + S5_WIDTH].astype(BF16)


def _row(v):
    return v.reshape(1, -1).astype(F32)


def _token_mixers(x, xc, mx, mc, lp, rope, seq_len, ctx_len, ctx_out):
    bsz = x.shape[0] // seq_len
    w_main, w_gates, w_u = _reorder_w_in(lp['w_in'])
    g_pre = _row(lp['norm_mix_pre'])
    conv_w = lp['m_conv_w'].astype(F32)
    conv_b = _row(lp['m_conv_b'])
    px, gx, ux = _inproj(x, g_pre, mx[0], mx[1], w_main, w_gates, w_u, conv_w, conv_b, rope[0], rope[1],
                         seq_len, seq_len, True)
    pc, gc, uc = _inproj(xc, g_pre, mc[0], mc[1], w_main, w_gates, w_u, conv_w, conv_b, rope[0], rope[1],
                         xc.shape[0], ctx_len, False)

    gate_b = jnp.pad(_row(lp['m_gate_b']), ((0, 0), (0, LANES - M_GATES)))
    m_norm = _row(lp['m_norm'])
    zero = _mlstm_zero_state(bsz)
    ym_c, fin_f, fin_b = _mlstm_bidir(pc, pc, gc, gate_b, m_norm, zero, zero, ctx_len)
    ym_x, _, _ = _mlstm_bidir(px, px, gx, gate_b, m_norm, fin_f, fin_b, seq_len)

    yn_x = _na_attention(px, pc, _na_bias_table(lp['na_rpb']), seq_len, ctx_len)

    tables = _s5_tables(lp['s5_lam_re'], lp['s5_lam_im'], lp['s5_log_dt'], lp['s5_b_re'], lp['s5_b_im'],
                        lp['s5_c_re'], lp['s5_c_im'])
    zs = jnp.zeros((bsz, S5_GROUPS * S5_STATE), F32)
    ys_c, fin_s = _s5_branch(uc, tables, ctx_len, (zs, zs, zs, zs), ctx_out)
    ys_x, _ = _s5_branch(ux, tables, seq_len, fin_s, True)

    wts = (_row(lp['s5_d']), lp['s5_glu_w'].astype(BF16), _row(lp['s5_glu_b']),
           lp['w_branch_m'].astype(BF16), lp['w_branch_na'].astype(BF16), lp['w_branch_s5'].astype(BF16),
           lp['w_out'].astype(BF16))
    g_post = _row(lp['norm_mix_post'])
    x = _merge(x, ym_x, yn_x, ys_x, ux, px, wts, mx[2], g_post, seq_len)
    if ctx_out:
        yn_c = _ctx_attention(pc, ctx_len)
        xc = _merge(xc, ym_c, yn_c, ys_c, uc, pc, wts, mc[2], g_post, xc.shape[0])
    return x, xc


def kernel(x, c, ctx, c_ctx, ada_w, ada_b, norm_mix_pre, norm_mix_post, norm_ffn_pre, norm_ffn_post, w_in, m_gate_b, m_conv_w, m_conv_b, m_norm, na_rpb, s5_lam_re, s5_lam_im, s5_log_dt, s5_b_re, s5_b_im, s5_c_re, s5_c_im, s5_d, s5_glu_w, s5_glu_b, w_branch_m, w_branch_na, w_branch_s5, w_out, ffn_w_gate, ffn_w_up, ffn_w_down, moe_router, moe_w_gate, moe_w_up, moe_w_down):
    bsz, seq_len, d = x.shape
    ctx_len = ctx.shape[1]
    depth = w_in.shape[0]
    assert d == D_MODEL and seq_len % (GRID_W * NA_ROWS) == 0 and ctx_len % M_CHUNK == 0
    rope = _rope_tables(seq_len)
    xf = x.reshape(bsz * seq_len, d)
    xc = ctx.reshape(bsz * ctx_len, d)
    cvecs = jnp.zeros((16, d), F32).at[:bsz].set(c).at[bsz].set(c_ctx)
    for l in range(depth):
        last = l == depth - 1
        mod = _adaln(cvecs, ada_w[l].astype(BF16), _row(ada_b[l]))
        mx = [mod[:bsz, k * d:(k + 1) * d].reshape(bsz, 1, d) for k in range(N_MOD)]
        mc = [mod[bsz:bsz + 1, k * d:(k + 1) * d].reshape(1, 1, d) for k in range(N_MOD)]
        lp = {
            'norm_mix_pre': norm_mix_pre[l], 'norm_mix_post': norm_mix_post[l],
            'w_in': w_in[l], 'm_gate_b': m_gate_b[l], 'm_conv_w': m_conv_w[l], 'm_conv_b': m_conv_b[l],
            'm_norm': m_norm[l], 'na_rpb': na_rpb[l], 's5_lam_re': s5_lam_re[l], 's5_lam_im': s5_lam_im[l],
            's5_log_dt': s5_log_dt[l], 's5_b_re': s5_b_re[l], 's5_b_im': s5_b_im[l], 's5_c_re': s5_c_re[l],
            's5_c_im': s5_c_im[l], 's5_d': s5_d[l], 's5_glu_w': s5_glu_w[l], 's5_glu_b': s5_glu_b[l],
            'w_branch_m': w_branch_m[l], 'w_branch_na': w_branch_na[l], 'w_branch_s5': w_branch_s5[l],
            'w_out': w_out[l],
        }
        xf, xc = _token_mixers(xf, xc, mx, mc, lp, rope, seq_len, ctx_len, not last)
        j = l // 2
        g_pre, g_post = _row(norm_ffn_pre[l]), _row(norm_ffn_post[l])
        if l % 2 == 0:
            wg, wu, wd = ffn_w_gate[j].astype(BF16), ffn_w_up[j].astype(BF16), ffn_w_down[j].astype(BF16)

            def channel(h, m, rows):
                return _ffn(h, g_pre, m[3], m[4], wg, wu, wd, m[5], g_post, rows)
        else:
            wg, wu, wd = moe_w_gate[j].astype(BF16), moe_w_up[j].astype(BF16), moe_w_down[j].astype(BF16)
            router = jnp.pad(moe_router[j].astype(F32), ((0, 0), (0, LANES - N_EXPERTS)))
            r_hi = router.astype(BF16)
            r_lo = (router - r_hi.astype(F32)).astype(BF16)

            def channel(h, m, rows):
                return _moe_global(h, g_pre, m[3], m[4], r_hi, r_lo, wg, wu, wd, m[5], g_post, rows)
        xf = channel(xf, mx, seq_len)
        if not last:
            xc = channel(xc, mc, xc.shape[0])
    return xf.reshape(bsz, seq_len, d)
```
